```python
import jax, jax.numpy as jnp
from jax import lax
import numpy as np

D_MODEL = 1024
BATCH = 8
SEQ = 4096
DEPTH = 1

D_MIX = D_MODEL
C_CONV = D_MIX // 2
C_POOL = D_MIX - C_CONV
CONV_HEADS = 8
CONV_WIDTH = 31
POOL_WINDOWS = (2, 4, 8, 16)
N_POOL_GROUPS = len(POOL_WINDOWS)
POOL_GROUP = C_POOL // N_POOL_GROUPS
D_IN = 2 * C_CONV + C_POOL
D_FF = ((8 * D_MODEL // 3 + 255) // 256) * 256
RMS_EPS = 1e-6
LN_EPS = 1e-5

kernel_name = "hybrid_conformer_conv_multiscale_pool_block"


def rmsnorm(x, g):
    xf = x.astype(jnp.float32)
    y = xf * lax.rsqrt(jnp.mean(xf * xf, axis=-1, keepdims=True) + RMS_EPS)
    return (y * g.astype(jnp.float32)).astype(x.dtype)


def layernorm(x, g, b):
    xf = x.astype(jnp.float32)
    mu = jnp.mean(xf, axis=-1, keepdims=True)
    var = jnp.mean(jnp.square(xf - mu), axis=-1, keepdims=True)
    y = (xf - mu) * lax.rsqrt(var + LN_EPS)
    return (y * g.astype(jnp.float32) + b.astype(jnp.float32)).astype(x.dtype)


def conformer_conv_group(a, gate, w_dw, b_dw, ln_g, ln_b):
    u = a * jax.nn.sigmoid(gate)
    k = w_dw.astype(u.dtype)[:, None, :]
    v = lax.conv_general_dilated(
        u, k, window_strides=(1,), padding=[(CONV_WIDTH - 1, 0)],
        dimension_numbers=("NWC", "WIO", "NWC"),
        feature_group_count=C_CONV) + b_dw.astype(u.dtype)
    v = layernorm(v, ln_g, ln_b)
    return jax.nn.silu(v)


def multiscale_pool_group(p, w_pool, s_pool):
    seq = p.shape[1]
    pos = jnp.arange(seq)
    outs = []
    for i, w in enumerate(POOL_WINDOWS):
        seg = p[..., i * POOL_GROUP:(i + 1) * POOL_GROUP].astype(jnp.float32)
        cs = jnp.cumsum(seg, axis=1)
        lag = jnp.pad(cs, ((0, 0), (w, 0), (0, 0)))[:, :seq]
        cnt = jnp.minimum(pos + 1, w).astype(jnp.float32)[None, :, None]
        y = ((cs - lag) / cnt - seg).astype(p.dtype)
        outs.append(jnp.einsum("bsg,gh->bsh", y, w_pool[i]))
    return jnp.concatenate(outs, axis=-1) * s_pool


def swiglu(x, w_gate, w_up, w_down):
    return (jax.nn.silu(x @ w_gate) * (x @ w_up)) @ w_down


def _fwd_setup_inputs(seed: int = 0) -> dict:
    key = jax.random.key(seed)
    ks = jax.random.split(key, 20)
    f = jnp.float32
    n = lambda k, s, sc: jax.random.normal(k, s, f) * sc
    return {
        "x": jax.random.normal(ks[0], (BATCH, SEQ, D_MODEL), f),
        "g_mix": 1.0 + n(ks[1], (DEPTH, D_MODEL), 0.05),
        "w_in": n(ks[2], (DEPTH, D_MODEL, D_IN), D_MODEL ** -0.5),
        "b_in": n(ks[3], (DEPTH, D_IN), 0.02),
        "w_dw": n(ks[4], (DEPTH, CONV_WIDTH, C_CONV), CONV_WIDTH ** -0.5),
        "b_dw": n(ks[5], (DEPTH, C_CONV), 0.02),
        "ln_g": 1.0 + n(ks[6], (DEPTH, C_CONV), 0.05),
        "ln_b": n(ks[7], (DEPTH, C_CONV), 0.02),
        "w_pool": n(ks[8], (DEPTH, N_POOL_GROUPS, POOL_GROUP, POOL_GROUP), POOL_GROUP ** -0.5),
        "s_pool": 1.0 + n(ks[9], (DEPTH, C_POOL), 0.1),
        "w_out": n(ks[10], (DEPTH, D_MIX, D_MODEL), D_MIX ** -0.5),
        "g_ffn": 1.0 + n(ks[11], (DEPTH, D_MODEL), 0.05),
        "w_gate": n(ks[12], (DEPTH, D_MODEL, D_FF), D_MODEL ** -0.5),
        "w_up": n(ks[13], (DEPTH, D_MODEL, D_FF), D_MODEL ** -0.5),
        "w_down": n(ks[14], (DEPTH, D_FF, D_MODEL), D_FF ** -0.5),
        "g_final": 1.0 + n(ks[15], (D_MODEL,), 0.05),
    }


def _fwd_reference(x, g_mix, w_in, b_in, w_dw, b_dw, ln_g, ln_b, w_pool, s_pool,
              w_out, g_ffn, w_gate, w_up, w_down, g_final):
    h = x
    for l in range(DEPTH):
        xn = rmsnorm(h, g_mix[l])
        z = xn @ w_in[l] + b_in[l]
        a = z[..., :C_CONV]
        gate = z[..., C_CONV:2 * C_CONV]
        p = z[..., 2 * C_CONV:]
        y_conv = conformer_conv_group(a, gate, w_dw[l], b_dw[l], ln_g[l], ln_b[l])
        y_pool = multiscale_pool_group(p, w_pool[l], s_pool[l])
        y = jnp.concatenate([y_conv, y_pool], axis=-1)
        h = h + y @ w_out[l]
        h = h + swiglu(rmsnorm(h, g_ffn[l]), w_gate[l], w_up[l], w_down[l])
    return rmsnorm(h, g_final)


import jax as _jax
import jax.numpy as _jnp

TWIN_FORMAT = 'train_step'
FWD_PARAMS = ['x', 'g_mix', 'w_in', 'b_in', 'w_dw', 'b_dw', 'ln_g', 'ln_b', 'w_pool', 's_pool', 'w_out', 'g_ffn', 'w_gate', 'w_up', 'w_down', 'g_final']
TWIN_WEIGHTS = ['g_mix', 'w_in', 'b_in', 'w_dw', 'b_dw', 'ln_g', 'ln_b', 'w_pool', 's_pool', 'w_out', 'g_ffn', 'w_gate', 'w_up', 'w_down', 'g_final']
TWIN_DIFF_INPUT = 'x'
TWIN_INPUTS = ['x', 'g_mix', 'w_in', 'b_in', 'w_dw', 'b_dw', 'ln_g', 'ln_b', 'w_pool', 's_pool', 'w_out', 'g_ffn', 'w_gate', 'w_up', 'w_down', 'g_final', 'loss_target', 'm_g_mix', 'm_w_in', 'm_b_in', 'm_w_dw', 'm_b_dw', 'm_ln_g', 'm_ln_b', 'm_w_pool', 'm_s_pool', 'm_w_out', 'm_g_ffn', 'm_w_gate', 'm_w_up', 'm_w_down', 'm_g_final', 'v_g_mix', 'v_w_in', 'v_b_in', 'v_w_dw', 'v_b_dw', 'v_ln_g', 'v_ln_b', 'v_w_pool', 'v_s_pool', 'v_w_out', 'v_g_ffn', 'v_w_gate', 'v_w_up', 'v_w_down', 'v_g_final']
TWIN_OUTPUTS = ['loss', 'grad_x', 'grad_g_mix', 'grad_w_in', 'grad_b_in', 'grad_w_dw', 'grad_b_dw', 'grad_ln_g', 'grad_ln_b', 'grad_w_pool', 'grad_s_pool', 'grad_w_out', 'grad_g_ffn', 'grad_w_gate', 'grad_w_up', 'grad_w_down', 'grad_g_final', 'delta_g_mix', 'delta_w_in', 'delta_b_in', 'delta_w_dw', 'delta_b_dw', 'delta_ln_g', 'delta_ln_b', 'delta_w_pool', 'delta_s_pool', 'delta_w_out', 'delta_g_ffn', 'delta_w_gate', 'delta_w_up', 'delta_w_down', 'delta_g_final', 'new_m_g_mix', 'new_m_w_in', 'new_m_b_in', 'new_m_w_dw', 'new_m_b_dw', 'new_m_ln_g', 'new_m_ln_b', 'new_m_w_pool', 'new_m_s_pool', 'new_m_w_out', 'new_m_g_ffn', 'new_m_w_gate', 'new_m_w_up', 'new_m_w_down', 'new_m_g_final', 'new_v_g_mix', 'new_v_w_in', 'new_v_b_in', 'new_v_w_dw', 'new_v_b_dw', 'new_v_ln_g', 'new_v_ln_b', 'new_v_w_pool', 'new_v_s_pool', 'new_v_w_out', 'new_v_g_ffn', 'new_v_w_gate', 'new_v_w_up', 'new_v_w_down', 'new_v_g_final']
TWIN_LEAF_KINDS = {'loss': 'loss', 'grad_x': 'grad_x', 'grad_g_mix': 'grad_w', 'grad_w_in': 'grad_w', 'grad_b_in': 'grad_w', 'grad_w_dw': 'grad_w', 'grad_b_dw': 'grad_w', 'grad_ln_g': 'grad_w', 'grad_ln_b': 'grad_w', 'grad_w_pool': 'grad_w', 'grad_s_pool': 'grad_w', 'grad_w_out': 'grad_w', 'grad_g_ffn': 'grad_w', 'grad_w_gate': 'grad_w', 'grad_w_up': 'grad_w', 'grad_w_down': 'grad_w', 'grad_g_final': 'grad_w', 'delta_g_mix': 'delta_w', 'delta_w_in': 'delta_w', 'delta_b_in': 'delta_w', 'delta_w_dw': 'delta_w', 'delta_b_dw': 'delta_w', 'delta_ln_g': 'delta_w', 'delta_ln_b': 'delta_w', 'delta_w_pool': 'delta_w', 'delta_s_pool': 'delta_w', 'delta_w_out': 'delta_w', 'delta_g_ffn': 'delta_w', 'delta_w_gate': 'delta_w', 'delta_w_up': 'delta_w', 'delta_w_down': 'delta_w', 'delta_g_final': 'delta_w', 'new_m_g_mix': 'new_m', 'new_m_w_in': 'new_m', 'new_m_b_in': 'new_m', 'new_m_w_dw': 'new_m', 'new_m_b_dw': 'new_m', 'new_m_ln_g': 'new_m', 'new_m_ln_b': 'new_m', 'new_m_w_pool': 'new_m', 'new_m_s_pool': 'new_m', 'new_m_w_out': 'new_m', 'new_m_g_ffn': 'new_m', 'new_m_w_gate': 'new_m', 'new_m_w_up': 'new_m', 'new_m_w_down': 'new_m', 'new_m_g_final': 'new_m', 'new_v_g_mix': 'new_v', 'new_v_w_in': 'new_v', 'new_v_b_in': 'new_v', 'new_v_w_dw': 'new_v', 'new_v_b_dw': 'new_v', 'new_v_ln_g': 'new_v', 'new_v_ln_b': 'new_v', 'new_v_w_pool': 'new_v', 'new_v_s_pool': 'new_v', 'new_v_w_out': 'new_v', 'new_v_g_ffn': 'new_v', 'new_v_w_gate': 'new_v', 'new_v_w_up': 'new_v', 'new_v_w_down': 'new_v', 'new_v_g_final': 'new_v'}


def _forward(args):
    return _fwd_reference(*[args[k] for k in FWD_PARAMS])


def _output_shape():
    out = _jax.eval_shape(lambda: _forward(_fwd_setup_inputs(0)))
    return out.shape, out.dtype

N_MICROBATCH = 1
ADAM_LR = 0.001
ADAM_B1 = 0.9
ADAM_B2 = 0.999
ADAM_EPS = 1e-08
ADAM_WD = 0.01
ADAM_STEP = 10
PER_EXAMPLE_BATCH_AXIS = {'x': 0, 'loss_target': 0}
SHARED_INPUTS = []
_WEIGHT_DTYPES = {'g_mix': _jnp.float32, 'w_in': _jnp.float32, 'b_in': _jnp.float32, 'w_dw': _jnp.float32, 'b_dw': _jnp.float32, 'ln_g': _jnp.float32, 'ln_b': _jnp.float32, 'w_pool': _jnp.float32, 's_pool': _jnp.float32, 'w_out': _jnp.float32, 'g_ffn': _jnp.float32, 'w_gate': _jnp.float32, 'w_up': _jnp.float32, 'w_down': _jnp.float32, 'g_final': _jnp.float32}
MOMENT_SCALE = {'g_mix': 1.200548e-01, 'w_in': 1.006874e-01, 'b_in': 8.450470e-02, 'w_dw': 9.714853e-02, 'b_dw': 2.235065e-01, 'ln_g': 1.548402e-01, 'ln_b': 1.402631e-01, 'w_pool': 1.431665e-01, 's_pool': 1.504997e-01, 'w_out': 1.225912e-01, 'g_ffn': 1.175449e-01, 'w_gate': 4.793142e-02, 'w_up': 4.668340e-02, 'w_down': 7.739620e-02, 'g_final': 3.206596e+01}


def _to_microbatches(a, axis):
    t = _jnp.moveaxis(a, axis, 0)
    t = t.reshape((N_MICROBATCH, t.shape[0] // N_MICROBATCH) + t.shape[1:])
    return _jnp.moveaxis(t, 1, axis + 1)


def setup_inputs(seed: int = 0) -> dict:
    inp = _fwd_setup_inputs(seed)
    key = _jax.random.fold_in(_jax.random.key(seed), 7919)
    shape, _ = _output_shape()
    out = dict(inp)
    out["loss_target"] = _jax.random.normal(_jax.random.fold_in(key, 0), shape, _jnp.float32)
    for i, name in enumerate(TWIN_WEIGHTS):
        w = inp[name].astype(_jnp.float32)
        if MOMENT_SCALE is None:
            s = _jnp.sqrt(_jnp.mean(_jnp.square(w)) + 1e-30)
        else:
            s = MOMENT_SCALE[name]
        km, kv = _jax.random.split(_jax.random.fold_in(key, i + 1))
        out[name] = w
        out["m_" + name] = s * _jax.random.normal(km, w.shape, _jnp.float32)
        out["v_" + name] = (s * s) * _jax.random.uniform(kv, w.shape, _jnp.float32, 0.5, 1.5)
    if N_MICROBATCH > 1:
        for name, axis in PER_EXAMPLE_BATCH_AXIS.items():
            out[name] = _to_microbatches(out[name], axis)
    return {'x': out['x'], 'g_mix': out['g_mix'], 'w_in': out['w_in'], 'b_in': out['b_in'], 'w_dw': out['w_dw'], 'b_dw': out['b_dw'], 'ln_g': out['ln_g'], 'ln_b': out['ln_b'], 'w_pool': out['w_pool'], 's_pool': out['s_pool'], 'w_out': out['w_out'], 'g_ffn': out['g_ffn'], 'w_gate': out['w_gate'], 'w_up': out['w_up'], 'w_down': out['w_down'], 'g_final': out['g_final'], 'loss_target': out['loss_target'], 'm_g_mix': out['m_g_mix'], 'm_w_in': out['m_w_in'], 'm_b_in': out['m_b_in'], 'm_w_dw': out['m_w_dw'], 'm_b_dw': out['m_b_dw'], 'm_ln_g': out['m_ln_g'], 'm_ln_b': out['m_ln_b'], 'm_w_pool': out['m_w_pool'], 'm_s_pool': out['m_s_pool'], 'm_w_out': out['m_w_out'], 'm_g_ffn': out['m_g_ffn'], 'm_w_gate': out['m_w_gate'], 'm_w_up': out['m_w_up'], 'm_w_down': out['m_w_down'], 'm_g_final': out['m_g_final'], 'v_g_mix': out['v_g_mix'], 'v_w_in': out['v_w_in'], 'v_b_in': out['v_b_in'], 'v_w_dw': out['v_w_dw'], 'v_b_dw': out['v_b_dw'], 'v_ln_g': out['v_ln_g'], 'v_ln_b': out['v_ln_b'], 'v_w_pool': out['v_w_pool'], 'v_s_pool': out['v_s_pool'], 'v_w_out': out['v_w_out'], 'v_g_ffn': out['v_g_ffn'], 'v_w_gate': out['v_w_gate'], 'v_w_up': out['v_w_up'], 'v_w_down': out['v_w_down'], 'v_g_final': out['v_g_final']}


def _loss(weights, diff, rest, loss_target):
    with _jax.named_scope("forward"):
        args = {**rest, TWIN_DIFF_INPUT: diff, **{k: w.astype(_WEIGHT_DTYPES[k]) for k, w in weights.items()}}
        y = _forward(args)
    with _jax.named_scope("loss_head"):
        err = _jnp.square(y.astype(_jnp.float32) - loss_target)
        return 0.5 * _jnp.sum(_jnp.mean(err, axis=-1)) if err.ndim else 0.5 * err


def _adamw(w, g, m, v):
    m = ADAM_B1 * m + (1.0 - ADAM_B1) * g
    v = ADAM_B2 * v + (1.0 - ADAM_B2) * _jnp.square(g)
    m_hat = m / (1.0 - ADAM_B1 ** ADAM_STEP)
    v_hat = v / (1.0 - ADAM_B2 ** ADAM_STEP)
    delta = -ADAM_LR * (m_hat / (_jnp.sqrt(v_hat) + ADAM_EPS) + ADAM_WD * w)
    return delta, m, v


def reference(x, g_mix, w_in, b_in, w_dw, b_dw, ln_g, ln_b, w_pool, s_pool, w_out, g_ffn, w_gate, w_up, w_down, g_final, loss_target, m_g_mix, m_w_in, m_b_in, m_w_dw, m_b_dw, m_ln_g, m_ln_b, m_w_pool, m_s_pool, m_w_out, m_g_ffn, m_w_gate, m_w_up, m_w_down, m_g_final, v_g_mix, v_w_in, v_b_in, v_w_dw, v_b_dw, v_ln_g, v_ln_b, v_w_pool, v_s_pool, v_w_out, v_g_ffn, v_w_gate, v_w_up, v_w_down, v_g_final):
    given = dict(x=x, g_mix=g_mix, w_in=w_in, b_in=b_in, w_dw=w_dw, b_dw=b_dw, ln_g=ln_g, ln_b=ln_b, w_pool=w_pool, s_pool=s_pool, w_out=w_out, g_ffn=g_ffn, w_gate=w_gate, w_up=w_up, w_down=w_down, g_final=g_final, loss_target=loss_target, m_g_mix=m_g_mix, m_w_in=m_w_in, m_b_in=m_b_in, m_w_dw=m_w_dw, m_b_dw=m_b_dw, m_ln_g=m_ln_g, m_ln_b=m_ln_b, m_w_pool=m_w_pool, m_s_pool=m_s_pool, m_w_out=m_w_out, m_g_ffn=m_g_ffn, m_w_gate=m_w_gate, m_w_up=m_w_up, m_w_down=m_w_down, m_g_final=m_g_final, v_g_mix=v_g_mix, v_w_in=v_w_in, v_b_in=v_b_in, v_w_dw=v_w_dw, v_b_dw=v_b_dw, v_ln_g=v_ln_g, v_ln_b=v_ln_b, v_w_pool=v_w_pool, v_s_pool=v_s_pool, v_w_out=v_w_out, v_g_ffn=v_g_ffn, v_w_gate=v_w_gate, v_w_up=v_w_up, v_w_down=v_w_down, v_g_final=v_g_final)
    weights = {n: given[n] for n in TWIN_WEIGHTS}
    shared = {n: given[n] for n in SHARED_INPUTS}
    per_example = {n: given[n] for n in ['x']}
    grad_fn = _jax.value_and_grad(_loss, argnums=(0, 1))

    def one_microbatch(ex, loss_target):
        ex = dict(ex)
        diff = ex.pop(TWIN_DIFF_INPUT)
        return grad_fn(weights, diff, {**shared, **ex}, loss_target)

    if N_MICROBATCH == 1:
        loss, (grad_w, grad_x) = one_microbatch(per_example, given["loss_target"])
    else:
        def body(carry, xs):
            loss_sum, grad_sum = carry
            l_k, (gw_k, gx_k) = one_microbatch(xs[0], xs[1])
            with _jax.named_scope("update"):
                return (loss_sum + l_k, _jax.tree.map(_jnp.add, grad_sum, gw_k)), gx_k

        init = (_jnp.zeros((), _jnp.float32), _jax.tree.map(_jnp.zeros_like, weights))
        (loss, grad_w), grad_x = _jax.lax.scan(body, init, (per_example, given["loss_target"]))
    with _jax.named_scope("update"):
        delta_w, new_m, new_v = {}, {}, {}
        for n in TWIN_WEIGHTS:
            delta_w[n], new_m[n], new_v[n] = _adamw(weights[n], grad_w[n], given["m_" + n], given["v_" + n])
    return (loss, grad_x, *[grad_w[n] for n in TWIN_WEIGHTS], *[delta_w[n] for n in TWIN_WEIGHTS],
            *[new_m[n] for n in TWIN_WEIGHTS], *[new_v[n] for n in TWIN_WEIGHTS])
```

```python
import functools

import jax
import jax.numpy as jnp
from jax import lax
from jax.experimental import pallas as pl
from jax.experimental.pallas import tpu as pltpu

F32 = jnp.float32
BF16 = jnp.bfloat16
MESH = pl.DeviceIdType.MESH
N_DEV = 8

C_CONV = 512
CONV_WIDTH = 31
POOL_WINDOWS = (2, 4, 8, 16)
POOL_GROUP = 128
RMS_EPS = 1e-6
LN_EPS = 1e-5

ADAM_LR = 0.001
ADAM_B1 = 0.9
ADAM_B2 = 0.999
ADAM_EPS = 1e-08
ADAM_WD = 0.01
ADAM_STEP = 10

HALO = 32
SUBLANES = 8
CONV_ROWS = 64
VMEM_LIMIT = 56 * 1024 * 1024


def _dot(a, b):
    return jnp.dot(a, b, preferred_element_type=F32)


def _dot_nt(a, b):
    return lax.dot_general(a, b, (((1,), (1,)), ((), ())), preferred_element_type=F32)


def _dot_tn(a, b):
    return lax.dot_general(a, b, (((0,), (0,)), ((), ())), preferred_element_type=F32)


def _mean_last(v):
    return jnp.mean(v, axis=-1, keepdims=True)


def _full(shape):
    nd = len(shape)
    return pl.BlockSpec(shape, lambda *_: (0,) * nd)


def _full1(shape):
    nd = len(shape)
    return pl.BlockSpec(shape, lambda *_: (0,) * nd, pipeline_mode=pl.Buffered(1))


def _shifted_copies(sh_ref, rows):
    for s in range(1, SUBLANES):
        sh_ref[s, 0:rows, :] = sh_ref[0, s:s + rows, :]


def _tap(sh_ref, off, r0, rows):
    q, s = divmod(off, SUBLANES)
    return sh_ref[s, pl.ds(r0 + q * SUBLANES, rows), :]


def _mix_fwd(x, g_mix, w_in, b_in, w_dw, b_dw, ln_g, ln_b, w_pool, s_pool, w_out, ts):
    S, D = x.shape
    d_in = w_in.shape[1]
    C = C_CONV
    nt = S // ts
    nrb = ts // CONV_ROWS

    def body(x_ref, g_ref, win_ref, bin_ref, wdw_ref, bdw_ref, lng_ref, lnb_ref, wp_ref, sp_ref, wout_ref,
             a_ref, gate_ref, v_ref, m_ref, y_ref, h1_ref, xn_ref, ush, pbuf):
        i = pl.program_id(0)

        @pl.when(i == 0)
        def _():
            ush[0, 0:HALO, :] = jnp.zeros((HALO, C), F32)
            pbuf[0:HALO, :] = jnp.zeros((HALO, C), F32)

        x = x_ref[...]
        r1 = lax.rsqrt(_mean_last(x * x) + RMS_EPS)
        xn = (x * r1 * g_ref[...]).astype(BF16)
        xn_ref[...] = xn
        z = _dot(xn, win_ref[...]) + bin_ref[...]
        a = z[:, 0:C]
        gate = z[:, C:2 * C]
        a_ref[...] = a
        gate_ref[...] = gate
        ush[0, HALO:HALO + ts, :] = a * jax.nn.sigmoid(gate)
        pbuf[HALO:HALO + ts, :] = z[:, 2 * C:]

        _shifted_copies(ush, ts + HALO - SUBLANES)

        def conv_block(rb, carry):
            r0 = pl.multiple_of(rb * CONV_ROWS, CONV_ROWS)
            acc = jnp.zeros((CONV_ROWS, C), F32)
            for k in range(CONV_WIDTH):
                acc = acc + wdw_ref[k:k + 1, :] * _tap(ush, HALO - (CONV_WIDTH - 1) + k, r0, CONV_ROWS)
            v_ref[pl.ds(r0, CONV_ROWS), :] = acc + bdw_ref[...]
            return carry

        lax.fori_loop(0, nrb, conv_block, 0)

        v = v_ref[...]
        mu = _mean_last(v)
        xc = v - mu
        rstd = lax.rsqrt(_mean_last(xc * xc) + LN_EPS)
        ln = xc * rstd * lng_ref[...] + lnb_ref[...]
        y_ref[:, 0:C] = (ln * jax.nn.sigmoid(ln)).astype(BF16)

        row = lax.broadcasted_iota(jnp.int32, (ts, 1), 0) + i * ts
        for gi, w in enumerate(POOL_WINDOWS):
            lanes = slice(gi * POOL_GROUP, (gi + 1) * POOL_GROUP)
            seg = pbuf[HALO:HALO + ts, lanes]
            ws = seg
            for k in range(1, w):
                ws = ws + pbuf[HALO - k:HALO - k + ts, lanes]
            cnt = jnp.minimum(row + 1, w).astype(F32)
            m = (ws / cnt - seg).astype(BF16)
            m_ref[:, lanes] = m
            ypre = _dot(m, wp_ref[gi])
            y_ref[:, C + gi * POOL_GROUP:C + (gi + 1) * POOL_GROUP] = (ypre * sp_ref[:, lanes]).astype(BF16)

        h1_ref[...] = x + _dot(y_ref[...], wout_ref[...])

        ush[0, 0:HALO, :] = ush[0, ts:ts + HALO, :]
        pbuf[0:HALO, :] = pbuf[ts:ts + HALO, :]

    tile = lambda w, dt: (pl.BlockSpec((ts, w), lambda i: (i, 0)), jax.ShapeDtypeStruct((S, w), dt))
    outs = [tile(C, F32), tile(C, F32), tile(C, F32), tile(C, BF16), tile(D, BF16), tile(D, F32), tile(D, BF16)]
    return pl.pallas_call(
        body, name="mix_fwd", grid=(nt,),
        in_specs=[pl.BlockSpec((ts, D), lambda i: (i, 0)), _full((1, D)), _full((D, d_in)), _full((1, d_in)),
                  _full(w_dw.shape), _full((1, C)), _full((1, C)), _full((1, C)), _full(w_pool.shape),
                  _full((1, C)), _full((D, D))],
        out_specs=[o[0] for o in outs], out_shape=[o[1] for o in outs],
        scratch_shapes=[pltpu.VMEM((SUBLANES, ts + HALO, C), F32), pltpu.VMEM((ts + HALO, C), F32)],
        compiler_params=pltpu.CompilerParams(dimension_semantics=("arbitrary",), vmem_limit_bytes=VMEM_LIMIT),
    )(x, g_mix, w_in, b_in, w_dw, b_dw, ln_g, ln_b, w_pool, s_pool, w_out)


def _ffn(h1, target, g_ffn, g_final, w_gate, w_up, w_down, ts, f_chunks):
    S, D = h1.shape
    Fd = w_gate.shape[1]
    nt = S // ts
    bounds = []
    lo = 0
    for n in f_chunks:
        bounds.append((lo, lo + n))
        lo += n
    assert lo == Fd

    def body(h1_ref, tgt_ref, gf_ref, gl_ref, wg_ref, wu_ref, wd_ref,
             dh1_ref, hn_ref, act_ref, dgt_ref, dup_ref, dh2_ref, loss_ref, dgl_ref, dgf_ref, gt_s, up_s):
        i = pl.program_id(0)

        @pl.when(i == 0)
        def _():
            loss_ref[...] = jnp.zeros_like(loss_ref)
            dgl_ref[...] = jnp.zeros_like(dgl_ref)
            dgf_ref[...] = jnp.zeros_like(dgf_ref)

        h1 = h1_ref[...]
        r2 = lax.rsqrt(_mean_last(h1 * h1) + RMS_EPS)
        hhat = h1 * r2
        hn = (hhat * gf_ref[...]).astype(BF16)
        hn_ref[...] = hn
        h2 = h1
        for lo, hi in bounds:
            gt = _dot(hn, wg_ref[:, lo:hi])
            up = _dot(hn, wu_ref[:, lo:hi])
            gt_s[:, lo:hi] = gt
            up_s[:, lo:hi] = up
            act = (gt * jax.nn.sigmoid(gt) * up).astype(BF16)
            act_ref[:, lo:hi] = act
            h2 = h2 + _dot(act, wd_ref[lo:hi, :])

        r3 = lax.rsqrt(_mean_last(h2 * h2) + RMS_EPS)
        n3 = h2 * r3
        gl = gl_ref[...]
        diff = n3 * gl - tgt_ref[...]
        loss_ref[...] += jnp.sum(0.5 * jnp.sum(diff * diff, axis=-1, keepdims=True) / D, axis=0, keepdims=True)
        dout = diff / D
        dgl_ref[...] += jnp.sum(dout * n3, axis=0, keepdims=True)
        dn = dout * gl
        dh2 = r3 * (dn - n3 * _mean_last(dn * n3))
        dh2b = dh2.astype(BF16)
        dh2_ref[...] = dh2b

        dhn = jnp.zeros((ts, D), F32)
        for lo, hi in bounds:
            gt = gt_s[:, lo:hi]
            up = up_s[:, lo:hi]
            sg = jax.nn.sigmoid(gt)
            dact = _dot_nt(dh2b, wd_ref[lo:hi, :])
            dgt = (dact * up * (sg * (1.0 + gt * (1.0 - sg)))).astype(BF16)
            dup = (dact * (gt * sg)).astype(BF16)
            dgt_ref[:, lo:hi] = dgt
            dup_ref[:, lo:hi] = dup
            dhn = dhn + _dot_nt(dgt, wg_ref[:, lo:hi]) + _dot_nt(dup, wu_ref[:, lo:hi])

        dgf_ref[...] += jnp.sum(dhn * hhat, axis=0, keepdims=True)
        dnn = dhn * gf_ref[...]
        dh1_ref[...] = dh2 + r2 * (dnn - hhat * _mean_last(dnn * hhat))

    tile = lambda w, dt: (pl.BlockSpec((ts, w), lambda i: (i, 0)), jax.ShapeDtypeStruct((S, w), dt))
    acc = lambda w: (_full((1, w)), jax.ShapeDtypeStruct((1, w), F32))
    outs = [tile(D, F32), tile(D, BF16), tile(Fd, BF16), tile(Fd, BF16), tile(Fd, BF16), tile(D, BF16),
            acc(128), acc(D), acc(D)]
    return pl.pallas_call(
        body, name="ffn_fwd_bwd", grid=(nt,),
        in_specs=[pl.BlockSpec((ts, D), lambda i: (i, 0)), pl.BlockSpec((ts, D), lambda i: (i, 0)),
                  _full((1, D)), _full((1, D)), _full1((D, Fd)), _full1((D, Fd)), _full1((Fd, D))],
        out_specs=[o[0] for o in outs], out_shape=[o[1] for o in outs],
        scratch_shapes=[pltpu.VMEM((ts, Fd), F32), pltpu.VMEM((ts, Fd), F32)],
        compiler_params=pltpu.CompilerParams(dimension_semantics=("arbitrary",), vmem_limit_bytes=VMEM_LIMIT),
    )(h1, target, g_ffn, g_final, w_gate, w_up, w_down)


def _mix_bwd(dh1, x, a, gate, v, m, g_mix, w_in, w_dw, ln_g, ln_b, w_pool, s_pool, w_out, ts):
    S, D = x.shape
    d_in = w_in.shape[1]
    C = C_CONV
    nt = S // ts
    nrb = ts // CONV_ROWS
    hb = ts // HALO
    wrows = ((CONV_WIDTH + SUBLANES - 1) // SUBLANES) * SUBLANES

    def body(dh1_ref, x_ref, a_ref, gate_ref, ah_ref, gh_ref, v_ref, m_ref, g_ref, win_ref, wdw_ref, lng_ref,
             lnb_ref, wp_ref, sp_ref, wout_ref,
             dx_ref, dz_ref, dgm_ref, dbin_ref, dwdw_ref, dbdw_ref, dlng_ref, dlnb_ref, dwp_ref, dsp_ref,
             ush, dvsh, dqbuf, du_s):
        i = pl.program_id(0)
        t = nt - 1 - i

        @pl.when(i == 0)
        def _():
            dvsh[0, ts:ts + HALO, :] = jnp.zeros((HALO, C), F32)
            dqbuf[ts:ts + HALO, :] = jnp.zeros((HALO, C), F32)
            for r in (dgm_ref, dbin_ref, dwdw_ref, dbdw_ref, dlng_ref, dlnb_ref, dwp_ref, dsp_ref):
                r[...] = jnp.zeros_like(r)

        dh1 = dh1_ref[...]
        dy = _dot_nt(dh1.astype(BF16), wout_ref[...])

        v = v_ref[...]
        mu = _mean_last(v)
        xc = v - mu
        rstd = lax.rsqrt(_mean_last(xc * xc) + LN_EPS)
        vhat = xc * rstd
        lng = lng_ref[...]
        ln = vhat * lng + lnb_ref[...]
        sg = jax.nn.sigmoid(ln)
        dln = dy[:, 0:C] * (sg * (1.0 + ln * (1.0 - sg)))
        dlng_ref[...] += jnp.sum(dln * vhat, axis=0, keepdims=True)
        dlnb_ref[...] += jnp.sum(dln, axis=0, keepdims=True)
        dvh = dln * lng
        dv = rstd * (dvh - _mean_last(dvh) - vhat * _mean_last(dvh * vhat))
        dbdw_ref[...] += jnp.sum(dv, axis=0, keepdims=True)
        dvsh[0, 0:ts, :] = dv
        _shifted_copies(dvsh, ts + HALO - SUBLANES)

        a = a_ref[...]
        sgate = jax.nn.sigmoid(gate_ref[...])
        ush[0, HALO:HALO + ts, :] = a * sgate
        uh = ah_ref[...] * jax.nn.sigmoid(gh_ref[...])
        ush[0, 0:HALO, :] = jnp.where(t > 0, uh, 0.0)
        _shifted_copies(ush, ts + HALO - SUBLANES)

        def conv_block(rb, carry):
            r0 = pl.multiple_of(rb * CONV_ROWS, CONV_ROWS)
            acc = jnp.zeros((CONV_ROWS, C), F32)
            for k in range(CONV_WIDTH):
                acc = acc + wdw_ref[k:k + 1, :] * _tap(dvsh, CONV_WIDTH - 1 - k, r0, CONV_ROWS)
            du_s[pl.ds(r0, CONV_ROWS), :] = acc
            return carry

        lax.fori_loop(0, nrb, conv_block, 0)

        for k in range(CONV_WIDTH):
            q, s = divmod(HALO - (CONV_WIDTH - 1) + k, SUBLANES)
            prod = dvsh[0, 0:ts, :] * ush[s, q * SUBLANES:q * SUBLANES + ts, :]
            dwdw_ref[k:k + 1, :] += jnp.sum(prod, axis=0, keepdims=True)

        du = du_s[...]
        da = du * sgate
        dgate = du * a * sgate * (1.0 - sgate)
        dz_ref[:, 0:C] = da.astype(BF16)
        dz_ref[:, C:2 * C] = dgate.astype(BF16)
        dbin_ref[:, 0:C] += jnp.sum(da, axis=0, keepdims=True)
        dbin_ref[:, C:2 * C] += jnp.sum(dgate, axis=0, keepdims=True)

        row = lax.broadcasted_iota(jnp.int32, (ts, 1), 0) + t * ts
        for gi, w in enumerate(POOL_WINDOWS):
            lanes = slice(gi * POOL_GROUP, (gi + 1) * POOL_GROUP)
            dyp = dy[:, C + gi * POOL_GROUP:C + (gi + 1) * POOL_GROUP]
            mg = m_ref[:, lanes]
            ypre = _dot(mg, wp_ref[gi])
            dsp_ref[:, lanes] += jnp.sum(dyp * ypre, axis=0, keepdims=True)
            dyi = (dyp * sp_ref[:, lanes]).astype(BF16)
            dwp_ref[gi] += _dot_tn(mg, dyi)
            dm = _dot_nt(dyi, wp_ref[gi])
            cnt = jnp.minimum(row + 1, w).astype(F32)
            dqbuf[0:ts, lanes] = dm / cnt
            dp = -dm
            for k in range(w):
                dp = dp + dqbuf[k:k + ts, lanes]
            dz_ref[:, 2 * C + gi * POOL_GROUP:2 * C + (gi + 1) * POOL_GROUP] = dp.astype(BF16)
            dbin_ref[:, 2 * C + gi * POOL_GROUP:2 * C + (gi + 1) * POOL_GROUP] += jnp.sum(dp, axis=0, keepdims=True)

        dxn = _dot_nt(dz_ref[...], win_ref[...])
        x = x_ref[...]
        r1 = lax.rsqrt(_mean_last(x * x) + RMS_EPS)
        xhat = x * r1
        dgm_ref[...] += jnp.sum(dxn * xhat, axis=0, keepdims=True)
        dnn = dxn * g_ref[...]
        dx_ref[...] = dh1 + r1 * (dnn - xhat * _mean_last(dnn * xhat))

        dvsh[0, ts:ts + HALO, :] = dvsh[0, 0:HALO, :]
        dqbuf[ts:ts + HALO, :] = dqbuf[0:HALO, :]

    rev = lambda w: pl.BlockSpec((ts, w), lambda i: (nt - 1 - i, 0))
    halo = pl.BlockSpec((HALO, C), lambda i: (jnp.maximum((nt - 1 - i) * hb - 1, 0), 0))
    acc = lambda shape: (_full(shape), jax.ShapeDtypeStruct(shape, F32))
    outs = [(rev(D), jax.ShapeDtypeStruct((S, D), F32)), (rev(d_in), jax.ShapeDtypeStruct((S, d_in), BF16)),
            acc((1, D)), acc((1, d_in)), acc((wrows, C)), acc((1, C)), acc((1, C)), acc((1, C)),
            acc(w_pool.shape), acc((1, C))]
    return pl.pallas_call(
        body, name="mix_bwd", grid=(nt,),
        in_specs=[rev(D), rev(D), rev(C), rev(C), halo, halo, rev(C), rev(C), _full((1, D)), _full((D, d_in)),
                  _full(w_dw.shape), _full((1, C)), _full((1, C)), _full(w_pool.shape), _full((1, C)),
                  _full((D, D))],
        out_specs=[o[0] for o in outs], out_shape=[o[1] for o in outs],
        scratch_shapes=[pltpu.VMEM((SUBLANES, ts + HALO, C), F32), pltpu.VMEM((SUBLANES, ts + HALO, C), F32),
                        pltpu.VMEM((ts + HALO, C), F32), pltpu.VMEM((ts, C), F32)],
        compiler_params=pltpu.CompilerParams(dimension_semantics=("arbitrary",), vmem_limit_bytes=VMEM_LIMIT),
    )(dh1, x, a, gate, a, gate, v, m, g_mix, w_in, w_dw, ln_g, ln_b, w_pool, s_pool, w_out)


def _grad_matmul(a, b, tm, tn, ts, name):
    S, M = a.shape
    N = b.shape[1]

    def body(a_ref, b_ref, o_ref):
        @pl.when(pl.program_id(2) == 0)
        def _():
            o_ref[...] = jnp.zeros_like(o_ref)

        o_ref[...] += _dot_tn(a_ref[...].astype(BF16), b_ref[...].astype(BF16))

    return pl.pallas_call(
        body, name=name, grid=(M // tm, N // tn, S // ts),
        in_specs=[pl.BlockSpec((ts, tm), lambda i, j, k: (k, i)), pl.BlockSpec((ts, tn), lambda i, j, k: (k, j))],
        out_specs=pl.BlockSpec((tm, tn), lambda i, j, k: (i, j)),
        out_shape=jax.ShapeDtypeStruct((M, N), F32),
        compiler_params=pltpu.CompilerParams(dimension_semantics=("parallel", "parallel", "arbitrary"),
                                             vmem_limit_bytes=VMEM_LIMIT),
    )(a, b)


def _position():
    return lax.axis_index("x"), lax.axis_index("y"), lax.axis_index("c")


def _slot(px, py, pc):
    return 4 * px + 2 * py + pc


def _all_gather(shards):
    n = len(shards)
    hbm = pl.BlockSpec(memory_space=pl.ANY)

    def body(*refs):
        ins, outs = refs[:n], refs[n:2 * n]
        send_sems, recv_sems, local_sems = refs[2 * n:]
        x, y, c = _position()
        me, sibling = (x, y, c), (x, y, 1 - c)
        chips = [(1 - x, y), (x, 1 - y), (1 - x, 1 - y)]

        def copy(k, sem, block, to, src=None):
            dst = outs[k].at[_slot(*block)]
            return pltpu.make_async_remote_copy(
                src_ref=dst if src is None else src, dst_ref=dst, send_sem=send_sems.at[k, sem],
                recv_sem=recv_sems.at[k, sem], device_id=to, device_id_type=MESH)

        mine = [pltpu.make_async_copy(ins[k], outs[k].at[_slot(*me)], local_sems.at[k]) for k in range(n)]
        for cp in mine:
            cp.start()
        first = []
        for k in range(n):
            first.append(copy(k, 0, me, sibling, src=ins[k]))
            first += [copy(k, 1 + j, me, (*chip, c), src=ins[k]) for j, chip in enumerate(chips)]
        for cp in first:
            cp.start()
        passed = []
        for j, chip in enumerate(chips):
            for k in range(n):
                copy(k, 1 + j, (*chip, c), me).wait_recv()
                fwd = copy(k, 4 + j, (*chip, c), sibling)
                fwd.start()
                passed.append(fwd)
        for k in range(n):
            copy(k, 0, sibling, me).wait_recv()
            for j, chip in enumerate(chips):
                copy(k, 4 + j, (*chip, 1 - c), me).wait_recv()
        for cp in first + passed:
            cp.wait_send()
        for cp in mine:
            cp.wait()

    return pl.pallas_call(
        body, name="all_gather_weights",
        out_shape=[jax.ShapeDtypeStruct((N_DEV,) + s.shape, s.dtype) for s in shards],
        in_specs=[hbm] * n, out_specs=[hbm] * n,
        scratch_shapes=[pltpu.SemaphoreType.DMA((n, 7)), pltpu.SemaphoreType.DMA((n, 7)),
                        pltpu.SemaphoreType.DMA((n,))],
    )(*shards)


_FLIPS = [(fx, fy, fc) for fx in (0, 1) for fy in (0, 1) for fc in (0, 1)][1:]


def _reduce_scatter_exchange(parts):
    n = len(parts)
    hbm = pl.BlockSpec(memory_space=pl.ANY)

    def body(*refs):
        ins, outs = refs[:n], refs[n:2 * n]
        send_sems, recv_sems, local_sems = refs[2 * n:]
        x, y, c = _position()
        me = _slot(x, y, c)
        mine = [pltpu.make_async_copy(ins[k].at[me], outs[k].at[me], local_sems.at[k]) for k in range(n)]
        for cp in mine:
            cp.start()
        copies = []
        for f, (fx, fy, fc) in enumerate(_FLIPS):
            peer = (x ^ fx, y ^ fy, c ^ fc)
            for k in range(n):
                cp = pltpu.make_async_remote_copy(
                    src_ref=ins[k].at[_slot(*peer)], dst_ref=outs[k].at[me], send_sem=send_sems.at[k, f],
                    recv_sem=recv_sems.at[k, f], device_id=peer, device_id_type=MESH)
                cp.start()
                copies.append((cp, k, f, peer))
        for cp, k, f, peer in copies:
            pltpu.make_async_remote_copy(
                src_ref=ins[k].at[me], dst_ref=outs[k].at[_slot(*peer)], send_sem=send_sems.at[k, f],
                recv_sem=recv_sems.at[k, f], device_id=peer, device_id_type=MESH).wait_recv()
        for cp, k, f, peer in copies:
            cp.wait_send()
        for cp in mine:
            cp.wait()

    return pl.pallas_call(
        body, name="reduce_scatter_exchange",
        out_shape=[jax.ShapeDtypeStruct(p.shape, p.dtype) for p in parts],
        in_specs=[hbm] * n, out_specs=[hbm] * n,
        scratch_shapes=[pltpu.SemaphoreType.DMA((n, 7)), pltpu.SemaphoreType.DMA((n, 7)),
                        pltpu.SemaphoreType.DMA((n,))],
    )(*parts)


def _all_reduce_small(vals):
    n = len(vals)
    vm = pl.BlockSpec(memory_space=pltpu.VMEM)

    def body(*refs):
        ins, outs = refs[:n], refs[n:2 * n]
        bufs = refs[2 * n:3 * n]
        send_sems, recv_sems = refs[3 * n:]
        x, y, c = _position()
        me = _slot(x, y, c)
        copies = []
        for f, (fx, fy, fc) in enumerate(_FLIPS):
            peer = (x ^ fx, y ^ fy, c ^ fc)
            for k in range(n):
                cp = pltpu.make_async_remote_copy(
                    src_ref=ins[k], dst_ref=bufs[k].at[me], send_sem=send_sems.at[k, f],
                    recv_sem=recv_sems.at[k, f], device_id=peer, device_id_type=MESH)
                cp.start()
                copies.append((cp, k, f, peer))
        for k in range(n):
            bufs[k][me] = ins[k][...]
        for cp, k, f, peer in copies:
            pltpu.make_async_remote_copy(
                src_ref=ins[k], dst_ref=bufs[k].at[_slot(*peer)], send_sem=send_sems.at[k, f],
                recv_sem=recv_sems.at[k, f], device_id=peer, device_id_type=MESH).wait_recv()
        for k in range(n):
            total = bufs[k][0]
            for j in range(1, N_DEV):
                total = total + bufs[k][j]
            outs[k][...] = total
        for cp, k, f, peer in copies:
            cp.wait_send()

    return pl.pallas_call(
        body, name="all_reduce_small",
        out_shape=[jax.ShapeDtypeStruct(v.shape, v.dtype) for v in vals],
        in_specs=[vm] * n, out_specs=[vm] * n,
        scratch_shapes=[pltpu.VMEM((N_DEV,) + v.shape, v.dtype) for v in vals]
        + [pltpu.SemaphoreType.DMA((n, 7)), pltpu.SemaphoreType.DMA((n, 7))],
    )(*vals)


def _adamw_math(w, g, m, v):
    m = ADAM_B1 * m + (1.0 - ADAM_B1) * g
    v = ADAM_B2 * v + (1.0 - ADAM_B2) * (g * g)
    m_hat = m / (1.0 - ADAM_B1 ** ADAM_STEP)
    v_hat = v / (1.0 - ADAM_B2 ** ADAM_STEP)
    delta = -ADAM_LR * (m_hat / (jnp.sqrt(v_hat) + ADAM_EPS) + ADAM_WD * w)
    return delta, m, v


def _adamw_shard(parts, w, m, v, tr, name):
    R, Cc = w.shape

    def body(p_ref, w_ref, m_ref, v_ref, g_out, d_out, m_out, v_out):
        g = p_ref[0].astype(F32)
        for j in range(1, N_DEV):
            g = g + p_ref[j].astype(F32)
        d, mn, vn = _adamw_math(w_ref[...], g, m_ref[...], v_ref[...])
        g_out[...] = g
        d_out[...] = d
        m_out[...] = mn
        v_out[...] = vn

    blk = pl.BlockSpec((tr, Cc), lambda i: (i, 0))
    return pl.pallas_call(
        body, name=name, grid=(R // tr,),
        in_specs=[pl.BlockSpec((N_DEV, tr, Cc), lambda i: (0, i, 0)), blk, blk, blk],
        out_specs=[blk] * 4, out_shape=[jax.ShapeDtypeStruct((R, Cc), F32)] * 4,
        compiler_params=pltpu.CompilerParams(dimension_semantics=("parallel",)),
    )(parts, w, m, v)


def _adamw_small(grads, ws, ms, vs):
    n = len(grads)
    vm = pl.BlockSpec(memory_space=pltpu.VMEM)

    def body(*refs):
        g_in, w_in, m_in, v_in = (refs[k * n:(k + 1) * n] for k in range(4))
        g_out, d_out, m_out, v_out = (refs[(4 + k) * n:(5 + k) * n] for k in range(4))
        for k in range(n):
            g = g_in[k][...]
            d, mn, vn = _adamw_math(w_in[k][...], g, m_in[k][...], v_in[k][...])
            g_out[k][...] = g
            d_out[k][...] = d
            m_out[k][...] = mn
            v_out[k][...] = vn

    shapes = [jax.ShapeDtypeStruct(g.shape, F32) for g in grads]
    return pl.pallas_call(
        body, name="adamw_small", out_shape=shapes * 4, in_specs=[vm] * (4 * n), out_specs=[vm] * (4 * n),
    )(*grads, *ws, *ms, *vs)


def _unshard_cols(g):
    n, R, Cc = g.shape
    return jnp.transpose(g, (1, 0, 2)).reshape(R, n * Cc)


def _shard_cols(full):
    R, Ct = full.shape
    return jnp.transpose(full.reshape(R, N_DEV, Ct // N_DEV), (1, 0, 2))


def _step(args, ts_mix_fwd, ts_ffn, ts_mix_bwd, ts_grad):
    (x, g_mix, w_in, b_in, w_dw, b_dw, ln_g, ln_b, w_pool, s_pool, w_out, g_ffn, w_gate, w_up, w_down, g_final,
     loss_target) = args[:17]
    names = ["g_mix", "w_in", "b_in", "w_dw", "b_dw", "ln_g", "ln_b", "w_pool", "s_pool", "w_out", "g_ffn",
             "w_gate", "w_up", "w_down", "g_final"]
    weights = dict(zip(names, args[1:16]))
    moms = dict(zip(names, args[17:32]))
    vars_ = dict(zip(names, args[32:47]))

    S, D = x.shape[1], x.shape[2]
    x2 = x.reshape(S, D)
    tgt2 = loss_target.reshape(S, D)

    shards = [w_in[0].astype(BF16), w_out[0].astype(BF16), w_gate[0].astype(BF16), w_up[0].astype(BF16),
              w_down[0].astype(BF16), w_dw[0]]
    g_in, g_out, g_gate, g_up, g_down, g_dw = _all_gather(shards)
    w_in_f = _unshard_cols(g_in)
    w_gate_f = _unshard_cols(g_gate)
    w_up_f = _unshard_cols(g_up)
    w_dw_f = _unshard_cols(g_dw)
    w_out_f = g_out.reshape(-1, g_out.shape[-1])
    w_down_f = g_down.reshape(-1, g_down.shape[-1])
    w_pool_b = w_pool[0].astype(BF16)
    Fd = w_gate_f.shape[1]

    a, gate, v, m, y, h1, xn = _mix_fwd(x2, g_mix, w_in_f, b_in, w_dw_f, b_dw, ln_g, ln_b, w_pool_b, s_pool, w_out_f,
                                       ts_mix_fwd)
    f_chunks = [1024] * (Fd // 1024) + ([Fd % 1024] if Fd % 1024 else [])
    dh1, hn, act, dgt, dup, dh2, loss_p, dg_final, dg_ffn = _ffn(
        h1, tgt2, g_ffn, g_final.reshape(1, D), w_gate_f, w_up_f, w_down_f, ts_ffn, f_chunks)
    dx, dz, dg_mix, db_in, dw_dw, db_dw, dln_g, dln_b, dw_pool, ds_pool = _mix_bwd(
        dh1, x2, a, gate, v, m, g_mix, w_in_f, w_dw_f, ln_g, ln_b, w_pool_b, s_pool, w_out_f, ts_mix_bwd)

    d_in = w_in_f.shape[1]
    dw_in = _grad_matmul(xn, dz, D, d_in // 2, ts_grad, "grad_w_in")
    dw_out = _grad_matmul(y, dh1, D, D, ts_grad, "grad_w_out")
    dw_gate = _grad_matmul(hn, dgt, D, Fd // 2, ts_grad, "grad_w_gate")
    dw_up = _grad_matmul(hn, dup, D, Fd // 2, ts_grad, "grad_w_up")
    dw_down = _grad_matmul(act, dh2, Fd // 2, D, ts_grad, "grad_w_down")

    parts = [_shard_cols(dw_in).astype(BF16), dw_out.reshape(N_DEV, -1, D).astype(BF16),
             _shard_cols(dw_gate).astype(BF16), _shard_cols(dw_up).astype(BF16),
             dw_down.reshape(N_DEV, -1, D).astype(BF16), _shard_cols(dw_dw[0:CONV_WIDTH])]
    recv = _reduce_scatter_exchange(parts)
    big = {}
    for nm, r, tr in zip(["w_in", "w_out", "w_gate", "w_up", "w_down", "w_dw"], recv,
                         [256, 128, 256, 256, 176, CONV_WIDTH]):
        big[nm] = _adamw_shard(r, weights[nm][0], moms[nm][0], vars_[nm][0], tr, "adamw_" + nm)

    small_names = ["g_mix", "b_in", "b_dw", "ln_g", "ln_b", "w_pool", "s_pool", "g_ffn", "g_final"]
    small_shape = lambda p: p.reshape(-1, p.shape[-1])
    partial = [dg_mix, db_in, db_dw, dln_g, dln_b, dw_pool.reshape(-1, POOL_GROUP), ds_pool, dg_ffn, dg_final, loss_p]
    summed = _all_reduce_small(partial)
    sm = _adamw_small(summed[:-1], [small_shape(weights[nm]) for nm in small_names],
                      [small_shape(moms[nm]) for nm in small_names], [small_shape(vars_[nm]) for nm in small_names])
    n_small = len(small_names)

    def result(kind, nm):
        if nm in big:
            return big[nm][kind].reshape(weights[nm].shape)
        return sm[kind * n_small + small_names.index(nm)].reshape(weights[nm].shape)

    loss = summed[-1][0, 0]
    out = [loss, dx.reshape(x.shape)]
    for kind in range(4):
        out += [result(kind, nm) for nm in names]
    return tuple(out)


def kernel(x, g_mix, w_in, b_in, w_dw, b_dw, ln_g, ln_b, w_pool, s_pool, w_out, g_ffn, w_gate, w_up, w_down, g_final, loss_target, m_g_mix, m_w_in, m_b_in, m_w_dw, m_b_dw, m_ln_g, m_ln_b, m_w_pool, m_s_pool, m_w_out, m_g_ffn, m_w_gate, m_w_up, m_w_down, m_g_final, v_g_mix, v_w_in, v_b_in, v_w_dw, v_b_dw, v_ln_g, v_ln_b, v_w_pool, v_s_pool, v_w_out, v_g_ffn, v_w_gate, v_w_up, v_w_down, v_g_final):
    args = (x, g_mix, w_in, b_in, w_dw, b_dw, ln_g, ln_b, w_pool, s_pool, w_out, g_ffn, w_gate, w_up, w_down, g_final, loss_target, m_g_mix, m_w_in, m_b_in, m_w_dw, m_b_dw, m_ln_g, m_ln_b, m_w_pool, m_s_pool, m_w_out, m_g_ffn, m_w_gate, m_w_up, m_w_down, m_g_final, v_g_mix, v_w_in, v_b_in, v_w_dw, v_b_dw, v_ln_g, v_ln_b, v_w_pool, v_s_pool, v_w_out, v_g_ffn, v_w_gate, v_w_up, v_w_down, v_g_final)
    return _step(args, ts_mix_fwd=512, ts_ffn=256, ts_mix_bwd=256, ts_grad=512)
```

```python
import functools

import jax
import jax.numpy as jnp
from jax import lax
from jax.experimental import pallas as pl
from jax.experimental.pallas import tpu as pltpu

F32 = jnp.float32
BF16 = jnp.bfloat16
MESH = pl.DeviceIdType.MESH
N_DEV = 8

C_CONV = 512
CONV_WIDTH = 31
POOL_WINDOWS = (2, 4, 8, 16)
POOL_GROUP = 128
RMS_EPS = 1e-6
LN_EPS = 1e-5

ADAM_LR = 0.001
ADAM_B1 = 0.9
ADAM_B2 = 0.999
ADAM_EPS = 1e-08
ADAM_WD = 0.01
ADAM_STEP = 10

HALO = 32
SUBLANES = 8
CONV_ROWS = 64
VMEM_LIMIT = 56 * 1024 * 1024


def _dot(a, b):
    return jnp.dot(a, b, preferred_element_type=F32)


def _dot_nt(a, b):
    return lax.dot_general(a, b, (((1,), (1,)), ((), ())), preferred_element_type=F32)


def _dot_tn(a, b):
    return lax.dot_general(a, b, (((0,), (0,)), ((), ())), preferred_element_type=F32)


def _mean_last(v):
    return jnp.mean(v, axis=-1, keepdims=True)


def _full(shape):
    nd = len(shape)
    return pl.BlockSpec(shape, lambda *_: (0,) * nd)


def _full1(shape):
    nd = len(shape)
    return pl.BlockSpec(shape, lambda *_: (0,) * nd, pipeline_mode=pl.Buffered(1))


def _shifted_copies(sh_ref, rows):
    for s in range(1, SUBLANES):
        sh_ref[s, 0:rows, :] = sh_ref[0, s:s + rows, :]


def _tap(sh_ref, off, r0, rows):
    q, s = divmod(off, SUBLANES)
    return sh_ref[s, pl.ds(r0 + q * SUBLANES, rows), :]


def _mix_fwd(x, g_mix, w_in, b_in, w_dw, b_dw, ln_g, ln_b, w_pool, s_pool, w_out, ts):
    S, D = x.shape
    d_in = w_in.shape[0]
    C = C_CONV
    nt = S // ts
    nrb = ts // CONV_ROWS

    def body(x_ref, g_ref, win_ref, bin_ref, wdw_ref, bdw_ref, lng_ref, lnb_ref, wp_ref, sp_ref, wout_ref,
             a_ref, gate_ref, v_ref, m_ref, y_ref, h1_ref, xn_ref, ush, pbuf):
        i = pl.program_id(0)

        @pl.when(i == 0)
        def _():
            ush[0, 0:HALO, :] = jnp.zeros((HALO, C), F32)
            pbuf[0:HALO, :] = jnp.zeros((HALO, C), F32)

        x = x_ref[...]
        r1 = lax.rsqrt(_mean_last(x * x) + RMS_EPS)
        xn = (x * r1 * g_ref[...]).astype(BF16)
        xn_ref[...] = xn
        z = _dot_nt(xn, win_ref[...]) + bin_ref[...]
        a = z[:, 0:C]
        gate = z[:, C:2 * C]
        a_ref[...] = a
        gate_ref[...] = gate
        ush[0, HALO:HALO + ts, :] = a * jax.nn.sigmoid(gate)
        pbuf[HALO:HALO + ts, :] = z[:, 2 * C:]

        _shifted_copies(ush, ts + HALO - SUBLANES)

        def conv_block(rb, carry):
            r0 = pl.multiple_of(rb * CONV_ROWS, CONV_ROWS)
            acc = jnp.zeros((CONV_ROWS, C), F32)
            for k in range(CONV_WIDTH):
                acc = acc + wdw_ref[k:k + 1, :] * _tap(ush, HALO - (CONV_WIDTH - 1) + k, r0, CONV_ROWS)
            v_ref[pl.ds(r0, CONV_ROWS), :] = acc + bdw_ref[...]
            return carry

        lax.fori_loop(0, nrb, conv_block, 0)

        v = v_ref[...]
        mu = _mean_last(v)
        xc = v - mu
        rstd = lax.rsqrt(_mean_last(xc * xc) + LN_EPS)
        ln = xc * rstd * lng_ref[...] + lnb_ref[...]
        y_ref[:, 0:C] = (ln * jax.nn.sigmoid(ln)).astype(BF16)

        row = lax.broadcasted_iota(jnp.int32, (ts, 1), 0) + i * ts
        for gi, w in enumerate(POOL_WINDOWS):
            lanes = slice(gi * POOL_GROUP, (gi + 1) * POOL_GROUP)
            seg = pbuf[HALO:HALO + ts, lanes]
            ws = seg
            for k in range(1, w):
                ws = ws + pbuf[HALO - k:HALO - k + ts, lanes]
            cnt = jnp.minimum(row + 1, w).astype(F32)
            m = (ws / cnt - seg).astype(BF16)
            m_ref[:, lanes] = m
            ypre = _dot(m, wp_ref[gi])
            y_ref[:, C + gi * POOL_GROUP:C + (gi + 1) * POOL_GROUP] = (ypre * sp_ref[:, lanes]).astype(BF16)

        h1_ref[...] = x + _dot(y_ref[...], wout_ref[...])

        ush[0, 0:HALO, :] = ush[0, ts:ts + HALO, :]
        pbuf[0:HALO, :] = pbuf[ts:ts + HALO, :]

    tile = lambda w, dt: (pl.BlockSpec((ts, w), lambda i: (i, 0)), jax.ShapeDtypeStruct((S, w), dt))
    outs = [tile(C, F32), tile(C, F32), tile(C, F32), tile(C, BF16), tile(D, BF16), tile(D, F32), tile(D, BF16)]
    return pl.pallas_call(
        body, name="mix_fwd", grid=(nt,),
        in_specs=[pl.BlockSpec((ts, D), lambda i: (i, 0)), _full((1, D)), _full((d_in, D)), _full((1, d_in)),
                  _full(w_dw.shape), _full((1, C)), _full((1, C)), _full((1, C)), _full(w_pool.shape),
                  _full((1, C)), _full((D, D))],
        out_specs=[o[0] for o in outs], out_shape=[o[1] for o in outs],
        scratch_shapes=[pltpu.VMEM((SUBLANES, ts + HALO, C), F32), pltpu.VMEM((ts + HALO, C), F32)],
        compiler_params=pltpu.CompilerParams(dimension_semantics=("arbitrary",), vmem_limit_bytes=VMEM_LIMIT),
    )(x, g_mix, w_in, b_in, w_dw, b_dw, ln_g, ln_b, w_pool, s_pool, w_out)


def _ffn(h1, target, g_ffn, g_final, w_gate, w_up, w_down, ts, f_chunks):
    S, D = h1.shape
    Fd = w_gate.shape[0]
    nt = S // ts
    bounds = []
    lo = 0
    for n in f_chunks:
        bounds.append((lo, lo + n))
        lo += n
    assert lo == Fd

    def body(h1_ref, tgt_ref, gf_ref, gl_ref, wg_ref, wu_ref, wd_ref,
             dh1_ref, hn_ref, act_ref, dgt_ref, dup_ref, dh2_ref, loss_ref, dgl_ref, dgf_ref, gt_s, up_s):
        i = pl.program_id(0)

        @pl.when(i == 0)
        def _():
            loss_ref[...] = jnp.zeros_like(loss_ref)
            dgl_ref[...] = jnp.zeros_like(dgl_ref)
            dgf_ref[...] = jnp.zeros_like(dgf_ref)

        h1 = h1_ref[...]
        r2 = lax.rsqrt(_mean_last(h1 * h1) + RMS_EPS)
        hhat = h1 * r2
        hn = (hhat * gf_ref[...]).astype(BF16)
        hn_ref[...] = hn
        h2 = h1
        for lo, hi in bounds:
            gt = _dot_nt(hn, wg_ref[lo:hi, :])
            up = _dot_nt(hn, wu_ref[lo:hi, :])
            gt_s[:, lo:hi] = gt
            up_s[:, lo:hi] = up
            act = (gt * jax.nn.sigmoid(gt) * up).astype(BF16)
            act_ref[:, lo:hi] = act
            h2 = h2 + _dot(act, wd_ref[lo:hi, :])

        r3 = lax.rsqrt(_mean_last(h2 * h2) + RMS_EPS)
        n3 = h2 * r3
        gl = gl_ref[...]
        diff = n3 * gl - tgt_ref[...]
        loss_ref[...] += jnp.sum(0.5 * jnp.sum(diff * diff, axis=-1, keepdims=True) / D, axis=0, keepdims=True)
        dout = diff / D
        dgl_ref[...] += jnp.sum(dout * n3, axis=0, keepdims=True)
        dn = dout * gl
        dh2 = r3 * (dn - n3 * _mean_last(dn * n3))
        dh2b = dh2.astype(BF16)
        dh2_ref[...] = dh2b

        dhn = jnp.zeros((ts, D), F32)
        for lo, hi in bounds:
            gt = gt_s[:, lo:hi]
            up = up_s[:, lo:hi]
            sg = jax.nn.sigmoid(gt)
            dact = _dot_nt(dh2b, wd_ref[lo:hi, :])
            dgt = (dact * up * (sg * (1.0 + gt * (1.0 - sg)))).astype(BF16)
            dup = (dact * (gt * sg)).astype(BF16)
            dgt_ref[:, lo:hi] = dgt
            dup_ref[:, lo:hi] = dup
            dhn = dhn + _dot(dgt, wg_ref[lo:hi, :]) + _dot(dup, wu_ref[lo:hi, :])

        dgf_ref[...] += jnp.sum(dhn * hhat, axis=0, keepdims=True)
        dnn = dhn * gf_ref[...]
        dh1_ref[...] = dh2 + r2 * (dnn - hhat * _mean_last(dnn * hhat))

    tile = lambda w, dt: (pl.BlockSpec((ts, w), lambda i: (i, 0)), jax.ShapeDtypeStruct((S, w), dt))
    acc = lambda w: (_full((1, w)), jax.ShapeDtypeStruct((1, w), F32))
    outs = [tile(D, F32), tile(D, BF16), tile(Fd, BF16), tile(Fd, BF16), tile(Fd, BF16), tile(D, BF16),
            acc(128), acc(D), acc(D)]
    return pl.pallas_call(
        body, name="ffn_fwd_bwd", grid=(nt,),
        in_specs=[pl.BlockSpec((ts, D), lambda i: (i, 0)), pl.BlockSpec((ts, D), lambda i: (i, 0)),
                  _full((1, D)), _full((1, D)), _full1((Fd, D)), _full1((Fd, D)), _full1((Fd, D))],
        out_specs=[o[0] for o in outs], out_shape=[o[1] for o in outs],
        scratch_shapes=[pltpu.VMEM((ts, Fd), F32), pltpu.VMEM((ts, Fd), F32)],
        compiler_params=pltpu.CompilerParams(dimension_semantics=("arbitrary",), vmem_limit_bytes=VMEM_LIMIT),
    )(h1, target, g_ffn, g_final, w_gate, w_up, w_down)


def _mix_bwd(dh1, x, a, gate, v, m, g_mix, w_in, w_dw, ln_g, ln_b, w_pool, s_pool, w_out, ts):
    S, D = x.shape
    d_in = w_in.shape[0]
    C = C_CONV
    nt = S // ts
    nrb = ts // CONV_ROWS
    hb = ts // HALO
    wrows = ((CONV_WIDTH + SUBLANES - 1) // SUBLANES) * SUBLANES

    def body(dh1_ref, x_ref, a_ref, gate_ref, ah_ref, gh_ref, v_ref, m_ref, g_ref, win_ref, wdw_ref, lng_ref,
             lnb_ref, wp_ref, sp_ref, wout_ref,
             dx_ref, dz_ref, dgm_ref, dbin_ref, dwdw_ref, dbdw_ref, dlng_ref, dlnb_ref, dwp_ref, dsp_ref,
             ush, dvsh, dqbuf, du_s):
        i = pl.program_id(0)
        t = nt - 1 - i

        @pl.when(i == 0)
        def _():
            dvsh[0, ts:ts + HALO, :] = jnp.zeros((HALO, C), F32)
            dqbuf[ts:ts + HALO, :] = jnp.zeros((HALO, C), F32)
            for r in (dgm_ref, dbin_ref, dwdw_ref, dbdw_ref, dlng_ref, dlnb_ref, dwp_ref, dsp_ref):
                r[...] = jnp.zeros_like(r)

        dh1 = dh1_ref[...]
        dy = _dot_nt(dh1.astype(BF16), wout_ref[...])

        v = v_ref[...]
        mu = _mean_last(v)
        xc = v - mu
        rstd = lax.rsqrt(_mean_last(xc * xc) + LN_EPS)
        vhat = xc * rstd
        lng = lng_ref[...]
        ln = vhat * lng + lnb_ref[...]
        sg = jax.nn.sigmoid(ln)
        dln = dy[:, 0:C] * (sg * (1.0 + ln * (1.0 - sg)))
        dlng_ref[...] += jnp.sum(dln * vhat, axis=0, keepdims=True)
        dlnb_ref[...] += jnp.sum(dln, axis=0, keepdims=True)
        dvh = dln * lng
        dv = rstd * (dvh - _mean_last(dvh) - vhat * _mean_last(dvh * vhat))
        dbdw_ref[...] += jnp.sum(dv, axis=0, keepdims=True)
        dvsh[0, 0:ts, :] = dv
        _shifted_copies(dvsh, ts + HALO - SUBLANES)

        a = a_ref[...]
        sgate = jax.nn.sigmoid(gate_ref[...])
        ush[0, HALO:HALO + ts, :] = a * sgate
        uh = ah_ref[...] * jax.nn.sigmoid(gh_ref[...])
        ush[0, 0:HALO, :] = jnp.where(t > 0, uh, 0.0)
        _shifted_copies(ush, ts + HALO - SUBLANES)

        def conv_block(rb, carry):
            r0 = pl.multiple_of(rb * CONV_ROWS, CONV_ROWS)
            acc = jnp.zeros((CONV_ROWS, C), F32)
            for k in range(CONV_WIDTH):
                acc = acc + wdw_ref[k:k + 1, :] * _tap(dvsh, CONV_WIDTH - 1 - k, r0, CONV_ROWS)
            du_s[pl.ds(r0, CONV_ROWS), :] = acc
            return carry

        lax.fori_loop(0, nrb, conv_block, 0)

        for k in range(CONV_WIDTH):
            q, s = divmod(HALO - (CONV_WIDTH - 1) + k, SUBLANES)
            prod = dvsh[0, 0:ts, :] * ush[s, q * SUBLANES:q * SUBLANES + ts, :]
            dwdw_ref[k:k + 1, :] += jnp.sum(prod, axis=0, keepdims=True)

        du = du_s[...]
        da = du * sgate
        dgate = du * a * sgate * (1.0 - sgate)
        dz_ref[:, 0:C] = da.astype(BF16)
        dz_ref[:, C:2 * C] = dgate.astype(BF16)
        dbin_ref[:, 0:C] += jnp.sum(da, axis=0, keepdims=True)
        dbin_ref[:, C:2 * C] += jnp.sum(dgate, axis=0, keepdims=True)

        row = lax.broadcasted_iota(jnp.int32, (ts, 1), 0) + t * ts
        for gi, w in enumerate(POOL_WINDOWS):
            lanes = slice(gi * POOL_GROUP, (gi + 1) * POOL_GROUP)
            dyp = dy[:, C + gi * POOL_GROUP:C + (gi + 1) * POOL_GROUP]
            mg = m_ref[:, lanes]
            ypre = _dot(mg, wp_ref[gi])
            dsp_ref[:, lanes] += jnp.sum(dyp * ypre, axis=0, keepdims=True)
            dyi = (dyp * sp_ref[:, lanes]).astype(BF16)
            dwp_ref[gi] += _dot_tn(mg, dyi)
            dm = _dot_nt(dyi, wp_ref[gi])
            cnt = jnp.minimum(row + 1, w).astype(F32)
            dqbuf[0:ts, lanes] = dm / cnt
            dp = -dm
            for k in range(w):
                dp = dp + dqbuf[k:k + ts, lanes]
            dz_ref[:, 2 * C + gi * POOL_GROUP:2 * C + (gi + 1) * POOL_GROUP] = dp.astype(BF16)
            dbin_ref[:, 2 * C + gi * POOL_GROUP:2 * C + (gi + 1) * POOL_GROUP] += jnp.sum(dp, axis=0, keepdims=True)

        dxn = _dot(dz_ref[...], win_ref[...])
        x = x_ref[...]
        r1 = lax.rsqrt(_mean_last(x * x) + RMS_EPS)
        xhat = x * r1
        dgm_ref[...] += jnp.sum(dxn * xhat, axis=0, keepdims=True)
        dnn = dxn * g_ref[...]
        dx_ref[...] = dh1 + r1 * (dnn - xhat * _mean_last(dnn * xhat))

        dvsh[0, ts:ts + HALO, :] = dvsh[0, 0:HALO, :]
        dqbuf[ts:ts + HALO, :] = dqbuf[0:HALO, :]

    rev = lambda w: pl.BlockSpec((ts, w), lambda i: (nt - 1 - i, 0))
    halo = pl.BlockSpec((HALO, C), lambda i: (jnp.maximum((nt - 1 - i) * hb - 1, 0), 0))
    acc = lambda shape: (_full(shape), jax.ShapeDtypeStruct(shape, F32))
    outs = [(rev(D), jax.ShapeDtypeStruct((S, D), F32)), (rev(d_in), jax.ShapeDtypeStruct((S, d_in), BF16)),
            acc((1, D)), acc((1, d_in)), acc((wrows, C)), acc((1, C)), acc((1, C)), acc((1, C)),
            acc(w_pool.shape), acc((1, C))]
    return pl.pallas_call(
        body, name="mix_bwd", grid=(nt,),
        in_specs=[rev(D), rev(D), rev(C), rev(C), halo, halo, rev(C), rev(C), _full((1, D)), _full((d_in, D)),
                  _full(w_dw.shape), _full((1, C)), _full((1, C)), _full(w_pool.shape), _full((1, C)),
                  _full((D, D))],
        out_specs=[o[0] for o in outs], out_shape=[o[1] for o in outs],
        scratch_shapes=[pltpu.VMEM((SUBLANES, ts + HALO, C), F32), pltpu.VMEM((SUBLANES, ts + HALO, C), F32),
                        pltpu.VMEM((ts + HALO, C), F32), pltpu.VMEM((ts, C), F32)],
        compiler_params=pltpu.CompilerParams(dimension_semantics=("arbitrary",), vmem_limit_bytes=VMEM_LIMIT),
    )(dh1, x, a, gate, a, gate, v, m, g_mix, w_in, w_dw, ln_g, ln_b, w_pool, s_pool, w_out)


def _grad_matmul(a, b, tm, tn, ts, name):
    S, M = a.shape
    N = b.shape[1]
    nk = S // ts

    def body(a_ref, b_ref, o_ref, acc):
        k = pl.program_id(2)

        @pl.when(k == 0)
        def _():
            acc[...] = jnp.zeros_like(acc)

        acc[...] += _dot_tn(a_ref[...].astype(BF16), b_ref[...].astype(BF16))

        @pl.when(k == nk - 1)
        def _():
            o_ref[...] = acc[...].astype(BF16)

    return pl.pallas_call(
        body, name=name, grid=(M // tm, N // tn, nk),
        in_specs=[pl.BlockSpec((ts, tm), lambda i, j, k: (k, i)), pl.BlockSpec((ts, tn), lambda i, j, k: (k, j))],
        out_specs=pl.BlockSpec((tm, tn), lambda i, j, k: (i, j)),
        out_shape=jax.ShapeDtypeStruct((M, N), BF16),
        scratch_shapes=[pltpu.VMEM((tm, tn), F32)],
        compiler_params=pltpu.CompilerParams(dimension_semantics=("parallel", "parallel", "arbitrary"),
                                             vmem_limit_bytes=VMEM_LIMIT),
    )(a, b)


def _position():
    return lax.axis_index("x"), lax.axis_index("y"), lax.axis_index("c")


def _slot(px, py, pc):
    return 4 * px + 2 * py + pc


def _all_gather(shards):
    n = len(shards)
    hbm = pl.BlockSpec(memory_space=pl.ANY)

    def body(*refs):
        ins, outs = refs[:n], refs[n:2 * n]
        send_sems, recv_sems, local_sems = refs[2 * n:]
        x, y, c = _position()
        me, sibling = (x, y, c), (x, y, 1 - c)
        chips = [(1 - x, y), (x, 1 - y), (1 - x, 1 - y)]

        def copy(k, sem, block, to, src=None):
            dst = outs[k].at[_slot(*block)]
            return pltpu.make_async_remote_copy(
                src_ref=dst if src is None else src, dst_ref=dst, send_sem=send_sems.at[k, sem],
                recv_sem=recv_sems.at[k, sem], device_id=to, device_id_type=MESH)

        mine = [pltpu.make_async_copy(ins[k], outs[k].at[_slot(*me)], local_sems.at[k]) for k in range(n)]
        for cp in mine:
            cp.start()
        first = []
        for k in range(n):
            first.append(copy(k, 0, me, sibling, src=ins[k]))
            first += [copy(k, 1 + j, me, (*chip, c), src=ins[k]) for j, chip in enumerate(chips)]
        for cp in first:
            cp.start()
        passed = []
        for j, chip in enumerate(chips):
            for k in range(n):
                copy(k, 1 + j, (*chip, c), me).wait_recv()
                fwd = copy(k, 4 + j, (*chip, c), sibling)
                fwd.start()
                passed.append(fwd)
        for k in range(n):
            copy(k, 0, sibling, me).wait_recv()
            for j, chip in enumerate(chips):
                copy(k, 4 + j, (*chip, 1 - c), me).wait_recv()
        for cp in first + passed:
            cp.wait_send()
        for cp in mine:
            cp.wait()

    return pl.pallas_call(
        body, name="all_gather_weights",
        out_shape=[jax.ShapeDtypeStruct((N_DEV,) + s.shape, s.dtype) for s in shards],
        in_specs=[hbm] * n, out_specs=[hbm] * n,
        scratch_shapes=[pltpu.SemaphoreType.DMA((n, 7)), pltpu.SemaphoreType.DMA((n, 7)),
                        pltpu.SemaphoreType.DMA((n,))],
    )(*shards)


_FLIPS = [(fx, fy, fc) for fx in (0, 1) for fy in (0, 1) for fc in (0, 1)][1:]


def _reduce_scatter_exchange(parts):
    n = len(parts)
    hbm = pl.BlockSpec(memory_space=pl.ANY)

    def body(*refs):
        ins, outs = refs[:n], refs[n:2 * n]
        send_sems, recv_sems, local_sems = refs[2 * n:]
        x, y, c = _position()
        me = _slot(x, y, c)
        mine = [pltpu.make_async_copy(ins[k].at[me], outs[k].at[me], local_sems.at[k]) for k in range(n)]
        for cp in mine:
            cp.start()
        copies = []
        for f, (fx, fy, fc) in enumerate(_FLIPS):
            peer = (x ^ fx, y ^ fy, c ^ fc)
            for k in range(n):
                cp = pltpu.make_async_remote_copy(
                    src_ref=ins[k].at[_slot(*peer)], dst_ref=outs[k].at[me], send_sem=send_sems.at[k, f],
                    recv_sem=recv_sems.at[k, f], device_id=peer, device_id_type=MESH)
                cp.start()
                copies.append((cp, k, f, peer))
        for cp, k, f, peer in copies:
            pltpu.make_async_remote_copy(
                src_ref=ins[k].at[me], dst_ref=outs[k].at[_slot(*peer)], send_sem=send_sems.at[k, f],
                recv_sem=recv_sems.at[k, f], device_id=peer, device_id_type=MESH).wait_recv()
        for cp, k, f, peer in copies:
            cp.wait_send()
        for cp in mine:
            cp.wait()

    return pl.pallas_call(
        body, name="reduce_scatter_exchange",
        out_shape=[jax.ShapeDtypeStruct(p.shape, p.dtype) for p in parts],
        in_specs=[hbm] * n, out_specs=[hbm] * n,
        scratch_shapes=[pltpu.SemaphoreType.DMA((n, 7)), pltpu.SemaphoreType.DMA((n, 7)),
                        pltpu.SemaphoreType.DMA((n,))],
    )(*parts)


def _all_reduce_small(vals):
    n = len(vals)
    vm = pl.BlockSpec(memory_space=pltpu.VMEM)

    def body(*refs):
        ins, outs = refs[:n], refs[n:2 * n]
        bufs = refs[2 * n:3 * n]
        send_sems, recv_sems = refs[3 * n:]
        x, y, c = _position()
        me = _slot(x, y, c)
        copies = []
        for f, (fx, fy, fc) in enumerate(_FLIPS):
            peer = (x ^ fx, y ^ fy, c ^ fc)
            for k in range(n):
                cp = pltpu.make_async_remote_copy(
                    src_ref=ins[k], dst_ref=bufs[k].at[me], send_sem=send_sems.at[k, f],
                    recv_sem=recv_sems.at[k, f], device_id=peer, device_id_type=MESH)
                cp.start()
                copies.append((cp, k, f, peer))
        for k in range(n):
            bufs[k][me] = ins[k][...]
        for cp, k, f, peer in copies:
            pltpu.make_async_remote_copy(
                src_ref=ins[k], dst_ref=bufs[k].at[_slot(*peer)], send_sem=send_sems.at[k, f],
                recv_sem=recv_sems.at[k, f], device_id=peer, device_id_type=MESH).wait_recv()
        for k in range(n):
            total = bufs[k][0]
            for j in range(1, N_DEV):
                total = total + bufs[k][j]
            outs[k][...] = total
        for cp, k, f, peer in copies:
            cp.wait_send()

    return pl.pallas_call(
        body, name="all_reduce_small",
        out_shape=[jax.ShapeDtypeStruct(v.shape, v.dtype) for v in vals],
        in_specs=[vm] * n, out_specs=[vm] * n,
        scratch_shapes=[pltpu.VMEM((N_DEV,) + v.shape, v.dtype) for v in vals]
        + [pltpu.SemaphoreType.DMA((n, 7)), pltpu.SemaphoreType.DMA((n, 7))],
    )(*vals)


def _adamw_math(w, g, m, v):
    m = ADAM_B1 * m + (1.0 - ADAM_B1) * g
    v = ADAM_B2 * v + (1.0 - ADAM_B2) * (g * g)
    m_hat = m / (1.0 - ADAM_B1 ** ADAM_STEP)
    v_hat = v / (1.0 - ADAM_B2 ** ADAM_STEP)
    delta = -ADAM_LR * (m_hat / (jnp.sqrt(v_hat) + ADAM_EPS) + ADAM_WD * w)
    return delta, m, v


def _adamw_shard(parts, w, m, v, tr, name):
    R, Cc = w.shape

    def body(p_ref, w_ref, m_ref, v_ref, g_out, d_out, m_out, v_out):
        g = p_ref[0].astype(F32)
        for j in range(1, N_DEV):
            g = g + p_ref[j].astype(F32)
        d, mn, vn = _adamw_math(w_ref[...], g, m_ref[...], v_ref[...])
        g_out[...] = g
        d_out[...] = d
        m_out[...] = mn
        v_out[...] = vn

    blk = pl.BlockSpec((tr, Cc), lambda i: (i, 0))
    return pl.pallas_call(
        body, name=name, grid=(R // tr,),
        in_specs=[pl.BlockSpec((N_DEV, tr, Cc), lambda i: (0, i, 0)), blk, blk, blk],
        out_specs=[blk] * 4, out_shape=[jax.ShapeDtypeStruct((R, Cc), F32)] * 4,
        compiler_params=pltpu.CompilerParams(dimension_semantics=("parallel",)),
    )(parts, w, m, v)


def _adamw_small(grads, ws, ms, vs):
    n = len(grads)
    vm = pl.BlockSpec(memory_space=pltpu.VMEM)

    def body(*refs):
        g_in, w_in, m_in, v_in = (refs[k * n:(k + 1) * n] for k in range(4))
        g_out, d_out, m_out, v_out = (refs[(4 + k) * n:(5 + k) * n] for k in range(4))
        for k in range(n):
            g = g_in[k][...]
            d, mn, vn = _adamw_math(w_in[k][...], g, m_in[k][...], v_in[k][...])
            g_out[k][...] = g
            d_out[k][...] = d
            m_out[k][...] = mn
            v_out[k][...] = vn

    shapes = [jax.ShapeDtypeStruct(g.shape, F32) for g in grads]
    return pl.pallas_call(
        body, name="adamw_small", out_shape=shapes * 4, in_specs=[vm] * (4 * n), out_specs=[vm] * (4 * n),
    )(*grads, *ws, *ms, *vs)


def _unshard_cols(g):
    n, R, Cc = g.shape
    return jnp.transpose(g, (1, 0, 2)).reshape(R, n * Cc)


def _shard_cols(full):
    R, Ct = full.shape
    return jnp.transpose(full.reshape(R, N_DEV, Ct // N_DEV), (1, 0, 2))


_COL_SHARDED = ("w_in", "w_gate", "w_up")
_BIG = ("w_in", "w_out", "w_gate", "w_up", "w_down")


def _rows(nm, p):
    return p[0].T if nm in _COL_SHARDED else p[0]


def _step(args, ts_mix_fwd, ts_ffn, ts_mix_bwd, ts_grad):
    (x, g_mix, w_in, b_in, w_dw, b_dw, ln_g, ln_b, w_pool, s_pool, w_out, g_ffn, w_gate, w_up, w_down, g_final,
     loss_target) = args[:17]
    names = ["g_mix", "w_in", "b_in", "w_dw", "b_dw", "ln_g", "ln_b", "w_pool", "s_pool", "w_out", "g_ffn",
             "w_gate", "w_up", "w_down", "g_final"]
    weights = dict(zip(names, args[1:16]))
    moms = dict(zip(names, args[17:32]))
    vars_ = dict(zip(names, args[32:47]))

    S, D = x.shape[1], x.shape[2]
    x2 = x.reshape(S, D)
    tgt2 = loss_target.reshape(S, D)

    shards = [_rows(nm, weights[nm]).astype(BF16) for nm in _BIG] + [w_dw[0]]
    gathered = _all_gather(shards)
    wt_in, w_out_f, wt_gate, wt_up, w_down_f = [g.reshape(-1, D) for g in gathered[:5]]
    w_dw_f = _unshard_cols(gathered[5])
    w_pool_b = w_pool[0].astype(BF16)
    Fd = wt_gate.shape[0]

    a, gate, v, m, y, h1, xn = _mix_fwd(x2, g_mix, wt_in, b_in, w_dw_f, b_dw, ln_g, ln_b, w_pool_b, s_pool, w_out_f,
                                       ts_mix_fwd)
    f_chunks = [1024] * (Fd // 1024) + ([Fd % 1024] if Fd % 1024 else [])
    dh1, hn, act, dgt, dup, dh2, loss_p, dg_final, dg_ffn = _ffn(
        h1, tgt2, g_ffn, g_final.reshape(1, D), wt_gate, wt_up, w_down_f, ts_ffn, f_chunks)
    dx, dz, dg_mix, db_in, dw_dw, db_dw, dln_g, dln_b, dw_pool, ds_pool = _mix_bwd(
        dh1, x2, a, gate, v, m, g_mix, wt_in, w_dw_f, ln_g, ln_b, w_pool_b, s_pool, w_out_f, ts_mix_bwd)

    d_in = wt_in.shape[0]
    grads = [_grad_matmul(dz, xn, d_in // 2, D, ts_grad, "grad_w_in"),
             _grad_matmul(y, dh1, D, D, ts_grad, "grad_w_out"),
             _grad_matmul(dgt, hn, Fd // 2, D, ts_grad, "grad_w_gate"),
             _grad_matmul(dup, hn, Fd // 2, D, ts_grad, "grad_w_up"),
             _grad_matmul(act, dh2, Fd // 2, D, ts_grad, "grad_w_down")]

    parts = [g.reshape(N_DEV, -1, D) for g in grads] + [_shard_cols(dw_dw[0:CONV_WIDTH])]
    recv = _reduce_scatter_exchange(parts)
    big = {}
    for nm, r in zip(_BIG, recv[:5]):
        rows = r.shape[1]
        res = _adamw_shard(r, _rows(nm, weights[nm]), _rows(nm, moms[nm]), _rows(nm, vars_[nm]),
                           rows // 2 if rows % 32 == 0 else rows, "adamw_" + nm)
        big[nm] = [o.T if nm in _COL_SHARDED else o for o in res]
    big["w_dw"] = _adamw_shard(recv[5], w_dw[0], moms["w_dw"][0], vars_["w_dw"][0], CONV_WIDTH, "adamw_w_dw")

    small_names = ["g_mix", "b_in", "b_dw", "ln_g", "ln_b", "w_pool", "s_pool", "g_ffn", "g_final"]
    small_shape = lambda p: p.reshape(-1, p.shape[-1])
    partial = [dg_mix, db_in, db_dw, dln_g, dln_b, dw_pool.reshape(-1, POOL_GROUP), ds_pool, dg_ffn, dg_final, loss_p]
    summed = _all_reduce_small(partial)
    sm = _adamw_small(summed[:-1], [small_shape(weights[nm]) for nm in small_names],
                      [small_shape(moms[nm]) for nm in small_names], [small_shape(vars_[nm]) for nm in small_names])
    n_small = len(small_names)

    def result(kind, nm):
        if nm in big:
            return big[nm][kind].reshape(weights[nm].shape)
        return sm[kind * n_small + small_names.index(nm)].reshape(weights[nm].shape)

    loss = summed[-1][0, 0]
    out = [loss, dx.reshape(x.shape)]
    for kind in range(4):
        out += [result(kind, nm) for nm in names]
    return tuple(out)


def kernel(x, g_mix, w_in, b_in, w_dw, b_dw, ln_g, ln_b, w_pool, s_pool, w_out, g_ffn, w_gate, w_up, w_down, g_final, loss_target, m_g_mix, m_w_in, m_b_in, m_w_dw, m_b_dw, m_ln_g, m_ln_b, m_w_pool, m_s_pool, m_w_out, m_g_ffn, m_w_gate, m_w_up, m_w_down, m_g_final, v_g_mix, v_w_in, v_b_in, v_w_dw, v_b_dw, v_ln_g, v_ln_b, v_w_pool, v_s_pool, v_w_out, v_g_ffn, v_w_gate, v_w_up, v_w_down, v_g_final):
    args = (x, g_mix, w_in, b_in, w_dw, b_dw, ln_g, ln_b, w_pool, s_pool, w_out, g_ffn, w_gate, w_up, w_down, g_final, loss_target, m_g_mix, m_w_in, m_b_in, m_w_dw, m_b_dw, m_ln_g, m_ln_b, m_w_pool, m_s_pool, m_w_out, m_g_ffn, m_w_gate, m_w_up, m_w_down, m_g_final, v_g_mix, v_w_in, v_b_in, v_w_dw, v_b_dw, v_ln_g, v_ln_b, v_w_pool, v_s_pool, v_w_out, v_g_ffn, v_w_gate, v_w_up, v_w_down, v_g_final)
    return _step(args, ts_mix_fwd=512, ts_ffn=256, ts_mix_bwd=256, ts_grad=512)
```

```python
import functools

import jax
import jax.numpy as jnp
from jax import lax
from jax.experimental import pallas as pl
from jax.experimental.pallas import tpu as pltpu

F32 = jnp.float32
BF16 = jnp.bfloat16
MESH = pl.DeviceIdType.MESH
N_DEV = 8

C_CONV = 512
CONV_WIDTH = 31
POOL_WINDOWS = (2, 4, 8, 16)
POOL_GROUP = 128
RMS_EPS = 1e-6
LN_EPS = 1e-5

ADAM_LR = 0.001
ADAM_B1 = 0.9
ADAM_B2 = 0.999
ADAM_EPS = 1e-08
ADAM_WD = 0.01
ADAM_STEP = 10

HALO = 32
SUBLANES = 8
CONV_ROWS = 64
VMEM_LIMIT = 56 * 1024 * 1024


def _dot(a, b):
    return jnp.dot(a, b, preferred_element_type=F32)


def _dot_nt(a, b):
    return lax.dot_general(a, b, (((1,), (1,)), ((), ())), preferred_element_type=F32)


def _dot_tn(a, b):
    return lax.dot_general(a, b, (((0,), (0,)), ((), ())), preferred_element_type=F32)


def _mean_last(v):
    return jnp.mean(v, axis=-1, keepdims=True)


def _full(shape):
    nd = len(shape)
    return pl.BlockSpec(shape, lambda *_: (0,) * nd)


def _full1(shape):
    nd = len(shape)
    return pl.BlockSpec(shape, lambda *_: (0,) * nd, pipeline_mode=pl.Buffered(1))


def _shifted_copies(sh_ref, rows):
    for s in range(1, SUBLANES):
        sh_ref[s, 0:rows, :] = sh_ref[0, s:s + rows, :]


def _tap(sh_ref, off, r0, rows):
    q, s = divmod(off, SUBLANES)
    return sh_ref[s, pl.ds(r0 + q * SUBLANES, rows), :]


def _mix_fwd(x, g_mix, w_in, b_in, w_dw, b_dw, ln_g, ln_b, w_pool, s_pool, w_out, ag_shards, ts):
    S, D = x.shape
    d_in = w_in.shape[0]
    C = C_CONV
    nt = S // ts
    nrb = ts // CONV_ROWS
    n_ag = len(ag_shards)

    def body(x_ref, g_ref, win_ref, bin_ref, wdw_ref, bdw_ref, lng_ref, lnb_ref, wp_ref, sp_ref, wout_ref, *rest):
        ag_in, rest = rest[:n_ag], rest[n_ag:]
        a_ref, gate_ref, v_ref, m_ref, y_ref, h1_ref = rest[:6]
        ag_out, rest = rest[6:6 + n_ag], rest[6 + n_ag:]
        ush, pbuf, send_sems, recv_sems, local_sems = rest
        i = pl.program_id(0)
        ag_start, ag_finish = _gather_plan(ag_in, ag_out, send_sems, recv_sems, local_sems)

        @pl.when(i == 0)
        def _():
            ag_start()
            ush[0, 0:HALO, :] = jnp.zeros((HALO, C), F32)
            pbuf[0:HALO, :] = jnp.zeros((HALO, C), F32)

        x = x_ref[...]
        r1 = lax.rsqrt(_mean_last(x * x) + RMS_EPS)
        xn = (x * r1 * g_ref[...]).astype(BF16)
        z = _dot_nt(xn, win_ref[...]) + bin_ref[...]
        a = z[:, 0:C]
        gate = z[:, C:2 * C]
        a_ref[...] = a
        gate_ref[...] = gate
        ush[0, HALO:HALO + ts, :] = a * jax.nn.sigmoid(gate)
        pbuf[HALO:HALO + ts, :] = z[:, 2 * C:]

        _shifted_copies(ush, ts + HALO - SUBLANES)

        def conv_block(rb, carry):
            r0 = pl.multiple_of(rb * CONV_ROWS, CONV_ROWS)
            acc = jnp.zeros((CONV_ROWS, C), F32)
            for k in range(CONV_WIDTH):
                acc = acc + wdw_ref[k:k + 1, :] * _tap(ush, HALO - (CONV_WIDTH - 1) + k, r0, CONV_ROWS)
            v_ref[pl.ds(r0, CONV_ROWS), :] = acc + bdw_ref[...]
            return carry

        lax.fori_loop(0, nrb, conv_block, 0)

        v = v_ref[...]
        mu = _mean_last(v)
        xc = v - mu
        rstd = lax.rsqrt(_mean_last(xc * xc) + LN_EPS)
        ln = xc * rstd * lng_ref[...] + lnb_ref[...]
        y_ref[:, 0:C] = (ln * jax.nn.sigmoid(ln)).astype(BF16)

        row = lax.broadcasted_iota(jnp.int32, (ts, 1), 0) + i * ts
        for gi, w in enumerate(POOL_WINDOWS):
            lanes = slice(gi * POOL_GROUP, (gi + 1) * POOL_GROUP)
            seg = pbuf[HALO:HALO + ts, lanes]
            ws = seg
            for k in range(1, w):
                ws = ws + pbuf[HALO - k:HALO - k + ts, lanes]
            cnt = jnp.minimum(row + 1, w).astype(F32)
            m = (ws / cnt - seg).astype(BF16)
            m_ref[:, lanes] = m
            ypre = _dot(m, wp_ref[gi])
            y_ref[:, C + gi * POOL_GROUP:C + (gi + 1) * POOL_GROUP] = (ypre * sp_ref[:, lanes]).astype(BF16)

        h1_ref[...] = x + _dot(y_ref[...], wout_ref[...])

        ush[0, 0:HALO, :] = ush[0, ts:ts + HALO, :]
        pbuf[0:HALO, :] = pbuf[ts:ts + HALO, :]

        @pl.when(i == nt - 1)
        def _():
            ag_finish()

    tile = lambda w, dt: (pl.BlockSpec((ts, w), lambda i: (i, 0)), jax.ShapeDtypeStruct((S, w), dt))
    outs = [tile(C, F32), tile(C, F32), tile(C, F32), tile(C, BF16), tile(D, BF16), tile(D, F32)]
    return pl.pallas_call(
        body, name="mix_fwd", grid=(nt,),
        in_specs=[pl.BlockSpec((ts, D), lambda i: (i, 0)), _full((1, D)), _full((d_in, D)), _full((1, d_in)),
                  _full(w_dw.shape), _full((1, C)), _full((1, C)), _full((1, C)), _full(w_pool.shape),
                  _full((1, C)), _full((D, D))] + [_HBM] * n_ag,
        out_specs=[o[0] for o in outs] + [_HBM] * n_ag,
        out_shape=[o[1] for o in outs] + _gather_out_shapes(ag_shards),
        scratch_shapes=[pltpu.VMEM((SUBLANES, ts + HALO, C), F32), pltpu.VMEM((ts + HALO, C), F32)]
        + _comm_sems(n_ag),
        compiler_params=pltpu.CompilerParams(dimension_semantics=("arbitrary",), vmem_limit_bytes=VMEM_LIMIT),
    )(x, g_mix, w_in, b_in, w_dw, b_dw, ln_g, ln_b, w_pool, s_pool, w_out, *ag_shards)


def _ffn(h1, target, g_ffn, g_final, w_gate, w_up, w_down, ts, f_chunks):
    S, D = h1.shape
    Fd = w_gate.shape[0]
    nt = S // ts
    bounds = []
    lo = 0
    for n in f_chunks:
        bounds.append((lo, lo + n))
        lo += n
    assert lo == Fd

    def body(h1_ref, tgt_ref, gf_ref, gl_ref, wg_ref, wu_ref, wd_ref,
             dh1_ref, hn_ref, act_ref, dgt_ref, dup_ref, dh2_ref, loss_ref, dgl_ref, dgf_ref, gt_s, up_s):
        i = pl.program_id(0)

        @pl.when(i == 0)
        def _():
            loss_ref[...] = jnp.zeros_like(loss_ref)
            dgl_ref[...] = jnp.zeros_like(dgl_ref)
            dgf_ref[...] = jnp.zeros_like(dgf_ref)

        h1 = h1_ref[...]
        r2 = lax.rsqrt(_mean_last(h1 * h1) + RMS_EPS)
        hhat = h1 * r2
        hn = (hhat * gf_ref[...]).astype(BF16)
        hn_ref[...] = hn
        h2 = h1
        for lo, hi in bounds:
            gt = _dot_nt(hn, wg_ref[lo:hi, :])
            up = _dot_nt(hn, wu_ref[lo:hi, :])
            gt_s[:, lo:hi] = gt
            up_s[:, lo:hi] = up
            act = (gt * jax.nn.sigmoid(gt) * up).astype(BF16)
            act_ref[:, lo:hi] = act
            h2 = h2 + _dot(act, wd_ref[lo:hi, :])

        r3 = lax.rsqrt(_mean_last(h2 * h2) + RMS_EPS)
        n3 = h2 * r3
        gl = gl_ref[...]
        diff = n3 * gl - tgt_ref[...]
        loss_ref[...] += jnp.sum(0.5 * jnp.sum(diff * diff, axis=-1, keepdims=True) / D, axis=0, keepdims=True)
        dout = diff / D
        dgl_ref[...] += jnp.sum(dout * n3, axis=0, keepdims=True)
        dn = dout * gl
        dh2 = r3 * (dn - n3 * _mean_last(dn * n3))
        dh2b = dh2.astype(BF16)
        dh2_ref[...] = dh2b

        dhn = jnp.zeros((ts, D), F32)
        for lo, hi in bounds:
            gt = gt_s[:, lo:hi]
            up = up_s[:, lo:hi]
            sg = jax.nn.sigmoid(gt)
            dact = _dot_nt(dh2b, wd_ref[lo:hi, :])
            dgt = (dact * up * (sg * (1.0 + gt * (1.0 - sg)))).astype(BF16)
            dup = (dact * (gt * sg)).astype(BF16)
            dgt_ref[:, lo:hi] = dgt
            dup_ref[:, lo:hi] = dup
            dhn = dhn + _dot(dgt, wg_ref[lo:hi, :]) + _dot(dup, wu_ref[lo:hi, :])

        dgf_ref[...] += jnp.sum(dhn * hhat, axis=0, keepdims=True)
        dnn = dhn * gf_ref[...]
        dh1_ref[...] = dh2 + r2 * (dnn - hhat * _mean_last(dnn * hhat))

    tile = lambda w, dt: (pl.BlockSpec((ts, w), lambda i: (i, 0)), jax.ShapeDtypeStruct((S, w), dt))
    acc = lambda w: (_full((1, w)), jax.ShapeDtypeStruct((1, w), F32))
    outs = [tile(D, F32), tile(D, BF16), tile(Fd, BF16), tile(Fd, BF16), tile(Fd, BF16), tile(D, BF16),
            acc(128), acc(D), acc(D)]
    return pl.pallas_call(
        body, name="ffn_fwd_bwd", grid=(nt,),
        in_specs=[pl.BlockSpec((ts, D), lambda i: (i, 0)), pl.BlockSpec((ts, D), lambda i: (i, 0)),
                  _full((1, D)), _full((1, D)), _full1((Fd, D)), _full1((Fd, D)), _full1((Fd, D))],
        out_specs=[o[0] for o in outs], out_shape=[o[1] for o in outs],
        scratch_shapes=[pltpu.VMEM((ts, Fd), F32), pltpu.VMEM((ts, Fd), F32)],
        compiler_params=pltpu.CompilerParams(dimension_semantics=("arbitrary",), vmem_limit_bytes=VMEM_LIMIT),
    )(h1, target, g_ffn, g_final, w_gate, w_up, w_down)


def _mix_bwd(dh1, x, a, gate, v, m, y, g_mix, w_in, w_dw, ln_g, ln_b, w_pool, s_pool, w_out, rs_parts, ts):
    S, D = x.shape
    n_rs = len(rs_parts)
    d_in = w_in.shape[0]
    C = C_CONV
    nt = S // ts
    nrb = ts // CONV_ROWS
    hb = ts // HALO
    wrows = ((CONV_WIDTH + SUBLANES - 1) // SUBLANES) * SUBLANES

    def body(dh1_ref, x_ref, a_ref, gate_ref, ah_ref, gh_ref, v_ref, m_ref, y_ref, g_ref, win_ref, wdw_ref, lng_ref,
             lnb_ref, wp_ref, sp_ref, wout_ref, *rest):
        rs_in, rest = rest[:n_rs], rest[n_rs:]
        (dx_ref, dgm_ref, dbin_ref, dwdw_ref, dbdw_ref, dlng_ref, dlnb_ref, dwp_ref, dsp_ref, dwin_ref,
         dwout_ref) = rest[:11]
        rs_out, rest = rest[11:11 + n_rs], rest[11 + n_rs:]
        ush, dvsh, dqbuf, du_s, dz_ref, acc_in, acc_out, send_sems, recv_sems, local_sems = rest
        i = pl.program_id(0)
        t = nt - 1 - i
        rs_start, rs_finish = _exchange_plan(rs_in, rs_out, send_sems, recv_sems, local_sems)

        @pl.when(i == 0)
        def _():
            rs_start()
            dvsh[0, ts:ts + HALO, :] = jnp.zeros((HALO, C), F32)
            dqbuf[ts:ts + HALO, :] = jnp.zeros((HALO, C), F32)
            for r in (dgm_ref, dbin_ref, dwdw_ref, dbdw_ref, dlng_ref, dlnb_ref, dwp_ref, dsp_ref, acc_in, acc_out):
                r[...] = jnp.zeros_like(r)

        dh1 = dh1_ref[...]
        dh1b = dh1.astype(BF16)
        dy = _dot_nt(dh1b, wout_ref[...])
        acc_out[...] += _dot_tn(y_ref[...], dh1b)

        v = v_ref[...]
        mu = _mean_last(v)
        xc = v - mu
        rstd = lax.rsqrt(_mean_last(xc * xc) + LN_EPS)
        vhat = xc * rstd
        lng = lng_ref[...]
        ln = vhat * lng + lnb_ref[...]
        sg = jax.nn.sigmoid(ln)
        dln = dy[:, 0:C] * (sg * (1.0 + ln * (1.0 - sg)))
        dlng_ref[...] += jnp.sum(dln * vhat, axis=0, keepdims=True)
        dlnb_ref[...] += jnp.sum(dln, axis=0, keepdims=True)
        dvh = dln * lng
        dv = rstd * (dvh - _mean_last(dvh) - vhat * _mean_last(dvh * vhat))
        dbdw_ref[...] += jnp.sum(dv, axis=0, keepdims=True)
        dvsh[0, 0:ts, :] = dv
        _shifted_copies(dvsh, ts + HALO - SUBLANES)

        a = a_ref[...]
        sgate = jax.nn.sigmoid(gate_ref[...])
        ush[0, HALO:HALO + ts, :] = a * sgate
        uh = ah_ref[...] * jax.nn.sigmoid(gh_ref[...])
        ush[0, 0:HALO, :] = jnp.where(t > 0, uh, 0.0)
        _shifted_copies(ush, ts + HALO - SUBLANES)

        def conv_block(rb, carry):
            r0 = pl.multiple_of(rb * CONV_ROWS, CONV_ROWS)
            acc = jnp.zeros((CONV_ROWS, C), F32)
            for k in range(CONV_WIDTH):
                acc = acc + wdw_ref[k:k + 1, :] * _tap(dvsh, CONV_WIDTH - 1 - k, r0, CONV_ROWS)
            du_s[pl.ds(r0, CONV_ROWS), :] = acc
            return carry

        lax.fori_loop(0, nrb, conv_block, 0)

        for k in range(CONV_WIDTH):
            q, s = divmod(HALO - (CONV_WIDTH - 1) + k, SUBLANES)
            prod = dvsh[0, 0:ts, :] * ush[s, q * SUBLANES:q * SUBLANES + ts, :]
            dwdw_ref[k:k + 1, :] += jnp.sum(prod, axis=0, keepdims=True)

        du = du_s[...]
        da = du * sgate
        dgate = du * a * sgate * (1.0 - sgate)
        dz_ref[:, 0:C] = da.astype(BF16)
        dz_ref[:, C:2 * C] = dgate.astype(BF16)
        dbin_ref[:, 0:C] += jnp.sum(da, axis=0, keepdims=True)
        dbin_ref[:, C:2 * C] += jnp.sum(dgate, axis=0, keepdims=True)

        row = lax.broadcasted_iota(jnp.int32, (ts, 1), 0) + t * ts
        for gi, w in enumerate(POOL_WINDOWS):
            lanes = slice(gi * POOL_GROUP, (gi + 1) * POOL_GROUP)
            dyp = dy[:, C + gi * POOL_GROUP:C + (gi + 1) * POOL_GROUP]
            mg = m_ref[:, lanes]
            ypre = _dot(mg, wp_ref[gi])
            dsp_ref[:, lanes] += jnp.sum(dyp * ypre, axis=0, keepdims=True)
            dyi = (dyp * sp_ref[:, lanes]).astype(BF16)
            dwp_ref[gi] += _dot_tn(mg, dyi)
            dm = _dot_nt(dyi, wp_ref[gi])
            cnt = jnp.minimum(row + 1, w).astype(F32)
            dqbuf[0:ts, lanes] = dm / cnt
            dp = -dm
            for k in range(w):
                dp = dp + dqbuf[k:k + ts, lanes]
            dz_ref[:, 2 * C + gi * POOL_GROUP:2 * C + (gi + 1) * POOL_GROUP] = dp.astype(BF16)
            dbin_ref[:, 2 * C + gi * POOL_GROUP:2 * C + (gi + 1) * POOL_GROUP] += jnp.sum(dp, axis=0, keepdims=True)

        dxn = _dot(dz_ref[...], win_ref[...])
        x = x_ref[...]
        r1 = lax.rsqrt(_mean_last(x * x) + RMS_EPS)
        xhat = x * r1
        dgm_ref[...] += jnp.sum(dxn * xhat, axis=0, keepdims=True)
        g = g_ref[...]
        dnn = dxn * g
        dx_ref[...] = dh1 + r1 * (dnn - xhat * _mean_last(dnn * xhat))
        acc_in[...] += _dot_tn(dz_ref[...], (xhat * g).astype(BF16))

        dvsh[0, ts:ts + HALO, :] = dvsh[0, 0:HALO, :]
        dqbuf[ts:ts + HALO, :] = dqbuf[0:HALO, :]

        @pl.when(i == nt - 1)
        def _():
            dwin_ref[...] = acc_in[...].astype(BF16)
            dwout_ref[...] = acc_out[...].astype(BF16)
            rs_finish()

    rev = lambda w: pl.BlockSpec((ts, w), lambda i: (nt - 1 - i, 0))
    halo = pl.BlockSpec((HALO, C), lambda i: (jnp.maximum((nt - 1 - i) * hb - 1, 0), 0))
    acc = lambda shape, dt=F32: (_full(shape), jax.ShapeDtypeStruct(shape, dt))
    outs = [(rev(D), jax.ShapeDtypeStruct((S, D), F32)), acc((1, D)), acc((1, d_in)), acc((wrows, C)), acc((1, C)), acc((1, C)), acc((1, C)),
            acc(w_pool.shape), acc((1, C)), acc((d_in, D), BF16), acc((D, D), BF16)]
    return pl.pallas_call(
        body, name="mix_bwd", grid=(nt,),
        in_specs=[rev(D), rev(D), rev(C), rev(C), halo, halo, rev(C), rev(C), rev(D), _full((1, D)),
                  _full1((d_in, D)), _full(w_dw.shape), _full((1, C)), _full((1, C)), _full(w_pool.shape),
                  _full((1, C)), _full1((D, D))] + [_HBM] * n_rs,
        out_specs=[o[0] for o in outs] + [_HBM] * n_rs,
        out_shape=[o[1] for o in outs] + [jax.ShapeDtypeStruct(p.shape, p.dtype) for p in rs_parts],
        scratch_shapes=[pltpu.VMEM((SUBLANES, ts + HALO, C), F32), pltpu.VMEM((SUBLANES, ts + HALO, C), F32),
                        pltpu.VMEM((ts + HALO, C), F32), pltpu.VMEM((ts, C), F32), pltpu.VMEM((ts, d_in), BF16),
                        pltpu.VMEM((d_in, D), F32), pltpu.VMEM((D, D), F32)] + _comm_sems(n_rs),
        compiler_params=pltpu.CompilerParams(dimension_semantics=("arbitrary",), vmem_limit_bytes=VMEM_LIMIT),
    )(dh1, x, a, gate, a, gate, v, m, y, g_mix, w_in, w_dw, ln_g, ln_b, w_pool, s_pool, w_out, *rs_parts)


def _grad_matmul(a, b, tm, tn, ts, name):
    S, M = a.shape
    N = b.shape[1]
    nk = S // ts

    def body(a_ref, b_ref, o_ref, acc):
        k = pl.program_id(2)

        @pl.when(k == 0)
        def _():
            acc[...] = jnp.zeros_like(acc)

        acc[...] += _dot_tn(a_ref[...].astype(BF16), b_ref[...].astype(BF16))

        @pl.when(k == nk - 1)
        def _():
            o_ref[...] = acc[...].astype(BF16)

    return pl.pallas_call(
        body, name=name, grid=(M // tm, N // tn, nk),
        in_specs=[pl.BlockSpec((ts, tm), lambda i, j, k: (k, i)), pl.BlockSpec((ts, tn), lambda i, j, k: (k, j))],
        out_specs=pl.BlockSpec((tm, tn), lambda i, j, k: (i, j)),
        out_shape=jax.ShapeDtypeStruct((M, N), BF16),
        scratch_shapes=[pltpu.VMEM((tm, tn), F32)],
        compiler_params=pltpu.CompilerParams(dimension_semantics=("parallel", "parallel", "arbitrary"),
                                             vmem_limit_bytes=VMEM_LIMIT),
    )(a, b)


def _position():
    return lax.axis_index("x"), lax.axis_index("y"), lax.axis_index("c")


def _slot(px, py, pc):
    return 4 * px + 2 * py + pc


def _all_gather(shards):
    n = len(shards)

    def body(*refs):
        start, finish = _gather_plan(refs[:n], refs[n:2 * n], *refs[2 * n:])
        start()
        finish()

    return pl.pallas_call(
        body, name="all_gather_mix_weights",
        out_shape=_gather_out_shapes(shards), in_specs=[_HBM] * n, out_specs=[_HBM] * n,
        scratch_shapes=_comm_sems(n),
    )(*shards)


_HBM = pl.BlockSpec(memory_space=pl.ANY)
_FLIPS = [(fx, fy, fc) for fx in (0, 1) for fy in (0, 1) for fc in (0, 1)][1:]


def _comm_sems(n):
    return [pltpu.SemaphoreType.DMA((n, 7)), pltpu.SemaphoreType.DMA((n, 7)), pltpu.SemaphoreType.DMA((n,))]


def _gather_out_shapes(shards):
    return [jax.ShapeDtypeStruct((N_DEV,) + s.shape, s.dtype) for s in shards]


def _gather_plan(ins, outs, send_sems, recv_sems, local_sems):
    n = len(ins)
    x, y, c = _position()
    me, sibling = (x, y, c), (x, y, 1 - c)
    chips = [(1 - x, y), (x, 1 - y), (1 - x, 1 - y)]

    def copy(k, sem, block, to, src=None):
        dst = outs[k].at[_slot(*block)]
        return pltpu.make_async_remote_copy(
            src_ref=dst if src is None else src, dst_ref=dst, send_sem=send_sems.at[k, sem],
            recv_sem=recv_sems.at[k, sem], device_id=to, device_id_type=MESH)

    def mine():
        return [pltpu.make_async_copy(ins[k], outs[k].at[_slot(*me)], local_sems.at[k]) for k in range(n)]

    def first():
        cps = []
        for k in range(n):
            cps.append(copy(k, 0, me, sibling, src=ins[k]))
            cps += [copy(k, 1 + j, me, (*chip, c), src=ins[k]) for j, chip in enumerate(chips)]
        return cps

    def start():
        for cp in mine() + first():
            cp.start()

    def finish():
        passed = []
        for j, chip in enumerate(chips):
            for k in range(n):
                copy(k, 1 + j, (*chip, c), me).wait_recv()
                fwd = copy(k, 4 + j, (*chip, c), sibling)
                fwd.start()
                passed.append(fwd)
        for k in range(n):
            copy(k, 0, sibling, me).wait_recv()
            for j, chip in enumerate(chips):
                copy(k, 4 + j, (*chip, 1 - c), me).wait_recv()
        for cp in first() + passed:
            cp.wait_send()
        for cp in mine():
            cp.wait()

    return start, finish


def _exchange_plan(ins, outs, send_sems, recv_sems, local_sems):
    n = len(ins)
    x, y, c = _position()
    me = _slot(x, y, c)
    peers = [(x ^ fx, y ^ fy, c ^ fc) for fx, fy, fc in _FLIPS]

    def mine():
        return [pltpu.make_async_copy(ins[k].at[me], outs[k].at[me], local_sems.at[k]) for k in range(n)]

    def sends():
        return [pltpu.make_async_remote_copy(
            src_ref=ins[k].at[_slot(*peer)], dst_ref=outs[k].at[me], send_sem=send_sems.at[k, f],
            recv_sem=recv_sems.at[k, f], device_id=peer, device_id_type=MESH)
            for f, peer in enumerate(peers) for k in range(n)]

    def start():
        for cp in mine() + sends():
            cp.start()

    def finish():
        for f, peer in enumerate(peers):
            for k in range(n):
                pltpu.make_async_remote_copy(
                    src_ref=ins[k].at[me], dst_ref=outs[k].at[_slot(*peer)], send_sem=send_sems.at[k, f],
                    recv_sem=recv_sems.at[k, f], device_id=peer, device_id_type=MESH).wait_recv()
        for cp in sends():
            cp.wait_send()
        for cp in mine():
            cp.wait()

    return start, finish


def _tail_exchange(parts, vals):
    n, ns = len(parts), len(vals)
    vm = pl.BlockSpec(memory_space=pltpu.VMEM)

    def body(*refs):
        p_in, v_in = refs[:n], refs[n:n + ns]
        p_out, v_out = refs[n + ns:2 * n + ns], refs[2 * n + ns:2 * (n + ns)]
        bufs = refs[2 * (n + ns):2 * n + 3 * ns]
        p_send, p_recv, p_local, v_send, v_recv = refs[2 * n + 3 * ns:]
        start, finish = _exchange_plan(p_in, p_out, p_send, p_recv, p_local)
        start()
        x, y, c = _position()
        me = _slot(x, y, c)
        peers = [(x ^ fx, y ^ fy, c ^ fc) for fx, fy, fc in _FLIPS]

        def small(k, f, slot):
            return pltpu.make_async_remote_copy(
                src_ref=v_in[k], dst_ref=bufs[k].at[slot], send_sem=v_send.at[k, f], recv_sem=v_recv.at[k, f],
                device_id=peers[f], device_id_type=MESH)

        for f in range(7):
            for k in range(ns):
                small(k, f, me).start()
        for k in range(ns):
            bufs[k][me] = v_in[k][...]
        for f, peer in enumerate(peers):
            for k in range(ns):
                small(k, f, _slot(*peer)).wait_recv()
        for k in range(ns):
            total = bufs[k][0]
            for j in range(1, N_DEV):
                total = total + bufs[k][j]
            v_out[k][...] = total
        for f in range(7):
            for k in range(ns):
                small(k, f, me).wait_send()
        finish()

    return pl.pallas_call(
        body, name="tail_exchange",
        out_shape=[jax.ShapeDtypeStruct(p.shape, p.dtype) for p in parts]
        + [jax.ShapeDtypeStruct(v.shape, v.dtype) for v in vals],
        in_specs=[_HBM] * n + [vm] * ns, out_specs=[_HBM] * n + [vm] * ns,
        scratch_shapes=[pltpu.VMEM((N_DEV,) + v.shape, v.dtype) for v in vals] + _comm_sems(n)
        + [pltpu.SemaphoreType.DMA((ns, 7)), pltpu.SemaphoreType.DMA((ns, 7))],
    )(*parts, *vals)


def _adamw_math(w, g, m, v):
    m = ADAM_B1 * m + (1.0 - ADAM_B1) * g
    v = ADAM_B2 * v + (1.0 - ADAM_B2) * (g * g)
    m_hat = m / (1.0 - ADAM_B1 ** ADAM_STEP)
    v_hat = v / (1.0 - ADAM_B2 ** ADAM_STEP)
    delta = -ADAM_LR * (m_hat / (jnp.sqrt(v_hat) + ADAM_EPS) + ADAM_WD * w)
    return delta, m, v


def _adamw_shard(parts, w, m, v, tr, name):
    R, Cc = w.shape

    def body(p_ref, w_ref, m_ref, v_ref, g_out, d_out, m_out, v_out):
        g = p_ref[0].astype(F32)
        for j in range(1, N_DEV):
            g = g + p_ref[j].astype(F32)
        d, mn, vn = _adamw_math(w_ref[...], g, m_ref[...], v_ref[...])
        g_out[...] = g
        d_out[...] = d
        m_out[...] = mn
        v_out[...] = vn

    blk = pl.BlockSpec((tr, Cc), lambda i: (i, 0))
    return pl.pallas_call(
        body, name=name, grid=(R // tr,),
        in_specs=[pl.BlockSpec((N_DEV, tr, Cc), lambda i: (0, i, 0)), blk, blk, blk],
        out_specs=[blk] * 4, out_shape=[jax.ShapeDtypeStruct((R, Cc), F32)] * 4,
        compiler_params=pltpu.CompilerParams(dimension_semantics=("parallel",)),
    )(parts, w, m, v)


def _adamw_small(grads, ws, ms, vs):
    n = len(grads)
    vm = pl.BlockSpec(memory_space=pltpu.VMEM)

    def body(*refs):
        g_in, w_in, m_in, v_in = (refs[k * n:(k + 1) * n] for k in range(4))
        g_out, d_out, m_out, v_out = (refs[(4 + k) * n:(5 + k) * n] for k in range(4))
        for k in range(n):
            g = g_in[k][...]
            d, mn, vn = _adamw_math(w_in[k][...], g, m_in[k][...], v_in[k][...])
            g_out[k][...] = g
            d_out[k][...] = d
            m_out[k][...] = mn
            v_out[k][...] = vn

    shapes = [jax.ShapeDtypeStruct(g.shape, F32) for g in grads]
    return pl.pallas_call(
        body, name="adamw_small", out_shape=shapes * 4, in_specs=[vm] * (4 * n), out_specs=[vm] * (4 * n),
    )(*grads, *ws, *ms, *vs)


def _unshard_cols(g):
    n, R, Cc = g.shape
    return jnp.transpose(g, (1, 0, 2)).reshape(R, n * Cc)


def _shard_cols(full):
    R, Ct = full.shape
    return jnp.transpose(full.reshape(R, N_DEV, Ct // N_DEV), (1, 0, 2))


_COL_SHARDED = ("w_in", "w_gate", "w_up")
_BIG = ("w_in", "w_out", "w_gate", "w_up", "w_down")


def _rows(nm, p):
    return p[0].T if nm in _COL_SHARDED else p[0]


def _step(args, ts_mix_fwd, ts_ffn, ts_mix_bwd, ts_grad):
    (x, g_mix, w_in, b_in, w_dw, b_dw, ln_g, ln_b, w_pool, s_pool, w_out, g_ffn, w_gate, w_up, w_down, g_final,
     loss_target) = args[:17]
    names = ["g_mix", "w_in", "b_in", "w_dw", "b_dw", "ln_g", "ln_b", "w_pool", "s_pool", "w_out", "g_ffn",
             "w_gate", "w_up", "w_down", "g_final"]
    weights = dict(zip(names, args[1:16]))
    moms = dict(zip(names, args[17:32]))
    vars_ = dict(zip(names, args[32:47]))

    S, D = x.shape[1], x.shape[2]
    x2 = x.reshape(S, D)
    tgt2 = loss_target.reshape(S, D)

    bf = lambda nm: _rows(nm, weights[nm]).astype(BF16)
    g_in, g_out, g_dw = _all_gather([bf("w_in"), bf("w_out"), w_dw[0]])
    wt_in, w_out_f = g_in.reshape(-1, D), g_out.reshape(-1, D)
    w_dw_f = _unshard_cols(g_dw)
    w_pool_b = w_pool[0].astype(BF16)

    a, gate, v, m, y, h1, g_gate, g_up, g_down = _mix_fwd(
        x2, g_mix, wt_in, b_in, w_dw_f, b_dw, ln_g, ln_b, w_pool_b, s_pool, w_out_f,
        [bf("w_gate"), bf("w_up"), bf("w_down")], ts_mix_fwd)
    wt_gate, wt_up, w_down_f = g_gate.reshape(-1, D), g_up.reshape(-1, D), g_down.reshape(-1, D)
    Fd = wt_gate.shape[0]
    f_chunks = [1024] * (Fd // 1024) + ([Fd % 1024] if Fd % 1024 else [])
    dh1, hn, act, dgt, dup, dh2, loss_p, dg_final, dg_ffn = _ffn(
        h1, tgt2, g_ffn, g_final.reshape(1, D), wt_gate, wt_up, w_down_f, ts_ffn, f_chunks)

    ffn_grads = [_grad_matmul(dgt, hn, Fd // 2, D, ts_grad, "grad_w_gate"),
                 _grad_matmul(dup, hn, Fd // 2, D, ts_grad, "grad_w_up"),
                 _grad_matmul(act, dh2, Fd // 2, D, ts_grad, "grad_w_down")]
    (dx, dg_mix, db_in, dw_dw, db_dw, dln_g, dln_b, dw_pool, ds_pool, dwt_in, dw_out, r_gate, r_up,
     r_down) = _mix_bwd(dh1, x2, a, gate, v, m, y, g_mix, wt_in, w_dw_f, ln_g, ln_b, w_pool_b, s_pool, w_out_f,
                        [g.reshape(N_DEV, -1, D) for g in ffn_grads], ts_mix_bwd)

    small_names = ["g_mix", "b_in", "b_dw", "ln_g", "ln_b", "w_pool", "s_pool", "g_ffn", "g_final"]
    small_shape = lambda p: p.reshape(-1, p.shape[-1])
    partial = [dg_mix, db_in, db_dw, dln_g, dln_b, dw_pool.reshape(-1, POOL_GROUP), ds_pool, dg_ffn, dg_final, loss_p]
    tail = _tail_exchange([dwt_in.reshape(N_DEV, -1, D), dw_out.reshape(N_DEV, -1, D),
                           _shard_cols(dw_dw[0:CONV_WIDTH])], partial)
    r_in, r_out, r_dw = tail[:3]
    summed = tail[3:]

    big = {}
    for nm, r in zip(_BIG, [r_in, r_out, r_gate, r_up, r_down]):
        rows = r.shape[1]
        res = _adamw_shard(r, _rows(nm, weights[nm]), _rows(nm, moms[nm]), _rows(nm, vars_[nm]),
                           rows // 2 if rows % 32 == 0 else rows, "adamw_" + nm)
        big[nm] = [o.T if nm in _COL_SHARDED else o for o in res]
    big["w_dw"] = _adamw_shard(r_dw, w_dw[0], moms["w_dw"][0], vars_["w_dw"][0], CONV_WIDTH, "adamw_w_dw")

    sm = _adamw_small(summed[:-1], [small_shape(weights[nm]) for nm in small_names],
                      [small_shape(moms[nm]) for nm in small_names], [small_shape(vars_[nm]) for nm in small_names])
    n_small = len(small_names)

    def result(kind, nm):
        if nm in big:
            return big[nm][kind].reshape(weights[nm].shape)
        return sm[kind * n_small + small_names.index(nm)].reshape(weights[nm].shape)

    loss = summed[-1][0, 0]
    out = [loss, dx.reshape(x.shape)]
    for kind in range(4):
        out += [result(kind, nm) for nm in names]
    return tuple(out)


def kernel(x, g_mix, w_in, b_in, w_dw, b_dw, ln_g, ln_b, w_pool, s_pool, w_out, g_ffn, w_gate, w_up, w_down, g_final, loss_target, m_g_mix, m_w_in, m_b_in, m_w_dw, m_b_dw, m_ln_g, m_ln_b, m_w_pool, m_s_pool, m_w_out, m_g_ffn, m_w_gate, m_w_up, m_w_down, m_g_final, v_g_mix, v_w_in, v_b_in, v_w_dw, v_b_dw, v_ln_g, v_ln_b, v_w_pool, v_s_pool, v_w_out, v_g_ffn, v_w_gate, v_w_up, v_w_down, v_g_final):
    args = (x, g_mix, w_in, b_in, w_dw, b_dw, ln_g, ln_b, w_pool, s_pool, w_out, g_ffn, w_gate, w_up, w_down, g_final, loss_target, m_g_mix, m_w_in, m_b_in, m_w_dw, m_b_dw, m_ln_g, m_ln_b, m_w_pool, m_s_pool, m_w_out, m_g_ffn, m_w_gate, m_w_up, m_w_down, m_g_final, v_g_mix, v_w_in, v_b_in, v_w_dw, v_b_dw, v_ln_g, v_ln_b, v_w_pool, v_s_pool, v_w_out, v_g_ffn, v_w_gate, v_w_up, v_w_down, v_g_final)
    return _step(args, ts_mix_fwd=512, ts_ffn=256, ts_mix_bwd=256, ts_grad=512)
```

```python
import functools

import jax
import jax.numpy as jnp
from jax import lax
from jax.experimental import pallas as pl
from jax.experimental.pallas import tpu as pltpu

F32 = jnp.float32
BF16 = jnp.bfloat16
MESH = pl.DeviceIdType.MESH
N_DEV = 8

C_CONV = 512
CONV_WIDTH = 31
POOL_WINDOWS = (2, 4, 8, 16)
POOL_GROUP = 128
RMS_EPS = 1e-6
LN_EPS = 1e-5

ADAM_LR = 0.001
ADAM_B1 = 0.9
ADAM_B2 = 0.999
ADAM_EPS = 1e-08
ADAM_WD = 0.01
ADAM_STEP = 10

HALO = 32
SUBLANES = 8
CONV_ROWS = 64
VMEM_LIMIT = 56 * 1024 * 1024


def _dot(a, b):
    return jnp.dot(a, b, preferred_element_type=F32)


def _dot_nt(a, b):
    return lax.dot_general(a, b, (((1,), (1,)), ((), ())), preferred_element_type=F32)


def _dot_tn(a, b):
    return lax.dot_general(a, b, (((0,), (0,)), ((), ())), preferred_element_type=F32)


def _mean_last(v):
    return jnp.mean(v, axis=-1, keepdims=True)


def _full(shape):
    nd = len(shape)
    return pl.BlockSpec(shape, lambda *_: (0,) * nd)


def _full1(shape):
    nd = len(shape)
    return pl.BlockSpec(shape, lambda *_: (0,) * nd, pipeline_mode=pl.Buffered(1))


def _shifted_copies(sh_ref, rows):
    for s in range(1, SUBLANES):
        sh_ref[s, 0:rows, :] = sh_ref[0, s:s + rows, :]


def _tap(sh_ref, off, r0, rows):
    q, s = divmod(off, SUBLANES)
    return sh_ref[s, pl.ds(r0 + q * SUBLANES, rows), :]


def _mix_fwd(x, g_mix, w_in, b_in, w_dw, b_dw, ln_g, ln_b, w_pool, s_pool, w_out, ag_shards, ts):
    S, D = x.shape
    d_in = w_in.shape[0]
    C = C_CONV
    nt = S // ts
    nrb = ts // CONV_ROWS
    n_ag = len(ag_shards)

    def body(x_ref, g_ref, win_ref, bin_ref, wdw_ref, bdw_ref, lng_ref, lnb_ref, wp_ref, sp_ref, wout_ref, *rest):
        ag_in, rest = rest[:n_ag], rest[n_ag:]
        a_ref, gate_ref, v_ref, m_ref, y_ref, h1_ref, xn_ref = rest[:7]
        ag_out, rest = rest[7:7 + n_ag], rest[7 + n_ag:]
        ush, pbuf, send_sems, recv_sems, local_sems = rest
        i = pl.program_id(0)
        ag_start, ag_finish = _gather_plan(ag_in, ag_out, send_sems, recv_sems, local_sems)

        @pl.when(i == 0)
        def _():
            ag_start()
            ush[0, 0:HALO, :] = jnp.zeros((HALO, C), F32)
            pbuf[0:HALO, :] = jnp.zeros((HALO, C), F32)

        x = x_ref[...]
        r1 = lax.rsqrt(_mean_last(x * x) + RMS_EPS)
        xn = (x * r1 * g_ref[...]).astype(BF16)
        xn_ref[...] = xn
        z = _dot_nt(xn, win_ref[...]) + bin_ref[...]
        a = z[:, 0:C]
        gate = z[:, C:2 * C]
        a_ref[...] = a
        gate_ref[...] = gate
        ush[0, HALO:HALO + ts, :] = a * jax.nn.sigmoid(gate)
        pbuf[HALO:HALO + ts, :] = z[:, 2 * C:]

        _shifted_copies(ush, ts + HALO - SUBLANES)

        def conv_block(rb, carry):
            r0 = pl.multiple_of(rb * CONV_ROWS, CONV_ROWS)
            acc = jnp.zeros((CONV_ROWS, C), F32)
            for k in range(CONV_WIDTH):
                acc = acc + wdw_ref[k:k + 1, :] * _tap(ush, HALO - (CONV_WIDTH - 1) + k, r0, CONV_ROWS)
            v_ref[pl.ds(r0, CONV_ROWS), :] = acc + bdw_ref[...]
            return carry

        lax.fori_loop(0, nrb, conv_block, 0)

        v = v_ref[...]
        mu = _mean_last(v)
        xc = v - mu
        rstd = lax.rsqrt(_mean_last(xc * xc) + LN_EPS)
        ln = xc * rstd * lng_ref[...] + lnb_ref[...]
        y_ref[:, 0:C] = (ln * jax.nn.sigmoid(ln)).astype(BF16)

        row = lax.broadcasted_iota(jnp.int32, (ts, 1), 0) + i * ts
        for gi, w in enumerate(POOL_WINDOWS):
            lanes = slice(gi * POOL_GROUP, (gi + 1) * POOL_GROUP)
            seg = pbuf[HALO:HALO + ts, lanes]
            ws = seg
            for k in range(1, w):
                ws = ws + pbuf[HALO - k:HALO - k + ts, lanes]
            cnt = jnp.minimum(row + 1, w).astype(F32)
            m = (ws / cnt - seg).astype(BF16)
            m_ref[:, lanes] = m
            ypre = _dot(m, wp_ref[gi])
            y_ref[:, C + gi * POOL_GROUP:C + (gi + 1) * POOL_GROUP] = (ypre * sp_ref[:, lanes]).astype(BF16)

        h1_ref[...] = x + _dot(y_ref[...], wout_ref[...])

        ush[0, 0:HALO, :] = ush[0, ts:ts + HALO, :]
        pbuf[0:HALO, :] = pbuf[ts:ts + HALO, :]

        @pl.when(i == nt - 1)
        def _():
            ag_finish()

    tile = lambda w, dt: (pl.BlockSpec((ts, w), lambda i: (i, 0)), jax.ShapeDtypeStruct((S, w), dt))
    outs = [tile(C, F32), tile(C, F32), tile(C, F32), tile(C, BF16), tile(D, BF16), tile(D, F32), tile(D, BF16)]
    return pl.pallas_call(
        body, name="mix_fwd", grid=(nt,),
        in_specs=[pl.BlockSpec((ts, D), lambda i: (i, 0)), _full((1, D)), _full((d_in, D)), _full((1, d_in)),
                  _full(w_dw.shape), _full((1, C)), _full((1, C)), _full((1, C)), _full(w_pool.shape),
                  _full((1, C)), _full((D, D))] + [_HBM] * n_ag,
        out_specs=[o[0] for o in outs] + [_HBM] * n_ag,
        out_shape=[o[1] for o in outs] + _gather_out_shapes(ag_shards),
        scratch_shapes=[pltpu.VMEM((SUBLANES, ts + HALO, C), F32), pltpu.VMEM((ts + HALO, C), F32)]
        + _comm_sems(n_ag),
        compiler_params=pltpu.CompilerParams(dimension_semantics=("arbitrary",), vmem_limit_bytes=VMEM_LIMIT),
    )(x, g_mix, w_in, b_in, w_dw, b_dw, ln_g, ln_b, w_pool, s_pool, w_out, *ag_shards)


def _ffn(h1, target, g_ffn, g_final, w_gate, w_up, w_down, ts, f_chunks):
    S, D = h1.shape
    Fd = w_gate.shape[0]
    nt = S // ts
    bounds = []
    lo = 0
    for n in f_chunks:
        bounds.append((lo, lo + n))
        lo += n
    assert lo == Fd

    def body(h1_ref, tgt_ref, gf_ref, gl_ref, wg_ref, wu_ref, wd_ref,
             dh1_ref, hn_ref, act_ref, dgt_ref, dup_ref, dh2_ref, loss_ref, dgl_ref, dgf_ref, gt_s, up_s):
        i = pl.program_id(0)

        @pl.when(i == 0)
        def _():
            loss_ref[...] = jnp.zeros_like(loss_ref)
            dgl_ref[...] = jnp.zeros_like(dgl_ref)
            dgf_ref[...] = jnp.zeros_like(dgf_ref)

        h1 = h1_ref[...]
        r2 = lax.rsqrt(_mean_last(h1 * h1) + RMS_EPS)
        hhat = h1 * r2
        hn = (hhat * gf_ref[...]).astype(BF16)
        hn_ref[...] = hn
        h2 = h1
        for lo, hi in bounds:
            gt = _dot_nt(hn, wg_ref[lo:hi, :])
            up = _dot_nt(hn, wu_ref[lo:hi, :])
            gt_s[:, lo:hi] = gt
            up_s[:, lo:hi] = up
            act = (gt * jax.nn.sigmoid(gt) * up).astype(BF16)
            act_ref[:, lo:hi] = act
            h2 = h2 + _dot(act, wd_ref[lo:hi, :])

        r3 = lax.rsqrt(_mean_last(h2 * h2) + RMS_EPS)
        n3 = h2 * r3
        gl = gl_ref[...]
        diff = n3 * gl - tgt_ref[...]
        loss_ref[...] += jnp.sum(0.5 * jnp.sum(diff * diff, axis=-1, keepdims=True) / D, axis=0, keepdims=True)
        dout = diff / D
        dgl_ref[...] += jnp.sum(dout * n3, axis=0, keepdims=True)
        dn = dout * gl
        dh2 = r3 * (dn - n3 * _mean_last(dn * n3))
        dh2b = dh2.astype(BF16)
        dh2_ref[...] = dh2b

        dhn = jnp.zeros((ts, D), F32)
        for lo, hi in bounds:
            gt = gt_s[:, lo:hi]
            up = up_s[:, lo:hi]
            sg = jax.nn.sigmoid(gt)
            dact = _dot_nt(dh2b, wd_ref[lo:hi, :])
            dgt = (dact * up * (sg * (1.0 + gt * (1.0 - sg)))).astype(BF16)
            dup = (dact * (gt * sg)).astype(BF16)
            dgt_ref[:, lo:hi] = dgt
            dup_ref[:, lo:hi] = dup
            dhn = dhn + _dot(dgt, wg_ref[lo:hi, :]) + _dot(dup, wu_ref[lo:hi, :])

        dgf_ref[...] += jnp.sum(dhn * hhat, axis=0, keepdims=True)
        dnn = dhn * gf_ref[...]
        dh1_ref[...] = dh2 + r2 * (dnn - hhat * _mean_last(dnn * hhat))

    tile = lambda w, dt: (pl.BlockSpec((ts, w), lambda i: (i, 0)), jax.ShapeDtypeStruct((S, w), dt))
    acc = lambda w: (_full((1, w)), jax.ShapeDtypeStruct((1, w), F32))
    outs = [tile(D, F32), tile(D, BF16), tile(Fd, BF16), tile(Fd, BF16), tile(Fd, BF16), tile(D, BF16),
            acc(128), acc(D), acc(D)]
    return pl.pallas_call(
        body, name="ffn_fwd_bwd", grid=(nt,),
        in_specs=[pl.BlockSpec((ts, D), lambda i: (i, 0)), pl.BlockSpec((ts, D), lambda i: (i, 0)),
                  _full((1, D)), _full((1, D)), _full1((Fd, D)), _full1((Fd, D)), _full1((Fd, D))],
        out_specs=[o[0] for o in outs], out_shape=[o[1] for o in outs],
        scratch_shapes=[pltpu.VMEM((ts, Fd), F32), pltpu.VMEM((ts, Fd), F32)],
        compiler_params=pltpu.CompilerParams(dimension_semantics=("arbitrary",), vmem_limit_bytes=VMEM_LIMIT),
    )(h1, target, g_ffn, g_final, w_gate, w_up, w_down)


def _mix_bwd(dh1, x, a, gate, v, m, g_mix, w_in, w_dw, ln_g, ln_b, w_pool, s_pool, w_out, rs_parts, ts):
    S, D = x.shape
    n_rs = len(rs_parts)
    d_in = w_in.shape[0]
    C = C_CONV
    nt = S // ts
    nrb = ts // CONV_ROWS
    hb = ts // HALO
    wrows = ((CONV_WIDTH + SUBLANES - 1) // SUBLANES) * SUBLANES

    def body(dh1_ref, x_ref, a_ref, gate_ref, ah_ref, gh_ref, v_ref, m_ref, g_ref, win_ref, wdw_ref, lng_ref,
             lnb_ref, wp_ref, sp_ref, wout_ref, *rest):
        rs_in, rest = rest[:n_rs], rest[n_rs:]
        (dx_ref, dz_ref, dh1b_ref, dgm_ref, dbin_ref, dwdw_ref, dbdw_ref, dlng_ref, dlnb_ref, dwp_ref,
         dsp_ref) = rest[:11]
        rs_out, rest = rest[11:11 + n_rs], rest[11 + n_rs:]
        ush, dvsh, dqbuf, du_s, send_sems, recv_sems, local_sems = rest
        i = pl.program_id(0)
        t = nt - 1 - i
        rs_start, rs_finish = _chip_exchange_plan(rs_in, rs_out, send_sems, recv_sems, local_sems)

        @pl.when(i == 0)
        def _():
            rs_start()
            dvsh[0, ts:ts + HALO, :] = jnp.zeros((HALO, C), F32)
            dqbuf[ts:ts + HALO, :] = jnp.zeros((HALO, C), F32)
            for r in (dgm_ref, dbin_ref, dwdw_ref, dbdw_ref, dlng_ref, dlnb_ref, dwp_ref, dsp_ref):
                r[...] = jnp.zeros_like(r)

        dh1 = dh1_ref[...]
        dh1b = dh1.astype(BF16)
        dh1b_ref[...] = dh1b
        dy = _dot_nt(dh1b, wout_ref[...])

        v = v_ref[...]
        mu = _mean_last(v)
        xc = v - mu
        rstd = lax.rsqrt(_mean_last(xc * xc) + LN_EPS)
        vhat = xc * rstd
        lng = lng_ref[...]
        ln = vhat * lng + lnb_ref[...]
        sg = jax.nn.sigmoid(ln)
        dln = dy[:, 0:C] * (sg * (1.0 + ln * (1.0 - sg)))
        dlng_ref[...] += jnp.sum(dln * vhat, axis=0, keepdims=True)
        dlnb_ref[...] += jnp.sum(dln, axis=0, keepdims=True)
        dvh = dln * lng
        dv = rstd * (dvh - _mean_last(dvh) - vhat * _mean_last(dvh * vhat))
        dbdw_ref[...] += jnp.sum(dv, axis=0, keepdims=True)
        dvsh[0, 0:ts, :] = dv
        _shifted_copies(dvsh, ts + HALO - SUBLANES)

        a = a_ref[...]
        sgate = jax.nn.sigmoid(gate_ref[...])
        ush[0, HALO:HALO + ts, :] = a * sgate
        uh = ah_ref[...] * jax.nn.sigmoid(gh_ref[...])
        ush[0, 0:HALO, :] = jnp.where(t > 0, uh, 0.0)
        _shifted_copies(ush, ts + HALO - SUBLANES)

        def conv_block(rb, carry):
            r0 = pl.multiple_of(rb * CONV_ROWS, CONV_ROWS)
            acc = jnp.zeros((CONV_ROWS, C), F32)
            for k in range(CONV_WIDTH):
                acc = acc + wdw_ref[k:k + 1, :] * _tap(dvsh, CONV_WIDTH - 1 - k, r0, CONV_ROWS)
            du_s[pl.ds(r0, CONV_ROWS), :] = acc
            return carry

        lax.fori_loop(0, nrb, conv_block, 0)

        for k in range(CONV_WIDTH):
            q, s = divmod(HALO - (CONV_WIDTH - 1) + k, SUBLANES)
            prod = dvsh[0, 0:ts, :] * ush[s, q * SUBLANES:q * SUBLANES + ts, :]
            dwdw_ref[k:k + 1, :] += jnp.sum(prod, axis=0, keepdims=True)

        du = du_s[...]
        da = du * sgate
        dgate = du * a * sgate * (1.0 - sgate)
        dz_ref[:, 0:C] = da.astype(BF16)
        dz_ref[:, C:2 * C] = dgate.astype(BF16)
        dbin_ref[:, 0:C] += jnp.sum(da, axis=0, keepdims=True)
        dbin_ref[:, C:2 * C] += jnp.sum(dgate, axis=0, keepdims=True)

        row = lax.broadcasted_iota(jnp.int32, (ts, 1), 0) + t * ts
        for gi, w in enumerate(POOL_WINDOWS):
            lanes = slice(gi * POOL_GROUP, (gi + 1) * POOL_GROUP)
            dyp = dy[:, C + gi * POOL_GROUP:C + (gi + 1) * POOL_GROUP]
            mg = m_ref[:, lanes]
            ypre = _dot(mg, wp_ref[gi])
            dsp_ref[:, lanes] += jnp.sum(dyp * ypre, axis=0, keepdims=True)
            dyi = (dyp * sp_ref[:, lanes]).astype(BF16)
            dwp_ref[gi] += _dot_tn(mg, dyi)
            dm = _dot_nt(dyi, wp_ref[gi])
            cnt = jnp.minimum(row + 1, w).astype(F32)
            dqbuf[0:ts, lanes] = dm / cnt
            dp = -dm
            for k in range(w):
                dp = dp + dqbuf[k:k + ts, lanes]
            dz_ref[:, 2 * C + gi * POOL_GROUP:2 * C + (gi + 1) * POOL_GROUP] = dp.astype(BF16)
            dbin_ref[:, 2 * C + gi * POOL_GROUP:2 * C + (gi + 1) * POOL_GROUP] += jnp.sum(dp, axis=0, keepdims=True)

        dxn = _dot(dz_ref[...], win_ref[...])
        x = x_ref[...]
        r1 = lax.rsqrt(_mean_last(x * x) + RMS_EPS)
        xhat = x * r1
        dgm_ref[...] += jnp.sum(dxn * xhat, axis=0, keepdims=True)
        dnn = dxn * g_ref[...]
        dx_ref[...] = dh1 + r1 * (dnn - xhat * _mean_last(dnn * xhat))

        dvsh[0, ts:ts + HALO, :] = dvsh[0, 0:HALO, :]
        dqbuf[ts:ts + HALO, :] = dqbuf[0:HALO, :]

        @pl.when(i == nt - 1)
        def _():
            rs_finish()

    rev = lambda w: pl.BlockSpec((ts, w), lambda i: (nt - 1 - i, 0))
    halo = pl.BlockSpec((HALO, C), lambda i: (jnp.maximum((nt - 1 - i) * hb - 1, 0), 0))
    acc = lambda shape: (_full(shape), jax.ShapeDtypeStruct(shape, F32))
    outs = [(rev(D), jax.ShapeDtypeStruct((S, D), F32)), (rev(d_in), jax.ShapeDtypeStruct((S, d_in), BF16)),
            (rev(D), jax.ShapeDtypeStruct((S, D), BF16)), acc((1, D)), acc((1, d_in)), acc((wrows, C)),
            acc((1, C)), acc((1, C)), acc((1, C)), acc(w_pool.shape), acc((1, C))]
    return pl.pallas_call(
        body, name="mix_bwd", grid=(nt,),
        in_specs=[rev(D), rev(D), rev(C), rev(C), halo, halo, rev(C), rev(C), _full((1, D)), _full((d_in, D)),
                  _full(w_dw.shape), _full((1, C)), _full((1, C)), _full(w_pool.shape), _full((1, C)),
                  _full((D, D))] + [_HBM] * n_rs,
        out_specs=[o[0] for o in outs] + [_HBM] * n_rs,
        out_shape=[o[1] for o in outs] + [jax.ShapeDtypeStruct(p.shape, p.dtype) for p in rs_parts],
        scratch_shapes=[pltpu.VMEM((SUBLANES, ts + HALO, C), F32), pltpu.VMEM((SUBLANES, ts + HALO, C), F32),
                        pltpu.VMEM((ts + HALO, C), F32), pltpu.VMEM((ts, C), F32)] + _chip_sems(n_rs),
        compiler_params=pltpu.CompilerParams(dimension_semantics=("arbitrary",), vmem_limit_bytes=VMEM_LIMIT),
    )(dh1, x, a, gate, a, gate, v, m, g_mix, w_in, w_dw, ln_g, ln_b, w_pool, s_pool, w_out, *rs_parts)


def _grad_matmul(a, b, tm, name):
    S, M = a.shape
    N = b.shape[1]

    def body(a_ref, b_ref, o_ref):
        o_ref[...] = _dot_tn(a_ref[...], b_ref[...]).astype(BF16)

    return pl.pallas_call(
        body, name=name, grid=(M // tm,),
        in_specs=[pl.BlockSpec((S, tm), lambda i: (0, i)), _full1((S, N))],
        out_specs=pl.BlockSpec((tm, N), lambda i: (i, 0)),
        out_shape=jax.ShapeDtypeStruct((M, N), BF16),
        compiler_params=pltpu.CompilerParams(dimension_semantics=("parallel",), vmem_limit_bytes=VMEM_LIMIT),
    )(a, b)


def _position():
    return lax.axis_index("x"), lax.axis_index("y"), lax.axis_index("c")


def _slot(px, py, pc):
    return 4 * px + 2 * py + pc


def _all_gather(shards):
    n = len(shards)

    def body(*refs):
        start, finish = _gather_plan(refs[:n], refs[n:2 * n], *refs[2 * n:])
        start()
        finish()

    return pl.pallas_call(
        body, name="all_gather_mix_weights",
        out_shape=_gather_out_shapes(shards), in_specs=[_HBM] * n, out_specs=[_HBM] * n,
        scratch_shapes=_comm_sems(n),
    )(*shards)


_HBM = pl.BlockSpec(memory_space=pl.ANY)
_FLIPS = [(fx, fy, fc) for fx in (0, 1) for fy in (0, 1) for fc in (0, 1)][1:]


def _comm_sems(n):
    return [pltpu.SemaphoreType.DMA((n, 7)), pltpu.SemaphoreType.DMA((n, 7)), pltpu.SemaphoreType.DMA((n,))]


def _gather_out_shapes(shards):
    return [jax.ShapeDtypeStruct((N_DEV,) + s.shape, s.dtype) for s in shards]


def _gather_plan(ins, outs, send_sems, recv_sems, local_sems):
    n = len(ins)
    x, y, c = _position()
    me, sibling = (x, y, c), (x, y, 1 - c)
    chips = [(1 - x, y), (x, 1 - y), (1 - x, 1 - y)]

    def copy(k, sem, block, to, src=None):
        dst = outs[k].at[_slot(*block)]
        return pltpu.make_async_remote_copy(
            src_ref=dst if src is None else src, dst_ref=dst, send_sem=send_sems.at[k, sem],
            recv_sem=recv_sems.at[k, sem], device_id=to, device_id_type=MESH)

    def mine():
        return [pltpu.make_async_copy(ins[k], outs[k].at[_slot(*me)], local_sems.at[k]) for k in range(n)]

    def first():
        cps = []
        for k in range(n):
            cps.append(copy(k, 0, me, sibling, src=ins[k]))
            cps += [copy(k, 1 + j, me, (*chip, c), src=ins[k]) for j, chip in enumerate(chips)]
        return cps

    def start():
        for cp in mine() + first():
            cp.start()

    def finish():
        passed = []
        for j, chip in enumerate(chips):
            for k in range(n):
                copy(k, 1 + j, (*chip, c), me).wait_recv()
                fwd = copy(k, 4 + j, (*chip, c), sibling)
                fwd.start()
                passed.append(fwd)
        for k in range(n):
            copy(k, 0, sibling, me).wait_recv()
            for j, chip in enumerate(chips):
                copy(k, 4 + j, (*chip, 1 - c), me).wait_recv()
        for cp in first() + passed:
            cp.wait_send()
        for cp in mine():
            cp.wait()

    return start, finish


N_CHIPS = 4
_CHIP_FLIPS = [(1, 0), (0, 1), (1, 1)]


def _chip_sems(n):
    return [pltpu.SemaphoreType.DMA((n, 3)), pltpu.SemaphoreType.DMA((n, 3)), pltpu.SemaphoreType.DMA((n,))]


def _chip_exchange_plan(ins, outs, send_sems, recv_sems, local_sems):
    n = len(ins)
    x, y, c = _position()
    my_q = 2 * x + y
    peers = [(x ^ fx, y ^ fy) for fx, fy in _CHIP_FLIPS]

    def mine():
        return [pltpu.make_async_copy(ins[k].at[my_q], outs[k].at[my_q], local_sems.at[k]) for k in range(n)]

    def sends():
        return [pltpu.make_async_remote_copy(
            src_ref=ins[k].at[2 * px + py], dst_ref=outs[k].at[my_q], send_sem=send_sems.at[k, f],
            recv_sem=recv_sems.at[k, f], device_id=(px, py, c), device_id_type=MESH)
            for f, (px, py) in enumerate(peers) for k in range(n)]

    def start():
        for cp in mine() + sends():
            cp.start()

    def finish():
        for f, (px, py) in enumerate(peers):
            for k in range(n):
                pltpu.make_async_remote_copy(
                    src_ref=ins[k].at[my_q], dst_ref=outs[k].at[2 * px + py], send_sem=send_sems.at[k, f],
                    recv_sem=recv_sems.at[k, f], device_id=(px, py, c), device_id_type=MESH).wait_recv()
        for cp in sends():
            cp.wait_send()
        for cp in mine():
            cp.wait()

    return start, finish


def _pair_exchange(parts):
    n = len(parts)

    def body(*refs):
        ins, own, sib = refs[:n], refs[n:2 * n], refs[2 * n:3 * n]
        send_sems, recv_sems, local_sems = refs[3 * n:]
        x, y, c = _position()
        sibling = (x, y, 1 - c)
        local, remote = [], []
        for k in range(n):
            for q in range(N_CHIPS):
                local.append(pltpu.make_async_copy(ins[k].at[2 * q + c], own[k].at[q], local_sems.at[k, q]))
                remote.append(pltpu.make_async_remote_copy(
                    src_ref=ins[k].at[2 * q + 1 - c], dst_ref=sib[k].at[q], send_sem=send_sems.at[k, q],
                    recv_sem=recv_sems.at[k, q], device_id=sibling, device_id_type=MESH))
        for cp in remote + local:
            cp.start()
        for cp in remote:
            cp.wait()
        for cp in local:
            cp.wait()

    half = [jax.ShapeDtypeStruct((N_CHIPS,) + p.shape[1:], p.dtype) for p in parts]
    sems = pltpu.SemaphoreType.DMA((n, N_CHIPS))
    res = pl.pallas_call(
        body, name="pair_exchange", out_shape=half + half, in_specs=[_HBM] * n, out_specs=[_HBM] * (2 * n),
        scratch_shapes=[sems, sems, sems],
    )(*parts)
    return res[:n], res[n:]


def _pair_sum(own, sib):
    n = len(own)

    def body(*refs):
        for k in range(n):
            refs[2 * n + k][...] = (refs[k][...].astype(F32) + refs[n + k][...].astype(F32)).astype(BF16)

    specs = [pl.BlockSpec((1,) + o.shape[1:], lambda q: (q, 0, 0)) for o in own]
    return pl.pallas_call(
        body, name="pair_sum", grid=(N_CHIPS,), in_specs=specs + specs, out_specs=specs,
        out_shape=[jax.ShapeDtypeStruct(o.shape, BF16) for o in own],
        compiler_params=pltpu.CompilerParams(dimension_semantics=("parallel",)),
    )(*own, *sib)


def _tail_reduce(parts, vals):
    n, ns = len(parts), len(vals)
    vm = pl.BlockSpec(memory_space=pltpu.VMEM)

    def body(*refs):
        p_in, v_in = refs[:n], refs[n:n + ns]
        p_out, v_out = refs[n + ns:2 * n + ns], refs[2 * n + ns:2 * (n + ns)]
        rest = refs[2 * (n + ns):]
        p_sib, p_sum = rest[:n], rest[n:2 * n]
        v_sib, v_sum, v_all = rest[2 * n:2 * n + ns], rest[2 * n + ns:2 * n + 2 * ns], rest[2 * n + 2 * ns:2 * n + 3 * ns]
        p1_send, p1_recv, v1_send, v1_recv, p3_send, p3_recv, v3_send, v3_recv = rest[2 * n + 3 * ns:]
        x, y, c = _position()
        my_q = 2 * x + y
        sibling = (x, y, 1 - c)
        peers = [(x ^ fx, y ^ fy) for fx, fy in _CHIP_FLIPS]

        def to_sibling(src, dst, send, recv):
            return pltpu.make_async_remote_copy(src_ref=src, dst_ref=dst, send_sem=send, recv_sem=recv,
                                                device_id=sibling, device_id_type=MESH)

        level1 = [to_sibling(p_in[k].at[2 * q + 1 - c], p_sib[k].at[q], p1_send.at[k, q], p1_recv.at[k, q])
                  for k in range(n) for q in range(N_CHIPS)]
        level1 += [to_sibling(v_in[k], v_sib[k], v1_send.at[k], v1_recv.at[k]) for k in range(ns)]
        for cp in level1:
            cp.start()
        for cp in level1:
            cp.wait_recv()
        for k in range(n):
            for q in range(N_CHIPS):
                p_sum[k][q] = (p_in[k][2 * q + c].astype(F32) + p_sib[k][q].astype(F32)).astype(p_sum[k].dtype)
        for k in range(ns):
            v_sum[k][...] = v_in[k][...] + v_sib[k][...]

        def to_chip(f, src, dst, send, recv):
            px, py = peers[f]
            return pltpu.make_async_remote_copy(src_ref=src, dst_ref=dst, send_sem=send, recv_sem=recv,
                                                device_id=(px, py, c), device_id_type=MESH)

        def level2(sending):
            cps = []
            for f, (px, py) in enumerate(peers):
                their_q = 2 * px + py
                for k in range(n):
                    src, dst = (p_sum[k].at[their_q], p_out[k].at[my_q]) if sending else (
                        p_sum[k].at[my_q], p_out[k].at[their_q])
                    cps.append(to_chip(f, src, dst, p3_send.at[k, f], p3_recv.at[k, f]))
                for k in range(ns):
                    dst = v_all[k].at[my_q] if sending else v_all[k].at[their_q]
                    cps.append(to_chip(f, v_sum[k], dst, v3_send.at[k, f], v3_recv.at[k, f]))
            return cps

        for cp in level2(True):
            cp.start()
        for k in range(n):
            p_out[k][my_q] = p_sum[k][my_q]
        for k in range(ns):
            v_all[k][my_q] = v_sum[k][...]
        for cp in level2(False):
            cp.wait_recv()
        for k in range(ns):
            total = v_all[k][0]
            for q in range(1, N_CHIPS):
                total = total + v_all[k][q]
            v_out[k][...] = total
        for cp in level1 + level2(True):
            cp.wait_send()

    quarter = [(N_CHIPS,) + p.shape[1:] for p in parts]
    dma = pltpu.SemaphoreType.DMA
    return pl.pallas_call(
        body, name="tail_reduce",
        out_shape=[jax.ShapeDtypeStruct(s, p.dtype) for s, p in zip(quarter, parts)]
        + [jax.ShapeDtypeStruct(v.shape, v.dtype) for v in vals],
        in_specs=[vm] * (n + ns), out_specs=[vm] * (n + ns),
        scratch_shapes=[pltpu.VMEM(s, p.dtype) for s, p in zip(quarter, parts)] * 2
        + [pltpu.VMEM(v.shape, v.dtype) for v in vals] * 2
        + [pltpu.VMEM((N_CHIPS,) + v.shape, v.dtype) for v in vals]
        + [dma((n, N_CHIPS)), dma((n, N_CHIPS)), dma((ns,)), dma((ns,)), dma((n, 3)), dma((n, 3)), dma((ns, 3)),
           dma((ns, 3))],
        compiler_params=pltpu.CompilerParams(vmem_limit_bytes=VMEM_LIMIT),
    )(*parts, *vals)


def _adamw_math(w, g, m, v):
    m = ADAM_B1 * m + (1.0 - ADAM_B1) * g
    v = ADAM_B2 * v + (1.0 - ADAM_B2) * (g * g)
    m_hat = m / (1.0 - ADAM_B1 ** ADAM_STEP)
    v_hat = v / (1.0 - ADAM_B2 ** ADAM_STEP)
    delta = -ADAM_LR * (m_hat / (jnp.sqrt(v_hat) + ADAM_EPS) + ADAM_WD * w)
    return delta, m, v


def _adamw_shard(parts, w, m, v, tr, name):
    R, Cc = w.shape

    def body(p_ref, w_ref, m_ref, v_ref, g_out, d_out, m_out, v_out):
        g = p_ref[0].astype(F32)
        for j in range(1, N_CHIPS):
            g = g + p_ref[j].astype(F32)
        d, mn, vn = _adamw_math(w_ref[...], g, m_ref[...], v_ref[...])
        g_out[...] = g
        d_out[...] = d
        m_out[...] = mn
        v_out[...] = vn

    blk = pl.BlockSpec((tr, Cc), lambda i: (i, 0))
    return pl.pallas_call(
        body, name=name, grid=(R // tr,),
        in_specs=[pl.BlockSpec((N_CHIPS, tr, Cc), lambda i: (0, i, 0)), blk, blk, blk],
        out_specs=[blk] * 4, out_shape=[jax.ShapeDtypeStruct((R, Cc), F32)] * 4,
        compiler_params=pltpu.CompilerParams(dimension_semantics=("parallel",)),
    )(parts, w, m, v)


def _adamw_small(grads, ws, ms, vs):
    n = len(grads)
    vm = pl.BlockSpec(memory_space=pltpu.VMEM)

    def body(*refs):
        g_in, w_in, m_in, v_in = (refs[k * n:(k + 1) * n] for k in range(4))
        g_out, d_out, m_out, v_out = (refs[(4 + k) * n:(5 + k) * n] for k in range(4))
        for k in range(n):
            g = g_in[k][...]
            d, mn, vn = _adamw_math(w_in[k][...], g, m_in[k][...], v_in[k][...])
            g_out[k][...] = g
            d_out[k][...] = d
            m_out[k][...] = mn
            v_out[k][...] = vn

    shapes = [jax.ShapeDtypeStruct(g.shape, F32) for g in grads]
    return pl.pallas_call(
        body, name="adamw_small", out_shape=shapes * 4, in_specs=[vm] * (4 * n), out_specs=[vm] * (4 * n),
    )(*grads, *ws, *ms, *vs)


def _unshard_cols(g):
    n, R, Cc = g.shape
    return jnp.transpose(g, (1, 0, 2)).reshape(R, n * Cc)


def _shard_cols(full):
    R, Ct = full.shape
    return jnp.transpose(full.reshape(R, N_DEV, Ct // N_DEV), (1, 0, 2))


_COL_SHARDED = ("w_in", "w_gate", "w_up")
_BIG = ("w_in", "w_out", "w_gate", "w_up", "w_down")


def _rows(nm, p):
    return p[0].T if nm in _COL_SHARDED else p[0]


def _step(args, ts_mix_fwd, ts_ffn, ts_mix_bwd, tm_grad):
    (x, g_mix, w_in, b_in, w_dw, b_dw, ln_g, ln_b, w_pool, s_pool, w_out, g_ffn, w_gate, w_up, w_down, g_final,
     loss_target) = args[:17]
    names = ["g_mix", "w_in", "b_in", "w_dw", "b_dw", "ln_g", "ln_b", "w_pool", "s_pool", "w_out", "g_ffn",
             "w_gate", "w_up", "w_down", "g_final"]
    weights = dict(zip(names, args[1:16]))
    moms = dict(zip(names, args[17:32]))
    vars_ = dict(zip(names, args[32:47]))

    S, D = x.shape[1], x.shape[2]
    x2 = x.reshape(S, D)
    tgt2 = loss_target.reshape(S, D)

    bf = lambda nm: _rows(nm, weights[nm]).astype(BF16)
    g_in, g_out, g_dw = _all_gather([bf("w_in"), bf("w_out"), w_dw[0]])
    wt_in, w_out_f = g_in.reshape(-1, D), g_out.reshape(-1, D)
    w_dw_f = _unshard_cols(g_dw)
    w_pool_b = w_pool[0].astype(BF16)

    a, gate, v, m, y, h1, xn, g_gate, g_up, g_down = _mix_fwd(
        x2, g_mix, wt_in, b_in, w_dw_f, b_dw, ln_g, ln_b, w_pool_b, s_pool, w_out_f,
        [bf("w_gate"), bf("w_up"), bf("w_down")], ts_mix_fwd)
    wt_gate, wt_up, w_down_f = g_gate.reshape(-1, D), g_up.reshape(-1, D), g_down.reshape(-1, D)
    Fd = wt_gate.shape[0]
    f_chunks = [1024] * (Fd // 1024) + ([Fd % 1024] if Fd % 1024 else [])
    dh1, hn, act, dgt, dup, dh2, loss_p, dg_final, dg_ffn = _ffn(
        h1, tgt2, g_ffn, g_final.reshape(1, D), wt_gate, wt_up, w_down_f, ts_ffn, f_chunks)

    ffn_grads = [_grad_matmul(dgt, hn, tm_grad, "grad_w_gate"), _grad_matmul(dup, hn, tm_grad, "grad_w_up"),
                 _grad_matmul(act, dh2, tm_grad, "grad_w_down")]
    pair_sums = _pair_sum(*_pair_exchange([g.reshape(N_DEV, -1, D) for g in ffn_grads]))
    (dx, dz, dh1b, dg_mix, db_in, dw_dw, db_dw, dln_g, dln_b, dw_pool, ds_pool, r_gate, r_up, r_down) = _mix_bwd(
        dh1, x2, a, gate, v, m, g_mix, wt_in, w_dw_f, ln_g, ln_b, w_pool_b, s_pool, w_out_f, pair_sums, ts_mix_bwd)
    dwt_in = _grad_matmul(dz, xn, tm_grad, "grad_w_in")
    dw_out = _grad_matmul(y, dh1b, tm_grad, "grad_w_out")

    small_names = ["g_mix", "b_in", "b_dw", "ln_g", "ln_b", "w_pool", "s_pool", "g_ffn", "g_final"]
    small_shape = lambda p: p.reshape(-1, p.shape[-1])
    partial = [dg_mix, db_in, db_dw, dln_g, dln_b, dw_pool.reshape(-1, POOL_GROUP), ds_pool, dg_ffn, dg_final, loss_p]
    tail = _tail_reduce([dwt_in.reshape(N_DEV, -1, D), dw_out.reshape(N_DEV, -1, D),
                         _shard_cols(dw_dw[0:CONV_WIDTH])], partial)
    r_in, r_out, r_dw = tail[:3]
    summed = tail[3:]

    big = {}
    for nm, r in zip(_BIG, [r_in, r_out, r_gate, r_up, r_down]):
        rows = r.shape[1]
        res = _adamw_shard(r, _rows(nm, weights[nm]), _rows(nm, moms[nm]), _rows(nm, vars_[nm]),
                           rows // 2 if rows % 32 == 0 else rows, "adamw_" + nm)
        big[nm] = [o.T if nm in _COL_SHARDED else o for o in res]
    big["w_dw"] = _adamw_shard(r_dw, w_dw[0], moms["w_dw"][0], vars_["w_dw"][0], CONV_WIDTH, "adamw_w_dw")

    sm = _adamw_small(summed[:-1], [small_shape(weights[nm]) for nm in small_names],
                      [small_shape(moms[nm]) for nm in small_names], [small_shape(vars_[nm]) for nm in small_names])
    n_small = len(small_names)

    def result(kind, nm):
        if nm in big:
            return big[nm][kind].reshape(weights[nm].shape)
        return sm[kind * n_small + small_names.index(nm)].reshape(weights[nm].shape)

    loss = summed[-1][0, 0]
    out = [loss, dx.reshape(x.shape)]
    for kind in range(4):
        out += [result(kind, nm) for nm in names]
    return tuple(out)


def kernel(x, g_mix, w_in, b_in, w_dw, b_dw, ln_g, ln_b, w_pool, s_pool, w_out, g_ffn, w_gate, w_up, w_down, g_final, loss_target, m_g_mix, m_w_in, m_b_in, m_w_dw, m_b_dw, m_ln_g, m_ln_b, m_w_pool, m_s_pool, m_w_out, m_g_ffn, m_w_gate, m_w_up, m_w_down, m_g_final, v_g_mix, v_w_in, v_b_in, v_w_dw, v_b_dw, v_ln_g, v_ln_b, v_w_pool, v_s_pool, v_w_out, v_g_ffn, v_w_gate, v_w_up, v_w_down, v_g_final):
    args = (x, g_mix, w_in, b_in, w_dw, b_dw, ln_g, ln_b, w_pool, s_pool, w_out, g_ffn, w_gate, w_up, w_down, g_final, loss_target, m_g_mix, m_w_in, m_b_in, m_w_dw, m_b_dw, m_ln_g, m_ln_b, m_w_pool, m_s_pool, m_w_out, m_g_ffn, m_w_gate, m_w_up, m_w_down, m_g_final, v_g_mix, v_w_in, v_b_in, v_w_dw, v_b_dw, v_ln_g, v_ln_b, v_w_pool, v_s_pool, v_w_out, v_g_ffn, v_w_gate, v_w_up, v_w_down, v_g_final)
    return _step(args, ts_mix_fwd=512, ts_ffn=256, ts_mix_bwd=256, tm_grad=256)
```

```python
import functools

import jax
import jax.numpy as jnp
from jax import lax
from jax.experimental import pallas as pl
from jax.experimental.pallas import tpu as pltpu

F32 = jnp.float32
BF16 = jnp.bfloat16
MESH = pl.DeviceIdType.MESH
N_DEV = 8

C_CONV = 512
CONV_WIDTH = 31
POOL_WINDOWS = (2, 4, 8, 16)
POOL_GROUP = 128
RMS_EPS = 1e-6
LN_EPS = 1e-5

ADAM_LR = 0.001
ADAM_B1 = 0.9
ADAM_B2 = 0.999
ADAM_EPS = 1e-08
ADAM_WD = 0.01
ADAM_STEP = 10

HALO = 32
SUBLANES = 8
CONV_ROWS = 64
VMEM_LIMIT = 56 * 1024 * 1024


def _dot(a, b):
    return jnp.dot(a, b, preferred_element_type=F32)


def _dot_nt(a, b):
    return lax.dot_general(a, b, (((1,), (1,)), ((), ())), preferred_element_type=F32)


def _dot_tn(a, b):
    return lax.dot_general(a, b, (((0,), (0,)), ((), ())), preferred_element_type=F32)


def _mean_last(v):
    return jnp.mean(v, axis=-1, keepdims=True)


def _full(shape):
    nd = len(shape)
    return pl.BlockSpec(shape, lambda *_: (0,) * nd)


def _full1(shape):
    nd = len(shape)
    return pl.BlockSpec(shape, lambda *_: (0,) * nd, pipeline_mode=pl.Buffered(1))


def _shifted_copies(sh_ref, rows):
    for s in range(1, SUBLANES):
        sh_ref[s, 0:rows, :] = sh_ref[0, s:s + rows, :]


def _tap(sh_ref, off, r0, rows):
    q, s = divmod(off, SUBLANES)
    return sh_ref[s, pl.ds(r0 + q * SUBLANES, rows), :]


def _mix_fwd(x, g_mix, w_in, b_in, w_dw, b_dw, ln_g, ln_b, w_pool, s_pool, w_out, ag_shards, ts):
    S, D = x.shape
    d_in = w_in.shape[0]
    C = C_CONV
    nt = S // ts
    nrb = ts // CONV_ROWS
    n_ag = len(ag_shards)

    def body(x_ref, g_ref, win_ref, bin_ref, wdw_ref, bdw_ref, lng_ref, lnb_ref, wp_ref, sp_ref, wout_ref, *rest):
        ag_in, rest = rest[:n_ag], rest[n_ag:]
        a_ref, gate_ref, v_ref, m_ref, y_ref, h1_ref, xn_ref = rest[:7]
        ag_out, rest = rest[7:7 + n_ag], rest[7 + n_ag:]
        ush, pbuf = rest[:2]
        ag_bufs, (send_sems, recv_sems, local_sems) = rest[2:2 + n_ag], rest[2 + n_ag:]
        i = pl.program_id(0)
        ag_start, ag_finish = _gather_plan(ag_in, ag_out, ag_bufs, send_sems, recv_sems, local_sems)

        @pl.when(i == 0)
        def _():
            ag_start()
            ush[0, 0:HALO, :] = jnp.zeros((HALO, C), F32)
            pbuf[0:HALO, :] = jnp.zeros((HALO, C), F32)

        x = x_ref[...]
        r1 = lax.rsqrt(_mean_last(x * x) + RMS_EPS)
        xn = (x * r1 * g_ref[...]).astype(BF16)
        xn_ref[...] = xn
        z = _dot_nt(xn, win_ref[...]) + bin_ref[...]
        a = z[:, 0:C]
        gate = z[:, C:2 * C]
        a_ref[...] = a
        gate_ref[...] = gate
        ush[0, HALO:HALO + ts, :] = a * jax.nn.sigmoid(gate)
        pbuf[HALO:HALO + ts, :] = z[:, 2 * C:]

        _shifted_copies(ush, ts + HALO - SUBLANES)

        def conv_block(rb, carry):
            r0 = pl.multiple_of(rb * CONV_ROWS, CONV_ROWS)
            acc = jnp.zeros((CONV_ROWS, C), F32)
            for k in range(CONV_WIDTH):
                acc = acc + wdw_ref[k:k + 1, :] * _tap(ush, HALO - (CONV_WIDTH - 1) + k, r0, CONV_ROWS)
            v_ref[pl.ds(r0, CONV_ROWS), :] = acc + bdw_ref[...]
            return carry

        lax.fori_loop(0, nrb, conv_block, 0)

        v = v_ref[...]
        mu = _mean_last(v)
        xc = v - mu
        rstd = lax.rsqrt(_mean_last(xc * xc) + LN_EPS)
        ln = xc * rstd * lng_ref[...] + lnb_ref[...]
        y_ref[:, 0:C] = (ln * jax.nn.sigmoid(ln)).astype(BF16)

        row = lax.broadcasted_iota(jnp.int32, (ts, 1), 0) + i * ts
        for gi, w in enumerate(POOL_WINDOWS):
            lanes = slice(gi * POOL_GROUP, (gi + 1) * POOL_GROUP)
            seg = pbuf[HALO:HALO + ts, lanes]
            ws = seg
            for k in range(1, w):
                ws = ws + pbuf[HALO - k:HALO - k + ts, lanes]
            cnt = jnp.minimum(row + 1, w).astype(F32)
            m = (ws / cnt - seg).astype(BF16)
            m_ref[:, lanes] = m
            ypre = _dot(m, wp_ref[gi])
            y_ref[:, C + gi * POOL_GROUP:C + (gi + 1) * POOL_GROUP] = (ypre * sp_ref[:, lanes]).astype(BF16)

        h1_ref[...] = x + _dot(y_ref[...], wout_ref[...])

        ush[0, 0:HALO, :] = ush[0, ts:ts + HALO, :]
        pbuf[0:HALO, :] = pbuf[ts:ts + HALO, :]

        @pl.when(i == nt - 1)
        def _():
            ag_finish()

    tile = lambda w, dt: (pl.BlockSpec((ts, w), lambda i: (i, 0)), jax.ShapeDtypeStruct((S, w), dt))
    outs = [tile(C, F32), tile(C, F32), tile(C, F32), tile(C, BF16), tile(D, BF16), tile(D, F32), tile(D, BF16)]
    return pl.pallas_call(
        body, name="mix_fwd", grid=(nt,),
        in_specs=[pl.BlockSpec((ts, D), lambda i: (i, 0)), _full((1, D)), _full((d_in, D)), _full((1, d_in)),
                  _full(w_dw.shape), _full((1, C)), _full((1, C)), _full((1, C)), _full(w_pool.shape),
                  _full((1, C)), _full((D, D))] + [_HBM] * n_ag,
        out_specs=[o[0] for o in outs] + [_HBM] * n_ag,
        out_shape=[o[1] for o in outs] + _gather_out_shapes(ag_shards),
        scratch_shapes=[pltpu.VMEM((SUBLANES, ts + HALO, C), F32), pltpu.VMEM((ts + HALO, C), F32)]
        + _bounce_bufs(ag_shards) + _comm_sems(n_ag, 7),
        compiler_params=pltpu.CompilerParams(dimension_semantics=("arbitrary",), vmem_limit_bytes=VMEM_LIMIT),
    )(x, g_mix, w_in, b_in, w_dw, b_dw, ln_g, ln_b, w_pool, s_pool, w_out, *ag_shards)


def _ffn(h1, target, g_ffn, g_final, w_gate, w_up, w_down, ts, f_chunks):
    S, D = h1.shape
    Fd = w_gate.shape[0]
    nt = S // ts
    bounds = []
    lo = 0
    for n in f_chunks:
        bounds.append((lo, lo + n))
        lo += n
    assert lo == Fd

    def body(h1_ref, tgt_ref, gf_ref, gl_ref, wg_ref, wu_ref, wd_ref,
             dh1_ref, hn_ref, act_ref, dgt_ref, dup_ref, dh2_ref, loss_ref, dgl_ref, dgf_ref, gt_s, up_s):
        i = pl.program_id(0)

        @pl.when(i == 0)
        def _():
            loss_ref[...] = jnp.zeros_like(loss_ref)
            dgl_ref[...] = jnp.zeros_like(dgl_ref)
            dgf_ref[...] = jnp.zeros_like(dgf_ref)

        h1 = h1_ref[...]
        r2 = lax.rsqrt(_mean_last(h1 * h1) + RMS_EPS)
        hhat = h1 * r2
        hn = (hhat * gf_ref[...]).astype(BF16)
        hn_ref[...] = hn
        h2 = h1
        for lo, hi in bounds:
            gt = _dot_nt(hn, wg_ref[lo:hi, :])
            up = _dot_nt(hn, wu_ref[lo:hi, :])
            gt_s[:, lo:hi] = gt
            up_s[:, lo:hi] = up
            act = (gt * jax.nn.sigmoid(gt) * up).astype(BF16)
            act_ref[:, lo:hi] = act
            h2 = h2 + _dot(act, wd_ref[lo:hi, :])

        r3 = lax.rsqrt(_mean_last(h2 * h2) + RMS_EPS)
        n3 = h2 * r3
        gl = gl_ref[...]
        diff = n3 * gl - tgt_ref[...]
        loss_ref[...] += jnp.sum(0.5 * jnp.sum(diff * diff, axis=-1, keepdims=True) / D, axis=0, keepdims=True)
        dout = diff / D
        dgl_ref[...] += jnp.sum(dout * n3, axis=0, keepdims=True)
        dn = dout * gl
        dh2 = r3 * (dn - n3 * _mean_last(dn * n3))
        dh2b = dh2.astype(BF16)
        dh2_ref[...] = dh2b

        dhn = jnp.zeros((ts, D), F32)
        for lo, hi in bounds:
            gt = gt_s[:, lo:hi]
            up = up_s[:, lo:hi]
            sg = jax.nn.sigmoid(gt)
            dact = _dot_nt(dh2b, wd_ref[lo:hi, :])
            dgt = (dact * up * (sg * (1.0 + gt * (1.0 - sg)))).astype(BF16)
            dup = (dact * (gt * sg)).astype(BF16)
            dgt_ref[:, lo:hi] = dgt
            dup_ref[:, lo:hi] = dup
            dhn = dhn + _dot(dgt, wg_ref[lo:hi, :]) + _dot(dup, wu_ref[lo:hi, :])

        dgf_ref[...] += jnp.sum(dhn * hhat, axis=0, keepdims=True)
        dnn = dhn * gf_ref[...]
        dh1_ref[...] = dh2 + r2 * (dnn - hhat * _mean_last(dnn * hhat))

    tile = lambda w, dt: (pl.BlockSpec((ts, w), lambda i: (i, 0)), jax.ShapeDtypeStruct((S, w), dt))
    acc = lambda w: (_full((1, w)), jax.ShapeDtypeStruct((1, w), F32))
    outs = [tile(D, F32), tile(D, BF16), tile(Fd, BF16), tile(Fd, BF16), tile(Fd, BF16), tile(D, BF16),
            acc(128), acc(D), acc(D)]
    return pl.pallas_call(
        body, name="ffn_fwd_bwd", grid=(nt,),
        in_specs=[pl.BlockSpec((ts, D), lambda i: (i, 0)), pl.BlockSpec((ts, D), lambda i: (i, 0)),
                  _full((1, D)), _full((1, D)), _full1((Fd, D)), _full1((Fd, D)), _full1((Fd, D))],
        out_specs=[o[0] for o in outs], out_shape=[o[1] for o in outs],
        scratch_shapes=[pltpu.VMEM((ts, Fd), F32), pltpu.VMEM((ts, Fd), F32)],
        compiler_params=pltpu.CompilerParams(dimension_semantics=("arbitrary",), vmem_limit_bytes=VMEM_LIMIT),
    )(h1, target, g_ffn, g_final, w_gate, w_up, w_down)


def _mix_bwd(dh1, x, a, gate, v, m, g_mix, w_in, w_dw, ln_g, ln_b, w_pool, s_pool, w_out, rs_parts, ts):
    S, D = x.shape
    n_rs = len(rs_parts)
    d_in = w_in.shape[0]
    C = C_CONV
    nt = S // ts
    nrb = ts // CONV_ROWS
    hb = ts // HALO
    wrows = ((CONV_WIDTH + SUBLANES - 1) // SUBLANES) * SUBLANES

    def body(dh1_ref, x_ref, a_ref, gate_ref, ah_ref, gh_ref, v_ref, m_ref, g_ref, win_ref, wdw_ref, lng_ref,
             lnb_ref, wp_ref, sp_ref, wout_ref, *rest):
        rs_in, rest = rest[:n_rs], rest[n_rs:]
        (dx_ref, dz_ref, dh1b_ref, dgm_ref, dbin_ref, dwdw_ref, dbdw_ref, dlng_ref, dlnb_ref, dwp_ref,
         dsp_ref) = rest[:11]
        rs_out, rest = rest[11:11 + n_rs], rest[11 + n_rs:]
        ush, dvsh, dqbuf, du_s = rest[:4]
        rs_bufs, (send_sems, recv_sems, local_sems) = rest[4:4 + n_rs], rest[4 + n_rs:]
        i = pl.program_id(0)
        t = nt - 1 - i
        rs_start, rs_finish = _chip_exchange_plan(rs_in, rs_out, rs_bufs, send_sems, recv_sems, local_sems)

        @pl.when(i == 0)
        def _():
            rs_start()
            dvsh[0, ts:ts + HALO, :] = jnp.zeros((HALO, C), F32)
            dqbuf[ts:ts + HALO, :] = jnp.zeros((HALO, C), F32)
            for r in (dgm_ref, dbin_ref, dwdw_ref, dbdw_ref, dlng_ref, dlnb_ref, dwp_ref, dsp_ref):
                r[...] = jnp.zeros_like(r)

        dh1 = dh1_ref[...]
        dh1b = dh1.astype(BF16)
        dh1b_ref[...] = dh1b
        dy = _dot_nt(dh1b, wout_ref[...])

        v = v_ref[...]
        mu = _mean_last(v)
        xc = v - mu
        rstd = lax.rsqrt(_mean_last(xc * xc) + LN_EPS)
        vhat = xc * rstd
        lng = lng_ref[...]
        ln = vhat * lng + lnb_ref[...]
        sg = jax.nn.sigmoid(ln)
        dln = dy[:, 0:C] * (sg * (1.0 + ln * (1.0 - sg)))
        dlng_ref[...] += jnp.sum(dln * vhat, axis=0, keepdims=True)
        dlnb_ref[...] += jnp.sum(dln, axis=0, keepdims=True)
        dvh = dln * lng
        dv = rstd * (dvh - _mean_last(dvh) - vhat * _mean_last(dvh * vhat))
        dbdw_ref[...] += jnp.sum(dv, axis=0, keepdims=True)
        dvsh[0, 0:ts, :] = dv
        _shifted_copies(dvsh, ts + HALO - SUBLANES)

        a = a_ref[...]
        sgate = jax.nn.sigmoid(gate_ref[...])
        ush[0, HALO:HALO + ts, :] = a * sgate
        uh = ah_ref[...] * jax.nn.sigmoid(gh_ref[...])
        ush[0, 0:HALO, :] = jnp.where(t > 0, uh, 0.0)
        _shifted_copies(ush, ts + HALO - SUBLANES)

        def conv_block(rb, carry):
            r0 = pl.multiple_of(rb * CONV_ROWS, CONV_ROWS)
            acc = jnp.zeros((CONV_ROWS, C), F32)
            for k in range(CONV_WIDTH):
                acc = acc + wdw_ref[k:k + 1, :] * _tap(dvsh, CONV_WIDTH - 1 - k, r0, CONV_ROWS)
            du_s[pl.ds(r0, CONV_ROWS), :] = acc
            return carry

        lax.fori_loop(0, nrb, conv_block, 0)

        for k in range(CONV_WIDTH):
            q, s = divmod(HALO - (CONV_WIDTH - 1) + k, SUBLANES)
            prod = dvsh[0, 0:ts, :] * ush[s, q * SUBLANES:q * SUBLANES + ts, :]
            dwdw_ref[k:k + 1, :] += jnp.sum(prod, axis=0, keepdims=True)

        du = du_s[...]
        da = du * sgate
        dgate = du * a * sgate * (1.0 - sgate)
        dz_ref[:, 0:C] = da.astype(BF16)
        dz_ref[:, C:2 * C] = dgate.astype(BF16)
        dbin_ref[:, 0:C] += jnp.sum(da, axis=0, keepdims=True)
        dbin_ref[:, C:2 * C] += jnp.sum(dgate, axis=0, keepdims=True)

        row = lax.broadcasted_iota(jnp.int32, (ts, 1), 0) + t * ts
        for gi, w in enumerate(POOL_WINDOWS):
            lanes = slice(gi * POOL_GROUP, (gi + 1) * POOL_GROUP)
            dyp = dy[:, C + gi * POOL_GROUP:C + (gi + 1) * POOL_GROUP]
            mg = m_ref[:, lanes]
            ypre = _dot(mg, wp_ref[gi])
            dsp_ref[:, lanes] += jnp.sum(dyp * ypre, axis=0, keepdims=True)
            dyi = (dyp * sp_ref[:, lanes]).astype(BF16)
            dwp_ref[gi] += _dot_tn(mg, dyi)
            dm = _dot_nt(dyi, wp_ref[gi])
            cnt = jnp.minimum(row + 1, w).astype(F32)
            dqbuf[0:ts, lanes] = dm / cnt
            dp = -dm
            for k in range(w):
                dp = dp + dqbuf[k:k + ts, lanes]
            dz_ref[:, 2 * C + gi * POOL_GROUP:2 * C + (gi + 1) * POOL_GROUP] = dp.astype(BF16)
            dbin_ref[:, 2 * C + gi * POOL_GROUP:2 * C + (gi + 1) * POOL_GROUP] += jnp.sum(dp, axis=0, keepdims=True)

        dxn = _dot(dz_ref[...], win_ref[...])
        x = x_ref[...]
        r1 = lax.rsqrt(_mean_last(x * x) + RMS_EPS)
        xhat = x * r1
        dgm_ref[...] += jnp.sum(dxn * xhat, axis=0, keepdims=True)
        dnn = dxn * g_ref[...]
        dx_ref[...] = dh1 + r1 * (dnn - xhat * _mean_last(dnn * xhat))

        dvsh[0, ts:ts + HALO, :] = dvsh[0, 0:HALO, :]
        dqbuf[ts:ts + HALO, :] = dqbuf[0:HALO, :]

        @pl.when(i == nt - 1)
        def _():
            rs_finish()

    rev = lambda w: pl.BlockSpec((ts, w), lambda i: (nt - 1 - i, 0))
    halo = pl.BlockSpec((HALO, C), lambda i: (jnp.maximum((nt - 1 - i) * hb - 1, 0), 0))
    acc = lambda shape: (_full(shape), jax.ShapeDtypeStruct(shape, F32))
    outs = [(rev(D), jax.ShapeDtypeStruct((S, D), F32)), (rev(d_in), jax.ShapeDtypeStruct((S, d_in), BF16)),
            (rev(D), jax.ShapeDtypeStruct((S, D), BF16)), acc((1, D)), acc((1, d_in)), acc((wrows, C)),
            acc((1, C)), acc((1, C)), acc((1, C)), acc(w_pool.shape), acc((1, C))]
    return pl.pallas_call(
        body, name="mix_bwd", grid=(nt,),
        in_specs=[rev(D), rev(D), rev(C), rev(C), halo, halo, rev(C), rev(C), _full((1, D)), _full((d_in, D)),
                  _full(w_dw.shape), _full((1, C)), _full((1, C)), _full(w_pool.shape), _full((1, C)),
                  _full((D, D))] + [_HBM] * n_rs,
        out_specs=[o[0] for o in outs] + [_HBM] * n_rs,
        out_shape=[o[1] for o in outs] + [jax.ShapeDtypeStruct(p.shape, p.dtype) for p in rs_parts],
        scratch_shapes=[pltpu.VMEM((SUBLANES, ts + HALO, C), F32), pltpu.VMEM((SUBLANES, ts + HALO, C), F32),
                        pltpu.VMEM((ts + HALO, C), F32), pltpu.VMEM((ts, C), F32)]
        + [pltpu.VMEM(p.shape[1:], p.dtype) for p in rs_parts] + _comm_sems(n_rs, 3),
        compiler_params=pltpu.CompilerParams(dimension_semantics=("arbitrary",), vmem_limit_bytes=VMEM_LIMIT),
    )(dh1, x, a, gate, a, gate, v, m, g_mix, w_in, w_dw, ln_g, ln_b, w_pool, s_pool, w_out, *rs_parts)


def _grad_matmul(a, b, tm, name):
    S, M = a.shape
    N = b.shape[1]

    def body(a_ref, b_ref, o_ref):
        o_ref[...] = _dot_tn(a_ref[...], b_ref[...]).astype(BF16)

    return pl.pallas_call(
        body, name=name, grid=(M // tm,),
        in_specs=[pl.BlockSpec((S, tm), lambda i: (0, i)), _full1((S, N))],
        out_specs=pl.BlockSpec((tm, N), lambda i: (i, 0)),
        out_shape=jax.ShapeDtypeStruct((M, N), BF16),
        compiler_params=pltpu.CompilerParams(dimension_semantics=("parallel",), vmem_limit_bytes=VMEM_LIMIT),
    )(a, b)


def _position():
    return lax.axis_index("x"), lax.axis_index("y"), lax.axis_index("c")


def _slot(px, py, pc):
    return 4 * px + 2 * py + pc


def _all_gather(shards):
    n = len(shards)

    def body(*refs):
        start, finish = _gather_plan(refs[:n], refs[n:2 * n], refs[2 * n:3 * n], *refs[3 * n:])
        start()
        finish()

    return pl.pallas_call(
        body, name="all_gather_mix_weights",
        out_shape=_gather_out_shapes(shards), in_specs=[_HBM] * n, out_specs=[_HBM] * n,
        scratch_shapes=_bounce_bufs(shards) + _comm_sems(n, 7),
    )(*shards)


_HBM = pl.BlockSpec(memory_space=pl.ANY)


def _comm_sems(n, copies):
    return [pltpu.SemaphoreType.DMA((n, copies)), pltpu.SemaphoreType.DMA((n, copies)),
            pltpu.SemaphoreType.DMA((n, 2))]


def _bounce_bufs(blocks):
    return [pltpu.VMEM(b.shape, b.dtype) for b in blocks]


def _local_copy(srcs, dsts, bufs, local_sems):
    n = len(srcs)
    loads = [pltpu.make_async_copy(srcs[k], bufs[k], local_sems.at[k, 0]) for k in range(n)]
    for cp in loads:
        cp.start()
    for cp in loads:
        cp.wait()
    stores = _local_stores(dsts, bufs, local_sems)
    for cp in stores:
        cp.start()
    return stores


def _local_stores(dsts, bufs, local_sems):
    return [pltpu.make_async_copy(bufs[k], dsts[k], local_sems.at[k, 1]) for k in range(len(dsts))]


def _gather_out_shapes(shards):
    return [jax.ShapeDtypeStruct((N_DEV,) + s.shape, s.dtype) for s in shards]


def _gather_plan(ins, outs, bufs, send_sems, recv_sems, local_sems):
    n = len(ins)
    x, y, c = _position()
    me, sibling = (x, y, c), (x, y, 1 - c)
    chips = [(1 - x, y), (x, 1 - y), (1 - x, 1 - y)]
    own = [outs[k].at[_slot(*me)] for k in range(n)]

    def copy(k, sem, block, to, src=None):
        dst = outs[k].at[_slot(*block)]
        return pltpu.make_async_remote_copy(
            src_ref=dst if src is None else src, dst_ref=dst, send_sem=send_sems.at[k, sem],
            recv_sem=recv_sems.at[k, sem], device_id=to, device_id_type=MESH)

    def first():
        cps = []
        for k in range(n):
            cps.append(copy(k, 0, me, sibling, src=ins[k]))
            cps += [copy(k, 1 + j, me, (*chip, c), src=ins[k]) for j, chip in enumerate(chips)]
        return cps

    def start():
        for cp in first():
            cp.start()
        _local_copy(ins, own, bufs, local_sems)

    def finish():
        passed = []
        for j, chip in enumerate(chips):
            for k in range(n):
                copy(k, 1 + j, (*chip, c), me).wait_recv()
                fwd = copy(k, 4 + j, (*chip, c), sibling)
                fwd.start()
                passed.append(fwd)
        for k in range(n):
            copy(k, 0, sibling, me).wait_recv()
            for j, chip in enumerate(chips):
                copy(k, 4 + j, (*chip, 1 - c), me).wait_recv()
        for cp in first() + passed:
            cp.wait_send()
        for cp in _local_stores(own, bufs, local_sems):
            cp.wait()

    return start, finish


N_CHIPS = 4
_CHIP_FLIPS = [(1, 0), (0, 1), (1, 1)]


def _chip_exchange_plan(ins, outs, bufs, send_sems, recv_sems, local_sems):
    n = len(ins)
    x, y, c = _position()
    my_q = 2 * x + y
    peers = [(x ^ fx, y ^ fy) for fx, fy in _CHIP_FLIPS]
    own = [outs[k].at[my_q] for k in range(n)]

    def sends():
        return [pltpu.make_async_remote_copy(
            src_ref=ins[k].at[2 * px + py], dst_ref=outs[k].at[my_q], send_sem=send_sems.at[k, f],
            recv_sem=recv_sems.at[k, f], device_id=(px, py, c), device_id_type=MESH)
            for f, (px, py) in enumerate(peers) for k in range(n)]

    def start():
        for cp in sends():
            cp.start()
        _local_copy([ins[k].at[my_q] for k in range(n)], own, bufs, local_sems)

    def finish():
        for f, (px, py) in enumerate(peers):
            for k in range(n):
                pltpu.make_async_remote_copy(
                    src_ref=ins[k].at[my_q], dst_ref=outs[k].at[2 * px + py], send_sem=send_sems.at[k, f],
                    recv_sem=recv_sems.at[k, f], device_id=(px, py, c), device_id_type=MESH).wait_recv()
        for cp in sends():
            cp.wait_send()
        for cp in _local_stores(own, bufs, local_sems):
            cp.wait()

    return start, finish


def _pair_exchange(parts):
    n = len(parts)

    def body(*refs):
        ins, sib = refs[:n], refs[n:2 * n]
        send_sems, recv_sems = refs[2 * n:]
        x, y, c = _position()
        copies = [pltpu.make_async_remote_copy(
            src_ref=ins[k].at[2 * q + 1 - c], dst_ref=sib[k].at[q], send_sem=send_sems.at[k, q],
            recv_sem=recv_sems.at[k, q], device_id=(x, y, 1 - c), device_id_type=MESH)
            for k in range(n) for q in range(N_CHIPS)]
        for cp in copies:
            cp.start()
        for cp in copies:
            cp.wait()

    sems = pltpu.SemaphoreType.DMA((n, N_CHIPS))
    return pl.pallas_call(
        body, name="pair_exchange",
        out_shape=[jax.ShapeDtypeStruct((N_CHIPS,) + p.shape[1:], p.dtype) for p in parts],
        in_specs=[_HBM] * n, out_specs=[_HBM] * n, scratch_shapes=[sems, sems],
    )(*parts)


def _pair_sum(parts, sib):
    n = len(parts)

    def body(*refs):
        c = lax.axis_index("c")
        for k in range(n):
            refs[2 * n + k][0] = (refs[k][0, c].astype(F32) + refs[n + k][0].astype(F32)).astype(BF16)

    pair = [pl.BlockSpec((1, 2) + p.shape[1:], lambda q: (q, 0, 0, 0)) for p in parts]
    one = [pl.BlockSpec((1,) + p.shape[1:], lambda q: (q, 0, 0)) for p in parts]
    return pl.pallas_call(
        body, name="pair_sum", grid=(N_CHIPS,), in_specs=pair + one, out_specs=one,
        out_shape=[jax.ShapeDtypeStruct(s.shape, BF16) for s in sib],
        compiler_params=pltpu.CompilerParams(dimension_semantics=("parallel",)),
    )(*[p.reshape((N_CHIPS, 2) + p.shape[1:]) for p in parts], *sib)


def _tail_reduce(parts, vals):
    n, ns = len(parts), len(vals)
    vm = pl.BlockSpec(memory_space=pltpu.VMEM)

    def body(*refs):
        p_in, v_in = refs[:n], refs[n:n + ns]
        p_out, v_out = refs[n + ns:2 * n + ns], refs[2 * n + ns:2 * (n + ns)]
        rest = refs[2 * (n + ns):]
        p_sib, p_sum = rest[:n], rest[n:2 * n]
        v_sib, v_sum, v_all = rest[2 * n:2 * n + ns], rest[2 * n + ns:2 * n + 2 * ns], rest[2 * n + 2 * ns:2 * n + 3 * ns]
        p1_send, p1_recv, v1_send, v1_recv, p3_send, p3_recv, v3_send, v3_recv = rest[2 * n + 3 * ns:]
        x, y, c = _position()
        my_q = 2 * x + y
        sibling = (x, y, 1 - c)
        peers = [(x ^ fx, y ^ fy) for fx, fy in _CHIP_FLIPS]

        def to_sibling(src, dst, send, recv):
            return pltpu.make_async_remote_copy(src_ref=src, dst_ref=dst, send_sem=send, recv_sem=recv,
                                                device_id=sibling, device_id_type=MESH)

        level1 = [to_sibling(p_in[k].at[2 * q + 1 - c], p_sib[k].at[q], p1_send.at[k, q], p1_recv.at[k, q])
                  for k in range(n) for q in range(N_CHIPS)]
        level1 += [to_sibling(v_in[k], v_sib[k], v1_send.at[k], v1_recv.at[k]) for k in range(ns)]
        for cp in level1:
            cp.start()
        for cp in level1:
            cp.wait_recv()
        for k in range(n):
            for q in range(N_CHIPS):
                p_sum[k][q] = (p_in[k][2 * q + c].astype(F32) + p_sib[k][q].astype(F32)).astype(p_sum[k].dtype)
        for k in range(ns):
            v_sum[k][...] = v_in[k][...] + v_sib[k][...]

        def to_chip(f, src, dst, send, recv):
            px, py = peers[f]
            return pltpu.make_async_remote_copy(src_ref=src, dst_ref=dst, send_sem=send, recv_sem=recv,
                                                device_id=(px, py, c), device_id_type=MESH)

        def level2(sending):
            cps = []
            for f, (px, py) in enumerate(peers):
                their_q = 2 * px + py
                for k in range(n):
                    src, dst = (p_sum[k].at[their_q], p_out[k].at[my_q]) if sending else (
                        p_sum[k].at[my_q], p_out[k].at[their_q])
                    cps.append(to_chip(f, src, dst, p3_send.at[k, f], p3_recv.at[k, f]))
                for k in range(ns):
                    dst = v_all[k].at[my_q] if sending else v_all[k].at[their_q]
                    cps.append(to_chip(f, v_sum[k], dst, v3_send.at[k, f], v3_recv.at[k, f]))
            return cps

        for cp in level2(True):
            cp.start()
        for k in range(n):
            p_out[k][my_q] = p_sum[k][my_q]
        for k in range(ns):
            v_all[k][my_q] = v_sum[k][...]
        for cp in level2(False):
            cp.wait_recv()
        for k in range(ns):
            total = v_all[k][0]
            for q in range(1, N_CHIPS):
                total = total + v_all[k][q]
            v_out[k][...] = total
        for cp in level1 + level2(True):
            cp.wait_send()

    quarter = [(N_CHIPS,) + p.shape[1:] for p in parts]
    dma = pltpu.SemaphoreType.DMA
    return pl.pallas_call(
        body, name="tail_reduce",
        out_shape=[jax.ShapeDtypeStruct(s, p.dtype) for s, p in zip(quarter, parts)]
        + [jax.ShapeDtypeStruct(v.shape, v.dtype) for v in vals],
        in_specs=[vm] * (n + ns), out_specs=[vm] * (n + ns),
        scratch_shapes=[pltpu.VMEM(s, p.dtype) for s, p in zip(quarter, parts)] * 2
        + [pltpu.VMEM(v.shape, v.dtype) for v in vals] * 2
        + [pltpu.VMEM((N_CHIPS,) + v.shape, v.dtype) for v in vals]
        + [dma((n, N_CHIPS)), dma((n, N_CHIPS)), dma((ns,)), dma((ns,)), dma((n, 3)), dma((n, 3)), dma((ns, 3)),
           dma((ns, 3))],
        compiler_params=pltpu.CompilerParams(vmem_limit_bytes=VMEM_LIMIT),
    )(*parts, *vals)


def _adamw_math(w, g, m, v):
    m = ADAM_B1 * m + (1.0 - ADAM_B1) * g
    v = ADAM_B2 * v + (1.0 - ADAM_B2) * (g * g)
    m_hat = m / (1.0 - ADAM_B1 ** ADAM_STEP)
    v_hat = v / (1.0 - ADAM_B2 ** ADAM_STEP)
    delta = -ADAM_LR * (m_hat / (jnp.sqrt(v_hat) + ADAM_EPS) + ADAM_WD * w)
    return delta, m, v


def _adamw_shard(parts, w, m, v, tr, name):
    R, Cc = w.shape

    def body(p_ref, w_ref, m_ref, v_ref, g_out, d_out, m_out, v_out):
        g = p_ref[0].astype(F32)
        for j in range(1, N_CHIPS):
            g = g + p_ref[j].astype(F32)
        d, mn, vn = _adamw_math(w_ref[...], g, m_ref[...], v_ref[...])
        g_out[...] = g
        d_out[...] = d
        m_out[...] = mn
        v_out[...] = vn

    blk = pl.BlockSpec((tr, Cc), lambda i: (i, 0))
    return pl.pallas_call(
        body, name=name, grid=(R // tr,),
        in_specs=[pl.BlockSpec((N_CHIPS, tr, Cc), lambda i: (0, i, 0)), blk, blk, blk],
        out_specs=[blk] * 4, out_shape=[jax.ShapeDtypeStruct((R, Cc), F32)] * 4,
        compiler_params=pltpu.CompilerParams(dimension_semantics=("parallel",)),
    )(parts, w, m, v)


def _adamw_small(grads, ws, ms, vs):
    n = len(grads)
    vm = pl.BlockSpec(memory_space=pltpu.VMEM)

    def body(*refs):
        g_in, w_in, m_in, v_in = (refs[k * n:(k + 1) * n] for k in range(4))
        g_out, d_out, m_out, v_out = (refs[(4 + k) * n:(5 + k) * n] for k in range(4))
        for k in range(n):
            g = g_in[k][...]
            d, mn, vn = _adamw_math(w_in[k][...], g, m_in[k][...], v_in[k][...])
            g_out[k][...] = g
            d_out[k][...] = d
            m_out[k][...] = mn
            v_out[k][...] = vn

    shapes = [jax.ShapeDtypeStruct(g.shape, F32) for g in grads]
    return pl.pallas_call(
        body, name="adamw_small", out_shape=shapes * 4, in_specs=[vm] * (4 * n), out_specs=[vm] * (4 * n),
    )(*grads, *ws, *ms, *vs)


def _unshard_cols(g):
    n, R, Cc = g.shape
    return jnp.transpose(g, (1, 0, 2)).reshape(R, n * Cc)


def _shard_cols(full):
    R, Ct = full.shape
    return jnp.transpose(full.reshape(R, N_DEV, Ct // N_DEV), (1, 0, 2))


_COL_SHARDED = ("w_in", "w_gate", "w_up")
_BIG = ("w_in", "w_out", "w_gate", "w_up", "w_down")


def _rows(nm, p):
    return p[0].T if nm in _COL_SHARDED else p[0]


def _step(args, ts_mix_fwd, ts_ffn, ts_mix_bwd, tm_grad):
    (x, g_mix, w_in, b_in, w_dw, b_dw, ln_g, ln_b, w_pool, s_pool, w_out, g_ffn, w_gate, w_up, w_down, g_final,
     loss_target) = args[:17]
    names = ["g_mix", "w_in", "b_in", "w_dw", "b_dw", "ln_g", "ln_b", "w_pool", "s_pool", "w_out", "g_ffn",
             "w_gate", "w_up", "w_down", "g_final"]
    weights = dict(zip(names, args[1:16]))
    moms = dict(zip(names, args[17:32]))
    vars_ = dict(zip(names, args[32:47]))

    S, D = x.shape[1], x.shape[2]
    x2 = x.reshape(S, D)
    tgt2 = loss_target.reshape(S, D)

    bf = lambda nm: _rows(nm, weights[nm]).astype(BF16)
    g_in, g_out, g_dw = _all_gather([bf("w_in"), bf("w_out"), w_dw[0]])
    wt_in, w_out_f = g_in.reshape(-1, D), g_out.reshape(-1, D)
    w_dw_f = _unshard_cols(g_dw)
    w_pool_b = w_pool[0].astype(BF16)

    a, gate, v, m, y, h1, xn, g_gate, g_up, g_down = _mix_fwd(
        x2, g_mix, wt_in, b_in, w_dw_f, b_dw, ln_g, ln_b, w_pool_b, s_pool, w_out_f,
        [bf("w_gate"), bf("w_up"), bf("w_down")], ts_mix_fwd)
    wt_gate, wt_up, w_down_f = g_gate.reshape(-1, D), g_up.reshape(-1, D), g_down.reshape(-1, D)
    Fd = wt_gate.shape[0]
    f_chunks = [1024] * (Fd // 1024) + ([Fd % 1024] if Fd % 1024 else [])
    dh1, hn, act, dgt, dup, dh2, loss_p, dg_final, dg_ffn = _ffn(
        h1, tgt2, g_ffn, g_final.reshape(1, D), wt_gate, wt_up, w_down_f, ts_ffn, f_chunks)

    ffn_grads = [_grad_matmul(dgt, hn, tm_grad, "grad_w_gate"), _grad_matmul(dup, hn, tm_grad, "grad_w_up"),
                 _grad_matmul(act, dh2, tm_grad, "grad_w_down")]
    ffn_parts = [g.reshape(N_DEV, -1, D) for g in ffn_grads]
    pair_sums = _pair_sum(ffn_parts, _pair_exchange(ffn_parts))
    (dx, dz, dh1b, dg_mix, db_in, dw_dw, db_dw, dln_g, dln_b, dw_pool, ds_pool, r_gate, r_up, r_down) = _mix_bwd(
        dh1, x2, a, gate, v, m, g_mix, wt_in, w_dw_f, ln_g, ln_b, w_pool_b, s_pool, w_out_f, pair_sums, ts_mix_bwd)
    dwt_in = _grad_matmul(dz, xn, tm_grad, "grad_w_in")
    dw_out = _grad_matmul(y, dh1b, tm_grad, "grad_w_out")

    small_names = ["g_mix", "b_in", "b_dw", "ln_g", "ln_b", "w_pool", "s_pool", "g_ffn", "g_final"]
    small_shape = lambda p: p.reshape(-1, p.shape[-1])
    partial = [dg_mix, db_in, db_dw, dln_g, dln_b, dw_pool.reshape(-1, POOL_GROUP), ds_pool, dg_ffn, dg_final, loss_p]
    tail = _tail_reduce([dwt_in.reshape(N_DEV, -1, D), dw_out.reshape(N_DEV, -1, D),
                         _shard_cols(dw_dw[0:CONV_WIDTH])], partial)
    r_in, r_out, r_dw = tail[:3]
    summed = tail[3:]

    big = {}
    for nm, r in zip(_BIG, [r_in, r_out, r_gate, r_up, r_down]):
        rows = r.shape[1]
        res = _adamw_shard(r, _rows(nm, weights[nm]), _rows(nm, moms[nm]), _rows(nm, vars_[nm]),
                           rows // 2 if rows % 32 == 0 else rows, "adamw_" + nm)
        big[nm] = [o.T if nm in _COL_SHARDED else o for o in res]
    big["w_dw"] = _adamw_shard(r_dw, w_dw[0], moms["w_dw"][0], vars_["w_dw"][0], CONV_WIDTH, "adamw_w_dw")

    sm = _adamw_small(summed[:-1], [small_shape(weights[nm]) for nm in small_names],
                      [small_shape(moms[nm]) for nm in small_names], [small_shape(vars_[nm]) for nm in small_names])
    n_small = len(small_names)

    def result(kind, nm):
        if nm in big:
            return big[nm][kind].reshape(weights[nm].shape)
        return sm[kind * n_small + small_names.index(nm)].reshape(weights[nm].shape)

    loss = summed[-1][0, 0]
    out = [loss, dx.reshape(x.shape)]
    for kind in range(4):
        out += [result(kind, nm) for nm in names]
    return tuple(out)


def kernel(x, g_mix, w_in, b_in, w_dw, b_dw, ln_g, ln_b, w_pool, s_pool, w_out, g_ffn, w_gate, w_up, w_down, g_final, loss_target, m_g_mix, m_w_in, m_b_in, m_w_dw, m_b_dw, m_ln_g, m_ln_b, m_w_pool, m_s_pool, m_w_out, m_g_ffn, m_w_gate, m_w_up, m_w_down, m_g_final, v_g_mix, v_w_in, v_b_in, v_w_dw, v_b_dw, v_ln_g, v_ln_b, v_w_pool, v_s_pool, v_w_out, v_g_ffn, v_w_gate, v_w_up, v_w_down, v_g_final):
    args = (x, g_mix, w_in, b_in, w_dw, b_dw, ln_g, ln_b, w_pool, s_pool, w_out, g_ffn, w_gate, w_up, w_down, g_final, loss_target, m_g_mix, m_w_in, m_b_in, m_w_dw, m_b_dw, m_ln_g, m_ln_b, m_w_pool, m_s_pool, m_w_out, m_g_ffn, m_w_gate, m_w_up, m_w_down, m_g_final, v_g_mix, v_w_in, v_b_in, v_w_dw, v_b_dw, v_ln_g, v_ln_b, v_w_pool, v_s_pool, v_w_out, v_g_ffn, v_w_gate, v_w_up, v_w_down, v_g_final)
    return _step(args, ts_mix_fwd=512, ts_ffn=256, ts_mix_bwd=256, tm_grad=256)
```

```python
import functools

import jax
import jax.numpy as jnp
from jax import lax
from jax.experimental import pallas as pl
from jax.experimental.pallas import tpu as pltpu

F32 = jnp.float32
BF16 = jnp.bfloat16
MESH = pl.DeviceIdType.MESH
N_DEV = 8

C_CONV = 512
CONV_WIDTH = 31
POOL_WINDOWS = (2, 4, 8, 16)
POOL_GROUP = 128
RMS_EPS = 1e-6
LN_EPS = 1e-5

ADAM_LR = 0.001
ADAM_B1 = 0.9
ADAM_B2 = 0.999
ADAM_EPS = 1e-08
ADAM_WD = 0.01
ADAM_STEP = 10

HALO = 32
SUBLANES = 8
CONV_ROWS = 64
VMEM_LIMIT = 56 * 1024 * 1024


def _dot(a, b):
    return jnp.dot(a, b, preferred_element_type=F32)


def _dot_nt(a, b):
    return lax.dot_general(a, b, (((1,), (1,)), ((), ())), preferred_element_type=F32)


def _dot_tn(a, b):
    return lax.dot_general(a, b, (((0,), (0,)), ((), ())), preferred_element_type=F32)


def _mean_last(v):
    return jnp.mean(v, axis=-1, keepdims=True)


def _full(shape):
    nd = len(shape)
    return pl.BlockSpec(shape, lambda *_: (0,) * nd)


def _full1(shape):
    nd = len(shape)
    return pl.BlockSpec(shape, lambda *_: (0,) * nd, pipeline_mode=pl.Buffered(1))


def _shifted_copies(sh_ref, rows):
    for s in range(1, SUBLANES):
        sh_ref[s, 0:rows, :] = sh_ref[0, s:s + rows, :]


def _tap(sh_ref, off, r0, rows):
    q, s = divmod(off, SUBLANES)
    return sh_ref[s, pl.ds(r0 + q * SUBLANES, rows), :]


def _mix_fwd(x, g_mix, w_in, b_in, w_dw, b_dw, ln_g, ln_b, w_pool, s_pool, w_out, ag_shards, ts):
    S, D = x.shape
    d_in = w_in.shape[0]
    C = C_CONV
    nt = S // ts
    nrb = ts // CONV_ROWS
    n_ag = len(ag_shards)
    relay_step = (5 * nt) // 8

    def body(x_ref, g_ref, win_ref, bin_ref, wdw_ref, bdw_ref, lng_ref, lnb_ref, wp_ref, sp_ref, wout_ref, *rest):
        ag_in, rest = rest[:n_ag], rest[n_ag:]
        a_ref, gate_ref, v_ref, m_ref, y_ref, h1_ref, xn_ref = rest[:7]
        ag_out, rest = rest[7:7 + n_ag], rest[7 + n_ag:]
        ush, pbuf = rest[:2]
        ag_bufs, (send_sems, recv_sems, local_sems) = rest[2:2 + n_ag], rest[2 + n_ag:]
        i = pl.program_id(0)
        ag_start, ag_relay, ag_finish = _gather_plan(ag_in, ag_out, ag_bufs, send_sems, recv_sems, local_sems)

        @pl.when(i == 0)
        def _():
            ag_start()
            ush[0, 0:HALO, :] = jnp.zeros((HALO, C), F32)
            pbuf[0:HALO, :] = jnp.zeros((HALO, C), F32)

        @pl.when(i == relay_step)
        def _():
            ag_relay()

        x = x_ref[...]
        r1 = lax.rsqrt(_mean_last(x * x) + RMS_EPS)
        xn = (x * r1 * g_ref[...]).astype(BF16)
        xn_ref[...] = xn
        z = _dot_nt(xn, win_ref[...]) + bin_ref[...]
        a = z[:, 0:C]
        gate = z[:, C:2 * C]
        a_ref[...] = a
        gate_ref[...] = gate
        ush[0, HALO:HALO + ts, :] = a * jax.nn.sigmoid(gate)
        pbuf[HALO:HALO + ts, :] = z[:, 2 * C:]

        _shifted_copies(ush, ts + HALO - SUBLANES)

        def conv_block(rb, carry):
            r0 = pl.multiple_of(rb * CONV_ROWS, CONV_ROWS)
            acc = jnp.zeros((CONV_ROWS, C), F32)
            for k in range(CONV_WIDTH):
                acc = acc + wdw_ref[k:k + 1, :] * _tap(ush, HALO - (CONV_WIDTH - 1) + k, r0, CONV_ROWS)
            v_ref[pl.ds(r0, CONV_ROWS), :] = acc + bdw_ref[...]
            return carry

        lax.fori_loop(0, nrb, conv_block, 0)

        v = v_ref[...]
        mu = _mean_last(v)
        xc = v - mu
        rstd = lax.rsqrt(_mean_last(xc * xc) + LN_EPS)
        ln = xc * rstd * lng_ref[...] + lnb_ref[...]
        y_ref[:, 0:C] = (ln * jax.nn.sigmoid(ln)).astype(BF16)

        row = lax.broadcasted_iota(jnp.int32, (ts, 1), 0) + i * ts
        for gi, w in enumerate(POOL_WINDOWS):
            lanes = slice(gi * POOL_GROUP, (gi + 1) * POOL_GROUP)
            seg = pbuf[HALO:HALO + ts, lanes]
            ws = seg
            for k in range(1, w):
                ws = ws + pbuf[HALO - k:HALO - k + ts, lanes]
            cnt = jnp.minimum(row + 1, w).astype(F32)
            m = (ws / cnt - seg).astype(BF16)
            m_ref[:, lanes] = m
            ypre = _dot(m, wp_ref[gi])
            y_ref[:, C + gi * POOL_GROUP:C + (gi + 1) * POOL_GROUP] = (ypre * sp_ref[:, lanes]).astype(BF16)

        h1_ref[...] = x + _dot(y_ref[...], wout_ref[...])

        ush[0, 0:HALO, :] = ush[0, ts:ts + HALO, :]
        pbuf[0:HALO, :] = pbuf[ts:ts + HALO, :]

        @pl.when(i == nt - 1)
        def _():
            ag_finish()

    tile = lambda w, dt: (pl.BlockSpec((ts, w), lambda i: (i, 0)), jax.ShapeDtypeStruct((S, w), dt))
    outs = [tile(C, F32), tile(C, F32), tile(C, F32), tile(C, BF16), tile(D, BF16), tile(D, F32), tile(D, BF16)]
    return pl.pallas_call(
        body, name="mix_fwd", grid=(nt,),
        in_specs=[pl.BlockSpec((ts, D), lambda i: (i, 0)), _full((1, D)), _full((d_in, D)), _full((1, d_in)),
                  _full(w_dw.shape), _full((1, C)), _full((1, C)), _full((1, C)), _full(w_pool.shape),
                  _full((1, C)), _full((D, D))] + [_HBM] * n_ag,
        out_specs=[o[0] for o in outs] + [_HBM] * n_ag,
        out_shape=[o[1] for o in outs] + _gather_out_shapes(ag_shards),
        scratch_shapes=[pltpu.VMEM((SUBLANES, ts + HALO, C), F32), pltpu.VMEM((ts + HALO, C), F32)]
        + _bounce_bufs(ag_shards) + _comm_sems(n_ag, 7),
        compiler_params=pltpu.CompilerParams(dimension_semantics=("arbitrary",), vmem_limit_bytes=VMEM_LIMIT),
    )(x, g_mix, w_in, b_in, w_dw, b_dw, ln_g, ln_b, w_pool, s_pool, w_out, *ag_shards)


def _ffn(h1, target, g_ffn, g_final, w_gate, w_up, w_down, ts, f_chunks):
    S, D = h1.shape
    Fd = w_gate.shape[0]
    nt = S // ts
    bounds = []
    lo = 0
    for n in f_chunks:
        bounds.append((lo, lo + n))
        lo += n
    assert lo == Fd

    def body(h1_ref, tgt_ref, gf_ref, gl_ref, wg_ref, wu_ref, wd_ref,
             dh1_ref, hn_ref, act_ref, dgt_ref, dup_ref, dh2_ref, loss_ref, dgl_ref, dgf_ref, gt_s, up_s):
        i = pl.program_id(0)

        @pl.when(i == 0)
        def _():
            loss_ref[...] = jnp.zeros_like(loss_ref)
            dgl_ref[...] = jnp.zeros_like(dgl_ref)
            dgf_ref[...] = jnp.zeros_like(dgf_ref)

        h1 = h1_ref[...]
        r2 = lax.rsqrt(_mean_last(h1 * h1) + RMS_EPS)
        hhat = h1 * r2
        hn = (hhat * gf_ref[...]).astype(BF16)
        hn_ref[...] = hn
        h2 = h1
        for lo, hi in bounds:
            gt = _dot_nt(hn, wg_ref[lo:hi, :])
            up = _dot_nt(hn, wu_ref[lo:hi, :])
            gt_s[:, lo:hi] = gt
            up_s[:, lo:hi] = up
            act = (gt * jax.nn.sigmoid(gt) * up).astype(BF16)
            act_ref[:, lo:hi] = act
            h2 = h2 + _dot(act, wd_ref[lo:hi, :])

        r3 = lax.rsqrt(_mean_last(h2 * h2) + RMS_EPS)
        n3 = h2 * r3
        gl = gl_ref[...]
        diff = n3 * gl - tgt_ref[...]
        loss_ref[...] += jnp.sum(0.5 * jnp.sum(diff * diff, axis=-1, keepdims=True) / D, axis=0, keepdims=True)
        dout = diff / D
        dgl_ref[...] += jnp.sum(dout * n3, axis=0, keepdims=True)
        dn = dout * gl
        dh2 = r3 * (dn - n3 * _mean_last(dn * n3))
        dh2b = dh2.astype(BF16)
        dh2_ref[...] = dh2b

        dhn = jnp.zeros((ts, D), F32)
        for lo, hi in bounds:
            gt = gt_s[:, lo:hi]
            up = up_s[:, lo:hi]
            sg = jax.nn.sigmoid(gt)
            dact = _dot_nt(dh2b, wd_ref[lo:hi, :])
            dgt = (dact * up * (sg * (1.0 + gt * (1.0 - sg)))).astype(BF16)
            dup = (dact * (gt * sg)).astype(BF16)
            dgt_ref[:, lo:hi] = dgt
            dup_ref[:, lo:hi] = dup
            dhn = dhn + _dot(dgt, wg_ref[lo:hi, :]) + _dot(dup, wu_ref[lo:hi, :])

        dgf_ref[...] += jnp.sum(dhn * hhat, axis=0, keepdims=True)
        dnn = dhn * gf_ref[...]
        dh1_ref[...] = dh2 + r2 * (dnn - hhat * _mean_last(dnn * hhat))

    tile = lambda w, dt: (pl.BlockSpec((ts, w), lambda i: (i, 0)), jax.ShapeDtypeStruct((S, w), dt))
    acc = lambda w: (_full((1, w)), jax.ShapeDtypeStruct((1, w), F32))
    outs = [tile(D, F32), tile(D, BF16), tile(Fd, BF16), tile(Fd, BF16), tile(Fd, BF16), tile(D, BF16),
            acc(128), acc(D), acc(D)]
    return pl.pallas_call(
        body, name="ffn_fwd_bwd", grid=(nt,),
        in_specs=[pl.BlockSpec((ts, D), lambda i: (i, 0)), pl.BlockSpec((ts, D), lambda i: (i, 0)),
                  _full((1, D)), _full((1, D)), _full1((Fd, D)), _full1((Fd, D)), _full1((Fd, D))],
        out_specs=[o[0] for o in outs], out_shape=[o[1] for o in outs],
        scratch_shapes=[pltpu.VMEM((ts, Fd), F32), pltpu.VMEM((ts, Fd), F32)],
        compiler_params=pltpu.CompilerParams(dimension_semantics=("arbitrary",), vmem_limit_bytes=VMEM_LIMIT),
    )(h1, target, g_ffn, g_final, w_gate, w_up, w_down)


def _mix_bwd(dh1, x, a, gate, v, m, g_mix, w_in, w_dw, ln_g, ln_b, w_pool, s_pool, w_out, rs_parts, ts):
    S, D = x.shape
    n_rs = len(rs_parts)
    d_in = w_in.shape[0]
    C = C_CONV
    nt = S // ts
    nrb = ts // CONV_ROWS
    wrows =((CONV_WIDTH + SUBLANES - 1) // SUBLANES) * SUBLANES

    def body(dh1_ref, x_ref, a_ref, gate_ref, v_ref, m_ref, g_ref, win_ref, wdw_ref, lng_ref,
             lnb_ref, wp_ref, sp_ref, wout_ref, *rest):
        rs_in, rest = rest[:n_rs], rest[n_rs:]
        (dx_ref, dz_ref, dh1b_ref, dgm_ref, dbin_ref, dwdw_ref, dbdw_ref, dlng_ref, dlnb_ref, dwp_ref,
         dsp_ref) = rest[:11]
        rs_out, rest = rest[11:11 + n_rs], rest[11 + n_rs:]
        dvsh, dqbuf, du_s = rest[:3]
        rs_bufs, (send_sems, recv_sems, local_sems) = rest[3:3 + n_rs], rest[3 + n_rs:]
        i = pl.program_id(0)
        t = nt - 1 - i
        rs_start, rs_finish = _chip_exchange_plan(rs_in, rs_out, rs_bufs, send_sems, recv_sems, local_sems)

        @pl.when(i == 0)
        def _():
            rs_start()
            dvsh[0, ts:ts + HALO, :] = jnp.zeros((HALO, C), F32)
            dqbuf[ts:ts + HALO, :] = jnp.zeros((HALO, C), F32)
            for r in (dgm_ref, dbin_ref, dwdw_ref, dbdw_ref, dlng_ref, dlnb_ref, dwp_ref, dsp_ref):
                r[...] = jnp.zeros_like(r)

        dh1 = dh1_ref[...]
        dh1b = dh1.astype(BF16)
        dh1b_ref[...] = dh1b
        dy = _dot_nt(dh1b, wout_ref[...])

        v = v_ref[...]
        mu = _mean_last(v)
        xc = v - mu
        rstd = lax.rsqrt(_mean_last(xc * xc) + LN_EPS)
        vhat = xc * rstd
        lng = lng_ref[...]
        ln = vhat * lng + lnb_ref[...]
        sg = jax.nn.sigmoid(ln)
        dln = dy[:, 0:C] * (sg * (1.0 + ln * (1.0 - sg)))
        dlng_ref[...] += jnp.sum(dln * vhat, axis=0, keepdims=True)
        dlnb_ref[...] += jnp.sum(dln, axis=0, keepdims=True)
        dvh = dln * lng
        dv = rstd * (dvh - _mean_last(dvh) - vhat * _mean_last(dvh * vhat))
        dbdw_ref[...] += jnp.sum(dv, axis=0, keepdims=True)
        dvsh[0, 0:ts, :] = dv
        _shifted_copies(dvsh, ts + HALO - SUBLANES)

        def conv_block(rb, carry):
            r0 = pl.multiple_of(rb * CONV_ROWS, CONV_ROWS)
            acc = jnp.zeros((CONV_ROWS, C), F32)
            for k in range(CONV_WIDTH):
                acc = acc + wdw_ref[k:k + 1, :] * _tap(dvsh, CONV_WIDTH - 1 - k, r0, CONV_ROWS)
            du_s[pl.ds(r0, CONV_ROWS), :] = acc
            return carry

        lax.fori_loop(0, nrb, conv_block, 0)

        a = a_ref[...]
        sgate = jax.nn.sigmoid(gate_ref[...])
        u = a * sgate
        for k in range(CONV_WIDTH):
            q, s = divmod(CONV_WIDTH - 1 - k, SUBLANES)
            prod = u * dvsh[s, q * SUBLANES:q * SUBLANES + ts, :]
            dwdw_ref[k:k + 1, :] += jnp.sum(prod, axis=0, keepdims=True)

        du = du_s[...]
        da = du * sgate
        dgate = du * a * sgate * (1.0 - sgate)
        dz_ref[:, 0:C] = da.astype(BF16)
        dz_ref[:, C:2 * C] = dgate.astype(BF16)
        dbin_ref[:, 0:C] += jnp.sum(da, axis=0, keepdims=True)
        dbin_ref[:, C:2 * C] += jnp.sum(dgate, axis=0, keepdims=True)

        row = lax.broadcasted_iota(jnp.int32, (ts, 1), 0) + t * ts
        for gi, w in enumerate(POOL_WINDOWS):
            lanes = slice(gi * POOL_GROUP, (gi + 1) * POOL_GROUP)
            dyp = dy[:, C + gi * POOL_GROUP:C + (gi + 1) * POOL_GROUP]
            mg = m_ref[:, lanes]
            ypre = _dot(mg, wp_ref[gi])
            dsp_ref[:, lanes] += jnp.sum(dyp * ypre, axis=0, keepdims=True)
            dyi = (dyp * sp_ref[:, lanes]).astype(BF16)
            dwp_ref[gi] += _dot_tn(mg, dyi)
            dm = _dot_nt(dyi, wp_ref[gi])
            cnt = jnp.minimum(row + 1, w).astype(F32)
            dqbuf[0:ts, lanes] = dm / cnt
            dp = -dm
            for k in range(w):
                dp = dp + dqbuf[k:k + ts, lanes]
            dz_ref[:, 2 * C + gi * POOL_GROUP:2 * C + (gi + 1) * POOL_GROUP] = dp.astype(BF16)
            dbin_ref[:, 2 * C + gi * POOL_GROUP:2 * C + (gi + 1) * POOL_GROUP] += jnp.sum(dp, axis=0, keepdims=True)

        dxn = _dot(dz_ref[...], win_ref[...])
        x = x_ref[...]
        r1 = lax.rsqrt(_mean_last(x * x) + RMS_EPS)
        xhat = x * r1
        dgm_ref[...] += jnp.sum(dxn * xhat, axis=0, keepdims=True)
        dnn = dxn * g_ref[...]
        dx_ref[...] = dh1 + r1 * (dnn - xhat * _mean_last(dnn * xhat))

        dvsh[0, ts:ts + HALO, :] = dvsh[0, 0:HALO, :]
        dqbuf[ts:ts + HALO, :] = dqbuf[0:HALO, :]

        @pl.when(i == nt - 1)
        def _():
            rs_finish()

    rev = lambda w: pl.BlockSpec((ts, w), lambda i: (nt - 1 - i, 0))
    acc = lambda shape: (_full(shape), jax.ShapeDtypeStruct(shape, F32))
    outs = [(rev(D), jax.ShapeDtypeStruct((S, D), F32)), (rev(d_in), jax.ShapeDtypeStruct((S, d_in), BF16)),
            (rev(D), jax.ShapeDtypeStruct((S, D), BF16)), acc((1, D)), acc((1, d_in)), acc((wrows, C)),
            acc((1, C)), acc((1, C)), acc((1, C)), acc(w_pool.shape), acc((1, C))]
    return pl.pallas_call(
        body, name="mix_bwd", grid=(nt,),
        in_specs=[rev(D), rev(D), rev(C), rev(C), rev(C), rev(C), _full((1, D)), _full((d_in, D)),
                  _full(w_dw.shape), _full((1, C)), _full((1, C)), _full(w_pool.shape), _full((1, C)),
                  _full((D, D))] + [_HBM] * n_rs,
        out_specs=[o[0] for o in outs] + [_HBM] * n_rs,
        out_shape=[o[1] for o in outs] + [jax.ShapeDtypeStruct(p.shape, p.dtype) for p in rs_parts],
        scratch_shapes=[pltpu.VMEM((SUBLANES, ts + HALO, C), F32), pltpu.VMEM((ts + HALO, C), F32),
                        pltpu.VMEM((ts, C), F32)]
        + [pltpu.VMEM(p.shape[1:], p.dtype) for p in rs_parts] + _comm_sems(n_rs, 3),
        compiler_params=pltpu.CompilerParams(dimension_semantics=("arbitrary",), vmem_limit_bytes=VMEM_LIMIT),
    )(dh1, x, a, gate, v, m, g_mix, w_in, w_dw, ln_g, ln_b, w_pool, s_pool, w_out, *rs_parts)


def _grad_matmul(a, b, tm, name):
    S, M = a.shape
    N = b.shape[1]

    def body(a_ref, b_ref, o_ref):
        o_ref[...] = _dot_tn(a_ref[...], b_ref[...]).astype(BF16)

    return pl.pallas_call(
        body, name=name, grid=(M // tm,),
        in_specs=[pl.BlockSpec((S, tm), lambda i: (0, i)), _full1((S, N))],
        out_specs=pl.BlockSpec((tm, N), lambda i: (i, 0)),
        out_shape=jax.ShapeDtypeStruct((M, N), BF16),
        compiler_params=pltpu.CompilerParams(dimension_semantics=("parallel",), vmem_limit_bytes=VMEM_LIMIT),
    )(a, b)


def _position():
    return lax.axis_index("x"), lax.axis_index("y"), lax.axis_index("c")


def _slot(px, py, pc):
    return 4 * px + 2 * py + pc


def _all_gather(shards):
    n = len(shards)

    def body(*refs):
        start, relay, finish = _gather_plan(refs[:n], refs[n:2 * n], refs[2 * n:3 * n], *refs[3 * n:])
        start()
        relay()
        finish()

    return pl.pallas_call(
        body, name="all_gather_mix_weights",
        out_shape=_gather_out_shapes(shards), in_specs=[_HBM] * n, out_specs=[_HBM] * n,
        scratch_shapes=_bounce_bufs(shards) + _comm_sems(n, 7),
    )(*shards)


_HBM = pl.BlockSpec(memory_space=pl.ANY)


def _comm_sems(n, copies):
    return [pltpu.SemaphoreType.DMA((n, copies)), pltpu.SemaphoreType.DMA((n, copies)),
            pltpu.SemaphoreType.DMA((n, 2))]


def _bounce_bufs(blocks):
    return [pltpu.VMEM(b.shape, b.dtype) for b in blocks]


def _local_copy(srcs, dsts, bufs, local_sems):
    n = len(srcs)
    loads = [pltpu.make_async_copy(srcs[k], bufs[k], local_sems.at[k, 0]) for k in range(n)]
    for cp in loads:
        cp.start()
    for cp in loads:
        cp.wait()
    stores = _local_stores(dsts, bufs, local_sems)
    for cp in stores:
        cp.start()
    return stores


def _local_stores(dsts, bufs, local_sems):
    return [pltpu.make_async_copy(bufs[k], dsts[k], local_sems.at[k, 1]) for k in range(len(dsts))]


def _gather_out_shapes(shards):
    return [jax.ShapeDtypeStruct((N_DEV,) + s.shape, s.dtype) for s in shards]


def _gather_plan(ins, outs, bufs, send_sems, recv_sems, local_sems):
    n = len(ins)
    x, y, c = _position()
    me, sibling = (x, y, c), (x, y, 1 - c)
    na, nb, dg = (x ^ (1 - c), y ^ c), (x ^ c, y ^ (1 - c)), (1 - x, 1 - y)
    own = [outs[k].at[_slot(*me)] for k in range(n)]

    def copy(k, sem, block, to, src=None):
        dst = outs[k].at[_slot(*block)]
        return pltpu.make_async_remote_copy(
            src_ref=dst if src is None else src, dst_ref=dst, send_sem=send_sems.at[k, sem],
            recv_sem=recv_sems.at[k, sem], device_id=to, device_id_type=MESH)

    def first():
        cps = []
        for k in range(n):
            cps += [copy(k, 0, me, sibling, src=ins[k]), copy(k, 1, me, (*na, c), src=ins[k]),
                    copy(k, 2, me, (*nb, c), src=ins[k])]
        return cps

    def onward():
        return [copy(k, 3, (*na, c), (*nb, c)) for k in range(n)]

    def to_sibling(j, chip):
        return [copy(k, 4 + j, (*chip, c), sibling) for k in range(n)]

    def start():
        for cp in first():
            cp.start()
        _local_copy(ins, own, bufs, local_sems)

    def relay():
        for k in range(n):
            copy(k, 1, (*na, c), me).wait_recv()
        for cp in onward() + to_sibling(0, na):
            cp.start()
        for k in range(n):
            copy(k, 2, (*nb, c), me).wait_recv()
        for cp in to_sibling(1, nb):
            cp.start()

    def finish():
        for k in range(n):
            copy(k, 3, (*dg, c), me).wait_recv()
        for cp in to_sibling(2, dg):
            cp.start()
        for k in range(n):
            copy(k, 0, sibling, me).wait_recv()
            for j, chip in enumerate((nb, na, dg)):
                copy(k, 4 + j, (*chip, 1 - c), me).wait_recv()
        for cp in first() + onward() + to_sibling(0, na) + to_sibling(1, nb) + to_sibling(2, dg):
            cp.wait_send()
        for cp in _local_stores(own, bufs, local_sems):
            cp.wait()

    return start, relay, finish


N_CHIPS = 4
_CHIP_FLIPS = [(1, 0), (0, 1), (1, 1)]


def _chip_exchange_plan(ins, outs, bufs, send_sems, recv_sems, local_sems):
    n = len(ins)
    x, y, c = _position()
    my_q = 2 * x + y
    peers = [(x ^ fx, y ^ fy) for fx, fy in _CHIP_FLIPS]
    own = [outs[k].at[my_q] for k in range(n)]

    def sends():
        return [pltpu.make_async_remote_copy(
            src_ref=ins[k].at[2 * px + py], dst_ref=outs[k].at[my_q], send_sem=send_sems.at[k, f],
            recv_sem=recv_sems.at[k, f], device_id=(px, py, c), device_id_type=MESH)
            for f, (px, py) in enumerate(peers) for k in range(n)]

    def start():
        for cp in sends():
            cp.start()
        _local_copy([ins[k].at[my_q] for k in range(n)], own, bufs, local_sems)

    def finish():
        for f, (px, py) in enumerate(peers):
            for k in range(n):
                pltpu.make_async_remote_copy(
                    src_ref=ins[k].at[my_q], dst_ref=outs[k].at[2 * px + py], send_sem=send_sems.at[k, f],
                    recv_sem=recv_sems.at[k, f], device_id=(px, py, c), device_id_type=MESH).wait_recv()
        for cp in sends():
            cp.wait_send()
        for cp in _local_stores(own, bufs, local_sems):
            cp.wait()

    return start, finish


def _pair_exchange(parts):
    n = len(parts)

    def body(*refs):
        ins, sib = refs[:n], refs[n:2 * n]
        send_sems, recv_sems = refs[2 * n:]
        x, y, c = _position()
        copies = [pltpu.make_async_remote_copy(
            src_ref=ins[k].at[2 * q + 1 - c], dst_ref=sib[k].at[q], send_sem=send_sems.at[k, q],
            recv_sem=recv_sems.at[k, q], device_id=(x, y, 1 - c), device_id_type=MESH)
            for k in range(n) for q in range(N_CHIPS)]
        for cp in copies:
            cp.start()
        for cp in copies:
            cp.wait()

    sems = pltpu.SemaphoreType.DMA((n, N_CHIPS))
    return pl.pallas_call(
        body, name="pair_exchange",
        out_shape=[jax.ShapeDtypeStruct((N_CHIPS,) + p.shape[1:], p.dtype) for p in parts],
        in_specs=[_HBM] * n, out_specs=[_HBM] * n, scratch_shapes=[sems, sems],
    )(*parts)


def _pair_sum(parts, sib):
    n = len(parts)

    def body(*refs):
        c = lax.axis_index("c")
        for k in range(n):
            refs[2 * n + k][0] = (refs[k][0, c].astype(F32) + refs[n + k][0].astype(F32)).astype(BF16)

    pair = [pl.BlockSpec((1, 2) + p.shape[1:], lambda q: (q, 0, 0, 0)) for p in parts]
    one = [pl.BlockSpec((1,) + p.shape[1:], lambda q: (q, 0, 0)) for p in parts]
    return pl.pallas_call(
        body, name="pair_sum", grid=(N_CHIPS,), in_specs=pair + one, out_specs=one,
        out_shape=[jax.ShapeDtypeStruct(s.shape, BF16) for s in sib],
        compiler_params=pltpu.CompilerParams(dimension_semantics=("parallel",)),
    )(*[p.reshape((N_CHIPS, 2) + p.shape[1:]) for p in parts], *sib)


def _tail_reduce(parts, vals):
    n, ns = len(parts), len(vals)
    vm = pl.BlockSpec(memory_space=pltpu.VMEM)

    def body(*refs):
        p_in, v_in = refs[:n], refs[n:n + ns]
        p_out, v_out = refs[n + ns:2 * n + ns], refs[2 * n + ns:2 * (n + ns)]
        rest = refs[2 * (n + ns):]
        p_sib, p_sum = rest[:n], rest[n:2 * n]
        v_sib, v_sum, v_all = rest[2 * n:2 * n + ns], rest[2 * n + ns:2 * n + 2 * ns], rest[2 * n + 2 * ns:2 * n + 3 * ns]
        p1_send, p1_recv, v1_send, v1_recv, p3_send, p3_recv, v3_send, v3_recv = rest[2 * n + 3 * ns:]
        x, y, c = _position()
        my_q = 2 * x + y
        sibling = (x, y, 1 - c)
        peers = [(x ^ fx, y ^ fy) for fx, fy in _CHIP_FLIPS]

        def to_sibling(src, dst, send, recv):
            return pltpu.make_async_remote_copy(src_ref=src, dst_ref=dst, send_sem=send, recv_sem=recv,
                                                device_id=sibling, device_id_type=MESH)

        level1 = [to_sibling(p_in[k].at[2 * q + 1 - c], p_sib[k].at[q], p1_send.at[k, q], p1_recv.at[k, q])
                  for k in range(n) for q in range(N_CHIPS)]
        level1 += [to_sibling(v_in[k], v_sib[k], v1_send.at[k], v1_recv.at[k]) for k in range(ns)]
        for cp in level1:
            cp.start()
        for cp in level1:
            cp.wait_recv()
        for k in range(n):
            for q in range(N_CHIPS):
                p_sum[k][q] = (p_in[k][2 * q + c].astype(F32) + p_sib[k][q].astype(F32)).astype(p_sum[k].dtype)
        for k in range(ns):
            v_sum[k][...] = v_in[k][...] + v_sib[k][...]

        def to_chip(f, src, dst, send, recv):
            px, py = peers[f]
            return pltpu.make_async_remote_copy(src_ref=src, dst_ref=dst, send_sem=send, recv_sem=recv,
                                                device_id=(px, py, c), device_id_type=MESH)

        def level2(sending):
            cps = []
            for f, (px, py) in enumerate(peers):
                their_q = 2 * px + py
                for k in range(n):
                    src, dst = (p_sum[k].at[their_q], p_out[k].at[my_q]) if sending else (
                        p_sum[k].at[my_q], p_out[k].at[their_q])
                    cps.append(to_chip(f, src, dst, p3_send.at[k, f], p3_recv.at[k, f]))
                for k in range(ns):
                    dst = v_all[k].at[my_q] if sending else v_all[k].at[their_q]
                    cps.append(to_chip(f, v_sum[k], dst, v3_send.at[k, f], v3_recv.at[k, f]))
            return cps

        for cp in level2(True):
            cp.start()
        for k in range(n):
            p_out[k][my_q] = p_sum[k][my_q]
        for k in range(ns):
            v_all[k][my_q] = v_sum[k][...]
        for cp in level2(False):
            cp.wait_recv()
        for k in range(ns):
            total = v_all[k][0]
            for q in range(1, N_CHIPS):
                total = total + v_all[k][q]
            v_out[k][...] = total
        for cp in level1 + level2(True):
            cp.wait_send()

    quarter = [(N_CHIPS,) + p.shape[1:] for p in parts]
    dma = pltpu.SemaphoreType.DMA
    return pl.pallas_call(
        body, name="tail_reduce",
        out_shape=[jax.ShapeDtypeStruct(s, p.dtype) for s, p in zip(quarter, parts)]
        + [jax.ShapeDtypeStruct(v.shape, v.dtype) for v in vals],
        in_specs=[vm] * (n + ns), out_specs=[vm] * (n + ns),
        scratch_shapes=[pltpu.VMEM(s, p.dtype) for s, p in zip(quarter, parts)] * 2
        + [pltpu.VMEM(v.shape, v.dtype) for v in vals] * 2
        + [pltpu.VMEM((N_CHIPS,) + v.shape, v.dtype) for v in vals]
        + [dma((n, N_CHIPS)), dma((n, N_CHIPS)), dma((ns,)), dma((ns,)), dma((n, 3)), dma((n, 3)), dma((ns, 3)),
           dma((ns, 3))],
        compiler_params=pltpu.CompilerParams(vmem_limit_bytes=VMEM_LIMIT),
    )(*parts, *vals)


def _adamw_math(w, g, m, v):
    m = ADAM_B1 * m + (1.0 - ADAM_B1) * g
    v = ADAM_B2 * v + (1.0 - ADAM_B2) * (g * g)
    m_hat = m / (1.0 - ADAM_B1 ** ADAM_STEP)
    v_hat = v / (1.0 - ADAM_B2 ** ADAM_STEP)
    delta = -ADAM_LR * (m_hat / (jnp.sqrt(v_hat) + ADAM_EPS) + ADAM_WD * w)
    return delta, m, v


def _adamw_shard(parts, w, m, v, tr, name):
    R, Cc = w.shape

    def body(p_ref, w_ref, m_ref, v_ref, g_out, d_out, m_out, v_out):
        g = p_ref[0].astype(F32)
        for j in range(1, N_CHIPS):
            g = g + p_ref[j].astype(F32)
        d, mn, vn = _adamw_math(w_ref[...], g, m_ref[...], v_ref[...])
        g_out[...] = g
        d_out[...] = d
        m_out[...] = mn
        v_out[...] = vn

    blk = pl.BlockSpec((tr, Cc), lambda i: (i, 0))
    return pl.pallas_call(
        body, name=name, grid=(R // tr,),
        in_specs=[pl.BlockSpec((N_CHIPS, tr, Cc), lambda i: (0, i, 0)), blk, blk, blk],
        out_specs=[blk] * 4, out_shape=[jax.ShapeDtypeStruct((R, Cc), F32)] * 4,
        compiler_params=pltpu.CompilerParams(dimension_semantics=("parallel",)),
    )(parts, w, m, v)


def _adamw_small(grads, ws, ms, vs):
    n = len(grads)
    vm = pl.BlockSpec(memory_space=pltpu.VMEM)

    def body(*refs):
        g_in, w_in, m_in, v_in = (refs[k * n:(k + 1) * n] for k in range(4))
        g_out, d_out, m_out, v_out = (refs[(4 + k) * n:(5 + k) * n] for k in range(4))
        for k in range(n):
            g = g_in[k][...]
            d, mn, vn = _adamw_math(w_in[k][...], g, m_in[k][...], v_in[k][...])
            g_out[k][...] = g
            d_out[k][...] = d
            m_out[k][...] = mn
            v_out[k][...] = vn

    shapes = [jax.ShapeDtypeStruct(g.shape, F32) for g in grads]
    return pl.pallas_call(
        body, name="adamw_small", out_shape=shapes * 4, in_specs=[vm] * (4 * n), out_specs=[vm] * (4 * n),
    )(*grads, *ws, *ms, *vs)


def _unshard_cols(g):
    n, R, Cc = g.shape
    return jnp.transpose(g, (1, 0, 2)).reshape(R, n * Cc)


def _shard_cols(full):
    R, Ct = full.shape
    return jnp.transpose(full.reshape(R, N_DEV, Ct // N_DEV), (1, 0, 2))


_COL_SHARDED = ("w_in", "w_gate", "w_up")
_BIG = ("w_in", "w_out", "w_gate", "w_up", "w_down")


def _rows(nm, p):
    return p[0].T if nm in _COL_SHARDED else p[0]


def _step(args, ts_mix_fwd, ts_ffn, ts_mix_bwd, tm_grad):
    (x, g_mix, w_in, b_in, w_dw, b_dw, ln_g, ln_b, w_pool, s_pool, w_out, g_ffn, w_gate, w_up, w_down, g_final,
     loss_target) = args[:17]
    names = ["g_mix", "w_in", "b_in", "w_dw", "b_dw", "ln_g", "ln_b", "w_pool", "s_pool", "w_out", "g_ffn",
             "w_gate", "w_up", "w_down", "g_final"]
    weights = dict(zip(names, args[1:16]))
    moms = dict(zip(names, args[17:32]))
    vars_ = dict(zip(names, args[32:47]))

    S, D = x.shape[1], x.shape[2]
    x2 = x.reshape(S, D)
    tgt2 = loss_target.reshape(S, D)

    bf = lambda nm: _rows(nm, weights[nm]).astype(BF16)
    g_in, g_out, g_dw = _all_gather([bf("w_in"), bf("w_out"), w_dw[0]])
    wt_in, w_out_f = g_in.reshape(-1, D), g_out.reshape(-1, D)
    w_dw_f = _unshard_cols(g_dw)
    w_pool_b = w_pool[0].astype(BF16)

    a, gate, v, m, y, h1, xn, g_gate, g_up, g_down = _mix_fwd(
        x2, g_mix, wt_in, b_in, w_dw_f, b_dw, ln_g, ln_b, w_pool_b, s_pool, w_out_f,
        [bf("w_gate"), bf("w_up"), bf("w_down")], ts_mix_fwd)
    wt_gate, wt_up, w_down_f = g_gate.reshape(-1, D), g_up.reshape(-1, D), g_down.reshape(-1, D)
    Fd = wt_gate.shape[0]
    f_chunks = [1024] * (Fd // 1024) + ([Fd % 1024] if Fd % 1024 else [])
    dh1, hn, act, dgt, dup, dh2, loss_p, dg_final, dg_ffn = _ffn(
        h1, tgt2, g_ffn, g_final.reshape(1, D), wt_gate, wt_up, w_down_f, ts_ffn, f_chunks)

    ffn_grads = [_grad_matmul(dgt, hn, tm_grad, "grad_w_gate"), _grad_matmul(dup, hn, tm_grad, "grad_w_up"),
                 _grad_matmul(act, dh2, tm_grad, "grad_w_down")]
    ffn_parts = [g.reshape(N_DEV, -1, D) for g in ffn_grads]
    pair_sums = _pair_sum(ffn_parts, _pair_exchange(ffn_parts))
    (dx, dz, dh1b, dg_mix, db_in, dw_dw, db_dw, dln_g, dln_b, dw_pool, ds_pool, r_gate, r_up, r_down) = _mix_bwd(
        dh1, x2, a, gate, v, m, g_mix, wt_in, w_dw_f, ln_g, ln_b, w_pool_b, s_pool, w_out_f, pair_sums, ts_mix_bwd)
    dwt_in = _grad_matmul(dz, xn, tm_grad, "grad_w_in")
    dw_out = _grad_matmul(y, dh1b, tm_grad, "grad_w_out")

    small_names = ["g_mix", "b_in", "b_dw", "ln_g", "ln_b", "w_pool", "s_pool", "g_ffn", "g_final"]
    small_shape = lambda p: p.reshape(-1, p.shape[-1])
    partial = [dg_mix, db_in, db_dw, dln_g, dln_b, dw_pool.reshape(-1, POOL_GROUP), ds_pool, dg_ffn, dg_final, loss_p]
    tail = _tail_reduce([dwt_in.reshape(N_DEV, -1, D), dw_out.reshape(N_DEV, -1, D),
                         _shard_cols(dw_dw[0:CONV_WIDTH])], partial)
    r_in, r_out, r_dw = tail[:3]
    summed = tail[3:]

    big = {}
    for nm, r in zip(_BIG, [r_in, r_out, r_gate, r_up, r_down]):
        rows = r.shape[1]
        res = _adamw_shard(r, _rows(nm, weights[nm]), _rows(nm, moms[nm]), _rows(nm, vars_[nm]),
                           rows // 2 if rows % 32 == 0 else rows, "adamw_" + nm)
        big[nm] = [o.T if nm in _COL_SHARDED else o for o in res]
    big["w_dw"] = _adamw_shard(r_dw, w_dw[0], moms["w_dw"][0], vars_["w_dw"][0], CONV_WIDTH, "adamw_w_dw")

    sm = _adamw_small(summed[:-1], [small_shape(weights[nm]) for nm in small_names],
                      [small_shape(moms[nm]) for nm in small_names], [small_shape(vars_[nm]) for nm in small_names])
    n_small = len(small_names)

    def result(kind, nm):
        if nm in big:
            return big[nm][kind].reshape(weights[nm].shape)
        return sm[kind * n_small + small_names.index(nm)].reshape(weights[nm].shape)

    loss = summed[-1][0, 0]
    out = [loss, dx.reshape(x.shape)]
    for kind in range(4):
        out += [result(kind, nm) for nm in names]
    return tuple(out)


def kernel(x, g_mix, w_in, b_in, w_dw, b_dw, ln_g, ln_b, w_pool, s_pool, w_out, g_ffn, w_gate, w_up, w_down, g_final, loss_target, m_g_mix, m_w_in, m_b_in, m_w_dw, m_b_dw, m_ln_g, m_ln_b, m_w_pool, m_s_pool, m_w_out, m_g_ffn, m_w_gate, m_w_up, m_w_down, m_g_final, v_g_mix, v_w_in, v_b_in, v_w_dw, v_b_dw, v_ln_g, v_ln_b, v_w_pool, v_s_pool, v_w_out, v_g_ffn, v_w_gate, v_w_up, v_w_down, v_g_final):
    args = (x, g_mix, w_in, b_in, w_dw, b_dw, ln_g, ln_b, w_pool, s_pool, w_out, g_ffn, w_gate, w_up, w_down, g_final, loss_target, m_g_mix, m_w_in, m_b_in, m_w_dw, m_b_dw, m_ln_g, m_ln_b, m_w_pool, m_s_pool, m_w_out, m_g_ffn, m_w_gate, m_w_up, m_w_down, m_g_final, v_g_mix, v_w_in, v_b_in, v_w_dw, v_b_dw, v_ln_g, v_ln_b, v_w_pool, v_s_pool, v_w_out, v_g_ffn, v_w_gate, v_w_up, v_w_down, v_g_final)
    return _step(args, ts_mix_fwd=512, ts_ffn=256, ts_mix_bwd=256, tm_grad=256)
```

```python
import functools

import jax
import jax.numpy as jnp
from jax import lax
from jax.experimental import pallas as pl
from jax.experimental.pallas import tpu as pltpu

F32 = jnp.float32
BF16 = jnp.bfloat16
MESH = pl.DeviceIdType.MESH
N_DEV = 8

C_CONV = 512
CONV_WIDTH = 31
POOL_WINDOWS = (2, 4, 8, 16)
POOL_GROUP = 128
RMS_EPS = 1e-6
LN_EPS = 1e-5

ADAM_LR = 0.001
ADAM_B1 = 0.9
ADAM_B2 = 0.999
ADAM_EPS = 1e-08
ADAM_WD = 0.01
ADAM_STEP = 10

HALO = 32
SUBLANES = 8
CONV_ROWS = 64
VMEM_LIMIT = 56 * 1024 * 1024


def _dot(a, b):
    return jnp.dot(a, b, preferred_element_type=F32)


def _dot_nt(a, b):
    return lax.dot_general(a, b, (((1,), (1,)), ((), ())), preferred_element_type=F32)


def _dot_tn(a, b):
    return lax.dot_general(a, b, (((0,), (0,)), ((), ())), preferred_element_type=F32)


def _mean_last(v):
    return jnp.mean(v, axis=-1, keepdims=True)


def _full(shape):
    nd = len(shape)
    return pl.BlockSpec(shape, lambda *_: (0,) * nd)


def _full1(shape):
    nd = len(shape)
    return pl.BlockSpec(shape, lambda *_: (0,) * nd, pipeline_mode=pl.Buffered(1))


def _shifted_copies(sh_ref, rows):
    for s in range(1, SUBLANES):
        sh_ref[s, 0:rows, :] = sh_ref[0, s:s + rows, :]


def _tap(sh_ref, off, r0, rows):
    q, s = divmod(off, SUBLANES)
    return sh_ref[s, pl.ds(r0 + q * SUBLANES, rows), :]


def _mix_fwd(x, g_mix, w_in, b_in, w_dw, b_dw, ln_g, ln_b, w_pool, s_pool, w_out, ag_shards, ts):
    S, D = x.shape
    d_in = w_in.shape[0]
    C = C_CONV
    nt = S // ts
    nrb = ts // CONV_ROWS
    n_ag = len(ag_shards)
    relay_step = (5 * nt) // 8

    def body(x_ref, g_ref, win_ref, bin_ref, wdw_ref, bdw_ref, lng_ref, lnb_ref, wp_ref, sp_ref, wout_ref, *rest):
        ag_in, rest = rest[:n_ag], rest[n_ag:]
        a_ref, gate_ref, v_ref, m_ref, y_ref, h1_ref, xn_ref = rest[:7]
        ag_out, rest = rest[7:7 + n_ag], rest[7 + n_ag:]
        ush, pbuf = rest[:2]
        ag_bufs, (send_sems, recv_sems, local_sems) = rest[2:2 + n_ag], rest[2 + n_ag:]
        i = pl.program_id(0)
        ag_start, ag_relay, ag_finish = _gather_plan(ag_in, ag_out, ag_bufs, send_sems, recv_sems, local_sems)

        @pl.when(i == 0)
        def _():
            ag_start()
            ush[0, 0:HALO, :] = jnp.zeros((HALO, C), F32)
            pbuf[0:HALO, :] = jnp.zeros((HALO, C), F32)

        @pl.when(i == relay_step)
        def _():
            ag_relay()

        x = x_ref[...]
        r1 = lax.rsqrt(_mean_last(x * x) + RMS_EPS)
        xn = (x * r1 * g_ref[...]).astype(BF16)
        xn_ref[...] = xn
        z = _dot_nt(xn, win_ref[...]) + bin_ref[...]
        a = z[:, 0:C]
        gate = z[:, C:2 * C]
        a_ref[...] = a
        gate_ref[...] = gate
        ush[0, HALO:HALO + ts, :] = a * jax.nn.sigmoid(gate)
        pbuf[HALO:HALO + ts, :] = z[:, 2 * C:]

        _shifted_copies(ush, ts + HALO - SUBLANES)

        def conv_block(rb, carry):
            r0 = pl.multiple_of(rb * CONV_ROWS, CONV_ROWS)
            acc = jnp.zeros((CONV_ROWS, C), F32)
            for k in range(CONV_WIDTH):
                acc = acc + wdw_ref[k:k + 1, :] * _tap(ush, HALO - (CONV_WIDTH - 1) + k, r0, CONV_ROWS)
            v_ref[pl.ds(r0, CONV_ROWS), :] = acc + bdw_ref[...]
            return carry

        lax.fori_loop(0, nrb, conv_block, 0)

        v = v_ref[...]
        mu = _mean_last(v)
        xc = v - mu
        rstd = lax.rsqrt(_mean_last(xc * xc) + LN_EPS)
        ln = xc * rstd * lng_ref[...] + lnb_ref[...]
        y_ref[:, 0:C] = (ln * jax.nn.sigmoid(ln)).astype(BF16)

        row = lax.broadcasted_iota(jnp.int32, (ts, 1), 0) + i * ts
        for gi, w in enumerate(POOL_WINDOWS):
            lanes = slice(gi * POOL_GROUP, (gi + 1) * POOL_GROUP)
            seg = pbuf[HALO:HALO + ts, lanes]
            ws = seg
            for k in range(1, w):
                ws = ws + pbuf[HALO - k:HALO - k + ts, lanes]
            cnt = jnp.minimum(row + 1, w).astype(F32)
            m = (ws / cnt - seg).astype(BF16)
            m_ref[:, lanes] = m
            ypre = _dot(m, wp_ref[gi])
            y_ref[:, C + gi * POOL_GROUP:C + (gi + 1) * POOL_GROUP] = (ypre * sp_ref[:, lanes]).astype(BF16)

        h1_ref[...] = x + _dot(y_ref[...], wout_ref[...])

        ush[0, 0:HALO, :] = ush[0, ts:ts + HALO, :]
        pbuf[0:HALO, :] = pbuf[ts:ts + HALO, :]

        @pl.when(i == nt - 1)
        def _():
            ag_finish()

    tile = lambda w, dt: (pl.BlockSpec((ts, w), lambda i: (i, 0)), jax.ShapeDtypeStruct((S, w), dt))
    outs = [tile(C, F32), tile(C, F32), tile(C, F32), tile(C, BF16), tile(D, BF16), tile(D, F32), tile(D, BF16)]
    return pl.pallas_call(
        body, name="mix_fwd", grid=(nt,),
        in_specs=[pl.BlockSpec((ts, D), lambda i: (i, 0)), _full((1, D)), _full((d_in, D)), _full((1, d_in)),
                  _full(w_dw.shape), _full((1, C)), _full((1, C)), _full((1, C)), _full(w_pool.shape),
                  _full((1, C)), _full((D, D))] + [_HBM] * n_ag,
        out_specs=[o[0] for o in outs] + [_HBM] * n_ag,
        out_shape=[o[1] for o in outs] + _gather_out_shapes(ag_shards),
        scratch_shapes=[pltpu.VMEM((SUBLANES, ts + HALO, C), F32), pltpu.VMEM((ts + HALO, C), F32)]
        + _bounce_bufs(ag_shards) + _comm_sems(n_ag, 7),
        compiler_params=pltpu.CompilerParams(dimension_semantics=("arbitrary",), vmem_limit_bytes=VMEM_LIMIT),
    )(x, g_mix, w_in, b_in, w_dw, b_dw, ln_g, ln_b, w_pool, s_pool, w_out, *ag_shards)


def _ffn(h1, target, g_ffn, g_final, w_gate, w_up, w_down, ts, f_chunks):
    S, D = h1.shape
    Fd = w_gate.shape[0]
    nt = S // ts
    bounds = []
    lo = 0
    for n in f_chunks:
        bounds.append((lo, lo + n))
        lo += n
    assert lo == Fd

    def body(h1_ref, tgt_ref, gf_ref, gl_ref, wg_ref, wu_ref, wd_ref,
             dh1_ref, hn_ref, act_ref, dgt_ref, dup_ref, dh2_ref, loss_ref, dgl_ref, dgf_ref, gt_s, up_s):
        i = pl.program_id(0)

        @pl.when(i == 0)
        def _():
            loss_ref[...] = jnp.zeros_like(loss_ref)
            dgl_ref[...] = jnp.zeros_like(dgl_ref)
            dgf_ref[...] = jnp.zeros_like(dgf_ref)

        h1 = h1_ref[...]
        r2 = lax.rsqrt(_mean_last(h1 * h1) + RMS_EPS)
        hhat = h1 * r2
        hn = (hhat * gf_ref[...]).astype(BF16)
        hn_ref[...] = hn
        h2 = h1
        for lo, hi in bounds:
            gt = _dot_nt(hn, wg_ref[lo:hi, :])
            up = _dot_nt(hn, wu_ref[lo:hi, :])
            gt_s[:, lo:hi] = gt
            up_s[:, lo:hi] = up
            act = (gt * jax.nn.sigmoid(gt) * up).astype(BF16)
            act_ref[:, lo:hi] = act
            h2 = h2 + _dot(act, wd_ref[lo:hi, :])

        r3 = lax.rsqrt(_mean_last(h2 * h2) + RMS_EPS)
        n3 = h2 * r3
        gl = gl_ref[...]
        diff = n3 * gl - tgt_ref[...]
        loss_ref[...] += jnp.sum(0.5 * jnp.sum(diff * diff, axis=-1, keepdims=True) / D, axis=0, keepdims=True)
        dout = diff / D
        dgl_ref[...] += jnp.sum(dout * n3, axis=0, keepdims=True)
        dn = dout * gl
        dh2 = r3 * (dn - n3 * _mean_last(dn * n3))
        dh2b = dh2.astype(BF16)
        dh2_ref[...] = dh2b

        dhn = jnp.zeros((ts, D), F32)
        for lo, hi in bounds:
            gt = gt_s[:, lo:hi]
            up = up_s[:, lo:hi]
            sg = jax.nn.sigmoid(gt)
            dact = _dot_nt(dh2b, wd_ref[lo:hi, :])
            dgt = (dact * up * (sg * (1.0 + gt * (1.0 - sg)))).astype(BF16)
            dup = (dact * (gt * sg)).astype(BF16)
            dgt_ref[:, lo:hi] = dgt
            dup_ref[:, lo:hi] = dup
            dhn = dhn + _dot(dgt, wg_ref[lo:hi, :]) + _dot(dup, wu_ref[lo:hi, :])

        dgf_ref[...] += jnp.sum(dhn * hhat, axis=0, keepdims=True)
        dnn = dhn * gf_ref[...]
        dh1_ref[...] = dh2 + r2 * (dnn - hhat * _mean_last(dnn * hhat))

    tile = lambda w, dt: (pl.BlockSpec((ts, w), lambda i: (i, 0)), jax.ShapeDtypeStruct((S, w), dt))
    acc = lambda w: (_full((1, w)), jax.ShapeDtypeStruct((1, w), F32))
    outs = [tile(D, F32), tile(D, BF16), tile(Fd, BF16), tile(Fd, BF16), tile(Fd, BF16), tile(D, BF16),
            acc(128), acc(D), acc(D)]
    return pl.pallas_call(
        body, name="ffn_fwd_bwd", grid=(nt,),
        in_specs=[pl.BlockSpec((ts, D), lambda i: (i, 0)), pl.BlockSpec((ts, D), lambda i: (i, 0)),
                  _full((1, D)), _full((1, D)), _full1((Fd, D)), _full1((Fd, D)), _full1((Fd, D))],
        out_specs=[o[0] for o in outs], out_shape=[o[1] for o in outs],
        scratch_shapes=[pltpu.VMEM((ts, Fd), F32), pltpu.VMEM((ts, Fd), F32)],
        compiler_params=pltpu.CompilerParams(dimension_semantics=("arbitrary",), vmem_limit_bytes=VMEM_LIMIT),
    )(h1, target, g_ffn, g_final, w_gate, w_up, w_down)


def _mix_bwd(dh1, x, a, gate, v, m, g_mix, w_in, w_dw, ln_g, ln_b, w_pool, s_pool, w_out, rs_parts, ts):
    S, D = x.shape
    n_rs = len(rs_parts)
    d_in = w_in.shape[0]
    C = C_CONV
    nt = S // ts
    nrb = ts // CONV_ROWS
    wrows =((CONV_WIDTH + SUBLANES - 1) // SUBLANES) * SUBLANES

    def body(dh1_ref, x_ref, a_ref, gate_ref, v_ref, m_ref, g_ref, win_ref, wdw_ref, lng_ref,
             lnb_ref, wp_ref, sp_ref, wout_ref, *rest):
        rs_in, rest = rest[:n_rs], rest[n_rs:]
        (dx_ref, dz_ref, dh1b_ref, dgm_ref, dbin_ref, dwdw_ref, dbdw_ref, dlng_ref, dlnb_ref, dwp_ref,
         dsp_ref) = rest[:11]
        rs_out, rest = rest[11:11 + n_rs], rest[11 + n_rs:]
        dvsh, dqbuf, du_s = rest[:3]
        rs_bufs, (send_sems, recv_sems, local_sems) = rest[3:3 + n_rs], rest[3 + n_rs:]
        i = pl.program_id(0)
        t = nt - 1 - i
        rs_start, rs_finish = _chip_exchange_plan(rs_in, rs_out, rs_bufs, send_sems, recv_sems, local_sems)

        @pl.when(i == 0)
        def _():
            rs_start()
            dvsh[0, ts:ts + HALO, :] = jnp.zeros((HALO, C), F32)
            dqbuf[ts:ts + HALO, :] = jnp.zeros((HALO, C), F32)
            for r in (dgm_ref, dbin_ref, dwdw_ref, dbdw_ref, dlng_ref, dlnb_ref, dwp_ref, dsp_ref):
                r[...] = jnp.zeros_like(r)

        dh1 = dh1_ref[...]
        dh1b = dh1.astype(BF16)
        dh1b_ref[...] = dh1b
        dy = _dot_nt(dh1b, wout_ref[...])

        v = v_ref[...]
        mu = _mean_last(v)
        xc = v - mu
        rstd = lax.rsqrt(_mean_last(xc * xc) + LN_EPS)
        vhat = xc * rstd
        lng = lng_ref[...]
        ln = vhat * lng + lnb_ref[...]
        sg = jax.nn.sigmoid(ln)
        dln = dy[:, 0:C] * (sg * (1.0 + ln * (1.0 - sg)))
        dlng_ref[...] += jnp.sum(dln * vhat, axis=0, keepdims=True)
        dlnb_ref[...] += jnp.sum(dln, axis=0, keepdims=True)
        dvh = dln * lng
        dv = rstd * (dvh - _mean_last(dvh) - vhat * _mean_last(dvh * vhat))
        dbdw_ref[...] += jnp.sum(dv, axis=0, keepdims=True)
        dvsh[0, 0:ts, :] = dv
        _shifted_copies(dvsh, ts + HALO - SUBLANES)

        def conv_block(rb, carry):
            r0 = pl.multiple_of(rb * CONV_ROWS, CONV_ROWS)
            acc = jnp.zeros((CONV_ROWS, C), F32)
            for k in range(CONV_WIDTH):
                acc = acc + wdw_ref[k:k + 1, :] * _tap(dvsh, CONV_WIDTH - 1 - k, r0, CONV_ROWS)
            du_s[pl.ds(r0, CONV_ROWS), :] = acc
            return carry

        lax.fori_loop(0, nrb, conv_block, 0)

        a = a_ref[...]
        sgate = jax.nn.sigmoid(gate_ref[...])
        u = a * sgate
        for k in range(CONV_WIDTH):
            q, s = divmod(CONV_WIDTH - 1 - k, SUBLANES)
            prod = u * dvsh[s, q * SUBLANES:q * SUBLANES + ts, :]
            dwdw_ref[k:k + 1, :] += jnp.sum(prod, axis=0, keepdims=True)

        du = du_s[...]
        da = du * sgate
        dgate = du * a * sgate * (1.0 - sgate)
        dz_ref[:, 0:C] = da.astype(BF16)
        dz_ref[:, C:2 * C] = dgate.astype(BF16)
        dbin_ref[:, 0:C] += jnp.sum(da, axis=0, keepdims=True)
        dbin_ref[:, C:2 * C] += jnp.sum(dgate, axis=0, keepdims=True)

        row = lax.broadcasted_iota(jnp.int32, (ts, 1), 0) + t * ts
        for gi, w in enumerate(POOL_WINDOWS):
            lanes = slice(gi * POOL_GROUP, (gi + 1) * POOL_GROUP)
            dyp = dy[:, C + gi * POOL_GROUP:C + (gi + 1) * POOL_GROUP]
            mg = m_ref[:, lanes]
            ypre = _dot(mg, wp_ref[gi])
            dsp_ref[:, lanes] += jnp.sum(dyp * ypre, axis=0, keepdims=True)
            dyi = (dyp * sp_ref[:, lanes]).astype(BF16)
            dwp_ref[gi] += _dot_tn(mg, dyi)
            dm = _dot_nt(dyi, wp_ref[gi])
            cnt = jnp.minimum(row + 1, w).astype(F32)
            dqbuf[0:ts, lanes] = dm / cnt
            dp = -dm
            for k in range(w):
                dp = dp + dqbuf[k:k + ts, lanes]
            dz_ref[:, 2 * C + gi * POOL_GROUP:2 * C + (gi + 1) * POOL_GROUP] = dp.astype(BF16)
            dbin_ref[:, 2 * C + gi * POOL_GROUP:2 * C + (gi + 1) * POOL_GROUP] += jnp.sum(dp, axis=0, keepdims=True)

        dxn = _dot(dz_ref[...], win_ref[...])
        x = x_ref[...]
        r1 = lax.rsqrt(_mean_last(x * x) + RMS_EPS)
        xhat = x * r1
        dgm_ref[...] += jnp.sum(dxn * xhat, axis=0, keepdims=True)
        dnn = dxn * g_ref[...]
        dx_ref[...] = dh1 + r1 * (dnn - xhat * _mean_last(dnn * xhat))

        dvsh[0, ts:ts + HALO, :] = dvsh[0, 0:HALO, :]
        dqbuf[ts:ts + HALO, :] = dqbuf[0:HALO, :]

        @pl.when(i == nt - 1)
        def _():
            rs_finish()

    rev = lambda w: pl.BlockSpec((ts, w), lambda i: (nt - 1 - i, 0))
    acc = lambda shape: (_full(shape), jax.ShapeDtypeStruct(shape, F32))
    outs = [(rev(D), jax.ShapeDtypeStruct((S, D), F32)), (rev(d_in), jax.ShapeDtypeStruct((S, d_in), BF16)),
            (rev(D), jax.ShapeDtypeStruct((S, D), BF16)), acc((1, D)), acc((1, d_in)), acc((wrows, C)),
            acc((1, C)), acc((1, C)), acc((1, C)), acc(w_pool.shape), acc((1, C))]
    return pl.pallas_call(
        body, name="mix_bwd", grid=(nt,),
        in_specs=[rev(D), rev(D), rev(C), rev(C), rev(C), rev(C), _full((1, D)), _full((d_in, D)),
                  _full(w_dw.shape), _full((1, C)), _full((1, C)), _full(w_pool.shape), _full((1, C)),
                  _full((D, D))] + [_HBM] * n_rs,
        out_specs=[o[0] for o in outs] + [_HBM] * n_rs,
        out_shape=[o[1] for o in outs] + [jax.ShapeDtypeStruct(p.shape, p.dtype) for p in rs_parts],
        scratch_shapes=[pltpu.VMEM((SUBLANES, ts + HALO, C), F32), pltpu.VMEM((ts + HALO, C), F32),
                        pltpu.VMEM((ts, C), F32)]
        + [pltpu.VMEM(p.shape[1:], p.dtype) for p in rs_parts] + _comm_sems(n_rs, 3),
        compiler_params=pltpu.CompilerParams(dimension_semantics=("arbitrary",), vmem_limit_bytes=VMEM_LIMIT),
    )(dh1, x, a, gate, v, m, g_mix, w_in, w_dw, ln_g, ln_b, w_pool, s_pool, w_out, *rs_parts)


def _grad_matmul(a, b, tm, name, reduce_parts=(), reduce_vals=()):
    S, M = a.shape
    N = b.shape[1]
    steps = M // tm
    n, ns = len(reduce_parts), len(reduce_vals)

    def body(a_ref, b_ref, *rest):
        o_ref = rest[n + ns]
        if n + ns:
            start, middle, finish = _reduce_plan(rest[:n], rest[n:n + ns], rest[n + ns + 1:2 * n + ns + 1],
                                                 rest[2 * n + ns + 1:2 * (n + ns) + 1], rest[2 * (n + ns) + 1:])
            i = pl.program_id(0)
            pl.when(i == 0)(start)
            pl.when(i == 1)(middle)
        o_ref[...] = _dot_tn(a_ref[...], b_ref[...]).astype(BF16)
        if n + ns:
            pl.when(i == steps - 1)(finish)

    res = pl.pallas_call(
        body, name=name, grid=(steps,),
        in_specs=[pl.BlockSpec((S, tm), lambda i: (0, i)), _full1((S, N))] + [_VMEM] * (n + ns),
        out_specs=[pl.BlockSpec((tm, N), lambda i: (i, 0))] + [_VMEM] * (n + ns),
        out_shape=[jax.ShapeDtypeStruct((M, N), BF16)] + _reduce_out_shapes(reduce_parts, reduce_vals),
        scratch_shapes=_reduce_scratch(reduce_parts, reduce_vals) if n + ns else [],
        compiler_params=pltpu.CompilerParams(dimension_semantics=("arbitrary",) if n + ns else ("parallel",),
                                             vmem_limit_bytes=VMEM_LIMIT),
    )(a, b, *reduce_parts, *reduce_vals)
    return res if n + ns else res[0]


def _position():
    return lax.axis_index("x"), lax.axis_index("y"), lax.axis_index("c")


def _slot(px, py, pc):
    return 4 * px + 2 * py + pc


def _all_gather(shards):
    n = len(shards)

    def body(*refs):
        start, relay, finish = _gather_plan(refs[:n], refs[n:2 * n], refs[2 * n:3 * n], *refs[3 * n:])
        start()
        relay()
        finish()

    return pl.pallas_call(
        body, name="all_gather_mix_weights",
        out_shape=_gather_out_shapes(shards), in_specs=[_HBM] * n, out_specs=[_HBM] * n,
        scratch_shapes=_bounce_bufs(shards) + _comm_sems(n, 7),
    )(*shards)


_HBM = pl.BlockSpec(memory_space=pl.ANY)


def _comm_sems(n, copies):
    return [pltpu.SemaphoreType.DMA((n, copies)), pltpu.SemaphoreType.DMA((n, copies)),
            pltpu.SemaphoreType.DMA((n, 2))]


def _bounce_bufs(blocks):
    return [pltpu.VMEM(b.shape, b.dtype) for b in blocks]


def _local_copy(srcs, dsts, bufs, local_sems):
    n = len(srcs)
    loads = [pltpu.make_async_copy(srcs[k], bufs[k], local_sems.at[k, 0]) for k in range(n)]
    for cp in loads:
        cp.start()
    for cp in loads:
        cp.wait()
    stores = _local_stores(dsts, bufs, local_sems)
    for cp in stores:
        cp.start()
    return stores


def _local_stores(dsts, bufs, local_sems):
    return [pltpu.make_async_copy(bufs[k], dsts[k], local_sems.at[k, 1]) for k in range(len(dsts))]


def _gather_out_shapes(shards):
    return [jax.ShapeDtypeStruct((N_DEV,) + s.shape, s.dtype) for s in shards]


def _gather_plan(ins, outs, bufs, send_sems, recv_sems, local_sems):
    n = len(ins)
    x, y, c = _position()
    me, sibling = (x, y, c), (x, y, 1 - c)
    na, nb, dg = (x ^ (1 - c), y ^ c), (x ^ c, y ^ (1 - c)), (1 - x, 1 - y)
    own = [outs[k].at[_slot(*me)] for k in range(n)]

    def copy(k, sem, block, to, src=None):
        dst = outs[k].at[_slot(*block)]
        return pltpu.make_async_remote_copy(
            src_ref=dst if src is None else src, dst_ref=dst, send_sem=send_sems.at[k, sem],
            recv_sem=recv_sems.at[k, sem], device_id=to, device_id_type=MESH)

    def first():
        cps = []
        for k in range(n):
            cps += [copy(k, 0, me, sibling, src=ins[k]), copy(k, 1, me, (*na, c), src=ins[k]),
                    copy(k, 2, me, (*nb, c), src=ins[k])]
        return cps

    def onward():
        return [copy(k, 3, (*na, c), (*nb, c)) for k in range(n)]

    def to_sibling(j, chip):
        return [copy(k, 4 + j, (*chip, c), sibling) for k in range(n)]

    def start():
        for cp in first():
            cp.start()
        _local_copy(ins, own, bufs, local_sems)

    def relay():
        for k in range(n):
            copy(k, 1, (*na, c), me).wait_recv()
        for cp in onward() + to_sibling(0, na):
            cp.start()
        for k in range(n):
            copy(k, 2, (*nb, c), me).wait_recv()
        for cp in to_sibling(1, nb):
            cp.start()

    def finish():
        for k in range(n):
            copy(k, 3, (*dg, c), me).wait_recv()
        for cp in to_sibling(2, dg):
            cp.start()
        for k in range(n):
            copy(k, 0, sibling, me).wait_recv()
            for j, chip in enumerate((nb, na, dg)):
                copy(k, 4 + j, (*chip, 1 - c), me).wait_recv()
        for cp in first() + onward() + to_sibling(0, na) + to_sibling(1, nb) + to_sibling(2, dg):
            cp.wait_send()
        for cp in _local_stores(own, bufs, local_sems):
            cp.wait()

    return start, relay, finish


N_CHIPS = 4
_CHIP_FLIPS = [(1, 0), (0, 1), (1, 1)]


def _chip_exchange_plan(ins, outs, bufs, send_sems, recv_sems, local_sems):
    n = len(ins)
    x, y, c = _position()
    my_q = 2 * x + y
    peers = [(x ^ fx, y ^ fy) for fx, fy in _CHIP_FLIPS]
    own = [outs[k].at[my_q] for k in range(n)]

    def sends():
        return [pltpu.make_async_remote_copy(
            src_ref=ins[k].at[2 * px + py], dst_ref=outs[k].at[my_q], send_sem=send_sems.at[k, f],
            recv_sem=recv_sems.at[k, f], device_id=(px, py, c), device_id_type=MESH)
            for f, (px, py) in enumerate(peers) for k in range(n)]

    def start():
        for cp in sends():
            cp.start()
        _local_copy([ins[k].at[my_q] for k in range(n)], own, bufs, local_sems)

    def finish():
        for f, (px, py) in enumerate(peers):
            for k in range(n):
                pltpu.make_async_remote_copy(
                    src_ref=ins[k].at[my_q], dst_ref=outs[k].at[2 * px + py], send_sem=send_sems.at[k, f],
                    recv_sem=recv_sems.at[k, f], device_id=(px, py, c), device_id_type=MESH).wait_recv()
        for cp in sends():
            cp.wait_send()
        for cp in _local_stores(own, bufs, local_sems):
            cp.wait()

    return start, finish


def _pair_exchange(parts):
    n = len(parts)

    def body(*refs):
        ins, sib = refs[:n], refs[n:2 * n]
        send_sems, recv_sems = refs[2 * n:]
        x, y, c = _position()
        copies = [pltpu.make_async_remote_copy(
            src_ref=ins[k].at[2 * q + 1 - c], dst_ref=sib[k].at[q], send_sem=send_sems.at[k, q],
            recv_sem=recv_sems.at[k, q], device_id=(x, y, 1 - c), device_id_type=MESH)
            for k in range(n) for q in range(N_CHIPS)]
        for cp in copies:
            cp.start()
        for cp in copies:
            cp.wait()

    sems = pltpu.SemaphoreType.DMA((n, N_CHIPS))
    return pl.pallas_call(
        body, name="pair_exchange",
        out_shape=[jax.ShapeDtypeStruct((N_CHIPS,) + p.shape[1:], p.dtype) for p in parts],
        in_specs=[_HBM] * n, out_specs=[_HBM] * n, scratch_shapes=[sems, sems],
    )(*parts)


def _pair_sum(parts, sib):
    n = len(parts)

    def body(*refs):
        c = lax.axis_index("c")
        for k in range(n):
            refs[2 * n + k][0] = (refs[k][0, c].astype(F32) + refs[n + k][0].astype(F32)).astype(BF16)

    pair = [pl.BlockSpec((1, 2) + p.shape[1:], lambda q: (q, 0, 0, 0)) for p in parts]
    one = [pl.BlockSpec((1,) + p.shape[1:], lambda q: (q, 0, 0)) for p in parts]
    return pl.pallas_call(
        body, name="pair_sum", grid=(N_CHIPS,), in_specs=pair + one, out_specs=one,
        out_shape=[jax.ShapeDtypeStruct(s.shape, BF16) for s in sib],
        compiler_params=pltpu.CompilerParams(dimension_semantics=("parallel",)),
    )(*[p.reshape((N_CHIPS, 2) + p.shape[1:]) for p in parts], *sib)


def _tail_reduce(parts, vals):
    n, ns = len(parts), len(vals)

    def body(*refs):
        start, middle, finish = _reduce_plan(refs[:n], refs[n:n + ns], refs[n + ns:2 * n + ns],
                                             refs[2 * n + ns:2 * (n + ns)], refs[2 * (n + ns):])
        start()
        middle()
        finish()

    return pl.pallas_call(
        body, name="tail_reduce", out_shape=_reduce_out_shapes(parts, vals),
        in_specs=[_VMEM] * (n + ns), out_specs=[_VMEM] * (n + ns), scratch_shapes=_reduce_scratch(parts, vals),
        compiler_params=pltpu.CompilerParams(vmem_limit_bytes=VMEM_LIMIT),
    )(*parts, *vals)


_VMEM = pl.BlockSpec(memory_space=pltpu.VMEM)


def _reduce_out_shapes(parts, vals):
    return ([jax.ShapeDtypeStruct((N_CHIPS,) + p.shape[1:], p.dtype) for p in parts]
            + [jax.ShapeDtypeStruct(v.shape, v.dtype) for v in vals])


def _reduce_scratch(parts, vals):
    n, ns = len(parts), len(vals)
    quarter = [pltpu.VMEM((N_CHIPS,) + p.shape[1:], p.dtype) for p in parts]
    dma = pltpu.SemaphoreType.DMA
    return (quarter * 3 + [pltpu.VMEM(v.shape, v.dtype) for v in vals] * 2
            + [pltpu.VMEM((N_CHIPS,) + v.shape, v.dtype) for v in vals]
            + [dma((max(n, 1), N_CHIPS)), dma((max(n, 1), N_CHIPS)), dma((max(ns, 1),)), dma((max(ns, 1),)),
               dma((max(n, 1), 3)), dma((max(n, 1), 3)), dma((max(ns, 1), 3)), dma((max(ns, 1), 3))])


def _reduce_plan(p_in, v_in, p_out, v_out, scratch):
    n, ns = len(p_in), len(v_in)
    p_sib, p_sum, p_all = scratch[:n], scratch[n:2 * n], scratch[2 * n:3 * n]
    v_sib, v_sum, v_all = (scratch[3 * n + j * ns:3 * n + (j + 1) * ns] for j in range(3))
    p1_send, p1_recv, v1_send, v1_recv, p3_send, p3_recv, v3_send, v3_recv = scratch[3 * n + 3 * ns:]
    x, y, c = _position()
    my_q = 2 * x + y
    peers = [(x ^ fx, y ^ fy) for fx, fy in _CHIP_FLIPS]

    def to_sibling(src, dst, send, recv):
        return pltpu.make_async_remote_copy(src_ref=src, dst_ref=dst, send_sem=send, recv_sem=recv,
                                            device_id=(x, y, 1 - c), device_id_type=MESH)

    def level1():
        cps = [to_sibling(p_in[k].at[2 * q + 1 - c], p_sib[k].at[q], p1_send.at[k, q], p1_recv.at[k, q])
               for k in range(n) for q in range(N_CHIPS)]
        return cps + [to_sibling(v_in[k], v_sib[k], v1_send.at[k], v1_recv.at[k]) for k in range(ns)]

    def to_chip(f, src, dst, send, recv):
        px, py = peers[f]
        return pltpu.make_async_remote_copy(src_ref=src, dst_ref=dst, send_sem=send, recv_sem=recv,
                                            device_id=(px, py, c), device_id_type=MESH)

    def level2(sending):
        cps = []
        for f, (px, py) in enumerate(peers):
            their_q = 2 * px + py
            for k in range(n):
                src, dst = (p_sum[k].at[their_q], p_all[k].at[my_q]) if sending else (
                    p_sum[k].at[my_q], p_all[k].at[their_q])
                cps.append(to_chip(f, src, dst, p3_send.at[k, f], p3_recv.at[k, f]))
            for k in range(ns):
                dst = v_all[k].at[my_q] if sending else v_all[k].at[their_q]
                cps.append(to_chip(f, v_sum[k], dst, v3_send.at[k, f], v3_recv.at[k, f]))
        return cps

    def start():
        for cp in level1():
            cp.start()

    def middle():
        for cp in level1():
            cp.wait_recv()
        for k in range(n):
            for q in range(N_CHIPS):
                p_sum[k][q] = (p_in[k][2 * q + c].astype(F32) + p_sib[k][q].astype(F32)).astype(p_sum[k].dtype)
        for k in range(ns):
            v_sum[k][...] = v_in[k][...] + v_sib[k][...]
        for cp in level2(True):
            cp.start()
        for k in range(n):
            p_all[k][my_q] = p_sum[k][my_q]
        for k in range(ns):
            v_all[k][my_q] = v_sum[k][...]

    def finish():
        for cp in level2(False):
            cp.wait_recv()
        for k in range(n):
            p_out[k][...] = p_all[k][...]
        for k in range(ns):
            total = v_all[k][0]
            for q in range(1, N_CHIPS):
                total = total + v_all[k][q]
            v_out[k][...] = total
        for cp in level1() + level2(True):
            cp.wait_send()

    return start, middle, finish


def _adamw_math(w, g, m, v):
    m = ADAM_B1 * m + (1.0 - ADAM_B1) * g
    v = ADAM_B2 * v + (1.0 - ADAM_B2) * (g * g)
    m_hat = m / (1.0 - ADAM_B1 ** ADAM_STEP)
    v_hat = v / (1.0 - ADAM_B2 ** ADAM_STEP)
    delta = -ADAM_LR * (m_hat / (jnp.sqrt(v_hat) + ADAM_EPS) + ADAM_WD * w)
    return delta, m, v


def _adamw_shard(parts, w, m, v, tr, name):
    R, Cc = w.shape

    def body(p_ref, w_ref, m_ref, v_ref, g_out, d_out, m_out, v_out):
        g = p_ref[0].astype(F32)
        for j in range(1, N_CHIPS):
            g = g + p_ref[j].astype(F32)
        d, mn, vn = _adamw_math(w_ref[...], g, m_ref[...], v_ref[...])
        g_out[...] = g
        d_out[...] = d
        m_out[...] = mn
        v_out[...] = vn

    blk = pl.BlockSpec((tr, Cc), lambda i: (i, 0))
    return pl.pallas_call(
        body, name=name, grid=(R // tr,),
        in_specs=[pl.BlockSpec((N_CHIPS, tr, Cc), lambda i: (0, i, 0)), blk, blk, blk],
        out_specs=[blk] * 4, out_shape=[jax.ShapeDtypeStruct((R, Cc), F32)] * 4,
        compiler_params=pltpu.CompilerParams(dimension_semantics=("parallel",)),
    )(parts, w, m, v)


def _adamw_small(grads, ws, ms, vs):
    n = len(grads)
    vm = pl.BlockSpec(memory_space=pltpu.VMEM)

    def body(*refs):
        g_in, w_in, m_in, v_in = (refs[k * n:(k + 1) * n] for k in range(4))
        g_out, d_out, m_out, v_out = (refs[(4 + k) * n:(5 + k) * n] for k in range(4))
        for k in range(n):
            g = g_in[k][...]
            d, mn, vn = _adamw_math(w_in[k][...], g, m_in[k][...], v_in[k][...])
            g_out[k][...] = g
            d_out[k][...] = d
            m_out[k][...] = mn
            v_out[k][...] = vn

    shapes = [jax.ShapeDtypeStruct(g.shape, F32) for g in grads]
    return pl.pallas_call(
        body, name="adamw_small", out_shape=shapes * 4, in_specs=[vm] * (4 * n), out_specs=[vm] * (4 * n),
    )(*grads, *ws, *ms, *vs)


def _unshard_cols(g):
    n, R, Cc = g.shape
    return jnp.transpose(g, (1, 0, 2)).reshape(R, n * Cc)


def _shard_cols(full):
    R, Ct = full.shape
    return jnp.transpose(full.reshape(R, N_DEV, Ct // N_DEV), (1, 0, 2))


_COL_SHARDED = ("w_in", "w_gate", "w_up")
_BIG = ("w_in", "w_out", "w_gate", "w_up", "w_down")


def _rows(nm, p):
    return p[0].T if nm in _COL_SHARDED else p[0]


def _step(args, ts_mix_fwd, ts_ffn, ts_mix_bwd, tm_grad):
    (x, g_mix, w_in, b_in, w_dw, b_dw, ln_g, ln_b, w_pool, s_pool, w_out, g_ffn, w_gate, w_up, w_down, g_final,
     loss_target) = args[:17]
    names = ["g_mix", "w_in", "b_in", "w_dw", "b_dw", "ln_g", "ln_b", "w_pool", "s_pool", "w_out", "g_ffn",
             "w_gate", "w_up", "w_down", "g_final"]
    weights = dict(zip(names, args[1:16]))
    moms = dict(zip(names, args[17:32]))
    vars_ = dict(zip(names, args[32:47]))

    S, D = x.shape[1], x.shape[2]
    x2 = x.reshape(S, D)
    tgt2 = loss_target.reshape(S, D)

    bf = lambda nm: _rows(nm, weights[nm]).astype(BF16)
    g_in, g_out, g_dw = _all_gather([bf("w_in"), bf("w_out"), w_dw[0]])
    wt_in, w_out_f = g_in.reshape(-1, D), g_out.reshape(-1, D)
    w_dw_f = _unshard_cols(g_dw)
    w_pool_b = w_pool[0].astype(BF16)

    a, gate, v, m, y, h1, xn, g_gate, g_up, g_down = _mix_fwd(
        x2, g_mix, wt_in, b_in, w_dw_f, b_dw, ln_g, ln_b, w_pool_b, s_pool, w_out_f,
        [bf("w_gate"), bf("w_up"), bf("w_down")], ts_mix_fwd)
    wt_gate, wt_up, w_down_f = g_gate.reshape(-1, D), g_up.reshape(-1, D), g_down.reshape(-1, D)
    Fd = wt_gate.shape[0]
    f_chunks = [1024] * (Fd // 1024) + ([Fd % 1024] if Fd % 1024 else [])
    dh1, hn, act, dgt, dup, dh2, loss_p, dg_final, dg_ffn = _ffn(
        h1, tgt2, g_ffn, g_final.reshape(1, D), wt_gate, wt_up, w_down_f, ts_ffn, f_chunks)

    ffn_grads = [_grad_matmul(dgt, hn, tm_grad, "grad_w_gate"), _grad_matmul(dup, hn, tm_grad, "grad_w_up"),
                 _grad_matmul(act, dh2, tm_grad, "grad_w_down")]
    ffn_parts = [g.reshape(N_DEV, -1, D) for g in ffn_grads]
    pair_sums = _pair_sum(ffn_parts, _pair_exchange(ffn_parts))
    (dx, dz, dh1b, dg_mix, db_in, dw_dw, db_dw, dln_g, dln_b, dw_pool, ds_pool, r_gate, r_up, r_down) = _mix_bwd(
        dh1, x2, a, gate, v, m, g_mix, wt_in, w_dw_f, ln_g, ln_b, w_pool_b, s_pool, w_out_f, pair_sums, ts_mix_bwd)

    small_names = ["g_mix", "b_in", "b_dw", "ln_g", "ln_b", "w_pool", "s_pool", "g_ffn", "g_final"]
    small_shape = lambda p: p.reshape(-1, p.shape[-1])
    partial = [dg_mix, db_in, db_dw, dln_g, dln_b, dw_pool.reshape(-1, POOL_GROUP), ds_pool, dg_ffn, dg_final, loss_p]
    dw_out, *summed = _grad_matmul(y, dh1b, tm_grad, "grad_w_out", reduce_vals=partial)
    dwt_in, r_out, r_dw = _grad_matmul(
        dz, xn, tm_grad, "grad_w_in",
        reduce_parts=[dw_out.reshape(N_DEV, -1, D), _shard_cols(dw_dw[0:CONV_WIDTH])])
    r_in, = _tail_reduce([dwt_in.reshape(N_DEV, -1, D)], [])

    big = {}
    for nm, r in zip(_BIG, [r_in, r_out, r_gate, r_up, r_down]):
        rows = r.shape[1]
        res = _adamw_shard(r, _rows(nm, weights[nm]), _rows(nm, moms[nm]), _rows(nm, vars_[nm]),
                           rows // 2 if rows % 32 == 0 else rows, "adamw_" + nm)
        big[nm] = [o.T if nm in _COL_SHARDED else o for o in res]
    big["w_dw"] = _adamw_shard(r_dw, w_dw[0], moms["w_dw"][0], vars_["w_dw"][0], CONV_WIDTH, "adamw_w_dw")

    sm = _adamw_small(summed[:-1], [small_shape(weights[nm]) for nm in small_names],
                      [small_shape(moms[nm]) for nm in small_names], [small_shape(vars_[nm]) for nm in small_names])
    n_small = len(small_names)

    def result(kind, nm):
        if nm in big:
            return big[nm][kind].reshape(weights[nm].shape)
        return sm[kind * n_small + small_names.index(nm)].reshape(weights[nm].shape)

    loss = summed[-1][0, 0]
    out = [loss, dx.reshape(x.shape)]
    for kind in range(4):
        out += [result(kind, nm) for nm in names]
    return tuple(out)


def kernel(x, g_mix, w_in, b_in, w_dw, b_dw, ln_g, ln_b, w_pool, s_pool, w_out, g_ffn, w_gate, w_up, w_down, g_final, loss_target, m_g_mix, m_w_in, m_b_in, m_w_dw, m_b_dw, m_ln_g, m_ln_b, m_w_pool, m_s_pool, m_w_out, m_g_ffn, m_w_gate, m_w_up, m_w_down, m_g_final, v_g_mix, v_w_in, v_b_in, v_w_dw, v_b_dw, v_ln_g, v_ln_b, v_w_pool, v_s_pool, v_w_out, v_g_ffn, v_w_gate, v_w_up, v_w_down, v_g_final):
    args = (x, g_mix, w_in, b_in, w_dw, b_dw, ln_g, ln_b, w_pool, s_pool, w_out, g_ffn, w_gate, w_up, w_down, g_final, loss_target, m_g_mix, m_w_in, m_b_in, m_w_dw, m_b_dw, m_ln_g, m_ln_b, m_w_pool, m_s_pool, m_w_out, m_g_ffn, m_w_gate, m_w_up, m_w_down, m_g_final, v_g_mix, v_w_in, v_b_in, v_w_dw, v_b_dw, v_ln_g, v_ln_b, v_w_pool, v_s_pool, v_w_out, v_g_ffn, v_w_gate, v_w_up, v_w_down, v_g_final)
    return _step(args, ts_mix_fwd=512, ts_ffn=256, ts_mix_bwd=256, tm_grad=256)
```

```python
import functools

import jax
import jax.numpy as jnp
from jax import lax
from jax.experimental import pallas as pl
from jax.experimental.pallas import tpu as pltpu

F32 = jnp.float32
BF16 = jnp.bfloat16
MESH = pl.DeviceIdType.MESH
N_DEV = 8

C_CONV = 512
CONV_WIDTH = 31
POOL_WINDOWS = (2, 4, 8, 16)
POOL_GROUP = 128
RMS_EPS = 1e-6
LN_EPS = 1e-5

ADAM_LR = 0.001
ADAM_B1 = 0.9
ADAM_B2 = 0.999
ADAM_EPS = 1e-08
ADAM_WD = 0.01
ADAM_STEP = 10

HALO = 32
SUBLANES = 8
CONV_ROWS = 64
VMEM_LIMIT = 56 * 1024 * 1024


def _dot(a, b):
    return jnp.dot(a, b, preferred_element_type=F32)


def _dot_nt(a, b):
    return lax.dot_general(a, b, (((1,), (1,)), ((), ())), preferred_element_type=F32)


def _dot_tn(a, b):
    return lax.dot_general(a, b, (((0,), (0,)), ((), ())), preferred_element_type=F32)


def _mean_last(v):
    return jnp.mean(v, axis=-1, keepdims=True)


def _full(shape):
    nd = len(shape)
    return pl.BlockSpec(shape, lambda *_: (0,) * nd)


def _full1(shape):
    nd = len(shape)
    return pl.BlockSpec(shape, lambda *_: (0,) * nd, pipeline_mode=pl.Buffered(1))


def _shifted_copies(sh_ref, rows):
    for s in range(1, SUBLANES):
        sh_ref[s, 0:rows, :] = sh_ref[0, s:s + rows, :]


def _tap(sh_ref, off, r0, rows):
    q, s = divmod(off, SUBLANES)
    return sh_ref[s, pl.ds(r0 + q * SUBLANES, rows), :]


def _mix_fwd(x, g_mix, w_in, b_in, w_dw, b_dw, ln_g, ln_b, w_pool, s_pool, w_out, ag_shards, ts):
    S, D = x.shape
    d_in = w_in.shape[0]
    C = C_CONV
    nt = S // ts
    nrb = ts // CONV_ROWS
    n_ag = len(ag_shards)
    relay_step = (5 * nt) // 8

    def body(x_ref, g_ref, win_ref, bin_ref, wdw_ref, bdw_ref, lng_ref, lnb_ref, wp_ref, sp_ref, wout_ref, *rest):
        ag_in, rest = rest[:n_ag], rest[n_ag:]
        a_ref, gate_ref, v_ref, m_ref, y_ref, h1_ref, xn_ref = rest[:7]
        ag_out, rest = rest[7:7 + n_ag], rest[7 + n_ag:]
        ush, pbuf = rest[:2]
        ag_bufs, (send_sems, recv_sems, local_sems) = rest[2:2 + n_ag], rest[2 + n_ag:]
        i = pl.program_id(0)
        ag_start, ag_relay, ag_finish = _gather_plan(ag_in, ag_out, ag_bufs, send_sems, recv_sems, local_sems)

        @pl.when(i == 0)
        def _():
            ag_start()
            ush[0, 0:HALO, :] = jnp.zeros((HALO, C), F32)
            pbuf[0:HALO, :] = jnp.zeros((HALO, C), F32)

        @pl.when(i == relay_step)
        def _():
            ag_relay()

        x = x_ref[...]
        r1 = lax.rsqrt(_mean_last(x * x) + RMS_EPS)
        xn = (x * r1 * g_ref[...]).astype(BF16)
        xn_ref[...] = xn
        z = _dot_nt(xn, win_ref[...]) + bin_ref[...]
        a = z[:, 0:C]
        gate = z[:, C:2 * C]
        a_ref[...] = a
        gate_ref[...] = gate
        ush[0, HALO:HALO + ts, :] = a * jax.nn.sigmoid(gate)
        pbuf[HALO:HALO + ts, :] = z[:, 2 * C:]

        _shifted_copies(ush, ts + HALO - SUBLANES)

        def conv_block(rb, carry):
            r0 = pl.multiple_of(rb * CONV_ROWS, CONV_ROWS)
            acc = jnp.zeros((CONV_ROWS, C), F32)
            for k in range(CONV_WIDTH):
                acc = acc + wdw_ref[k:k + 1, :] * _tap(ush, HALO - (CONV_WIDTH - 1) + k, r0, CONV_ROWS)
            v_ref[pl.ds(r0, CONV_ROWS), :] = acc + bdw_ref[...]
            return carry

        lax.fori_loop(0, nrb, conv_block, 0)

        v = v_ref[...]
        mu = _mean_last(v)
        xc = v - mu
        rstd = lax.rsqrt(_mean_last(xc * xc) + LN_EPS)
        ln = xc * rstd * lng_ref[...] + lnb_ref[...]
        y_ref[:, 0:C] = (ln * jax.nn.sigmoid(ln)).astype(BF16)

        row = lax.broadcasted_iota(jnp.int32, (ts, 1), 0) + i * ts
        for gi, w in enumerate(POOL_WINDOWS):
            lanes = slice(gi * POOL_GROUP, (gi + 1) * POOL_GROUP)
            seg = pbuf[HALO:HALO + ts, lanes]
            ws = seg
            for k in range(1, w):
                ws = ws + pbuf[HALO - k:HALO - k + ts, lanes]
            cnt = jnp.minimum(row + 1, w).astype(F32)
            m = (ws / cnt - seg).astype(BF16)
            m_ref[:, lanes] = m
            ypre = _dot(m, wp_ref[gi])
            y_ref[:, C + gi * POOL_GROUP:C + (gi + 1) * POOL_GROUP] = (ypre * sp_ref[:, lanes]).astype(BF16)

        h1_ref[...] = x + _dot(y_ref[...], wout_ref[...])

        ush[0, 0:HALO, :] = ush[0, ts:ts + HALO, :]
        pbuf[0:HALO, :] = pbuf[ts:ts + HALO, :]

        @pl.when(i == nt - 1)
        def _():
            ag_finish()

    tile = lambda w, dt: (pl.BlockSpec((ts, w), lambda i: (i, 0)), jax.ShapeDtypeStruct((S, w), dt))
    outs = [tile(C, F32), tile(C, F32), tile(C, F32), tile(C, BF16), tile(D, BF16), tile(D, F32), tile(D, BF16)]
    return pl.pallas_call(
        body, name="mix_fwd", grid=(nt,),
        in_specs=[pl.BlockSpec((ts, D), lambda i: (i, 0)), _full((1, D)), _full((d_in, D)), _full((1, d_in)),
                  _full(w_dw.shape), _full((1, C)), _full((1, C)), _full((1, C)), _full(w_pool.shape),
                  _full((1, C)), _full((D, D))] + [_HBM] * n_ag,
        out_specs=[o[0] for o in outs] + [_HBM] * n_ag,
        out_shape=[o[1] for o in outs] + _gather_out_shapes(ag_shards),
        scratch_shapes=[pltpu.VMEM((SUBLANES, ts + HALO, C), F32), pltpu.VMEM((ts + HALO, C), F32)]
        + _bounce_bufs(ag_shards) + _comm_sems(n_ag, 7),
        compiler_params=pltpu.CompilerParams(dimension_semantics=("arbitrary",), vmem_limit_bytes=VMEM_LIMIT),
    )(x, g_mix, w_in, b_in, w_dw, b_dw, ln_g, ln_b, w_pool, s_pool, w_out, *ag_shards)


def _ffn(h1, target, g_ffn, g_final, w_gate, w_up, w_down, ts, f_chunks):
    S, D = h1.shape
    Fd = w_gate.shape[0]
    nt = S // ts
    bounds = []
    lo = 0
    for n in f_chunks:
        bounds.append((lo, lo + n))
        lo += n
    assert lo == Fd

    def body(h1_ref, tgt_ref, gf_ref, gl_ref, wg_ref, wu_ref, wd_ref,
             dh1_ref, hn_ref, act_ref, dgt_ref, dup_ref, dh2_ref, loss_ref, dgl_ref, dgf_ref, gt_s, up_s):
        i = pl.program_id(0)

        @pl.when(i == 0)
        def _():
            loss_ref[...] = jnp.zeros_like(loss_ref)
            dgl_ref[...] = jnp.zeros_like(dgl_ref)
            dgf_ref[...] = jnp.zeros_like(dgf_ref)

        h1 = h1_ref[...]
        r2 = lax.rsqrt(_mean_last(h1 * h1) + RMS_EPS)
        hhat = h1 * r2
        hn = (hhat * gf_ref[...]).astype(BF16)
        hn_ref[...] = hn
        h2 = h1
        for lo, hi in bounds:
            gt = _dot_nt(hn, wg_ref[lo:hi, :])
            up = _dot_nt(hn, wu_ref[lo:hi, :])
            gt_s[:, lo:hi] = gt
            up_s[:, lo:hi] = up
            act = (gt * jax.nn.sigmoid(gt) * up).astype(BF16)
            act_ref[:, lo:hi] = act
            h2 = h2 + _dot(act, wd_ref[lo:hi, :])

        r3 = lax.rsqrt(_mean_last(h2 * h2) + RMS_EPS)
        n3 = h2 * r3
        gl = gl_ref[...]
        diff = n3 * gl - tgt_ref[...]
        loss_ref[...] += jnp.sum(0.5 * jnp.sum(diff * diff, axis=-1, keepdims=True) / D, axis=0, keepdims=True)
        dout = diff / D
        dgl_ref[...] += jnp.sum(dout * n3, axis=0, keepdims=True)
        dn = dout * gl
        dh2 = r3 * (dn - n3 * _mean_last(dn * n3))
        dh2b = dh2.astype(BF16)
        dh2_ref[...] = dh2b

        dhn = jnp.zeros((ts, D), F32)
        for lo, hi in bounds:
            gt = gt_s[:, lo:hi]
            up = up_s[:, lo:hi]
            sg = jax.nn.sigmoid(gt)
            dact = _dot_nt(dh2b, wd_ref[lo:hi, :])
            dgt = (dact * up * (sg * (1.0 + gt * (1.0 - sg)))).astype(BF16)
            dup = (dact * (gt * sg)).astype(BF16)
            dgt_ref[:, lo:hi] = dgt
            dup_ref[:, lo:hi] = dup
            dhn = dhn + _dot(dgt, wg_ref[lo:hi, :]) + _dot(dup, wu_ref[lo:hi, :])

        dgf_ref[...] += jnp.sum(dhn * hhat, axis=0, keepdims=True)
        dnn = dhn * gf_ref[...]
        dh1_ref[...] = dh2 + r2 * (dnn - hhat * _mean_last(dnn * hhat))

    tile = lambda w, dt: (pl.BlockSpec((ts, w), lambda i: (i, 0)), jax.ShapeDtypeStruct((S, w), dt))
    acc = lambda w: (_full((1, w)), jax.ShapeDtypeStruct((1, w), F32))
    outs = [tile(D, F32), tile(D, BF16), tile(Fd, BF16), tile(Fd, BF16), tile(Fd, BF16), tile(D, BF16),
            acc(128), acc(D), acc(D)]
    return pl.pallas_call(
        body, name="ffn_fwd_bwd", grid=(nt,),
        in_specs=[pl.BlockSpec((ts, D), lambda i: (i, 0)), pl.BlockSpec((ts, D), lambda i: (i, 0)),
                  _full((1, D)), _full((1, D)), _full1((Fd, D)), _full1((Fd, D)), _full1((Fd, D))],
        out_specs=[o[0] for o in outs], out_shape=[o[1] for o in outs],
        scratch_shapes=[pltpu.VMEM((ts, Fd), F32), pltpu.VMEM((ts, Fd), F32)],
        compiler_params=pltpu.CompilerParams(dimension_semantics=("arbitrary",), vmem_limit_bytes=VMEM_LIMIT),
    )(h1, target, g_ffn, g_final, w_gate, w_up, w_down)


def _mix_bwd(dh1, x, a, gate, v, m, g_mix, w_in, w_dw, ln_g, ln_b, w_pool, s_pool, w_out, rs_parts, ts):
    S, D = x.shape
    n_rs = len(rs_parts)
    d_in = w_in.shape[0]
    C = C_CONV
    nt = S // ts
    nrb = ts // CONV_ROWS
    wrows =((CONV_WIDTH + SUBLANES - 1) // SUBLANES) * SUBLANES

    def body(dh1_ref, x_ref, a_ref, gate_ref, v_ref, m_ref, g_ref, win_ref, wdw_ref, lng_ref,
             lnb_ref, wp_ref, sp_ref, wout_ref, *rest):
        rs_in, rest = rest[:n_rs], rest[n_rs:]
        (dx_ref, dz_ref, dh1b_ref, dgm_ref, dbin_ref, dwdw_ref, dbdw_ref, dlng_ref, dlnb_ref, dwp_ref,
         dsp_ref) = rest[:11]
        rs_out, rest = rest[11:11 + n_rs], rest[11 + n_rs:]
        dvsh, dqbuf, du_s = rest[:3]
        rs_bufs, (send_sems, recv_sems, local_sems) = rest[3:3 + n_rs], rest[3 + n_rs:]
        i = pl.program_id(0)
        t = nt - 1 - i
        rs_start, rs_finish = _chip_exchange_plan(rs_in, rs_out, rs_bufs, send_sems, recv_sems, local_sems)

        @pl.when(i == 0)
        def _():
            rs_start()
            dvsh[0, ts:ts + HALO, :] = jnp.zeros((HALO, C), F32)
            dqbuf[ts:ts + HALO, :] = jnp.zeros((HALO, C), F32)
            for r in (dgm_ref, dbin_ref, dwdw_ref, dbdw_ref, dlng_ref, dlnb_ref, dwp_ref, dsp_ref):
                r[...] = jnp.zeros_like(r)

        dh1 = dh1_ref[...]
        dh1b = dh1.astype(BF16)
        dh1b_ref[...] = dh1b
        dy = _dot_nt(dh1b, wout_ref[...])

        v = v_ref[...]
        mu = _mean_last(v)
        xc = v - mu
        rstd = lax.rsqrt(_mean_last(xc * xc) + LN_EPS)
        vhat = xc * rstd
        lng = lng_ref[...]
        ln = vhat * lng + lnb_ref[...]
        sg = jax.nn.sigmoid(ln)
        dln = dy[:, 0:C] * (sg * (1.0 + ln * (1.0 - sg)))
        dlng_ref[...] += jnp.sum(dln * vhat, axis=0, keepdims=True)
        dlnb_ref[...] += jnp.sum(dln, axis=0, keepdims=True)
        dvh = dln * lng
        dv = rstd * (dvh - _mean_last(dvh) - vhat * _mean_last(dvh * vhat))
        dbdw_ref[...] += jnp.sum(dv, axis=0, keepdims=True)
        dvsh[0, 0:ts, :] = dv
        _shifted_copies(dvsh, ts + HALO - SUBLANES)

        def conv_block(rb, carry):
            r0 = pl.multiple_of(rb * CONV_ROWS, CONV_ROWS)
            acc = jnp.zeros((CONV_ROWS, C), F32)
            for k in range(CONV_WIDTH):
                acc = acc + wdw_ref[k:k + 1, :] * _tap(dvsh, CONV_WIDTH - 1 - k, r0, CONV_ROWS)
            du_s[pl.ds(r0, CONV_ROWS), :] = acc
            return carry

        lax.fori_loop(0, nrb, conv_block, 0)

        a = a_ref[...]
        sgate = jax.nn.sigmoid(gate_ref[...])
        u = a * sgate
        for k in range(CONV_WIDTH):
            q, s = divmod(CONV_WIDTH - 1 - k, SUBLANES)
            prod = u * dvsh[s, q * SUBLANES:q * SUBLANES + ts, :]
            dwdw_ref[k:k + 1, :] += jnp.sum(prod, axis=0, keepdims=True)

        du = du_s[...]
        da = du * sgate
        dgate = du * a * sgate * (1.0 - sgate)
        dz_ref[:, 0:C] = da.astype(BF16)
        dz_ref[:, C:2 * C] = dgate.astype(BF16)
        dbin_ref[:, 0:C] += jnp.sum(da, axis=0, keepdims=True)
        dbin_ref[:, C:2 * C] += jnp.sum(dgate, axis=0, keepdims=True)

        row = lax.broadcasted_iota(jnp.int32, (ts, 1), 0) + t * ts
        for gi, w in enumerate(POOL_WINDOWS):
            lanes = slice(gi * POOL_GROUP, (gi + 1) * POOL_GROUP)
            dyp = dy[:, C + gi * POOL_GROUP:C + (gi + 1) * POOL_GROUP]
            mg = m_ref[:, lanes]
            ypre = _dot(mg, wp_ref[gi])
            dsp_ref[:, lanes] += jnp.sum(dyp * ypre, axis=0, keepdims=True)
            dyi = (dyp * sp_ref[:, lanes]).astype(BF16)
            dwp_ref[gi] += _dot_tn(mg, dyi)
            dm = _dot_nt(dyi, wp_ref[gi])
            cnt = jnp.minimum(row + 1, w).astype(F32)
            dqbuf[0:ts, lanes] = dm / cnt
            dp = -dm
            for k in range(w):
                dp = dp + dqbuf[k:k + ts, lanes]
            dz_ref[:, 2 * C + gi * POOL_GROUP:2 * C + (gi + 1) * POOL_GROUP] = dp.astype(BF16)
            dbin_ref[:, 2 * C + gi * POOL_GROUP:2 * C + (gi + 1) * POOL_GROUP] += jnp.sum(dp, axis=0, keepdims=True)

        dxn = _dot(dz_ref[...], win_ref[...])
        x = x_ref[...]
        r1 = lax.rsqrt(_mean_last(x * x) + RMS_EPS)
        xhat = x * r1
        dgm_ref[...] += jnp.sum(dxn * xhat, axis=0, keepdims=True)
        dnn = dxn * g_ref[...]
        dx_ref[...] = dh1 + r1 * (dnn - xhat * _mean_last(dnn * xhat))

        dvsh[0, ts:ts + HALO, :] = dvsh[0, 0:HALO, :]
        dqbuf[ts:ts + HALO, :] = dqbuf[0:HALO, :]

        @pl.when(i == nt - 1)
        def _():
            rs_finish()

    rev = lambda w: pl.BlockSpec((ts, w), lambda i: (nt - 1 - i, 0))
    acc = lambda shape: (_full(shape), jax.ShapeDtypeStruct(shape, F32))
    outs = [(rev(D), jax.ShapeDtypeStruct((S, D), F32)), (rev(d_in), jax.ShapeDtypeStruct((S, d_in), BF16)),
            (rev(D), jax.ShapeDtypeStruct((S, D), BF16)), acc((1, D)), acc((1, d_in)), acc((wrows, C)),
            acc((1, C)), acc((1, C)), acc((1, C)), acc(w_pool.shape), acc((1, C))]
    return pl.pallas_call(
        body, name="mix_bwd", grid=(nt,),
        in_specs=[rev(D), rev(D), rev(C), rev(C), rev(C), rev(C), _full((1, D)), _full((d_in, D)),
                  _full(w_dw.shape), _full((1, C)), _full((1, C)), _full(w_pool.shape), _full((1, C)),
                  _full((D, D))] + [_HBM] * n_rs,
        out_specs=[o[0] for o in outs] + [_HBM] * n_rs,
        out_shape=[o[1] for o in outs] + [jax.ShapeDtypeStruct(p.shape, p.dtype) for p in rs_parts],
        scratch_shapes=[pltpu.VMEM((SUBLANES, ts + HALO, C), F32), pltpu.VMEM((ts + HALO, C), F32),
                        pltpu.VMEM((ts, C), F32)]
        + [pltpu.VMEM(p.shape[1:], p.dtype) for p in rs_parts] + _comm_sems(n_rs, 3),
        compiler_params=pltpu.CompilerParams(dimension_semantics=("arbitrary",), vmem_limit_bytes=VMEM_LIMIT),
    )(dh1, x, a, gate, v, m, g_mix, w_in, w_dw, ln_g, ln_b, w_pool, s_pool, w_out, *rs_parts)


def _grad_matmul(a, b, tm, name, reduce_parts=(), reduce_vals=()):
    S, M = a.shape
    N = b.shape[1]
    steps = M // tm
    n, ns = len(reduce_parts), len(reduce_vals)

    def body(a_ref, b_ref, *rest):
        o_ref = rest[n + ns]
        if n + ns:
            start, middle, finish = _reduce_plan(rest[:n], rest[n:n + ns], rest[n + ns + 1:2 * n + ns + 1],
                                                 rest[2 * n + ns + 1:2 * (n + ns) + 1], rest[2 * (n + ns) + 1:])
            i = pl.program_id(0)
            pl.when(i == 0)(start)
            pl.when(i == 1)(middle)
        o_ref[...] = _dot_tn(a_ref[...], b_ref[...]).astype(BF16)
        if n + ns:
            pl.when(i == steps - 1)(finish)

    res = pl.pallas_call(
        body, name=name, grid=(steps,),
        in_specs=[pl.BlockSpec((S, tm), lambda i: (0, i)), _full1((S, N))] + [_VMEM] * (n + ns),
        out_specs=[pl.BlockSpec((tm, N), lambda i: (i, 0))] + [_VMEM] * (n + ns),
        out_shape=[jax.ShapeDtypeStruct((M, N), BF16)] + _reduce_out_shapes(reduce_parts, reduce_vals),
        scratch_shapes=_reduce_scratch(reduce_parts, reduce_vals) if n + ns else [],
        compiler_params=pltpu.CompilerParams(dimension_semantics=("arbitrary",) if n + ns else ("parallel",),
                                             vmem_limit_bytes=VMEM_LIMIT),
    )(a, b, *reduce_parts, *reduce_vals)
    return res if n + ns else res[0]


def _grad_matmul_pair_exchange(a, b, tm, name, parts):
    S, M = a.shape
    N = b.shape[1]
    steps = M // tm
    n = len(parts)

    def body(a_ref, b_ref, *rest):
        o_ref = rest[n]
        start, finish = _pair_exchange_plan(rest[:n], rest[n + 1:2 * n + 1], *rest[2 * n + 1:])
        i = pl.program_id(0)
        pl.when(i == 0)(start)
        o_ref[...] = _dot_tn(a_ref[...], b_ref[...]).astype(BF16)
        pl.when(i == steps - 1)(finish)

    sems = pltpu.SemaphoreType.DMA((n, N_CHIPS))
    return pl.pallas_call(
        body, name=name, grid=(steps,),
        in_specs=[pl.BlockSpec((S, tm), lambda i: (0, i)), _full1((S, N))] + [_HBM] * n,
        out_specs=[pl.BlockSpec((tm, N), lambda i: (i, 0))] + [_HBM] * n,
        out_shape=[jax.ShapeDtypeStruct((M, N), BF16)] + _pair_out_shapes(parts),
        scratch_shapes=[sems, sems],
        compiler_params=pltpu.CompilerParams(dimension_semantics=("arbitrary",), vmem_limit_bytes=VMEM_LIMIT),
    )(a, b, *parts)


def _position():
    return lax.axis_index("x"), lax.axis_index("y"), lax.axis_index("c")


def _slot(px, py, pc):
    return 4 * px + 2 * py + pc


def _all_gather(shards):
    n = len(shards)

    def body(*refs):
        start, relay, finish = _gather_plan(refs[:n], refs[n:2 * n], refs[2 * n:3 * n], *refs[3 * n:])
        start()
        relay()
        finish()

    return pl.pallas_call(
        body, name="all_gather_mix_weights",
        out_shape=_gather_out_shapes(shards), in_specs=[_HBM] * n, out_specs=[_HBM] * n,
        scratch_shapes=_bounce_bufs(shards) + _comm_sems(n, 7),
    )(*shards)


_HBM = pl.BlockSpec(memory_space=pl.ANY)


def _comm_sems(n, copies):
    return [pltpu.SemaphoreType.DMA((n, copies)), pltpu.SemaphoreType.DMA((n, copies)),
            pltpu.SemaphoreType.DMA((n, 2))]


def _bounce_bufs(blocks):
    return [pltpu.VMEM(b.shape, b.dtype) for b in blocks]


def _local_copy(srcs, dsts, bufs, local_sems):
    n = len(srcs)
    loads = [pltpu.make_async_copy(srcs[k], bufs[k], local_sems.at[k, 0]) for k in range(n)]
    for cp in loads:
        cp.start()
    for cp in loads:
        cp.wait()
    stores = _local_stores(dsts, bufs, local_sems)
    for cp in stores:
        cp.start()
    return stores


def _local_stores(dsts, bufs, local_sems):
    return [pltpu.make_async_copy(bufs[k], dsts[k], local_sems.at[k, 1]) for k in range(len(dsts))]


def _gather_out_shapes(shards):
    return [jax.ShapeDtypeStruct((N_DEV,) + s.shape, s.dtype) for s in shards]


def _gather_plan(ins, outs, bufs, send_sems, recv_sems, local_sems):
    n = len(ins)
    x, y, c = _position()
    me, sibling = (x, y, c), (x, y, 1 - c)
    na, nb, dg = (x ^ (1 - c), y ^ c), (x ^ c, y ^ (1 - c)), (1 - x, 1 - y)
    own = [outs[k].at[_slot(*me)] for k in range(n)]

    def copy(k, sem, block, to, src=None):
        dst = outs[k].at[_slot(*block)]
        return pltpu.make_async_remote_copy(
            src_ref=dst if src is None else src, dst_ref=dst, send_sem=send_sems.at[k, sem],
            recv_sem=recv_sems.at[k, sem], device_id=to, device_id_type=MESH)

    def first():
        cps = []
        for k in range(n):
            cps += [copy(k, 0, me, sibling, src=ins[k]), copy(k, 1, me, (*na, c), src=ins[k]),
                    copy(k, 2, me, (*nb, c), src=ins[k])]
        return cps

    def onward():
        return [copy(k, 3, (*na, c), (*nb, c)) for k in range(n)]

    def to_sibling(j, chip):
        return [copy(k, 4 + j, (*chip, c), sibling) for k in range(n)]

    def start():
        for cp in first():
            cp.start()
        _local_copy(ins, own, bufs, local_sems)

    def relay():
        for k in range(n):
            copy(k, 1, (*na, c), me).wait_recv()
        for cp in onward() + to_sibling(0, na):
            cp.start()
        for k in range(n):
            copy(k, 2, (*nb, c), me).wait_recv()
        for cp in to_sibling(1, nb):
            cp.start()

    def finish():
        for k in range(n):
            copy(k, 3, (*dg, c), me).wait_recv()
        for cp in to_sibling(2, dg):
            cp.start()
        for k in range(n):
            copy(k, 0, sibling, me).wait_recv()
            for j, chip in enumerate((nb, na, dg)):
                copy(k, 4 + j, (*chip, 1 - c), me).wait_recv()
        for cp in first() + onward() + to_sibling(0, na) + to_sibling(1, nb) + to_sibling(2, dg):
            cp.wait_send()
        for cp in _local_stores(own, bufs, local_sems):
            cp.wait()

    return start, relay, finish


N_CHIPS = 4
_CHIP_FLIPS = [(1, 0), (0, 1), (1, 1)]
F32_TRAVEL_LIMIT = 4096


def _chip_exchange_plan(ins, outs, bufs, send_sems, recv_sems, local_sems):
    n = len(ins)
    x, y, c = _position()
    my_q = 2 * x + y
    peers = [(x ^ fx, y ^ fy) for fx, fy in _CHIP_FLIPS]
    own = [outs[k].at[my_q] for k in range(n)]

    def sends():
        return [pltpu.make_async_remote_copy(
            src_ref=ins[k].at[2 * px + py], dst_ref=outs[k].at[my_q], send_sem=send_sems.at[k, f],
            recv_sem=recv_sems.at[k, f], device_id=(px, py, c), device_id_type=MESH)
            for f, (px, py) in enumerate(peers) for k in range(n)]

    def start():
        for cp in sends():
            cp.start()
        _local_copy([ins[k].at[my_q] for k in range(n)], own, bufs, local_sems)

    def finish():
        for f, (px, py) in enumerate(peers):
            for k in range(n):
                pltpu.make_async_remote_copy(
                    src_ref=ins[k].at[my_q], dst_ref=outs[k].at[2 * px + py], send_sem=send_sems.at[k, f],
                    recv_sem=recv_sems.at[k, f], device_id=(px, py, c), device_id_type=MESH).wait_recv()
        for cp in sends():
            cp.wait_send()
        for cp in _local_stores(own, bufs, local_sems):
            cp.wait()

    return start, finish


def _pair_exchange(parts):
    n = len(parts)

    def body(*refs):
        start, finish = _pair_exchange_plan(refs[:n], refs[n:2 * n], *refs[2 * n:])
        start()
        finish()

    sems = pltpu.SemaphoreType.DMA((n, N_CHIPS))
    return pl.pallas_call(
        body, name="pair_exchange", out_shape=_pair_out_shapes(parts),
        in_specs=[_HBM] * n, out_specs=[_HBM] * n, scratch_shapes=[sems, sems],
    )(*parts)


def _pair_out_shapes(parts):
    return [jax.ShapeDtypeStruct((N_CHIPS,) + p.shape[1:], p.dtype) for p in parts]


def _pair_exchange_plan(ins, sib, send_sems, recv_sems):
    x, y, c = _position()

    def copies():
        return [pltpu.make_async_remote_copy(
            src_ref=ins[k].at[2 * q + 1 - c], dst_ref=sib[k].at[q], send_sem=send_sems.at[k, q],
            recv_sem=recv_sems.at[k, q], device_id=(x, y, 1 - c), device_id_type=MESH)
            for k in range(len(ins)) for q in range(N_CHIPS)]

    def start():
        for cp in copies():
            cp.start()

    def finish():
        for cp in copies():
            cp.wait()

    return start, finish


def _pair_sum(parts, sib):
    n = len(parts)

    def body(*refs):
        c = lax.axis_index("c")
        for k in range(n):
            refs[2 * n + k][0] = (refs[k][0, c].astype(F32) + refs[n + k][0].astype(F32)).astype(BF16)

    pair = [pl.BlockSpec((1, 2) + p.shape[1:], lambda q: (q, 0, 0, 0)) for p in parts]
    one = [pl.BlockSpec((1,) + p.shape[1:], lambda q: (q, 0, 0)) for p in parts]
    return pl.pallas_call(
        body, name="pair_sum", grid=(N_CHIPS,), in_specs=pair + one, out_specs=one,
        out_shape=[jax.ShapeDtypeStruct(s.shape, BF16) for s in sib],
        compiler_params=pltpu.CompilerParams(dimension_semantics=("parallel",)),
    )(*[p.reshape((N_CHIPS, 2) + p.shape[1:]) for p in parts], *sib)


def _tail_reduce(parts, vals):
    n, ns = len(parts), len(vals)

    def body(*refs):
        start, middle, finish = _reduce_plan(refs[:n], refs[n:n + ns], refs[n + ns:2 * n + ns],
                                             refs[2 * n + ns:2 * (n + ns)], refs[2 * (n + ns):])
        start()
        middle()
        finish()

    return pl.pallas_call(
        body, name="tail_reduce", out_shape=_reduce_out_shapes(parts, vals),
        in_specs=[_VMEM] * (n + ns), out_specs=[_VMEM] * (n + ns), scratch_shapes=_reduce_scratch(parts, vals),
        compiler_params=pltpu.CompilerParams(vmem_limit_bytes=VMEM_LIMIT),
    )(*parts, *vals)


_VMEM = pl.BlockSpec(memory_space=pltpu.VMEM)


def _reduce_out_shapes(parts, vals):
    return ([jax.ShapeDtypeStruct((N_CHIPS,) + p.shape[1:], p.dtype) for p in parts]
            + [jax.ShapeDtypeStruct(v.shape, v.dtype) for v in vals])


def _reduce_scratch(parts, vals):
    n, ns = len(parts), len(vals)
    quarter = [pltpu.VMEM((N_CHIPS,) + p.shape[1:], p.dtype) for p in parts]
    dma = pltpu.SemaphoreType.DMA
    travel = [BF16 if v.size > F32_TRAVEL_LIMIT else v.dtype for v in vals]
    return (quarter * 3 + [pltpu.VMEM(v.shape, v.dtype) for v in vals]
            + [pltpu.VMEM(v.shape, t) for v, t in zip(vals, travel)]
            + [pltpu.VMEM((N_CHIPS,) + v.shape, t) for v, t in zip(vals, travel)]
            + [dma((max(n, 1), N_CHIPS)), dma((max(n, 1), N_CHIPS)), dma((max(ns, 1),)), dma((max(ns, 1),)),
               dma((max(n, 1), 3)), dma((max(n, 1), 3)), dma((max(ns, 1), 3)), dma((max(ns, 1), 3))])


def _reduce_plan(p_in, v_in, p_out, v_out, scratch):
    n, ns = len(p_in), len(v_in)
    p_sib, p_sum, p_all = scratch[:n], scratch[n:2 * n], scratch[2 * n:3 * n]
    v_sib, v_sum, v_all = (scratch[3 * n + j * ns:3 * n + (j + 1) * ns] for j in range(3))
    p1_send, p1_recv, v1_send, v1_recv, p3_send, p3_recv, v3_send, v3_recv = scratch[3 * n + 3 * ns:]
    x, y, c = _position()
    my_q = 2 * x + y
    peers = [(x ^ fx, y ^ fy) for fx, fy in _CHIP_FLIPS]

    def to_sibling(src, dst, send, recv):
        return pltpu.make_async_remote_copy(src_ref=src, dst_ref=dst, send_sem=send, recv_sem=recv,
                                            device_id=(x, y, 1 - c), device_id_type=MESH)

    def level1():
        cps = [to_sibling(p_in[k].at[2 * q + 1 - c], p_sib[k].at[q], p1_send.at[k, q], p1_recv.at[k, q])
               for k in range(n) for q in range(N_CHIPS)]
        return cps + [to_sibling(v_in[k], v_sib[k], v1_send.at[k], v1_recv.at[k]) for k in range(ns)]

    def to_chip(f, src, dst, send, recv):
        px, py = peers[f]
        return pltpu.make_async_remote_copy(src_ref=src, dst_ref=dst, send_sem=send, recv_sem=recv,
                                            device_id=(px, py, c), device_id_type=MESH)

    def level2(sending):
        cps = []
        for f, (px, py) in enumerate(peers):
            their_q = 2 * px + py
            for k in range(n):
                src, dst = (p_sum[k].at[their_q], p_all[k].at[my_q]) if sending else (
                    p_sum[k].at[my_q], p_all[k].at[their_q])
                cps.append(to_chip(f, src, dst, p3_send.at[k, f], p3_recv.at[k, f]))
            for k in range(ns):
                dst = v_all[k].at[my_q] if sending else v_all[k].at[their_q]
                cps.append(to_chip(f, v_sum[k], dst, v3_send.at[k, f], v3_recv.at[k, f]))
        return cps

    def start():
        for cp in level1():
            cp.start()

    def middle():
        for cp in level1():
            cp.wait_recv()
        for k in range(n):
            for q in range(N_CHIPS):
                p_sum[k][q] = (p_in[k][2 * q + c].astype(F32) + p_sib[k][q].astype(F32)).astype(p_sum[k].dtype)
        for k in range(ns):
            v_sum[k][...] = (v_in[k][...] + v_sib[k][...]).astype(v_sum[k].dtype)
        for cp in level2(True):
            cp.start()
        for k in range(n):
            p_all[k][my_q] = p_sum[k][my_q]
        for k in range(ns):
            v_all[k][my_q] = v_sum[k][...]

    def finish():
        for cp in level2(False):
            cp.wait_recv()
        for k in range(n):
            p_out[k][...] = p_all[k][...]
        for k in range(ns):
            total = v_all[k][0].astype(F32)
            for q in range(1, N_CHIPS):
                total = total + v_all[k][q].astype(F32)
            v_out[k][...] = total
        for cp in level1() + level2(True):
            cp.wait_send()

    return start, middle, finish


def _adamw_math(w, g, m, v):
    m = ADAM_B1 * m + (1.0 - ADAM_B1) * g
    v = ADAM_B2 * v + (1.0 - ADAM_B2) * (g * g)
    m_hat = m / (1.0 - ADAM_B1 ** ADAM_STEP)
    v_hat = v / (1.0 - ADAM_B2 ** ADAM_STEP)
    delta = -ADAM_LR * (m_hat / (jnp.sqrt(v_hat) + ADAM_EPS) + ADAM_WD * w)
    return delta, m, v


def _adamw_shard(parts, w, m, v, tr, name):
    R, Cc = w.shape

    def body(p_ref, w_ref, m_ref, v_ref, g_out, d_out, m_out, v_out):
        g = p_ref[0].astype(F32)
        for j in range(1, N_CHIPS):
            g = g + p_ref[j].astype(F32)
        d, mn, vn = _adamw_math(w_ref[...], g, m_ref[...], v_ref[...])
        g_out[...] = g
        d_out[...] = d
        m_out[...] = mn
        v_out[...] = vn

    blk = pl.BlockSpec((tr, Cc), lambda i: (i, 0))
    return pl.pallas_call(
        body, name=name, grid=(R // tr,),
        in_specs=[pl.BlockSpec((N_CHIPS, tr, Cc), lambda i: (0, i, 0)), blk, blk, blk],
        out_specs=[blk] * 4, out_shape=[jax.ShapeDtypeStruct((R, Cc), F32)] * 4,
        compiler_params=pltpu.CompilerParams(dimension_semantics=("parallel",)),
    )(parts, w, m, v)


def _adamw_small(grads, ws, ms, vs):
    n = len(grads)
    vm = pl.BlockSpec(memory_space=pltpu.VMEM)

    def body(*refs):
        g_in, w_in, m_in, v_in = (refs[k * n:(k + 1) * n] for k in range(4))
        g_out, d_out, m_out, v_out = (refs[(4 + k) * n:(5 + k) * n] for k in range(4))
        for k in range(n):
            g = g_in[k][...]
            d, mn, vn = _adamw_math(w_in[k][...], g, m_in[k][...], v_in[k][...])
            g_out[k][...] = g
            d_out[k][...] = d
            m_out[k][...] = mn
            v_out[k][...] = vn

    shapes = [jax.ShapeDtypeStruct(g.shape, F32) for g in grads]
    return pl.pallas_call(
        body, name="adamw_small", out_shape=shapes * 4, in_specs=[vm] * (4 * n), out_specs=[vm] * (4 * n),
    )(*grads, *ws, *ms, *vs)


def _unshard_cols(g):
    n, R, Cc = g.shape
    return jnp.transpose(g, (1, 0, 2)).reshape(R, n * Cc)


def _shard_cols(full):
    R, Ct = full.shape
    return jnp.transpose(full.reshape(R, N_DEV, Ct // N_DEV), (1, 0, 2))


_COL_SHARDED = ("w_in", "w_gate", "w_up")
_BIG = ("w_in", "w_out", "w_gate", "w_up", "w_down")


def _rows(nm, p):
    return p[0].T if nm in _COL_SHARDED else p[0]


def _step(args, ts_mix_fwd, ts_ffn, ts_mix_bwd, tm_grad):
    (x, g_mix, w_in, b_in, w_dw, b_dw, ln_g, ln_b, w_pool, s_pool, w_out, g_ffn, w_gate, w_up, w_down, g_final,
     loss_target) = args[:17]
    names = ["g_mix", "w_in", "b_in", "w_dw", "b_dw", "ln_g", "ln_b", "w_pool", "s_pool", "w_out", "g_ffn",
             "w_gate", "w_up", "w_down", "g_final"]
    weights = dict(zip(names, args[1:16]))
    moms = dict(zip(names, args[17:32]))
    vars_ = dict(zip(names, args[32:47]))

    S, D = x.shape[1], x.shape[2]
    x2 = x.reshape(S, D)
    tgt2 = loss_target.reshape(S, D)

    bf = lambda nm: _rows(nm, weights[nm]).astype(BF16)
    g_in, g_out, g_dw = _all_gather([bf("w_in"), bf("w_out"), w_dw[0]])
    wt_in, w_out_f = g_in.reshape(-1, D), g_out.reshape(-1, D)
    w_dw_f = _unshard_cols(g_dw)
    w_pool_b = w_pool[0].astype(BF16)

    a, gate, v, m, y, h1, xn, g_gate, g_up, g_down = _mix_fwd(
        x2, g_mix, wt_in, b_in, w_dw_f, b_dw, ln_g, ln_b, w_pool_b, s_pool, w_out_f,
        [bf("w_gate"), bf("w_up"), bf("w_down")], ts_mix_fwd)
    wt_gate, wt_up, w_down_f = g_gate.reshape(-1, D), g_up.reshape(-1, D), g_down.reshape(-1, D)
    Fd = wt_gate.shape[0]
    f_chunks = [1024] * (Fd // 1024) + ([Fd % 1024] if Fd % 1024 else [])
    dh1, hn, act, dgt, dup, dh2, loss_p, dg_final, dg_ffn = _ffn(
        h1, tgt2, g_ffn, g_final.reshape(1, D), wt_gate, wt_up, w_down_f, ts_ffn, f_chunks)

    by_shard = lambda g: g.reshape(N_DEV, -1, D)
    p_gate = by_shard(_grad_matmul(dgt, hn, tm_grad, "grad_w_gate"))
    g_up, s_gate = _grad_matmul_pair_exchange(dup, hn, tm_grad, "grad_w_up", [p_gate])
    p_up = by_shard(g_up)
    g_down, s_up = _grad_matmul_pair_exchange(act, dh2, tm_grad, "grad_w_down", [p_up])
    p_down = by_shard(g_down)
    s_down, = _pair_exchange([p_down])
    pair_sums = _pair_sum([p_gate, p_up, p_down], [s_gate, s_up, s_down])
    (dx, dz, dh1b, dg_mix, db_in, dw_dw, db_dw, dln_g, dln_b, dw_pool, ds_pool, r_gate, r_up, r_down) = _mix_bwd(
        dh1, x2, a, gate, v, m, g_mix, wt_in, w_dw_f, ln_g, ln_b, w_pool_b, s_pool, w_out_f, pair_sums, ts_mix_bwd)

    small_names = ["g_mix", "b_in", "b_dw", "ln_g", "ln_b", "w_pool", "s_pool", "g_ffn", "g_final"]
    small_shape = lambda p: p.reshape(-1, p.shape[-1])
    partial = [dg_mix, db_in, db_dw, dln_g, dln_b, dw_pool.reshape(-1, POOL_GROUP), ds_pool, dg_ffn, dg_final, loss_p]
    dw_out, *summed = _grad_matmul(y, dh1b, tm_grad, "grad_w_out", reduce_vals=partial)
    dwt_in, r_out, r_dw = _grad_matmul(
        dz, xn, tm_grad, "grad_w_in",
        reduce_parts=[dw_out.reshape(N_DEV, -1, D), _shard_cols(dw_dw[0:CONV_WIDTH])])
    r_in, = _tail_reduce([dwt_in.reshape(N_DEV, -1, D)], [])

    big = {}
    for nm, r in zip(_BIG, [r_in, r_out, r_gate, r_up, r_down]):
        rows = r.shape[1]
        res = _adamw_shard(r, _rows(nm, weights[nm]), _rows(nm, moms[nm]), _rows(nm, vars_[nm]),
                           rows // 2 if rows % 32 == 0 else rows, "adamw_" + nm)
        big[nm] = [o.T if nm in _COL_SHARDED else o for o in res]
    big["w_dw"] = _adamw_shard(r_dw, w_dw[0], moms["w_dw"][0], vars_["w_dw"][0], CONV_WIDTH, "adamw_w_dw")

    sm = _adamw_small(summed[:-1], [small_shape(weights[nm]) for nm in small_names],
                      [small_shape(moms[nm]) for nm in small_names], [small_shape(vars_[nm]) for nm in small_names])
    n_small = len(small_names)

    def result(kind, nm):
        if nm in big:
            return big[nm][kind].reshape(weights[nm].shape)
        return sm[kind * n_small + small_names.index(nm)].reshape(weights[nm].shape)

    loss = summed[-1][0, 0]
    out = [loss, dx.reshape(x.shape)]
    for kind in range(4):
        out += [result(kind, nm) for nm in names]
    return tuple(out)


def kernel(x, g_mix, w_in, b_in, w_dw, b_dw, ln_g, ln_b, w_pool, s_pool, w_out, g_ffn, w_gate, w_up, w_down, g_final, loss_target, m_g_mix, m_w_in, m_b_in, m_w_dw, m_b_dw, m_ln_g, m_ln_b, m_w_pool, m_s_pool, m_w_out, m_g_ffn, m_w_gate, m_w_up, m_w_down, m_g_final, v_g_mix, v_w_in, v_b_in, v_w_dw, v_b_dw, v_ln_g, v_ln_b, v_w_pool, v_s_pool, v_w_out, v_g_ffn, v_w_gate, v_w_up, v_w_down, v_g_final):
    args = (x, g_mix, w_in, b_in, w_dw, b_dw, ln_g, ln_b, w_pool, s_pool, w_out, g_ffn, w_gate, w_up, w_down, g_final, loss_target, m_g_mix, m_w_in, m_b_in, m_w_dw, m_b_dw, m_ln_g, m_ln_b, m_w_pool, m_s_pool, m_w_out, m_g_ffn, m_w_gate, m_w_up, m_w_down, m_g_final, v_g_mix, v_w_in, v_b_in, v_w_dw, v_b_dw, v_ln_g, v_ln_b, v_w_pool, v_s_pool, v_w_out, v_g_ffn, v_w_gate, v_w_up, v_w_down, v_g_final)
    return _step(args, ts_mix_fwd=512, ts_ffn=256, ts_mix_bwd=512, tm_grad=256)
```

```python
import functools

import jax
import jax.numpy as jnp
from jax import lax
from jax.experimental import pallas as pl
from jax.experimental.pallas import tpu as pltpu

F32 = jnp.float32
BF16 = jnp.bfloat16
MESH = pl.DeviceIdType.MESH
N_DEV = 8

C_CONV = 512
CONV_WIDTH = 31
POOL_WINDOWS = (2, 4, 8, 16)
POOL_GROUP = 128
RMS_EPS = 1e-6
LN_EPS = 1e-5

ADAM_LR = 0.001
ADAM_B1 = 0.9
ADAM_B2 = 0.999
ADAM_EPS = 1e-08
ADAM_WD = 0.01
ADAM_STEP = 10

HALO = 32
SUBLANES = 8
CONV_ROWS = 64
VMEM_LIMIT = 56 * 1024 * 1024


def _dot(a, b):
    return jnp.dot(a, b, preferred_element_type=F32)


def _dot_nt(a, b):
    return lax.dot_general(a, b, (((1,), (1,)), ((), ())), preferred_element_type=F32)


def _dot_tn(a, b):
    return lax.dot_general(a, b, (((0,), (0,)), ((), ())), preferred_element_type=F32)


def _mean_last(v):
    return jnp.mean(v, axis=-1, keepdims=True)


def _full(shape):
    nd = len(shape)
    return pl.BlockSpec(shape, lambda *_: (0,) * nd)


def _full1(shape):
    nd = len(shape)
    return pl.BlockSpec(shape, lambda *_: (0,) * nd, pipeline_mode=pl.Buffered(1))


def _shifted_copies(sh_ref, rows):
    for s in range(1, SUBLANES):
        sh_ref[s, 0:rows, :] = sh_ref[0, s:s + rows, :]


def _tap(sh_ref, off, r0, rows):
    q, s = divmod(off, SUBLANES)
    return sh_ref[s, pl.ds(r0 + q * SUBLANES, rows), :]


def _mix_fwd(x, g_mix, w_in, b_in, w_dw, b_dw, ln_g, ln_b, w_pool, s_pool, w_out, ag_shards, ts):
    S, D = x.shape
    d_in = w_in.shape[0]
    C = C_CONV
    nt = S // ts
    nrb = ts // CONV_ROWS
    n_ag = len(ag_shards)
    relay_step = (5 * nt) // 8

    def body(x_ref, g_ref, win_ref, bin_ref, wdw_ref, bdw_ref, lng_ref, lnb_ref, wp_ref, sp_ref, wout_ref, *rest):
        ag_in, rest = rest[:n_ag], rest[n_ag:]
        a_ref, gate_ref, v_ref, m_ref, y_ref, h1_ref, xn_ref = rest[:7]
        ag_out, rest = rest[7:7 + n_ag], rest[7 + n_ag:]
        ush, pbuf = rest[:2]
        ag_stage, ag_bufs, ag_sems = rest[2:2 + n_ag], rest[2 + n_ag:2 + 2 * n_ag], rest[2 + 2 * n_ag:]
        i = pl.program_id(0)
        ag_start, ag_relay, ag_finish = _gather_plan(ag_in, ag_out, ag_stage, ag_bufs, *ag_sems)

        @pl.when(i == 0)
        def _():
            ag_start()
            ush[0, 0:HALO, :] = jnp.zeros((HALO, C), F32)
            pbuf[0:HALO, :] = jnp.zeros((HALO, C), F32)

        @pl.when(i == relay_step)
        def _():
            ag_relay()

        x = x_ref[...]
        r1 = lax.rsqrt(_mean_last(x * x) + RMS_EPS)
        xn = (x * r1 * g_ref[...]).astype(BF16)
        xn_ref[...] = xn
        z = _dot_nt(xn, win_ref[...]) + bin_ref[...]
        a = z[:, 0:C]
        gate = z[:, C:2 * C]
        a_ref[...] = a
        gate_ref[...] = gate
        ush[0, HALO:HALO + ts, :] = a * jax.nn.sigmoid(gate)
        pbuf[HALO:HALO + ts, :] = z[:, 2 * C:]

        _shifted_copies(ush, ts + HALO - SUBLANES)

        def conv_block(rb, carry):
            r0 = pl.multiple_of(rb * CONV_ROWS, CONV_ROWS)
            acc = jnp.zeros((CONV_ROWS, C), F32)
            for k in range(CONV_WIDTH):
                acc = acc + wdw_ref[k:k + 1, :] * _tap(ush, HALO - (CONV_WIDTH - 1) + k, r0, CONV_ROWS)
            v_ref[pl.ds(r0, CONV_ROWS), :] = acc + bdw_ref[...]
            return carry

        lax.fori_loop(0, nrb, conv_block, 0)

        v = v_ref[...]
        mu = _mean_last(v)
        xc = v - mu
        rstd = lax.rsqrt(_mean_last(xc * xc) + LN_EPS)
        ln = xc * rstd * lng_ref[...] + lnb_ref[...]
        y_ref[:, 0:C] = (ln * jax.nn.sigmoid(ln)).astype(BF16)

        row = lax.broadcasted_iota(jnp.int32, (ts, 1), 0) + i * ts
        for gi, w in enumerate(POOL_WINDOWS):
            lanes = slice(gi * POOL_GROUP, (gi + 1) * POOL_GROUP)
            seg = pbuf[HALO:HALO + ts, lanes]
            ws = seg
            for k in range(1, w):
                ws = ws + pbuf[HALO - k:HALO - k + ts, lanes]
            cnt = jnp.minimum(row + 1, w).astype(F32)
            m = (ws / cnt - seg).astype(BF16)
            m_ref[:, lanes] = m
            ypre = _dot(m, wp_ref[gi])
            y_ref[:, C + gi * POOL_GROUP:C + (gi + 1) * POOL_GROUP] = (ypre * sp_ref[:, lanes]).astype(BF16)

        h1_ref[...] = x + _dot(y_ref[...], wout_ref[...])

        ush[0, 0:HALO, :] = ush[0, ts:ts + HALO, :]
        pbuf[0:HALO, :] = pbuf[ts:ts + HALO, :]

        @pl.when(i == nt - 1)
        def _():
            ag_finish()

    tile = lambda w, dt: (pl.BlockSpec((ts, w), lambda i: (i, 0)), jax.ShapeDtypeStruct((S, w), dt))
    outs = [tile(C, F32), tile(C, F32), tile(C, F32), tile(C, BF16), tile(D, BF16), tile(D, F32), tile(D, BF16)]
    return pl.pallas_call(
        body, name="mix_fwd", grid=(nt,),
        in_specs=[pl.BlockSpec((ts, D), lambda i: (i, 0)), _full((1, D)), _full1((d_in, D)), _full((1, d_in)),
                  _full(w_dw.shape), _full((1, C)), _full((1, C)), _full((1, C)), _full(w_pool.shape),
                  _full((1, C)), _full1((D, D))] + [_HBM] * n_ag,
        out_specs=[o[0] for o in outs] + [_HBM] * n_ag,
        out_shape=[o[1] for o in outs] + _gather_out_shapes(ag_shards, [BF16] * n_ag),
        scratch_shapes=[pltpu.VMEM((SUBLANES, ts + HALO, C), F32), pltpu.VMEM((ts + HALO, C), F32)]
        + _gather_scratch(ag_shards, [BF16] * n_ag),
        compiler_params=pltpu.CompilerParams(dimension_semantics=("arbitrary",), vmem_limit_bytes=VMEM_LIMIT),
    )(x, g_mix, w_in, b_in, w_dw, b_dw, ln_g, ln_b, w_pool, s_pool, w_out, *ag_shards)


def _ffn(h1, target, g_ffn, g_final, w_gate, w_up, w_down, ts, f_chunks):
    S, D = h1.shape
    Fd = w_gate.shape[0]
    nt = S // ts
    bounds = []
    lo = 0
    for n in f_chunks:
        bounds.append((lo, lo + n))
        lo += n
    assert lo == Fd

    def body(h1_ref, tgt_ref, gf_ref, gl_ref, wg_ref, wu_ref, wd_ref,
             dh1_ref, hn_ref, act_ref, dgt_ref, dup_ref, dh2_ref, loss_ref, dgl_ref, dgf_ref, gt_s, up_s):
        i = pl.program_id(0)

        @pl.when(i == 0)
        def _():
            loss_ref[...] = jnp.zeros_like(loss_ref)
            dgl_ref[...] = jnp.zeros_like(dgl_ref)
            dgf_ref[...] = jnp.zeros_like(dgf_ref)

        h1 = h1_ref[...]
        r2 = lax.rsqrt(_mean_last(h1 * h1) + RMS_EPS)
        hhat = h1 * r2
        hn = (hhat * gf_ref[...]).astype(BF16)
        hn_ref[...] = hn
        h2 = h1
        for lo, hi in bounds:
            gt = _dot_nt(hn, wg_ref[lo:hi, :])
            up = _dot_nt(hn, wu_ref[lo:hi, :])
            gt_s[:, lo:hi] = gt
            up_s[:, lo:hi] = up
            act = (gt * jax.nn.sigmoid(gt) * up).astype(BF16)
            act_ref[:, lo:hi] = act
            h2 = h2 + _dot(act, wd_ref[lo:hi, :])

        r3 = lax.rsqrt(_mean_last(h2 * h2) + RMS_EPS)
        n3 = h2 * r3
        gl = gl_ref[...]
        diff = n3 * gl - tgt_ref[...]
        loss_ref[...] += jnp.sum(0.5 * jnp.sum(diff * diff, axis=-1, keepdims=True) / D, axis=0, keepdims=True)
        dout = diff / D
        dgl_ref[...] += jnp.sum(dout * n3, axis=0, keepdims=True)
        dn = dout * gl
        dh2 = r3 * (dn - n3 * _mean_last(dn * n3))
        dh2b = dh2.astype(BF16)
        dh2_ref[...] = dh2b

        dhn = jnp.zeros((ts, D), F32)
        for lo, hi in bounds:
            gt = gt_s[:, lo:hi]
            up = up_s[:, lo:hi]
            sg = jax.nn.sigmoid(gt)
            dact = _dot_nt(dh2b, wd_ref[lo:hi, :])
            dgt = (dact * up * (sg * (1.0 + gt * (1.0 - sg)))).astype(BF16)
            dup = (dact * (gt * sg)).astype(BF16)
            dgt_ref[:, lo:hi] = dgt
            dup_ref[:, lo:hi] = dup
            dhn = dhn + _dot(dgt, wg_ref[lo:hi, :]) + _dot(dup, wu_ref[lo:hi, :])

        dgf_ref[...] += jnp.sum(dhn * hhat, axis=0, keepdims=True)
        dnn = dhn * gf_ref[...]
        dh1_ref[...] = dh2 + r2 * (dnn - hhat * _mean_last(dnn * hhat))

    tile = lambda w, dt: (pl.BlockSpec((ts, w), lambda i: (i, 0)), jax.ShapeDtypeStruct((S, w), dt))
    acc = lambda w: (_full((1, w)), jax.ShapeDtypeStruct((1, w), F32))
    outs = [tile(D, F32), tile(D, BF16), tile(Fd, BF16), tile(Fd, BF16), tile(Fd, BF16), tile(D, BF16),
            acc(128), acc(D), acc(D)]
    return pl.pallas_call(
        body, name="ffn_fwd_bwd", grid=(nt,),
        in_specs=[pl.BlockSpec((ts, D), lambda i: (i, 0)), pl.BlockSpec((ts, D), lambda i: (i, 0)),
                  _full((1, D)), _full((1, D)), _full1((Fd, D)), _full1((Fd, D)), _full1((Fd, D))],
        out_specs=[o[0] for o in outs], out_shape=[o[1] for o in outs],
        scratch_shapes=[pltpu.VMEM((ts, Fd), F32), pltpu.VMEM((ts, Fd), F32)],
        compiler_params=pltpu.CompilerParams(dimension_semantics=("arbitrary",), vmem_limit_bytes=VMEM_LIMIT),
    )(h1, target, g_ffn, g_final, w_gate, w_up, w_down)


def _mix_bwd(dh1, x, a, gate, v, m, g_mix, w_in, w_dw, ln_g, ln_b, w_pool, s_pool, w_out, rs_parts, ts):
    S, D = x.shape
    n_rs = len(rs_parts)
    d_in = w_in.shape[0]
    C = C_CONV
    nt = S // ts
    nrb = ts // CONV_ROWS
    wrows =((CONV_WIDTH + SUBLANES - 1) // SUBLANES) * SUBLANES

    def body(dh1_ref, x_ref, a_ref, gate_ref, v_ref, m_ref, g_ref, win_ref, wdw_ref, lng_ref,
             lnb_ref, wp_ref, sp_ref, wout_ref, *rest):
        rs_in, rest = rest[:n_rs], rest[n_rs:]
        (dx_ref, dz_ref, dh1b_ref, dgm_ref, dbin_ref, dwdw_ref, dbdw_ref, dlng_ref, dlnb_ref, dwp_ref,
         dsp_ref) = rest[:11]
        rs_out, rest = rest[11:11 + n_rs], rest[11 + n_rs:]
        dvsh, dqbuf, du_s = rest[:3]
        rs_bufs, (send_sems, recv_sems, local_sems) = rest[3:3 + n_rs], rest[3 + n_rs:]
        i = pl.program_id(0)
        t = nt - 1 - i
        rs_start, rs_finish = _chip_exchange_plan(rs_in, rs_out, rs_bufs, send_sems, recv_sems, local_sems)

        @pl.when(i == 0)
        def _():
            rs_start()
            dvsh[0, ts:ts + HALO, :] = jnp.zeros((HALO, C), F32)
            dqbuf[ts:ts + HALO, :] = jnp.zeros((HALO, C), F32)
            for r in (dgm_ref, dbin_ref, dwdw_ref, dbdw_ref, dlng_ref, dlnb_ref, dwp_ref, dsp_ref):
                r[...] = jnp.zeros_like(r)

        dh1 = dh1_ref[...]
        dh1b = dh1.astype(BF16)
        dh1b_ref[...] = dh1b
        dy = _dot_nt(dh1b, wout_ref[...])

        v = v_ref[...]
        mu = _mean_last(v)
        xc = v - mu
        rstd = lax.rsqrt(_mean_last(xc * xc) + LN_EPS)
        vhat = xc * rstd
        lng = lng_ref[...]
        ln = vhat * lng + lnb_ref[...]
        sg = jax.nn.sigmoid(ln)
        dln = dy[:, 0:C] * (sg * (1.0 + ln * (1.0 - sg)))
        dlng_ref[...] += jnp.sum(dln * vhat, axis=0, keepdims=True)
        dlnb_ref[...] += jnp.sum(dln, axis=0, keepdims=True)
        dvh = dln * lng
        dv = rstd * (dvh - _mean_last(dvh) - vhat * _mean_last(dvh * vhat))
        dbdw_ref[...] += jnp.sum(dv, axis=0, keepdims=True)
        dvsh[0, 0:ts, :] = dv
        _shifted_copies(dvsh, ts + HALO - SUBLANES)

        def conv_block(rb, carry):
            r0 = pl.multiple_of(rb * CONV_ROWS, CONV_ROWS)
            acc = jnp.zeros((CONV_ROWS, C), F32)
            for k in range(CONV_WIDTH):
                acc = acc + wdw_ref[k:k + 1, :] * _tap(dvsh, CONV_WIDTH - 1 - k, r0, CONV_ROWS)
            du_s[pl.ds(r0, CONV_ROWS), :] = acc
            return carry

        lax.fori_loop(0, nrb, conv_block, 0)

        a = a_ref[...]
        sgate = jax.nn.sigmoid(gate_ref[...])
        u = a * sgate
        for k in range(CONV_WIDTH):
            q, s = divmod(CONV_WIDTH - 1 - k, SUBLANES)
            prod = u * dvsh[s, q * SUBLANES:q * SUBLANES + ts, :]
            dwdw_ref[k:k + 1, :] += jnp.sum(prod, axis=0, keepdims=True)

        du = du_s[...]
        da = du * sgate
        dgate = du * a * sgate * (1.0 - sgate)
        dz_ref[:, 0:C] = da.astype(BF16)
        dz_ref[:, C:2 * C] = dgate.astype(BF16)
        dbin_ref[:, 0:C] += jnp.sum(da, axis=0, keepdims=True)
        dbin_ref[:, C:2 * C] += jnp.sum(dgate, axis=0, keepdims=True)

        row = lax.broadcasted_iota(jnp.int32, (ts, 1), 0) + t * ts
        for gi, w in enumerate(POOL_WINDOWS):
            lanes = slice(gi * POOL_GROUP, (gi + 1) * POOL_GROUP)
            dyp = dy[:, C + gi * POOL_GROUP:C + (gi + 1) * POOL_GROUP]
            mg = m_ref[:, lanes]
            ypre = _dot(mg, wp_ref[gi])
            dsp_ref[:, lanes] += jnp.sum(dyp * ypre, axis=0, keepdims=True)
            dyi = (dyp * sp_ref[:, lanes]).astype(BF16)
            dwp_ref[gi] += _dot_tn(mg, dyi)
            dm = _dot_nt(dyi, wp_ref[gi])
            cnt = jnp.minimum(row + 1, w).astype(F32)
            dqbuf[0:ts, lanes] = dm / cnt
            dp = -dm
            for k in range(w):
                dp = dp + dqbuf[k:k + ts, lanes]
            dz_ref[:, 2 * C + gi * POOL_GROUP:2 * C + (gi + 1) * POOL_GROUP] = dp.astype(BF16)
            dbin_ref[:, 2 * C + gi * POOL_GROUP:2 * C + (gi + 1) * POOL_GROUP] += jnp.sum(dp, axis=0, keepdims=True)

        dxn = _dot(dz_ref[...], win_ref[...])
        x = x_ref[...]
        r1 = lax.rsqrt(_mean_last(x * x) + RMS_EPS)
        xhat = x * r1
        dgm_ref[...] += jnp.sum(dxn * xhat, axis=0, keepdims=True)
        dnn = dxn * g_ref[...]
        dx_ref[...] = dh1 + r1 * (dnn - xhat * _mean_last(dnn * xhat))

        dvsh[0, ts:ts + HALO, :] = dvsh[0, 0:HALO, :]
        dqbuf[ts:ts + HALO, :] = dqbuf[0:HALO, :]

        @pl.when(i == nt - 1)
        def _():
            rs_finish()

    rev = lambda w: pl.BlockSpec((ts, w), lambda i: (nt - 1 - i, 0))
    acc = lambda shape: (_full(shape), jax.ShapeDtypeStruct(shape, F32))
    outs = [(rev(D), jax.ShapeDtypeStruct((S, D), F32)), (rev(d_in), jax.ShapeDtypeStruct((S, d_in), BF16)),
            (rev(D), jax.ShapeDtypeStruct((S, D), BF16)), acc((1, D)), acc((1, d_in)), acc((wrows, C)),
            acc((1, C)), acc((1, C)), acc((1, C)), acc(w_pool.shape), acc((1, C))]
    return pl.pallas_call(
        body, name="mix_bwd", grid=(nt,),
        in_specs=[rev(D), rev(D), rev(C), rev(C), rev(C), rev(C), _full((1, D)), _full((d_in, D)),
                  _full(w_dw.shape), _full((1, C)), _full((1, C)), _full(w_pool.shape), _full((1, C)),
                  _full((D, D))] + [_HBM] * n_rs,
        out_specs=[o[0] for o in outs] + [_HBM] * n_rs,
        out_shape=[o[1] for o in outs] + [jax.ShapeDtypeStruct(p.shape, p.dtype) for p in rs_parts],
        scratch_shapes=[pltpu.VMEM((SUBLANES, ts + HALO, C), F32), pltpu.VMEM((ts + HALO, C), F32),
                        pltpu.VMEM((ts, C), F32)]
        + [pltpu.VMEM(p.shape[1:], p.dtype) for p in rs_parts] + _comm_sems(n_rs, 3),
        compiler_params=pltpu.CompilerParams(dimension_semantics=("arbitrary",), vmem_limit_bytes=VMEM_LIMIT),
    )(dh1, x, a, gate, v, m, g_mix, w_in, w_dw, ln_g, ln_b, w_pool, s_pool, w_out, *rs_parts)


def _grad_matmul(a, b, tm, name, reduce_parts=(), reduce_vals=()):
    S, M = a.shape
    N = b.shape[1]
    steps = M // tm
    n, ns = len(reduce_parts), len(reduce_vals)

    def body(a_ref, b_ref, *rest):
        o_ref = rest[n + ns]
        if n + ns:
            start, middle, finish = _reduce_plan(rest[:n], rest[n:n + ns], rest[n + ns + 1:2 * n + ns + 1],
                                                 rest[2 * n + ns + 1:2 * (n + ns) + 1], rest[2 * (n + ns) + 1:])
            i = pl.program_id(0)
            pl.when(i == 0)(start)
            pl.when(i == 1)(middle)
        o_ref[...] = _dot_tn(a_ref[...], b_ref[...]).astype(BF16)
        if n + ns:
            pl.when(i == steps - 1)(finish)

    res = pl.pallas_call(
        body, name=name, grid=(steps,),
        in_specs=[pl.BlockSpec((S, tm), lambda i: (0, i)), _full1((S, N))] + [_VMEM] * (n + ns),
        out_specs=[pl.BlockSpec((tm, N), lambda i: (i, 0))] + [_VMEM] * (n + ns),
        out_shape=[jax.ShapeDtypeStruct((M, N), BF16)] + _reduce_out_shapes(reduce_parts, reduce_vals),
        scratch_shapes=_reduce_scratch(reduce_parts, reduce_vals) if n + ns else [],
        compiler_params=pltpu.CompilerParams(dimension_semantics=("arbitrary",) if n + ns else ("parallel",),
                                             vmem_limit_bytes=VMEM_LIMIT),
    )(a, b, *reduce_parts, *reduce_vals)
    return res if n + ns else res[0]


def _grad_matmul_pair_exchange(a, b, tm, name, parts):
    S, M = a.shape
    N = b.shape[1]
    steps = M // tm
    n = len(parts)

    def body(a_ref, b_ref, *rest):
        o_ref = rest[n]
        start, finish = _pair_exchange_plan(rest[:n], rest[n + 1:2 * n + 1], *rest[2 * n + 1:])
        i = pl.program_id(0)
        pl.when(i == 0)(start)
        o_ref[...] = _dot_tn(a_ref[...], b_ref[...]).astype(BF16)
        pl.when(i == steps - 1)(finish)

    sems = pltpu.SemaphoreType.DMA((n, N_CHIPS))
    return pl.pallas_call(
        body, name=name, grid=(steps,),
        in_specs=[pl.BlockSpec((S, tm), lambda i: (0, i)), _full1((S, N))] + [_HBM] * n,
        out_specs=[pl.BlockSpec((tm, N), lambda i: (i, 0))] + [_HBM] * n,
        out_shape=[jax.ShapeDtypeStruct((M, N), BF16)] + _pair_out_shapes(parts),
        scratch_shapes=[sems, sems],
        compiler_params=pltpu.CompilerParams(dimension_semantics=("arbitrary",), vmem_limit_bytes=VMEM_LIMIT),
    )(a, b, *parts)


def _position():
    return lax.axis_index("x"), lax.axis_index("y"), lax.axis_index("c")


def _slot(px, py, pc):
    return 4 * px + 2 * py + pc


def _all_gather(shards, dtypes):
    n = len(shards)

    def body(*refs):
        start, relay, finish = _gather_plan(refs[:n], refs[n:2 * n], refs[2 * n:3 * n], refs[3 * n:4 * n],
                                            *refs[4 * n:])
        start()
        relay()
        finish()

    return pl.pallas_call(
        body, name="all_gather_mix_weights",
        out_shape=_gather_out_shapes(shards, dtypes), in_specs=[_HBM] * n, out_specs=[_HBM] * n,
        scratch_shapes=_gather_scratch(shards, dtypes),
    )(*shards)


_HBM = pl.BlockSpec(memory_space=pl.ANY)


def _comm_sems(n, copies):
    return [pltpu.SemaphoreType.DMA((n, copies)), pltpu.SemaphoreType.DMA((n, copies)),
            pltpu.SemaphoreType.DMA((n, 2))]


def _bounce_bufs(blocks):
    return [pltpu.VMEM(b.shape, b.dtype) for b in blocks]


def _local_copy(srcs, dsts, bufs, local_sems):
    n = len(srcs)
    loads = [pltpu.make_async_copy(srcs[k], bufs[k], local_sems.at[k, 0]) for k in range(n)]
    for cp in loads:
        cp.start()
    for cp in loads:
        cp.wait()
    stores = _local_stores(dsts, bufs, local_sems)
    for cp in stores:
        cp.start()
    return stores


def _local_stores(dsts, bufs, local_sems):
    return [pltpu.make_async_copy(bufs[k], dsts[k], local_sems.at[k, 1]) for k in range(len(dsts))]


def _gather_out_shapes(shards, dtypes):
    return [jax.ShapeDtypeStruct((N_DEV,) + s.shape, dt) for s, dt in zip(shards, dtypes)]


def _gather_scratch(shards, dtypes):
    return ([pltpu.VMEM(s.shape, s.dtype) for s in shards] + [pltpu.VMEM(s.shape, dt) for s, dt in zip(shards, dtypes)]
            + _comm_sems(len(shards), 7))


def _gather_plan(ins, outs, stage, bufs, send_sems, recv_sems, local_sems):
    n = len(ins)
    x, y, c = _position()
    me, sibling = (x, y, c), (x, y, 1 - c)
    na, nb, dg = (x ^ (1 - c), y ^ c), (x ^ c, y ^ (1 - c)), (1 - x, 1 - y)
    own = [outs[k].at[_slot(*me)] for k in range(n)]

    def copy(k, sem, block, to, src=None):
        dst = outs[k].at[_slot(*block)]
        return pltpu.make_async_remote_copy(
            src_ref=dst if src is None else src, dst_ref=dst, send_sem=send_sems.at[k, sem],
            recv_sem=recv_sems.at[k, sem], device_id=to, device_id_type=MESH)

    def first():
        cps = []
        for k in range(n):
            cps += [copy(k, 0, me, sibling, src=bufs[k]), copy(k, 1, me, (*na, c), src=bufs[k]),
                    copy(k, 2, me, (*nb, c), src=bufs[k])]
        return cps

    def onward():
        return [copy(k, 3, (*na, c), (*nb, c)) for k in range(n)]

    def to_sibling(j, chip):
        return [copy(k, 4 + j, (*chip, c), sibling) for k in range(n)]

    def start():
        loads = [pltpu.make_async_copy(ins[k], stage[k], local_sems.at[k, 0]) for k in range(n)]
        for cp in loads:
            cp.start()
        for cp in loads:
            cp.wait()
        for k in range(n):
            bufs[k][...] = stage[k][...].astype(bufs[k].dtype)
        for cp in first() + _local_stores(own, bufs, local_sems):
            cp.start()

    def relay():
        for k in range(n):
            copy(k, 1, (*na, c), me).wait_recv()
        for cp in onward() + to_sibling(0, na):
            cp.start()
        for k in range(n):
            copy(k, 2, (*nb, c), me).wait_recv()
        for cp in to_sibling(1, nb):
            cp.start()

    def finish():
        for k in range(n):
            copy(k, 3, (*dg, c), me).wait_recv()
        for cp in to_sibling(2, dg):
            cp.start()
        for k in range(n):
            copy(k, 0, sibling, me).wait_recv()
            for j, chip in enumerate((nb, na, dg)):
                copy(k, 4 + j, (*chip, 1 - c), me).wait_recv()
        for cp in first() + onward() + to_sibling(0, na) + to_sibling(1, nb) + to_sibling(2, dg):
            cp.wait_send()
        for cp in _local_stores(own, bufs, local_sems):
            cp.wait()

    return start, relay, finish


N_CHIPS = 4
_CHIP_FLIPS = [(1, 0), (0, 1), (1, 1)]
F32_TRAVEL_LIMIT = 4096


def _chip_exchange_plan(ins, outs, bufs, send_sems, recv_sems, local_sems):
    n = len(ins)
    x, y, c = _position()
    my_q = 2 * x + y
    peers = [(x ^ fx, y ^ fy) for fx, fy in _CHIP_FLIPS]
    own = [outs[k].at[my_q] for k in range(n)]

    def sends():
        return [pltpu.make_async_remote_copy(
            src_ref=ins[k].at[2 * px + py], dst_ref=outs[k].at[my_q], send_sem=send_sems.at[k, f],
            recv_sem=recv_sems.at[k, f], device_id=(px, py, c), device_id_type=MESH)
            for f, (px, py) in enumerate(peers) for k in range(n)]

    def start():
        for cp in sends():
            cp.start()
        _local_copy([ins[k].at[my_q] for k in range(n)], own, bufs, local_sems)

    def finish():
        for f, (px, py) in enumerate(peers):
            for k in range(n):
                pltpu.make_async_remote_copy(
                    src_ref=ins[k].at[my_q], dst_ref=outs[k].at[2 * px + py], send_sem=send_sems.at[k, f],
                    recv_sem=recv_sems.at[k, f], device_id=(px, py, c), device_id_type=MESH).wait_recv()
        for cp in sends():
            cp.wait_send()
        for cp in _local_stores(own, bufs, local_sems):
            cp.wait()

    return start, finish


def _pair_exchange(parts):
    n = len(parts)

    def body(*refs):
        start, finish = _pair_exchange_plan(refs[:n], refs[n:2 * n], *refs[2 * n:])
        start()
        finish()

    sems = pltpu.SemaphoreType.DMA((n, N_CHIPS))
    return pl.pallas_call(
        body, name="pair_exchange", out_shape=_pair_out_shapes(parts),
        in_specs=[_HBM] * n, out_specs=[_HBM] * n, scratch_shapes=[sems, sems],
    )(*parts)


def _pair_out_shapes(parts):
    return [jax.ShapeDtypeStruct((N_CHIPS,) + p.shape[1:], p.dtype) for p in parts]


def _pair_exchange_plan(ins, sib, send_sems, recv_sems):
    x, y, c = _position()

    def copies():
        return [pltpu.make_async_remote_copy(
            src_ref=ins[k].at[2 * q + 1 - c], dst_ref=sib[k].at[q], send_sem=send_sems.at[k, q],
            recv_sem=recv_sems.at[k, q], device_id=(x, y, 1 - c), device_id_type=MESH)
            for k in range(len(ins)) for q in range(N_CHIPS)]

    def start():
        for cp in copies():
            cp.start()

    def finish():
        for cp in copies():
            cp.wait()

    return start, finish


def _pair_sum(parts, sib):
    n = len(parts)

    def body(*refs):
        c = lax.axis_index("c")
        for k in range(n):
            refs[2 * n + k][0] = (refs[k][0, c].astype(F32) + refs[n + k][0].astype(F32)).astype(BF16)

    pair = [pl.BlockSpec((1, 2) + p.shape[1:], lambda q: (q, 0, 0, 0)) for p in parts]
    one = [pl.BlockSpec((1,) + p.shape[1:], lambda q: (q, 0, 0)) for p in parts]
    return pl.pallas_call(
        body, name="pair_sum", grid=(N_CHIPS,), in_specs=pair + one, out_specs=one,
        out_shape=[jax.ShapeDtypeStruct(s.shape, BF16) for s in sib],
        compiler_params=pltpu.CompilerParams(dimension_semantics=("parallel",)),
    )(*[p.reshape((N_CHIPS, 2) + p.shape[1:]) for p in parts], *sib)


def _tail_reduce(parts, vals):
    n, ns = len(parts), len(vals)

    def body(*refs):
        start, middle, finish = _reduce_plan(refs[:n], refs[n:n + ns], refs[n + ns:2 * n + ns],
                                             refs[2 * n + ns:2 * (n + ns)], refs[2 * (n + ns):])
        start()
        middle()
        finish()

    return pl.pallas_call(
        body, name="tail_reduce", out_shape=_reduce_out_shapes(parts, vals),
        in_specs=[_VMEM] * (n + ns), out_specs=[_VMEM] * (n + ns), scratch_shapes=_reduce_scratch(parts, vals),
        compiler_params=pltpu.CompilerParams(vmem_limit_bytes=VMEM_LIMIT),
    )(*parts, *vals)


_VMEM = pl.BlockSpec(memory_space=pltpu.VMEM)


def _reduce_out_shapes(parts, vals):
    return ([jax.ShapeDtypeStruct((N_CHIPS,) + p.shape[1:], p.dtype) for p in parts]
            + [jax.ShapeDtypeStruct(v.shape, v.dtype) for v in vals])


def _reduce_scratch(parts, vals):
    n, ns = len(parts), len(vals)
    quarter = [pltpu.VMEM((N_CHIPS,) + p.shape[1:], p.dtype) for p in parts]
    dma = pltpu.SemaphoreType.DMA
    travel = [BF16 if v.size > F32_TRAVEL_LIMIT else v.dtype for v in vals]
    return (quarter * 3 + [pltpu.VMEM(v.shape, v.dtype) for v in vals]
            + [pltpu.VMEM(v.shape, t) for v, t in zip(vals, travel)]
            + [pltpu.VMEM((N_CHIPS,) + v.shape, t) for v, t in zip(vals, travel)]
            + [dma((max(n, 1), N_CHIPS)), dma((max(n, 1), N_CHIPS)), dma((max(ns, 1),)), dma((max(ns, 1),)),
               dma((max(n, 1), 3)), dma((max(n, 1), 3)), dma((max(ns, 1), 3)), dma((max(ns, 1), 3))])


def _reduce_plan(p_in, v_in, p_out, v_out, scratch):
    n, ns = len(p_in), len(v_in)
    p_sib, p_sum, p_all = scratch[:n], scratch[n:2 * n], scratch[2 * n:3 * n]
    v_sib, v_sum, v_all = (scratch[3 * n + j * ns:3 * n + (j + 1) * ns] for j in range(3))
    p1_send, p1_recv, v1_send, v1_recv, p3_send, p3_recv, v3_send, v3_recv = scratch[3 * n + 3 * ns:]
    x, y, c = _position()
    my_q = 2 * x + y
    peers = [(x ^ fx, y ^ fy) for fx, fy in _CHIP_FLIPS]

    def to_sibling(src, dst, send, recv):
        return pltpu.make_async_remote_copy(src_ref=src, dst_ref=dst, send_sem=send, recv_sem=recv,
                                            device_id=(x, y, 1 - c), device_id_type=MESH)

    def level1():
        cps = [to_sibling(p_in[k].at[2 * q + 1 - c], p_sib[k].at[q], p1_send.at[k, q], p1_recv.at[k, q])
               for k in range(n) for q in range(N_CHIPS)]
        return cps + [to_sibling(v_in[k], v_sib[k], v1_send.at[k], v1_recv.at[k]) for k in range(ns)]

    def to_chip(f, src, dst, send, recv):
        px, py = peers[f]
        return pltpu.make_async_remote_copy(src_ref=src, dst_ref=dst, send_sem=send, recv_sem=recv,
                                            device_id=(px, py, c), device_id_type=MESH)

    def level2(sending):
        cps = []
        for f, (px, py) in enumerate(peers):
            their_q = 2 * px + py
            for k in range(n):
                src, dst = (p_sum[k].at[their_q], p_all[k].at[my_q]) if sending else (
                    p_sum[k].at[my_q], p_all[k].at[their_q])
                cps.append(to_chip(f, src, dst, p3_send.at[k, f], p3_recv.at[k, f]))
            for k in range(ns):
                dst = v_all[k].at[my_q] if sending else v_all[k].at[their_q]
                cps.append(to_chip(f, v_sum[k], dst, v3_send.at[k, f], v3_recv.at[k, f]))
        return cps

    def start():
        for cp in level1():
            cp.start()

    def middle():
        for cp in level1():
            cp.wait_recv()
        for k in range(n):
            for q in range(N_CHIPS):
                p_sum[k][q] = (p_in[k][2 * q + c].astype(F32) + p_sib[k][q].astype(F32)).astype(p_sum[k].dtype)
        for k in range(ns):
            v_sum[k][...] = (v_in[k][...] + v_sib[k][...]).astype(v_sum[k].dtype)
        for cp in level2(True):
            cp.start()
        for k in range(n):
            p_all[k][my_q] = p_sum[k][my_q]
        for k in range(ns):
            v_all[k][my_q] = v_sum[k][...]

    def finish():
        for cp in level2(False):
            cp.wait_recv()
        for k in range(n):
            p_out[k][...] = p_all[k][...]
        for k in range(ns):
            total = v_all[k][0].astype(F32)
            for q in range(1, N_CHIPS):
                total = total + v_all[k][q].astype(F32)
            v_out[k][...] = total
        for cp in level1() + level2(True):
            cp.wait_send()

    return start, middle, finish


def _adamw_math(w, g, m, v):
    m = ADAM_B1 * m + (1.0 - ADAM_B1) * g
    v = ADAM_B2 * v + (1.0 - ADAM_B2) * (g * g)
    m_hat = m / (1.0 - ADAM_B1 ** ADAM_STEP)
    v_hat = v / (1.0 - ADAM_B2 ** ADAM_STEP)
    delta = -ADAM_LR * (m_hat / (jnp.sqrt(v_hat) + ADAM_EPS) + ADAM_WD * w)
    return delta, m, v


def _adamw_shard(parts, w, m, v, tr, name):
    R, Cc = w.shape

    def body(p_ref, w_ref, m_ref, v_ref, g_out, d_out, m_out, v_out):
        g = p_ref[0].astype(F32)
        for j in range(1, N_CHIPS):
            g = g + p_ref[j].astype(F32)
        d, mn, vn = _adamw_math(w_ref[...], g, m_ref[...], v_ref[...])
        g_out[...] = g
        d_out[...] = d
        m_out[...] = mn
        v_out[...] = vn

    blk = pl.BlockSpec((tr, Cc), lambda i: (i, 0))
    return pl.pallas_call(
        body, name=name, grid=(R // tr,),
        in_specs=[pl.BlockSpec((N_CHIPS, tr, Cc), lambda i: (0, i, 0)), blk, blk, blk],
        out_specs=[blk] * 4, out_shape=[jax.ShapeDtypeStruct((R, Cc), F32)] * 4,
        compiler_params=pltpu.CompilerParams(dimension_semantics=("parallel",)),
    )(parts, w, m, v)


def _adamw_small(grads, ws, ms, vs):
    n = len(grads)
    vm = pl.BlockSpec(memory_space=pltpu.VMEM)

    def body(*refs):
        g_in, w_in, m_in, v_in = (refs[k * n:(k + 1) * n] for k in range(4))
        g_out, d_out, m_out, v_out = (refs[(4 + k) * n:(5 + k) * n] for k in range(4))
        for k in range(n):
            g = g_in[k][...]
            d, mn, vn = _adamw_math(w_in[k][...], g, m_in[k][...], v_in[k][...])
            g_out[k][...] = g
            d_out[k][...] = d
            m_out[k][...] = mn
            v_out[k][...] = vn

    shapes = [jax.ShapeDtypeStruct(g.shape, F32) for g in grads]
    return pl.pallas_call(
        body, name="adamw_small", out_shape=shapes * 4, in_specs=[vm] * (4 * n), out_specs=[vm] * (4 * n),
    )(*grads, *ws, *ms, *vs)


def _unshard_cols(g):
    n, R, Cc = g.shape
    return jnp.transpose(g, (1, 0, 2)).reshape(R, n * Cc)


def _shard_cols(full):
    R, Ct = full.shape
    return jnp.transpose(full.reshape(R, N_DEV, Ct // N_DEV), (1, 0, 2))


_COL_SHARDED = ("w_in", "w_gate", "w_up")
_BIG = ("w_in", "w_out", "w_gate", "w_up", "w_down")


def _rows(nm, p):
    return p[0].T if nm in _COL_SHARDED else p[0]


def _step(args, ts_mix_fwd, ts_ffn, ts_mix_bwd, tm_grad):
    (x, g_mix, w_in, b_in, w_dw, b_dw, ln_g, ln_b, w_pool, s_pool, w_out, g_ffn, w_gate, w_up, w_down, g_final,
     loss_target) = args[:17]
    names = ["g_mix", "w_in", "b_in", "w_dw", "b_dw", "ln_g", "ln_b", "w_pool", "s_pool", "w_out", "g_ffn",
             "w_gate", "w_up", "w_down", "g_final"]
    weights = dict(zip(names, args[1:16]))
    moms = dict(zip(names, args[17:32]))
    vars_ = dict(zip(names, args[32:47]))

    S, D = x.shape[1], x.shape[2]
    x2 = x.reshape(S, D)
    tgt2 = loss_target.reshape(S, D)

    shard = lambda nm: _rows(nm, weights[nm])
    g_in, g_out, g_dw = _all_gather([shard("w_in"), shard("w_out"), w_dw[0]], [BF16, BF16, F32])
    wt_in, w_out_f = g_in.reshape(-1, D), g_out.reshape(-1, D)
    w_dw_f = _unshard_cols(g_dw)
    w_pool_b = w_pool[0].astype(BF16)

    a, gate, v, m, y, h1, xn, g_gate, g_up, g_down = _mix_fwd(
        x2, g_mix, wt_in, b_in, w_dw_f, b_dw, ln_g, ln_b, w_pool_b, s_pool, w_out_f,
        [shard("w_gate"), shard("w_up"), shard("w_down")], ts_mix_fwd)
    wt_gate, wt_up, w_down_f = g_gate.reshape(-1, D), g_up.reshape(-1, D), g_down.reshape(-1, D)
    Fd = wt_gate.shape[0]
    f_chunks = [1024] * (Fd // 1024) + ([Fd % 1024] if Fd % 1024 else [])
    dh1, hn, act, dgt, dup, dh2, loss_p, dg_final, dg_ffn = _ffn(
        h1, tgt2, g_ffn, g_final.reshape(1, D), wt_gate, wt_up, w_down_f, ts_ffn, f_chunks)

    by_shard = lambda g: g.reshape(N_DEV, -1, D)
    p_gate = by_shard(_grad_matmul(dgt, hn, tm_grad, "grad_w_gate"))
    g_up, s_gate = _grad_matmul_pair_exchange(dup, hn, tm_grad, "grad_w_up", [p_gate])
    p_up = by_shard(g_up)
    g_down, s_up = _grad_matmul_pair_exchange(act, dh2, tm_grad, "grad_w_down", [p_up])
    p_down = by_shard(g_down)
    s_down, = _pair_exchange([p_down])
    pair_sums = _pair_sum([p_gate, p_up, p_down], [s_gate, s_up, s_down])
    (dx, dz, dh1b, dg_mix, db_in, dw_dw, db_dw, dln_g, dln_b, dw_pool, ds_pool, r_gate, r_up, r_down) = _mix_bwd(
        dh1, x2, a, gate, v, m, g_mix, wt_in, w_dw_f, ln_g, ln_b, w_pool_b, s_pool, w_out_f, pair_sums, ts_mix_bwd)

    small_names = ["g_mix", "b_in", "b_dw", "ln_g", "ln_b", "w_pool", "s_pool", "g_ffn", "g_final"]
    small_shape = lambda p: p.reshape(-1, p.shape[-1])
    partial = [dg_mix, db_in, db_dw, dln_g, dln_b, dw_pool.reshape(-1, POOL_GROUP), ds_pool, dg_ffn, dg_final, loss_p]
    dwt_in, *summed = _grad_matmul(dz, xn, tm_grad, "grad_w_in", reduce_vals=partial)
    dw_out, r_in, r_dw = _grad_matmul(
        y, dh1b, tm_grad, "grad_w_out",
        reduce_parts=[dwt_in.reshape(N_DEV, -1, D), _shard_cols(dw_dw[0:CONV_WIDTH])])
    r_out, = _tail_reduce([dw_out.reshape(N_DEV, -1, D)], [])

    big = {}
    for nm, r in zip(_BIG, [r_in, r_out, r_gate, r_up, r_down]):
        rows = r.shape[1]
        res = _adamw_shard(r, _rows(nm, weights[nm]), _rows(nm, moms[nm]), _rows(nm, vars_[nm]),
                           rows // 2 if rows % 32 == 0 else rows, "adamw_" + nm)
        big[nm] = [o.T if nm in _COL_SHARDED else o for o in res]
    big["w_dw"] = _adamw_shard(r_dw, w_dw[0], moms["w_dw"][0], vars_["w_dw"][0], CONV_WIDTH, "adamw_w_dw")

    sm = _adamw_small(summed[:-1], [small_shape(weights[nm]) for nm in small_names],
                      [small_shape(moms[nm]) for nm in small_names], [small_shape(vars_[nm]) for nm in small_names])
    n_small = len(small_names)

    def result(kind, nm):
        if nm in big:
            return big[nm][kind].reshape(weights[nm].shape)
        return sm[kind * n_small + small_names.index(nm)].reshape(weights[nm].shape)

    loss = summed[-1][0, 0]
    out = [loss, dx.reshape(x.shape)]
    for kind in range(4):
        out += [result(kind, nm) for nm in names]
    return tuple(out)


def kernel(x, g_mix, w_in, b_in, w_dw, b_dw, ln_g, ln_b, w_pool, s_pool, w_out, g_ffn, w_gate, w_up, w_down, g_final, loss_target, m_g_mix, m_w_in, m_b_in, m_w_dw, m_b_dw, m_ln_g, m_ln_b, m_w_pool, m_s_pool, m_w_out, m_g_ffn, m_w_gate, m_w_up, m_w_down, m_g_final, v_g_mix, v_w_in, v_b_in, v_w_dw, v_b_dw, v_ln_g, v_ln_b, v_w_pool, v_s_pool, v_w_out, v_g_ffn, v_w_gate, v_w_up, v_w_down, v_g_final):
    args = (x, g_mix, w_in, b_in, w_dw, b_dw, ln_g, ln_b, w_pool, s_pool, w_out, g_ffn, w_gate, w_up, w_down, g_final, loss_target, m_g_mix, m_w_in, m_b_in, m_w_dw, m_b_dw, m_ln_g, m_ln_b, m_w_pool, m_s_pool, m_w_out, m_g_ffn, m_w_gate, m_w_up, m_w_down, m_g_final, v_g_mix, v_w_in, v_b_in, v_w_dw, v_b_dw, v_ln_g, v_ln_b, v_w_pool, v_s_pool, v_w_out, v_g_ffn, v_w_gate, v_w_up, v_w_down, v_g_final)
    return _step(args, ts_mix_fwd=512, ts_ffn=256, ts_mix_bwd=512, tm_grad=256)
```

```python
import jax
import jax.numpy as jnp
from jax import lax
from jax.experimental import pallas as pl
from jax.experimental.pallas import tpu as pltpu

F32 = jnp.float32
BF16 = jnp.bfloat16
MESH = pl.DeviceIdType.MESH
N_DEV = 8

C_CONV = 512
CONV_WIDTH = 31
POOL_WINDOWS = (2, 4, 8, 16)
POOL_GROUP = 128
RMS_EPS = 1e-6
LN_EPS = 1e-5

ADAM_LR = 0.001
ADAM_B1 = 0.9
ADAM_B2 = 0.999
ADAM_EPS = 1e-08
ADAM_WD = 0.01
ADAM_STEP = 10

HALO = 32
SUBLANES = 8
CONV_ROWS = 64
VMEM_LIMIT = 56 * 1024 * 1024


def _dot(a, b):
    return jnp.dot(a, b, preferred_element_type=F32)


def _dot_nt(a, b):
    return lax.dot_general(a, b, (((1,), (1,)), ((), ())), preferred_element_type=F32)


def _dot_tn(a, b):
    return lax.dot_general(a, b, (((0,), (0,)), ((), ())), preferred_element_type=F32)


def _mean_last(v):
    return jnp.mean(v, axis=-1, keepdims=True)


def _full(shape):
    nd = len(shape)
    return pl.BlockSpec(shape, lambda *_: (0,) * nd)


def _full1(shape):
    nd = len(shape)
    return pl.BlockSpec(shape, lambda *_: (0,) * nd, pipeline_mode=pl.Buffered(1))


def _shifted_copies(sh_ref, rows):
    for s in range(1, SUBLANES):
        sh_ref[s, 0:rows, :] = sh_ref[0, s:s + rows, :]


def _tap(sh_ref, off, r0, rows):
    q, s = divmod(off, SUBLANES)
    return sh_ref[s, pl.ds(r0 + q * SUBLANES, rows), :]


def _mix_fwd(x, g_mix, b_in, b_dw, ln_g, ln_b, w_pool, s_pool, mix_shards, ag_shards, ts):
    S, D = x.shape
    d_in = mix_shards[0].shape[0] * N_DEV
    C = C_CONV
    nt = S // ts
    nrb = ts // CONV_ROWS
    n_ag = len(ag_shards)
    relay_step = (5 * nt) // 8
    mix_dtypes = [BF16, BF16, F32]

    def body(x_ref, g_ref, bin_ref, bdw_ref, lng_ref, lnb_ref, wp_ref, sp_ref, *rest):
        mx_in, ag_in, rest = rest[:3], rest[3:3 + n_ag], rest[3 + n_ag:]
        a_ref, gate_ref, v_ref, m_ref, y_ref, h1_ref, xn_ref, wdw_ref = rest[:8]
        mx_out, ag_out, rest = rest[8:11], rest[11:11 + n_ag], rest[11 + n_ag:]
        ush, pbuf, win_ref, wout_ref, gdw, load_sems = rest[:6]
        rest = rest[6:]
        mx_stage, mx_bufs, mx_sems, rest = rest[:3], rest[3:6], rest[6:9], rest[9:]
        ag_stage, ag_bufs, ag_sems = rest[:n_ag], rest[n_ag:2 * n_ag], rest[2 * n_ag:]
        i = pl.program_id(0)
        mx_start, mx_relay, mx_finish = _gather_plan(mx_in, mx_out, mx_stage, mx_bufs, *mx_sems)
        ag_start, ag_relay, ag_finish = _gather_plan(ag_in, ag_out, ag_stage, ag_bufs, *ag_sems)

        @pl.when(i == 0)
        def _():
            mx_start()
            mx_relay()
            ag_start()
            mx_finish()
            loads = [pltpu.make_async_copy(mx_out[2], gdw, load_sems.at[2 * N_DEV])]
            for j in range(N_DEV):
                r_in, r_out = mx_out[0].shape[1], mx_out[1].shape[1]
                loads += [pltpu.make_async_copy(mx_out[0].at[j], win_ref.at[pl.ds(j * r_in, r_in), :],
                                                load_sems.at[2 * j]),
                          pltpu.make_async_copy(mx_out[1].at[j], wout_ref.at[pl.ds(j * r_out, r_out), :],
                                                load_sems.at[2 * j + 1])]
            for cp in loads:
                cp.start()
            for cp in loads:
                cp.wait()
            first_half = lax.broadcasted_iota(jnp.int32, gdw.shape[1:], 1) < C // N_DEV
            for p in range(N_DEV // 2):
                wdw_ref[:, 128 * p:128 * (p + 1)] = jnp.where(
                    first_half, gdw[2 * p], pltpu.roll(gdw[2 * p + 1], C // N_DEV, axis=1))
            ush[0, 0:HALO, :] = jnp.zeros((HALO, C), F32)
            pbuf[0:HALO, :] = jnp.zeros((HALO, C), F32)

        @pl.when(i == relay_step)
        def _():
            ag_relay()

        x = x_ref[...]
        r1 = lax.rsqrt(_mean_last(x * x) + RMS_EPS)
        xn = (x * r1 * g_ref[...]).astype(BF16)
        xn_ref[...] = xn
        z = _dot_nt(xn, win_ref[...]) + bin_ref[...]
        a = z[:, 0:C]
        gate = z[:, C:2 * C]
        a_ref[...] = a
        gate_ref[...] = gate
        ush[0, HALO:HALO + ts, :] = a * jax.nn.sigmoid(gate)
        pbuf[HALO:HALO + ts, :] = z[:, 2 * C:]

        _shifted_copies(ush, ts + HALO - SUBLANES)

        def conv_block(rb, carry):
            r0 = pl.multiple_of(rb * CONV_ROWS, CONV_ROWS)
            acc = jnp.zeros((CONV_ROWS, C), F32)
            for k in range(CONV_WIDTH):
                acc = acc + wdw_ref[k:k + 1, :] * _tap(ush, HALO - (CONV_WIDTH - 1) + k, r0, CONV_ROWS)
            v_ref[pl.ds(r0, CONV_ROWS), :] = acc + bdw_ref[...]
            return carry

        lax.fori_loop(0, nrb, conv_block, 0)

        v = v_ref[...]
        mu = _mean_last(v)
        xc = v - mu
        rstd = lax.rsqrt(_mean_last(xc * xc) + LN_EPS)
        ln = xc * rstd * lng_ref[...] + lnb_ref[...]
        y_ref[:, 0:C] = (ln * jax.nn.sigmoid(ln)).astype(BF16)

        row = lax.broadcasted_iota(jnp.int32, (ts, 1), 0) + i * ts
        for gi, w in enumerate(POOL_WINDOWS):
            lanes = slice(gi * POOL_GROUP, (gi + 1) * POOL_GROUP)
            seg = pbuf[HALO:HALO + ts, lanes]
            ws = seg
            for k in range(1, w):
                ws = ws + pbuf[HALO - k:HALO - k + ts, lanes]
            cnt = jnp.minimum(row + 1, w).astype(F32)
            m = (ws / cnt - seg).astype(BF16)
            m_ref[:, lanes] = m
            ypre = _dot(m, wp_ref[gi])
            y_ref[:, C + gi * POOL_GROUP:C + (gi + 1) * POOL_GROUP] = (ypre * sp_ref[:, lanes]).astype(BF16)

        h1_ref[...] = x + _dot(y_ref[...], wout_ref[...])

        ush[0, 0:HALO, :] = ush[0, ts:ts + HALO, :]
        pbuf[0:HALO, :] = pbuf[ts:ts + HALO, :]

        @pl.when(i == nt - 1)
        def _():
            ag_finish()

    tile = lambda w, dt: (pl.BlockSpec((ts, w), lambda i: (i, 0)), jax.ShapeDtypeStruct((S, w), dt))
    wdw_rows = mix_shards[2].shape[0]
    outs = [tile(C, F32), tile(C, F32), tile(C, F32), tile(C, BF16), tile(D, BF16), tile(D, F32), tile(D, BF16),
            (_full((wdw_rows, C)), jax.ShapeDtypeStruct((wdw_rows, C), F32))]
    return pl.pallas_call(
        body, name="mix_fwd", grid=(nt,),
        in_specs=[pl.BlockSpec((ts, D), lambda i: (i, 0)), _full((1, D)), _full((1, d_in)), _full((1, C)),
                  _full((1, C)), _full((1, C)), _full(w_pool.shape), _full((1, C))] + [_HBM] * (3 + n_ag),
        out_specs=[o[0] for o in outs] + [_HBM] * (3 + n_ag),
        out_shape=[o[1] for o in outs] + _gather_out_shapes(mix_shards, mix_dtypes)
        + _gather_out_shapes(ag_shards, [BF16] * n_ag),
        scratch_shapes=[pltpu.VMEM((SUBLANES, ts + HALO, C), F32), pltpu.VMEM((ts + HALO, C), F32),
                        pltpu.VMEM((d_in, D), BF16), pltpu.VMEM((D, D), BF16),
                        pltpu.VMEM((N_DEV,) + mix_shards[2].shape, F32), pltpu.SemaphoreType.DMA((2 * N_DEV + 1,))]
        + _gather_scratch(mix_shards, mix_dtypes) + _gather_scratch(ag_shards, [BF16] * n_ag),
        compiler_params=pltpu.CompilerParams(dimension_semantics=("arbitrary",), vmem_limit_bytes=VMEM_LIMIT),
    )(x, g_mix, b_in, b_dw, ln_g, ln_b, w_pool, s_pool, *mix_shards, *ag_shards)


def _ffn(h1, target, g_ffn, g_final, w_gate, w_up, w_down, ts, f_chunks):
    S, D = h1.shape
    Fd = w_gate.shape[0]
    nt = S // ts
    bounds = []
    lo = 0
    for n in f_chunks:
        bounds.append((lo, lo + n))
        lo += n
    assert lo == Fd

    def body(h1_ref, tgt_ref, gf_ref, gl_ref, wg_ref, wu_ref, wd_ref,
             dh1_ref, hn_ref, act_ref, dgt_ref, dup_ref, dh2_ref, loss_ref, dgl_ref, dgf_ref, gt_s, up_s):
        i = pl.program_id(0)

        @pl.when(i == 0)
        def _():
            loss_ref[...] = jnp.zeros_like(loss_ref)
            dgl_ref[...] = jnp.zeros_like(dgl_ref)
            dgf_ref[...] = jnp.zeros_like(dgf_ref)

        h1 = h1_ref[...]
        r2 = lax.rsqrt(_mean_last(h1 * h1) + RMS_EPS)
        hhat = h1 * r2
        hn = (hhat * gf_ref[...]).astype(BF16)
        hn_ref[...] = hn
        h2 = h1
        for lo, hi in bounds:
            gt = _dot_nt(hn, wg_ref[lo:hi, :])
            up = _dot_nt(hn, wu_ref[lo:hi, :])
            gt_s[:, lo:hi] = gt
            up_s[:, lo:hi] = up
            act = (gt * jax.nn.sigmoid(gt) * up).astype(BF16)
            act_ref[:, lo:hi] = act
            h2 = h2 + _dot(act, wd_ref[lo:hi, :])

        r3 = lax.rsqrt(_mean_last(h2 * h2) + RMS_EPS)
        n3 = h2 * r3
        gl = gl_ref[...]
        diff = n3 * gl - tgt_ref[...]
        loss_ref[...] += jnp.sum(0.5 * jnp.sum(diff * diff, axis=-1, keepdims=True) / D, axis=0, keepdims=True)
        dout = diff / D
        dgl_ref[...] += jnp.sum(dout * n3, axis=0, keepdims=True)
        dn = dout * gl
        dh2 = r3 * (dn - n3 * _mean_last(dn * n3))
        dh2b = dh2.astype(BF16)
        dh2_ref[...] = dh2b

        dhn = jnp.zeros((ts, D), F32)
        for lo, hi in bounds:
            gt = gt_s[:, lo:hi]
            up = up_s[:, lo:hi]
            sg = jax.nn.sigmoid(gt)
            dact = _dot_nt(dh2b, wd_ref[lo:hi, :])
            dgt = (dact * up * (sg * (1.0 + gt * (1.0 - sg)))).astype(BF16)
            dup = (dact * (gt * sg)).astype(BF16)
            dgt_ref[:, lo:hi] = dgt
            dup_ref[:, lo:hi] = dup
            dhn = dhn + _dot(dgt, wg_ref[lo:hi, :]) + _dot(dup, wu_ref[lo:hi, :])

        dgf_ref[...] += jnp.sum(dhn * hhat, axis=0, keepdims=True)
        dnn = dhn * gf_ref[...]
        dh1_ref[...] = dh2 + r2 * (dnn - hhat * _mean_last(dnn * hhat))

    tile = lambda w, dt: (pl.BlockSpec((ts, w), lambda i: (i, 0)), jax.ShapeDtypeStruct((S, w), dt))
    acc = lambda w: (_full((1, w)), jax.ShapeDtypeStruct((1, w), F32))
    outs = [tile(D, F32), tile(D, BF16), tile(Fd, BF16), tile(Fd, BF16), tile(Fd, BF16), tile(D, BF16),
            acc(128), acc(D), acc(D)]
    return pl.pallas_call(
        body, name="ffn_fwd_bwd", grid=(nt,),
        in_specs=[pl.BlockSpec((ts, D), lambda i: (i, 0)), pl.BlockSpec((ts, D), lambda i: (i, 0)),
                  _full((1, D)), _full((1, D)), _full1((Fd, D)), _full1((Fd, D)), _full1((Fd, D))],
        out_specs=[o[0] for o in outs], out_shape=[o[1] for o in outs],
        scratch_shapes=[pltpu.VMEM((ts, Fd), F32), pltpu.VMEM((ts, Fd), F32)],
        compiler_params=pltpu.CompilerParams(dimension_semantics=("arbitrary",), vmem_limit_bytes=VMEM_LIMIT),
    )(h1, target, g_ffn, g_final, w_gate, w_up, w_down)


def _mix_bwd(dh1, x, a, gate, v, m, g_mix, w_in, w_dw, ln_g, ln_b, w_pool, s_pool, w_out, rs_parts, ts):
    S, D = x.shape
    n_rs = len(rs_parts)
    d_in = w_in.shape[0]
    C = C_CONV
    nt = S // ts
    nrb = ts // CONV_ROWS
    wrows =((CONV_WIDTH + SUBLANES - 1) // SUBLANES) * SUBLANES

    def body(dh1_ref, x_ref, a_ref, gate_ref, v_ref, m_ref, g_ref, win_ref, wdw_ref, lng_ref,
             lnb_ref, wp_ref, sp_ref, wout_ref, *rest):
        rs_in, rest = rest[:n_rs], rest[n_rs:]
        (dx_ref, dz_ref, dh1b_ref, dgm_ref, dbin_ref, dwdw_ref, dbdw_ref, dlng_ref, dlnb_ref, dwp_ref,
         dsp_ref) = rest[:11]
        rs_out, rest = rest[11:11 + n_rs], rest[11 + n_rs:]
        dvsh, dqbuf, du_s = rest[:3]
        rs_bufs, (send_sems, recv_sems, local_sems) = rest[3:3 + n_rs], rest[3 + n_rs:]
        i = pl.program_id(0)
        t = nt - 1 - i
        rs_start, rs_finish = _chip_exchange_plan(rs_in, rs_out, rs_bufs, send_sems, recv_sems, local_sems)

        @pl.when(i == 0)
        def _():
            rs_start()
            dvsh[0, ts:ts + HALO, :] = jnp.zeros((HALO, C), F32)
            dqbuf[ts:ts + HALO, :] = jnp.zeros((HALO, C), F32)
            for r in (dgm_ref, dbin_ref, dwdw_ref, dbdw_ref, dlng_ref, dlnb_ref, dwp_ref, dsp_ref):
                r[...] = jnp.zeros_like(r)

        dh1 = dh1_ref[...]
        dh1b = dh1.astype(BF16)
        dh1b_ref[...] = dh1b
        dy = _dot_nt(dh1b, wout_ref[...])

        v = v_ref[...]
        mu = _mean_last(v)
        xc = v - mu
        rstd = lax.rsqrt(_mean_last(xc * xc) + LN_EPS)
        vhat = xc * rstd
        lng = lng_ref[...]
        ln = vhat * lng + lnb_ref[...]
        sg = jax.nn.sigmoid(ln)
        dln = dy[:, 0:C] * (sg * (1.0 + ln * (1.0 - sg)))
        dlng_ref[...] += jnp.sum(dln * vhat, axis=0, keepdims=True)
        dlnb_ref[...] += jnp.sum(dln, axis=0, keepdims=True)
        dvh = dln * lng
        dv = rstd * (dvh - _mean_last(dvh) - vhat * _mean_last(dvh * vhat))
        dbdw_ref[...] += jnp.sum(dv, axis=0, keepdims=True)
        dvsh[0, 0:ts, :] = dv
        _shifted_copies(dvsh, ts + HALO - SUBLANES)

        def conv_block(rb, carry):
            r0 = pl.multiple_of(rb * CONV_ROWS, CONV_ROWS)
            acc = jnp.zeros((CONV_ROWS, C), F32)
            for k in range(CONV_WIDTH):
                acc = acc + wdw_ref[k:k + 1, :] * _tap(dvsh, CONV_WIDTH - 1 - k, r0, CONV_ROWS)
            du_s[pl.ds(r0, CONV_ROWS), :] = acc
            return carry

        lax.fori_loop(0, nrb, conv_block, 0)

        a = a_ref[...]
        sgate = jax.nn.sigmoid(gate_ref[...])
        u = a * sgate
        for k in range(CONV_WIDTH):
            q, s = divmod(CONV_WIDTH - 1 - k, SUBLANES)
            prod = u * dvsh[s, q * SUBLANES:q * SUBLANES + ts, :]
            dwdw_ref[k:k + 1, :] += jnp.sum(prod, axis=0, keepdims=True)

        du = du_s[...]
        da = du * sgate
        dgate = du * a * sgate * (1.0 - sgate)
        dz_ref[:, 0:C] = da.astype(BF16)
        dz_ref[:, C:2 * C] = dgate.astype(BF16)
        dbin_ref[:, 0:C] += jnp.sum(da, axis=0, keepdims=True)
        dbin_ref[:, C:2 * C] += jnp.sum(dgate, axis=0, keepdims=True)

        row = lax.broadcasted_iota(jnp.int32, (ts, 1), 0) + t * ts
        for gi, w in enumerate(POOL_WINDOWS):
            lanes = slice(gi * POOL_GROUP, (gi + 1) * POOL_GROUP)
            dyp = dy[:, C + gi * POOL_GROUP:C + (gi + 1) * POOL_GROUP]
            mg = m_ref[:, lanes]
            ypre = _dot(mg, wp_ref[gi])
            dsp_ref[:, lanes] += jnp.sum(dyp * ypre, axis=0, keepdims=True)
            dyi = (dyp * sp_ref[:, lanes]).astype(BF16)
            dwp_ref[gi] += _dot_tn(mg, dyi)
            dm = _dot_nt(dyi, wp_ref[gi])
            cnt = jnp.minimum(row + 1, w).astype(F32)
            dqbuf[0:ts, lanes] = dm / cnt
            dp = -dm
            for k in range(w):
                dp = dp + dqbuf[k:k + ts, lanes]
            dz_ref[:, 2 * C + gi * POOL_GROUP:2 * C + (gi + 1) * POOL_GROUP] = dp.astype(BF16)
            dbin_ref[:, 2 * C + gi * POOL_GROUP:2 * C + (gi + 1) * POOL_GROUP] += jnp.sum(dp, axis=0, keepdims=True)

        dxn = _dot(dz_ref[...], win_ref[...])
        x = x_ref[...]
        r1 = lax.rsqrt(_mean_last(x * x) + RMS_EPS)
        xhat = x * r1
        dgm_ref[...] += jnp.sum(dxn * xhat, axis=0, keepdims=True)
        dnn = dxn * g_ref[...]
        dx_ref[...] = dh1 + r1 * (dnn - xhat * _mean_last(dnn * xhat))

        dvsh[0, ts:ts + HALO, :] = dvsh[0, 0:HALO, :]
        dqbuf[ts:ts + HALO, :] = dqbuf[0:HALO, :]

        @pl.when(i == nt - 1)
        def _():
            rs_finish()

    rev = lambda w: pl.BlockSpec((ts, w), lambda i: (nt - 1 - i, 0))
    acc = lambda shape: (_full(shape), jax.ShapeDtypeStruct(shape, F32))
    outs = [(rev(D), jax.ShapeDtypeStruct((S, D), F32)), (rev(d_in), jax.ShapeDtypeStruct((S, d_in), BF16)),
            (rev(D), jax.ShapeDtypeStruct((S, D), BF16)), acc((1, D)), acc((1, d_in)), acc((wrows, C)),
            acc((1, C)), acc((1, C)), acc((1, C)), acc(w_pool.shape), acc((1, C))]
    return pl.pallas_call(
        body, name="mix_bwd", grid=(nt,),
        in_specs=[rev(D), rev(D), rev(C), rev(C), rev(C), rev(C), _full((1, D)), _full((d_in, D)),
                  _full(w_dw.shape), _full((1, C)), _full((1, C)), _full(w_pool.shape), _full((1, C)),
                  _full((D, D))] + [_HBM] * n_rs,
        out_specs=[o[0] for o in outs] + [_HBM] * n_rs,
        out_shape=[o[1] for o in outs] + [jax.ShapeDtypeStruct(p.shape, p.dtype) for p in rs_parts],
        scratch_shapes=[pltpu.VMEM((SUBLANES, ts + HALO, C), F32), pltpu.VMEM((ts + HALO, C), F32),
                        pltpu.VMEM((ts, C), F32)]
        + [pltpu.VMEM(p.shape[1:], p.dtype) for p in rs_parts] + _comm_sems(n_rs, 3),
        compiler_params=pltpu.CompilerParams(dimension_semantics=("arbitrary",), vmem_limit_bytes=VMEM_LIMIT),
    )(dh1, x, a, gate, v, m, g_mix, w_in, w_dw, ln_g, ln_b, w_pool, s_pool, w_out, *rs_parts)


def _grad_matmul(a, b, tm, name, reduce_parts=(), reduce_vals=()):
    S, M = a.shape
    N = b.shape[1]
    steps = M // tm
    n, ns = len(reduce_parts), len(reduce_vals)

    def body(a_ref, b_ref, *rest):
        o_ref = rest[n + ns]
        if n + ns:
            start, middle, finish = _reduce_plan(rest[:n], rest[n:n + ns], rest[n + ns + 1:2 * n + ns + 1],
                                                 rest[2 * n + ns + 1:2 * (n + ns) + 1], rest[2 * (n + ns) + 1:])
            i = pl.program_id(0)
            pl.when(i == 0)(start)
            pl.when(i == 1)(middle)
        o_ref[...] = _dot_tn(a_ref[...], b_ref[...]).astype(BF16)
        if n + ns:
            pl.when(i == steps - 1)(finish)

    res = pl.pallas_call(
        body, name=name, grid=(steps,),
        in_specs=[pl.BlockSpec((S, tm), lambda i: (0, i)), _full1((S, N))] + [_VMEM] * (n + ns),
        out_specs=[pl.BlockSpec((tm, N), lambda i: (i, 0))] + [_VMEM] * (n + ns),
        out_shape=[jax.ShapeDtypeStruct((M, N), BF16)] + _reduce_out_shapes(reduce_parts, reduce_vals),
        scratch_shapes=_reduce_scratch(reduce_parts, reduce_vals) if n + ns else [],
        compiler_params=pltpu.CompilerParams(dimension_semantics=("arbitrary",) if n + ns else ("parallel",),
                                             vmem_limit_bytes=VMEM_LIMIT),
    )(a, b, *reduce_parts, *reduce_vals)
    return res if n + ns else res[0]


def _grad_matmul_pair_exchange(a, b, tm, name, parts):
    S, M = a.shape
    N = b.shape[1]
    steps = M // tm
    n = len(parts)

    def body(a_ref, b_ref, *rest):
        o_ref = rest[n]
        start, finish = _pair_exchange_plan(rest[:n], rest[n + 1:2 * n + 1], *rest[2 * n + 1:])
        i = pl.program_id(0)
        pl.when(i == 0)(start)
        o_ref[...] = _dot_tn(a_ref[...], b_ref[...]).astype(BF16)
        pl.when(i == steps - 1)(finish)

    sems = pltpu.SemaphoreType.DMA((n, N_CHIPS))
    return pl.pallas_call(
        body, name=name, grid=(steps,),
        in_specs=[pl.BlockSpec((S, tm), lambda i: (0, i)), _full1((S, N))] + [_HBM] * n,
        out_specs=[pl.BlockSpec((tm, N), lambda i: (i, 0))] + [_HBM] * n,
        out_shape=[jax.ShapeDtypeStruct((M, N), BF16)] + _pair_out_shapes(parts),
        scratch_shapes=[sems, sems],
        compiler_params=pltpu.CompilerParams(dimension_semantics=("arbitrary",), vmem_limit_bytes=VMEM_LIMIT),
    )(a, b, *parts)


def _position():
    return lax.axis_index("x"), lax.axis_index("y"), lax.axis_index("c")


def _slot(px, py, pc):
    return 4 * px + 2 * py + pc


_HBM = pl.BlockSpec(memory_space=pl.ANY)


def _comm_sems(n, copies):
    return [pltpu.SemaphoreType.DMA((n, copies)), pltpu.SemaphoreType.DMA((n, copies)),
            pltpu.SemaphoreType.DMA((n, 2))]


def _local_copy(srcs, dsts, bufs, local_sems):
    n = len(srcs)
    loads = [pltpu.make_async_copy(srcs[k], bufs[k], local_sems.at[k, 0]) for k in range(n)]
    for cp in loads:
        cp.start()
    for cp in loads:
        cp.wait()
    stores = _local_stores(dsts, bufs, local_sems)
    for cp in stores:
        cp.start()
    return stores


def _local_stores(dsts, bufs, local_sems):
    return [pltpu.make_async_copy(bufs[k], dsts[k], local_sems.at[k, 1]) for k in range(len(dsts))]


def _gather_out_shapes(shards, dtypes):
    return [jax.ShapeDtypeStruct((N_DEV,) + s.shape, dt) for s, dt in zip(shards, dtypes)]


def _gather_scratch(shards, dtypes):
    return ([pltpu.VMEM(s.shape, s.dtype) for s in shards] + [pltpu.VMEM(s.shape, dt) for s, dt in zip(shards, dtypes)]
            + _comm_sems(len(shards), 7))


def _gather_plan(ins, outs, stage, bufs, send_sems, recv_sems, local_sems):
    n = len(ins)
    x, y, c = _position()
    me, sibling = (x, y, c), (x, y, 1 - c)
    na, nb, dg = (x ^ (1 - c), y ^ c), (x ^ c, y ^ (1 - c)), (1 - x, 1 - y)
    own = [outs[k].at[_slot(*me)] for k in range(n)]

    def copy(k, sem, block, to, src=None):
        dst = outs[k].at[_slot(*block)]
        return pltpu.make_async_remote_copy(
            src_ref=dst if src is None else src, dst_ref=dst, send_sem=send_sems.at[k, sem],
            recv_sem=recv_sems.at[k, sem], device_id=to, device_id_type=MESH)

    def first():
        cps = []
        for k in range(n):
            cps += [copy(k, 0, me, sibling, src=bufs[k]), copy(k, 1, me, (*na, c), src=bufs[k]),
                    copy(k, 2, me, (*nb, c), src=bufs[k])]
        return cps

    def onward():
        return [copy(k, 3, (*na, c), (*nb, c)) for k in range(n)]

    def to_sibling(j, chip):
        return [copy(k, 4 + j, (*chip, c), sibling) for k in range(n)]

    def start():
        loads = [pltpu.make_async_copy(ins[k], stage[k], local_sems.at[k, 0]) for k in range(n)]
        for cp in loads:
            cp.start()
        for cp in loads:
            cp.wait()
        for k in range(n):
            bufs[k][...] = stage[k][...].astype(bufs[k].dtype)
        for cp in first() + _local_stores(own, bufs, local_sems):
            cp.start()

    def relay():
        for k in range(n):
            copy(k, 1, (*na, c), me).wait_recv()
        for cp in onward() + to_sibling(0, na):
            cp.start()
        for k in range(n):
            copy(k, 2, (*nb, c), me).wait_recv()
        for cp in to_sibling(1, nb):
            cp.start()

    def finish():
        for k in range(n):
            copy(k, 3, (*dg, c), me).wait_recv()
        for cp in to_sibling(2, dg):
            cp.start()
        for k in range(n):
            copy(k, 0, sibling, me).wait_recv()
            for j, chip in enumerate((nb, na, dg)):
                copy(k, 4 + j, (*chip, 1 - c), me).wait_recv()
        for cp in first() + onward() + to_sibling(0, na) + to_sibling(1, nb) + to_sibling(2, dg):
            cp.wait_send()
        for cp in _local_stores(own, bufs, local_sems):
            cp.wait()

    return start, relay, finish


N_CHIPS = 4
_CHIP_FLIPS = [(1, 0), (0, 1), (1, 1)]
F32_TRAVEL_LIMIT = 4096


def _chip_exchange_plan(ins, outs, bufs, send_sems, recv_sems, local_sems):
    n = len(ins)
    x, y, c = _position()
    my_q = 2 * x + y
    peers = [(x ^ fx, y ^ fy) for fx, fy in _CHIP_FLIPS]
    own = [outs[k].at[my_q] for k in range(n)]

    def sends():
        return [pltpu.make_async_remote_copy(
            src_ref=ins[k].at[2 * px + py], dst_ref=outs[k].at[my_q], send_sem=send_sems.at[k, f],
            recv_sem=recv_sems.at[k, f], device_id=(px, py, c), device_id_type=MESH)
            for f, (px, py) in enumerate(peers) for k in range(n)]

    def start():
        for cp in sends():
            cp.start()
        _local_copy([ins[k].at[my_q] for k in range(n)], own, bufs, local_sems)

    def finish():
        for f, (px, py) in enumerate(peers):
            for k in range(n):
                pltpu.make_async_remote_copy(
                    src_ref=ins[k].at[my_q], dst_ref=outs[k].at[2 * px + py], send_sem=send_sems.at[k, f],
                    recv_sem=recv_sems.at[k, f], device_id=(px, py, c), device_id_type=MESH).wait_recv()
        for cp in sends():
            cp.wait_send()
        for cp in _local_stores(own, bufs, local_sems):
            cp.wait()

    return start, finish


def _pair_exchange(parts):
    n = len(parts)

    def body(*refs):
        start, finish = _pair_exchange_plan(refs[:n], refs[n:2 * n], *refs[2 * n:])
        start()
        finish()

    sems = pltpu.SemaphoreType.DMA((n, N_CHIPS))
    return pl.pallas_call(
        body, name="pair_exchange", out_shape=_pair_out_shapes(parts),
        in_specs=[_HBM] * n, out_specs=[_HBM] * n, scratch_shapes=[sems, sems],
    )(*parts)


def _pair_out_shapes(parts):
    return [jax.ShapeDtypeStruct((N_CHIPS,) + p.shape[1:], p.dtype) for p in parts]


def _pair_exchange_plan(ins, sib, send_sems, recv_sems):
    x, y, c = _position()

    def copies():
        return [pltpu.make_async_remote_copy(
            src_ref=ins[k].at[2 * q + 1 - c], dst_ref=sib[k].at[q], send_sem=send_sems.at[k, q],
            recv_sem=recv_sems.at[k, q], device_id=(x, y, 1 - c), device_id_type=MESH)
            for k in range(len(ins)) for q in range(N_CHIPS)]

    def start():
        for cp in copies():
            cp.start()

    def finish():
        for cp in copies():
            cp.wait()

    return start, finish


def _pair_sum(parts, sib):
    n = len(parts)

    def body(*refs):
        c = lax.axis_index("c")
        for k in range(n):
            refs[2 * n + k][0] = (refs[k][0, c].astype(F32) + refs[n + k][0].astype(F32)).astype(BF16)

    pair = [pl.BlockSpec((1, 2) + p.shape[1:], lambda q: (q, 0, 0, 0)) for p in parts]
    one = [pl.BlockSpec((1,) + p.shape[1:], lambda q: (q, 0, 0)) for p in parts]
    return pl.pallas_call(
        body, name="pair_sum", grid=(N_CHIPS,), in_specs=pair + one, out_specs=one,
        out_shape=[jax.ShapeDtypeStruct(s.shape, BF16) for s in sib],
        compiler_params=pltpu.CompilerParams(dimension_semantics=("parallel",)),
    )(*[p.reshape((N_CHIPS, 2) + p.shape[1:]) for p in parts], *sib)


def _tail_reduce(parts, vals):
    n, ns = len(parts), len(vals)

    def body(*refs):
        start, middle, finish = _reduce_plan(refs[:n], refs[n:n + ns], refs[n + ns:2 * n + ns],
                                             refs[2 * n + ns:2 * (n + ns)], refs[2 * (n + ns):])
        start()
        middle()
        finish()

    return pl.pallas_call(
        body, name="tail_reduce", out_shape=_reduce_out_shapes(parts, vals),
        in_specs=[_VMEM] * (n + ns), out_specs=[_VMEM] * (n + ns), scratch_shapes=_reduce_scratch(parts, vals),
        compiler_params=pltpu.CompilerParams(vmem_limit_bytes=VMEM_LIMIT),
    )(*parts, *vals)


_VMEM = pl.BlockSpec(memory_space=pltpu.VMEM)


def _reduce_out_shapes(parts, vals):
    return ([jax.ShapeDtypeStruct((N_CHIPS,) + p.shape[1:], p.dtype) for p in parts]
            + [jax.ShapeDtypeStruct(v.shape, v.dtype) for v in vals])


def _reduce_scratch(parts, vals):
    n, ns = len(parts), len(vals)
    quarter = [pltpu.VMEM((N_CHIPS,) + p.shape[1:], p.dtype) for p in parts]
    dma = pltpu.SemaphoreType.DMA
    travel = [BF16 if v.size > F32_TRAVEL_LIMIT else v.dtype for v in vals]
    return (quarter * 3 + [pltpu.VMEM(v.shape, v.dtype) for v in vals]
            + [pltpu.VMEM(v.shape, t) for v, t in zip(vals, travel)]
            + [pltpu.VMEM((N_CHIPS,) + v.shape, t) for v, t in zip(vals, travel)]
            + [dma((max(n, 1), N_CHIPS)), dma((max(n, 1), N_CHIPS)), dma((max(ns, 1),)), dma((max(ns, 1),)),
               dma((max(n, 1), 3)), dma((max(n, 1), 3)), dma((max(ns, 1), 3)), dma((max(ns, 1), 3))])


def _reduce_plan(p_in, v_in, p_out, v_out, scratch):
    n, ns = len(p_in), len(v_in)
    p_sib, p_sum, p_all = scratch[:n], scratch[n:2 * n], scratch[2 * n:3 * n]
    v_sib, v_sum, v_all = (scratch[3 * n + j * ns:3 * n + (j + 1) * ns] for j in range(3))
    p1_send, p1_recv, v1_send, v1_recv, p3_send, p3_recv, v3_send, v3_recv = scratch[3 * n + 3 * ns:]
    x, y, c = _position()
    my_q = 2 * x + y
    peers = [(x ^ fx, y ^ fy) for fx, fy in _CHIP_FLIPS]

    def to_sibling(src, dst, send, recv):
        return pltpu.make_async_remote_copy(src_ref=src, dst_ref=dst, send_sem=send, recv_sem=recv,
                                            device_id=(x, y, 1 - c), device_id_type=MESH)

    def level1():
        cps = [to_sibling(p_in[k].at[2 * q + 1 - c], p_sib[k].at[q], p1_send.at[k, q], p1_recv.at[k, q])
               for k in range(n) for q in range(N_CHIPS)]
        return cps + [to_sibling(v_in[k], v_sib[k], v1_send.at[k], v1_recv.at[k]) for k in range(ns)]

    def to_chip(f, src, dst, send, recv):
        px, py = peers[f]
        return pltpu.make_async_remote_copy(src_ref=src, dst_ref=dst, send_sem=send, recv_sem=recv,
                                            device_id=(px, py, c), device_id_type=MESH)

    def level2(sending):
        cps = []
        for f, (px, py) in enumerate(peers):
            their_q = 2 * px + py
            for k in range(n):
                src, dst = (p_sum[k].at[their_q], p_all[k].at[my_q]) if sending else (
                    p_sum[k].at[my_q], p_all[k].at[their_q])
                cps.append(to_chip(f, src, dst, p3_send.at[k, f], p3_recv.at[k, f]))
            for k in range(ns):
                dst = v_all[k].at[my_q] if sending else v_all[k].at[their_q]
                cps.append(to_chip(f, v_sum[k], dst, v3_send.at[k, f], v3_recv.at[k, f]))
        return cps

    def start():
        for cp in level1():
            cp.start()

    def middle():
        for cp in level1():
            cp.wait_recv()
        for k in range(n):
            for q in range(N_CHIPS):
                p_sum[k][q] = (p_in[k][2 * q + c].astype(F32) + p_sib[k][q].astype(F32)).astype(p_sum[k].dtype)
        for k in range(ns):
            v_sum[k][...] = (v_in[k][...] + v_sib[k][...]).astype(v_sum[k].dtype)
        for cp in level2(True):
            cp.start()
        for k in range(n):
            p_all[k][my_q] = p_sum[k][my_q]
        for k in range(ns):
            v_all[k][my_q] = v_sum[k][...]

    def finish():
        for cp in level2(False):
            cp.wait_recv()
        for k in range(n):
            p_out[k][...] = p_all[k][...]
        for k in range(ns):
            total = v_all[k][0].astype(F32)
            for q in range(1, N_CHIPS):
                total = total + v_all[k][q].astype(F32)
            v_out[k][...] = total
        for cp in level1() + level2(True):
            cp.wait_send()

    return start, middle, finish


def _adamw_math(w, g, m, v):
    m = ADAM_B1 * m + (1.0 - ADAM_B1) * g
    v = ADAM_B2 * v + (1.0 - ADAM_B2) * (g * g)
    m_hat = m / (1.0 - ADAM_B1 ** ADAM_STEP)
    v_hat = v / (1.0 - ADAM_B2 ** ADAM_STEP)
    delta = -ADAM_LR * (m_hat / (jnp.sqrt(v_hat) + ADAM_EPS) + ADAM_WD * w)
    return delta, m, v


def _adamw_shard(parts, w, m, v, tr, name):
    R, Cc = w.shape

    def body(p_ref, w_ref, m_ref, v_ref, g_out, d_out, m_out, v_out):
        g = p_ref[0].astype(F32)
        for j in range(1, N_CHIPS):
            g = g + p_ref[j].astype(F32)
        d, mn, vn = _adamw_math(w_ref[...], g, m_ref[...], v_ref[...])
        g_out[...] = g
        d_out[...] = d
        m_out[...] = mn
        v_out[...] = vn

    blk = pl.BlockSpec((tr, Cc), lambda i: (i, 0))
    return pl.pallas_call(
        body, name=name, grid=(R // tr,),
        in_specs=[pl.BlockSpec((N_CHIPS, tr, Cc), lambda i: (0, i, 0)), blk, blk, blk],
        out_specs=[blk] * 4, out_shape=[jax.ShapeDtypeStruct((R, Cc), F32)] * 4,
        compiler_params=pltpu.CompilerParams(dimension_semantics=("parallel",)),
    )(parts, w, m, v)


def _adamw_small(grads, ws, ms, vs):
    n = len(grads)
    vm = pl.BlockSpec(memory_space=pltpu.VMEM)

    def body(*refs):
        g_in, w_in, m_in, v_in = (refs[k * n:(k + 1) * n] for k in range(4))
        g_out, d_out, m_out, v_out = (refs[(4 + k) * n:(5 + k) * n] for k in range(4))
        for k in range(n):
            g = g_in[k][...]
            d, mn, vn = _adamw_math(w_in[k][...], g, m_in[k][...], v_in[k][...])
            g_out[k][...] = g
            d_out[k][...] = d
            m_out[k][...] = mn
            v_out[k][...] = vn

    shapes = [jax.ShapeDtypeStruct(g.shape, F32) for g in grads]
    return pl.pallas_call(
        body, name="adamw_small", out_shape=shapes * 4, in_specs=[vm] * (4 * n), out_specs=[vm] * (4 * n),
    )(*grads, *ws, *ms, *vs)


def _shard_cols(full):
    R, Ct = full.shape
    return jnp.transpose(full.reshape(R, N_DEV, Ct // N_DEV), (1, 0, 2))


_COL_SHARDED = ("w_in", "w_gate", "w_up")
_BIG = ("w_in", "w_out", "w_gate", "w_up", "w_down")


def _rows(nm, p):
    return p[0].T if nm in _COL_SHARDED else p[0]


def _step(args, ts_mix_fwd, ts_ffn, ts_mix_bwd, tm_grad):
    (x, g_mix, w_in, b_in, w_dw, b_dw, ln_g, ln_b, w_pool, s_pool, w_out, g_ffn, w_gate, w_up, w_down, g_final,
     loss_target) = args[:17]
    names = ["g_mix", "w_in", "b_in", "w_dw", "b_dw", "ln_g", "ln_b", "w_pool", "s_pool", "w_out", "g_ffn",
             "w_gate", "w_up", "w_down", "g_final"]
    weights = dict(zip(names, args[1:16]))
    moms = dict(zip(names, args[17:32]))
    vars_ = dict(zip(names, args[32:47]))

    S, D = x.shape[1], x.shape[2]
    x2 = x.reshape(S, D)
    tgt2 = loss_target.reshape(S, D)

    shard = lambda nm: _rows(nm, weights[nm])
    w_pool_b = w_pool[0].astype(BF16)
    dw_shard = jnp.pad(w_dw[0], ((0, SUBLANES * 4 - CONV_WIDTH), (0, 128 - w_dw.shape[2])))

    a, gate, v, m, y, h1, xn, w_dw_f, g_in, g_out, _, g_gate, g_up, g_down = _mix_fwd(
        x2, g_mix, b_in, b_dw, ln_g, ln_b, w_pool_b, s_pool, [shard("w_in"), shard("w_out"), dw_shard],
        [shard("w_gate"), shard("w_up"), shard("w_down")], ts_mix_fwd)
    wt_in, w_out_f = g_in.reshape(-1, D), g_out.reshape(-1, D)
    wt_gate, wt_up, w_down_f = g_gate.reshape(-1, D), g_up.reshape(-1, D), g_down.reshape(-1, D)
    Fd = wt_gate.shape[0]
    f_chunks = [1024] * (Fd // 1024) + ([Fd % 1024] if Fd % 1024 else [])
    dh1, hn, act, dgt, dup, dh2, loss_p, dg_final, dg_ffn = _ffn(
        h1, tgt2, g_ffn, g_final.reshape(1, D), wt_gate, wt_up, w_down_f, ts_ffn, f_chunks)

    by_shard = lambda g: g.reshape(N_DEV, -1, D)
    p_gate = by_shard(_grad_matmul(dgt, hn, tm_grad, "grad_w_gate"))
    g_up, s_gate = _grad_matmul_pair_exchange(dup, hn, tm_grad, "grad_w_up", [p_gate])
    p_up = by_shard(g_up)
    g_down, s_up = _grad_matmul_pair_exchange(act, dh2, tm_grad, "grad_w_down", [p_up])
    p_down = by_shard(g_down)
    s_down, = _pair_exchange([p_down])
    pair_sums = _pair_sum([p_gate, p_up, p_down], [s_gate, s_up, s_down])
    (dx, dz, dh1b, dg_mix, db_in, dw_dw, db_dw, dln_g, dln_b, dw_pool, ds_pool, r_gate, r_up, r_down) = _mix_bwd(
        dh1, x2, a, gate, v, m, g_mix, wt_in, w_dw_f, ln_g, ln_b, w_pool_b, s_pool, w_out_f, pair_sums, ts_mix_bwd)

    small_names = ["g_mix", "b_in", "b_dw", "ln_g", "ln_b", "w_pool", "s_pool", "g_ffn", "g_final"]
    small_shape = lambda p: p.reshape(-1, p.shape[-1])
    partial = [dg_mix, db_in, db_dw, dln_g, dln_b, dw_pool.reshape(-1, POOL_GROUP), ds_pool, dg_ffn, dg_final, loss_p]
    dw_out, *summed = _grad_matmul(y, dh1b, tm_grad, "grad_w_out", reduce_vals=partial)
    dwt_in, r_out, r_dw = _grad_matmul(
        dz, xn, tm_grad, "grad_w_in",
        reduce_parts=[dw_out.reshape(N_DEV, -1, D), _shard_cols(dw_dw[0:CONV_WIDTH])])
    r_in, = _tail_reduce([dwt_in.reshape(N_DEV, -1, D)], [])

    big = {}
    for nm, r in zip(_BIG, [r_in, r_out, r_gate, r_up, r_down]):
        rows = r.shape[1]
        res = _adamw_shard(r, _rows(nm, weights[nm]), _rows(nm, moms[nm]), _rows(nm, vars_[nm]),
                           rows // 2 if rows % 32 == 0 else rows, "adamw_" + nm)
        big[nm] = [o.T if nm in _COL_SHARDED else o for o in res]
    big["w_dw"] = _adamw_shard(r_dw, w_dw[0], moms["w_dw"][0], vars_["w_dw"][0], CONV_WIDTH, "adamw_w_dw")

    sm = _adamw_small(summed[:-1], [small_shape(weights[nm]) for nm in small_names],
                      [small_shape(moms[nm]) for nm in small_names], [small_shape(vars_[nm]) for nm in small_names])
    n_small = len(small_names)

    def result(kind, nm):
        if nm in big:
            return big[nm][kind].reshape(weights[nm].shape)
        return sm[kind * n_small + small_names.index(nm)].reshape(weights[nm].shape)

    loss = summed[-1][0, 0]
    out = [loss, dx.reshape(x.shape)]
    for kind in range(4):
        out += [result(kind, nm) for nm in names]
    return tuple(out)


def kernel(x, g_mix, w_in, b_in, w_dw, b_dw, ln_g, ln_b, w_pool, s_pool, w_out, g_ffn, w_gate, w_up, w_down, g_final, loss_target, m_g_mix, m_w_in, m_b_in, m_w_dw, m_b_dw, m_ln_g, m_ln_b, m_w_pool, m_s_pool, m_w_out, m_g_ffn, m_w_gate, m_w_up, m_w_down, m_g_final, v_g_mix, v_w_in, v_b_in, v_w_dw, v_b_dw, v_ln_g, v_ln_b, v_w_pool, v_s_pool, v_w_out, v_g_ffn, v_w_gate, v_w_up, v_w_down, v_g_final):
    args = (x, g_mix, w_in, b_in, w_dw, b_dw, ln_g, ln_b, w_pool, s_pool, w_out, g_ffn, w_gate, w_up, w_down, g_final, loss_target, m_g_mix, m_w_in, m_b_in, m_w_dw, m_b_dw, m_ln_g, m_ln_b, m_w_pool, m_s_pool, m_w_out, m_g_ffn, m_w_gate, m_w_up, m_w_down, m_g_final, v_g_mix, v_w_in, v_b_in, v_w_dw, v_b_dw, v_ln_g, v_ln_b, v_w_pool, v_s_pool, v_w_out, v_g_ffn, v_w_gate, v_w_up, v_w_down, v_g_final)
    return _step(args, ts_mix_fwd=512, ts_ffn=256, ts_mix_bwd=512, tm_grad=256)
```

```python
import jax
import jax.numpy as jnp
from jax import lax
from jax.experimental import pallas as pl
from jax.experimental.pallas import tpu as pltpu

F32 = jnp.float32
BF16 = jnp.bfloat16
MESH = pl.DeviceIdType.MESH
N_DEV = 8

C_CONV = 512
CONV_WIDTH = 31
POOL_WINDOWS = (2, 4, 8, 16)
POOL_GROUP = 128
RMS_EPS = 1e-6
LN_EPS = 1e-5

ADAM_LR = 0.001
ADAM_B1 = 0.9
ADAM_B2 = 0.999
ADAM_EPS = 1e-08
ADAM_WD = 0.01
ADAM_STEP = 10

HALO = 32
SUBLANES = 8
CONV_ROWS = 64
VMEM_LIMIT = 56 * 1024 * 1024


def _dot(a, b):
    return jnp.dot(a, b, preferred_element_type=F32)


def _dot_nt(a, b):
    return lax.dot_general(a, b, (((1,), (1,)), ((), ())), preferred_element_type=F32)


def _dot_tn(a, b):
    return lax.dot_general(a, b, (((0,), (0,)), ((), ())), preferred_element_type=F32)


def _mean_last(v):
    return jnp.mean(v, axis=-1, keepdims=True)


def _full(shape):
    nd = len(shape)
    return pl.BlockSpec(shape, lambda *_: (0,) * nd)


def _full1(shape):
    nd = len(shape)
    return pl.BlockSpec(shape, lambda *_: (0,) * nd, pipeline_mode=pl.Buffered(1))


def _shifted_copies(sh_ref, rows):
    for s in range(1, SUBLANES):
        sh_ref[s, 0:rows, :] = sh_ref[0, s:s + rows, :]


def _tap(sh_ref, off, r0, rows):
    q, s = divmod(off, SUBLANES)
    return sh_ref[s, pl.ds(r0 + q * SUBLANES, rows), :]


def _mix_fwd(x, g_mix, b_in, b_dw, ln_g, ln_b, w_pool, s_pool, mix_shards, ag_shards, ts):
    S, D = x.shape
    d_in = mix_shards[0].shape[0] * N_DEV
    C = C_CONV
    nt = S // ts
    nrb = ts // CONV_ROWS
    n_ag = len(ag_shards)
    relay_step = (3 * nt) // 8
    mix_dtypes = [BF16, BF16, F32]

    def body(x_ref, g_ref, bin_ref, bdw_ref, lng_ref, lnb_ref, wp_ref, sp_ref, *rest):
        mx_in, ag_in, rest = rest[:3], rest[3:3 + n_ag], rest[3 + n_ag:]
        a_ref, gate_ref, v_ref, m_ref, y_ref, h1_ref, xn_ref, wdw_ref = rest[:8]
        mx_out, ag_out, rest = rest[8:11], rest[11:11 + n_ag], rest[11 + n_ag:]
        ush, pbuf, win_ref, wout_ref, gdw, load_sems = rest[:6]
        rest = rest[6:]
        mx_stage, mx_bufs, mx_sems, rest = rest[:3], rest[3:6], rest[6:9], rest[9:]
        ag_stage, ag_bufs, ag_sems = rest[:n_ag], rest[n_ag:2 * n_ag], rest[2 * n_ag:]
        i = pl.program_id(0)
        mx_start, mx_relay, mx_finish = _gather_plan(mx_in, mx_out, mx_stage, mx_bufs, *mx_sems)
        ag_start, ag_relay, ag_finish = _gather_plan(ag_in, ag_out, ag_stage, ag_bufs, *ag_sems)

        @pl.when(i == 0)
        def _():
            mx_start()
            mx_relay()
            ag_start()
            mx_finish()
            loads = [pltpu.make_async_copy(mx_out[2], gdw, load_sems.at[2 * N_DEV])]
            for j in range(N_DEV):
                r_in, r_out = mx_out[0].shape[1], mx_out[1].shape[1]
                loads += [pltpu.make_async_copy(mx_out[0].at[j], win_ref.at[pl.ds(j * r_in, r_in), :],
                                                load_sems.at[2 * j]),
                          pltpu.make_async_copy(mx_out[1].at[j], wout_ref.at[pl.ds(j * r_out, r_out), :],
                                                load_sems.at[2 * j + 1])]
            for cp in loads:
                cp.start()
            for cp in loads:
                cp.wait()
            first_half = lax.broadcasted_iota(jnp.int32, gdw.shape[1:], 1) < C // N_DEV
            for p in range(N_DEV // 2):
                wdw_ref[:, 128 * p:128 * (p + 1)] = jnp.where(
                    first_half, gdw[2 * p], pltpu.roll(gdw[2 * p + 1], C // N_DEV, axis=1))
            ush[0, 0:HALO, :] = jnp.zeros((HALO, C), F32)
            pbuf[0:HALO, :] = jnp.zeros((HALO, C), F32)

        @pl.when(i == relay_step)
        def _():
            ag_relay()

        x = x_ref[...]
        r1 = lax.rsqrt(_mean_last(x * x) + RMS_EPS)
        xn = (x * r1 * g_ref[...]).astype(BF16)
        xn_ref[...] = xn
        z = _dot_nt(xn, win_ref[...]) + bin_ref[...]
        a = z[:, 0:C]
        gate = z[:, C:2 * C]
        a_ref[...] = a
        gate_ref[...] = gate
        ush[0, HALO:HALO + ts, :] = a * jax.nn.sigmoid(gate)
        pbuf[HALO:HALO + ts, :] = z[:, 2 * C:]

        _shifted_copies(ush, ts + HALO - SUBLANES)

        def conv_block(rb, carry):
            r0 = pl.multiple_of(rb * CONV_ROWS, CONV_ROWS)
            acc = jnp.zeros((CONV_ROWS, C), F32)
            for k in range(CONV_WIDTH):
                acc = acc + wdw_ref[k:k + 1, :] * _tap(ush, HALO - (CONV_WIDTH - 1) + k, r0, CONV_ROWS)
            v_ref[pl.ds(r0, CONV_ROWS), :] = acc + bdw_ref[...]
            return carry

        lax.fori_loop(0, nrb, conv_block, 0)

        v = v_ref[...]
        mu = _mean_last(v)
        xc = v - mu
        rstd = lax.rsqrt(_mean_last(xc * xc) + LN_EPS)
        ln = xc * rstd * lng_ref[...] + lnb_ref[...]
        y_ref[:, 0:C] = (ln * jax.nn.sigmoid(ln)).astype(BF16)

        row = lax.broadcasted_iota(jnp.int32, (ts, 1), 0) + i * ts
        for gi, w in enumerate(POOL_WINDOWS):
            lanes = slice(gi * POOL_GROUP, (gi + 1) * POOL_GROUP)
            seg = pbuf[HALO:HALO + ts, lanes]
            ws = seg
            for k in range(1, w):
                ws = ws + pbuf[HALO - k:HALO - k + ts, lanes]
            cnt = jnp.minimum(row + 1, w).astype(F32)
            m = (ws / cnt - seg).astype(BF16)
            m_ref[:, lanes] = m
            ypre = _dot(m, wp_ref[gi])
            y_ref[:, C + gi * POOL_GROUP:C + (gi + 1) * POOL_GROUP] = (ypre * sp_ref[:, lanes]).astype(BF16)

        h1_ref[...] = x + _dot(y_ref[...], wout_ref[...])

        ush[0, 0:HALO, :] = ush[0, ts:ts + HALO, :]
        pbuf[0:HALO, :] = pbuf[ts:ts + HALO, :]

        @pl.when(i == nt - 1)
        def _():
            ag_finish()

    tile = lambda w, dt: (pl.BlockSpec((ts, w), lambda i: (i, 0)), jax.ShapeDtypeStruct((S, w), dt))
    wdw_rows = mix_shards[2].shape[0]
    outs = [tile(C, F32), tile(C, F32), tile(C, F32), tile(C, BF16), tile(D, BF16), tile(D, F32), tile(D, BF16),
            (_full((wdw_rows, C)), jax.ShapeDtypeStruct((wdw_rows, C), F32))]
    return pl.pallas_call(
        body, name="mix_fwd", grid=(nt,),
        in_specs=[pl.BlockSpec((ts, D), lambda i: (i, 0)), _full((1, D)), _full((1, d_in)), _full((1, C)),
                  _full((1, C)), _full((1, C)), _full(w_pool.shape), _full((1, C))] + [_HBM] * (3 + n_ag),
        out_specs=[o[0] for o in outs] + [_HBM] * (3 + n_ag),
        out_shape=[o[1] for o in outs] + _gather_out_shapes(mix_shards, mix_dtypes)
        + _gather_out_shapes(ag_shards, [BF16] * n_ag),
        scratch_shapes=[pltpu.VMEM((SUBLANES, ts + HALO, C), F32), pltpu.VMEM((ts + HALO, C), F32),
                        pltpu.VMEM((d_in, D), BF16), pltpu.VMEM((D, D), BF16),
                        pltpu.VMEM((N_DEV,) + mix_shards[2].shape, F32), pltpu.SemaphoreType.DMA((2 * N_DEV + 1,))]
        + _gather_scratch(mix_shards, mix_dtypes) + _gather_scratch(ag_shards, [BF16] * n_ag),
        compiler_params=pltpu.CompilerParams(dimension_semantics=("arbitrary",), vmem_limit_bytes=VMEM_LIMIT),
    )(x, g_mix, b_in, b_dw, ln_g, ln_b, w_pool, s_pool, *mix_shards, *ag_shards)


def _ffn(h1, target, g_ffn, g_final, w_gate, w_up, w_down, ts, f_chunks):
    S, D = h1.shape
    Fd = w_gate.shape[0]
    nt = S // ts
    bounds = []
    lo = 0
    for n in f_chunks:
        bounds.append((lo, lo + n))
        lo += n
    assert lo == Fd

    def body(h1_ref, tgt_ref, gf_ref, gl_ref, wg_ref, wu_ref, wd_ref,
             dh1_ref, hn_ref, act_ref, dgt_ref, dup_ref, dh2_ref, loss_ref, dgl_ref, dgf_ref, gt_s, up_s):
        i = pl.program_id(0)

        @pl.when(i == 0)
        def _():
            loss_ref[...] = jnp.zeros_like(loss_ref)
            dgl_ref[...] = jnp.zeros_like(dgl_ref)
            dgf_ref[...] = jnp.zeros_like(dgf_ref)

        h1 = h1_ref[...]
        r2 = lax.rsqrt(_mean_last(h1 * h1) + RMS_EPS)
        hhat = h1 * r2
        hn = (hhat * gf_ref[...]).astype(BF16)
        hn_ref[...] = hn
        h2 = h1
        for lo, hi in bounds:
            gt = _dot_nt(hn, wg_ref[lo:hi, :])
            up = _dot_nt(hn, wu_ref[lo:hi, :])
            gt_s[:, lo:hi] = gt
            up_s[:, lo:hi] = up
            act = (gt * jax.nn.sigmoid(gt) * up).astype(BF16)
            act_ref[:, lo:hi] = act
            h2 = h2 + _dot(act, wd_ref[lo:hi, :])

        r3 = lax.rsqrt(_mean_last(h2 * h2) + RMS_EPS)
        n3 = h2 * r3
        gl = gl_ref[...]
        diff = n3 * gl - tgt_ref[...]
        loss_ref[...] += jnp.sum(0.5 * jnp.sum(diff * diff, axis=-1, keepdims=True) / D, axis=0, keepdims=True)
        dout = diff / D
        dgl_ref[...] += jnp.sum(dout * n3, axis=0, keepdims=True)
        dn = dout * gl
        dh2 = r3 * (dn - n3 * _mean_last(dn * n3))
        dh2b = dh2.astype(BF16)
        dh2_ref[...] = dh2b

        dhn = jnp.zeros((ts, D), F32)
        for lo, hi in bounds:
            gt = gt_s[:, lo:hi]
            up = up_s[:, lo:hi]
            sg = jax.nn.sigmoid(gt)
            dact = _dot_nt(dh2b, wd_ref[lo:hi, :])
            dgt = (dact * up * (sg * (1.0 + gt * (1.0 - sg)))).astype(BF16)
            dup = (dact * (gt * sg)).astype(BF16)
            dgt_ref[:, lo:hi] = dgt
            dup_ref[:, lo:hi] = dup
            dhn = dhn + _dot(dgt, wg_ref[lo:hi, :]) + _dot(dup, wu_ref[lo:hi, :])

        dgf_ref[...] += jnp.sum(dhn * hhat, axis=0, keepdims=True)
        dnn = dhn * gf_ref[...]
        dh1_ref[...] = dh2 + r2 * (dnn - hhat * _mean_last(dnn * hhat))

    tile = lambda w, dt: (pl.BlockSpec((ts, w), lambda i: (i, 0)), jax.ShapeDtypeStruct((S, w), dt))
    acc = lambda w: (_full((1, w)), jax.ShapeDtypeStruct((1, w), F32))
    outs = [tile(D, F32), tile(D, BF16), tile(Fd, BF16), tile(Fd, BF16), tile(Fd, BF16), tile(D, BF16),
            acc(128), acc(D), acc(D)]
    return pl.pallas_call(
        body, name="ffn_fwd_bwd", grid=(nt,),
        in_specs=[pl.BlockSpec((ts, D), lambda i: (i, 0)), pl.BlockSpec((ts, D), lambda i: (i, 0)),
                  _full((1, D)), _full((1, D)), _full1((Fd, D)), _full1((Fd, D)), _full1((Fd, D))],
        out_specs=[o[0] for o in outs], out_shape=[o[1] for o in outs],
        scratch_shapes=[pltpu.VMEM((ts, Fd), F32), pltpu.VMEM((ts, Fd), F32)],
        compiler_params=pltpu.CompilerParams(dimension_semantics=("arbitrary",), vmem_limit_bytes=VMEM_LIMIT),
    )(h1, target, g_ffn, g_final, w_gate, w_up, w_down)


def _mix_bwd(dh1, x, a, gate, v, m, g_mix, w_in, w_dw, ln_g, ln_b, w_pool, s_pool, w_out, rs_parts, ts):
    S, D = x.shape
    n_rs = len(rs_parts)
    d_in = w_in.shape[0]
    C = C_CONV
    nt = S // ts
    nrb = ts // CONV_ROWS
    wrows =((CONV_WIDTH + SUBLANES - 1) // SUBLANES) * SUBLANES

    def body(dh1_ref, x_ref, a_ref, gate_ref, v_ref, m_ref, g_ref, win_ref, wdw_ref, lng_ref,
             lnb_ref, wp_ref, sp_ref, wout_ref, *rest):
        rs_in, rest = rest[:n_rs], rest[n_rs:]
        (dx_ref, dz_ref, dh1b_ref, dgm_ref, dbin_ref, dwdw_ref, dbdw_ref, dlng_ref, dlnb_ref, dwp_ref,
         dsp_ref) = rest[:11]
        rs_out, rest = rest[11:11 + n_rs], rest[11 + n_rs:]
        dvsh, dqbuf, du_s = rest[:3]
        rs_bufs, (send_sems, recv_sems, local_sems) = rest[3:3 + n_rs], rest[3 + n_rs:]
        i = pl.program_id(0)
        t = nt - 1 - i
        rs_start, rs_finish = _chip_exchange_plan(rs_in, rs_out, rs_bufs, send_sems, recv_sems, local_sems)

        @pl.when(i == 0)
        def _():
            rs_start()
            dvsh[0, ts:ts + HALO, :] = jnp.zeros((HALO, C), F32)
            dqbuf[ts:ts + HALO, :] = jnp.zeros((HALO, C), F32)
            for r in (dgm_ref, dbin_ref, dwdw_ref, dbdw_ref, dlng_ref, dlnb_ref, dwp_ref, dsp_ref):
                r[...] = jnp.zeros_like(r)

        dh1 = dh1_ref[...]
        dh1b = dh1.astype(BF16)
        dh1b_ref[...] = dh1b
        dy = _dot_nt(dh1b, wout_ref[...])

        v = v_ref[...]
        mu = _mean_last(v)
        xc = v - mu
        rstd = lax.rsqrt(_mean_last(xc * xc) + LN_EPS)
        vhat = xc * rstd
        lng = lng_ref[...]
        ln = vhat * lng + lnb_ref[...]
        sg = jax.nn.sigmoid(ln)
        dln = dy[:, 0:C] * (sg * (1.0 + ln * (1.0 - sg)))
        dlng_ref[...] += jnp.sum(dln * vhat, axis=0, keepdims=True)
        dlnb_ref[...] += jnp.sum(dln, axis=0, keepdims=True)
        dvh = dln * lng
        dv = rstd * (dvh - _mean_last(dvh) - vhat * _mean_last(dvh * vhat))
        dbdw_ref[...] += jnp.sum(dv, axis=0, keepdims=True)
        dvsh[0, 0:ts, :] = dv
        _shifted_copies(dvsh, ts + HALO - SUBLANES)

        def conv_block(rb, carry):
            r0 = pl.multiple_of(rb * CONV_ROWS, CONV_ROWS)
            acc = jnp.zeros((CONV_ROWS, C), F32)
            for k in range(CONV_WIDTH):
                acc = acc + wdw_ref[k:k + 1, :] * _tap(dvsh, CONV_WIDTH - 1 - k, r0, CONV_ROWS)
            du_s[pl.ds(r0, CONV_ROWS), :] = acc
            return carry

        lax.fori_loop(0, nrb, conv_block, 0)

        a = a_ref[...]
        sgate = jax.nn.sigmoid(gate_ref[...])
        u = a * sgate
        for k in range(CONV_WIDTH):
            q, s = divmod(CONV_WIDTH - 1 - k, SUBLANES)
            prod = u * dvsh[s, q * SUBLANES:q * SUBLANES + ts, :]
            dwdw_ref[k:k + 1, :] += jnp.sum(prod, axis=0, keepdims=True)

        du = du_s[...]
        da = du * sgate
        dgate = du * a * sgate * (1.0 - sgate)
        dz_ref[:, 0:C] = da.astype(BF16)
        dz_ref[:, C:2 * C] = dgate.astype(BF16)
        dbin_ref[:, 0:C] += jnp.sum(da, axis=0, keepdims=True)
        dbin_ref[:, C:2 * C] += jnp.sum(dgate, axis=0, keepdims=True)

        row = lax.broadcasted_iota(jnp.int32, (ts, 1), 0) + t * ts
        for gi, w in enumerate(POOL_WINDOWS):
            lanes = slice(gi * POOL_GROUP, (gi + 1) * POOL_GROUP)
            dyp = dy[:, C + gi * POOL_GROUP:C + (gi + 1) * POOL_GROUP]
            mg = m_ref[:, lanes]
            ypre = _dot(mg, wp_ref[gi])
            dsp_ref[:, lanes] += jnp.sum(dyp * ypre, axis=0, keepdims=True)
            dyi = (dyp * sp_ref[:, lanes]).astype(BF16)
            dwp_ref[gi] += _dot_tn(mg, dyi)
            dm = _dot_nt(dyi, wp_ref[gi])
            cnt = jnp.minimum(row + 1, w).astype(F32)
            dqbuf[0:ts, lanes] = dm / cnt
            dp = -dm
            for k in range(w):
                dp = dp + dqbuf[k:k + ts, lanes]
            dz_ref[:, 2 * C + gi * POOL_GROUP:2 * C + (gi + 1) * POOL_GROUP] = dp.astype(BF16)
            dbin_ref[:, 2 * C + gi * POOL_GROUP:2 * C + (gi + 1) * POOL_GROUP] += jnp.sum(dp, axis=0, keepdims=True)

        dxn = _dot(dz_ref[...], win_ref[...])
        x = x_ref[...]
        r1 = lax.rsqrt(_mean_last(x * x) + RMS_EPS)
        xhat = x * r1
        dgm_ref[...] += jnp.sum(dxn * xhat, axis=0, keepdims=True)
        dnn = dxn * g_ref[...]
        dx_ref[...] = dh1 + r1 * (dnn - xhat * _mean_last(dnn * xhat))

        dvsh[0, ts:ts + HALO, :] = dvsh[0, 0:HALO, :]
        dqbuf[ts:ts + HALO, :] = dqbuf[0:HALO, :]

        @pl.when(i == nt - 1)
        def _():
            rs_finish()

    rev = lambda w: pl.BlockSpec((ts, w), lambda i: (nt - 1 - i, 0))
    acc = lambda shape: (_full(shape), jax.ShapeDtypeStruct(shape, F32))
    outs = [(rev(D), jax.ShapeDtypeStruct((S, D), F32)), (rev(d_in), jax.ShapeDtypeStruct((S, d_in), BF16)),
            (rev(D), jax.ShapeDtypeStruct((S, D), BF16)), acc((1, D)), acc((1, d_in)), acc((wrows, C)),
            acc((1, C)), acc((1, C)), acc((1, C)), acc(w_pool.shape), acc((1, C))]
    return pl.pallas_call(
        body, name="mix_bwd", grid=(nt,),
        in_specs=[rev(D), rev(D), rev(C), rev(C), rev(C), rev(C), _full((1, D)), _full((d_in, D)),
                  _full(w_dw.shape), _full((1, C)), _full((1, C)), _full(w_pool.shape), _full((1, C)),
                  _full((D, D))] + [_HBM] * n_rs,
        out_specs=[o[0] for o in outs] + [_HBM] * n_rs,
        out_shape=[o[1] for o in outs] + [jax.ShapeDtypeStruct(p.shape, p.dtype) for p in rs_parts],
        scratch_shapes=[pltpu.VMEM((SUBLANES, ts + HALO, C), F32), pltpu.VMEM((ts + HALO, C), F32),
                        pltpu.VMEM((ts, C), F32)]
        + [pltpu.VMEM(p.shape[1:], p.dtype) for p in rs_parts] + _comm_sems(n_rs, 3),
        compiler_params=pltpu.CompilerParams(dimension_semantics=("arbitrary",), vmem_limit_bytes=VMEM_LIMIT),
    )(dh1, x, a, gate, v, m, g_mix, w_in, w_dw, ln_g, ln_b, w_pool, s_pool, w_out, *rs_parts)


def _grad_matmul(a, b, tm, name, reduce_parts=(), reduce_vals=()):
    S, M = a.shape
    N = b.shape[1]
    steps = M // tm
    n, ns = len(reduce_parts), len(reduce_vals)

    def body(a_ref, b_ref, *rest):
        o_ref = rest[n + ns]
        if n + ns:
            start, middle, finish = _reduce_plan(rest[:n], rest[n:n + ns], rest[n + ns + 1:2 * n + ns + 1],
                                                 rest[2 * n + ns + 1:2 * (n + ns) + 1], rest[2 * (n + ns) + 1:])
            i = pl.program_id(0)
            pl.when(i == 0)(start)
            pl.when(i == 1)(middle)
        o_ref[...] = _dot_tn(a_ref[...], b_ref[...]).astype(BF16)
        if n + ns:
            pl.when(i == steps - 1)(finish)

    res = pl.pallas_call(
        body, name=name, grid=(steps,),
        in_specs=[pl.BlockSpec((S, tm), lambda i: (0, i)), _full1((S, N))] + [_VMEM] * (n + ns),
        out_specs=[pl.BlockSpec((tm, N), lambda i: (i, 0))] + [_VMEM] * (n + ns),
        out_shape=[jax.ShapeDtypeStruct((M, N), BF16)] + _reduce_out_shapes(reduce_parts, reduce_vals),
        scratch_shapes=_reduce_scratch(reduce_parts, reduce_vals) if n + ns else [],
        compiler_params=pltpu.CompilerParams(dimension_semantics=("arbitrary",) if n + ns else ("parallel",),
                                             vmem_limit_bytes=VMEM_LIMIT),
    )(a, b, *reduce_parts, *reduce_vals)
    return res if n + ns else res[0]


def _grad_matmul_pair_exchange(a, b, tm, name, parts):
    S, M = a.shape
    N = b.shape[1]
    steps = M // tm
    n = len(parts)

    def body(a_ref, b_ref, *rest):
        o_ref = rest[n]
        start, finish = _pair_exchange_plan(rest[:n], rest[n + 1:2 * n + 1], *rest[2 * n + 1:])
        i = pl.program_id(0)
        pl.when(i == 0)(start)
        o_ref[...] = _dot_tn(a_ref[...], b_ref[...]).astype(BF16)
        pl.when(i == steps - 1)(finish)

    sems = pltpu.SemaphoreType.DMA((n, N_CHIPS))
    return pl.pallas_call(
        body, name=name, grid=(steps,),
        in_specs=[pl.BlockSpec((S, tm), lambda i: (0, i)), _full1((S, N))] + [_HBM] * n,
        out_specs=[pl.BlockSpec((tm, N), lambda i: (i, 0))] + [_HBM] * n,
        out_shape=[jax.ShapeDtypeStruct((M, N), BF16)] + _pair_out_shapes(parts),
        scratch_shapes=[sems, sems],
        compiler_params=pltpu.CompilerParams(dimension_semantics=("arbitrary",), vmem_limit_bytes=VMEM_LIMIT),
    )(a, b, *parts)


def _position():
    return lax.axis_index("x"), lax.axis_index("y"), lax.axis_index("c")


def _slot(px, py, pc):
    return 4 * px + 2 * py + pc


_HBM = pl.BlockSpec(memory_space=pl.ANY)


def _comm_sems(n, copies):
    return [pltpu.SemaphoreType.DMA((n, copies)), pltpu.SemaphoreType.DMA((n, copies)),
            pltpu.SemaphoreType.DMA((n, 2))]


def _local_copy(srcs, dsts, bufs, local_sems):
    n = len(srcs)
    loads = [pltpu.make_async_copy(srcs[k], bufs[k], local_sems.at[k, 0]) for k in range(n)]
    for cp in loads:
        cp.start()
    for cp in loads:
        cp.wait()
    stores = _local_stores(dsts, bufs, local_sems)
    for cp in stores:
        cp.start()
    return stores


def _local_stores(dsts, bufs, local_sems):
    return [pltpu.make_async_copy(bufs[k], dsts[k], local_sems.at[k, 1]) for k in range(len(dsts))]


def _gather_out_shapes(shards, dtypes):
    return [jax.ShapeDtypeStruct((N_DEV,) + s.shape, dt) for s, dt in zip(shards, dtypes)]


def _gather_scratch(shards, dtypes):
    return ([pltpu.VMEM(s.shape, s.dtype) for s in shards] + [pltpu.VMEM(s.shape, dt) for s, dt in zip(shards, dtypes)]
            + _comm_sems(len(shards), 7))


def _gather_plan(ins, outs, stage, bufs, send_sems, recv_sems, local_sems):
    n = len(ins)
    x, y, c = _position()
    me, sibling = (x, y, c), (x, y, 1 - c)
    na, nb, dg = (x ^ (1 - c), y ^ c), (x ^ c, y ^ (1 - c)), (1 - x, 1 - y)
    own = [outs[k].at[_slot(*me)] for k in range(n)]

    def copy(k, sem, block, to, src=None):
        dst = outs[k].at[_slot(*block)]
        return pltpu.make_async_remote_copy(
            src_ref=dst if src is None else src, dst_ref=dst, send_sem=send_sems.at[k, sem],
            recv_sem=recv_sems.at[k, sem], device_id=to, device_id_type=MESH)

    def first():
        cps = []
        for k in range(n):
            cps += [copy(k, 0, me, sibling, src=bufs[k]), copy(k, 1, me, (*na, c), src=bufs[k]),
                    copy(k, 2, me, (*nb, c), src=bufs[k])]
        return cps

    def onward():
        return [copy(k, 3, (*na, c), (*nb, c)) for k in range(n)]

    def to_sibling(j, chip):
        return [copy(k, 4 + j, (*chip, c), sibling) for k in range(n)]

    def start():
        loads = [pltpu.make_async_copy(ins[k], stage[k], local_sems.at[k, 0]) for k in range(n)]
        for cp in loads:
            cp.start()
        for cp in loads:
            cp.wait()
        for k in range(n):
            bufs[k][...] = stage[k][...].astype(bufs[k].dtype)
        for cp in first() + _local_stores(own, bufs, local_sems):
            cp.start()

    def relay():
        for k in range(n):
            copy(k, 1, (*na, c), me).wait_recv()
        for cp in onward() + to_sibling(0, na):
            cp.start()
        for k in range(n):
            copy(k, 2, (*nb, c), me).wait_recv()
        for cp in to_sibling(1, nb):
            cp.start()

    def finish():
        for k in range(n):
            copy(k, 3, (*dg, c), me).wait_recv()
        for cp in to_sibling(2, dg):
            cp.start()
        for k in range(n):
            copy(k, 0, sibling, me).wait_recv()
            for j, chip in enumerate((nb, na, dg)):
                copy(k, 4 + j, (*chip, 1 - c), me).wait_recv()
        for cp in first() + onward() + to_sibling(0, na) + to_sibling(1, nb) + to_sibling(2, dg):
            cp.wait_send()
        for cp in _local_stores(own, bufs, local_sems):
            cp.wait()

    return start, relay, finish


N_CHIPS = 4
_CHIP_FLIPS = [(1, 0), (0, 1), (1, 1)]
F32_TRAVEL_LIMIT = 4096


def _chip_exchange_plan(ins, outs, bufs, send_sems, recv_sems, local_sems):
    n = len(ins)
    x, y, c = _position()
    my_q = 2 * x + y
    peers = [(x ^ fx, y ^ fy) for fx, fy in _CHIP_FLIPS]
    own = [outs[k].at[my_q] for k in range(n)]

    def sends():
        return [pltpu.make_async_remote_copy(
            src_ref=ins[k].at[2 * px + py], dst_ref=outs[k].at[my_q], send_sem=send_sems.at[k, f],
            recv_sem=recv_sems.at[k, f], device_id=(px, py, c), device_id_type=MESH)
            for f, (px, py) in enumerate(peers) for k in range(n)]

    def start():
        for cp in sends():
            cp.start()
        _local_copy([ins[k].at[my_q] for k in range(n)], own, bufs, local_sems)

    def finish():
        for f, (px, py) in enumerate(peers):
            for k in range(n):
                pltpu.make_async_remote_copy(
                    src_ref=ins[k].at[my_q], dst_ref=outs[k].at[2 * px + py], send_sem=send_sems.at[k, f],
                    recv_sem=recv_sems.at[k, f], device_id=(px, py, c), device_id_type=MESH).wait_recv()
        for cp in sends():
            cp.wait_send()
        for cp in _local_stores(own, bufs, local_sems):
            cp.wait()

    return start, finish


def _pair_exchange(parts):
    n = len(parts)

    def body(*refs):
        start, finish = _pair_exchange_plan(refs[:n], refs[n:2 * n], *refs[2 * n:])
        start()
        finish()

    sems = pltpu.SemaphoreType.DMA((n, N_CHIPS))
    return pl.pallas_call(
        body, name="pair_exchange", out_shape=_pair_out_shapes(parts),
        in_specs=[_HBM] * n, out_specs=[_HBM] * n, scratch_shapes=[sems, sems],
    )(*parts)


def _pair_out_shapes(parts):
    return [jax.ShapeDtypeStruct((N_CHIPS,) + p.shape[1:], p.dtype) for p in parts]


def _pair_exchange_plan(ins, sib, send_sems, recv_sems):
    x, y, c = _position()

    def copies():
        return [pltpu.make_async_remote_copy(
            src_ref=ins[k].at[2 * q + 1 - c], dst_ref=sib[k].at[q], send_sem=send_sems.at[k, q],
            recv_sem=recv_sems.at[k, q], device_id=(x, y, 1 - c), device_id_type=MESH)
            for k in range(len(ins)) for q in range(N_CHIPS)]

    def start():
        for cp in copies():
            cp.start()

    def finish():
        for cp in copies():
            cp.wait()

    return start, finish


def _pair_sum(parts, sib):
    n = len(parts)

    def body(*refs):
        c = lax.axis_index("c")
        for k in range(n):
            refs[2 * n + k][0] = (refs[k][0, c].astype(F32) + refs[n + k][0].astype(F32)).astype(BF16)

    pair = [pl.BlockSpec((1, 2) + p.shape[1:], lambda q: (q, 0, 0, 0)) for p in parts]
    one = [pl.BlockSpec((1,) + p.shape[1:], lambda q: (q, 0, 0)) for p in parts]
    return pl.pallas_call(
        body, name="pair_sum", grid=(N_CHIPS,), in_specs=pair + one, out_specs=one,
        out_shape=[jax.ShapeDtypeStruct(s.shape, BF16) for s in sib],
        compiler_params=pltpu.CompilerParams(dimension_semantics=("parallel",)),
    )(*[p.reshape((N_CHIPS, 2) + p.shape[1:]) for p in parts], *sib)


def _tail_reduce(parts, vals):
    n, ns = len(parts), len(vals)

    def body(*refs):
        start, middle, finish = _reduce_plan(refs[:n], refs[n:n + ns], refs[n + ns:2 * n + ns],
                                             refs[2 * n + ns:2 * (n + ns)], refs[2 * (n + ns):])
        start()
        middle()
        finish()

    return pl.pallas_call(
        body, name="tail_reduce", out_shape=_reduce_out_shapes(parts, vals),
        in_specs=[_VMEM] * (n + ns), out_specs=[_VMEM] * (n + ns), scratch_shapes=_reduce_scratch(parts, vals),
        compiler_params=pltpu.CompilerParams(vmem_limit_bytes=VMEM_LIMIT),
    )(*parts, *vals)


_VMEM = pl.BlockSpec(memory_space=pltpu.VMEM)


def _reduce_out_shapes(parts, vals):
    return ([jax.ShapeDtypeStruct((N_CHIPS,) + p.shape[1:], p.dtype) for p in parts]
            + [jax.ShapeDtypeStruct(v.shape, v.dtype) for v in vals])


def _reduce_scratch(parts, vals):
    n, ns = len(parts), len(vals)
    quarter = [pltpu.VMEM((N_CHIPS,) + p.shape[1:], p.dtype) for p in parts]
    dma = pltpu.SemaphoreType.DMA
    travel = [BF16 if v.size > F32_TRAVEL_LIMIT else v.dtype for v in vals]
    return (quarter * 3 + [pltpu.VMEM(v.shape, v.dtype) for v in vals]
            + [pltpu.VMEM(v.shape, t) for v, t in zip(vals, travel)]
            + [pltpu.VMEM((N_CHIPS,) + v.shape, t) for v, t in zip(vals, travel)]
            + [dma((max(n, 1), N_CHIPS)), dma((max(n, 1), N_CHIPS)), dma((max(ns, 1),)), dma((max(ns, 1),)),
               dma((max(n, 1), 3)), dma((max(n, 1), 3)), dma((max(ns, 1), 3)), dma((max(ns, 1), 3))])


def _reduce_plan(p_in, v_in, p_out, v_out, scratch):
    n, ns = len(p_in), len(v_in)
    p_sib, p_sum, p_all = scratch[:n], scratch[n:2 * n], scratch[2 * n:3 * n]
    v_sib, v_sum, v_all = (scratch[3 * n + j * ns:3 * n + (j + 1) * ns] for j in range(3))
    p1_send, p1_recv, v1_send, v1_recv, p3_send, p3_recv, v3_send, v3_recv = scratch[3 * n + 3 * ns:]
    x, y, c = _position()
    my_q = 2 * x + y
    peers = [(x ^ fx, y ^ fy) for fx, fy in _CHIP_FLIPS]

    def to_sibling(src, dst, send, recv):
        return pltpu.make_async_remote_copy(src_ref=src, dst_ref=dst, send_sem=send, recv_sem=recv,
                                            device_id=(x, y, 1 - c), device_id_type=MESH)

    def level1():
        cps = [to_sibling(p_in[k].at[2 * q + 1 - c], p_sib[k].at[q], p1_send.at[k, q], p1_recv.at[k, q])
               for k in range(n) for q in range(N_CHIPS)]
        return cps + [to_sibling(v_in[k], v_sib[k], v1_send.at[k], v1_recv.at[k]) for k in range(ns)]

    def to_chip(f, src, dst, send, recv):
        px, py = peers[f]
        return pltpu.make_async_remote_copy(src_ref=src, dst_ref=dst, send_sem=send, recv_sem=recv,
                                            device_id=(px, py, c), device_id_type=MESH)

    def level2(sending):
        cps = []
        for f, (px, py) in enumerate(peers):
            their_q = 2 * px + py
            for k in range(n):
                src, dst = (p_sum[k].at[their_q], p_all[k].at[my_q]) if sending else (
                    p_sum[k].at[my_q], p_all[k].at[their_q])
                cps.append(to_chip(f, src, dst, p3_send.at[k, f], p3_recv.at[k, f]))
            for k in range(ns):
                dst = v_all[k].at[my_q] if sending else v_all[k].at[their_q]
                cps.append(to_chip(f, v_sum[k], dst, v3_send.at[k, f], v3_recv.at[k, f]))
        return cps

    def start():
        for cp in level1():
            cp.start()

    def middle():
        for cp in level1():
            cp.wait_recv()
        for k in range(n):
            for q in range(N_CHIPS):
                p_sum[k][q] = (p_in[k][2 * q + c].astype(F32) + p_sib[k][q].astype(F32)).astype(p_sum[k].dtype)
        for k in range(ns):
            v_sum[k][...] = (v_in[k][...] + v_sib[k][...]).astype(v_sum[k].dtype)
        for cp in level2(True):
            cp.start()
        for k in range(n):
            p_all[k][my_q] = p_sum[k][my_q]
        for k in range(ns):
            v_all[k][my_q] = v_sum[k][...]

    def finish():
        for cp in level2(False):
            cp.wait_recv()
        for k in range(n):
            p_out[k][...] = p_all[k][...]
        for k in range(ns):
            total = v_all[k][0].astype(F32)
            for q in range(1, N_CHIPS):
                total = total + v_all[k][q].astype(F32)
            v_out[k][...] = total
        for cp in level1() + level2(True):
            cp.wait_send()

    return start, middle, finish


def _adamw_math(w, g, m, v):
    m = ADAM_B1 * m + (1.0 - ADAM_B1) * g
    v = ADAM_B2 * v + (1.0 - ADAM_B2) * (g * g)
    m_hat = m / (1.0 - ADAM_B1 ** ADAM_STEP)
    v_hat = v / (1.0 - ADAM_B2 ** ADAM_STEP)
    delta = -ADAM_LR * (m_hat / (jnp.sqrt(v_hat) + ADAM_EPS) + ADAM_WD * w)
    return delta, m, v


def _adamw_shard(parts, w, m, v, tr, name):
    R, Cc = w.shape

    def body(p_ref, w_ref, m_ref, v_ref, g_out, d_out, m_out, v_out):
        g = p_ref[0].astype(F32)
        for j in range(1, N_CHIPS):
            g = g + p_ref[j].astype(F32)
        d, mn, vn = _adamw_math(w_ref[...], g, m_ref[...], v_ref[...])
        g_out[...] = g
        d_out[...] = d
        m_out[...] = mn
        v_out[...] = vn

    blk = pl.BlockSpec((tr, Cc), lambda i: (i, 0))
    return pl.pallas_call(
        body, name=name, grid=(R // tr,),
        in_specs=[pl.BlockSpec((N_CHIPS, tr, Cc), lambda i: (0, i, 0)), blk, blk, blk],
        out_specs=[blk] * 4, out_shape=[jax.ShapeDtypeStruct((R, Cc), F32)] * 4,
        compiler_params=pltpu.CompilerParams(dimension_semantics=("parallel",)),
    )(parts, w, m, v)


def _adamw_small(grads, ws, ms, vs):
    n = len(grads)
    vm = pl.BlockSpec(memory_space=pltpu.VMEM)

    def body(*refs):
        g_in, w_in, m_in, v_in = (refs[k * n:(k + 1) * n] for k in range(4))
        g_out, d_out, m_out, v_out = (refs[(4 + k) * n:(5 + k) * n] for k in range(4))
        for k in range(n):
            g = g_in[k][...]
            d, mn, vn = _adamw_math(w_in[k][...], g, m_in[k][...], v_in[k][...])
            g_out[k][...] = g
            d_out[k][...] = d
            m_out[k][...] = mn
            v_out[k][...] = vn

    shapes = [jax.ShapeDtypeStruct(g.shape, F32) for g in grads]
    return pl.pallas_call(
        body, name="adamw_small", out_shape=shapes * 4, in_specs=[vm] * (4 * n), out_specs=[vm] * (4 * n),
    )(*grads, *ws, *ms, *vs)


def _shard_cols(full):
    R, Ct = full.shape
    return jnp.transpose(full.reshape(R, N_DEV, Ct // N_DEV), (1, 0, 2))


_COL_SHARDED = ("w_in", "w_gate", "w_up")
_BIG = ("w_in", "w_out", "w_gate", "w_up", "w_down")


def _rows(nm, p):
    return p[0].T if nm in _COL_SHARDED else p[0]


def _step(args, ts_mix_fwd, ts_ffn, ts_mix_bwd, tm_grad):
    (x, g_mix, w_in, b_in, w_dw, b_dw, ln_g, ln_b, w_pool, s_pool, w_out, g_ffn, w_gate, w_up, w_down, g_final,
     loss_target) = args[:17]
    names = ["g_mix", "w_in", "b_in", "w_dw", "b_dw", "ln_g", "ln_b", "w_pool", "s_pool", "w_out", "g_ffn",
             "w_gate", "w_up", "w_down", "g_final"]
    weights = dict(zip(names, args[1:16]))
    moms = dict(zip(names, args[17:32]))
    vars_ = dict(zip(names, args[32:47]))

    S, D = x.shape[1], x.shape[2]
    x2 = x.reshape(S, D)
    tgt2 = loss_target.reshape(S, D)

    shard = lambda nm: _rows(nm, weights[nm])
    w_pool_b = w_pool[0].astype(BF16)
    dw_shard = jnp.pad(w_dw[0], ((0, SUBLANES * 4 - CONV_WIDTH), (0, 128 - w_dw.shape[2])))

    a, gate, v, m, y, h1, xn, w_dw_f, g_in, g_out, _, g_gate, g_up, g_down = _mix_fwd(
        x2, g_mix, b_in, b_dw, ln_g, ln_b, w_pool_b, s_pool, [shard("w_in"), shard("w_out"), dw_shard],
        [shard("w_gate"), shard("w_up"), shard("w_down")], ts_mix_fwd)
    wt_in, w_out_f = g_in.reshape(-1, D), g_out.reshape(-1, D)
    wt_gate, wt_up, w_down_f = g_gate.reshape(-1, D), g_up.reshape(-1, D), g_down.reshape(-1, D)
    Fd = wt_gate.shape[0]
    f_chunks = [1024] * (Fd // 1024) + ([Fd % 1024] if Fd % 1024 else [])
    dh1, hn, act, dgt, dup, dh2, loss_p, dg_final, dg_ffn = _ffn(
        h1, tgt2, g_ffn, g_final.reshape(1, D), wt_gate, wt_up, w_down_f, ts_ffn, f_chunks)

    by_shard = lambda g: g.reshape(N_DEV, -1, D)
    p_gate = by_shard(_grad_matmul(dgt, hn, tm_grad, "grad_w_gate"))
    g_up, s_gate = _grad_matmul_pair_exchange(dup, hn, tm_grad, "grad_w_up", [p_gate])
    p_up = by_shard(g_up)
    g_down, s_up = _grad_matmul_pair_exchange(act, dh2, tm_grad, "grad_w_down", [p_up])
    p_down = by_shard(g_down)
    s_down, = _pair_exchange([p_down])
    pair_sums = _pair_sum([p_gate, p_up, p_down], [s_gate, s_up, s_down])
    (dx, dz, dh1b, dg_mix, db_in, dw_dw, db_dw, dln_g, dln_b, dw_pool, ds_pool, r_gate, r_up, r_down) = _mix_bwd(
        dh1, x2, a, gate, v, m, g_mix, wt_in, w_dw_f, ln_g, ln_b, w_pool_b, s_pool, w_out_f, pair_sums, ts_mix_bwd)

    small_names = ["g_mix", "b_in", "b_dw", "ln_g", "ln_b", "w_pool", "s_pool", "g_ffn", "g_final"]
    small_shape = lambda p: p.reshape(-1, p.shape[-1])
    partial = [dg_mix, db_in, db_dw, dln_g, dln_b, dw_pool.reshape(-1, POOL_GROUP), ds_pool, dg_ffn, dg_final, loss_p]
    dw_out, *summed = _grad_matmul(y, dh1b, tm_grad, "grad_w_out", reduce_vals=partial)
    dwt_in, r_out, r_dw = _grad_matmul(
        dz, xn, tm_grad, "grad_w_in",
        reduce_parts=[dw_out.reshape(N_DEV, -1, D), _shard_cols(dw_dw[0:CONV_WIDTH])])
    r_in, = _tail_reduce([dwt_in.reshape(N_DEV, -1, D)], [])

    big = {}
    for nm, r in zip(_BIG, [r_in, r_out, r_gate, r_up, r_down]):
        rows = r.shape[1]
        res = _adamw_shard(r, _rows(nm, weights[nm]), _rows(nm, moms[nm]), _rows(nm, vars_[nm]),
                           rows // 2 if rows % 32 == 0 else rows, "adamw_" + nm)
        big[nm] = [o.T if nm in _COL_SHARDED else o for o in res]
    big["w_dw"] = _adamw_shard(r_dw, w_dw[0], moms["w_dw"][0], vars_["w_dw"][0], CONV_WIDTH, "adamw_w_dw")

    sm = _adamw_small(summed[:-1], [small_shape(weights[nm]) for nm in small_names],
                      [small_shape(moms[nm]) for nm in small_names], [small_shape(vars_[nm]) for nm in small_names])
    n_small = len(small_names)

    def result(kind, nm):
        if nm in big:
            return big[nm][kind].reshape(weights[nm].shape)
        return sm[kind * n_small + small_names.index(nm)].reshape(weights[nm].shape)

    loss = summed[-1][0, 0]
    out = [loss, dx.reshape(x.shape)]
    for kind in range(4):
        out += [result(kind, nm) for nm in names]
    return tuple(out)


def kernel(x, g_mix, w_in, b_in, w_dw, b_dw, ln_g, ln_b, w_pool, s_pool, w_out, g_ffn, w_gate, w_up, w_down, g_final, loss_target, m_g_mix, m_w_in, m_b_in, m_w_dw, m_b_dw, m_ln_g, m_ln_b, m_w_pool, m_s_pool, m_w_out, m_g_ffn, m_w_gate, m_w_up, m_w_down, m_g_final, v_g_mix, v_w_in, v_b_in, v_w_dw, v_b_dw, v_ln_g, v_ln_b, v_w_pool, v_s_pool, v_w_out, v_g_ffn, v_w_gate, v_w_up, v_w_down, v_g_final):
    args = (x, g_mix, w_in, b_in, w_dw, b_dw, ln_g, ln_b, w_pool, s_pool, w_out, g_ffn, w_gate, w_up, w_down, g_final, loss_target, m_g_mix, m_w_in, m_b_in, m_w_dw, m_b_dw, m_ln_g, m_ln_b, m_w_pool, m_s_pool, m_w_out, m_g_ffn, m_w_gate, m_w_up, m_w_down, m_g_final, v_g_mix, v_w_in, v_b_in, v_w_dw, v_b_dw, v_ln_g, v_ln_b, v_w_pool, v_s_pool, v_w_out, v_g_ffn, v_w_gate, v_w_up, v_w_down, v_g_final)
    return _step(args, ts_mix_fwd=512, ts_ffn=256, ts_mix_bwd=512, tm_grad=256)
```

```python
import jax
import jax.numpy as jnp
from jax import lax
from jax.experimental import pallas as pl
from jax.experimental.pallas import tpu as pltpu

F32 = jnp.float32
BF16 = jnp.bfloat16
MESH = pl.DeviceIdType.MESH
N_DEV = 8

C_CONV = 512
CONV_WIDTH = 31
POOL_WINDOWS = (2, 4, 8, 16)
POOL_GROUP = 128
RMS_EPS = 1e-6
LN_EPS = 1e-5

ADAM_LR = 0.001
ADAM_B1 = 0.9
ADAM_B2 = 0.999
ADAM_EPS = 1e-08
ADAM_WD = 0.01
ADAM_STEP = 10

HALO = 32
SUBLANES = 8
CONV_ROWS = 64
VMEM_LIMIT = 56 * 1024 * 1024


def _dot(a, b):
    return jnp.dot(a, b, preferred_element_type=F32)


def _dot_nt(a, b):
    return lax.dot_general(a, b, (((1,), (1,)), ((), ())), preferred_element_type=F32)


def _dot_tn(a, b):
    return lax.dot_general(a, b, (((0,), (0,)), ((), ())), preferred_element_type=F32)


def _mean_last(v):
    return jnp.mean(v, axis=-1, keepdims=True)


def _full(shape):
    nd = len(shape)
    return pl.BlockSpec(shape, lambda *_: (0,) * nd)


def _full1(shape):
    nd = len(shape)
    return pl.BlockSpec(shape, lambda *_: (0,) * nd, pipeline_mode=pl.Buffered(1))


def _shifted_copies(sh_ref, rows):
    for s in range(1, SUBLANES):
        sh_ref[s, 0:rows, :] = sh_ref[0, s:s + rows, :]


def _tap(sh_ref, off, r0, rows):
    q, s = divmod(off, SUBLANES)
    return sh_ref[s, pl.ds(r0 + q * SUBLANES, rows), :]


def _mix_fwd(x, g_mix, b_in, b_dw, ln_g, ln_b, w_pool, s_pool, mix_shards, ag_shards, ts):
    S, D = x.shape
    d_in = mix_shards[0].shape[0] * N_DEV
    C = C_CONV
    nt = S // ts
    nrb = ts // CONV_ROWS
    n_ag = len(ag_shards)
    relay_step = nt // 2
    mix_dtypes = [BF16, BF16, F32]

    def body(x_ref, g_ref, bin_ref, bdw_ref, lng_ref, lnb_ref, wp_ref, sp_ref, *rest):
        mx_in, ag_in, rest = rest[:3], rest[3:3 + n_ag], rest[3 + n_ag:]
        a_ref, gate_ref, v_ref, m_ref, y_ref, h1_ref, xn_ref, wdw_ref = rest[:8]
        mx_out, ag_out, rest = rest[8:11], rest[11:11 + n_ag], rest[11 + n_ag:]
        ush, pbuf, win_ref, wout_ref, gdw, load_sems = rest[:6]
        rest = rest[6:]
        mx_stage, mx_bufs, mx_sems, rest = rest[:3], rest[3:6], rest[6:9], rest[9:]
        ag_stage, ag_bufs, ag_sems = rest[:n_ag], rest[n_ag:2 * n_ag], rest[2 * n_ag:]
        i = pl.program_id(0)
        mx_start, mx_relay, mx_pass_on, mx_finish = _gather_plan(mx_in, mx_out, mx_stage, mx_bufs, *mx_sems)
        ag_start, ag_relay, ag_pass_on, ag_finish = _gather_plan(ag_in, ag_out, ag_stage, ag_bufs, *ag_sems)

        @pl.when(i == 0)
        def _():
            mx_start()
            mx_relay()
            ag_start()
            mx_pass_on()
            mx_finish()
            loads = [pltpu.make_async_copy(mx_out[2], gdw, load_sems.at[2 * N_DEV])]
            for j in range(N_DEV):
                r_in, r_out = mx_out[0].shape[1], mx_out[1].shape[1]
                loads += [pltpu.make_async_copy(mx_out[0].at[j], win_ref.at[pl.ds(j * r_in, r_in), :],
                                                load_sems.at[2 * j]),
                          pltpu.make_async_copy(mx_out[1].at[j], wout_ref.at[pl.ds(j * r_out, r_out), :],
                                                load_sems.at[2 * j + 1])]
            for cp in loads:
                cp.start()
            for cp in loads:
                cp.wait()
            first_half = lax.broadcasted_iota(jnp.int32, gdw.shape[1:], 1) < C // N_DEV
            for p in range(N_DEV // 2):
                wdw_ref[:, 128 * p:128 * (p + 1)] = jnp.where(
                    first_half, gdw[2 * p], pltpu.roll(gdw[2 * p + 1], C // N_DEV, axis=1))
            ush[0, 0:HALO, :] = jnp.zeros((HALO, C), F32)
            pbuf[0:HALO, :] = jnp.zeros((HALO, C), F32)

        @pl.when(i == relay_step)
        def _():
            ag_relay()

        @pl.when(i == nt - 1)
        def _():
            ag_pass_on()

        x = x_ref[...]
        r1 = lax.rsqrt(_mean_last(x * x) + RMS_EPS)
        xn = (x * r1 * g_ref[...]).astype(BF16)
        xn_ref[...] = xn
        z = _dot_nt(xn, win_ref[...]) + bin_ref[...]
        a = z[:, 0:C]
        gate = z[:, C:2 * C]
        a_ref[...] = a
        gate_ref[...] = gate
        ush[0, HALO:HALO + ts, :] = a * jax.nn.sigmoid(gate)
        pbuf[HALO:HALO + ts, :] = z[:, 2 * C:]

        _shifted_copies(ush, ts + HALO - SUBLANES)

        def conv_block(rb, carry):
            r0 = pl.multiple_of(rb * CONV_ROWS, CONV_ROWS)
            acc = jnp.zeros((CONV_ROWS, C), F32)
            for k in range(CONV_WIDTH):
                acc = acc + wdw_ref[k:k + 1, :] * _tap(ush, HALO - (CONV_WIDTH - 1) + k, r0, CONV_ROWS)
            v_ref[pl.ds(r0, CONV_ROWS), :] = acc + bdw_ref[...]
            return carry

        lax.fori_loop(0, nrb, conv_block, 0)

        v = v_ref[...]
        mu = _mean_last(v)
        xc = v - mu
        rstd = lax.rsqrt(_mean_last(xc * xc) + LN_EPS)
        ln = xc * rstd * lng_ref[...] + lnb_ref[...]
        y_ref[:, 0:C] = (ln * jax.nn.sigmoid(ln)).astype(BF16)

        row = lax.broadcasted_iota(jnp.int32, (ts, 1), 0) + i * ts
        for gi, w in enumerate(POOL_WINDOWS):
            lanes = slice(gi * POOL_GROUP, (gi + 1) * POOL_GROUP)
            seg = pbuf[HALO:HALO + ts, lanes]
            ws = seg
            for k in range(1, w):
                ws = ws + pbuf[HALO - k:HALO - k + ts, lanes]
            cnt = jnp.minimum(row + 1, w).astype(F32)
            m = (ws / cnt - seg).astype(BF16)
            m_ref[:, lanes] = m
            ypre = _dot(m, wp_ref[gi])
            y_ref[:, C + gi * POOL_GROUP:C + (gi + 1) * POOL_GROUP] = (ypre * sp_ref[:, lanes]).astype(BF16)

        h1_ref[...] = x + _dot(y_ref[...], wout_ref[...])

        ush[0, 0:HALO, :] = ush[0, ts:ts + HALO, :]
        pbuf[0:HALO, :] = pbuf[ts:ts + HALO, :]

        @pl.when(i == nt - 1)
        def _():
            ag_finish()

    tile = lambda w, dt: (pl.BlockSpec((ts, w), lambda i: (i, 0)), jax.ShapeDtypeStruct((S, w), dt))
    wdw_rows = mix_shards[2].shape[0]
    outs = [tile(C, F32), tile(C, F32), tile(C, F32), tile(C, BF16), tile(D, BF16), tile(D, F32), tile(D, BF16),
            (_full((wdw_rows, C)), jax.ShapeDtypeStruct((wdw_rows, C), F32))]
    return pl.pallas_call(
        body, name="mix_fwd", grid=(nt,),
        in_specs=[pl.BlockSpec((ts, D), lambda i: (i, 0)), _full((1, D)), _full((1, d_in)), _full((1, C)),
                  _full((1, C)), _full((1, C)), _full(w_pool.shape), _full((1, C))] + [_HBM] * (3 + n_ag),
        out_specs=[o[0] for o in outs] + [_HBM] * (3 + n_ag),
        out_shape=[o[1] for o in outs] + _gather_out_shapes(mix_shards, mix_dtypes)
        + _gather_out_shapes(ag_shards, [BF16] * n_ag),
        scratch_shapes=[pltpu.VMEM((SUBLANES, ts + HALO, C), F32), pltpu.VMEM((ts + HALO, C), F32),
                        pltpu.VMEM((d_in, D), BF16), pltpu.VMEM((D, D), BF16),
                        pltpu.VMEM((N_DEV,) + mix_shards[2].shape, F32), pltpu.SemaphoreType.DMA((2 * N_DEV + 1,))]
        + _gather_scratch(mix_shards, mix_dtypes) + _gather_scratch(ag_shards, [BF16] * n_ag),
        compiler_params=pltpu.CompilerParams(dimension_semantics=("arbitrary",), vmem_limit_bytes=VMEM_LIMIT),
    )(x, g_mix, b_in, b_dw, ln_g, ln_b, w_pool, s_pool, *mix_shards, *ag_shards)


def _ffn(h1, target, g_ffn, g_final, w_gate, w_up, w_down, ts, f_chunks):
    S, D = h1.shape
    Fd = w_gate.shape[0]
    nt = S // ts
    bounds = []
    lo = 0
    for n in f_chunks:
        bounds.append((lo, lo + n))
        lo += n
    assert lo == Fd

    def body(h1_ref, tgt_ref, gf_ref, gl_ref, wg_ref, wu_ref, wd_ref,
             dh1_ref, hn_ref, act_ref, dgt_ref, dup_ref, dh2_ref, loss_ref, dgl_ref, dgf_ref, gt_s, up_s):
        i = pl.program_id(0)

        @pl.when(i == 0)
        def _():
            loss_ref[...] = jnp.zeros_like(loss_ref)
            dgl_ref[...] = jnp.zeros_like(dgl_ref)
            dgf_ref[...] = jnp.zeros_like(dgf_ref)

        h1 = h1_ref[...]
        r2 = lax.rsqrt(_mean_last(h1 * h1) + RMS_EPS)
        hhat = h1 * r2
        hn = (hhat * gf_ref[...]).astype(BF16)
        hn_ref[...] = hn
        h2 = h1
        for lo, hi in bounds:
            gt = _dot_nt(hn, wg_ref[lo:hi, :])
            up = _dot_nt(hn, wu_ref[lo:hi, :])
            gt_s[:, lo:hi] = gt
            up_s[:, lo:hi] = up
            act = (gt * jax.nn.sigmoid(gt) * up).astype(BF16)
            act_ref[:, lo:hi] = act
            h2 = h2 + _dot(act, wd_ref[lo:hi, :])

        r3 = lax.rsqrt(_mean_last(h2 * h2) + RMS_EPS)
        n3 = h2 * r3
        gl = gl_ref[...]
        diff = n3 * gl - tgt_ref[...]
        loss_ref[...] += jnp.sum(0.5 * jnp.sum(diff * diff, axis=-1, keepdims=True) / D, axis=0, keepdims=True)
        dout = diff / D
        dgl_ref[...] += jnp.sum(dout * n3, axis=0, keepdims=True)
        dn = dout * gl
        dh2 = r3 * (dn - n3 * _mean_last(dn * n3))
        dh2b = dh2.astype(BF16)
        dh2_ref[...] = dh2b

        dhn = jnp.zeros((ts, D), F32)
        for lo, hi in bounds:
            gt = gt_s[:, lo:hi]
            up = up_s[:, lo:hi]
            sg = jax.nn.sigmoid(gt)
            dact = _dot_nt(dh2b, wd_ref[lo:hi, :])
            dgt = (dact * up * (sg * (1.0 + gt * (1.0 - sg)))).astype(BF16)
            dup = (dact * (gt * sg)).astype(BF16)
            dgt_ref[:, lo:hi] = dgt
            dup_ref[:, lo:hi] = dup
            dhn = dhn + _dot(dgt, wg_ref[lo:hi, :]) + _dot(dup, wu_ref[lo:hi, :])

        dgf_ref[...] += jnp.sum(dhn * hhat, axis=0, keepdims=True)
        dnn = dhn * gf_ref[...]
        dh1_ref[...] = dh2 + r2 * (dnn - hhat * _mean_last(dnn * hhat))

    tile = lambda w, dt: (pl.BlockSpec((ts, w), lambda i: (i, 0)), jax.ShapeDtypeStruct((S, w), dt))
    acc = lambda w: (_full((1, w)), jax.ShapeDtypeStruct((1, w), F32))
    outs = [tile(D, F32), tile(D, BF16), tile(Fd, BF16), tile(Fd, BF16), tile(Fd, BF16), tile(D, BF16),
            acc(128), acc(D), acc(D)]
    return pl.pallas_call(
        body, name="ffn_fwd_bwd", grid=(nt,),
        in_specs=[pl.BlockSpec((ts, D), lambda i: (i, 0)), pl.BlockSpec((ts, D), lambda i: (i, 0)),
                  _full((1, D)), _full((1, D)), _full1((Fd, D)), _full1((Fd, D)), _full1((Fd, D))],
        out_specs=[o[0] for o in outs], out_shape=[o[1] for o in outs],
        scratch_shapes=[pltpu.VMEM((ts, Fd), F32), pltpu.VMEM((ts, Fd), F32)],
        compiler_params=pltpu.CompilerParams(dimension_semantics=("arbitrary",), vmem_limit_bytes=VMEM_LIMIT),
    )(h1, target, g_ffn, g_final, w_gate, w_up, w_down)


def _mix_bwd(dh1, x, a, gate, v, m, g_mix, w_in, w_dw, ln_g, ln_b, w_pool, s_pool, w_out, rs_parts, ts):
    S, D = x.shape
    n_rs = len(rs_parts)
    d_in = w_in.shape[0]
    C = C_CONV
    nt = S // ts
    nrb = ts // CONV_ROWS
    wrows =((CONV_WIDTH + SUBLANES - 1) // SUBLANES) * SUBLANES

    def body(dh1_ref, x_ref, a_ref, gate_ref, v_ref, m_ref, g_ref, win_ref, wdw_ref, lng_ref,
             lnb_ref, wp_ref, sp_ref, wout_ref, *rest):
        rs_in, rest = rest[:n_rs], rest[n_rs:]
        (dx_ref, dz_ref, dh1b_ref, dgm_ref, dbin_ref, dwdw_ref, dbdw_ref, dlng_ref, dlnb_ref, dwp_ref,
         dsp_ref) = rest[:11]
        rs_out, rest = rest[11:11 + n_rs], rest[11 + n_rs:]
        dvsh, dqbuf, du_s = rest[:3]
        rs_bufs, (send_sems, recv_sems, local_sems) = rest[3:3 + n_rs], rest[3 + n_rs:]
        i = pl.program_id(0)
        t = nt - 1 - i
        rs_start, rs_finish = _chip_exchange_plan(rs_in, rs_out, rs_bufs, send_sems, recv_sems, local_sems)

        @pl.when(i == 0)
        def _():
            rs_start()
            dvsh[0, ts:ts + HALO, :] = jnp.zeros((HALO, C), F32)
            dqbuf[ts:ts + HALO, :] = jnp.zeros((HALO, C), F32)
            for r in (dgm_ref, dbin_ref, dwdw_ref, dbdw_ref, dlng_ref, dlnb_ref, dwp_ref, dsp_ref):
                r[...] = jnp.zeros_like(r)

        dh1 = dh1_ref[...]
        dh1b = dh1.astype(BF16)
        dh1b_ref[...] = dh1b
        dy = _dot_nt(dh1b, wout_ref[...])

        v = v_ref[...]
        mu = _mean_last(v)
        xc = v - mu
        rstd = lax.rsqrt(_mean_last(xc * xc) + LN_EPS)
        vhat = xc * rstd
        lng = lng_ref[...]
        ln = vhat * lng + lnb_ref[...]
        sg = jax.nn.sigmoid(ln)
        dln = dy[:, 0:C] * (sg * (1.0 + ln * (1.0 - sg)))
        dlng_ref[...] += jnp.sum(dln * vhat, axis=0, keepdims=True)
        dlnb_ref[...] += jnp.sum(dln, axis=0, keepdims=True)
        dvh = dln * lng
        dv = rstd * (dvh - _mean_last(dvh) - vhat * _mean_last(dvh * vhat))
        dbdw_ref[...] += jnp.sum(dv, axis=0, keepdims=True)
        dvsh[0, 0:ts, :] = dv
        _shifted_copies(dvsh, ts + HALO - SUBLANES)

        def conv_block(rb, carry):
            r0 = pl.multiple_of(rb * CONV_ROWS, CONV_ROWS)
            acc = jnp.zeros((CONV_ROWS, C), F32)
            for k in range(CONV_WIDTH):
                acc = acc + wdw_ref[k:k + 1, :] * _tap(dvsh, CONV_WIDTH - 1 - k, r0, CONV_ROWS)
            du_s[pl.ds(r0, CONV_ROWS), :] = acc
            return carry

        lax.fori_loop(0, nrb, conv_block, 0)

        a = a_ref[...]
        sgate = jax.nn.sigmoid(gate_ref[...])
        u = a * sgate
        for k in range(CONV_WIDTH):
            q, s = divmod(CONV_WIDTH - 1 - k, SUBLANES)
            prod = u * dvsh[s, q * SUBLANES:q * SUBLANES + ts, :]
            dwdw_ref[k:k + 1, :] += jnp.sum(prod, axis=0, keepdims=True)

        du = du_s[...]
        da = du * sgate
        dgate = du * a * sgate * (1.0 - sgate)
        dz_ref[:, 0:C] = da.astype(BF16)
        dz_ref[:, C:2 * C] = dgate.astype(BF16)
        dbin_ref[:, 0:C] += jnp.sum(da, axis=0, keepdims=True)
        dbin_ref[:, C:2 * C] += jnp.sum(dgate, axis=0, keepdims=True)

        row = lax.broadcasted_iota(jnp.int32, (ts, 1), 0) + t * ts
        for gi, w in enumerate(POOL_WINDOWS):
            lanes = slice(gi * POOL_GROUP, (gi + 1) * POOL_GROUP)
            dyp = dy[:, C + gi * POOL_GROUP:C + (gi + 1) * POOL_GROUP]
            mg = m_ref[:, lanes]
            ypre = _dot(mg, wp_ref[gi])
            dsp_ref[:, lanes] += jnp.sum(dyp * ypre, axis=0, keepdims=True)
            dyi = (dyp * sp_ref[:, lanes]).astype(BF16)
            dwp_ref[gi] += _dot_tn(mg, dyi)
            dm = _dot_nt(dyi, wp_ref[gi])
            cnt = jnp.minimum(row + 1, w).astype(F32)
            dqbuf[0:ts, lanes] = dm / cnt
            dp = -dm
            for k in range(w):
                dp = dp + dqbuf[k:k + ts, lanes]
            dz_ref[:, 2 * C + gi * POOL_GROUP:2 * C + (gi + 1) * POOL_GROUP] = dp.astype(BF16)
            dbin_ref[:, 2 * C + gi * POOL_GROUP:2 * C + (gi + 1) * POOL_GROUP] += jnp.sum(dp, axis=0, keepdims=True)

        dxn = _dot(dz_ref[...], win_ref[...])
        x = x_ref[...]
        r1 = lax.rsqrt(_mean_last(x * x) + RMS_EPS)
        xhat = x * r1
        dgm_ref[...] += jnp.sum(dxn * xhat, axis=0, keepdims=True)
        dnn = dxn * g_ref[...]
        dx_ref[...] = dh1 + r1 * (dnn - xhat * _mean_last(dnn * xhat))

        dvsh[0, ts:ts + HALO, :] = dvsh[0, 0:HALO, :]
        dqbuf[ts:ts + HALO, :] = dqbuf[0:HALO, :]

        @pl.when(i == nt - 1)
        def _():
            rs_finish()

    rev = lambda w: pl.BlockSpec((ts, w), lambda i: (nt - 1 - i, 0))
    acc = lambda shape: (_full(shape), jax.ShapeDtypeStruct(shape, F32))
    outs = [(rev(D), jax.ShapeDtypeStruct((S, D), F32)), (rev(d_in), jax.ShapeDtypeStruct((S, d_in), BF16)),
            (rev(D), jax.ShapeDtypeStruct((S, D), BF16)), acc((1, D)), acc((1, d_in)), acc((wrows, C)),
            acc((1, C)), acc((1, C)), acc((1, C)), acc(w_pool.shape), acc((1, C))]
    return pl.pallas_call(
        body, name="mix_bwd", grid=(nt,),
        in_specs=[rev(D), rev(D), rev(C), rev(C), rev(C), rev(C), _full((1, D)), _full((d_in, D)),
                  _full(w_dw.shape), _full((1, C)), _full((1, C)), _full(w_pool.shape), _full((1, C)),
                  _full((D, D))] + [_HBM] * n_rs,
        out_specs=[o[0] for o in outs] + [_HBM] * n_rs,
        out_shape=[o[1] for o in outs] + [jax.ShapeDtypeStruct(p.shape, p.dtype) for p in rs_parts],
        scratch_shapes=[pltpu.VMEM((SUBLANES, ts + HALO, C), F32), pltpu.VMEM((ts + HALO, C), F32),
                        pltpu.VMEM((ts, C), F32)]
        + [pltpu.VMEM(p.shape[1:], p.dtype) for p in rs_parts] + _comm_sems(n_rs, 3),
        compiler_params=pltpu.CompilerParams(dimension_semantics=("arbitrary",), vmem_limit_bytes=VMEM_LIMIT),
    )(dh1, x, a, gate, v, m, g_mix, w_in, w_dw, ln_g, ln_b, w_pool, s_pool, w_out, *rs_parts)


def _grad_matmul(a, b, tm, name, reduce_parts=(), reduce_vals=()):
    S, M = a.shape
    N = b.shape[1]
    steps = M // tm
    n, ns = len(reduce_parts), len(reduce_vals)

    def body(a_ref, b_ref, *rest):
        o_ref = rest[n + ns]
        if n + ns:
            start, middle, finish = _reduce_plan(rest[:n], rest[n:n + ns], rest[n + ns + 1:2 * n + ns + 1],
                                                 rest[2 * n + ns + 1:2 * (n + ns) + 1], rest[2 * (n + ns) + 1:])
            i = pl.program_id(0)
            pl.when(i == 0)(start)
            pl.when(i == 1)(middle)
        o_ref[...] = _dot_tn(a_ref[...], b_ref[...]).astype(BF16)
        if n + ns:
            pl.when(i == steps - 1)(finish)

    res = pl.pallas_call(
        body, name=name, grid=(steps,),
        in_specs=[pl.BlockSpec((S, tm), lambda i: (0, i)), _full1((S, N))] + [_VMEM] * (n + ns),
        out_specs=[pl.BlockSpec((tm, N), lambda i: (i, 0))] + [_VMEM] * (n + ns),
        out_shape=[jax.ShapeDtypeStruct((M, N), BF16)] + _reduce_out_shapes(reduce_parts, reduce_vals),
        scratch_shapes=_reduce_scratch(reduce_parts, reduce_vals) if n + ns else [],
        compiler_params=pltpu.CompilerParams(dimension_semantics=("arbitrary",) if n + ns else ("parallel",),
                                             vmem_limit_bytes=VMEM_LIMIT),
    )(a, b, *reduce_parts, *reduce_vals)
    return res if n + ns else res[0]


def _grad_matmul_pair_exchange(a, b, tm, name, parts):
    S, M = a.shape
    N = b.shape[1]
    steps = M // tm
    n = len(parts)

    def body(a_ref, b_ref, *rest):
        o_ref = rest[n]
        start, finish = _pair_exchange_plan(rest[:n], rest[n + 1:2 * n + 1], *rest[2 * n + 1:])
        i = pl.program_id(0)
        pl.when(i == 0)(start)
        o_ref[...] = _dot_tn(a_ref[...], b_ref[...]).astype(BF16)
        pl.when(i == steps - 1)(finish)

    sems = pltpu.SemaphoreType.DMA((n, N_CHIPS))
    return pl.pallas_call(
        body, name=name, grid=(steps,),
        in_specs=[pl.BlockSpec((S, tm), lambda i: (0, i)), _full1((S, N))] + [_HBM] * n,
        out_specs=[pl.BlockSpec((tm, N), lambda i: (i, 0))] + [_HBM] * n,
        out_shape=[jax.ShapeDtypeStruct((M, N), BF16)] + _pair_out_shapes(parts),
        scratch_shapes=[sems, sems],
        compiler_params=pltpu.CompilerParams(dimension_semantics=("arbitrary",), vmem_limit_bytes=VMEM_LIMIT),
    )(a, b, *parts)


def _position():
    return lax.axis_index("x"), lax.axis_index("y"), lax.axis_index("c")


def _slot(px, py, pc):
    return 4 * px + 2 * py + pc


_HBM = pl.BlockSpec(memory_space=pl.ANY)


def _comm_sems(n, copies):
    return [pltpu.SemaphoreType.DMA((n, copies)), pltpu.SemaphoreType.DMA((n, copies)),
            pltpu.SemaphoreType.DMA((n, 2))]


def _local_copy(srcs, dsts, bufs, local_sems):
    n = len(srcs)
    loads = [pltpu.make_async_copy(srcs[k], bufs[k], local_sems.at[k, 0]) for k in range(n)]
    for cp in loads:
        cp.start()
    for cp in loads:
        cp.wait()
    stores = _local_stores(dsts, bufs, local_sems)
    for cp in stores:
        cp.start()
    return stores


def _local_stores(dsts, bufs, local_sems):
    return [pltpu.make_async_copy(bufs[k], dsts[k], local_sems.at[k, 1]) for k in range(len(dsts))]


def _gather_out_shapes(shards, dtypes):
    return [jax.ShapeDtypeStruct((N_DEV,) + s.shape, dt) for s, dt in zip(shards, dtypes)]


def _gather_scratch(shards, dtypes):
    return ([pltpu.VMEM(s.shape, s.dtype) for s in shards] + [pltpu.VMEM(s.shape, dt) for s, dt in zip(shards, dtypes)]
            + _comm_sems(len(shards), 7))


def _gather_plan(ins, outs, stage, bufs, send_sems, recv_sems, local_sems):
    n = len(ins)
    x, y, c = _position()
    me, sibling = (x, y, c), (x, y, 1 - c)
    na, nb, dg = (x ^ (1 - c), y ^ c), (x ^ c, y ^ (1 - c)), (1 - x, 1 - y)
    own = [outs[k].at[_slot(*me)] for k in range(n)]

    def copy(k, sem, block, to, src=None):
        dst = outs[k].at[_slot(*block)]
        return pltpu.make_async_remote_copy(
            src_ref=dst if src is None else src, dst_ref=dst, send_sem=send_sems.at[k, sem],
            recv_sem=recv_sems.at[k, sem], device_id=to, device_id_type=MESH)

    def first():
        cps = []
        for k in range(n):
            cps += [copy(k, 0, me, sibling, src=bufs[k]), copy(k, 1, me, (*na, c), src=bufs[k]),
                    copy(k, 2, me, (*nb, c), src=bufs[k])]
        return cps

    def onward():
        return [copy(k, 3, (*na, c), (*nb, c)) for k in range(n)]

    def to_sibling(j, chip):
        return [copy(k, 4 + j, (*chip, c), sibling) for k in range(n)]

    def start():
        loads = [pltpu.make_async_copy(ins[k], stage[k], local_sems.at[k, 0]) for k in range(n)]
        for cp in loads:
            cp.start()
        for cp in loads:
            cp.wait()
        for k in range(n):
            bufs[k][...] = stage[k][...].astype(bufs[k].dtype)
        for cp in first() + _local_stores(own, bufs, local_sems):
            cp.start()

    def relay():
        for k in range(n):
            copy(k, 1, (*na, c), me).wait_recv()
        for cp in onward() + to_sibling(0, na):
            cp.start()
        for k in range(n):
            copy(k, 2, (*nb, c), me).wait_recv()
        for cp in to_sibling(1, nb):
            cp.start()

    def pass_on():
        for k in range(n):
            copy(k, 3, (*dg, c), me).wait_recv()
        for cp in to_sibling(2, dg):
            cp.start()

    def finish():
        for k in range(n):
            copy(k, 0, sibling, me).wait_recv()
            for j, chip in enumerate((nb, na, dg)):
                copy(k, 4 + j, (*chip, 1 - c), me).wait_recv()
        for cp in first() + onward() + to_sibling(0, na) + to_sibling(1, nb) + to_sibling(2, dg):
            cp.wait_send()
        for cp in _local_stores(own, bufs, local_sems):
            cp.wait()

    return start, relay, pass_on, finish


N_CHIPS = 4
_CHIP_FLIPS = [(1, 0), (0, 1), (1, 1)]
F32_TRAVEL_LIMIT = 4096


def _chip_exchange_plan(ins, outs, bufs, send_sems, recv_sems, local_sems):
    n = len(ins)
    x, y, c = _position()
    my_q = 2 * x + y
    peers = [(x ^ fx, y ^ fy) for fx, fy in _CHIP_FLIPS]
    own = [outs[k].at[my_q] for k in range(n)]

    def sends():
        return [pltpu.make_async_remote_copy(
            src_ref=ins[k].at[2 * px + py], dst_ref=outs[k].at[my_q], send_sem=send_sems.at[k, f],
            recv_sem=recv_sems.at[k, f], device_id=(px, py, c), device_id_type=MESH)
            for f, (px, py) in enumerate(peers) for k in range(n)]

    def start():
        for cp in sends():
            cp.start()
        _local_copy([ins[k].at[my_q] for k in range(n)], own, bufs, local_sems)

    def finish():
        for f, (px, py) in enumerate(peers):
            for k in range(n):
                pltpu.make_async_remote_copy(
                    src_ref=ins[k].at[my_q], dst_ref=outs[k].at[2 * px + py], send_sem=send_sems.at[k, f],
                    recv_sem=recv_sems.at[k, f], device_id=(px, py, c), device_id_type=MESH).wait_recv()
        for cp in sends():
            cp.wait_send()
        for cp in _local_stores(own, bufs, local_sems):
            cp.wait()

    return start, finish


def _pair_exchange(parts):
    n = len(parts)

    def body(*refs):
        start, finish = _pair_exchange_plan(refs[:n], refs[n:2 * n], *refs[2 * n:])
        start()
        finish()

    sems = pltpu.SemaphoreType.DMA((n, N_CHIPS))
    return pl.pallas_call(
        body, name="pair_exchange", out_shape=_pair_out_shapes(parts),
        in_specs=[_HBM] * n, out_specs=[_HBM] * n, scratch_shapes=[sems, sems],
    )(*parts)


def _pair_out_shapes(parts):
    return [jax.ShapeDtypeStruct((N_CHIPS,) + p.shape[1:], p.dtype) for p in parts]


def _pair_exchange_plan(ins, sib, send_sems, recv_sems):
    x, y, c = _position()

    def copies():
        return [pltpu.make_async_remote_copy(
            src_ref=ins[k].at[2 * q + 1 - c], dst_ref=sib[k].at[q], send_sem=send_sems.at[k, q],
            recv_sem=recv_sems.at[k, q], device_id=(x, y, 1 - c), device_id_type=MESH)
            for k in range(len(ins)) for q in range(N_CHIPS)]

    def start():
        for cp in copies():
            cp.start()

    def finish():
        for cp in copies():
            cp.wait()

    return start, finish


def _pair_sum(parts, sib):
    n = len(parts)

    def body(*refs):
        c = lax.axis_index("c")
        for k in range(n):
            refs[2 * n + k][0] = (refs[k][0, c].astype(F32) + refs[n + k][0].astype(F32)).astype(BF16)

    pair = [pl.BlockSpec((1, 2) + p.shape[1:], lambda q: (q, 0, 0, 0)) for p in parts]
    one = [pl.BlockSpec((1,) + p.shape[1:], lambda q: (q, 0, 0)) for p in parts]
    return pl.pallas_call(
        body, name="pair_sum", grid=(N_CHIPS,), in_specs=pair + one, out_specs=one,
        out_shape=[jax.ShapeDtypeStruct(s.shape, BF16) for s in sib],
        compiler_params=pltpu.CompilerParams(dimension_semantics=("parallel",)),
    )(*[p.reshape((N_CHIPS, 2) + p.shape[1:]) for p in parts], *sib)


def _tail_reduce(parts, vals):
    n, ns = len(parts), len(vals)

    def body(*refs):
        start, middle, finish = _reduce_plan(refs[:n], refs[n:n + ns], refs[n + ns:2 * n + ns],
                                             refs[2 * n + ns:2 * (n + ns)], refs[2 * (n + ns):])
        start()
        middle()
        finish()

    return pl.pallas_call(
        body, name="tail_reduce", out_shape=_reduce_out_shapes(parts, vals),
        in_specs=[_VMEM] * (n + ns), out_specs=[_VMEM] * (n + ns), scratch_shapes=_reduce_scratch(parts, vals),
        compiler_params=pltpu.CompilerParams(vmem_limit_bytes=VMEM_LIMIT),
    )(*parts, *vals)


_VMEM = pl.BlockSpec(memory_space=pltpu.VMEM)


def _reduce_out_shapes(parts, vals):
    return ([jax.ShapeDtypeStruct((N_CHIPS,) + p.shape[1:], p.dtype) for p in parts]
            + [jax.ShapeDtypeStruct(v.shape, v.dtype) for v in vals])


def _reduce_scratch(parts, vals):
    n, ns = len(parts), len(vals)
    quarter = [pltpu.VMEM((N_CHIPS,) + p.shape[1:], p.dtype) for p in parts]
    dma = pltpu.SemaphoreType.DMA
    travel = [BF16 if v.size > F32_TRAVEL_LIMIT else v.dtype for v in vals]
    return (quarter * 3 + [pltpu.VMEM(v.shape, v.dtype) for v in vals]
            + [pltpu.VMEM(v.shape, t) for v, t in zip(vals, travel)]
            + [pltpu.VMEM((N_CHIPS,) + v.shape, t) for v, t in zip(vals, travel)]
            + [dma((max(n, 1), N_CHIPS)), dma((max(n, 1), N_CHIPS)), dma((max(ns, 1),)), dma((max(ns, 1),)),
               dma((max(n, 1), 3)), dma((max(n, 1), 3)), dma((max(ns, 1), 3)), dma((max(ns, 1), 3))])


def _reduce_plan(p_in, v_in, p_out, v_out, scratch):
    n, ns = len(p_in), len(v_in)
    p_sib, p_sum, p_all = scratch[:n], scratch[n:2 * n], scratch[2 * n:3 * n]
    v_sib, v_sum, v_all = (scratch[3 * n + j * ns:3 * n + (j + 1) * ns] for j in range(3))
    p1_send, p1_recv, v1_send, v1_recv, p3_send, p3_recv, v3_send, v3_recv = scratch[3 * n + 3 * ns:]
    x, y, c = _position()
    my_q = 2 * x + y
    peers = [(x ^ fx, y ^ fy) for fx, fy in _CHIP_FLIPS]

    def to_sibling(src, dst, send, recv):
        return pltpu.make_async_remote_copy(src_ref=src, dst_ref=dst, send_sem=send, recv_sem=recv,
                                            device_id=(x, y, 1 - c), device_id_type=MESH)

    def level1():
        cps = [to_sibling(p_in[k].at[2 * q + 1 - c], p_sib[k].at[q], p1_send.at[k, q], p1_recv.at[k, q])
               for k in range(n) for q in range(N_CHIPS)]
        return cps + [to_sibling(v_in[k], v_sib[k], v1_send.at[k], v1_recv.at[k]) for k in range(ns)]

    def to_chip(f, src, dst, send, recv):
        px, py = peers[f]
        return pltpu.make_async_remote_copy(src_ref=src, dst_ref=dst, send_sem=send, recv_sem=recv,
                                            device_id=(px, py, c), device_id_type=MESH)

    def level2(sending):
        cps = []
        for f, (px, py) in enumerate(peers):
            their_q = 2 * px + py
            for k in range(n):
                src, dst = (p_sum[k].at[their_q], p_all[k].at[my_q]) if sending else (
                    p_sum[k].at[my_q], p_all[k].at[their_q])
                cps.append(to_chip(f, src, dst, p3_send.at[k, f], p3_recv.at[k, f]))
            for k in range(ns):
                dst = v_all[k].at[my_q] if sending else v_all[k].at[their_q]
                cps.append(to_chip(f, v_sum[k], dst, v3_send.at[k, f], v3_recv.at[k, f]))
        return cps

    def start():
        for cp in level1():
            cp.start()

    def middle():
        for cp in level1():
            cp.wait_recv()
        for k in range(n):
            for q in range(N_CHIPS):
                p_sum[k][q] = (p_in[k][2 * q + c].astype(F32) + p_sib[k][q].astype(F32)).astype(p_sum[k].dtype)
        for k in range(ns):
            v_sum[k][...] = (v_in[k][...] + v_sib[k][...]).astype(v_sum[k].dtype)
        for cp in level2(True):
            cp.start()
        for k in range(n):
            p_all[k][my_q] = p_sum[k][my_q]
        for k in range(ns):
            v_all[k][my_q] = v_sum[k][...]

    def finish():
        for cp in level2(False):
            cp.wait_recv()
        for k in range(n):
            p_out[k][...] = p_all[k][...]
        for k in range(ns):
            total = v_all[k][0].astype(F32)
            for q in range(1, N_CHIPS):
                total = total + v_all[k][q].astype(F32)
            v_out[k][...] = total
        for cp in level1() + level2(True):
            cp.wait_send()

    return start, middle, finish


def _adamw_math(w, g, m, v):
    m = ADAM_B1 * m + (1.0 - ADAM_B1) * g
    v = ADAM_B2 * v + (1.0 - ADAM_B2) * (g * g)
    m_hat = m / (1.0 - ADAM_B1 ** ADAM_STEP)
    v_hat = v / (1.0 - ADAM_B2 ** ADAM_STEP)
    delta = -ADAM_LR * (m_hat / (jnp.sqrt(v_hat) + ADAM_EPS) + ADAM_WD * w)
    return delta, m, v


def _adamw_shard(parts, w, m, v, tr, name):
    R, Cc = w.shape

    def body(p_ref, w_ref, m_ref, v_ref, g_out, d_out, m_out, v_out):
        g = p_ref[0].astype(F32)
        for j in range(1, N_CHIPS):
            g = g + p_ref[j].astype(F32)
        d, mn, vn = _adamw_math(w_ref[...], g, m_ref[...], v_ref[...])
        g_out[...] = g
        d_out[...] = d
        m_out[...] = mn
        v_out[...] = vn

    blk = pl.BlockSpec((tr, Cc), lambda i: (i, 0))
    return pl.pallas_call(
        body, name=name, grid=(R // tr,),
        in_specs=[pl.BlockSpec((N_CHIPS, tr, Cc), lambda i: (0, i, 0)), blk, blk, blk],
        out_specs=[blk] * 4, out_shape=[jax.ShapeDtypeStruct((R, Cc), F32)] * 4,
        compiler_params=pltpu.CompilerParams(dimension_semantics=("parallel",)),
    )(parts, w, m, v)


def _adamw_small(grads, ws, ms, vs):
    n = len(grads)
    vm = pl.BlockSpec(memory_space=pltpu.VMEM)

    def body(*refs):
        g_in, w_in, m_in, v_in = (refs[k * n:(k + 1) * n] for k in range(4))
        g_out, d_out, m_out, v_out = (refs[(4 + k) * n:(5 + k) * n] for k in range(4))
        for k in range(n):
            g = g_in[k][...]
            d, mn, vn = _adamw_math(w_in[k][...], g, m_in[k][...], v_in[k][...])
            g_out[k][...] = g
            d_out[k][...] = d
            m_out[k][...] = mn
            v_out[k][...] = vn

    shapes = [jax.ShapeDtypeStruct(g.shape, F32) for g in grads]
    return pl.pallas_call(
        body, name="adamw_small", out_shape=shapes * 4, in_specs=[vm] * (4 * n), out_specs=[vm] * (4 * n),
    )(*grads, *ws, *ms, *vs)


def _shard_cols(full):
    R, Ct = full.shape
    return jnp.transpose(full.reshape(R, N_DEV, Ct // N_DEV), (1, 0, 2))


_COL_SHARDED = ("w_in", "w_gate", "w_up")
_BIG = ("w_in", "w_out", "w_gate", "w_up", "w_down")


def _rows(nm, p):
    return p[0].T if nm in _COL_SHARDED else p[0]


def _step(args, ts_mix_fwd, ts_ffn, ts_mix_bwd, tm_grad):
    (x, g_mix, w_in, b_in, w_dw, b_dw, ln_g, ln_b, w_pool, s_pool, w_out, g_ffn, w_gate, w_up, w_down, g_final,
     loss_target) = args[:17]
    names = ["g_mix", "w_in", "b_in", "w_dw", "b_dw", "ln_g", "ln_b", "w_pool", "s_pool", "w_out", "g_ffn",
             "w_gate", "w_up", "w_down", "g_final"]
    weights = dict(zip(names, args[1:16]))
    moms = dict(zip(names, args[17:32]))
    vars_ = dict(zip(names, args[32:47]))

    S, D = x.shape[1], x.shape[2]
    x2 = x.reshape(S, D)
    tgt2 = loss_target.reshape(S, D)

    shard = lambda nm: _rows(nm, weights[nm])
    w_pool_b = w_pool[0].astype(BF16)
    dw_shard = jnp.pad(w_dw[0], ((0, SUBLANES * 4 - CONV_WIDTH), (0, 128 - w_dw.shape[2])))

    a, gate, v, m, y, h1, xn, w_dw_f, g_in, g_out, _, g_gate, g_up, g_down = _mix_fwd(
        x2, g_mix, b_in, b_dw, ln_g, ln_b, w_pool_b, s_pool, [shard("w_in"), shard("w_out"), dw_shard],
        [shard("w_gate"), shard("w_up"), shard("w_down")], ts_mix_fwd)
    wt_in, w_out_f = g_in.reshape(-1, D), g_out.reshape(-1, D)
    wt_gate, wt_up, w_down_f = g_gate.reshape(-1, D), g_up.reshape(-1, D), g_down.reshape(-1, D)
    Fd = wt_gate.shape[0]
    f_chunks = [1024] * (Fd // 1024) + ([Fd % 1024] if Fd % 1024 else [])
    dh1, hn, act, dgt, dup, dh2, loss_p, dg_final, dg_ffn = _ffn(
        h1, tgt2, g_ffn, g_final.reshape(1, D), wt_gate, wt_up, w_down_f, ts_ffn, f_chunks)

    by_shard = lambda g: g.reshape(N_DEV, -1, D)
    p_gate = by_shard(_grad_matmul(dgt, hn, tm_grad, "grad_w_gate"))
    g_up, s_gate = _grad_matmul_pair_exchange(dup, hn, tm_grad, "grad_w_up", [p_gate])
    p_up = by_shard(g_up)
    g_down, s_up = _grad_matmul_pair_exchange(act, dh2, tm_grad, "grad_w_down", [p_up])
    p_down = by_shard(g_down)
    s_down, = _pair_exchange([p_down])
    pair_sums = _pair_sum([p_gate, p_up, p_down], [s_gate, s_up, s_down])
    (dx, dz, dh1b, dg_mix, db_in, dw_dw, db_dw, dln_g, dln_b, dw_pool, ds_pool, r_gate, r_up, r_down) = _mix_bwd(
        dh1, x2, a, gate, v, m, g_mix, wt_in, w_dw_f, ln_g, ln_b, w_pool_b, s_pool, w_out_f, pair_sums, ts_mix_bwd)

    small_names = ["g_mix", "b_in", "b_dw", "ln_g", "ln_b", "w_pool", "s_pool", "g_ffn", "g_final"]
    small_shape = lambda p: p.reshape(-1, p.shape[-1])
    partial = [dg_mix, db_in, db_dw, dln_g, dln_b, dw_pool.reshape(-1, POOL_GROUP), ds_pool, dg_ffn, dg_final, loss_p]
    dw_out, *summed = _grad_matmul(y, dh1b, tm_grad, "grad_w_out", reduce_vals=partial)
    dwt_in, r_out, r_dw = _grad_matmul(
        dz, xn, tm_grad, "grad_w_in",
        reduce_parts=[dw_out.reshape(N_DEV, -1, D), _shard_cols(dw_dw[0:CONV_WIDTH])])
    r_in, = _tail_reduce([dwt_in.reshape(N_DEV, -1, D)], [])

    big = {}
    for nm, r in zip(_BIG, [r_in, r_out, r_gate, r_up, r_down]):
        rows = r.shape[1]
        res = _adamw_shard(r, _rows(nm, weights[nm]), _rows(nm, moms[nm]), _rows(nm, vars_[nm]),
                           rows // 2 if rows % 32 == 0 else rows, "adamw_" + nm)
        big[nm] = [o.T if nm in _COL_SHARDED else o for o in res]
    big["w_dw"] = _adamw_shard(r_dw, w_dw[0], moms["w_dw"][0], vars_["w_dw"][0], CONV_WIDTH, "adamw_w_dw")

    sm = _adamw_small(summed[:-1], [small_shape(weights[nm]) for nm in small_names],
                      [small_shape(moms[nm]) for nm in small_names], [small_shape(vars_[nm]) for nm in small_names])
    n_small = len(small_names)

    def result(kind, nm):
        if nm in big:
            return big[nm][kind].reshape(weights[nm].shape)
        return sm[kind * n_small + small_names.index(nm)].reshape(weights[nm].shape)

    loss = summed[-1][0, 0]
    out = [loss, dx.reshape(x.shape)]
    for kind in range(4):
        out += [result(kind, nm) for nm in names]
    return tuple(out)


def kernel(x, g_mix, w_in, b_in, w_dw, b_dw, ln_g, ln_b, w_pool, s_pool, w_out, g_ffn, w_gate, w_up, w_down, g_final, loss_target, m_g_mix, m_w_in, m_b_in, m_w_dw, m_b_dw, m_ln_g, m_ln_b, m_w_pool, m_s_pool, m_w_out, m_g_ffn, m_w_gate, m_w_up, m_w_down, m_g_final, v_g_mix, v_w_in, v_b_in, v_w_dw, v_b_dw, v_ln_g, v_ln_b, v_w_pool, v_s_pool, v_w_out, v_g_ffn, v_w_gate, v_w_up, v_w_down, v_g_final):
    args = (x, g_mix, w_in, b_in, w_dw, b_dw, ln_g, ln_b, w_pool, s_pool, w_out, g_ffn, w_gate, w_up, w_down, g_final, loss_target, m_g_mix, m_w_in, m_b_in, m_w_dw, m_b_dw, m_ln_g, m_ln_b, m_w_pool, m_s_pool, m_w_out, m_g_ffn, m_w_gate, m_w_up, m_w_down, m_g_final, v_g_mix, v_w_in, v_b_in, v_w_dw, v_b_dw, v_ln_g, v_ln_b, v_w_pool, v_s_pool, v_w_out, v_g_ffn, v_w_gate, v_w_up, v_w_down, v_g_final)
    return _step(args, ts_mix_fwd=512, ts_ffn=256, ts_mix_bwd=512, tm_grad=256)
```

```python
import jax
import jax.numpy as jnp
from jax import lax
from jax.experimental import pallas as pl
from jax.experimental.pallas import tpu as pltpu

F32 = jnp.float32
BF16 = jnp.bfloat16
MESH = pl.DeviceIdType.MESH
N_DEV = 8

C_CONV = 512
CONV_WIDTH = 31
POOL_WINDOWS = (2, 4, 8, 16)
POOL_GROUP = 128
RMS_EPS = 1e-6
LN_EPS = 1e-5

ADAM_LR = 0.001
ADAM_B1 = 0.9
ADAM_B2 = 0.999
ADAM_EPS = 1e-08
ADAM_WD = 0.01
ADAM_STEP = 10

HALO = 32
SUBLANES = 8
LANES = 128
CONV_ROWS = 64
VMEM_LIMIT = 56 * 1024 * 1024


def _dot(a, b):
    return jnp.dot(a, b, preferred_element_type=F32)


def _dot_nt(a, b):
    return lax.dot_general(a, b, (((1,), (1,)), ((), ())), preferred_element_type=F32)


def _dot_tn(a, b):
    return lax.dot_general(a, b, (((0,), (0,)), ((), ())), preferred_element_type=F32)


def _mean_last(v):
    return jnp.mean(v, axis=-1, keepdims=True)


def _full(shape):
    nd = len(shape)
    return pl.BlockSpec(shape, lambda *_: (0,) * nd)


def _full1(shape):
    nd = len(shape)
    return pl.BlockSpec(shape, lambda *_: (0,) * nd, pipeline_mode=pl.Buffered(1))


def _shifted_copies(sh_ref, rows):
    for s in range(1, SUBLANES):
        sh_ref[s, 0:rows, :] = sh_ref[0, s:s + rows, :]


def _tap(sh_ref, off, r0, rows):
    q, s = divmod(off, SUBLANES)
    return sh_ref[s, pl.ds(r0 + q * SUBLANES, rows), :]


def _window_sums(src, bufs, rows, backward):
    assert all(w == 2 << g for g, w in enumerate(POOL_WINDOWS))
    n = len(POOL_WINDOWS)
    out = []
    for level in range(n):
        dst, shift = bufs[level % 2], 1 << level
        lanes = slice(level * POOL_GROUP, n * POOL_GROUP)
        lo, hi = (SUBLANES * (level + 1), rows) if backward else (0, rows - SUBLANES * (level + 1))
        other = slice(lo - shift, hi - shift) if backward else slice(lo + shift, hi + shift)
        dst[lo:hi, lanes] = src[lo:hi, lanes] + src[other, lanes]
        src = dst
        out.append(dst)
    return out


def _mix_fwd(x, g_mix, b_in, b_dw, ln_g, ln_b, w_pool, s_pool, mix_shards, ag_shards, ts):
    S, D = x.shape
    d_in = mix_shards[0].shape[0] * N_DEV
    C = C_CONV
    nt = S // ts
    nrb = ts // CONV_ROWS
    n_ag = len(ag_shards)
    relay_step = nt // 2
    mix_dtypes = [BF16, BF16, F32]

    def body(x_ref, g_ref, bin_ref, bdw_ref, lng_ref, lnb_ref, wp_ref, sp_ref, *rest):
        mx_in, ag_in, rest = rest[:3], rest[3:3 + n_ag], rest[3 + n_ag:]
        a_ref, gate_ref, v_ref, m_ref, y_ref, h1_ref, xn_ref, wdw_ref = rest[:8]
        mx_out, ag_out, rest = rest[8:11], rest[11:11 + n_ag], rest[11 + n_ag:]
        ush, pbuf, pa, pb, win_ref, wout_ref, gdw, load_sems = rest[:8]
        rest = rest[8:]
        mx_stage, mx_bufs, mx_sems, rest = rest[:3], rest[3:6], rest[6:9], rest[9:]
        ag_stage, ag_bufs, ag_sems = rest[:n_ag], rest[n_ag:2 * n_ag], rest[2 * n_ag:]
        i = pl.program_id(0)
        mx_start, mx_relay, mx_pass_on, mx_finish = _gather_plan(mx_in, mx_out, mx_stage, mx_bufs, *mx_sems)
        ag_start, ag_relay, ag_pass_on, ag_finish = _gather_plan(ag_in, ag_out, ag_stage, ag_bufs, *ag_sems)

        @pl.when(i == 0)
        def _():
            mx_start()
            mx_relay()
            ag_start()
            mx_pass_on()
            mx_finish()
            loads = [pltpu.make_async_copy(mx_out[2], gdw, load_sems.at[2 * N_DEV])]
            for j in range(N_DEV):
                r_in, r_out = mx_out[0].shape[1], mx_out[1].shape[1]
                loads += [pltpu.make_async_copy(mx_out[0].at[j], win_ref.at[pl.ds(j * r_in, r_in), :],
                                                load_sems.at[2 * j]),
                          pltpu.make_async_copy(mx_out[1].at[j], wout_ref.at[pl.ds(j * r_out, r_out), :],
                                                load_sems.at[2 * j + 1])]
            for cp in loads:
                cp.start()
            for cp in loads:
                cp.wait()
            first_half = lax.broadcasted_iota(jnp.int32, gdw.shape[1:], 1) < C // N_DEV
            for p in range(N_DEV // 2):
                wdw_ref[:, LANES * p:LANES * (p + 1)] = jnp.where(
                    first_half, gdw[2 * p], pltpu.roll(gdw[2 * p + 1], C // N_DEV, axis=1))
            ush[0, 0:HALO, :] = jnp.zeros((HALO, C), F32)
            pbuf[0:HALO, :] = jnp.zeros((HALO, C), F32)

        @pl.when(i == relay_step)
        def _():
            ag_relay()

        @pl.when(i == nt - 1)
        def _():
            ag_pass_on()

        x = x_ref[...]
        r1 = lax.rsqrt(_mean_last(x * x) + RMS_EPS)
        xn = (x * r1 * g_ref[...]).astype(BF16)
        xn_ref[...] = xn
        z = _dot_nt(xn, win_ref[...]) + bin_ref[...]
        a = z[:, 0:C]
        gate = z[:, C:2 * C]
        a_ref[...] = a
        gate_ref[...] = gate
        ush[0, HALO:HALO + ts, :] = a * jax.nn.sigmoid(gate)
        pbuf[HALO:HALO + ts, :] = z[:, 2 * C:]

        _shifted_copies(ush, ts + HALO - SUBLANES)

        def conv_block(rb, carry):
            r0 = pl.multiple_of(rb * CONV_ROWS, CONV_ROWS)
            acc = jnp.zeros((CONV_ROWS, C), F32)
            for k in range(CONV_WIDTH):
                acc = acc + wdw_ref[k:k + 1, :] * _tap(ush, HALO - (CONV_WIDTH - 1) + k, r0, CONV_ROWS)
            v_ref[pl.ds(r0, CONV_ROWS), :] = acc + bdw_ref[...]
            return carry

        lax.fori_loop(0, nrb, conv_block, 0)

        v = v_ref[...]
        mu = _mean_last(v)
        xc = v - mu
        rstd = lax.rsqrt(_mean_last(xc * xc) + LN_EPS)
        ln = xc * rstd * lng_ref[...] + lnb_ref[...]
        y_ref[:, 0:C] = (ln * jax.nn.sigmoid(ln)).astype(BF16)

        sums = _window_sums(pbuf, (pa, pb), ts + HALO, backward=True)
        row = lax.broadcasted_iota(jnp.int32, (ts, 1), 0) + i * ts
        for gi, w in enumerate(POOL_WINDOWS):
            lanes = slice(gi * POOL_GROUP, (gi + 1) * POOL_GROUP)
            seg = pbuf[HALO:HALO + ts, lanes]
            ws = sums[gi][HALO:HALO + ts, lanes]
            cnt = jnp.minimum(row + 1, w).astype(F32)
            m = (ws / cnt - seg).astype(BF16)
            m_ref[:, lanes] = m
            ypre = _dot(m, wp_ref[gi])
            y_ref[:, C + gi * POOL_GROUP:C + (gi + 1) * POOL_GROUP] = (ypre * sp_ref[:, lanes]).astype(BF16)

        h1_ref[...] = x + _dot(y_ref[...], wout_ref[...])

        ush[0, 0:HALO, :] = ush[0, ts:ts + HALO, :]
        pbuf[0:HALO, :] = pbuf[ts:ts + HALO, :]

        @pl.when(i == nt - 1)
        def _():
            ag_finish()

    tile = lambda w, dt: (pl.BlockSpec((ts, w), lambda i: (i, 0)), jax.ShapeDtypeStruct((S, w), dt))
    wdw_rows = mix_shards[2].shape[0]
    outs = [tile(C, F32), tile(C, F32), tile(C, F32), tile(C, BF16), tile(D, BF16), tile(D, F32), tile(D, BF16),
            (_full((wdw_rows, C)), jax.ShapeDtypeStruct((wdw_rows, C), F32))]
    return pl.pallas_call(
        body, name="mix_fwd", grid=(nt,),
        in_specs=[pl.BlockSpec((ts, D), lambda i: (i, 0)), _full((1, D)), _full((1, d_in)), _full((1, C)),
                  _full((1, C)), _full((1, C)), _full(w_pool.shape), _full((1, C))] + [_HBM] * (3 + n_ag),
        out_specs=[o[0] for o in outs] + [_HBM] * (3 + n_ag),
        out_shape=[o[1] for o in outs] + _gather_out_shapes(mix_shards, mix_dtypes)
        + _gather_out_shapes(ag_shards, [BF16] * n_ag),
        scratch_shapes=[pltpu.VMEM((SUBLANES, ts + HALO, C), F32)] + [pltpu.VMEM((ts + HALO, C), F32)] * 3
        + [pltpu.VMEM((d_in, D), BF16), pltpu.VMEM((D, D), BF16),
                        pltpu.VMEM((N_DEV,) + mix_shards[2].shape, F32), pltpu.SemaphoreType.DMA((2 * N_DEV + 1,))]
        + _gather_scratch(mix_shards, mix_dtypes) + _gather_scratch(ag_shards, [BF16] * n_ag),
        compiler_params=pltpu.CompilerParams(dimension_semantics=("arbitrary",), vmem_limit_bytes=VMEM_LIMIT),
    )(x, g_mix, b_in, b_dw, ln_g, ln_b, w_pool, s_pool, *mix_shards, *ag_shards)


def _ffn(h1, target, g_ffn, g_final, w_gate, w_up, w_down, ts, f_chunks):
    S, D = h1.shape
    Fd = w_gate.shape[0]
    nt = S // ts
    bounds = []
    lo = 0
    for n in f_chunks:
        bounds.append((lo, lo + n))
        lo += n
    assert lo == Fd

    def body(h1_ref, tgt_ref, gf_ref, gl_ref, wg_ref, wu_ref, wd_ref,
             dh1_ref, hn_ref, act_ref, dgt_ref, dup_ref, dh2_ref, loss_ref, dgl_ref, dgf_ref, gt_s, up_s):
        i = pl.program_id(0)

        @pl.when(i == 0)
        def _():
            loss_ref[...] = jnp.zeros_like(loss_ref)
            dgl_ref[...] = jnp.zeros_like(dgl_ref)
            dgf_ref[...] = jnp.zeros_like(dgf_ref)

        h1 = h1_ref[...]
        r2 = lax.rsqrt(_mean_last(h1 * h1) + RMS_EPS)
        hhat = h1 * r2
        hn = (hhat * gf_ref[...]).astype(BF16)
        hn_ref[...] = hn
        h2 = h1
        for lo, hi in bounds:
            gt = _dot_nt(hn, wg_ref[lo:hi, :])
            up = _dot_nt(hn, wu_ref[lo:hi, :])
            gt_s[:, lo:hi] = gt
            up_s[:, lo:hi] = up
            act = (gt * jax.nn.sigmoid(gt) * up).astype(BF16)
            act_ref[:, lo:hi] = act
            h2 = h2 + _dot(act, wd_ref[lo:hi, :])

        r3 = lax.rsqrt(_mean_last(h2 * h2) + RMS_EPS)
        n3 = h2 * r3
        gl = gl_ref[...]
        diff = n3 * gl - tgt_ref[...]
        loss_ref[...] += jnp.sum(0.5 * jnp.sum(diff * diff, axis=-1, keepdims=True) / D, axis=0, keepdims=True)
        dout = diff / D
        dgl_ref[...] += jnp.sum(dout * n3, axis=0, keepdims=True)
        dn = dout * gl
        dh2 = r3 * (dn - n3 * _mean_last(dn * n3))
        dh2b = dh2.astype(BF16)
        dh2_ref[...] = dh2b

        dhn = jnp.zeros((ts, D), F32)
        for lo, hi in bounds:
            gt = gt_s[:, lo:hi]
            up = up_s[:, lo:hi]
            sg = jax.nn.sigmoid(gt)
            dact = _dot_nt(dh2b, wd_ref[lo:hi, :])
            dgt = (dact * up * (sg * (1.0 + gt * (1.0 - sg)))).astype(BF16)
            dup = (dact * (gt * sg)).astype(BF16)
            dgt_ref[:, lo:hi] = dgt
            dup_ref[:, lo:hi] = dup
            dhn = dhn + _dot(dgt, wg_ref[lo:hi, :]) + _dot(dup, wu_ref[lo:hi, :])

        dgf_ref[...] += jnp.sum(dhn * hhat, axis=0, keepdims=True)
        dnn = dhn * gf_ref[...]
        dh1_ref[...] = dh2 + r2 * (dnn - hhat * _mean_last(dnn * hhat))

    tile = lambda w, dt: (pl.BlockSpec((ts, w), lambda i: (i, 0)), jax.ShapeDtypeStruct((S, w), dt))
    acc = lambda w: (_full((1, w)), jax.ShapeDtypeStruct((1, w), F32))
    outs = [tile(D, F32), tile(D, BF16), tile(Fd, BF16), tile(Fd, BF16), tile(Fd, BF16), tile(D, BF16),
            acc(LANES), acc(D), acc(D)]
    return pl.pallas_call(
        body, name="ffn_fwd_bwd", grid=(nt,),
        in_specs=[pl.BlockSpec((ts, D), lambda i: (i, 0)), pl.BlockSpec((ts, D), lambda i: (i, 0)),
                  _full((1, D)), _full((1, D)), _full1((Fd, D)), _full1((Fd, D)), _full1((Fd, D))],
        out_specs=[o[0] for o in outs], out_shape=[o[1] for o in outs],
        scratch_shapes=[pltpu.VMEM((ts, Fd), F32), pltpu.VMEM((ts, Fd), F32)],
        compiler_params=pltpu.CompilerParams(dimension_semantics=("arbitrary",), vmem_limit_bytes=VMEM_LIMIT),
    )(h1, target, g_ffn, g_final, w_gate, w_up, w_down)


def _mix_bwd(dh1, x, a, gate, v, m, g_mix, w_in, w_dw, ln_g, ln_b, w_pool, s_pool, w_out, rs_parts, ts):
    S, D = x.shape
    n_rs = len(rs_parts)
    d_in = w_in.shape[0]
    C = C_CONV
    nt = S // ts
    nrb = ts // CONV_ROWS
    wrows =((CONV_WIDTH + SUBLANES - 1) // SUBLANES) * SUBLANES

    def body(dh1_ref, x_ref, a_ref, gate_ref, v_ref, m_ref, g_ref, win_ref, wdw_ref, lng_ref,
             lnb_ref, wp_ref, sp_ref, wout_ref, *rest):
        rs_in, rest = rest[:n_rs], rest[n_rs:]
        (dx_ref, dz_ref, dh1b_ref, dgm_ref, dbin_ref, dwdw_ref, dbdw_ref, dlng_ref, dlnb_ref, dwp_ref,
         dsp_ref) = rest[:11]
        rs_out, rest = rest[11:11 + n_rs], rest[11 + n_rs:]
        dvsh, dqbuf, qa, qb, du_s, dm_s = rest[:6]
        rs_bufs, (send_sems, recv_sems, local_sems) = rest[6:6 + n_rs], rest[6 + n_rs:]
        i = pl.program_id(0)
        t = nt - 1 - i
        rs_start, rs_finish = _chip_exchange_plan(rs_in, rs_out, rs_bufs, send_sems, recv_sems, local_sems)

        @pl.when(i == 0)
        def _():
            rs_start()
            dvsh[0, ts:ts + HALO, :] = jnp.zeros((HALO, C), F32)
            dqbuf[ts:ts + HALO, :] = jnp.zeros((HALO, C), F32)
            for r in (dgm_ref, dbin_ref, dwdw_ref, dbdw_ref, dlng_ref, dlnb_ref, dwp_ref, dsp_ref):
                r[...] = jnp.zeros_like(r)

        dh1 = dh1_ref[...]
        dh1b = dh1.astype(BF16)
        dh1b_ref[...] = dh1b
        dy = _dot_nt(dh1b, wout_ref[...])

        v = v_ref[...]
        mu = _mean_last(v)
        xc = v - mu
        rstd = lax.rsqrt(_mean_last(xc * xc) + LN_EPS)
        vhat = xc * rstd
        lng = lng_ref[...]
        ln = vhat * lng + lnb_ref[...]
        sg = jax.nn.sigmoid(ln)
        dln = dy[:, 0:C] * (sg * (1.0 + ln * (1.0 - sg)))
        dlng_ref[...] += jnp.sum(dln * vhat, axis=0, keepdims=True)
        dlnb_ref[...] += jnp.sum(dln, axis=0, keepdims=True)
        dvh = dln * lng
        dv = rstd * (dvh - _mean_last(dvh) - vhat * _mean_last(dvh * vhat))
        dbdw_ref[...] += jnp.sum(dv, axis=0, keepdims=True)
        dvsh[0, 0:ts, :] = dv
        _shifted_copies(dvsh, ts + HALO - SUBLANES)

        def conv_block(rb, carry):
            r0 = pl.multiple_of(rb * CONV_ROWS, CONV_ROWS)
            acc = jnp.zeros((CONV_ROWS, C), F32)
            for k in range(CONV_WIDTH):
                acc = acc + wdw_ref[k:k + 1, :] * _tap(dvsh, CONV_WIDTH - 1 - k, r0, CONV_ROWS)
            du_s[pl.ds(r0, CONV_ROWS), :] = acc
            return carry

        lax.fori_loop(0, nrb, conv_block, 0)

        a = a_ref[...]
        sgate = jax.nn.sigmoid(gate_ref[...])
        u = a * sgate
        for k in range(CONV_WIDTH):
            q, s = divmod(CONV_WIDTH - 1 - k, SUBLANES)
            prod = u * dvsh[s, q * SUBLANES:q * SUBLANES + ts, :]
            dwdw_ref[k:k + 1, :] += jnp.sum(prod, axis=0, keepdims=True)

        du = du_s[...]
        da = du * sgate
        dgate = du * a * sgate * (1.0 - sgate)
        dz_ref[:, 0:C] = da.astype(BF16)
        dz_ref[:, C:2 * C] = dgate.astype(BF16)
        dbin_ref[:, 0:C] += jnp.sum(da, axis=0, keepdims=True)
        dbin_ref[:, C:2 * C] += jnp.sum(dgate, axis=0, keepdims=True)

        row = lax.broadcasted_iota(jnp.int32, (ts, 1), 0) + t * ts
        for gi, w in enumerate(POOL_WINDOWS):
            lanes = slice(gi * POOL_GROUP, (gi + 1) * POOL_GROUP)
            dyp = dy[:, C + gi * POOL_GROUP:C + (gi + 1) * POOL_GROUP]
            mg = m_ref[:, lanes]
            ypre = _dot(mg, wp_ref[gi])
            dsp_ref[:, lanes] += jnp.sum(dyp * ypre, axis=0, keepdims=True)
            dyi = (dyp * sp_ref[:, lanes]).astype(BF16)
            dwp_ref[gi] += _dot_tn(mg, dyi)
            dm = _dot_nt(dyi, wp_ref[gi])
            cnt = jnp.minimum(row + 1, w).astype(F32)
            dqbuf[0:ts, lanes] = dm / cnt
            dm_s[:, lanes] = dm
        sums = _window_sums(dqbuf, (qa, qb), ts + HALO, backward=False)
        for gi in range(len(POOL_WINDOWS)):
            lanes = slice(gi * POOL_GROUP, (gi + 1) * POOL_GROUP)
            dp = sums[gi][0:ts, lanes] - dm_s[:, lanes]
            dz_ref[:, 2 * C + gi * POOL_GROUP:2 * C + (gi + 1) * POOL_GROUP] = dp.astype(BF16)
            dbin_ref[:, 2 * C + gi * POOL_GROUP:2 * C + (gi + 1) * POOL_GROUP] += jnp.sum(dp, axis=0, keepdims=True)

        dxn = _dot(dz_ref[...], win_ref[...])
        x = x_ref[...]
        r1 = lax.rsqrt(_mean_last(x * x) + RMS_EPS)
        xhat = x * r1
        dgm_ref[...] += jnp.sum(dxn * xhat, axis=0, keepdims=True)
        dnn = dxn * g_ref[...]
        dx_ref[...] = dh1 + r1 * (dnn - xhat * _mean_last(dnn * xhat))

        dvsh[0, ts:ts + HALO, :] = dvsh[0, 0:HALO, :]
        dqbuf[ts:ts + HALO, :] = dqbuf[0:HALO, :]

        @pl.when(i == nt - 1)
        def _():
            rs_finish()

    rev = lambda w: pl.BlockSpec((ts, w), lambda i: (nt - 1 - i, 0))
    acc = lambda shape: (_full(shape), jax.ShapeDtypeStruct(shape, F32))
    outs = [(rev(D), jax.ShapeDtypeStruct((S, D), F32)), (rev(d_in), jax.ShapeDtypeStruct((S, d_in), BF16)),
            (rev(D), jax.ShapeDtypeStruct((S, D), BF16)), acc((1, D)), acc((1, d_in)), acc((wrows, C)),
            acc((1, C)), acc((1, C)), acc((1, C)), acc(w_pool.shape), acc((1, C))]
    return pl.pallas_call(
        body, name="mix_bwd", grid=(nt,),
        in_specs=[rev(D), rev(D), rev(C), rev(C), rev(C), rev(C), _full((1, D)), _full((d_in, D)),
                  _full(w_dw.shape), _full((1, C)), _full((1, C)), _full(w_pool.shape), _full((1, C)),
                  _full((D, D))] + [_HBM] * n_rs,
        out_specs=[o[0] for o in outs] + [_HBM] * n_rs,
        out_shape=[o[1] for o in outs] + [jax.ShapeDtypeStruct(p.shape, p.dtype) for p in rs_parts],
        scratch_shapes=[pltpu.VMEM((SUBLANES, ts + HALO, C), F32)] + [pltpu.VMEM((ts + HALO, C), F32)] * 3
        + [pltpu.VMEM((ts, C), F32)] * 2
        + [pltpu.VMEM(p.shape[1:], p.dtype) for p in rs_parts] + _comm_sems(n_rs, 3),
        compiler_params=pltpu.CompilerParams(dimension_semantics=("arbitrary",), vmem_limit_bytes=VMEM_LIMIT),
    )(dh1, x, a, gate, v, m, g_mix, w_in, w_dw, ln_g, ln_b, w_pool, s_pool, w_out, *rs_parts)


def _grad_matmul(a, b, tm, name, reduce_parts=(), reduce_vals=()):
    S, M = a.shape
    N = b.shape[1]
    steps = M // tm
    n, ns = len(reduce_parts), len(reduce_vals)

    def body(a_ref, b_ref, *rest):
        o_ref = rest[n + ns]
        if n + ns:
            start, middle, finish = _reduce_plan(rest[:n], rest[n:n + ns], rest[n + ns + 1:2 * n + ns + 1],
                                                 rest[2 * n + ns + 1:2 * (n + ns) + 1], rest[2 * (n + ns) + 1:])
            i = pl.program_id(0)
            pl.when(i == 0)(start)
            pl.when(i == 1)(middle)
        o_ref[...] = _dot_tn(a_ref[...], b_ref[...]).astype(BF16)
        if n + ns:
            pl.when(i == steps - 1)(finish)

    res = pl.pallas_call(
        body, name=name, grid=(steps,),
        in_specs=[pl.BlockSpec((S, tm), lambda i: (0, i)), _full1((S, N))] + [_VMEM] * (n + ns),
        out_specs=[pl.BlockSpec((tm, N), lambda i: (i, 0))] + [_VMEM] * (n + ns),
        out_shape=[jax.ShapeDtypeStruct((M, N), BF16)] + _reduce_out_shapes(reduce_parts, reduce_vals),
        scratch_shapes=_reduce_scratch(reduce_parts, reduce_vals) if n + ns else [],
        compiler_params=pltpu.CompilerParams(dimension_semantics=("arbitrary",) if n + ns else ("parallel",),
                                             vmem_limit_bytes=VMEM_LIMIT),
    )(a, b, *reduce_parts, *reduce_vals)
    return res if n + ns else res[0]


def _grad_matmul_pair_exchange(a, b, tm, name, parts):
    S, M = a.shape
    N = b.shape[1]
    steps = M // tm
    n = len(parts)

    def body(a_ref, b_ref, *rest):
        o_ref = rest[n]
        start, finish = _pair_exchange_plan(rest[:n], rest[n + 1:2 * n + 1], *rest[2 * n + 1:])
        i = pl.program_id(0)
        pl.when(i == 0)(start)
        o_ref[...] = _dot_tn(a_ref[...], b_ref[...]).astype(BF16)
        pl.when(i == steps - 1)(finish)

    sems = pltpu.SemaphoreType.DMA((n, N_CHIPS))
    return pl.pallas_call(
        body, name=name, grid=(steps,),
        in_specs=[pl.BlockSpec((S, tm), lambda i: (0, i)), _full1((S, N))] + [_HBM] * n,
        out_specs=[pl.BlockSpec((tm, N), lambda i: (i, 0))] + [_HBM] * n,
        out_shape=[jax.ShapeDtypeStruct((M, N), BF16)] + _pair_out_shapes(parts),
        scratch_shapes=[sems, sems],
        compiler_params=pltpu.CompilerParams(dimension_semantics=("arbitrary",), vmem_limit_bytes=VMEM_LIMIT),
    )(a, b, *parts)


def _position():
    return lax.axis_index("x"), lax.axis_index("y"), lax.axis_index("c")


def _slot(px, py, pc):
    return 4 * px + 2 * py + pc


_HBM = pl.BlockSpec(memory_space=pl.ANY)


def _comm_sems(n, copies):
    return [pltpu.SemaphoreType.DMA((n, copies)), pltpu.SemaphoreType.DMA((n, copies)),
            pltpu.SemaphoreType.DMA((n, 2))]


def _local_copy(srcs, dsts, bufs, local_sems):
    n = len(srcs)
    loads = [pltpu.make_async_copy(srcs[k], bufs[k], local_sems.at[k, 0]) for k in range(n)]
    for cp in loads:
        cp.start()
    for cp in loads:
        cp.wait()
    stores = _local_stores(dsts, bufs, local_sems)
    for cp in stores:
        cp.start()
    return stores


def _local_stores(dsts, bufs, local_sems):
    return [pltpu.make_async_copy(bufs[k], dsts[k], local_sems.at[k, 1]) for k in range(len(dsts))]


def _gather_out_shapes(shards, dtypes):
    return [jax.ShapeDtypeStruct((N_DEV,) + s.shape, dt) for s, dt in zip(shards, dtypes)]


def _gather_scratch(shards, dtypes):
    return ([pltpu.VMEM(s.shape, s.dtype) for s in shards] + [pltpu.VMEM(s.shape, dt) for s, dt in zip(shards, dtypes)]
            + _comm_sems(len(shards), 7))


def _gather_plan(ins, outs, stage, bufs, send_sems, recv_sems, local_sems):
    n = len(ins)
    x, y, c = _position()
    me, sibling = (x, y, c), (x, y, 1 - c)
    na, nb, dg = (x ^ (1 - c), y ^ c), (x ^ c, y ^ (1 - c)), (1 - x, 1 - y)
    own = [outs[k].at[_slot(*me)] for k in range(n)]

    def copy(k, sem, block, to, src=None):
        dst = outs[k].at[_slot(*block)]
        return pltpu.make_async_remote_copy(
            src_ref=dst if src is None else src, dst_ref=dst, send_sem=send_sems.at[k, sem],
            recv_sem=recv_sems.at[k, sem], device_id=to, device_id_type=MESH)

    def first():
        cps = []
        for k in range(n):
            cps += [copy(k, 0, me, sibling, src=bufs[k]), copy(k, 1, me, (*na, c), src=bufs[k]),
                    copy(k, 2, me, (*nb, c), src=bufs[k])]
        return cps

    def onward():
        return [copy(k, 3, (*na, c), (*nb, c)) for k in range(n)]

    def to_sibling(j, chip):
        return [copy(k, 4 + j, (*chip, c), sibling) for k in range(n)]

    def start():
        loads = [pltpu.make_async_copy(ins[k], stage[k], local_sems.at[k, 0]) for k in range(n)]
        for cp in loads:
            cp.start()
        for cp in loads:
            cp.wait()
        for k in range(n):
            bufs[k][...] = stage[k][...].astype(bufs[k].dtype)
        for cp in first() + _local_stores(own, bufs, local_sems):
            cp.start()

    def relay():
        for k in range(n):
            copy(k, 1, (*na, c), me).wait_recv()
        for cp in onward() + to_sibling(0, na):
            cp.start()
        for k in range(n):
            copy(k, 2, (*nb, c), me).wait_recv()
        for cp in to_sibling(1, nb):
            cp.start()

    def pass_on():
        for k in range(n):
            copy(k, 3, (*dg, c), me).wait_recv()
        for cp in to_sibling(2, dg):
            cp.start()

    def finish():
        for k in range(n):
            copy(k, 0, sibling, me).wait_recv()
            for j, chip in enumerate((nb, na, dg)):
                copy(k, 4 + j, (*chip, 1 - c), me).wait_recv()
        for cp in first() + onward() + to_sibling(0, na) + to_sibling(1, nb) + to_sibling(2, dg):
            cp.wait_send()
        for cp in _local_stores(own, bufs, local_sems):
            cp.wait()

    return start, relay, pass_on, finish


N_CHIPS = 4
_CHIP_FLIPS = [(1, 0), (0, 1), (1, 1)]
F32_TRAVEL_LIMIT = 4096


def _chip_exchange_plan(ins, outs, bufs, send_sems, recv_sems, local_sems):
    n = len(ins)
    x, y, c = _position()
    my_q = 2 * x + y
    peers = [(x ^ fx, y ^ fy) for fx, fy in _CHIP_FLIPS]
    own = [outs[k].at[my_q] for k in range(n)]

    def sends():
        return [pltpu.make_async_remote_copy(
            src_ref=ins[k].at[2 * px + py], dst_ref=outs[k].at[my_q], send_sem=send_sems.at[k, f],
            recv_sem=recv_sems.at[k, f], device_id=(px, py, c), device_id_type=MESH)
            for f, (px, py) in enumerate(peers) for k in range(n)]

    def start():
        for cp in sends():
            cp.start()
        _local_copy([ins[k].at[my_q] for k in range(n)], own, bufs, local_sems)

    def finish():
        for f, (px, py) in enumerate(peers):
            for k in range(n):
                pltpu.make_async_remote_copy(
                    src_ref=ins[k].at[my_q], dst_ref=outs[k].at[2 * px + py], send_sem=send_sems.at[k, f],
                    recv_sem=recv_sems.at[k, f], device_id=(px, py, c), device_id_type=MESH).wait_recv()
        for cp in sends():
            cp.wait_send()
        for cp in _local_stores(own, bufs, local_sems):
            cp.wait()

    return start, finish


def _pair_exchange(parts):
    n = len(parts)

    def body(*refs):
        start, finish = _pair_exchange_plan(refs[:n], refs[n:2 * n], *refs[2 * n:])
        start()
        finish()

    sems = pltpu.SemaphoreType.DMA((n, N_CHIPS))
    return pl.pallas_call(
        body, name="pair_exchange", out_shape=_pair_out_shapes(parts),
        in_specs=[_HBM] * n, out_specs=[_HBM] * n, scratch_shapes=[sems, sems],
    )(*parts)


def _pair_out_shapes(parts):
    return [jax.ShapeDtypeStruct((N_CHIPS,) + p.shape[1:], p.dtype) for p in parts]


def _pair_exchange_plan(ins, sib, send_sems, recv_sems):
    x, y, c = _position()

    def copies():
        return [pltpu.make_async_remote_copy(
            src_ref=ins[k].at[2 * q + 1 - c], dst_ref=sib[k].at[q], send_sem=send_sems.at[k, q],
            recv_sem=recv_sems.at[k, q], device_id=(x, y, 1 - c), device_id_type=MESH)
            for k in range(len(ins)) for q in range(N_CHIPS)]

    def start():
        for cp in copies():
            cp.start()

    def finish():
        for cp in copies():
            cp.wait()

    return start, finish


def _pair_sum(parts, sib):
    n = len(parts)

    def body(*refs):
        c = lax.axis_index("c")
        for k in range(n):
            refs[2 * n + k][0] = (refs[k][0, c].astype(F32) + refs[n + k][0].astype(F32)).astype(BF16)

    pair = [pl.BlockSpec((1, 2) + p.shape[1:], lambda q: (q, 0, 0, 0)) for p in parts]
    one = [pl.BlockSpec((1,) + p.shape[1:], lambda q: (q, 0, 0)) for p in parts]
    return pl.pallas_call(
        body, name="pair_sum", grid=(N_CHIPS,), in_specs=pair + one, out_specs=one,
        out_shape=[jax.ShapeDtypeStruct(s.shape, BF16) for s in sib],
        compiler_params=pltpu.CompilerParams(dimension_semantics=("parallel",)),
    )(*[p.reshape((N_CHIPS, 2) + p.shape[1:]) for p in parts], *sib)


def _tail_reduce(parts, vals):
    n, ns = len(parts), len(vals)

    def body(*refs):
        start, middle, finish = _reduce_plan(refs[:n], refs[n:n + ns], refs[n + ns:2 * n + ns],
                                             refs[2 * n + ns:2 * (n + ns)], refs[2 * (n + ns):])
        start()
        middle()
        finish()

    return pl.pallas_call(
        body, name="tail_reduce", out_shape=_reduce_out_shapes(parts, vals),
        in_specs=[_VMEM] * (n + ns), out_specs=[_VMEM] * (n + ns), scratch_shapes=_reduce_scratch(parts, vals),
        compiler_params=pltpu.CompilerParams(vmem_limit_bytes=VMEM_LIMIT),
    )(*parts, *vals)


_VMEM = pl.BlockSpec(memory_space=pltpu.VMEM)


def _reduce_out_shapes(parts, vals):
    return ([jax.ShapeDtypeStruct((N_CHIPS,) + p.shape[1:], p.dtype) for p in parts]
            + [jax.ShapeDtypeStruct(v.shape, v.dtype) for v in vals])


def _reduce_scratch(parts, vals):
    n, ns = len(parts), len(vals)
    quarter = [pltpu.VMEM((N_CHIPS,) + p.shape[1:], p.dtype) for p in parts]
    dma = pltpu.SemaphoreType.DMA
    travel = [BF16 if v.size > F32_TRAVEL_LIMIT else v.dtype for v in vals]
    return (quarter * 3 + [pltpu.VMEM(v.shape, v.dtype) for v in vals]
            + [pltpu.VMEM(v.shape, t) for v, t in zip(vals, travel)]
            + [pltpu.VMEM((N_CHIPS,) + v.shape, t) for v, t in zip(vals, travel)]
            + [dma((max(n, 1), N_CHIPS)), dma((max(n, 1), N_CHIPS)), dma((max(ns, 1),)), dma((max(ns, 1),)),
               dma((max(n, 1), 3)), dma((max(n, 1), 3)), dma((max(ns, 1), 3)), dma((max(ns, 1), 3))])


def _reduce_plan(p_in, v_in, p_out, v_out, scratch):
    n, ns = len(p_in), len(v_in)
    p_sib, p_sum, p_all = scratch[:n], scratch[n:2 * n], scratch[2 * n:3 * n]
    v_sib, v_sum, v_all = (scratch[3 * n + j * ns:3 * n + (j + 1) * ns] for j in range(3))
    p1_send, p1_recv, v1_send, v1_recv, p3_send, p3_recv, v3_send, v3_recv = scratch[3 * n + 3 * ns:]
    x, y, c = _position()
    my_q = 2 * x + y
    peers = [(x ^ fx, y ^ fy) for fx, fy in _CHIP_FLIPS]

    def to_sibling(src, dst, send, recv):
        return pltpu.make_async_remote_copy(src_ref=src, dst_ref=dst, send_sem=send, recv_sem=recv,
                                            device_id=(x, y, 1 - c), device_id_type=MESH)

    def level1():
        cps = [to_sibling(p_in[k].at[2 * q + 1 - c], p_sib[k].at[q], p1_send.at[k, q], p1_recv.at[k, q])
               for k in range(n) for q in range(N_CHIPS)]
        return cps + [to_sibling(v_in[k], v_sib[k], v1_send.at[k], v1_recv.at[k]) for k in range(ns)]

    def to_chip(f, src, dst, send, recv):
        px, py = peers[f]
        return pltpu.make_async_remote_copy(src_ref=src, dst_ref=dst, send_sem=send, recv_sem=recv,
                                            device_id=(px, py, c), device_id_type=MESH)

    def level2(sending):
        cps = []
        for f, (px, py) in enumerate(peers):
            their_q = 2 * px + py
            for k in range(n):
                src, dst = (p_sum[k].at[their_q], p_all[k].at[my_q]) if sending else (
                    p_sum[k].at[my_q], p_all[k].at[their_q])
                cps.append(to_chip(f, src, dst, p3_send.at[k, f], p3_recv.at[k, f]))
            for k in range(ns):
                dst = v_all[k].at[my_q] if sending else v_all[k].at[their_q]
                cps.append(to_chip(f, v_sum[k], dst, v3_send.at[k, f], v3_recv.at[k, f]))
        return cps

    def start():
        for cp in level1():
            cp.start()

    def middle():
        for cp in level1():
            cp.wait_recv()
        for k in range(n):
            for q in range(N_CHIPS):
                p_sum[k][q] = (p_in[k][2 * q + c].astype(F32) + p_sib[k][q].astype(F32)).astype(p_sum[k].dtype)
        for k in range(ns):
            v_sum[k][...] = (v_in[k][...] + v_sib[k][...]).astype(v_sum[k].dtype)
        for cp in level2(True):
            cp.start()
        for k in range(n):
            p_all[k][my_q] = p_sum[k][my_q]
        for k in range(ns):
            v_all[k][my_q] = v_sum[k][...]

    def finish():
        for cp in level2(False):
            cp.wait_recv()
        for k in range(n):
            p_out[k][...] = p_all[k][...]
        for k in range(ns):
            total = v_all[k][0].astype(F32)
            for q in range(1, N_CHIPS):
                total = total + v_all[k][q].astype(F32)
            v_out[k][...] = total
        for cp in level1() + level2(True):
            cp.wait_send()

    return start, middle, finish


def _adamw_math(w, g, m, v):
    m = ADAM_B1 * m + (1.0 - ADAM_B1) * g
    v = ADAM_B2 * v + (1.0 - ADAM_B2) * (g * g)
    m_hat = m / (1.0 - ADAM_B1 ** ADAM_STEP)
    v_hat = v / (1.0 - ADAM_B2 ** ADAM_STEP)
    delta = -ADAM_LR * (m_hat / (jnp.sqrt(v_hat) + ADAM_EPS) + ADAM_WD * w)
    return delta, m, v


def _adamw_shard(parts, w, m, v, tr, name):
    R, Cc = w.shape

    def body(p_ref, w_ref, m_ref, v_ref, g_out, d_out, m_out, v_out):
        g = p_ref[0].astype(F32)
        for j in range(1, N_CHIPS):
            g = g + p_ref[j].astype(F32)
        d, mn, vn = _adamw_math(w_ref[...], g, m_ref[...], v_ref[...])
        g_out[...] = g
        d_out[...] = d
        m_out[...] = mn
        v_out[...] = vn

    blk = pl.BlockSpec((tr, Cc), lambda i: (i, 0))
    return pl.pallas_call(
        body, name=name, grid=(R // tr,),
        in_specs=[pl.BlockSpec((N_CHIPS, tr, Cc), lambda i: (0, i, 0)), blk, blk, blk],
        out_specs=[blk] * 4, out_shape=[jax.ShapeDtypeStruct((R, Cc), F32)] * 4,
        compiler_params=pltpu.CompilerParams(dimension_semantics=("parallel",)),
    )(parts, w, m, v)


def _adamw_small(grads, ws, ms, vs):
    n = len(grads)
    vm = pl.BlockSpec(memory_space=pltpu.VMEM)

    def body(*refs):
        g_in, w_in, m_in, v_in = (refs[k * n:(k + 1) * n] for k in range(4))
        g_out, d_out, m_out, v_out = (refs[(4 + k) * n:(5 + k) * n] for k in range(4))
        for k in range(n):
            g = g_in[k][...]
            d, mn, vn = _adamw_math(w_in[k][...], g, m_in[k][...], v_in[k][...])
            g_out[k][...] = g
            d_out[k][...] = d
            m_out[k][...] = mn
            v_out[k][...] = vn

    shapes = [jax.ShapeDtypeStruct(g.shape, F32) for g in grads]
    return pl.pallas_call(
        body, name="adamw_small", out_shape=shapes * 4, in_specs=[vm] * (4 * n), out_specs=[vm] * (4 * n),
    )(*grads, *ws, *ms, *vs)


def _shard_cols(full):
    R, Ct = full.shape
    return jnp.transpose(full.reshape(R, N_DEV, Ct // N_DEV), (1, 0, 2))


_COL_SHARDED = ("w_in", "w_gate", "w_up")
_BIG = ("w_in", "w_out", "w_gate", "w_up", "w_down")


def _rows(nm, p):
    return p[0].T if nm in _COL_SHARDED else p[0]


def _step(args, ts_mix_fwd, ts_ffn, ts_mix_bwd, tm_grad):
    (x, g_mix, w_in, b_in, w_dw, b_dw, ln_g, ln_b, w_pool, s_pool, w_out, g_ffn, w_gate, w_up, w_down, g_final,
     loss_target) = args[:17]
    names = ["g_mix", "w_in", "b_in", "w_dw", "b_dw", "ln_g", "ln_b", "w_pool", "s_pool", "w_out", "g_ffn",
             "w_gate", "w_up", "w_down", "g_final"]
    weights = dict(zip(names, args[1:16]))
    moms = dict(zip(names, args[17:32]))
    vars_ = dict(zip(names, args[32:47]))

    S, D = x.shape[1], x.shape[2]
    x2 = x.reshape(S, D)
    tgt2 = loss_target.reshape(S, D)

    shard = lambda nm: _rows(nm, weights[nm])
    w_pool_b = w_pool[0].astype(BF16)
    dw_rows = -(-CONV_WIDTH // SUBLANES) * SUBLANES
    dw_shard = jnp.pad(w_dw[0], ((0, dw_rows - CONV_WIDTH), (0, LANES - w_dw.shape[2])))

    a, gate, v, m, y, h1, xn, w_dw_f, g_in, g_out, _, g_gate, g_up, g_down = _mix_fwd(
        x2, g_mix, b_in, b_dw, ln_g, ln_b, w_pool_b, s_pool, [shard("w_in"), shard("w_out"), dw_shard],
        [shard("w_gate"), shard("w_up"), shard("w_down")], ts_mix_fwd)
    wt_in, w_out_f = g_in.reshape(-1, D), g_out.reshape(-1, D)
    wt_gate, wt_up, w_down_f = g_gate.reshape(-1, D), g_up.reshape(-1, D), g_down.reshape(-1, D)
    Fd = wt_gate.shape[0]
    f_chunks = [1024] * (Fd // 1024) + ([Fd % 1024] if Fd % 1024 else [])
    dh1, hn, act, dgt, dup, dh2, loss_p, dg_final, dg_ffn = _ffn(
        h1, tgt2, g_ffn, g_final.reshape(1, D), wt_gate, wt_up, w_down_f, ts_ffn, f_chunks)

    by_shard = lambda g: g.reshape(N_DEV, -1, D)
    p_gate = by_shard(_grad_matmul(dgt, hn, tm_grad, "grad_w_gate"))
    g_up, s_gate = _grad_matmul_pair_exchange(dup, hn, tm_grad, "grad_w_up", [p_gate])
    p_up = by_shard(g_up)
    g_down, s_up = _grad_matmul_pair_exchange(act, dh2, tm_grad, "grad_w_down", [p_up])
    p_down = by_shard(g_down)
    s_down, = _pair_exchange([p_down])
    pair_sums = _pair_sum([p_gate, p_up, p_down], [s_gate, s_up, s_down])
    (dx, dz, dh1b, dg_mix, db_in, dw_dw, db_dw, dln_g, dln_b, dw_pool, ds_pool, r_gate, r_up, r_down) = _mix_bwd(
        dh1, x2, a, gate, v, m, g_mix, wt_in, w_dw_f, ln_g, ln_b, w_pool_b, s_pool, w_out_f, pair_sums, ts_mix_bwd)

    small_names = ["g_mix", "b_in", "b_dw", "ln_g", "ln_b", "w_pool", "s_pool", "g_ffn", "g_final"]
    small_shape = lambda p: p.reshape(-1, p.shape[-1])
    partial = [dg_mix, db_in, db_dw, dln_g, dln_b, dw_pool.reshape(-1, POOL_GROUP), ds_pool, dg_ffn, dg_final, loss_p]
    dw_out, *summed = _grad_matmul(y, dh1b, tm_grad, "grad_w_out", reduce_vals=partial)
    dwt_in, r_out, r_dw = _grad_matmul(
        dz, xn, tm_grad, "grad_w_in",
        reduce_parts=[dw_out.reshape(N_DEV, -1, D), _shard_cols(dw_dw[0:CONV_WIDTH])])
    r_in, = _tail_reduce([dwt_in.reshape(N_DEV, -1, D)], [])

    big = {}
    for nm, r in zip(_BIG, [r_in, r_out, r_gate, r_up, r_down]):
        rows = r.shape[1]
        res = _adamw_shard(r, _rows(nm, weights[nm]), _rows(nm, moms[nm]), _rows(nm, vars_[nm]),
                           rows // 2 if rows % 32 == 0 else rows, "adamw_" + nm)
        big[nm] = [o.T if nm in _COL_SHARDED else o for o in res]
    big["w_dw"] = _adamw_shard(r_dw, w_dw[0], moms["w_dw"][0], vars_["w_dw"][0], CONV_WIDTH, "adamw_w_dw")

    sm = _adamw_small(summed[:-1], [small_shape(weights[nm]) for nm in small_names],
                      [small_shape(moms[nm]) for nm in small_names], [small_shape(vars_[nm]) for nm in small_names])
    n_small = len(small_names)

    def result(kind, nm):
        if nm in big:
            return big[nm][kind].reshape(weights[nm].shape)
        return sm[kind * n_small + small_names.index(nm)].reshape(weights[nm].shape)

    loss = summed[-1][0, 0]
    out = [loss, dx.reshape(x.shape)]
    for kind in range(4):
        out += [result(kind, nm) for nm in names]
    return tuple(out)


def kernel(x, g_mix, w_in, b_in, w_dw, b_dw, ln_g, ln_b, w_pool, s_pool, w_out, g_ffn, w_gate, w_up, w_down, g_final, loss_target, m_g_mix, m_w_in, m_b_in, m_w_dw, m_b_dw, m_ln_g, m_ln_b, m_w_pool, m_s_pool, m_w_out, m_g_ffn, m_w_gate, m_w_up, m_w_down, m_g_final, v_g_mix, v_w_in, v_b_in, v_w_dw, v_b_dw, v_ln_g, v_ln_b, v_w_pool, v_s_pool, v_w_out, v_g_ffn, v_w_gate, v_w_up, v_w_down, v_g_final):
    args = (x, g_mix, w_in, b_in, w_dw, b_dw, ln_g, ln_b, w_pool, s_pool, w_out, g_ffn, w_gate, w_up, w_down, g_final, loss_target, m_g_mix, m_w_in, m_b_in, m_w_dw, m_b_dw, m_ln_g, m_ln_b, m_w_pool, m_s_pool, m_w_out, m_g_ffn, m_w_gate, m_w_up, m_w_down, m_g_final, v_g_mix, v_w_in, v_b_in, v_w_dw, v_b_dw, v_ln_g, v_ln_b, v_w_pool, v_s_pool, v_w_out, v_g_ffn, v_w_gate, v_w_up, v_w_down, v_g_final)
    return _step(args, ts_mix_fwd=512, ts_ffn=256, ts_mix_bwd=512, tm_grad=256)
```

```python
import jax
import jax.numpy as jnp
from jax import lax
from jax.experimental import pallas as pl
from jax.experimental.pallas import tpu as pltpu

F32 = jnp.float32
BF16 = jnp.bfloat16
MESH = pl.DeviceIdType.MESH
N_DEV = 8

C_CONV = 512
CONV_WIDTH = 31
POOL_WINDOWS = (2, 4, 8, 16)
POOL_GROUP = 128
RMS_EPS = 1e-6
LN_EPS = 1e-5

ADAM_LR = 0.001
ADAM_B1 = 0.9
ADAM_B2 = 0.999
ADAM_EPS = 1e-08
ADAM_WD = 0.01
ADAM_STEP = 10

HALO = 32
SUBLANES = 8
LANES = 128
CONV_ROWS = 64
VMEM_LIMIT = 56 * 1024 * 1024


def _dot(a, b):
    return jnp.dot(a, b, preferred_element_type=F32)


def _dot_nt(a, b):
    return lax.dot_general(a, b, (((1,), (1,)), ((), ())), preferred_element_type=F32)


def _dot_tn(a, b):
    return lax.dot_general(a, b, (((0,), (0,)), ((), ())), preferred_element_type=F32)


def _mean_last(v):
    return jnp.mean(v, axis=-1, keepdims=True)


def _full(shape):
    nd = len(shape)
    return pl.BlockSpec(shape, lambda *_: (0,) * nd)


def _full1(shape):
    nd = len(shape)
    return pl.BlockSpec(shape, lambda *_: (0,) * nd, pipeline_mode=pl.Buffered(1))


def _shifted_copies(sh_ref, rows):
    for s in range(1, SUBLANES):
        sh_ref[s, 0:rows, :] = sh_ref[0, s:s + rows, :]


def _tap(sh_ref, off, r0, rows):
    q, s = divmod(off, SUBLANES)
    return sh_ref[s, pl.ds(r0 + q * SUBLANES, rows), :]


def _window_sums(src, bufs, rows, backward):
    assert all(w == 2 << g for g, w in enumerate(POOL_WINDOWS))
    n = len(POOL_WINDOWS)
    out = []
    for level in range(n):
        dst, shift = bufs[level % 2], 1 << level
        lanes = slice(level * POOL_GROUP, n * POOL_GROUP)
        lo, hi = (SUBLANES * (level + 1), rows) if backward else (0, rows - SUBLANES * (level + 1))
        other = slice(lo - shift, hi - shift) if backward else slice(lo + shift, hi + shift)
        dst[lo:hi, lanes] = src[lo:hi, lanes] + src[other, lanes]
        src = dst
        out.append(dst)
    return out


def _mix_fwd(x, g_mix, b_in, b_dw, ln_g, ln_b, w_pool, s_pool, mix_shards, ag_shards, ts):
    S, D = x.shape
    d_in = mix_shards[0].shape[0] * N_DEV
    C = C_CONV
    nt = S // ts
    nrb = ts // CONV_ROWS
    n_ag = len(ag_shards)
    relay_step = nt // 2
    mix_dtypes = [BF16, BF16, F32]

    def body(x_ref, g_ref, bin_ref, bdw_ref, lng_ref, lnb_ref, wp_ref, sp_ref, *rest):
        mx_in, ag_in, rest = rest[:3], rest[3:3 + n_ag], rest[3 + n_ag:]
        a_ref, gate_ref, v_ref, m_ref, y_ref, h1_ref, xn_ref, wdw_ref = rest[:8]
        mx_out, ag_out, rest = rest[8:11], rest[11:11 + n_ag], rest[11 + n_ag:]
        ush, pbuf, pa, pb, win_ref, wout_ref, gdw, load_sems = rest[:8]
        rest = rest[8:]
        (ma_stage, ma_bufs, ma_sems), rest = (rest[:2], rest[2:4], rest[4:7]), rest[7:]
        (mb_stage, mb_bufs, mb_sems), rest = (rest[:1], rest[1:2], rest[2:5]), rest[5:]
        ag_stage, ag_bufs, ag_sems = rest[:n_ag], rest[n_ag:2 * n_ag], rest[2 * n_ag:]
        i = pl.program_id(0)
        ma_start, ma_relay, ma_pass_on, ma_finish = _gather_plan(
            mx_in[0::2], mx_out[0::2], ma_stage, ma_bufs, *ma_sems)
        mb_start, mb_relay, mb_pass_on, mb_finish = _gather_plan(
            mx_in[1:2], mx_out[1:2], mb_stage, mb_bufs, *mb_sems)
        ag_start, ag_relay, ag_pass_on, ag_finish = _gather_plan(ag_in, ag_out, ag_stage, ag_bufs, *ag_sems)

        def load_weight(k, dst_ref):
            rows = mx_out[k].shape[1]
            loads = [pltpu.make_async_copy(mx_out[k].at[j], dst_ref.at[pl.ds(j * rows, rows), :],
                                           load_sems.at[N_DEV * k + j]) for j in range(N_DEV)]
            for cp in loads:
                cp.start()
            for cp in loads:
                cp.wait()

        @pl.when(i == 0)
        def _():
            ma_start()
            mb_start()
            ma_relay()
            mb_relay()
            ag_start()
            ma_pass_on()
            ma_finish()
            load_weight(0, win_ref)
            dw_load = pltpu.make_async_copy(mx_out[2], gdw, load_sems.at[2 * N_DEV])
            dw_load.start()
            dw_load.wait()
            first_half = lax.broadcasted_iota(jnp.int32, gdw.shape[1:], 1) < C // N_DEV
            for p in range(N_DEV // 2):
                wdw_ref[:, LANES * p:LANES * (p + 1)] = jnp.where(
                    first_half, gdw[2 * p], pltpu.roll(gdw[2 * p + 1], C // N_DEV, axis=1))
            ush[0, 0:HALO, :] = jnp.zeros((HALO, C), F32)
            pbuf[0:HALO, :] = jnp.zeros((HALO, C), F32)

        @pl.when(i == relay_step)
        def _():
            ag_relay()

        @pl.when(i == nt - 1)
        def _():
            ag_pass_on()

        x = x_ref[...]
        r1 = lax.rsqrt(_mean_last(x * x) + RMS_EPS)
        xn = (x * r1 * g_ref[...]).astype(BF16)
        xn_ref[...] = xn
        z = _dot_nt(xn, win_ref[...]) + bin_ref[...]
        a = z[:, 0:C]
        gate = z[:, C:2 * C]
        a_ref[...] = a
        gate_ref[...] = gate
        ush[0, HALO:HALO + ts, :] = a * jax.nn.sigmoid(gate)
        pbuf[HALO:HALO + ts, :] = z[:, 2 * C:]

        _shifted_copies(ush, ts + HALO - SUBLANES)

        def conv_block(rb, carry):
            r0 = pl.multiple_of(rb * CONV_ROWS, CONV_ROWS)
            acc = jnp.zeros((CONV_ROWS, C), F32)
            for k in range(CONV_WIDTH):
                acc = acc + wdw_ref[k:k + 1, :] * _tap(ush, HALO - (CONV_WIDTH - 1) + k, r0, CONV_ROWS)
            v_ref[pl.ds(r0, CONV_ROWS), :] = acc + bdw_ref[...]
            return carry

        lax.fori_loop(0, nrb, conv_block, 0)

        v = v_ref[...]
        mu = _mean_last(v)
        xc = v - mu
        rstd = lax.rsqrt(_mean_last(xc * xc) + LN_EPS)
        ln = xc * rstd * lng_ref[...] + lnb_ref[...]
        y_ref[:, 0:C] = (ln * jax.nn.sigmoid(ln)).astype(BF16)

        sums = _window_sums(pbuf, (pa, pb), ts + HALO, backward=True)
        row = lax.broadcasted_iota(jnp.int32, (ts, 1), 0) + i * ts
        for gi, w in enumerate(POOL_WINDOWS):
            lanes = slice(gi * POOL_GROUP, (gi + 1) * POOL_GROUP)
            seg = pbuf[HALO:HALO + ts, lanes]
            ws = sums[gi][HALO:HALO + ts, lanes]
            cnt = jnp.minimum(row + 1, w).astype(F32)
            m = (ws / cnt - seg).astype(BF16)
            m_ref[:, lanes] = m
            ypre = _dot(m, wp_ref[gi])
            y_ref[:, C + gi * POOL_GROUP:C + (gi + 1) * POOL_GROUP] = (ypre * sp_ref[:, lanes]).astype(BF16)

        @pl.when(i == 0)
        def _():
            mb_pass_on()
            mb_finish()
            load_weight(1, wout_ref)

        h1_ref[...] = x + _dot(y_ref[...], wout_ref[...])

        ush[0, 0:HALO, :] = ush[0, ts:ts + HALO, :]
        pbuf[0:HALO, :] = pbuf[ts:ts + HALO, :]

        @pl.when(i == nt - 1)
        def _():
            ag_finish()

    tile = lambda w, dt: (pl.BlockSpec((ts, w), lambda i: (i, 0)), jax.ShapeDtypeStruct((S, w), dt))
    wdw_rows = mix_shards[2].shape[0]
    outs = [tile(C, F32), tile(C, F32), tile(C, F32), tile(C, BF16), tile(D, BF16), tile(D, F32), tile(D, BF16),
            (_full((wdw_rows, C)), jax.ShapeDtypeStruct((wdw_rows, C), F32))]
    return pl.pallas_call(
        body, name="mix_fwd", grid=(nt,),
        in_specs=[pl.BlockSpec((ts, D), lambda i: (i, 0)), _full((1, D)), _full((1, d_in)), _full((1, C)),
                  _full((1, C)), _full((1, C)), _full(w_pool.shape), _full((1, C))] + [_HBM] * (3 + n_ag),
        out_specs=[o[0] for o in outs] + [_HBM] * (3 + n_ag),
        out_shape=[o[1] for o in outs] + _gather_out_shapes(mix_shards, mix_dtypes)
        + _gather_out_shapes(ag_shards, [BF16] * n_ag),
        scratch_shapes=[pltpu.VMEM((SUBLANES, ts + HALO, C), F32)] + [pltpu.VMEM((ts + HALO, C), F32)] * 3
        + [pltpu.VMEM((d_in, D), BF16), pltpu.VMEM((D, D), BF16),
                        pltpu.VMEM((N_DEV,) + mix_shards[2].shape, F32), pltpu.SemaphoreType.DMA((2 * N_DEV + 1,))]
        + _gather_scratch(mix_shards[0::2], mix_dtypes[0::2]) + _gather_scratch(mix_shards[1:2], mix_dtypes[1:2])
        + _gather_scratch(ag_shards, [BF16] * n_ag),
        compiler_params=pltpu.CompilerParams(dimension_semantics=("arbitrary",), vmem_limit_bytes=VMEM_LIMIT),
    )(x, g_mix, b_in, b_dw, ln_g, ln_b, w_pool, s_pool, *mix_shards, *ag_shards)


def _ffn(h1, target, g_ffn, g_final, w_gate, w_up, w_down, ts, f_chunks):
    S, D = h1.shape
    Fd = w_gate.shape[0]
    nt = S // ts
    bounds = []
    lo = 0
    for n in f_chunks:
        bounds.append((lo, lo + n))
        lo += n
    assert lo == Fd

    def body(h1_ref, tgt_ref, gf_ref, gl_ref, wg_ref, wu_ref, wd_ref,
             dh1_ref, hn_ref, act_ref, dgt_ref, dup_ref, dh2_ref, loss_ref, dgl_ref, dgf_ref, gt_s, up_s):
        i = pl.program_id(0)

        @pl.when(i == 0)
        def _():
            loss_ref[...] = jnp.zeros_like(loss_ref)
            dgl_ref[...] = jnp.zeros_like(dgl_ref)
            dgf_ref[...] = jnp.zeros_like(dgf_ref)

        h1 = h1_ref[...]
        r2 = lax.rsqrt(_mean_last(h1 * h1) + RMS_EPS)
        hhat = h1 * r2
        hn = (hhat * gf_ref[...]).astype(BF16)
        hn_ref[...] = hn
        h2 = h1
        for lo, hi in bounds:
            gt = _dot_nt(hn, wg_ref[lo:hi, :])
            up = _dot_nt(hn, wu_ref[lo:hi, :])
            gt_s[:, lo:hi] = gt
            up_s[:, lo:hi] = up
            act = (gt * jax.nn.sigmoid(gt) * up).astype(BF16)
            act_ref[:, lo:hi] = act
            h2 = h2 + _dot(act, wd_ref[lo:hi, :])

        r3 = lax.rsqrt(_mean_last(h2 * h2) + RMS_EPS)
        n3 = h2 * r3
        gl = gl_ref[...]
        diff = n3 * gl - tgt_ref[...]
        loss_ref[...] += jnp.sum(0.5 * jnp.sum(diff * diff, axis=-1, keepdims=True) / D, axis=0, keepdims=True)
        dout = diff / D
        dgl_ref[...] += jnp.sum(dout * n3, axis=0, keepdims=True)
        dn = dout * gl
        dh2 = r3 * (dn - n3 * _mean_last(dn * n3))
        dh2b = dh2.astype(BF16)
        dh2_ref[...] = dh2b

        dhn = jnp.zeros((ts, D), F32)
        for lo, hi in bounds:
            gt = gt_s[:, lo:hi]
            up = up_s[:, lo:hi]
            sg = jax.nn.sigmoid(gt)
            dact = _dot_nt(dh2b, wd_ref[lo:hi, :])
            dgt = (dact * up * (sg * (1.0 + gt * (1.0 - sg)))).astype(BF16)
            dup = (dact * (gt * sg)).astype(BF16)
            dgt_ref[:, lo:hi] = dgt
            dup_ref[:, lo:hi] = dup
            dhn = dhn + _dot(dgt, wg_ref[lo:hi, :]) + _dot(dup, wu_ref[lo:hi, :])

        dgf_ref[...] += jnp.sum(dhn * hhat, axis=0, keepdims=True)
        dnn = dhn * gf_ref[...]
        dh1_ref[...] = dh2 + r2 * (dnn - hhat * _mean_last(dnn * hhat))

    tile = lambda w, dt: (pl.BlockSpec((ts, w), lambda i: (i, 0)), jax.ShapeDtypeStruct((S, w), dt))
    acc = lambda w: (_full((1, w)), jax.ShapeDtypeStruct((1, w), F32))
    outs = [tile(D, F32), tile(D, BF16), tile(Fd, BF16), tile(Fd, BF16), tile(Fd, BF16), tile(D, BF16),
            acc(LANES), acc(D), acc(D)]
    return pl.pallas_call(
        body, name="ffn_fwd_bwd", grid=(nt,),
        in_specs=[pl.BlockSpec((ts, D), lambda i: (i, 0)), pl.BlockSpec((ts, D), lambda i: (i, 0)),
                  _full((1, D)), _full((1, D)), _full1((Fd, D)), _full1((Fd, D)), _full1((Fd, D))],
        out_specs=[o[0] for o in outs], out_shape=[o[1] for o in outs],
        scratch_shapes=[pltpu.VMEM((ts, Fd), F32), pltpu.VMEM((ts, Fd), F32)],
        compiler_params=pltpu.CompilerParams(dimension_semantics=("arbitrary",), vmem_limit_bytes=VMEM_LIMIT),
    )(h1, target, g_ffn, g_final, w_gate, w_up, w_down)


def _mix_bwd(dh1, x, a, gate, v, m, g_mix, w_in, w_dw, ln_g, ln_b, w_pool, s_pool, w_out, rs_parts, ts):
    S, D = x.shape
    n_rs = len(rs_parts)
    d_in = w_in.shape[0]
    C = C_CONV
    nt = S // ts
    nrb = ts // CONV_ROWS
    wrows =((CONV_WIDTH + SUBLANES - 1) // SUBLANES) * SUBLANES

    def body(dh1_ref, x_ref, a_ref, gate_ref, v_ref, m_ref, g_ref, win_ref, wdw_ref, lng_ref,
             lnb_ref, wp_ref, sp_ref, wout_ref, *rest):
        rs_in, rest = rest[:n_rs], rest[n_rs:]
        (dx_ref, dz_ref, dh1b_ref, dgm_ref, dbin_ref, dwdw_ref, dbdw_ref, dlng_ref, dlnb_ref, dwp_ref,
         dsp_ref) = rest[:11]
        rs_out, rest = rest[11:11 + n_rs], rest[11 + n_rs:]
        dvsh, dqbuf, qa, qb, du_s, dm_s = rest[:6]
        rs_bufs, (send_sems, recv_sems, local_sems) = rest[6:6 + n_rs], rest[6 + n_rs:]
        i = pl.program_id(0)
        t = nt - 1 - i
        rs_start, rs_finish = _chip_exchange_plan(rs_in, rs_out, rs_bufs, send_sems, recv_sems, local_sems)

        @pl.when(i == 0)
        def _():
            rs_start()
            dvsh[0, ts:ts + HALO, :] = jnp.zeros((HALO, C), F32)
            dqbuf[ts:ts + HALO, :] = jnp.zeros((HALO, C), F32)
            for r in (dgm_ref, dbin_ref, dwdw_ref, dbdw_ref, dlng_ref, dlnb_ref, dwp_ref, dsp_ref):
                r[...] = jnp.zeros_like(r)

        dh1 = dh1_ref[...]
        dh1b = dh1.astype(BF16)
        dh1b_ref[...] = dh1b
        dy = _dot_nt(dh1b, wout_ref[...])

        v = v_ref[...]
        mu = _mean_last(v)
        xc = v - mu
        rstd = lax.rsqrt(_mean_last(xc * xc) + LN_EPS)
        vhat = xc * rstd
        lng = lng_ref[...]
        ln = vhat * lng + lnb_ref[...]
        sg = jax.nn.sigmoid(ln)
        dln = dy[:, 0:C] * (sg * (1.0 + ln * (1.0 - sg)))
        dlng_ref[...] += jnp.sum(dln * vhat, axis=0, keepdims=True)
        dlnb_ref[...] += jnp.sum(dln, axis=0, keepdims=True)
        dvh = dln * lng
        dv = rstd * (dvh - _mean_last(dvh) - vhat * _mean_last(dvh * vhat))
        dbdw_ref[...] += jnp.sum(dv, axis=0, keepdims=True)
        dvsh[0, 0:ts, :] = dv
        _shifted_copies(dvsh, ts + HALO - SUBLANES)

        def conv_block(rb, carry):
            r0 = pl.multiple_of(rb * CONV_ROWS, CONV_ROWS)
            acc = jnp.zeros((CONV_ROWS, C), F32)
            for k in range(CONV_WIDTH):
                acc = acc + wdw_ref[k:k + 1, :] * _tap(dvsh, CONV_WIDTH - 1 - k, r0, CONV_ROWS)
            du_s[pl.ds(r0, CONV_ROWS), :] = acc
            return carry

        lax.fori_loop(0, nrb, conv_block, 0)

        a = a_ref[...]
        sgate = jax.nn.sigmoid(gate_ref[...])
        u = a * sgate
        for k in range(CONV_WIDTH):
            q, s = divmod(CONV_WIDTH - 1 - k, SUBLANES)
            prod = u * dvsh[s, q * SUBLANES:q * SUBLANES + ts, :]
            dwdw_ref[k:k + 1, :] += jnp.sum(prod, axis=0, keepdims=True)

        du = du_s[...]
        da = du * sgate
        dgate = du * a * sgate * (1.0 - sgate)
        dz_ref[:, 0:C] = da.astype(BF16)
        dz_ref[:, C:2 * C] = dgate.astype(BF16)
        dbin_ref[:, 0:C] += jnp.sum(da, axis=0, keepdims=True)
        dbin_ref[:, C:2 * C] += jnp.sum(dgate, axis=0, keepdims=True)

        row = lax.broadcasted_iota(jnp.int32, (ts, 1), 0) + t * ts
        for gi, w in enumerate(POOL_WINDOWS):
            lanes = slice(gi * POOL_GROUP, (gi + 1) * POOL_GROUP)
            dyp = dy[:, C + gi * POOL_GROUP:C + (gi + 1) * POOL_GROUP]
            mg = m_ref[:, lanes]
            ypre = _dot(mg, wp_ref[gi])
            dsp_ref[:, lanes] += jnp.sum(dyp * ypre, axis=0, keepdims=True)
            dyi = (dyp * sp_ref[:, lanes]).astype(BF16)
            dwp_ref[gi] += _dot_tn(mg, dyi)
            dm = _dot_nt(dyi, wp_ref[gi])
            cnt = jnp.minimum(row + 1, w).astype(F32)
            dqbuf[0:ts, lanes] = dm / cnt
            dm_s[:, lanes] = dm
        sums = _window_sums(dqbuf, (qa, qb), ts + HALO, backward=False)
        for gi in range(len(POOL_WINDOWS)):
            lanes = slice(gi * POOL_GROUP, (gi + 1) * POOL_GROUP)
            dp = sums[gi][0:ts, lanes] - dm_s[:, lanes]
            dz_ref[:, 2 * C + gi * POOL_GROUP:2 * C + (gi + 1) * POOL_GROUP] = dp.astype(BF16)
            dbin_ref[:, 2 * C + gi * POOL_GROUP:2 * C + (gi + 1) * POOL_GROUP] += jnp.sum(dp, axis=0, keepdims=True)

        dxn = _dot(dz_ref[...], win_ref[...])
        x = x_ref[...]
        r1 = lax.rsqrt(_mean_last(x * x) + RMS_EPS)
        xhat = x * r1
        dgm_ref[...] += jnp.sum(dxn * xhat, axis=0, keepdims=True)
        dnn = dxn * g_ref[...]
        dx_ref[...] = dh1 + r1 * (dnn - xhat * _mean_last(dnn * xhat))

        dvsh[0, ts:ts + HALO, :] = dvsh[0, 0:HALO, :]
        dqbuf[ts:ts + HALO, :] = dqbuf[0:HALO, :]

        @pl.when(i == nt - 1)
        def _():
            rs_finish()

    rev = lambda w: pl.BlockSpec((ts, w), lambda i: (nt - 1 - i, 0))
    acc = lambda shape: (_full(shape), jax.ShapeDtypeStruct(shape, F32))
    outs = [(rev(D), jax.ShapeDtypeStruct((S, D), F32)), (rev(d_in), jax.ShapeDtypeStruct((S, d_in), BF16)),
            (rev(D), jax.ShapeDtypeStruct((S, D), BF16)), acc((1, D)), acc((1, d_in)), acc((wrows, C)),
            acc((1, C)), acc((1, C)), acc((1, C)), acc(w_pool.shape), acc((1, C))]
    return pl.pallas_call(
        body, name="mix_bwd", grid=(nt,),
        in_specs=[rev(D), rev(D), rev(C), rev(C), rev(C), rev(C), _full((1, D)), _full((d_in, D)),
                  _full(w_dw.shape), _full((1, C)), _full((1, C)), _full(w_pool.shape), _full((1, C)),
                  _full((D, D))] + [_HBM] * n_rs,
        out_specs=[o[0] for o in outs] + [_HBM] * n_rs,
        out_shape=[o[1] for o in outs] + [jax.ShapeDtypeStruct(p.shape, p.dtype) for p in rs_parts],
        scratch_shapes=[pltpu.VMEM((SUBLANES, ts + HALO, C), F32)] + [pltpu.VMEM((ts + HALO, C), F32)] * 3
        + [pltpu.VMEM((ts, C), F32)] * 2
        + [pltpu.VMEM(p.shape[1:], p.dtype) for p in rs_parts] + _comm_sems(n_rs, 3),
        compiler_params=pltpu.CompilerParams(dimension_semantics=("arbitrary",), vmem_limit_bytes=VMEM_LIMIT),
    )(dh1, x, a, gate, v, m, g_mix, w_in, w_dw, ln_g, ln_b, w_pool, s_pool, w_out, *rs_parts)


def _grad_matmul(a, b, tm, name, reduce_parts=(), reduce_vals=()):
    S, M = a.shape
    N = b.shape[1]
    steps = M // tm
    n, ns = len(reduce_parts), len(reduce_vals)

    def body(a_ref, b_ref, *rest):
        o_ref = rest[n + ns]
        if n + ns:
            start, middle, finish = _reduce_plan(rest[:n], rest[n:n + ns], rest[n + ns + 1:2 * n + ns + 1],
                                                 rest[2 * n + ns + 1:2 * (n + ns) + 1], rest[2 * (n + ns) + 1:])
            i = pl.program_id(0)
            pl.when(i == 0)(start)
            pl.when(i == 1)(middle)
        o_ref[...] = _dot_tn(a_ref[...], b_ref[...]).astype(BF16)
        if n + ns:
            pl.when(i == steps - 1)(finish)

    res = pl.pallas_call(
        body, name=name, grid=(steps,),
        in_specs=[pl.BlockSpec((S, tm), lambda i: (0, i)), _full1((S, N))] + [_VMEM] * (n + ns),
        out_specs=[pl.BlockSpec((tm, N), lambda i: (i, 0))] + [_VMEM] * (n + ns),
        out_shape=[jax.ShapeDtypeStruct((M, N), BF16)] + _reduce_out_shapes(reduce_parts, reduce_vals),
        scratch_shapes=_reduce_scratch(reduce_parts, reduce_vals) if n + ns else [],
        compiler_params=pltpu.CompilerParams(dimension_semantics=("arbitrary",) if n + ns else ("parallel",),
                                             vmem_limit_bytes=VMEM_LIMIT),
    )(a, b, *reduce_parts, *reduce_vals)
    return res if n + ns else res[0]


def _grad_matmul_pair_exchange(a, b, tm, name, parts):
    S, M = a.shape
    N = b.shape[1]
    steps = M // tm
    n = len(parts)

    def body(a_ref, b_ref, *rest):
        o_ref = rest[n]
        start, finish = _pair_exchange_plan(rest[:n], rest[n + 1:2 * n + 1], *rest[2 * n + 1:])
        i = pl.program_id(0)
        pl.when(i == 0)(start)
        o_ref[...] = _dot_tn(a_ref[...], b_ref[...]).astype(BF16)
        pl.when(i == steps - 1)(finish)

    sems = pltpu.SemaphoreType.DMA((n, N_CHIPS))
    return pl.pallas_call(
        body, name=name, grid=(steps,),
        in_specs=[pl.BlockSpec((S, tm), lambda i: (0, i)), _full1((S, N))] + [_HBM] * n,
        out_specs=[pl.BlockSpec((tm, N), lambda i: (i, 0))] + [_HBM] * n,
        out_shape=[jax.ShapeDtypeStruct((M, N), BF16)] + _pair_out_shapes(parts),
        scratch_shapes=[sems, sems],
        compiler_params=pltpu.CompilerParams(dimension_semantics=("arbitrary",), vmem_limit_bytes=VMEM_LIMIT),
    )(a, b, *parts)


def _position():
    return lax.axis_index("x"), lax.axis_index("y"), lax.axis_index("c")


def _slot(px, py, pc):
    return 4 * px + 2 * py + pc


_HBM = pl.BlockSpec(memory_space=pl.ANY)


def _comm_sems(n, copies):
    return [pltpu.SemaphoreType.DMA((n, copies)), pltpu.SemaphoreType.DMA((n, copies)),
            pltpu.SemaphoreType.DMA((n, 2))]


def _local_copy(srcs, dsts, bufs, local_sems):
    n = len(srcs)
    loads = [pltpu.make_async_copy(srcs[k], bufs[k], local_sems.at[k, 0]) for k in range(n)]
    for cp in loads:
        cp.start()
    for cp in loads:
        cp.wait()
    stores = _local_stores(dsts, bufs, local_sems)
    for cp in stores:
        cp.start()
    return stores


def _local_stores(dsts, bufs, local_sems):
    return [pltpu.make_async_copy(bufs[k], dsts[k], local_sems.at[k, 1]) for k in range(len(dsts))]


def _gather_out_shapes(shards, dtypes):
    return [jax.ShapeDtypeStruct((N_DEV,) + s.shape, dt) for s, dt in zip(shards, dtypes)]


def _gather_scratch(shards, dtypes):
    return ([pltpu.VMEM(s.shape, s.dtype) for s in shards] + [pltpu.VMEM(s.shape, dt) for s, dt in zip(shards, dtypes)]
            + _comm_sems(len(shards), 7))


def _gather_plan(ins, outs, stage, bufs, send_sems, recv_sems, local_sems):
    n = len(ins)
    x, y, c = _position()
    me, sibling = (x, y, c), (x, y, 1 - c)
    na, nb, dg = (x ^ (1 - c), y ^ c), (x ^ c, y ^ (1 - c)), (1 - x, 1 - y)
    own = [outs[k].at[_slot(*me)] for k in range(n)]

    def copy(k, sem, block, to, src=None):
        dst = outs[k].at[_slot(*block)]
        return pltpu.make_async_remote_copy(
            src_ref=dst if src is None else src, dst_ref=dst, send_sem=send_sems.at[k, sem],
            recv_sem=recv_sems.at[k, sem], device_id=to, device_id_type=MESH)

    def first():
        cps = []
        for k in range(n):
            cps += [copy(k, 0, me, sibling, src=bufs[k]), copy(k, 1, me, (*na, c), src=bufs[k]),
                    copy(k, 2, me, (*nb, c), src=bufs[k])]
        return cps

    def onward():
        return [copy(k, 3, (*na, c), (*nb, c)) for k in range(n)]

    def to_sibling(j, chip):
        return [copy(k, 4 + j, (*chip, c), sibling) for k in range(n)]

    def start():
        loads = [pltpu.make_async_copy(ins[k], stage[k], local_sems.at[k, 0]) for k in range(n)]
        for cp in loads:
            cp.start()
        for cp in loads:
            cp.wait()
        for k in range(n):
            bufs[k][...] = stage[k][...].astype(bufs[k].dtype)
        for cp in first() + _local_stores(own, bufs, local_sems):
            cp.start()

    def relay():
        for k in range(n):
            copy(k, 1, (*na, c), me).wait_recv()
        for cp in onward() + to_sibling(0, na):
            cp.start()
        for k in range(n):
            copy(k, 2, (*nb, c), me).wait_recv()
        for cp in to_sibling(1, nb):
            cp.start()

    def pass_on():
        for k in range(n):
            copy(k, 3, (*dg, c), me).wait_recv()
        for cp in to_sibling(2, dg):
            cp.start()

    def finish():
        for k in range(n):
            copy(k, 0, sibling, me).wait_recv()
            for j, chip in enumerate((nb, na, dg)):
                copy(k, 4 + j, (*chip, 1 - c), me).wait_recv()
        for cp in first() + onward() + to_sibling(0, na) + to_sibling(1, nb) + to_sibling(2, dg):
            cp.wait_send()
        for cp in _local_stores(own, bufs, local_sems):
            cp.wait()

    return start, relay, pass_on, finish


N_CHIPS = 4
_CHIP_FLIPS = [(1, 0), (0, 1), (1, 1)]
F32_TRAVEL_LIMIT = 4096


def _chip_exchange_plan(ins, outs, bufs, send_sems, recv_sems, local_sems):
    n = len(ins)
    x, y, c = _position()
    my_q = 2 * x + y
    peers = [(x ^ fx, y ^ fy) for fx, fy in _CHIP_FLIPS]
    own = [outs[k].at[my_q] for k in range(n)]

    def sends():
        return [pltpu.make_async_remote_copy(
            src_ref=ins[k].at[2 * px + py], dst_ref=outs[k].at[my_q], send_sem=send_sems.at[k, f],
            recv_sem=recv_sems.at[k, f], device_id=(px, py, c), device_id_type=MESH)
            for f, (px, py) in enumerate(peers) for k in range(n)]

    def start():
        for cp in sends():
            cp.start()
        _local_copy([ins[k].at[my_q] for k in range(n)], own, bufs, local_sems)

    def finish():
        for f, (px, py) in enumerate(peers):
            for k in range(n):
                pltpu.make_async_remote_copy(
                    src_ref=ins[k].at[my_q], dst_ref=outs[k].at[2 * px + py], send_sem=send_sems.at[k, f],
                    recv_sem=recv_sems.at[k, f], device_id=(px, py, c), device_id_type=MESH).wait_recv()
        for cp in sends():
            cp.wait_send()
        for cp in _local_stores(own, bufs, local_sems):
            cp.wait()

    return start, finish


def _pair_exchange(parts):
    n = len(parts)

    def body(*refs):
        start, finish = _pair_exchange_plan(refs[:n], refs[n:2 * n], *refs[2 * n:])
        start()
        finish()

    sems = pltpu.SemaphoreType.DMA((n, N_CHIPS))
    return pl.pallas_call(
        body, name="pair_exchange", out_shape=_pair_out_shapes(parts),
        in_specs=[_HBM] * n, out_specs=[_HBM] * n, scratch_shapes=[sems, sems],
    )(*parts)


def _pair_out_shapes(parts):
    return [jax.ShapeDtypeStruct((N_CHIPS,) + p.shape[1:], p.dtype) for p in parts]


def _pair_exchange_plan(ins, sib, send_sems, recv_sems):
    x, y, c = _position()

    def copies():
        return [pltpu.make_async_remote_copy(
            src_ref=ins[k].at[2 * q + 1 - c], dst_ref=sib[k].at[q], send_sem=send_sems.at[k, q],
            recv_sem=recv_sems.at[k, q], device_id=(x, y, 1 - c), device_id_type=MESH)
            for k in range(len(ins)) for q in range(N_CHIPS)]

    def start():
        for cp in copies():
            cp.start()

    def finish():
        for cp in copies():
            cp.wait()

    return start, finish


def _pair_sum(parts, sib):
    n = len(parts)

    def body(*refs):
        c = lax.axis_index("c")
        for k in range(n):
            refs[2 * n + k][0] = (refs[k][0, c].astype(F32) + refs[n + k][0].astype(F32)).astype(BF16)

    pair = [pl.BlockSpec((1, 2) + p.shape[1:], lambda q: (q, 0, 0, 0)) for p in parts]
    one = [pl.BlockSpec((1,) + p.shape[1:], lambda q: (q, 0, 0)) for p in parts]
    return pl.pallas_call(
        body, name="pair_sum", grid=(N_CHIPS,), in_specs=pair + one, out_specs=one,
        out_shape=[jax.ShapeDtypeStruct(s.shape, BF16) for s in sib],
        compiler_params=pltpu.CompilerParams(dimension_semantics=("parallel",)),
    )(*[p.reshape((N_CHIPS, 2) + p.shape[1:]) for p in parts], *sib)


def _tail_reduce(parts, vals):
    n, ns = len(parts), len(vals)

    def body(*refs):
        start, middle, finish = _reduce_plan(refs[:n], refs[n:n + ns], refs[n + ns:2 * n + ns],
                                             refs[2 * n + ns:2 * (n + ns)], refs[2 * (n + ns):])
        start()
        middle()
        finish()

    return pl.pallas_call(
        body, name="tail_reduce", out_shape=_reduce_out_shapes(parts, vals),
        in_specs=[_VMEM] * (n + ns), out_specs=[_VMEM] * (n + ns), scratch_shapes=_reduce_scratch(parts, vals),
        compiler_params=pltpu.CompilerParams(vmem_limit_bytes=VMEM_LIMIT),
    )(*parts, *vals)


_VMEM = pl.BlockSpec(memory_space=pltpu.VMEM)


def _reduce_out_shapes(parts, vals):
    return ([jax.ShapeDtypeStruct((N_CHIPS,) + p.shape[1:], p.dtype) for p in parts]
            + [jax.ShapeDtypeStruct(v.shape, v.dtype) for v in vals])


def _reduce_scratch(parts, vals):
    n, ns = len(parts), len(vals)
    quarter = [pltpu.VMEM((N_CHIPS,) + p.shape[1:], p.dtype) for p in parts]
    dma = pltpu.SemaphoreType.DMA
    travel = [BF16 if v.size > F32_TRAVEL_LIMIT else v.dtype for v in vals]
    return (quarter * 3 + [pltpu.VMEM(v.shape, v.dtype) for v in vals]
            + [pltpu.VMEM(v.shape, t) for v, t in zip(vals, travel)]
            + [pltpu.VMEM((N_CHIPS,) + v.shape, t) for v, t in zip(vals, travel)]
            + [dma((max(n, 1), N_CHIPS)), dma((max(n, 1), N_CHIPS)), dma((max(ns, 1),)), dma((max(ns, 1),)),
               dma((max(n, 1), 3)), dma((max(n, 1), 3)), dma((max(ns, 1), 3)), dma((max(ns, 1), 3))])


def _reduce_plan(p_in, v_in, p_out, v_out, scratch):
    n, ns = len(p_in), len(v_in)
    p_sib, p_sum, p_all = scratch[:n], scratch[n:2 * n], scratch[2 * n:3 * n]
    v_sib, v_sum, v_all = (scratch[3 * n + j * ns:3 * n + (j + 1) * ns] for j in range(3))
    p1_send, p1_recv, v1_send, v1_recv, p3_send, p3_recv, v3_send, v3_recv = scratch[3 * n + 3 * ns:]
    x, y, c = _position()
    my_q = 2 * x + y
    peers = [(x ^ fx, y ^ fy) for fx, fy in _CHIP_FLIPS]

    def to_sibling(src, dst, send, recv):
        return pltpu.make_async_remote_copy(src_ref=src, dst_ref=dst, send_sem=send, recv_sem=recv,
                                            device_id=(x, y, 1 - c), device_id_type=MESH)

    def level1():
        cps = [to_sibling(p_in[k].at[2 * q + 1 - c], p_sib[k].at[q], p1_send.at[k, q], p1_recv.at[k, q])
               for k in range(n) for q in range(N_CHIPS)]
        return cps + [to_sibling(v_in[k], v_sib[k], v1_send.at[k], v1_recv.at[k]) for k in range(ns)]

    def to_chip(f, src, dst, send, recv):
        px, py = peers[f]
        return pltpu.make_async_remote_copy(src_ref=src, dst_ref=dst, send_sem=send, recv_sem=recv,
                                            device_id=(px, py, c), device_id_type=MESH)

    def level2(sending):
        cps = []
        for f, (px, py) in enumerate(peers):
            their_q = 2 * px + py
            for k in range(n):
                src, dst = (p_sum[k].at[their_q], p_all[k].at[my_q]) if sending else (
                    p_sum[k].at[my_q], p_all[k].at[their_q])
                cps.append(to_chip(f, src, dst, p3_send.at[k, f], p3_recv.at[k, f]))
            for k in range(ns):
                dst = v_all[k].at[my_q] if sending else v_all[k].at[their_q]
                cps.append(to_chip(f, v_sum[k], dst, v3_send.at[k, f], v3_recv.at[k, f]))
        return cps

    def start():
        for cp in level1():
            cp.start()

    def middle():
        for cp in level1():
            cp.wait_recv()
        for k in range(n):
            for q in range(N_CHIPS):
                p_sum[k][q] = (p_in[k][2 * q + c].astype(F32) + p_sib[k][q].astype(F32)).astype(p_sum[k].dtype)
        for k in range(ns):
            v_sum[k][...] = (v_in[k][...] + v_sib[k][...]).astype(v_sum[k].dtype)
        for cp in level2(True):
            cp.start()
        for k in range(n):
            p_all[k][my_q] = p_sum[k][my_q]
        for k in range(ns):
            v_all[k][my_q] = v_sum[k][...]

    def finish():
        for cp in level2(False):
            cp.wait_recv()
        for k in range(n):
            p_out[k][...] = p_all[k][...]
        for k in range(ns):
            total = v_all[k][0].astype(F32)
            for q in range(1, N_CHIPS):
                total = total + v_all[k][q].astype(F32)
            v_out[k][...] = total
        for cp in level1() + level2(True):
            cp.wait_send()

    return start, middle, finish


def _adamw_math(w, g, m, v):
    m = ADAM_B1 * m + (1.0 - ADAM_B1) * g
    v = ADAM_B2 * v + (1.0 - ADAM_B2) * (g * g)
    m_hat = m / (1.0 - ADAM_B1 ** ADAM_STEP)
    v_hat = v / (1.0 - ADAM_B2 ** ADAM_STEP)
    delta = -ADAM_LR * (m_hat / (jnp.sqrt(v_hat) + ADAM_EPS) + ADAM_WD * w)
    return delta, m, v


def _adamw_shard(parts, w, m, v, tr, name):
    R, Cc = w.shape

    def body(p_ref, w_ref, m_ref, v_ref, g_out, d_out, m_out, v_out):
        g = p_ref[0].astype(F32)
        for j in range(1, N_CHIPS):
            g = g + p_ref[j].astype(F32)
        d, mn, vn = _adamw_math(w_ref[...], g, m_ref[...], v_ref[...])
        g_out[...] = g
        d_out[...] = d
        m_out[...] = mn
        v_out[...] = vn

    blk = pl.BlockSpec((tr, Cc), lambda i: (i, 0))
    return pl.pallas_call(
        body, name=name, grid=(R // tr,),
        in_specs=[pl.BlockSpec((N_CHIPS, tr, Cc), lambda i: (0, i, 0)), blk, blk, blk],
        out_specs=[blk] * 4, out_shape=[jax.ShapeDtypeStruct((R, Cc), F32)] * 4,
        compiler_params=pltpu.CompilerParams(dimension_semantics=("parallel",)),
    )(parts, w, m, v)


def _adamw_small(grads, ws, ms, vs):
    n = len(grads)
    vm = pl.BlockSpec(memory_space=pltpu.VMEM)

    def body(*refs):
        g_in, w_in, m_in, v_in = (refs[k * n:(k + 1) * n] for k in range(4))
        g_out, d_out, m_out, v_out = (refs[(4 + k) * n:(5 + k) * n] for k in range(4))
        for k in range(n):
            g = g_in[k][...]
            d, mn, vn = _adamw_math(w_in[k][...], g, m_in[k][...], v_in[k][...])
            g_out[k][...] = g
            d_out[k][...] = d
            m_out[k][...] = mn
            v_out[k][...] = vn

    shapes = [jax.ShapeDtypeStruct(g.shape, F32) for g in grads]
    return pl.pallas_call(
        body, name="adamw_small", out_shape=shapes * 4, in_specs=[vm] * (4 * n), out_specs=[vm] * (4 * n),
    )(*grads, *ws, *ms, *vs)


def _shard_cols(full):
    R, Ct = full.shape
    return jnp.transpose(full.reshape(R, N_DEV, Ct // N_DEV), (1, 0, 2))


_COL_SHARDED = ("w_in", "w_gate", "w_up")
_BIG = ("w_in", "w_out", "w_gate", "w_up", "w_down")


def _rows(nm, p):
    return p[0].T if nm in _COL_SHARDED else p[0]


def _step(args, ts_mix_fwd, ts_ffn, ts_mix_bwd, tm_grad):
    (x, g_mix, w_in, b_in, w_dw, b_dw, ln_g, ln_b, w_pool, s_pool, w_out, g_ffn, w_gate, w_up, w_down, g_final,
     loss_target) = args[:17]
    names = ["g_mix", "w_in", "b_in", "w_dw", "b_dw", "ln_g", "ln_b", "w_pool", "s_pool", "w_out", "g_ffn",
             "w_gate", "w_up", "w_down", "g_final"]
    weights = dict(zip(names, args[1:16]))
    moms = dict(zip(names, args[17:32]))
    vars_ = dict(zip(names, args[32:47]))

    S, D = x.shape[1], x.shape[2]
    x2 = x.reshape(S, D)
    tgt2 = loss_target.reshape(S, D)

    shard = lambda nm: _rows(nm, weights[nm])
    w_pool_b = w_pool[0].astype(BF16)
    dw_rows = -(-CONV_WIDTH // SUBLANES) * SUBLANES
    dw_shard = jnp.pad(w_dw[0], ((0, dw_rows - CONV_WIDTH), (0, LANES - w_dw.shape[2])))

    a, gate, v, m, y, h1, xn, w_dw_f, g_in, g_out, _, g_gate, g_up, g_down = _mix_fwd(
        x2, g_mix, b_in, b_dw, ln_g, ln_b, w_pool_b, s_pool, [shard("w_in"), shard("w_out"), dw_shard],
        [shard("w_gate"), shard("w_up"), shard("w_down")], ts_mix_fwd)
    wt_in, w_out_f = g_in.reshape(-1, D), g_out.reshape(-1, D)
    wt_gate, wt_up, w_down_f = g_gate.reshape(-1, D), g_up.reshape(-1, D), g_down.reshape(-1, D)
    Fd = wt_gate.shape[0]
    f_chunks = [1024] * (Fd // 1024) + ([Fd % 1024] if Fd % 1024 else [])
    dh1, hn, act, dgt, dup, dh2, loss_p, dg_final, dg_ffn = _ffn(
        h1, tgt2, g_ffn, g_final.reshape(1, D), wt_gate, wt_up, w_down_f, ts_ffn, f_chunks)

    by_shard = lambda g: g.reshape(N_DEV, -1, D)
    p_gate = by_shard(_grad_matmul(dgt, hn, tm_grad, "grad_w_gate"))
    g_up, s_gate = _grad_matmul_pair_exchange(dup, hn, tm_grad, "grad_w_up", [p_gate])
    p_up = by_shard(g_up)
    g_down, s_up = _grad_matmul_pair_exchange(act, dh2, tm_grad, "grad_w_down", [p_up])
    p_down = by_shard(g_down)
    s_down, = _pair_exchange([p_down])
    pair_sums = _pair_sum([p_gate, p_up, p_down], [s_gate, s_up, s_down])
    (dx, dz, dh1b, dg_mix, db_in, dw_dw, db_dw, dln_g, dln_b, dw_pool, ds_pool, r_gate, r_up, r_down) = _mix_bwd(
        dh1, x2, a, gate, v, m, g_mix, wt_in, w_dw_f, ln_g, ln_b, w_pool_b, s_pool, w_out_f, pair_sums, ts_mix_bwd)

    small_names = ["g_mix", "b_in", "b_dw", "ln_g", "ln_b", "w_pool", "s_pool", "g_ffn", "g_final"]
    small_shape = lambda p: p.reshape(-1, p.shape[-1])
    partial = [dg_mix, db_in, db_dw, dln_g, dln_b, dw_pool.reshape(-1, POOL_GROUP), ds_pool, dg_ffn, dg_final, loss_p]
    dw_out, *summed = _grad_matmul(y, dh1b, tm_grad, "grad_w_out", reduce_vals=partial)
    dwt_in, r_out, r_dw = _grad_matmul(
        dz, xn, tm_grad, "grad_w_in",
        reduce_parts=[dw_out.reshape(N_DEV, -1, D), _shard_cols(dw_dw[0:CONV_WIDTH])])
    r_in, = _tail_reduce([dwt_in.reshape(N_DEV, -1, D)], [])

    big = {}
    for nm, r in zip(_BIG, [r_in, r_out, r_gate, r_up, r_down]):
        rows = r.shape[1]
        res = _adamw_shard(r, _rows(nm, weights[nm]), _rows(nm, moms[nm]), _rows(nm, vars_[nm]),
                           rows // 2 if rows % 32 == 0 else rows, "adamw_" + nm)
        big[nm] = [o.T if nm in _COL_SHARDED else o for o in res]
    big["w_dw"] = _adamw_shard(r_dw, w_dw[0], moms["w_dw"][0], vars_["w_dw"][0], CONV_WIDTH, "adamw_w_dw")

    sm = _adamw_small(summed[:-1], [small_shape(weights[nm]) for nm in small_names],
                      [small_shape(moms[nm]) for nm in small_names], [small_shape(vars_[nm]) for nm in small_names])
    n_small = len(small_names)

    def result(kind, nm):
        if nm in big:
            return big[nm][kind].reshape(weights[nm].shape)
        return sm[kind * n_small + small_names.index(nm)].reshape(weights[nm].shape)

    loss = summed[-1][0, 0]
    out = [loss, dx.reshape(x.shape)]
    for kind in range(4):
        out += [result(kind, nm) for nm in names]
    return tuple(out)


def kernel(x, g_mix, w_in, b_in, w_dw, b_dw, ln_g, ln_b, w_pool, s_pool, w_out, g_ffn, w_gate, w_up, w_down, g_final, loss_target, m_g_mix, m_w_in, m_b_in, m_w_dw, m_b_dw, m_ln_g, m_ln_b, m_w_pool, m_s_pool, m_w_out, m_g_ffn, m_w_gate, m_w_up, m_w_down, m_g_final, v_g_mix, v_w_in, v_b_in, v_w_dw, v_b_dw, v_ln_g, v_ln_b, v_w_pool, v_s_pool, v_w_out, v_g_ffn, v_w_gate, v_w_up, v_w_down, v_g_final):
    args = (x, g_mix, w_in, b_in, w_dw, b_dw, ln_g, ln_b, w_pool, s_pool, w_out, g_ffn, w_gate, w_up, w_down, g_final, loss_target, m_g_mix, m_w_in, m_b_in, m_w_dw, m_b_dw, m_ln_g, m_ln_b, m_w_pool, m_s_pool, m_w_out, m_g_ffn, m_w_gate, m_w_up, m_w_down, m_g_final, v_g_mix, v_w_in, v_b_in, v_w_dw, v_b_dw, v_ln_g, v_ln_b, v_w_pool, v_s_pool, v_w_out, v_g_ffn, v_w_gate, v_w_up, v_w_down, v_g_final)
    return _step(args, ts_mix_fwd=512, ts_ffn=256, ts_mix_bwd=512, tm_grad=256)
```

```python
import jax
import jax.numpy as jnp
from jax import lax
from jax.experimental import pallas as pl
from jax.experimental.pallas import tpu as pltpu

F32 = jnp.float32
BF16 = jnp.bfloat16
MESH = pl.DeviceIdType.MESH
N_DEV = 8

C_CONV = 512
CONV_WIDTH = 31
POOL_WINDOWS = (2, 4, 8, 16)
POOL_GROUP = 128
RMS_EPS = 1e-6
LN_EPS = 1e-5

ADAM_LR = 0.001
ADAM_B1 = 0.9
ADAM_B2 = 0.999
ADAM_EPS = 1e-08
ADAM_WD = 0.01
ADAM_STEP = 10

HALO = 32
SUBLANES = 8
LANES = 128
CONV_ROWS = 64
VMEM_LIMIT = 56 * 1024 * 1024


def _dot(a, b):
    return jnp.dot(a, b, preferred_element_type=F32)


def _dot_nt(a, b):
    return lax.dot_general(a, b, (((1,), (1,)), ((), ())), preferred_element_type=F32)


def _dot_tn(a, b):
    return lax.dot_general(a, b, (((0,), (0,)), ((), ())), preferred_element_type=F32)


def _mean_last(v):
    return jnp.mean(v, axis=-1, keepdims=True)


def _full(shape):
    nd = len(shape)
    return pl.BlockSpec(shape, lambda *_: (0,) * nd)


def _full1(shape):
    nd = len(shape)
    return pl.BlockSpec(shape, lambda *_: (0,) * nd, pipeline_mode=pl.Buffered(1))


def _shifted_copies(sh_ref, rows):
    for s in range(1, SUBLANES):
        sh_ref[s, 0:rows, :] = sh_ref[0, s:s + rows, :]


def _tap(sh_ref, off, r0, rows):
    q, s = divmod(off, SUBLANES)
    return sh_ref[s, pl.ds(r0 + q * SUBLANES, rows), :]


def _window_sums(src, bufs, rows, backward):
    assert all(w == 2 << g for g, w in enumerate(POOL_WINDOWS))
    n = len(POOL_WINDOWS)
    out = []
    for level in range(n):
        dst, shift = bufs[level % 2], 1 << level
        lanes = slice(level * POOL_GROUP, n * POOL_GROUP)
        lo, hi = (SUBLANES * (level + 1), rows) if backward else (0, rows - SUBLANES * (level + 1))
        other = slice(lo - shift, hi - shift) if backward else slice(lo + shift, hi + shift)
        dst[lo:hi, lanes] = src[lo:hi, lanes] + src[other, lanes]
        src = dst
        out.append(dst)
    return out


def _mix_fwd(x, g_mix, b_in, b_dw, ln_g, ln_b, w_pool, s_pool, mix_shards, ag_shards, ts):
    S, D = x.shape
    d_in = mix_shards[0].shape[0] * N_DEV
    C = C_CONV
    nt = S // ts
    nrb = ts // CONV_ROWS
    n_ag = len(ag_shards)
    relay_step = nt // 2
    mix_dtypes = [BF16, BF16, F32]

    def body(x_ref, g_ref, bin_ref, bdw_ref, lng_ref, lnb_ref, wp_ref, sp_ref, *rest):
        mx_in, ag_in, rest = rest[:3], rest[3:3 + n_ag], rest[3 + n_ag:]
        a_ref, gate_ref, v_ref, m_ref, y_ref, h1_ref, xn_ref, wdw_ref = rest[:8]
        mx_out, ag_out, rest = rest[8:11], rest[11:11 + n_ag], rest[11 + n_ag:]
        ush, pbuf, pa, pb, win_ref, wout_ref, gdw, load_sems = rest[:8]
        rest = rest[8:]
        (ma_stage, ma_bufs, ma_sems), rest = (rest[:2], rest[2:4], rest[4:7]), rest[7:]
        (mb_stage, mb_bufs, mb_sems), rest = (rest[:1], rest[1:2], rest[2:5]), rest[5:]
        ag_stage, ag_bufs, ag_sems = rest[:n_ag], rest[n_ag:2 * n_ag], rest[2 * n_ag:]
        i = pl.program_id(0)
        ma_start, ma_relay, ma_pass_on, ma_finish = _gather_plan(
            mx_in[0::2], mx_out[0::2], ma_stage, ma_bufs, *ma_sems)
        mb_start, mb_relay, mb_pass_on, mb_finish = _gather_plan(
            mx_in[1:2], mx_out[1:2], mb_stage, mb_bufs, *mb_sems)
        ag_start, ag_relay, ag_pass_on, ag_finish = _gather_plan(ag_in, ag_out, ag_stage, ag_bufs, *ag_sems)

        def load_weight(k, dst_ref):
            rows = mx_out[k].shape[1]
            loads = [pltpu.make_async_copy(mx_out[k].at[j], dst_ref.at[pl.ds(j * rows, rows), :],
                                           load_sems.at[N_DEV * k + j]) for j in range(N_DEV)]
            for cp in loads:
                cp.start()
            for cp in loads:
                cp.wait()

        @pl.when(i == 0)
        def _():
            ma_start()
            mb_start()
            ma_relay()
            mb_relay()
            ag_start()
            ma_pass_on()
            ma_finish()
            load_weight(0, win_ref)
            dw_load = pltpu.make_async_copy(mx_out[2], gdw, load_sems.at[2 * N_DEV])
            dw_load.start()
            dw_load.wait()
            first_half = lax.broadcasted_iota(jnp.int32, gdw.shape[1:], 1) < C // N_DEV
            for p in range(N_DEV // 2):
                wdw_ref[:, LANES * p:LANES * (p + 1)] = jnp.where(
                    first_half, gdw[2 * p], pltpu.roll(gdw[2 * p + 1], C // N_DEV, axis=1))
            ush[0, 0:HALO, :] = jnp.zeros((HALO, C), F32)
            pbuf[0:HALO, :] = jnp.zeros((HALO, C), F32)

        @pl.when(i == relay_step)
        def _():
            ag_relay()

        @pl.when(i == nt - 1)
        def _():
            ag_pass_on()

        x = x_ref[...]
        r1 = lax.rsqrt(_mean_last(x * x) + RMS_EPS)
        xn = (x * r1 * g_ref[...]).astype(BF16)
        xn_ref[...] = xn
        z = _dot_nt(xn, win_ref[...]) + bin_ref[...]
        a = z[:, 0:C]
        gate = z[:, C:2 * C]
        a_ref[...] = a
        gate_ref[...] = gate
        ush[0, HALO:HALO + ts, :] = a * jax.nn.sigmoid(gate)
        pbuf[HALO:HALO + ts, :] = z[:, 2 * C:]

        _shifted_copies(ush, ts + HALO - SUBLANES)

        def conv_block(rb, carry):
            r0 = pl.multiple_of(rb * CONV_ROWS, CONV_ROWS)
            acc = jnp.zeros((CONV_ROWS, C), F32)
            for k in range(CONV_WIDTH):
                acc = acc + wdw_ref[k:k + 1, :] * _tap(ush, HALO - (CONV_WIDTH - 1) + k, r0, CONV_ROWS)
            v_ref[pl.ds(r0, CONV_ROWS), :] = acc + bdw_ref[...]
            return carry

        lax.fori_loop(0, nrb, conv_block, 0)

        v = v_ref[...]
        mu = _mean_last(v)
        xc = v - mu
        rstd = lax.rsqrt(_mean_last(xc * xc) + LN_EPS)
        ln = xc * rstd * lng_ref[...] + lnb_ref[...]
        y_ref[:, 0:C] = (ln * jax.nn.sigmoid(ln)).astype(BF16)

        sums = _window_sums(pbuf, (pa, pb), ts + HALO, backward=True)
        row = lax.broadcasted_iota(jnp.int32, (ts, 1), 0) + i * ts
        for gi, w in enumerate(POOL_WINDOWS):
            lanes = slice(gi * POOL_GROUP, (gi + 1) * POOL_GROUP)
            seg = pbuf[HALO:HALO + ts, lanes]
            ws = sums[gi][HALO:HALO + ts, lanes]
            cnt = jnp.minimum(row + 1, w).astype(F32)
            m = (ws / cnt - seg).astype(BF16)
            m_ref[:, lanes] = m
            ypre = _dot(m, wp_ref[gi])
            y_ref[:, C + gi * POOL_GROUP:C + (gi + 1) * POOL_GROUP] = (ypre * sp_ref[:, lanes]).astype(BF16)

        @pl.when(i == 0)
        def _():
            mb_pass_on()
            mb_finish()
            load_weight(1, wout_ref)

        h1_ref[...] = x + _dot(y_ref[...], wout_ref[...])

        ush[0, 0:HALO, :] = ush[0, ts:ts + HALO, :]
        pbuf[0:HALO, :] = pbuf[ts:ts + HALO, :]

        @pl.when(i == nt - 1)
        def _():
            ag_finish()

    tile = lambda w, dt: (pl.BlockSpec((ts, w), lambda i: (i, 0)), jax.ShapeDtypeStruct((S, w), dt))
    wdw_rows = mix_shards[2].shape[0]
    outs = [tile(C, F32), tile(C, F32), tile(C, F32), tile(C, BF16), tile(D, BF16), tile(D, F32), tile(D, BF16),
            (_full((wdw_rows, C)), jax.ShapeDtypeStruct((wdw_rows, C), F32))]
    return pl.pallas_call(
        body, name="mix_fwd", grid=(nt,),
        in_specs=[pl.BlockSpec((ts, D), lambda i: (i, 0)), _full((1, D)), _full((1, d_in)), _full((1, C)),
                  _full((1, C)), _full((1, C)), _full(w_pool.shape), _full((1, C))] + [_HBM] * (3 + n_ag),
        out_specs=[o[0] for o in outs] + [_HBM] * (3 + n_ag),
        out_shape=[o[1] for o in outs] + _gather_out_shapes(mix_shards, mix_dtypes)
        + _gather_out_shapes(ag_shards, [BF16] * n_ag),
        scratch_shapes=[pltpu.VMEM((SUBLANES, ts + HALO, C), F32)] + [pltpu.VMEM((ts + HALO, C), F32)] * 3
        + [pltpu.VMEM((d_in, D), BF16), pltpu.VMEM((D, D), BF16),
                        pltpu.VMEM((N_DEV,) + mix_shards[2].shape, F32), pltpu.SemaphoreType.DMA((2 * N_DEV + 1,))]
        + _gather_scratch(mix_shards[0::2], mix_dtypes[0::2]) + _gather_scratch(mix_shards[1:2], mix_dtypes[1:2])
        + _gather_scratch(ag_shards, [BF16] * n_ag),
        compiler_params=pltpu.CompilerParams(dimension_semantics=("arbitrary",), vmem_limit_bytes=VMEM_LIMIT),
    )(x, g_mix, b_in, b_dw, ln_g, ln_b, w_pool, s_pool, *mix_shards, *ag_shards)


def _ffn(h1, target, g_ffn, g_final, w_gate, w_up, w_down, ts, f_chunks):
    S, D = h1.shape
    Fd = w_gate.shape[0]
    nt = S // ts
    bounds = []
    lo = 0
    for n in f_chunks:
        bounds.append((lo, lo + n))
        lo += n
    assert lo == Fd

    def body(h1_ref, tgt_ref, gf_ref, gl_ref, wg_ref, wu_hbm, wd_hbm,
             dh1_ref, hn_ref, act_ref, dgt_ref, dup_ref, dh2_ref, loss_ref, dgl_ref, dgf_ref, gt_s, up_s, wu_ref, wd_ref,
             w_sems):
        i = pl.program_id(0)
        wu_load = pltpu.make_async_copy(wu_hbm, wu_ref, w_sems.at[0])
        wd_load = pltpu.make_async_copy(wd_hbm, wd_ref, w_sems.at[1])

        @pl.when(i == 0)
        def _():
            wu_load.start()
            wd_load.start()
            loss_ref[...] = jnp.zeros_like(loss_ref)
            dgl_ref[...] = jnp.zeros_like(dgl_ref)
            dgf_ref[...] = jnp.zeros_like(dgf_ref)

        h1 = h1_ref[...]
        r2 = lax.rsqrt(_mean_last(h1 * h1) + RMS_EPS)
        hhat = h1 * r2
        hn = (hhat * gf_ref[...]).astype(BF16)
        hn_ref[...] = hn
        h2 = h1
        for lo, hi in bounds:
            gt = _dot_nt(hn, wg_ref[lo:hi, :])
            if lo == 0:
                pl.when(i == 0)(wu_load.wait)
            up = _dot_nt(hn, wu_ref[lo:hi, :])
            gt_s[:, lo:hi] = gt
            up_s[:, lo:hi] = up
            act = (gt * jax.nn.sigmoid(gt) * up).astype(BF16)
            act_ref[:, lo:hi] = act
            if lo == 0:
                pl.when(i == 0)(wd_load.wait)
            h2 = h2 + _dot(act, wd_ref[lo:hi, :])

        r3 = lax.rsqrt(_mean_last(h2 * h2) + RMS_EPS)
        n3 = h2 * r3
        gl = gl_ref[...]
        diff = n3 * gl - tgt_ref[...]
        loss_ref[...] += jnp.sum(0.5 * jnp.sum(diff * diff, axis=-1, keepdims=True) / D, axis=0, keepdims=True)
        dout = diff / D
        dgl_ref[...] += jnp.sum(dout * n3, axis=0, keepdims=True)
        dn = dout * gl
        dh2 = r3 * (dn - n3 * _mean_last(dn * n3))
        dh2b = dh2.astype(BF16)
        dh2_ref[...] = dh2b

        dhn = jnp.zeros((ts, D), F32)
        for lo, hi in bounds:
            gt = gt_s[:, lo:hi]
            up = up_s[:, lo:hi]
            sg = jax.nn.sigmoid(gt)
            dact = _dot_nt(dh2b, wd_ref[lo:hi, :])
            dgt = (dact * up * (sg * (1.0 + gt * (1.0 - sg)))).astype(BF16)
            dup = (dact * (gt * sg)).astype(BF16)
            dgt_ref[:, lo:hi] = dgt
            dup_ref[:, lo:hi] = dup
            dhn = dhn + _dot(dgt, wg_ref[lo:hi, :]) + _dot(dup, wu_ref[lo:hi, :])

        dgf_ref[...] += jnp.sum(dhn * hhat, axis=0, keepdims=True)
        dnn = dhn * gf_ref[...]
        dh1_ref[...] = dh2 + r2 * (dnn - hhat * _mean_last(dnn * hhat))

    tile = lambda w, dt: (pl.BlockSpec((ts, w), lambda i: (i, 0)), jax.ShapeDtypeStruct((S, w), dt))
    acc = lambda w: (_full((1, w)), jax.ShapeDtypeStruct((1, w), F32))
    outs = [tile(D, F32), tile(D, BF16), tile(Fd, BF16), tile(Fd, BF16), tile(Fd, BF16), tile(D, BF16),
            acc(LANES), acc(D), acc(D)]
    return pl.pallas_call(
        body, name="ffn_fwd_bwd", grid=(nt,),
        in_specs=[pl.BlockSpec((ts, D), lambda i: (i, 0)), pl.BlockSpec((ts, D), lambda i: (i, 0)),
                  _full((1, D)), _full((1, D)), _full1((Fd, D)), _HBM, _HBM],
        out_specs=[o[0] for o in outs], out_shape=[o[1] for o in outs],
        scratch_shapes=[pltpu.VMEM((ts, Fd), F32), pltpu.VMEM((ts, Fd), F32), pltpu.VMEM((Fd, D), BF16),
                        pltpu.VMEM((Fd, D), BF16), pltpu.SemaphoreType.DMA((2,))],
        compiler_params=pltpu.CompilerParams(dimension_semantics=("arbitrary",), vmem_limit_bytes=VMEM_LIMIT),
    )(h1, target, g_ffn, g_final, w_gate, w_up, w_down)


def _mix_bwd(dh1, x, a, gate, v, m, g_mix, w_in, w_dw, ln_g, ln_b, w_pool, s_pool, w_out, rs_parts, ts):
    S, D = x.shape
    n_rs = len(rs_parts)
    d_in = w_in.shape[0]
    C = C_CONV
    nt = S // ts
    nrb = ts // CONV_ROWS
    wrows =((CONV_WIDTH + SUBLANES - 1) // SUBLANES) * SUBLANES

    def body(dh1_ref, x_ref, a_ref, gate_ref, v_ref, m_ref, g_ref, win_ref, wdw_ref, lng_ref,
             lnb_ref, wp_ref, sp_ref, wout_ref, *rest):
        rs_in, rest = rest[:n_rs], rest[n_rs:]
        (dx_ref, dz_ref, dh1b_ref, dgm_ref, dbin_ref, dwdw_ref, dbdw_ref, dlng_ref, dlnb_ref, dwp_ref,
         dsp_ref) = rest[:11]
        rs_out, rest = rest[11:11 + n_rs], rest[11 + n_rs:]
        dvsh, dqbuf, qa, qb, du_s, dm_s = rest[:6]
        rs_bufs, (send_sems, recv_sems, local_sems) = rest[6:6 + n_rs], rest[6 + n_rs:]
        i = pl.program_id(0)
        t = nt - 1 - i
        rs_start, rs_finish = _chip_exchange_plan(rs_in, rs_out, rs_bufs, send_sems, recv_sems, local_sems)

        @pl.when(i == 0)
        def _():
            rs_start()
            dvsh[0, ts:ts + HALO, :] = jnp.zeros((HALO, C), F32)
            dqbuf[ts:ts + HALO, :] = jnp.zeros((HALO, C), F32)
            for r in (dgm_ref, dbin_ref, dwdw_ref, dbdw_ref, dlng_ref, dlnb_ref, dwp_ref, dsp_ref):
                r[...] = jnp.zeros_like(r)

        dh1 = dh1_ref[...]
        dh1b = dh1.astype(BF16)
        dh1b_ref[...] = dh1b
        dy = _dot_nt(dh1b, wout_ref[...])

        v = v_ref[...]
        mu = _mean_last(v)
        xc = v - mu
        rstd = lax.rsqrt(_mean_last(xc * xc) + LN_EPS)
        vhat = xc * rstd
        lng = lng_ref[...]
        ln = vhat * lng + lnb_ref[...]
        sg = jax.nn.sigmoid(ln)
        dln = dy[:, 0:C] * (sg * (1.0 + ln * (1.0 - sg)))
        dlng_ref[...] += jnp.sum(dln * vhat, axis=0, keepdims=True)
        dlnb_ref[...] += jnp.sum(dln, axis=0, keepdims=True)
        dvh = dln * lng
        dv = rstd * (dvh - _mean_last(dvh) - vhat * _mean_last(dvh * vhat))
        dbdw_ref[...] += jnp.sum(dv, axis=0, keepdims=True)
        dvsh[0, 0:ts, :] = dv
        _shifted_copies(dvsh, ts + HALO - SUBLANES)

        def conv_block(rb, carry):
            r0 = pl.multiple_of(rb * CONV_ROWS, CONV_ROWS)
            acc = jnp.zeros((CONV_ROWS, C), F32)
            for k in range(CONV_WIDTH):
                acc = acc + wdw_ref[k:k + 1, :] * _tap(dvsh, CONV_WIDTH - 1 - k, r0, CONV_ROWS)
            du_s[pl.ds(r0, CONV_ROWS), :] = acc
            return carry

        lax.fori_loop(0, nrb, conv_block, 0)

        a = a_ref[...]
        sgate = jax.nn.sigmoid(gate_ref[...])
        u = a * sgate
        for k in range(CONV_WIDTH):
            q, s = divmod(CONV_WIDTH - 1 - k, SUBLANES)
            prod = u * dvsh[s, q * SUBLANES:q * SUBLANES + ts, :]
            dwdw_ref[k:k + 1, :] += jnp.sum(prod, axis=0, keepdims=True)

        du = du_s[...]
        da = du * sgate
        dgate = du * a * sgate * (1.0 - sgate)
        dz_ref[:, 0:C] = da.astype(BF16)
        dz_ref[:, C:2 * C] = dgate.astype(BF16)
        dbin_ref[:, 0:C] += jnp.sum(da, axis=0, keepdims=True)
        dbin_ref[:, C:2 * C] += jnp.sum(dgate, axis=0, keepdims=True)

        row = lax.broadcasted_iota(jnp.int32, (ts, 1), 0) + t * ts
        for gi, w in enumerate(POOL_WINDOWS):
            lanes = slice(gi * POOL_GROUP, (gi + 1) * POOL_GROUP)
            dyp = dy[:, C + gi * POOL_GROUP:C + (gi + 1) * POOL_GROUP]
            mg = m_ref[:, lanes]
            ypre = _dot(mg, wp_ref[gi])
            dsp_ref[:, lanes] += jnp.sum(dyp * ypre, axis=0, keepdims=True)
            dyi = (dyp * sp_ref[:, lanes]).astype(BF16)
            dwp_ref[gi] += _dot_tn(mg, dyi)
            dm = _dot_nt(dyi, wp_ref[gi])
            cnt = jnp.minimum(row + 1, w).astype(F32)
            dqbuf[0:ts, lanes] = dm / cnt
            dm_s[:, lanes] = dm
        sums = _window_sums(dqbuf, (qa, qb), ts + HALO, backward=False)
        for gi in range(len(POOL_WINDOWS)):
            lanes = slice(gi * POOL_GROUP, (gi + 1) * POOL_GROUP)
            dp = sums[gi][0:ts, lanes] - dm_s[:, lanes]
            dz_ref[:, 2 * C + gi * POOL_GROUP:2 * C + (gi + 1) * POOL_GROUP] = dp.astype(BF16)
            dbin_ref[:, 2 * C + gi * POOL_GROUP:2 * C + (gi + 1) * POOL_GROUP] += jnp.sum(dp, axis=0, keepdims=True)

        dxn = _dot(dz_ref[...], win_ref[...])
        x = x_ref[...]
        r1 = lax.rsqrt(_mean_last(x * x) + RMS_EPS)
        xhat = x * r1
        dgm_ref[...] += jnp.sum(dxn * xhat, axis=0, keepdims=True)
        dnn = dxn * g_ref[...]
        dx_ref[...] = dh1 + r1 * (dnn - xhat * _mean_last(dnn * xhat))

        dvsh[0, ts:ts + HALO, :] = dvsh[0, 0:HALO, :]
        dqbuf[ts:ts + HALO, :] = dqbuf[0:HALO, :]

        @pl.when(i == nt - 1)
        def _():
            rs_finish()

    rev = lambda w: pl.BlockSpec((ts, w), lambda i: (nt - 1 - i, 0))
    acc = lambda shape: (_full(shape), jax.ShapeDtypeStruct(shape, F32))
    outs = [(rev(D), jax.ShapeDtypeStruct((S, D), F32)), (rev(d_in), jax.ShapeDtypeStruct((S, d_in), BF16)),
            (rev(D), jax.ShapeDtypeStruct((S, D), BF16)), acc((1, D)), acc((1, d_in)), acc((wrows, C)),
            acc((1, C)), acc((1, C)), acc((1, C)), acc(w_pool.shape), acc((1, C))]
    return pl.pallas_call(
        body, name="mix_bwd", grid=(nt,),
        in_specs=[rev(D), rev(D), rev(C), rev(C), rev(C), rev(C), _full((1, D)), _full((d_in, D)),
                  _full(w_dw.shape), _full((1, C)), _full((1, C)), _full(w_pool.shape), _full((1, C)),
                  _full((D, D))] + [_HBM] * n_rs,
        out_specs=[o[0] for o in outs] + [_HBM] * n_rs,
        out_shape=[o[1] for o in outs] + [jax.ShapeDtypeStruct(p.shape, p.dtype) for p in rs_parts],
        scratch_shapes=[pltpu.VMEM((SUBLANES, ts + HALO, C), F32)] + [pltpu.VMEM((ts + HALO, C), F32)] * 3
        + [pltpu.VMEM((ts, C), F32)] * 2
        + [pltpu.VMEM(p.shape[1:], p.dtype) for p in rs_parts] + _comm_sems(n_rs, 3),
        compiler_params=pltpu.CompilerParams(dimension_semantics=("arbitrary",), vmem_limit_bytes=VMEM_LIMIT),
    )(dh1, x, a, gate, v, m, g_mix, w_in, w_dw, ln_g, ln_b, w_pool, s_pool, w_out, *rs_parts)


def _grad_matmul(a, b, tm, name, reduce_parts=(), reduce_vals=()):
    S, M = a.shape
    N = b.shape[1]
    steps = M // tm
    n, ns = len(reduce_parts), len(reduce_vals)

    def body(a_ref, b_ref, *rest):
        o_ref = rest[n + ns]
        if n + ns:
            start, middle, finish = _reduce_plan(rest[:n], rest[n:n + ns], rest[n + ns + 1:2 * n + ns + 1],
                                                 rest[2 * n + ns + 1:2 * (n + ns) + 1], rest[2 * (n + ns) + 1:])
            i = pl.program_id(0)
            pl.when(i == 0)(start)
            pl.when(i == 1)(middle)
        o_ref[...] = _dot_tn(a_ref[...], b_ref[...]).astype(BF16)
        if n + ns:
            pl.when(i == steps - 1)(finish)

    res = pl.pallas_call(
        body, name=name, grid=(steps,),
        in_specs=[pl.BlockSpec((S, tm), lambda i: (0, i)), _full1((S, N))] + [_VMEM] * (n + ns),
        out_specs=[pl.BlockSpec((tm, N), lambda i: (i, 0))] + [_VMEM] * (n + ns),
        out_shape=[jax.ShapeDtypeStruct((M, N), BF16)] + _reduce_out_shapes(reduce_parts, reduce_vals),
        scratch_shapes=_reduce_scratch(reduce_parts, reduce_vals) if n + ns else [],
        compiler_params=pltpu.CompilerParams(dimension_semantics=("arbitrary",) if n + ns else ("parallel",),
                                             vmem_limit_bytes=VMEM_LIMIT),
    )(a, b, *reduce_parts, *reduce_vals)
    return res if n + ns else res[0]


def _grad_matmul_pair_exchange(a, b, tm, name, parts):
    S, M = a.shape
    N = b.shape[1]
    steps = M // tm
    n = len(parts)

    def body(a_ref, b_ref, *rest):
        o_ref = rest[n]
        start, finish = _pair_exchange_plan(rest[:n], rest[n + 1:2 * n + 1], *rest[2 * n + 1:])
        i = pl.program_id(0)
        pl.when(i == 0)(start)
        o_ref[...] = _dot_tn(a_ref[...], b_ref[...]).astype(BF16)
        pl.when(i == steps - 1)(finish)

    sems = pltpu.SemaphoreType.DMA((n, N_CHIPS))
    return pl.pallas_call(
        body, name=name, grid=(steps,),
        in_specs=[pl.BlockSpec((S, tm), lambda i: (0, i)), _full1((S, N))] + [_HBM] * n,
        out_specs=[pl.BlockSpec((tm, N), lambda i: (i, 0))] + [_HBM] * n,
        out_shape=[jax.ShapeDtypeStruct((M, N), BF16)] + _pair_out_shapes(parts),
        scratch_shapes=[sems, sems],
        compiler_params=pltpu.CompilerParams(dimension_semantics=("arbitrary",), vmem_limit_bytes=VMEM_LIMIT),
    )(a, b, *parts)


def _position():
    return lax.axis_index("x"), lax.axis_index("y"), lax.axis_index("c")


def _slot(px, py, pc):
    return 4 * px + 2 * py + pc


_HBM = pl.BlockSpec(memory_space=pl.ANY)


def _comm_sems(n, copies):
    return [pltpu.SemaphoreType.DMA((n, copies)), pltpu.SemaphoreType.DMA((n, copies)),
            pltpu.SemaphoreType.DMA((n, 2))]


def _local_copy(srcs, dsts, bufs, local_sems):
    n = len(srcs)
    loads = [pltpu.make_async_copy(srcs[k], bufs[k], local_sems.at[k, 0]) for k in range(n)]
    for cp in loads:
        cp.start()
    for cp in loads:
        cp.wait()
    stores = _local_stores(dsts, bufs, local_sems)
    for cp in stores:
        cp.start()
    return stores


def _local_stores(dsts, bufs, local_sems):
    return [pltpu.make_async_copy(bufs[k], dsts[k], local_sems.at[k, 1]) for k in range(len(dsts))]


def _gather_out_shapes(shards, dtypes):
    return [jax.ShapeDtypeStruct((N_DEV,) + s.shape, dt) for s, dt in zip(shards, dtypes)]


def _gather_scratch(shards, dtypes):
    return ([pltpu.VMEM(s.shape, s.dtype) for s in shards] + [pltpu.VMEM(s.shape, dt) for s, dt in zip(shards, dtypes)]
            + _comm_sems(len(shards), 7))


def _gather_plan(ins, outs, stage, bufs, send_sems, recv_sems, local_sems):
    n = len(ins)
    x, y, c = _position()
    me, sibling = (x, y, c), (x, y, 1 - c)
    na, nb, dg = (x ^ (1 - c), y ^ c), (x ^ c, y ^ (1 - c)), (1 - x, 1 - y)
    own = [outs[k].at[_slot(*me)] for k in range(n)]

    def copy(k, sem, block, to, src=None):
        dst = outs[k].at[_slot(*block)]
        return pltpu.make_async_remote_copy(
            src_ref=dst if src is None else src, dst_ref=dst, send_sem=send_sems.at[k, sem],
            recv_sem=recv_sems.at[k, sem], device_id=to, device_id_type=MESH)

    def first():
        cps = []
        for k in range(n):
            cps += [copy(k, 0, me, sibling, src=bufs[k]), copy(k, 1, me, (*na, c), src=bufs[k]),
                    copy(k, 2, me, (*nb, c), src=bufs[k])]
        return cps

    def onward():
        return [copy(k, 3, (*na, c), (*nb, c)) for k in range(n)]

    def to_sibling(j, chip):
        return [copy(k, 4 + j, (*chip, c), sibling) for k in range(n)]

    def start():
        loads = [pltpu.make_async_copy(ins[k], stage[k], local_sems.at[k, 0]) for k in range(n)]
        for cp in loads:
            cp.start()
        for cp in loads:
            cp.wait()
        for k in range(n):
            bufs[k][...] = stage[k][...].astype(bufs[k].dtype)
        for cp in first() + _local_stores(own, bufs, local_sems):
            cp.start()

    def relay():
        for k in range(n):
            copy(k, 1, (*na, c), me).wait_recv()
        for cp in onward() + to_sibling(0, na):
            cp.start()
        for k in range(n):
            copy(k, 2, (*nb, c), me).wait_recv()
        for cp in to_sibling(1, nb):
            cp.start()

    def pass_on():
        for k in range(n):
            copy(k, 3, (*dg, c), me).wait_recv()
        for cp in to_sibling(2, dg):
            cp.start()

    def finish():
        for k in range(n):
            copy(k, 0, sibling, me).wait_recv()
            for j, chip in enumerate((nb, na, dg)):
                copy(k, 4 + j, (*chip, 1 - c), me).wait_recv()
        for cp in first() + onward() + to_sibling(0, na) + to_sibling(1, nb) + to_sibling(2, dg):
            cp.wait_send()
        for cp in _local_stores(own, bufs, local_sems):
            cp.wait()

    return start, relay, pass_on, finish


N_CHIPS = 4
_CHIP_FLIPS = [(1, 0), (0, 1), (1, 1)]
F32_TRAVEL_LIMIT = 4096


def _chip_exchange_plan(ins, outs, bufs, send_sems, recv_sems, local_sems):
    n = len(ins)
    x, y, c = _position()
    my_q = 2 * x + y
    peers = [(x ^ fx, y ^ fy) for fx, fy in _CHIP_FLIPS]
    own = [outs[k].at[my_q] for k in range(n)]

    def sends():
        return [pltpu.make_async_remote_copy(
            src_ref=ins[k].at[2 * px + py], dst_ref=outs[k].at[my_q], send_sem=send_sems.at[k, f],
            recv_sem=recv_sems.at[k, f], device_id=(px, py, c), device_id_type=MESH)
            for f, (px, py) in enumerate(peers) for k in range(n)]

    def start():
        for cp in sends():
            cp.start()
        _local_copy([ins[k].at[my_q] for k in range(n)], own, bufs, local_sems)

    def finish():
        for f, (px, py) in enumerate(peers):
            for k in range(n):
                pltpu.make_async_remote_copy(
                    src_ref=ins[k].at[my_q], dst_ref=outs[k].at[2 * px + py], send_sem=send_sems.at[k, f],
                    recv_sem=recv_sems.at[k, f], device_id=(px, py, c), device_id_type=MESH).wait_recv()
        for cp in sends():
            cp.wait_send()
        for cp in _local_stores(own, bufs, local_sems):
            cp.wait()

    return start, finish


def _pair_exchange(parts):
    n = len(parts)

    def body(*refs):
        start, finish = _pair_exchange_plan(refs[:n], refs[n:2 * n], *refs[2 * n:])
        start()
        finish()

    sems = pltpu.SemaphoreType.DMA((n, N_CHIPS))
    return pl.pallas_call(
        body, name="pair_exchange", out_shape=_pair_out_shapes(parts),
        in_specs=[_HBM] * n, out_specs=[_HBM] * n, scratch_shapes=[sems, sems],
    )(*parts)


def _pair_out_shapes(parts):
    return [jax.ShapeDtypeStruct((N_CHIPS,) + p.shape[1:], p.dtype) for p in parts]


def _pair_exchange_plan(ins, sib, send_sems, recv_sems):
    x, y, c = _position()

    def copies():
        return [pltpu.make_async_remote_copy(
            src_ref=ins[k].at[2 * q + 1 - c], dst_ref=sib[k].at[q], send_sem=send_sems.at[k, q],
            recv_sem=recv_sems.at[k, q], device_id=(x, y, 1 - c), device_id_type=MESH)
            for k in range(len(ins)) for q in range(N_CHIPS)]

    def start():
        for cp in copies():
            cp.start()

    def finish():
        for cp in copies():
            cp.wait()

    return start, finish


def _pair_sum(parts, sib):
    n = len(parts)

    def body(*refs):
        c = lax.axis_index("c")
        for k in range(n):
            refs[2 * n + k][0] = (refs[k][0, c].astype(F32) + refs[n + k][0].astype(F32)).astype(BF16)

    pair = [pl.BlockSpec((1, 2) + p.shape[1:], lambda q: (q, 0, 0, 0)) for p in parts]
    one = [pl.BlockSpec((1,) + p.shape[1:], lambda q: (q, 0, 0)) for p in parts]
    return pl.pallas_call(
        body, name="pair_sum", grid=(N_CHIPS,), in_specs=pair + one, out_specs=one,
        out_shape=[jax.ShapeDtypeStruct(s.shape, BF16) for s in sib],
        compiler_params=pltpu.CompilerParams(dimension_semantics=("parallel",)),
    )(*[p.reshape((N_CHIPS, 2) + p.shape[1:]) for p in parts], *sib)


def _tail_reduce(parts, vals):
    n, ns = len(parts), len(vals)

    def body(*refs):
        start, middle, finish = _reduce_plan(refs[:n], refs[n:n + ns], refs[n + ns:2 * n + ns],
                                             refs[2 * n + ns:2 * (n + ns)], refs[2 * (n + ns):])
        start()
        middle()
        finish()

    return pl.pallas_call(
        body, name="tail_reduce", out_shape=_reduce_out_shapes(parts, vals),
        in_specs=[_VMEM] * (n + ns), out_specs=[_VMEM] * (n + ns), scratch_shapes=_reduce_scratch(parts, vals),
        compiler_params=pltpu.CompilerParams(vmem_limit_bytes=VMEM_LIMIT),
    )(*parts, *vals)


_VMEM = pl.BlockSpec(memory_space=pltpu.VMEM)


def _reduce_out_shapes(parts, vals):
    return ([jax.ShapeDtypeStruct((N_CHIPS,) + p.shape[1:], p.dtype) for p in parts]
            + [jax.ShapeDtypeStruct(v.shape, v.dtype) for v in vals])


def _reduce_scratch(parts, vals):
    n, ns = len(parts), len(vals)
    quarter = [pltpu.VMEM((N_CHIPS,) + p.shape[1:], p.dtype) for p in parts]
    dma = pltpu.SemaphoreType.DMA
    travel = [BF16 if v.size > F32_TRAVEL_LIMIT else v.dtype for v in vals]
    return (quarter * 3 + [pltpu.VMEM(v.shape, v.dtype) for v in vals]
            + [pltpu.VMEM(v.shape, t) for v, t in zip(vals, travel)]
            + [pltpu.VMEM((N_CHIPS,) + v.shape, t) for v, t in zip(vals, travel)]
            + [dma((max(n, 1), N_CHIPS)), dma((max(n, 1), N_CHIPS)), dma((max(ns, 1),)), dma((max(ns, 1),)),
               dma((max(n, 1), 3)), dma((max(n, 1), 3)), dma((max(ns, 1), 3)), dma((max(ns, 1), 3))])


def _reduce_plan(p_in, v_in, p_out, v_out, scratch):
    n, ns = len(p_in), len(v_in)
    p_sib, p_sum, p_all = scratch[:n], scratch[n:2 * n], scratch[2 * n:3 * n]
    v_sib, v_sum, v_all = (scratch[3 * n + j * ns:3 * n + (j + 1) * ns] for j in range(3))
    p1_send, p1_recv, v1_send, v1_recv, p3_send, p3_recv, v3_send, v3_recv = scratch[3 * n + 3 * ns:]
    x, y, c = _position()
    my_q = 2 * x + y
    peers = [(x ^ fx, y ^ fy) for fx, fy in _CHIP_FLIPS]

    def to_sibling(src, dst, send, recv):
        return pltpu.make_async_remote_copy(src_ref=src, dst_ref=dst, send_sem=send, recv_sem=recv,
                                            device_id=(x, y, 1 - c), device_id_type=MESH)

    def level1():
        cps = [to_sibling(p_in[k].at[2 * q + 1 - c], p_sib[k].at[q], p1_send.at[k, q], p1_recv.at[k, q])
               for k in range(n) for q in range(N_CHIPS)]
        return cps + [to_sibling(v_in[k], v_sib[k], v1_send.at[k], v1_recv.at[k]) for k in range(ns)]

    def to_chip(f, src, dst, send, recv):
        px, py = peers[f]
        return pltpu.make_async_remote_copy(src_ref=src, dst_ref=dst, send_sem=send, recv_sem=recv,
                                            device_id=(px, py, c), device_id_type=MESH)

    def level2(sending):
        cps = []
        for f, (px, py) in enumerate(peers):
            their_q = 2 * px + py
            for k in range(n):
                src, dst = (p_sum[k].at[their_q], p_all[k].at[my_q]) if sending else (
                    p_sum[k].at[my_q], p_all[k].at[their_q])
                cps.append(to_chip(f, src, dst, p3_send.at[k, f], p3_recv.at[k, f]))
            for k in range(ns):
                dst = v_all[k].at[my_q] if sending else v_all[k].at[their_q]
                cps.append(to_chip(f, v_sum[k], dst, v3_send.at[k, f], v3_recv.at[k, f]))
        return cps

    def start():
        for cp in level1():
            cp.start()

    def middle():
        for cp in level1():
            cp.wait_recv()
        for k in range(n):
            for q in range(N_CHIPS):
                p_sum[k][q] = (p_in[k][2 * q + c].astype(F32) + p_sib[k][q].astype(F32)).astype(p_sum[k].dtype)
        for k in range(ns):
            v_sum[k][...] = (v_in[k][...] + v_sib[k][...]).astype(v_sum[k].dtype)
        for cp in level2(True):
            cp.start()
        for k in range(n):
            p_all[k][my_q] = p_sum[k][my_q]
        for k in range(ns):
            v_all[k][my_q] = v_sum[k][...]

    def finish():
        for cp in level2(False):
            cp.wait_recv()
        for k in range(n):
            p_out[k][...] = p_all[k][...]
        for k in range(ns):
            total = v_all[k][0].astype(F32)
            for q in range(1, N_CHIPS):
                total = total + v_all[k][q].astype(F32)
            v_out[k][...] = total
        for cp in level1() + level2(True):
            cp.wait_send()

    return start, middle, finish


def _adamw_math(w, g, m, v):
    m = ADAM_B1 * m + (1.0 - ADAM_B1) * g
    v = ADAM_B2 * v + (1.0 - ADAM_B2) * (g * g)
    m_hat = m / (1.0 - ADAM_B1 ** ADAM_STEP)
    v_hat = v / (1.0 - ADAM_B2 ** ADAM_STEP)
    delta = -ADAM_LR * (m_hat / (jnp.sqrt(v_hat) + ADAM_EPS) + ADAM_WD * w)
    return delta, m, v


def _adamw_shards(parts, ws, ms, vs, steps, name):
    n = len(ws)

    def body(*refs):
        ins, outs = refs[:4 * n], refs[4 * n:]
        for k in range(n):
            p_ref, w_ref, m_ref, v_ref = ins[4 * k:4 * k + 4]
            g = p_ref[0].astype(F32)
            for j in range(1, N_CHIPS):
                g = g + p_ref[j].astype(F32)
            d, mn, vn = _adamw_math(w_ref[...], g, m_ref[...], v_ref[...])
            for o, val in zip(outs[4 * k:4 * k + 4], (g, d, mn, vn)):
                o[...] = val

    in_specs, out_specs, out_shape, operands = [], [], [], []
    for p, w, m, v in zip(parts, ws, ms, vs):
        R, Cc = w.shape
        blk = pl.BlockSpec((R // steps, Cc), lambda i: (i, 0))
        in_specs += [pl.BlockSpec((N_CHIPS, R // steps, Cc), lambda i: (0, i, 0)), blk, blk, blk]
        out_specs += [blk] * 4
        out_shape += [jax.ShapeDtypeStruct((R, Cc), F32)] * 4
        operands += [p, w, m, v]
    res = pl.pallas_call(
        body, name=name, grid=(steps,), in_specs=in_specs, out_specs=out_specs, out_shape=out_shape,
        compiler_params=pltpu.CompilerParams(dimension_semantics=("parallel",), vmem_limit_bytes=VMEM_LIMIT),
    )(*operands)
    return [res[4 * k:4 * k + 4] for k in range(n)]


def _adamw_small(grads, ws, ms, vs):
    n = len(grads)
    vm = pl.BlockSpec(memory_space=pltpu.VMEM)

    def body(*refs):
        g_in, w_in, m_in, v_in = (refs[k * n:(k + 1) * n] for k in range(4))
        g_out, d_out, m_out, v_out = (refs[(4 + k) * n:(5 + k) * n] for k in range(4))
        for k in range(n):
            g = g_in[k][...]
            d, mn, vn = _adamw_math(w_in[k][...], g, m_in[k][...], v_in[k][...])
            g_out[k][...] = g
            d_out[k][...] = d
            m_out[k][...] = mn
            v_out[k][...] = vn

    shapes = [jax.ShapeDtypeStruct(g.shape, F32) for g in grads]
    return pl.pallas_call(
        body, name="adamw_small", out_shape=shapes * 4, in_specs=[vm] * (4 * n), out_specs=[vm] * (4 * n),
    )(*grads, *ws, *ms, *vs)


def _shard_cols(full):
    R, Ct = full.shape
    return jnp.transpose(full.reshape(R, N_DEV, Ct // N_DEV), (1, 0, 2))


_COL_SHARDED = ("w_in", "w_gate", "w_up")
_BIG = ("w_in", "w_out", "w_gate", "w_up", "w_down")


def _rows(nm, p):
    return p[0].T if nm in _COL_SHARDED else p[0]


def _step(args, ts_mix_fwd, ts_ffn, ts_mix_bwd, tm_grad):
    (x, g_mix, w_in, b_in, w_dw, b_dw, ln_g, ln_b, w_pool, s_pool, w_out, g_ffn, w_gate, w_up, w_down, g_final,
     loss_target) = args[:17]
    names = ["g_mix", "w_in", "b_in", "w_dw", "b_dw", "ln_g", "ln_b", "w_pool", "s_pool", "w_out", "g_ffn",
             "w_gate", "w_up", "w_down", "g_final"]
    weights = dict(zip(names, args[1:16]))
    moms = dict(zip(names, args[17:32]))
    vars_ = dict(zip(names, args[32:47]))

    S, D = x.shape[1], x.shape[2]
    x2 = x.reshape(S, D)
    tgt2 = loss_target.reshape(S, D)

    shard = lambda nm: _rows(nm, weights[nm])
    w_pool_b = w_pool[0].astype(BF16)
    dw_rows = -(-CONV_WIDTH // SUBLANES) * SUBLANES
    dw_shard = jnp.pad(w_dw[0], ((0, dw_rows - CONV_WIDTH), (0, LANES - w_dw.shape[2])))

    a, gate, v, m, y, h1, xn, w_dw_f, g_in, g_out, _, g_gate, g_up, g_down = _mix_fwd(
        x2, g_mix, b_in, b_dw, ln_g, ln_b, w_pool_b, s_pool, [shard("w_in"), shard("w_out"), dw_shard],
        [shard("w_gate"), shard("w_up"), shard("w_down")], ts_mix_fwd)
    wt_in, w_out_f = g_in.reshape(-1, D), g_out.reshape(-1, D)
    wt_gate, wt_up, w_down_f = g_gate.reshape(-1, D), g_up.reshape(-1, D), g_down.reshape(-1, D)
    Fd = wt_gate.shape[0]
    f_chunks = [1024] * (Fd // 1024) + ([Fd % 1024] if Fd % 1024 else [])
    dh1, hn, act, dgt, dup, dh2, loss_p, dg_final, dg_ffn = _ffn(
        h1, tgt2, g_ffn, g_final.reshape(1, D), wt_gate, wt_up, w_down_f, ts_ffn, f_chunks)

    by_shard = lambda g: g.reshape(N_DEV, -1, D)
    p_gate = by_shard(_grad_matmul(dgt, hn, tm_grad, "grad_w_gate"))
    g_up, s_gate = _grad_matmul_pair_exchange(dup, hn, tm_grad, "grad_w_up", [p_gate])
    p_up = by_shard(g_up)
    g_down, s_up = _grad_matmul_pair_exchange(act, dh2, tm_grad, "grad_w_down", [p_up])
    p_down = by_shard(g_down)
    s_down, = _pair_exchange([p_down])
    pair_sums = _pair_sum([p_gate, p_up, p_down], [s_gate, s_up, s_down])
    (dx, dz, dh1b, dg_mix, db_in, dw_dw, db_dw, dln_g, dln_b, dw_pool, ds_pool, r_gate, r_up, r_down) = _mix_bwd(
        dh1, x2, a, gate, v, m, g_mix, wt_in, w_dw_f, ln_g, ln_b, w_pool_b, s_pool, w_out_f, pair_sums, ts_mix_bwd)

    small_names = ["g_mix", "b_in", "b_dw", "ln_g", "ln_b", "w_pool", "s_pool", "g_ffn", "g_final"]
    small_shape = lambda p: p.reshape(-1, p.shape[-1])
    partial = [dg_mix, db_in, db_dw, dln_g, dln_b, dw_pool.reshape(-1, POOL_GROUP), ds_pool, dg_ffn, dg_final, loss_p]
    dw_out, *summed = _grad_matmul(y, dh1b, tm_grad, "grad_w_out", reduce_vals=partial)
    dwt_in, r_out, r_dw = _grad_matmul(
        dz, xn, tm_grad, "grad_w_in",
        reduce_parts=[dw_out.reshape(N_DEV, -1, D), _shard_cols(dw_dw[0:CONV_WIDTH])])
    r_in, = _tail_reduce([dwt_in.reshape(N_DEV, -1, D)], [])

    big = {}
    recv = dict(zip(_BIG, [r_in, r_out, r_gate, r_up, r_down]))
    for call, group in (("adamw_ffn", ("w_gate", "w_up", "w_down")), ("adamw_mix", ("w_in", "w_out"))):
        results = _adamw_shards([recv[nm] for nm in group], [_rows(nm, weights[nm]) for nm in group],
                                [_rows(nm, moms[nm]) for nm in group], [_rows(nm, vars_[nm]) for nm in group], 2, call)
        for nm, res in zip(group, results):
            big[nm] = [o.T if nm in _COL_SHARDED else o for o in res]
    big["w_dw"], = _adamw_shards([r_dw], [w_dw[0]], [moms["w_dw"][0]], [vars_["w_dw"][0]], 1, "adamw_w_dw")

    sm = _adamw_small(summed[:-1], [small_shape(weights[nm]) for nm in small_names],
                      [small_shape(moms[nm]) for nm in small_names], [small_shape(vars_[nm]) for nm in small_names])
    n_small = len(small_names)

    def result(kind, nm):
        if nm in big:
            return big[nm][kind].reshape(weights[nm].shape)
        return sm[kind * n_small + small_names.index(nm)].reshape(weights[nm].shape)

    loss = summed[-1][0, 0]
    out = [loss, dx.reshape(x.shape)]
    for kind in range(4):
        out += [result(kind, nm) for nm in names]
    return tuple(out)


def kernel(x, g_mix, w_in, b_in, w_dw, b_dw, ln_g, ln_b, w_pool, s_pool, w_out, g_ffn, w_gate, w_up, w_down, g_final, loss_target, m_g_mix, m_w_in, m_b_in, m_w_dw, m_b_dw, m_ln_g, m_ln_b, m_w_pool, m_s_pool, m_w_out, m_g_ffn, m_w_gate, m_w_up, m_w_down, m_g_final, v_g_mix, v_w_in, v_b_in, v_w_dw, v_b_dw, v_ln_g, v_ln_b, v_w_pool, v_s_pool, v_w_out, v_g_ffn, v_w_gate, v_w_up, v_w_down, v_g_final):
    args = (x, g_mix, w_in, b_in, w_dw, b_dw, ln_g, ln_b, w_pool, s_pool, w_out, g_ffn, w_gate, w_up, w_down, g_final, loss_target, m_g_mix, m_w_in, m_b_in, m_w_dw, m_b_dw, m_ln_g, m_ln_b, m_w_pool, m_s_pool, m_w_out, m_g_ffn, m_w_gate, m_w_up, m_w_down, m_g_final, v_g_mix, v_w_in, v_b_in, v_w_dw, v_b_dw, v_ln_g, v_ln_b, v_w_pool, v_s_pool, v_w_out, v_g_ffn, v_w_gate, v_w_up, v_w_down, v_g_final)
    return _step(args, ts_mix_fwd=512, ts_ffn=256, ts_mix_bwd=512, tm_grad=256)
```

```python
import jax
import jax.numpy as jnp
from jax import lax
from jax.experimental import pallas as pl
from jax.experimental.pallas import tpu as pltpu

F32 = jnp.float32
BF16 = jnp.bfloat16
MESH = pl.DeviceIdType.MESH
N_DEV = 8

C_CONV = 512
CONV_WIDTH = 31
POOL_WINDOWS = (2, 4, 8, 16)
POOL_GROUP = 128
RMS_EPS = 1e-6
LN_EPS = 1e-5

ADAM_LR = 0.001
ADAM_B1 = 0.9
ADAM_B2 = 0.999
ADAM_EPS = 1e-08
ADAM_WD = 0.01
ADAM_STEP = 10

HALO = 32
SUBLANES = 8
LANES = 128
CONV_ROWS = 64
VMEM_LIMIT = 56 * 1024 * 1024


def _dot(a, b):
    return jnp.dot(a, b, preferred_element_type=F32)


def _dot_nt(a, b):
    return lax.dot_general(a, b, (((1,), (1,)), ((), ())), preferred_element_type=F32)


def _dot_tn(a, b):
    return lax.dot_general(a, b, (((0,), (0,)), ((), ())), preferred_element_type=F32)


def _mean_last(v):
    return jnp.mean(v, axis=-1, keepdims=True)


def _full(shape):
    nd = len(shape)
    return pl.BlockSpec(shape, lambda *_: (0,) * nd)


def _full1(shape):
    nd = len(shape)
    return pl.BlockSpec(shape, lambda *_: (0,) * nd, pipeline_mode=pl.Buffered(1))


def _shifted_copies(sh_ref, rows):
    for s in range(1, SUBLANES):
        sh_ref[s, 0:rows, :] = sh_ref[0, s:s + rows, :]


def _tap(sh_ref, off, r0, rows):
    q, s = divmod(off, SUBLANES)
    return sh_ref[s, pl.ds(r0 + q * SUBLANES, rows), :]


def _window_sums(src, bufs, rows, backward):
    assert all(w == 2 << g for g, w in enumerate(POOL_WINDOWS))
    n = len(POOL_WINDOWS)
    out = []
    for level in range(n):
        dst, shift = bufs[level % 2], 1 << level
        lanes = slice(level * POOL_GROUP, n * POOL_GROUP)
        lo, hi = (SUBLANES * (level + 1), rows) if backward else (0, rows - SUBLANES * (level + 1))
        other = slice(lo - shift, hi - shift) if backward else slice(lo + shift, hi + shift)
        dst[lo:hi, lanes] = src[lo:hi, lanes] + src[other, lanes]
        src = dst
        out.append(dst)
    return out


def _mix_fwd(x, g_mix, b_in, b_dw, ln_g, ln_b, w_pool, s_pool, mix_shards, ag_shards, ts):
    S, D = x.shape
    d_in = mix_shards[0].shape[0] * N_DEV
    C = C_CONV
    nt = S // ts
    nrb = ts // CONV_ROWS
    n_ag = len(ag_shards)
    relay_step = nt // 2
    mix_dtypes = [BF16, BF16, F32]

    def body(x_ref, g_ref, bin_ref, bdw_ref, lng_ref, lnb_ref, wp_ref, sp_ref, *rest):
        mx_in, ag_in, rest = rest[:3], rest[3:3 + n_ag], rest[3 + n_ag:]
        a_ref, gate_ref, v_ref, m_ref, y_ref, h1_ref, xn_ref, wdw_ref = rest[:8]
        mx_out, ag_out, rest = rest[8:11], rest[11:11 + n_ag], rest[11 + n_ag:]
        ush, pbuf, pa, pb, win_ref, wout_ref, gdw, load_sems = rest[:8]
        rest = rest[8:]
        (ma_stage, ma_bufs, ma_sems), rest = (rest[:2], rest[2:4], rest[4:7]), rest[7:]
        (mb_stage, mb_bufs, mb_sems), rest = (rest[:1], rest[1:2], rest[2:5]), rest[5:]
        ag_stage, ag_bufs, ag_sems = rest[:n_ag], rest[n_ag:2 * n_ag], rest[2 * n_ag:]
        i = pl.program_id(0)
        ma_start, ma_relay, ma_pass_on, ma_finish = _gather_plan(
            mx_in[0::2], mx_out[0::2], ma_stage, ma_bufs, *ma_sems)
        mb_start, mb_relay, mb_pass_on, mb_finish = _gather_plan(
            mx_in[1:2], mx_out[1:2], mb_stage, mb_bufs, *mb_sems)
        ag_start, ag_relay, ag_pass_on, ag_finish = _gather_plan(ag_in, ag_out, ag_stage, ag_bufs, *ag_sems)

        def load_weight(k, dst_ref):
            rows = mx_out[k].shape[1]
            loads = [pltpu.make_async_copy(mx_out[k].at[j], dst_ref.at[pl.ds(j * rows, rows), :],
                                           load_sems.at[N_DEV * k + j]) for j in range(N_DEV)]
            for cp in loads:
                cp.start()
            for cp in loads:
                cp.wait()

        @pl.when(i == 0)
        def _():
            ma_start()
            mb_start()
            ma_relay()
            mb_relay()
            ag_start()
            ma_pass_on()
            ma_finish()
            load_weight(0, win_ref)
            dw_load = pltpu.make_async_copy(mx_out[2], gdw, load_sems.at[2 * N_DEV])
            dw_load.start()
            dw_load.wait()
            first_half = lax.broadcasted_iota(jnp.int32, gdw.shape[1:], 1) < C // N_DEV
            for p in range(N_DEV // 2):
                wdw_ref[:, LANES * p:LANES * (p + 1)] = jnp.where(
                    first_half, gdw[2 * p], pltpu.roll(gdw[2 * p + 1], C // N_DEV, axis=1))
            ush[0, 0:HALO, :] = jnp.zeros((HALO, C), F32)
            pbuf[0:HALO, :] = jnp.zeros((HALO, C), F32)

        @pl.when(i == relay_step)
        def _():
            ag_relay()

        @pl.when(i == nt - 1)
        def _():
            ag_pass_on()

        x = x_ref[...]
        r1 = lax.rsqrt(_mean_last(x * x) + RMS_EPS)
        xn = (x * r1 * g_ref[...]).astype(BF16)
        xn_ref[...] = xn
        z = _dot_nt(xn, win_ref[...]) + bin_ref[...]
        a = z[:, 0:C]
        gate = z[:, C:2 * C]
        a_ref[...] = a
        gate_ref[...] = gate
        ush[0, HALO:HALO + ts, :] = a * jax.nn.sigmoid(gate)
        pbuf[HALO:HALO + ts, :] = z[:, 2 * C:]

        _shifted_copies(ush, ts + HALO - SUBLANES)

        def conv_block(rb, carry):
            r0 = pl.multiple_of(rb * CONV_ROWS, CONV_ROWS)
            acc = jnp.zeros((CONV_ROWS, C), F32)
            for k in range(CONV_WIDTH):
                acc = acc + wdw_ref[k:k + 1, :] * _tap(ush, HALO - (CONV_WIDTH - 1) + k, r0, CONV_ROWS)
            v_ref[pl.ds(r0, CONV_ROWS), :] = acc + bdw_ref[...]
            return carry

        lax.fori_loop(0, nrb, conv_block, 0)

        v = v_ref[...]
        mu = _mean_last(v)
        xc = v - mu
        rstd = lax.rsqrt(_mean_last(xc * xc) + LN_EPS)
        ln = xc * rstd * lng_ref[...] + lnb_ref[...]
        y_ref[:, 0:C] = (ln * jax.nn.sigmoid(ln)).astype(BF16)

        sums = _window_sums(pbuf, (pa, pb), ts + HALO, backward=True)
        row = lax.broadcasted_iota(jnp.int32, (ts, 1), 0) + i * ts
        for gi, w in enumerate(POOL_WINDOWS):
            lanes = slice(gi * POOL_GROUP, (gi + 1) * POOL_GROUP)
            seg = pbuf[HALO:HALO + ts, lanes]
            ws = sums[gi][HALO:HALO + ts, lanes]
            cnt = jnp.minimum(row + 1, w).astype(F32)
            m = (ws / cnt - seg).astype(BF16)
            m_ref[:, lanes] = m
            ypre = _dot(m, wp_ref[gi])
            y_ref[:, C + gi * POOL_GROUP:C + (gi + 1) * POOL_GROUP] = (ypre * sp_ref[:, lanes]).astype(BF16)

        @pl.when(i == 0)
        def _():
            mb_pass_on()
            mb_finish()
            load_weight(1, wout_ref)

        h1_ref[...] = x + _dot(y_ref[...], wout_ref[...])

        ush[0, 0:HALO, :] = ush[0, ts:ts + HALO, :]
        pbuf[0:HALO, :] = pbuf[ts:ts + HALO, :]

        @pl.when(i == nt - 1)
        def _():
            ag_finish()

    tile = lambda w, dt: (pl.BlockSpec((ts, w), lambda i: (i, 0)), jax.ShapeDtypeStruct((S, w), dt))
    wdw_rows = mix_shards[2].shape[0]
    outs = [tile(C, F32), tile(C, F32), tile(C, F32), tile(C, BF16), tile(D, BF16), tile(D, F32), tile(D, BF16),
            (_full((wdw_rows, C)), jax.ShapeDtypeStruct((wdw_rows, C), F32))]
    return pl.pallas_call(
        body, name="mix_fwd", grid=(nt,),
        in_specs=[pl.BlockSpec((ts, D), lambda i: (i, 0)), _full((1, D)), _full((1, d_in)), _full((1, C)),
                  _full((1, C)), _full((1, C)), _full(w_pool.shape), _full((1, C))] + [_HBM] * (3 + n_ag),
        out_specs=[o[0] for o in outs] + [_HBM] * (3 + n_ag),
        out_shape=[o[1] for o in outs] + _gather_out_shapes(mix_shards, mix_dtypes)
        + _gather_out_shapes(ag_shards, [BF16] * n_ag),
        scratch_shapes=[pltpu.VMEM((SUBLANES, ts + HALO, C), F32)] + [pltpu.VMEM((ts + HALO, C), F32)] * 3
        + [pltpu.VMEM((d_in, D), BF16), pltpu.VMEM((D, D), BF16),
                        pltpu.VMEM((N_DEV,) + mix_shards[2].shape, F32), pltpu.SemaphoreType.DMA((2 * N_DEV + 1,))]
        + _gather_scratch(mix_shards[0::2], mix_dtypes[0::2]) + _gather_scratch(mix_shards[1:2], mix_dtypes[1:2])
        + _gather_scratch(ag_shards, [BF16] * n_ag),
        compiler_params=pltpu.CompilerParams(dimension_semantics=("arbitrary",), vmem_limit_bytes=VMEM_LIMIT),
    )(x, g_mix, b_in, b_dw, ln_g, ln_b, w_pool, s_pool, *mix_shards, *ag_shards)


def _ffn(h1, target, g_ffn, g_final, w_gate, w_up, w_down, ts, f_chunks):
    S, D = h1.shape
    Fd = w_gate.shape[0]
    nt = S // ts
    bounds = []
    lo = 0
    for n in f_chunks:
        bounds.append((lo, lo + n))
        lo += n
    assert lo == Fd

    def body(h1_ref, tgt_ref, gf_ref, gl_ref, wg_ref, wu_ref, wd_ref,
             dh1_ref, hn_ref, act_ref, dgt_ref, dup_ref, dh2_ref, loss_ref, dgl_ref, dgf_ref, gt_s, up_s):
        i = pl.program_id(0)

        @pl.when(i == 0)
        def _():
            loss_ref[...] = jnp.zeros_like(loss_ref)
            dgl_ref[...] = jnp.zeros_like(dgl_ref)
            dgf_ref[...] = jnp.zeros_like(dgf_ref)

        h1 = h1_ref[...]
        r2 = lax.rsqrt(_mean_last(h1 * h1) + RMS_EPS)
        hhat = h1 * r2
        hn = (hhat * gf_ref[...]).astype(BF16)
        hn_ref[...] = hn
        h2 = h1
        for lo, hi in bounds:
            gt = _dot_nt(hn, wg_ref[lo:hi, :])
            up = _dot_nt(hn, wu_ref[lo:hi, :])
            gt_s[:, lo:hi] = gt
            up_s[:, lo:hi] = up
            act = (gt * jax.nn.sigmoid(gt) * up).astype(BF16)
            act_ref[:, lo:hi] = act
            h2 = h2 + _dot(act, wd_ref[lo:hi, :])

        r3 = lax.rsqrt(_mean_last(h2 * h2) + RMS_EPS)
        n3 = h2 * r3
        gl = gl_ref[...]
        diff = n3 * gl - tgt_ref[...]
        loss_ref[...] += jnp.sum(0.5 * jnp.sum(diff * diff, axis=-1, keepdims=True) / D, axis=0, keepdims=True)
        dout = diff / D
        dgl_ref[...] += jnp.sum(dout * n3, axis=0, keepdims=True)
        dn = dout * gl
        dh2 = r3 * (dn - n3 * _mean_last(dn * n3))
        dh2b = dh2.astype(BF16)
        dh2_ref[...] = dh2b

        dhn = jnp.zeros((ts, D), F32)
        for lo, hi in bounds:
            gt = gt_s[:, lo:hi]
            up = up_s[:, lo:hi]
            sg = jax.nn.sigmoid(gt)
            dact = _dot_nt(dh2b, wd_ref[lo:hi, :])
            dgt = (dact * up * (sg * (1.0 + gt * (1.0 - sg)))).astype(BF16)
            dup = (dact * (gt * sg)).astype(BF16)
            dgt_ref[:, lo:hi] = dgt
            dup_ref[:, lo:hi] = dup
            dhn = dhn + _dot(dgt, wg_ref[lo:hi, :]) + _dot(dup, wu_ref[lo:hi, :])

        dgf_ref[...] += jnp.sum(dhn * hhat, axis=0, keepdims=True)
        dnn = dhn * gf_ref[...]
        dh1_ref[...] = dh2 + r2 * (dnn - hhat * _mean_last(dnn * hhat))

    tile = lambda w, dt: (pl.BlockSpec((ts, w), lambda i: (i, 0)), jax.ShapeDtypeStruct((S, w), dt))
    acc = lambda w: (_full((1, w)), jax.ShapeDtypeStruct((1, w), F32))
    outs = [tile(D, F32), tile(D, BF16), tile(Fd, BF16), tile(Fd, BF16), tile(Fd, BF16), tile(D, BF16),
            acc(LANES), acc(D), acc(D)]
    return pl.pallas_call(
        body, name="ffn_fwd_bwd", grid=(nt,),
        in_specs=[pl.BlockSpec((ts, D), lambda i: (i, 0)), pl.BlockSpec((ts, D), lambda i: (i, 0)),
                  _full((1, D)), _full((1, D)), _full1((Fd, D)), _full1((Fd, D)), _full1((Fd, D))],
        out_specs=[o[0] for o in outs], out_shape=[o[1] for o in outs],
        scratch_shapes=[pltpu.VMEM((ts, Fd), F32), pltpu.VMEM((ts, Fd), F32)],
        compiler_params=pltpu.CompilerParams(dimension_semantics=("arbitrary",), vmem_limit_bytes=VMEM_LIMIT),
    )(h1, target, g_ffn, g_final, w_gate, w_up, w_down)


def _mix_bwd(dh1, x, a, gate, v, m, g_mix, w_in, w_dw, ln_g, ln_b, w_pool, s_pool, w_out, rs_parts, ts):
    S, D = x.shape
    n_rs = len(rs_parts)
    d_in = w_in.shape[0]
    C = C_CONV
    nt = S // ts
    nrb = ts // CONV_ROWS
    wrows =((CONV_WIDTH + SUBLANES - 1) // SUBLANES) * SUBLANES

    def body(dh1_ref, x_ref, a_ref, gate_ref, v_ref, m_ref, g_ref, win_ref, wdw_ref, lng_ref,
             lnb_ref, wp_ref, sp_ref, wout_ref, *rest):
        rs_in, rest = rest[:n_rs], rest[n_rs:]
        (dx_ref, dz_ref, dh1b_ref, dgm_ref, dbin_ref, dwdw_ref, dbdw_ref, dlng_ref, dlnb_ref, dwp_ref,
         dsp_ref) = rest[:11]
        rs_out, rest = rest[11:11 + n_rs], rest[11 + n_rs:]
        dvsh, dqbuf, qa, qb, du_s, dm_s = rest[:6]
        rs_bufs, (send_sems, recv_sems, local_sems) = rest[6:6 + n_rs], rest[6 + n_rs:]
        i = pl.program_id(0)
        t = nt - 1 - i
        rs_start, rs_finish = _chip_exchange_plan(rs_in, rs_out, rs_bufs, send_sems, recv_sems, local_sems)

        @pl.when(i == 0)
        def _():
            rs_start()
            dvsh[0, ts:ts + HALO, :] = jnp.zeros((HALO, C), F32)
            dqbuf[ts:ts + HALO, :] = jnp.zeros((HALO, C), F32)
            for r in (dgm_ref, dbin_ref, dwdw_ref, dbdw_ref, dlng_ref, dlnb_ref, dwp_ref, dsp_ref):
                r[...] = jnp.zeros_like(r)

        dh1 = dh1_ref[...]
        dh1b = dh1.astype(BF16)
        dh1b_ref[...] = dh1b
        dy = _dot_nt(dh1b, wout_ref[...])

        v = v_ref[...]
        mu = _mean_last(v)
        xc = v - mu
        rstd = lax.rsqrt(_mean_last(xc * xc) + LN_EPS)
        vhat = xc * rstd
        lng = lng_ref[...]
        ln = vhat * lng + lnb_ref[...]
        sg = jax.nn.sigmoid(ln)
        dln = dy[:, 0:C] * (sg * (1.0 + ln * (1.0 - sg)))
        dlng_ref[...] += jnp.sum(dln * vhat, axis=0, keepdims=True)
        dlnb_ref[...] += jnp.sum(dln, axis=0, keepdims=True)
        dvh = dln * lng
        dv = rstd * (dvh - _mean_last(dvh) - vhat * _mean_last(dvh * vhat))
        dbdw_ref[...] += jnp.sum(dv, axis=0, keepdims=True)
        dvsh[0, 0:ts, :] = dv
        _shifted_copies(dvsh, ts + HALO - SUBLANES)

        def conv_block(rb, carry):
            r0 = pl.multiple_of(rb * CONV_ROWS, CONV_ROWS)
            acc = jnp.zeros((CONV_ROWS, C), F32)
            for k in range(CONV_WIDTH):
                acc = acc + wdw_ref[k:k + 1, :] * _tap(dvsh, CONV_WIDTH - 1 - k, r0, CONV_ROWS)
            du_s[pl.ds(r0, CONV_ROWS), :] = acc
            return carry

        lax.fori_loop(0, nrb, conv_block, 0)

        a = a_ref[...]
        sgate = jax.nn.sigmoid(gate_ref[...])
        u = a * sgate
        for k in range(CONV_WIDTH):
            q, s = divmod(CONV_WIDTH - 1 - k, SUBLANES)
            prod = u * dvsh[s, q * SUBLANES:q * SUBLANES + ts, :]
            dwdw_ref[k:k + 1, :] += jnp.sum(prod, axis=0, keepdims=True)

        du = du_s[...]
        da = du * sgate
        dgate = du * a * sgate * (1.0 - sgate)
        dz_ref[:, 0:C] = da.astype(BF16)
        dz_ref[:, C:2 * C] = dgate.astype(BF16)
        dbin_ref[:, 0:C] += jnp.sum(da, axis=0, keepdims=True)
        dbin_ref[:, C:2 * C] += jnp.sum(dgate, axis=0, keepdims=True)

        row = lax.broadcasted_iota(jnp.int32, (ts, 1), 0) + t * ts
        for gi, w in enumerate(POOL_WINDOWS):
            lanes = slice(gi * POOL_GROUP, (gi + 1) * POOL_GROUP)
            dyp = dy[:, C + gi * POOL_GROUP:C + (gi + 1) * POOL_GROUP]
            mg = m_ref[:, lanes]
            ypre = _dot(mg, wp_ref[gi])
            dsp_ref[:, lanes] += jnp.sum(dyp * ypre, axis=0, keepdims=True)
            dyi = (dyp * sp_ref[:, lanes]).astype(BF16)
            dwp_ref[gi] += _dot_tn(mg, dyi)
            dm = _dot_nt(dyi, wp_ref[gi])
            cnt = jnp.minimum(row + 1, w).astype(F32)
            dqbuf[0:ts, lanes] = dm / cnt
            dm_s[:, lanes] = dm
        sums = _window_sums(dqbuf, (qa, qb), ts + HALO, backward=False)
        for gi in range(len(POOL_WINDOWS)):
            lanes = slice(gi * POOL_GROUP, (gi + 1) * POOL_GROUP)
            dp = sums[gi][0:ts, lanes] - dm_s[:, lanes]
            dz_ref[:, 2 * C + gi * POOL_GROUP:2 * C + (gi + 1) * POOL_GROUP] = dp.astype(BF16)
            dbin_ref[:, 2 * C + gi * POOL_GROUP:2 * C + (gi + 1) * POOL_GROUP] += jnp.sum(dp, axis=0, keepdims=True)

        dxn = _dot(dz_ref[...], win_ref[...])
        x = x_ref[...]
        r1 = lax.rsqrt(_mean_last(x * x) + RMS_EPS)
        xhat = x * r1
        dgm_ref[...] += jnp.sum(dxn * xhat, axis=0, keepdims=True)
        dnn = dxn * g_ref[...]
        dx_ref[...] = dh1 + r1 * (dnn - xhat * _mean_last(dnn * xhat))

        dvsh[0, ts:ts + HALO, :] = dvsh[0, 0:HALO, :]
        dqbuf[ts:ts + HALO, :] = dqbuf[0:HALO, :]

        @pl.when(i == nt - 1)
        def _():
            rs_finish()

    rev = lambda w: pl.BlockSpec((ts, w), lambda i: (nt - 1 - i, 0))
    acc = lambda shape: (_full(shape), jax.ShapeDtypeStruct(shape, F32))
    outs = [(rev(D), jax.ShapeDtypeStruct((S, D), F32)), (rev(d_in), jax.ShapeDtypeStruct((S, d_in), BF16)),
            (rev(D), jax.ShapeDtypeStruct((S, D), BF16)), acc((1, D)), acc((1, d_in)), acc((wrows, C)),
            acc((1, C)), acc((1, C)), acc((1, C)), acc(w_pool.shape), acc((1, C))]
    return pl.pallas_call(
        body, name="mix_bwd", grid=(nt,),
        in_specs=[rev(D), rev(D), rev(C), rev(C), rev(C), rev(C), _full((1, D)), _full((d_in, D)),
                  _full(w_dw.shape), _full((1, C)), _full((1, C)), _full(w_pool.shape), _full((1, C)),
                  _full((D, D))] + [_HBM] * n_rs,
        out_specs=[o[0] for o in outs] + [_HBM] * n_rs,
        out_shape=[o[1] for o in outs] + [jax.ShapeDtypeStruct(p.shape, p.dtype) for p in rs_parts],
        scratch_shapes=[pltpu.VMEM((SUBLANES, ts + HALO, C), F32)] + [pltpu.VMEM((ts + HALO, C), F32)] * 3
        + [pltpu.VMEM((ts, C), F32)] * 2
        + [pltpu.VMEM(p.shape[1:], p.dtype) for p in rs_parts] + _comm_sems(n_rs, 3),
        compiler_params=pltpu.CompilerParams(dimension_semantics=("arbitrary",), vmem_limit_bytes=VMEM_LIMIT),
    )(dh1, x, a, gate, v, m, g_mix, w_in, w_dw, ln_g, ln_b, w_pool, s_pool, w_out, *rs_parts)


def _grad_matmul(a, b, tm, name, reduce_parts=(), reduce_vals=()):
    S, M = a.shape
    N = b.shape[1]
    steps = M // tm
    n, ns = len(reduce_parts), len(reduce_vals)

    def body(a_ref, b_ref, *rest):
        o_ref = rest[n + ns]
        if n + ns:
            start, middle, finish = _reduce_plan(rest[:n], rest[n:n + ns], rest[n + ns + 1:2 * n + ns + 1],
                                                 rest[2 * n + ns + 1:2 * (n + ns) + 1], rest[2 * (n + ns) + 1:])
            i = pl.program_id(0)
            pl.when(i == 0)(start)
            pl.when(i == 1)(middle)
        o_ref[...] = _dot_tn(a_ref[...], b_ref[...]).astype(BF16)
        if n + ns:
            pl.when(i == steps - 1)(finish)

    res = pl.pallas_call(
        body, name=name, grid=(steps,),
        in_specs=[pl.BlockSpec((S, tm), lambda i: (0, i)), _full1((S, N))] + [_VMEM] * (n + ns),
        out_specs=[pl.BlockSpec((tm, N), lambda i: (i, 0))] + [_VMEM] * (n + ns),
        out_shape=[jax.ShapeDtypeStruct((M, N), BF16)] + _reduce_out_shapes(reduce_parts, reduce_vals),
        scratch_shapes=_reduce_scratch(reduce_parts, reduce_vals) if n + ns else [],
        compiler_params=pltpu.CompilerParams(dimension_semantics=("arbitrary",) if n + ns else ("parallel",),
                                             vmem_limit_bytes=VMEM_LIMIT),
    )(a, b, *reduce_parts, *reduce_vals)
    return res if n + ns else res[0]


def _grad_matmul_pair_exchange(a, b, tm, name, parts):
    S, M = a.shape
    N = b.shape[1]
    steps = M // tm
    n = len(parts)

    def body(a_ref, b_ref, *rest):
        o_ref = rest[n]
        start, finish = _pair_exchange_plan(rest[:n], rest[n + 1:2 * n + 1], *rest[2 * n + 1:])
        i = pl.program_id(0)
        pl.when(i == 0)(start)
        o_ref[...] = _dot_tn(a_ref[...], b_ref[...]).astype(BF16)
        pl.when(i == steps - 1)(finish)

    sems = pltpu.SemaphoreType.DMA((n, N_CHIPS))
    return pl.pallas_call(
        body, name=name, grid=(steps,),
        in_specs=[pl.BlockSpec((S, tm), lambda i: (0, i)), _full1((S, N))] + [_HBM] * n,
        out_specs=[pl.BlockSpec((tm, N), lambda i: (i, 0))] + [_HBM] * n,
        out_shape=[jax.ShapeDtypeStruct((M, N), BF16)] + _pair_out_shapes(parts),
        scratch_shapes=[sems, sems],
        compiler_params=pltpu.CompilerParams(dimension_semantics=("arbitrary",), vmem_limit_bytes=VMEM_LIMIT),
    )(a, b, *parts)


def _position():
    return lax.axis_index("x"), lax.axis_index("y"), lax.axis_index("c")


def _slot(px, py, pc):
    return 4 * px + 2 * py + pc


_HBM = pl.BlockSpec(memory_space=pl.ANY)


def _comm_sems(n, copies):
    return [pltpu.SemaphoreType.DMA((n, copies)), pltpu.SemaphoreType.DMA((n, copies)),
            pltpu.SemaphoreType.DMA((n, 2))]


def _local_copy(srcs, dsts, bufs, local_sems):
    n = len(srcs)
    loads = [pltpu.make_async_copy(srcs[k], bufs[k], local_sems.at[k, 0]) for k in range(n)]
    for cp in loads:
        cp.start()
    for cp in loads:
        cp.wait()
    stores = _local_stores(dsts, bufs, local_sems)
    for cp in stores:
        cp.start()
    return stores


def _local_stores(dsts, bufs, local_sems):
    return [pltpu.make_async_copy(bufs[k], dsts[k], local_sems.at[k, 1]) for k in range(len(dsts))]


def _gather_out_shapes(shards, dtypes):
    return [jax.ShapeDtypeStruct((N_DEV,) + s.shape, dt) for s, dt in zip(shards, dtypes)]


def _gather_scratch(shards, dtypes):
    return ([pltpu.VMEM(s.shape, s.dtype) for s in shards] + [pltpu.VMEM(s.shape, dt) for s, dt in zip(shards, dtypes)]
            + _comm_sems(len(shards), 7))


def _gather_plan(ins, outs, stage, bufs, send_sems, recv_sems, local_sems):
    n = len(ins)
    x, y, c = _position()
    me, sibling = (x, y, c), (x, y, 1 - c)
    na, nb, dg = (x ^ (1 - c), y ^ c), (x ^ c, y ^ (1 - c)), (1 - x, 1 - y)
    own = [outs[k].at[_slot(*me)] for k in range(n)]

    def copy(k, sem, block, to, src=None):
        dst = outs[k].at[_slot(*block)]
        return pltpu.make_async_remote_copy(
            src_ref=dst if src is None else src, dst_ref=dst, send_sem=send_sems.at[k, sem],
            recv_sem=recv_sems.at[k, sem], device_id=to, device_id_type=MESH)

    def first():
        cps = []
        for k in range(n):
            cps += [copy(k, 0, me, sibling, src=bufs[k]), copy(k, 1, me, (*na, c), src=bufs[k]),
                    copy(k, 2, me, (*nb, c), src=bufs[k])]
        return cps

    def onward():
        return [copy(k, 3, (*na, c), (*nb, c)) for k in range(n)]

    def to_sibling(j, chip):
        return [copy(k, 4 + j, (*chip, c), sibling) for k in range(n)]

    def start():
        loads = [pltpu.make_async_copy(ins[k], stage[k], local_sems.at[k, 0]) for k in range(n)]
        for cp in loads:
            cp.start()
        for cp in loads:
            cp.wait()
        for k in range(n):
            bufs[k][...] = stage[k][...].astype(bufs[k].dtype)
        for cp in first() + _local_stores(own, bufs, local_sems):
            cp.start()

    def relay():
        for k in range(n):
            copy(k, 1, (*na, c), me).wait_recv()
        for cp in onward() + to_sibling(0, na):
            cp.start()
        for k in range(n):
            copy(k, 2, (*nb, c), me).wait_recv()
        for cp in to_sibling(1, nb):
            cp.start()

    def pass_on():
        for k in range(n):
            copy(k, 3, (*dg, c), me).wait_recv()
        for cp in to_sibling(2, dg):
            cp.start()

    def finish():
        for k in range(n):
            copy(k, 0, sibling, me).wait_recv()
            for j, chip in enumerate((nb, na, dg)):
                copy(k, 4 + j, (*chip, 1 - c), me).wait_recv()
        for cp in first() + onward() + to_sibling(0, na) + to_sibling(1, nb) + to_sibling(2, dg):
            cp.wait_send()
        for cp in _local_stores(own, bufs, local_sems):
            cp.wait()

    return start, relay, pass_on, finish


N_CHIPS = 4
_CHIP_FLIPS = [(1, 0), (0, 1), (1, 1)]
F32_TRAVEL_LIMIT = 4096


def _chip_exchange_plan(ins, outs, bufs, send_sems, recv_sems, local_sems):
    n = len(ins)
    x, y, c = _position()
    my_q = 2 * x + y
    peers = [(x ^ fx, y ^ fy) for fx, fy in _CHIP_FLIPS]
    own = [outs[k].at[my_q] for k in range(n)]

    def sends():
        return [pltpu.make_async_remote_copy(
            src_ref=ins[k].at[2 * px + py], dst_ref=outs[k].at[my_q], send_sem=send_sems.at[k, f],
            recv_sem=recv_sems.at[k, f], device_id=(px, py, c), device_id_type=MESH)
            for f, (px, py) in enumerate(peers) for k in range(n)]

    def start():
        for cp in sends():
            cp.start()
        _local_copy([ins[k].at[my_q] for k in range(n)], own, bufs, local_sems)

    def finish():
        for f, (px, py) in enumerate(peers):
            for k in range(n):
                pltpu.make_async_remote_copy(
                    src_ref=ins[k].at[my_q], dst_ref=outs[k].at[2 * px + py], send_sem=send_sems.at[k, f],
                    recv_sem=recv_sems.at[k, f], device_id=(px, py, c), device_id_type=MESH).wait_recv()
        for cp in sends():
            cp.wait_send()
        for cp in _local_stores(own, bufs, local_sems):
            cp.wait()

    return start, finish


def _pair_exchange(parts):
    n = len(parts)

    def body(*refs):
        start, finish = _pair_exchange_plan(refs[:n], refs[n:2 * n], *refs[2 * n:])
        start()
        finish()

    sems = pltpu.SemaphoreType.DMA((n, N_CHIPS))
    return pl.pallas_call(
        body, name="pair_exchange", out_shape=_pair_out_shapes(parts),
        in_specs=[_HBM] * n, out_specs=[_HBM] * n, scratch_shapes=[sems, sems],
    )(*parts)


def _pair_out_shapes(parts):
    return [jax.ShapeDtypeStruct((N_CHIPS,) + p.shape[1:], p.dtype) for p in parts]


def _pair_exchange_plan(ins, sib, send_sems, recv_sems):
    x, y, c = _position()

    def copies():
        return [pltpu.make_async_remote_copy(
            src_ref=ins[k].at[2 * q + 1 - c], dst_ref=sib[k].at[q], send_sem=send_sems.at[k, q],
            recv_sem=recv_sems.at[k, q], device_id=(x, y, 1 - c), device_id_type=MESH)
            for k in range(len(ins)) for q in range(N_CHIPS)]

    def start():
        for cp in copies():
            cp.start()

    def finish():
        for cp in copies():
            cp.wait()

    return start, finish


def _pair_sum(parts, sib):
    n = len(parts)

    def body(*refs):
        c = lax.axis_index("c")
        for k in range(n):
            refs[2 * n + k][0] = (refs[k][0, c].astype(F32) + refs[n + k][0].astype(F32)).astype(BF16)

    pair = [pl.BlockSpec((1, 2) + p.shape[1:], lambda q: (q, 0, 0, 0)) for p in parts]
    one = [pl.BlockSpec((1,) + p.shape[1:], lambda q: (q, 0, 0)) for p in parts]
    return pl.pallas_call(
        body, name="pair_sum", grid=(N_CHIPS,), in_specs=pair + one, out_specs=one,
        out_shape=[jax.ShapeDtypeStruct(s.shape, BF16) for s in sib],
        compiler_params=pltpu.CompilerParams(dimension_semantics=("parallel",)),
    )(*[p.reshape((N_CHIPS, 2) + p.shape[1:]) for p in parts], *sib)


def _tail_reduce(parts, vals):
    n, ns = len(parts), len(vals)

    def body(*refs):
        start, middle, finish = _reduce_plan(refs[:n], refs[n:n + ns], refs[n + ns:2 * n + ns],
                                             refs[2 * n + ns:2 * (n + ns)], refs[2 * (n + ns):])
        start()
        middle()
        finish()

    return pl.pallas_call(
        body, name="tail_reduce", out_shape=_reduce_out_shapes(parts, vals),
        in_specs=[_VMEM] * (n + ns), out_specs=[_VMEM] * (n + ns), scratch_shapes=_reduce_scratch(parts, vals),
        compiler_params=pltpu.CompilerParams(vmem_limit_bytes=VMEM_LIMIT),
    )(*parts, *vals)


_VMEM = pl.BlockSpec(memory_space=pltpu.VMEM)


def _reduce_out_shapes(parts, vals):
    return ([jax.ShapeDtypeStruct((N_CHIPS,) + p.shape[1:], p.dtype) for p in parts]
            + [jax.ShapeDtypeStruct(v.shape, v.dtype) for v in vals])


def _reduce_scratch(parts, vals):
    n, ns = len(parts), len(vals)
    quarter = [pltpu.VMEM((N_CHIPS,) + p.shape[1:], p.dtype) for p in parts]
    dma = pltpu.SemaphoreType.DMA
    travel = [BF16 if v.size > F32_TRAVEL_LIMIT else v.dtype for v in vals]
    return (quarter * 3 + [pltpu.VMEM(v.shape, v.dtype) for v in vals]
            + [pltpu.VMEM(v.shape, t) for v, t in zip(vals, travel)]
            + [pltpu.VMEM((N_CHIPS,) + v.shape, t) for v, t in zip(vals, travel)]
            + [dma((max(n, 1), N_CHIPS)), dma((max(n, 1), N_CHIPS)), dma((max(ns, 1),)), dma((max(ns, 1),)),
               dma((max(n, 1), 3)), dma((max(n, 1), 3)), dma((max(ns, 1), 3)), dma((max(ns, 1), 3))])


def _reduce_plan(p_in, v_in, p_out, v_out, scratch):
    n, ns = len(p_in), len(v_in)
    p_sib, p_sum, p_all = scratch[:n], scratch[n:2 * n], scratch[2 * n:3 * n]
    v_sib, v_sum, v_all = (scratch[3 * n + j * ns:3 * n + (j + 1) * ns] for j in range(3))
    p1_send, p1_recv, v1_send, v1_recv, p3_send, p3_recv, v3_send, v3_recv = scratch[3 * n + 3 * ns:]
    x, y, c = _position()
    my_q = 2 * x + y
    peers = [(x ^ fx, y ^ fy) for fx, fy in _CHIP_FLIPS]

    def to_sibling(src, dst, send, recv):
        return pltpu.make_async_remote_copy(src_ref=src, dst_ref=dst, send_sem=send, recv_sem=recv,
                                            device_id=(x, y, 1 - c), device_id_type=MESH)

    def level1():
        cps = [to_sibling(p_in[k].at[2 * q + 1 - c], p_sib[k].at[q], p1_send.at[k, q], p1_recv.at[k, q])
               for k in range(n) for q in range(N_CHIPS)]
        return cps + [to_sibling(v_in[k], v_sib[k], v1_send.at[k], v1_recv.at[k]) for k in range(ns)]

    def to_chip(f, src, dst, send, recv):
        px, py = peers[f]
        return pltpu.make_async_remote_copy(src_ref=src, dst_ref=dst, send_sem=send, recv_sem=recv,
                                            device_id=(px, py, c), device_id_type=MESH)

    def level2(sending):
        cps = []
        for f, (px, py) in enumerate(peers):
            their_q = 2 * px + py
            for k in range(n):
                src, dst = (p_sum[k].at[their_q], p_all[k].at[my_q]) if sending else (
                    p_sum[k].at[my_q], p_all[k].at[their_q])
                cps.append(to_chip(f, src, dst, p3_send.at[k, f], p3_recv.at[k, f]))
            for k in range(ns):
                dst = v_all[k].at[my_q] if sending else v_all[k].at[their_q]
                cps.append(to_chip(f, v_sum[k], dst, v3_send.at[k, f], v3_recv.at[k, f]))
        return cps

    def start():
        for cp in level1():
            cp.start()

    def middle():
        for cp in level1():
            cp.wait_recv()
        for k in range(n):
            for q in range(N_CHIPS):
                p_sum[k][q] = (p_in[k][2 * q + c].astype(F32) + p_sib[k][q].astype(F32)).astype(p_sum[k].dtype)
        for k in range(ns):
            v_sum[k][...] = (v_in[k][...] + v_sib[k][...]).astype(v_sum[k].dtype)
        for cp in level2(True):
            cp.start()
        for k in range(n):
            p_all[k][my_q] = p_sum[k][my_q]
        for k in range(ns):
            v_all[k][my_q] = v_sum[k][...]

    def finish():
        for cp in level2(False):
            cp.wait_recv()
        for k in range(n):
            p_out[k][...] = p_all[k][...]
        for k in range(ns):
            total = v_all[k][0].astype(F32)
            for q in range(1, N_CHIPS):
                total = total + v_all[k][q].astype(F32)
            v_out[k][...] = total
        for cp in level1() + level2(True):
            cp.wait_send()

    return start, middle, finish


def _adamw_math(w, g, m, v):
    m = ADAM_B1 * m + (1.0 - ADAM_B1) * g
    v = ADAM_B2 * v + (1.0 - ADAM_B2) * (g * g)
    m_hat = m / (1.0 - ADAM_B1 ** ADAM_STEP)
    v_hat = v / (1.0 - ADAM_B2 ** ADAM_STEP)
    delta = -ADAM_LR * (m_hat / (jnp.sqrt(v_hat) + ADAM_EPS) + ADAM_WD * w)
    return delta, m, v


def _adamw_shards(parts, ws, ms, vs, steps, name):
    n = len(ws)

    def body(*refs):
        ins, outs = refs[:4 * n], refs[4 * n:]
        for k in range(n):
            p_ref, w_ref, m_ref, v_ref = ins[4 * k:4 * k + 4]
            g = p_ref[0].astype(F32)
            for j in range(1, N_CHIPS):
                g = g + p_ref[j].astype(F32)
            d, mn, vn = _adamw_math(w_ref[...], g, m_ref[...], v_ref[...])
            for o, val in zip(outs[4 * k:4 * k + 4], (g, d, mn, vn)):
                o[...] = val

    in_specs, out_specs, out_shape, operands = [], [], [], []
    for p, w, m, v in zip(parts, ws, ms, vs):
        R, Cc = w.shape
        blk = pl.BlockSpec((R // steps, Cc), lambda i: (i, 0))
        in_specs += [pl.BlockSpec((N_CHIPS, R // steps, Cc), lambda i: (0, i, 0)), blk, blk, blk]
        out_specs += [blk] * 4
        out_shape += [jax.ShapeDtypeStruct((R, Cc), F32)] * 4
        operands += [p, w, m, v]
    res = pl.pallas_call(
        body, name=name, grid=(steps,), in_specs=in_specs, out_specs=out_specs, out_shape=out_shape,
        compiler_params=pltpu.CompilerParams(dimension_semantics=("parallel",), vmem_limit_bytes=VMEM_LIMIT),
    )(*operands)
    return [res[4 * k:4 * k + 4] for k in range(n)]


def _adamw_small(grads, ws, ms, vs):
    n = len(grads)
    vm = pl.BlockSpec(memory_space=pltpu.VMEM)

    def body(*refs):
        g_in, w_in, m_in, v_in = (refs[k * n:(k + 1) * n] for k in range(4))
        g_out, d_out, m_out, v_out = (refs[(4 + k) * n:(5 + k) * n] for k in range(4))
        for k in range(n):
            g = g_in[k][...]
            d, mn, vn = _adamw_math(w_in[k][...], g, m_in[k][...], v_in[k][...])
            g_out[k][...] = g
            d_out[k][...] = d
            m_out[k][...] = mn
            v_out[k][...] = vn

    shapes = [jax.ShapeDtypeStruct(g.shape, F32) for g in grads]
    return pl.pallas_call(
        body, name="adamw_small", out_shape=shapes * 4, in_specs=[vm] * (4 * n), out_specs=[vm] * (4 * n),
    )(*grads, *ws, *ms, *vs)


def _shard_cols(full):
    R, Ct = full.shape
    return jnp.transpose(full.reshape(R, N_DEV, Ct // N_DEV), (1, 0, 2))


_COL_SHARDED = ("w_in", "w_gate", "w_up")
_BIG = ("w_in", "w_out", "w_gate", "w_up", "w_down")


def _rows(nm, p):
    return p[0].T if nm in _COL_SHARDED else p[0]


def _step(args, ts_mix_fwd, ts_ffn, ts_mix_bwd, tm_grad):
    (x, g_mix, w_in, b_in, w_dw, b_dw, ln_g, ln_b, w_pool, s_pool, w_out, g_ffn, w_gate, w_up, w_down, g_final,
     loss_target) = args[:17]
    names = ["g_mix", "w_in", "b_in", "w_dw", "b_dw", "ln_g", "ln_b", "w_pool", "s_pool", "w_out", "g_ffn",
             "w_gate", "w_up", "w_down", "g_final"]
    weights = dict(zip(names, args[1:16]))
    moms = dict(zip(names, args[17:32]))
    vars_ = dict(zip(names, args[32:47]))

    S, D = x.shape[1], x.shape[2]
    x2 = x.reshape(S, D)
    tgt2 = loss_target.reshape(S, D)

    shard = lambda nm: _rows(nm, weights[nm])
    w_pool_b = w_pool[0].astype(BF16)
    dw_rows = -(-CONV_WIDTH // SUBLANES) * SUBLANES
    dw_shard = jnp.pad(w_dw[0], ((0, dw_rows - CONV_WIDTH), (0, LANES - w_dw.shape[2])))

    a, gate, v, m, y, h1, xn, w_dw_f, g_in, g_out, _, g_gate, g_up, g_down = _mix_fwd(
        x2, g_mix, b_in, b_dw, ln_g, ln_b, w_pool_b, s_pool, [shard("w_in"), shard("w_out"), dw_shard],
        [shard("w_gate"), shard("w_up"), shard("w_down")], ts_mix_fwd)
    wt_in, w_out_f = g_in.reshape(-1, D), g_out.reshape(-1, D)
    wt_gate, wt_up, w_down_f = g_gate.reshape(-1, D), g_up.reshape(-1, D), g_down.reshape(-1, D)
    Fd = wt_gate.shape[0]
    f_chunks = [1024] * (Fd // 1024) + ([Fd % 1024] if Fd % 1024 else [])
    dh1, hn, act, dgt, dup, dh2, loss_p, dg_final, dg_ffn = _ffn(
        h1, tgt2, g_ffn, g_final.reshape(1, D), wt_gate, wt_up, w_down_f, ts_ffn, f_chunks)

    by_shard = lambda g: g.reshape(N_DEV, -1, D)
    p_gate = by_shard(_grad_matmul(dgt, hn, tm_grad, "grad_w_gate"))
    g_up, s_gate = _grad_matmul_pair_exchange(dup, hn, tm_grad, "grad_w_up", [p_gate])
    p_up = by_shard(g_up)
    g_down, s_up = _grad_matmul_pair_exchange(act, dh2, tm_grad, "grad_w_down", [p_up])
    p_down = by_shard(g_down)
    s_down, = _pair_exchange([p_down])
    pair_sums = _pair_sum([p_gate, p_up, p_down], [s_gate, s_up, s_down])
    (dx, dz, dh1b, dg_mix, db_in, dw_dw, db_dw, dln_g, dln_b, dw_pool, ds_pool, r_gate, r_up, r_down) = _mix_bwd(
        dh1, x2, a, gate, v, m, g_mix, wt_in, w_dw_f, ln_g, ln_b, w_pool_b, s_pool, w_out_f, pair_sums, ts_mix_bwd)

    small_names = ["g_mix", "b_in", "b_dw", "ln_g", "ln_b", "w_pool", "s_pool", "g_ffn", "g_final"]
    small_shape = lambda p: p.reshape(-1, p.shape[-1])
    partial = [dg_mix, db_in, db_dw, dln_g, dln_b, dw_pool.reshape(-1, POOL_GROUP), ds_pool, dg_ffn, dg_final, loss_p]
    dw_out, *summed = _grad_matmul(y, dh1b, tm_grad, "grad_w_out", reduce_vals=partial)
    dwt_in, r_out, r_dw = _grad_matmul(
        dz, xn, tm_grad, "grad_w_in",
        reduce_parts=[dw_out.reshape(N_DEV, -1, D), _shard_cols(dw_dw[0:CONV_WIDTH])])
    r_in, = _tail_reduce([dwt_in.reshape(N_DEV, -1, D)], [])

    big = {}
    recv = dict(zip(_BIG, [r_in, r_out, r_gate, r_up, r_down]))
    for call, group in (("adamw_ffn", ("w_gate", "w_up", "w_down")), ("adamw_mix", ("w_in", "w_out"))):
        results = _adamw_shards([recv[nm] for nm in group], [_rows(nm, weights[nm]) for nm in group],
                                [_rows(nm, moms[nm]) for nm in group], [_rows(nm, vars_[nm]) for nm in group], 2, call)
        for nm, res in zip(group, results):
            big[nm] = [o.T if nm in _COL_SHARDED else o for o in res]
    big["w_dw"], = _adamw_shards([r_dw], [w_dw[0]], [moms["w_dw"][0]], [vars_["w_dw"][0]], 1, "adamw_w_dw")

    sm = _adamw_small(summed[:-1], [small_shape(weights[nm]) for nm in small_names],
                      [small_shape(moms[nm]) for nm in small_names], [small_shape(vars_[nm]) for nm in small_names])
    n_small = len(small_names)

    def result(kind, nm):
        if nm in big:
            return big[nm][kind].reshape(weights[nm].shape)
        return sm[kind * n_small + small_names.index(nm)].reshape(weights[nm].shape)

    loss = summed[-1][0, 0]
    out = [loss, dx.reshape(x.shape)]
    for kind in range(4):
        out += [result(kind, nm) for nm in names]
    return tuple(out)


def kernel(x, g_mix, w_in, b_in, w_dw, b_dw, ln_g, ln_b, w_pool, s_pool, w_out, g_ffn, w_gate, w_up, w_down, g_final, loss_target, m_g_mix, m_w_in, m_b_in, m_w_dw, m_b_dw, m_ln_g, m_ln_b, m_w_pool, m_s_pool, m_w_out, m_g_ffn, m_w_gate, m_w_up, m_w_down, m_g_final, v_g_mix, v_w_in, v_b_in, v_w_dw, v_b_dw, v_ln_g, v_ln_b, v_w_pool, v_s_pool, v_w_out, v_g_ffn, v_w_gate, v_w_up, v_w_down, v_g_final):
    args = (x, g_mix, w_in, b_in, w_dw, b_dw, ln_g, ln_b, w_pool, s_pool, w_out, g_ffn, w_gate, w_up, w_down, g_final, loss_target, m_g_mix, m_w_in, m_b_in, m_w_dw, m_b_dw, m_ln_g, m_ln_b, m_w_pool, m_s_pool, m_w_out, m_g_ffn, m_w_gate, m_w_up, m_w_down, m_g_final, v_g_mix, v_w_in, v_b_in, v_w_dw, v_b_dw, v_ln_g, v_ln_b, v_w_pool, v_s_pool, v_w_out, v_g_ffn, v_w_gate, v_w_up, v_w_down, v_g_final)
    return _step(args, ts_mix_fwd=512, ts_ffn=256, ts_mix_bwd=512, tm_grad=256)
```

```python
import jax
import jax.numpy as jnp
from jax import lax
from jax.experimental import pallas as pl
from jax.experimental.pallas import tpu as pltpu

F32 = jnp.float32
BF16 = jnp.bfloat16
MESH = pl.DeviceIdType.MESH
N_DEV = 8

C_CONV = 512
CONV_WIDTH = 31
POOL_WINDOWS = (2, 4, 8, 16)
POOL_GROUP = 128
RMS_EPS = 1e-6
LN_EPS = 1e-5

ADAM_LR = 0.001
ADAM_B1 = 0.9
ADAM_B2 = 0.999
ADAM_EPS = 1e-08
ADAM_WD = 0.01
ADAM_STEP = 10

HALO = 32
SUBLANES = 8
LANES = 128
CONV_ROWS = 64
VMEM_LIMIT = 56 * 1024 * 1024


def _dot(a, b):
    return jnp.dot(a, b, preferred_element_type=F32)


def _dot_nt(a, b):
    return lax.dot_general(a, b, (((1,), (1,)), ((), ())), preferred_element_type=F32)


def _dot_tn(a, b):
    return lax.dot_general(a, b, (((0,), (0,)), ((), ())), preferred_element_type=F32)


def _mean_last(v):
    return jnp.mean(v, axis=-1, keepdims=True)


def _full(shape):
    nd = len(shape)
    return pl.BlockSpec(shape, lambda *_: (0,) * nd)


def _full1(shape):
    nd = len(shape)
    return pl.BlockSpec(shape, lambda *_: (0,) * nd, pipeline_mode=pl.Buffered(1))


def _shifted_copies(sh_ref, rows):
    for s in range(1, SUBLANES):
        sh_ref[s, 0:rows, :] = sh_ref[0, s:s + rows, :]


def _tap(sh_ref, off, r0, rows):
    q, s = divmod(off, SUBLANES)
    return sh_ref[s, pl.ds(r0 + q * SUBLANES, rows), :]


def _window_sums(src, bufs, rows, backward):
    assert all(w == 2 << g for g, w in enumerate(POOL_WINDOWS))
    n = len(POOL_WINDOWS)
    out = []
    for level in range(n):
        dst, shift = bufs[level % 2], 1 << level
        lanes = slice(level * POOL_GROUP, n * POOL_GROUP)
        lo, hi = (SUBLANES * (level + 1), rows) if backward else (0, rows - SUBLANES * (level + 1))
        other = slice(lo - shift, hi - shift) if backward else slice(lo + shift, hi + shift)
        dst[lo:hi, lanes] = src[lo:hi, lanes] + src[other, lanes]
        src = dst
        out.append(dst)
    return out


def _mix_fwd(x, g_mix, b_in, b_dw, ln_g, ln_b, w_pool, s_pool, mix_shards, ag_shards, ts):
    S, D = x.shape
    d_in = mix_shards[0].shape[0] * N_DEV
    C = C_CONV
    nt = S // ts
    nrb = ts // CONV_ROWS
    n_ag = len(ag_shards)
    relay_step = nt // 2
    mix_dtypes = [BF16, BF16, F32]

    def body(x_ref, g_ref, bin_ref, bdw_ref, lng_ref, lnb_ref, wp_ref, sp_ref, *rest):
        mx_in, ag_in, rest = rest[:3], rest[3:3 + n_ag], rest[3 + n_ag:]
        a_ref, gate_ref, v_ref, m_ref, y_ref, h1_ref, xn_ref, wdw_ref = rest[:8]
        mx_out, ag_out, rest = rest[8:11], rest[11:11 + n_ag], rest[11 + n_ag:]
        ush, pbuf, pa, pb, win_ref, wout_ref, gdw, load_sems = rest[:8]
        rest = rest[8:]
        (ma_stage, ma_bufs, ma_sems), rest = (rest[:2], rest[2:4], rest[4:7]), rest[7:]
        (mb_stage, mb_bufs, mb_sems), rest = (rest[:1], rest[1:2], rest[2:5]), rest[5:]
        ag_stage, ag_bufs, ag_sems = rest[:n_ag], rest[n_ag:2 * n_ag], rest[2 * n_ag:]
        i = pl.program_id(0)
        ma_start, ma_relay, ma_pass_on, ma_finish = _gather_plan(
            mx_in[0::2], mx_out[0::2], ma_stage, ma_bufs, *ma_sems)
        mb_start, mb_relay, mb_pass_on, mb_finish = _gather_plan(
            mx_in[1:2], mx_out[1:2], mb_stage, mb_bufs, *mb_sems)
        ag_start, ag_relay, ag_pass_on, ag_finish = _gather_plan(ag_in, ag_out, ag_stage, ag_bufs, *ag_sems)

        def load_weight(k, dst_ref):
            rows = mx_out[k].shape[1]
            loads = [pltpu.make_async_copy(mx_out[k].at[j], dst_ref.at[pl.ds(j * rows, rows), :],
                                           load_sems.at[N_DEV * k + j]) for j in range(N_DEV)]
            for cp in loads:
                cp.start()
            for cp in loads:
                cp.wait()

        @pl.when(i == 0)
        def _():
            ma_start()
            mb_start()
            ma_relay()
            mb_relay()
            ag_start()
            ma_pass_on()
            ma_finish()
            load_weight(0, win_ref)
            dw_load = pltpu.make_async_copy(mx_out[2], gdw, load_sems.at[2 * N_DEV])
            dw_load.start()
            dw_load.wait()
            first_half = lax.broadcasted_iota(jnp.int32, gdw.shape[1:], 1) < C // N_DEV
            for p in range(N_DEV // 2):
                wdw_ref[:, LANES * p:LANES * (p + 1)] = jnp.where(
                    first_half, gdw[2 * p], pltpu.roll(gdw[2 * p + 1], C // N_DEV, axis=1))
            ush[0, 0:HALO, :] = jnp.zeros((HALO, C), F32)
            pbuf[0:HALO, :] = jnp.zeros((HALO, C), F32)

        @pl.when(i == relay_step)
        def _():
            ag_relay()

        @pl.when(i == nt - 1)
        def _():
            ag_pass_on()

        x = x_ref[...]
        r1 = lax.rsqrt(_mean_last(x * x) + RMS_EPS)
        xn = (x * r1 * g_ref[...]).astype(BF16)
        xn_ref[...] = xn
        z = _dot_nt(xn, win_ref[...]) + bin_ref[...]
        a = z[:, 0:C]
        gate = z[:, C:2 * C]
        a_ref[...] = a
        gate_ref[...] = gate
        ush[0, HALO:HALO + ts, :] = a * jax.nn.sigmoid(gate)
        pbuf[HALO:HALO + ts, :] = z[:, 2 * C:]

        _shifted_copies(ush, ts + HALO - SUBLANES)

        def conv_block(rb, carry):
            r0 = pl.multiple_of(rb * CONV_ROWS, CONV_ROWS)
            acc = jnp.zeros((CONV_ROWS, C), F32)
            for k in range(CONV_WIDTH):
                acc = acc + wdw_ref[k:k + 1, :] * _tap(ush, HALO - (CONV_WIDTH - 1) + k, r0, CONV_ROWS)
            v_ref[pl.ds(r0, CONV_ROWS), :] = acc + bdw_ref[...]
            return carry

        lax.fori_loop(0, nrb, conv_block, 0)

        v = v_ref[...]
        mu = _mean_last(v)
        xc = v - mu
        rstd = lax.rsqrt(_mean_last(xc * xc) + LN_EPS)
        ln = xc * rstd * lng_ref[...] + lnb_ref[...]
        y_ref[:, 0:C] = (ln * jax.nn.sigmoid(ln)).astype(BF16)

        sums = _window_sums(pbuf, (pa, pb), ts + HALO, backward=True)
        row = lax.broadcasted_iota(jnp.int32, (ts, 1), 0) + i * ts
        for gi, w in enumerate(POOL_WINDOWS):
            lanes = slice(gi * POOL_GROUP, (gi + 1) * POOL_GROUP)
            seg = pbuf[HALO:HALO + ts, lanes]
            ws = sums[gi][HALO:HALO + ts, lanes]
            cnt = jnp.minimum(row + 1, w).astype(F32)
            m = (ws / cnt - seg).astype(BF16)
            m_ref[:, lanes] = m
            ypre = _dot(m, wp_ref[gi])
            y_ref[:, C + gi * POOL_GROUP:C + (gi + 1) * POOL_GROUP] = (ypre * sp_ref[:, lanes]).astype(BF16)

        @pl.when(i == 0)
        def _():
            mb_pass_on()
            mb_finish()
            load_weight(1, wout_ref)

        h1_ref[...] = x + _dot(y_ref[...], wout_ref[...])

        ush[0, 0:HALO, :] = ush[0, ts:ts + HALO, :]
        pbuf[0:HALO, :] = pbuf[ts:ts + HALO, :]

        @pl.when(i == nt - 1)
        def _():
            ag_finish()

    tile = lambda w, dt: (pl.BlockSpec((ts, w), lambda i: (i, 0)), jax.ShapeDtypeStruct((S, w), dt))
    wdw_rows = mix_shards[2].shape[0]
    outs = [tile(C, F32), tile(C, F32), tile(C, F32), tile(C, BF16), tile(D, BF16), tile(D, F32), tile(D, BF16),
            (_full((wdw_rows, C)), jax.ShapeDtypeStruct((wdw_rows, C), F32))]
    return pl.pallas_call(
        body, name="mix_fwd", grid=(nt,),
        in_specs=[pl.BlockSpec((ts, D), lambda i: (i, 0)), _full((1, D)), _full((1, d_in)), _full((1, C)),
                  _full((1, C)), _full((1, C)), _full(w_pool.shape), _full((1, C))] + [_HBM] * (3 + n_ag),
        out_specs=[o[0] for o in outs] + [_HBM] * (3 + n_ag),
        out_shape=[o[1] for o in outs] + _gather_out_shapes(mix_shards, mix_dtypes)
        + _gather_out_shapes(ag_shards, [BF16] * n_ag),
        scratch_shapes=[pltpu.VMEM((SUBLANES, ts + HALO, C), F32)] + [pltpu.VMEM((ts + HALO, C), F32)] * 3
        + [pltpu.VMEM((d_in, D), BF16), pltpu.VMEM((D, D), BF16),
                        pltpu.VMEM((N_DEV,) + mix_shards[2].shape, F32), pltpu.SemaphoreType.DMA((2 * N_DEV + 1,))]
        + _gather_scratch(mix_shards[0::2], mix_dtypes[0::2]) + _gather_scratch(mix_shards[1:2], mix_dtypes[1:2])
        + _gather_scratch(ag_shards, [BF16] * n_ag),
        compiler_params=pltpu.CompilerParams(dimension_semantics=("arbitrary",), vmem_limit_bytes=VMEM_LIMIT),
    )(x, g_mix, b_in, b_dw, ln_g, ln_b, w_pool, s_pool, *mix_shards, *ag_shards)


def _ffn(h1, target, g_ffn, g_final, w_gate, w_up, w_down, ts, f_chunks):
    S, D = h1.shape
    Fd = w_gate.shape[0]
    nt = S // ts
    bounds = []
    lo = 0
    for n in f_chunks:
        bounds.append((lo, lo + n))
        lo += n
    assert lo == Fd

    def body(h1_ref, tgt_ref, gf_ref, gl_ref, wg_ref, wu_ref, wd_ref,
             dh1_ref, hn_ref, act_ref, dgt_ref, dup_ref, dh2_ref, loss_ref, dgl_ref, dgf_ref, gt_s, up_s):
        i = pl.program_id(0)

        @pl.when(i == 0)
        def _():
            loss_ref[...] = jnp.zeros_like(loss_ref)
            dgl_ref[...] = jnp.zeros_like(dgl_ref)
            dgf_ref[...] = jnp.zeros_like(dgf_ref)

        h1 = h1_ref[...]
        r2 = lax.rsqrt(_mean_last(h1 * h1) + RMS_EPS)
        hhat = h1 * r2
        hn = (hhat * gf_ref[...]).astype(BF16)
        hn_ref[...] = hn
        h2 = h1
        for lo, hi in bounds:
            gt = _dot_nt(hn, wg_ref[lo:hi, :])
            up = _dot_nt(hn, wu_ref[lo:hi, :])
            gt_s[:, lo:hi] = gt
            up_s[:, lo:hi] = up
            act = (gt * jax.nn.sigmoid(gt) * up).astype(BF16)
            act_ref[:, lo:hi] = act
            h2 = h2 + _dot(act, wd_ref[lo:hi, :])

        r3 = lax.rsqrt(_mean_last(h2 * h2) + RMS_EPS)
        n3 = h2 * r3
        gl = gl_ref[...]
        diff = n3 * gl - tgt_ref[...]
        loss_ref[...] += jnp.sum(0.5 * jnp.sum(diff * diff, axis=-1, keepdims=True) / D, axis=0, keepdims=True)
        dout = diff / D
        dgl_ref[...] += jnp.sum(dout * n3, axis=0, keepdims=True)
        dn = dout * gl
        dh2 = r3 * (dn - n3 * _mean_last(dn * n3))
        dh2b = dh2.astype(BF16)
        dh2_ref[...] = dh2b

        dhn = jnp.zeros((ts, D), F32)
        for lo, hi in bounds:
            gt = gt_s[:, lo:hi]
            up = up_s[:, lo:hi]
            sg = jax.nn.sigmoid(gt)
            dact = _dot_nt(dh2b, wd_ref[lo:hi, :])
            dgt = (dact * up * (sg * (1.0 + gt * (1.0 - sg)))).astype(BF16)
            dup = (dact * (gt * sg)).astype(BF16)
            dgt_ref[:, lo:hi] = dgt
            dup_ref[:, lo:hi] = dup
            dhn = dhn + _dot(dgt, wg_ref[lo:hi, :]) + _dot(dup, wu_ref[lo:hi, :])

        dgf_ref[...] += jnp.sum(dhn * hhat, axis=0, keepdims=True)
        dnn = dhn * gf_ref[...]
        dh1_ref[...] = dh2 + r2 * (dnn - hhat * _mean_last(dnn * hhat))

    tile = lambda w, dt: (pl.BlockSpec((ts, w), lambda i: (i, 0)), jax.ShapeDtypeStruct((S, w), dt))
    acc = lambda w: (_full((1, w)), jax.ShapeDtypeStruct((1, w), F32))
    outs = [tile(D, F32), tile(D, BF16), tile(Fd, BF16), tile(Fd, BF16), tile(Fd, BF16), tile(D, BF16),
            acc(LANES), acc(D), acc(D)]
    return pl.pallas_call(
        body, name="ffn_fwd_bwd", grid=(nt,),
        in_specs=[pl.BlockSpec((ts, D), lambda i: (i, 0)), pl.BlockSpec((ts, D), lambda i: (i, 0)),
                  _full((1, D)), _full((1, D)), _full1((Fd, D)), _full1((Fd, D)), _full1((Fd, D))],
        out_specs=[o[0] for o in outs], out_shape=[o[1] for o in outs],
        scratch_shapes=[pltpu.VMEM((ts, Fd), F32), pltpu.VMEM((ts, Fd), F32)],
        compiler_params=pltpu.CompilerParams(dimension_semantics=("arbitrary",), vmem_limit_bytes=VMEM_LIMIT),
    )(h1, target, g_ffn, g_final, w_gate, w_up, w_down)


def _mix_bwd(dh1, x, a, gate, v, m, g_mix, w_in, w_dw, ln_g, ln_b, w_pool, s_pool, w_out, rs_parts, ts):
    S, D = x.shape
    n_rs = len(rs_parts)
    d_in = w_in.shape[0]
    C = C_CONV
    nt = S // ts
    nrb = ts // CONV_ROWS
    wrows =((CONV_WIDTH + SUBLANES - 1) // SUBLANES) * SUBLANES

    def body(dh1_ref, x_ref, a_ref, gate_ref, v_ref, m_ref, g_ref, win_ref, wdw_ref, lng_ref,
             lnb_ref, wp_ref, sp_ref, wout_ref, *rest):
        rs_in, rest = rest[:n_rs], rest[n_rs:]
        (dx_ref, dz_ref, dh1b_ref, dgm_ref, dbin_ref, dwdw_ref, dbdw_ref, dlng_ref, dlnb_ref, dwp_ref,
         dsp_ref) = rest[:11]
        rs_out, rest = rest[11:11 + n_rs], rest[11 + n_rs:]
        dvsh, dqbuf, qa, qb, du_s, dm_s = rest[:6]
        rs_bufs, (send_sems, recv_sems, local_sems) = rest[6:6 + n_rs], rest[6 + n_rs:]
        i = pl.program_id(0)
        t = nt - 1 - i
        rs_start, rs_finish = _chip_exchange_plan(rs_in, rs_out, rs_bufs, send_sems, recv_sems, local_sems)

        @pl.when(i == 0)
        def _():
            rs_start()
            dvsh[0, ts:ts + HALO, :] = jnp.zeros((HALO, C), F32)
            dqbuf[ts:ts + HALO, :] = jnp.zeros((HALO, C), F32)
            for r in (dgm_ref, dbin_ref, dwdw_ref, dbdw_ref, dlng_ref, dlnb_ref, dwp_ref, dsp_ref):
                r[...] = jnp.zeros_like(r)

        dh1 = dh1_ref[...]
        dh1b = dh1.astype(BF16)
        dh1b_ref[...] = dh1b
        dy = _dot_nt(dh1b, wout_ref[...])

        v = v_ref[...]
        mu = _mean_last(v)
        xc = v - mu
        rstd = lax.rsqrt(_mean_last(xc * xc) + LN_EPS)
        vhat = xc * rstd
        lng = lng_ref[...]
        ln = vhat * lng + lnb_ref[...]
        sg = jax.nn.sigmoid(ln)
        dln = dy[:, 0:C] * (sg * (1.0 + ln * (1.0 - sg)))
        dlng_ref[...] += jnp.sum(dln * vhat, axis=0, keepdims=True)
        dlnb_ref[...] += jnp.sum(dln, axis=0, keepdims=True)
        dvh = dln * lng
        dv = rstd * (dvh - _mean_last(dvh) - vhat * _mean_last(dvh * vhat))
        dbdw_ref[...] += jnp.sum(dv, axis=0, keepdims=True)
        dvsh[0, 0:ts, :] = dv
        _shifted_copies(dvsh, ts + HALO - SUBLANES)

        def conv_block(rb, carry):
            r0 = pl.multiple_of(rb * CONV_ROWS, CONV_ROWS)
            acc = jnp.zeros((CONV_ROWS, C), F32)
            for k in range(CONV_WIDTH):
                acc = acc + wdw_ref[k:k + 1, :] * _tap(dvsh, CONV_WIDTH - 1 - k, r0, CONV_ROWS)
            du_s[pl.ds(r0, CONV_ROWS), :] = acc
            return carry

        lax.fori_loop(0, nrb, conv_block, 0)

        a = a_ref[...]
        sgate = jax.nn.sigmoid(gate_ref[...])
        u = a * sgate
        for k in range(CONV_WIDTH):
            q, s = divmod(CONV_WIDTH - 1 - k, SUBLANES)
            prod = u * dvsh[s, q * SUBLANES:q * SUBLANES + ts, :]
            dwdw_ref[k:k + 1, :] += jnp.sum(prod, axis=0, keepdims=True)

        du = du_s[...]
        da = du * sgate
        dgate = du * a * sgate * (1.0 - sgate)
        dz_ref[:, 0:C] = da.astype(BF16)
        dz_ref[:, C:2 * C] = dgate.astype(BF16)
        dbin_ref[:, 0:C] += jnp.sum(da, axis=0, keepdims=True)
        dbin_ref[:, C:2 * C] += jnp.sum(dgate, axis=0, keepdims=True)

        row = lax.broadcasted_iota(jnp.int32, (ts, 1), 0) + t * ts
        for gi, w in enumerate(POOL_WINDOWS):
            lanes = slice(gi * POOL_GROUP, (gi + 1) * POOL_GROUP)
            dyp = dy[:, C + gi * POOL_GROUP:C + (gi + 1) * POOL_GROUP]
            mg = m_ref[:, lanes]
            ypre = _dot(mg, wp_ref[gi])
            dsp_ref[:, lanes] += jnp.sum(dyp * ypre, axis=0, keepdims=True)
            dyi = (dyp * sp_ref[:, lanes]).astype(BF16)
            dwp_ref[gi] += _dot_tn(mg, dyi)
            dm = _dot_nt(dyi, wp_ref[gi])
            cnt = jnp.minimum(row + 1, w).astype(F32)
            dqbuf[0:ts, lanes] = dm / cnt
            dm_s[:, lanes] = dm
        sums = _window_sums(dqbuf, (qa, qb), ts + HALO, backward=False)
        for gi in range(len(POOL_WINDOWS)):
            lanes = slice(gi * POOL_GROUP, (gi + 1) * POOL_GROUP)
            dp = sums[gi][0:ts, lanes] - dm_s[:, lanes]
            dz_ref[:, 2 * C + gi * POOL_GROUP:2 * C + (gi + 1) * POOL_GROUP] = dp.astype(BF16)
            dbin_ref[:, 2 * C + gi * POOL_GROUP:2 * C + (gi + 1) * POOL_GROUP] += jnp.sum(dp, axis=0, keepdims=True)

        dxn = _dot(dz_ref[...], win_ref[...])
        x = x_ref[...]
        r1 = lax.rsqrt(_mean_last(x * x) + RMS_EPS)
        xhat = x * r1
        dgm_ref[...] += jnp.sum(dxn * xhat, axis=0, keepdims=True)
        dnn = dxn * g_ref[...]
        dx_ref[...] = dh1 + r1 * (dnn - xhat * _mean_last(dnn * xhat))

        dvsh[0, ts:ts + HALO, :] = dvsh[0, 0:HALO, :]
        dqbuf[ts:ts + HALO, :] = dqbuf[0:HALO, :]

        @pl.when(i == nt - 1)
        def _():
            rs_finish()

    rev = lambda w: pl.BlockSpec((ts, w), lambda i: (nt - 1 - i, 0))
    acc = lambda shape: (_full(shape), jax.ShapeDtypeStruct(shape, F32))
    outs = [(rev(D), jax.ShapeDtypeStruct((S, D), F32)), (rev(d_in), jax.ShapeDtypeStruct((S, d_in), BF16)),
            (rev(D), jax.ShapeDtypeStruct((S, D), BF16)), acc((1, D)), acc((1, d_in)), acc((wrows, C)),
            acc((1, C)), acc((1, C)), acc((1, C)), acc(w_pool.shape), acc((1, C))]
    return pl.pallas_call(
        body, name="mix_bwd", grid=(nt,),
        in_specs=[rev(D), rev(D), rev(C), rev(C), rev(C), rev(C), _full((1, D)), _full((d_in, D)),
                  _full(w_dw.shape), _full((1, C)), _full((1, C)), _full(w_pool.shape), _full((1, C)),
                  _full((D, D))] + [_HBM] * n_rs,
        out_specs=[o[0] for o in outs] + [_HBM] * n_rs,
        out_shape=[o[1] for o in outs] + [jax.ShapeDtypeStruct(p.shape, p.dtype) for p in rs_parts],
        scratch_shapes=[pltpu.VMEM((SUBLANES, ts + HALO, C), F32)] + [pltpu.VMEM((ts + HALO, C), F32)] * 3
        + [pltpu.VMEM((ts, C), F32)] * 2
        + [pltpu.VMEM(p.shape[1:], p.dtype) for p in rs_parts] + _comm_sems(n_rs, 3),
        compiler_params=pltpu.CompilerParams(dimension_semantics=("arbitrary",), vmem_limit_bytes=VMEM_LIMIT),
    )(dh1, x, a, gate, v, m, g_mix, w_in, w_dw, ln_g, ln_b, w_pool, s_pool, w_out, *rs_parts)


def _grad_matmul(a, b, tm, name, reduce_parts=(), reduce_vals=()):
    S, M = a.shape
    N = b.shape[1]
    steps = M // tm
    n, ns = len(reduce_parts), len(reduce_vals)

    def body(a_ref, b_ref, *rest):
        o_ref = rest[n + ns]
        if n + ns:
            start, middle, finish = _reduce_plan(rest[:n], rest[n:n + ns], rest[n + ns + 1:2 * n + ns + 1],
                                                 rest[2 * n + ns + 1:2 * (n + ns) + 1], rest[2 * (n + ns) + 1:])
            i = pl.program_id(0)
            pl.when(i == 0)(start)
            pl.when(i == 1)(middle)
        o_ref[...] = _dot_tn(a_ref[...], b_ref[...]).astype(BF16)
        if n + ns:
            pl.when(i == steps - 1)(finish)

    res = pl.pallas_call(
        body, name=name, grid=(steps,),
        in_specs=[pl.BlockSpec((S, tm), lambda i: (0, i)), _full1((S, N))] + [_VMEM] * (n + ns),
        out_specs=[pl.BlockSpec((tm, N), lambda i: (i, 0))] + [_VMEM] * (n + ns),
        out_shape=[jax.ShapeDtypeStruct((M, N), BF16)] + _reduce_out_shapes(reduce_parts, reduce_vals),
        scratch_shapes=_reduce_scratch(reduce_parts, reduce_vals) if n + ns else [],
        compiler_params=pltpu.CompilerParams(dimension_semantics=("arbitrary",) if n + ns else ("parallel",),
                                             vmem_limit_bytes=VMEM_LIMIT),
    )(a, b, *reduce_parts, *reduce_vals)
    return res if n + ns else res[0]


def _grad_ffn(a_list, b_list, tm):
    n = len(a_list)
    S, M = a_list[0].shape
    N = b_list[0].shape[1]
    nb = M // tm
    steps = n * nb
    R = M // N_DEV
    b_unique = [b for k, b in enumerate(b_list) if all(b is not o for o in b_list[:k])]
    b_index = [[b is o for o in b_unique].index(True) for b in b_list]

    def body(*refs):
        a_refs, b_refs = refs[:n], refs[n:n + len(b_unique)]
        rest = refs[n + len(b_unique):]
        g_refs, sib_refs, (obuf, w_sems, send_sems, recv_sems) = rest[:n], rest[n:2 * n - 1], rest[2 * n - 1:]
        s = pl.program_id(0)
        slot = s % 2
        x, y, c = _position()

        def write_back(dst):
            return pltpu.make_async_copy(obuf.at[slot], dst, w_sems.at[slot])

        def exchange(k):
            return [pltpu.make_async_remote_copy(
                src_ref=g_refs[k].at[pl.ds(pl.multiple_of((2 * q + 1 - c) * R, SUBLANES * 2), R), :],
                dst_ref=sib_refs[k].at[q], send_sem=send_sems.at[k, q], recv_sem=recv_sems.at[k, q],
                device_id=(x, y, 1 - c), device_id_type=MESH) for q in range(N_CHIPS)]

        pl.when(s >= 2)(write_back(g_refs[0].at[pl.ds(0, tm), :]).wait)
        for k in range(n):
            @pl.when(jnp.logical_and(s >= k * nb, s < (k + 1) * nb))
            def _(k=k):
                obuf[slot] = _dot_tn(a_refs[k][...], b_refs[b_index[k]][...]).astype(BF16)
                row = pl.multiple_of((s - k * nb) * tm, tm)
                write_back(g_refs[k].at[pl.ds(row, tm), :]).start()
            if k < n - 1:
                @pl.when(s == (k + 1) * nb + 2)
                def _(k=k):
                    for cp in exchange(k):
                        cp.start()

        @pl.when(s == steps - 1)
        def _():
            write_back(g_refs[0].at[pl.ds(0, tm), :]).wait()
            pltpu.make_async_copy(obuf.at[1 - slot], g_refs[0].at[pl.ds(0, tm), :], w_sems.at[1 - slot]).wait()
            for k in range(n - 1):
                for cp in exchange(k):
                    cp.wait()

    clamp = lambda k: (lambda s: (0, jnp.clip(s - k * nb, 0, nb - 1)))
    sems = pltpu.SemaphoreType.DMA((n - 1, N_CHIPS))
    res = pl.pallas_call(
        body, name="grad_w_ffn", grid=(steps,),
        in_specs=[pl.BlockSpec((S, tm), clamp(k)) for k in range(n)] + [_full1((S, N))] * len(b_unique),
        out_specs=[_HBM] * (2 * n - 1),
        out_shape=[jax.ShapeDtypeStruct((M, N), BF16)] * n
        + [jax.ShapeDtypeStruct((N_CHIPS, R, N), BF16)] * (n - 1),
        scratch_shapes=[pltpu.VMEM((2, tm, N), BF16), pltpu.SemaphoreType.DMA((2,)), sems, sems],
        compiler_params=pltpu.CompilerParams(dimension_semantics=("arbitrary",), vmem_limit_bytes=VMEM_LIMIT),
    )(*a_list, *b_unique)
    return res[:n], res[n:]


def _position():
    return lax.axis_index("x"), lax.axis_index("y"), lax.axis_index("c")


def _slot(px, py, pc):
    return 4 * px + 2 * py + pc


_HBM = pl.BlockSpec(memory_space=pl.ANY)


def _comm_sems(n, copies):
    return [pltpu.SemaphoreType.DMA((n, copies)), pltpu.SemaphoreType.DMA((n, copies)),
            pltpu.SemaphoreType.DMA((n, 2))]


def _local_copy(srcs, dsts, bufs, local_sems):
    n = len(srcs)
    loads = [pltpu.make_async_copy(srcs[k], bufs[k], local_sems.at[k, 0]) for k in range(n)]
    for cp in loads:
        cp.start()
    for cp in loads:
        cp.wait()
    stores = _local_stores(dsts, bufs, local_sems)
    for cp in stores:
        cp.start()
    return stores


def _local_stores(dsts, bufs, local_sems):
    return [pltpu.make_async_copy(bufs[k], dsts[k], local_sems.at[k, 1]) for k in range(len(dsts))]


def _gather_out_shapes(shards, dtypes):
    return [jax.ShapeDtypeStruct((N_DEV,) + s.shape, dt) for s, dt in zip(shards, dtypes)]


def _gather_scratch(shards, dtypes):
    return ([pltpu.VMEM(s.shape, s.dtype) for s in shards] + [pltpu.VMEM(s.shape, dt) for s, dt in zip(shards, dtypes)]
            + _comm_sems(len(shards), 7))


def _gather_plan(ins, outs, stage, bufs, send_sems, recv_sems, local_sems):
    n = len(ins)
    x, y, c = _position()
    me, sibling = (x, y, c), (x, y, 1 - c)
    na, nb, dg = (x ^ (1 - c), y ^ c), (x ^ c, y ^ (1 - c)), (1 - x, 1 - y)
    own = [outs[k].at[_slot(*me)] for k in range(n)]

    def copy(k, sem, block, to, src=None):
        dst = outs[k].at[_slot(*block)]
        return pltpu.make_async_remote_copy(
            src_ref=dst if src is None else src, dst_ref=dst, send_sem=send_sems.at[k, sem],
            recv_sem=recv_sems.at[k, sem], device_id=to, device_id_type=MESH)

    def first():
        cps = []
        for k in range(n):
            cps += [copy(k, 0, me, sibling, src=bufs[k]), copy(k, 1, me, (*na, c), src=bufs[k]),
                    copy(k, 2, me, (*nb, c), src=bufs[k])]
        return cps

    def onward():
        return [copy(k, 3, (*na, c), (*nb, c)) for k in range(n)]

    def to_sibling(j, chip):
        return [copy(k, 4 + j, (*chip, c), sibling) for k in range(n)]

    def start():
        loads = [pltpu.make_async_copy(ins[k], stage[k], local_sems.at[k, 0]) for k in range(n)]
        for cp in loads:
            cp.start()
        for cp in loads:
            cp.wait()
        for k in range(n):
            bufs[k][...] = stage[k][...].astype(bufs[k].dtype)
        for cp in first() + _local_stores(own, bufs, local_sems):
            cp.start()

    def relay():
        for k in range(n):
            copy(k, 1, (*na, c), me).wait_recv()
        for cp in onward() + to_sibling(0, na):
            cp.start()
        for k in range(n):
            copy(k, 2, (*nb, c), me).wait_recv()
        for cp in to_sibling(1, nb):
            cp.start()

    def pass_on():
        for k in range(n):
            copy(k, 3, (*dg, c), me).wait_recv()
        for cp in to_sibling(2, dg):
            cp.start()

    def finish():
        for k in range(n):
            copy(k, 0, sibling, me).wait_recv()
            for j, chip in enumerate((nb, na, dg)):
                copy(k, 4 + j, (*chip, 1 - c), me).wait_recv()
        for cp in first() + onward() + to_sibling(0, na) + to_sibling(1, nb) + to_sibling(2, dg):
            cp.wait_send()
        for cp in _local_stores(own, bufs, local_sems):
            cp.wait()

    return start, relay, pass_on, finish


N_CHIPS = 4
_CHIP_FLIPS = [(1, 0), (0, 1), (1, 1)]
F32_TRAVEL_LIMIT = 4096


def _chip_exchange_plan(ins, outs, bufs, send_sems, recv_sems, local_sems):
    n = len(ins)
    x, y, c = _position()
    my_q = 2 * x + y
    peers = [(x ^ fx, y ^ fy) for fx, fy in _CHIP_FLIPS]
    own = [outs[k].at[my_q] for k in range(n)]

    def sends():
        return [pltpu.make_async_remote_copy(
            src_ref=ins[k].at[2 * px + py], dst_ref=outs[k].at[my_q], send_sem=send_sems.at[k, f],
            recv_sem=recv_sems.at[k, f], device_id=(px, py, c), device_id_type=MESH)
            for f, (px, py) in enumerate(peers) for k in range(n)]

    def start():
        for cp in sends():
            cp.start()
        _local_copy([ins[k].at[my_q] for k in range(n)], own, bufs, local_sems)

    def finish():
        for f, (px, py) in enumerate(peers):
            for k in range(n):
                pltpu.make_async_remote_copy(
                    src_ref=ins[k].at[my_q], dst_ref=outs[k].at[2 * px + py], send_sem=send_sems.at[k, f],
                    recv_sem=recv_sems.at[k, f], device_id=(px, py, c), device_id_type=MESH).wait_recv()
        for cp in sends():
            cp.wait_send()
        for cp in _local_stores(own, bufs, local_sems):
            cp.wait()

    return start, finish


def _pair_exchange(parts):
    n = len(parts)

    def body(*refs):
        start, finish = _pair_exchange_plan(refs[:n], refs[n:2 * n], *refs[2 * n:])
        start()
        finish()

    sems = pltpu.SemaphoreType.DMA((n, N_CHIPS))
    return pl.pallas_call(
        body, name="pair_exchange", out_shape=_pair_out_shapes(parts),
        in_specs=[_HBM] * n, out_specs=[_HBM] * n, scratch_shapes=[sems, sems],
    )(*parts)


def _pair_out_shapes(parts):
    return [jax.ShapeDtypeStruct((N_CHIPS,) + p.shape[1:], p.dtype) for p in parts]


def _pair_exchange_plan(ins, sib, send_sems, recv_sems):
    x, y, c = _position()

    def copies():
        return [pltpu.make_async_remote_copy(
            src_ref=ins[k].at[2 * q + 1 - c], dst_ref=sib[k].at[q], send_sem=send_sems.at[k, q],
            recv_sem=recv_sems.at[k, q], device_id=(x, y, 1 - c), device_id_type=MESH)
            for k in range(len(ins)) for q in range(N_CHIPS)]

    def start():
        for cp in copies():
            cp.start()

    def finish():
        for cp in copies():
            cp.wait()

    return start, finish


def _pair_sum(parts, sib):
    n = len(parts)

    def body(*refs):
        c = lax.axis_index("c")
        for k in range(n):
            refs[2 * n + k][0] = (refs[k][0, c].astype(F32) + refs[n + k][0].astype(F32)).astype(BF16)

    pair = [pl.BlockSpec((1, 2) + p.shape[1:], lambda q: (q, 0, 0, 0)) for p in parts]
    one = [pl.BlockSpec((1,) + p.shape[1:], lambda q: (q, 0, 0)) for p in parts]
    return pl.pallas_call(
        body, name="pair_sum", grid=(N_CHIPS,), in_specs=pair + one, out_specs=one,
        out_shape=[jax.ShapeDtypeStruct(s.shape, BF16) for s in sib],
        compiler_params=pltpu.CompilerParams(dimension_semantics=("parallel",)),
    )(*[p.reshape((N_CHIPS, 2) + p.shape[1:]) for p in parts], *sib)


def _tail_reduce(parts, vals):
    n, ns = len(parts), len(vals)

    def body(*refs):
        start, middle, finish = _reduce_plan(refs[:n], refs[n:n + ns], refs[n + ns:2 * n + ns],
                                             refs[2 * n + ns:2 * (n + ns)], refs[2 * (n + ns):])
        start()
        middle()
        finish()

    return pl.pallas_call(
        body, name="tail_reduce", out_shape=_reduce_out_shapes(parts, vals),
        in_specs=[_VMEM] * (n + ns), out_specs=[_VMEM] * (n + ns), scratch_shapes=_reduce_scratch(parts, vals),
        compiler_params=pltpu.CompilerParams(vmem_limit_bytes=VMEM_LIMIT),
    )(*parts, *vals)


_VMEM = pl.BlockSpec(memory_space=pltpu.VMEM)


def _reduce_out_shapes(parts, vals):
    return ([jax.ShapeDtypeStruct((N_CHIPS,) + p.shape[1:], p.dtype) for p in parts]
            + [jax.ShapeDtypeStruct(v.shape, v.dtype) for v in vals])


def _reduce_scratch(parts, vals):
    n, ns = len(parts), len(vals)
    quarter = [pltpu.VMEM((N_CHIPS,) + p.shape[1:], p.dtype) for p in parts]
    dma = pltpu.SemaphoreType.DMA
    travel = [BF16 if v.size > F32_TRAVEL_LIMIT else v.dtype for v in vals]
    return (quarter * 3 + [pltpu.VMEM(v.shape, v.dtype) for v in vals]
            + [pltpu.VMEM(v.shape, t) for v, t in zip(vals, travel)]
            + [pltpu.VMEM((N_CHIPS,) + v.shape, t) for v, t in zip(vals, travel)]
            + [dma((max(n, 1), N_CHIPS)), dma((max(n, 1), N_CHIPS)), dma((max(ns, 1),)), dma((max(ns, 1),)),
               dma((max(n, 1), 3)), dma((max(n, 1), 3)), dma((max(ns, 1), 3)), dma((max(ns, 1), 3))])


def _reduce_plan(p_in, v_in, p_out, v_out, scratch):
    n, ns = len(p_in), len(v_in)
    p_sib, p_sum, p_all = scratch[:n], scratch[n:2 * n], scratch[2 * n:3 * n]
    v_sib, v_sum, v_all = (scratch[3 * n + j * ns:3 * n + (j + 1) * ns] for j in range(3))
    p1_send, p1_recv, v1_send, v1_recv, p3_send, p3_recv, v3_send, v3_recv = scratch[3 * n + 3 * ns:]
    x, y, c = _position()
    my_q = 2 * x + y
    peers = [(x ^ fx, y ^ fy) for fx, fy in _CHIP_FLIPS]

    def to_sibling(src, dst, send, recv):
        return pltpu.make_async_remote_copy(src_ref=src, dst_ref=dst, send_sem=send, recv_sem=recv,
                                            device_id=(x, y, 1 - c), device_id_type=MESH)

    def level1():
        cps = [to_sibling(p_in[k].at[2 * q + 1 - c], p_sib[k].at[q], p1_send.at[k, q], p1_recv.at[k, q])
               for k in range(n) for q in range(N_CHIPS)]
        return cps + [to_sibling(v_in[k], v_sib[k], v1_send.at[k], v1_recv.at[k]) for k in range(ns)]

    def to_chip(f, src, dst, send, recv):
        px, py = peers[f]
        return pltpu.make_async_remote_copy(src_ref=src, dst_ref=dst, send_sem=send, recv_sem=recv,
                                            device_id=(px, py, c), device_id_type=MESH)

    def level2(sending):
        cps = []
        for f, (px, py) in enumerate(peers):
            their_q = 2 * px + py
            for k in range(n):
                src, dst = (p_sum[k].at[their_q], p_all[k].at[my_q]) if sending else (
                    p_sum[k].at[my_q], p_all[k].at[their_q])
                cps.append(to_chip(f, src, dst, p3_send.at[k, f], p3_recv.at[k, f]))
            for k in range(ns):
                dst = v_all[k].at[my_q] if sending else v_all[k].at[their_q]
                cps.append(to_chip(f, v_sum[k], dst, v3_send.at[k, f], v3_recv.at[k, f]))
        return cps

    def start():
        for cp in level1():
            cp.start()

    def middle():
        for cp in level1():
            cp.wait_recv()
        for k in range(n):
            for q in range(N_CHIPS):
                p_sum[k][q] = (p_in[k][2 * q + c].astype(F32) + p_sib[k][q].astype(F32)).astype(p_sum[k].dtype)
        for k in range(ns):
            v_sum[k][...] = (v_in[k][...] + v_sib[k][...]).astype(v_sum[k].dtype)
        for cp in level2(True):
            cp.start()
        for k in range(n):
            p_all[k][my_q] = p_sum[k][my_q]
        for k in range(ns):
            v_all[k][my_q] = v_sum[k][...]

    def finish():
        for cp in level2(False):
            cp.wait_recv()
        for k in range(n):
            p_out[k][...] = p_all[k][...]
        for k in range(ns):
            total = v_all[k][0].astype(F32)
            for q in range(1, N_CHIPS):
                total = total + v_all[k][q].astype(F32)
            v_out[k][...] = total
        for cp in level1() + level2(True):
            cp.wait_send()

    return start, middle, finish


def _adamw_math(w, g, m, v):
    m = ADAM_B1 * m + (1.0 - ADAM_B1) * g
    v = ADAM_B2 * v + (1.0 - ADAM_B2) * (g * g)
    m_hat = m / (1.0 - ADAM_B1 ** ADAM_STEP)
    v_hat = v / (1.0 - ADAM_B2 ** ADAM_STEP)
    delta = -ADAM_LR * (m_hat / (jnp.sqrt(v_hat) + ADAM_EPS) + ADAM_WD * w)
    return delta, m, v


def _adamw_shards(parts, ws, ms, vs, steps, name):
    n = len(ws)

    def body(*refs):
        ins, outs = refs[:4 * n], refs[4 * n:]
        for k in range(n):
            p_ref, w_ref, m_ref, v_ref = ins[4 * k:4 * k + 4]
            g = p_ref[0].astype(F32)
            for j in range(1, N_CHIPS):
                g = g + p_ref[j].astype(F32)
            d, mn, vn = _adamw_math(w_ref[...], g, m_ref[...], v_ref[...])
            for o, val in zip(outs[4 * k:4 * k + 4], (g, d, mn, vn)):
                o[...] = val

    in_specs, out_specs, out_shape, operands = [], [], [], []
    for p, w, m, v in zip(parts, ws, ms, vs):
        R, Cc = w.shape
        blk = pl.BlockSpec((R // steps, Cc), lambda i: (i, 0))
        in_specs += [pl.BlockSpec((N_CHIPS, R // steps, Cc), lambda i: (0, i, 0)), blk, blk, blk]
        out_specs += [blk] * 4
        out_shape += [jax.ShapeDtypeStruct((R, Cc), F32)] * 4
        operands += [p, w, m, v]
    res = pl.pallas_call(
        body, name=name, grid=(steps,), in_specs=in_specs, out_specs=out_specs, out_shape=out_shape,
        compiler_params=pltpu.CompilerParams(dimension_semantics=("parallel",), vmem_limit_bytes=VMEM_LIMIT),
    )(*operands)
    return [res[4 * k:4 * k + 4] for k in range(n)]


def _adamw_small(grads, ws, ms, vs):
    n = len(grads)
    vm = pl.BlockSpec(memory_space=pltpu.VMEM)

    def body(*refs):
        g_in, w_in, m_in, v_in = (refs[k * n:(k + 1) * n] for k in range(4))
        g_out, d_out, m_out, v_out = (refs[(4 + k) * n:(5 + k) * n] for k in range(4))
        for k in range(n):
            g = g_in[k][...]
            d, mn, vn = _adamw_math(w_in[k][...], g, m_in[k][...], v_in[k][...])
            g_out[k][...] = g
            d_out[k][...] = d
            m_out[k][...] = mn
            v_out[k][...] = vn

    shapes = [jax.ShapeDtypeStruct(g.shape, F32) for g in grads]
    return pl.pallas_call(
        body, name="adamw_small", out_shape=shapes * 4, in_specs=[vm] * (4 * n), out_specs=[vm] * (4 * n),
    )(*grads, *ws, *ms, *vs)


def _shard_cols(full):
    R, Ct = full.shape
    return jnp.transpose(full.reshape(R, N_DEV, Ct // N_DEV), (1, 0, 2))


_COL_SHARDED = ("w_in", "w_gate", "w_up")
_BIG = ("w_in", "w_out", "w_gate", "w_up", "w_down")


def _rows(nm, p):
    return p[0].T if nm in _COL_SHARDED else p[0]


def _step(args, ts_mix_fwd, ts_ffn, ts_mix_bwd, tm_grad):
    (x, g_mix, w_in, b_in, w_dw, b_dw, ln_g, ln_b, w_pool, s_pool, w_out, g_ffn, w_gate, w_up, w_down, g_final,
     loss_target) = args[:17]
    names = ["g_mix", "w_in", "b_in", "w_dw", "b_dw", "ln_g", "ln_b", "w_pool", "s_pool", "w_out", "g_ffn",
             "w_gate", "w_up", "w_down", "g_final"]
    weights = dict(zip(names, args[1:16]))
    moms = dict(zip(names, args[17:32]))
    vars_ = dict(zip(names, args[32:47]))

    S, D = x.shape[1], x.shape[2]
    x2 = x.reshape(S, D)
    tgt2 = loss_target.reshape(S, D)

    shard = lambda nm: _rows(nm, weights[nm])
    w_pool_b = w_pool[0].astype(BF16)
    dw_rows = -(-CONV_WIDTH // SUBLANES) * SUBLANES
    dw_shard = jnp.pad(w_dw[0], ((0, dw_rows - CONV_WIDTH), (0, LANES - w_dw.shape[2])))

    a, gate, v, m, y, h1, xn, w_dw_f, g_in, g_out, _, g_gate, g_up, g_down = _mix_fwd(
        x2, g_mix, b_in, b_dw, ln_g, ln_b, w_pool_b, s_pool, [shard("w_in"), shard("w_out"), dw_shard],
        [shard("w_gate"), shard("w_up"), shard("w_down")], ts_mix_fwd)
    wt_in, w_out_f = g_in.reshape(-1, D), g_out.reshape(-1, D)
    wt_gate, wt_up, w_down_f = g_gate.reshape(-1, D), g_up.reshape(-1, D), g_down.reshape(-1, D)
    Fd = wt_gate.shape[0]
    f_chunks = [1024] * (Fd // 1024) + ([Fd % 1024] if Fd % 1024 else [])
    dh1, hn, act, dgt, dup, dh2, loss_p, dg_final, dg_ffn = _ffn(
        h1, tgt2, g_ffn, g_final.reshape(1, D), wt_gate, wt_up, w_down_f, ts_ffn, f_chunks)

    by_shard = lambda g: g.reshape(N_DEV, -1, D)
    ffn_grads, (s_gate, s_up) = _grad_ffn([dgt, dup, act], [hn, hn, dh2], tm_grad)
    p_gate, p_up, p_down = (by_shard(g) for g in ffn_grads)
    s_down, = _pair_exchange([p_down])
    pair_sums = _pair_sum([p_gate, p_up, p_down], [s_gate, s_up, s_down])
    (dx, dz, dh1b, dg_mix, db_in, dw_dw, db_dw, dln_g, dln_b, dw_pool, ds_pool, r_gate, r_up, r_down) = _mix_bwd(
        dh1, x2, a, gate, v, m, g_mix, wt_in, w_dw_f, ln_g, ln_b, w_pool_b, s_pool, w_out_f, pair_sums, ts_mix_bwd)

    small_names = ["g_mix", "b_in", "b_dw", "ln_g", "ln_b", "w_pool", "s_pool", "g_ffn", "g_final"]
    small_shape = lambda p: p.reshape(-1, p.shape[-1])
    partial = [dg_mix, db_in, db_dw, dln_g, dln_b, dw_pool.reshape(-1, POOL_GROUP), ds_pool, dg_ffn, dg_final, loss_p]
    dw_out, *summed = _grad_matmul(y, dh1b, tm_grad, "grad_w_out", reduce_vals=partial)
    dwt_in, r_out, r_dw = _grad_matmul(
        dz, xn, tm_grad, "grad_w_in",
        reduce_parts=[dw_out.reshape(N_DEV, -1, D), _shard_cols(dw_dw[0:CONV_WIDTH])])
    r_in, = _tail_reduce([dwt_in.reshape(N_DEV, -1, D)], [])

    big = {}
    recv = dict(zip(_BIG, [r_in, r_out, r_gate, r_up, r_down]))
    for call, group in (("adamw_ffn", ("w_gate", "w_up", "w_down")), ("adamw_mix", ("w_in", "w_out"))):
        results = _adamw_shards([recv[nm] for nm in group], [_rows(nm, weights[nm]) for nm in group],
                                [_rows(nm, moms[nm]) for nm in group], [_rows(nm, vars_[nm]) for nm in group], 2, call)
        for nm, res in zip(group, results):
            big[nm] = [o.T if nm in _COL_SHARDED else o for o in res]
    big["w_dw"], = _adamw_shards([r_dw], [w_dw[0]], [moms["w_dw"][0]], [vars_["w_dw"][0]], 1, "adamw_w_dw")

    sm = _adamw_small(summed[:-1], [small_shape(weights[nm]) for nm in small_names],
                      [small_shape(moms[nm]) for nm in small_names], [small_shape(vars_[nm]) for nm in small_names])
    n_small = len(small_names)

    def result(kind, nm):
        if nm in big:
            return big[nm][kind].reshape(weights[nm].shape)
        return sm[kind * n_small + small_names.index(nm)].reshape(weights[nm].shape)

    loss = summed[-1][0, 0]
    out = [loss, dx.reshape(x.shape)]
    for kind in range(4):
        out += [result(kind, nm) for nm in names]
    return tuple(out)


def kernel(x, g_mix, w_in, b_in, w_dw, b_dw, ln_g, ln_b, w_pool, s_pool, w_out, g_ffn, w_gate, w_up, w_down, g_final, loss_target, m_g_mix, m_w_in, m_b_in, m_w_dw, m_b_dw, m_ln_g, m_ln_b, m_w_pool, m_s_pool, m_w_out, m_g_ffn, m_w_gate, m_w_up, m_w_down, m_g_final, v_g_mix, v_w_in, v_b_in, v_w_dw, v_b_dw, v_ln_g, v_ln_b, v_w_pool, v_s_pool, v_w_out, v_g_ffn, v_w_gate, v_w_up, v_w_down, v_g_final):
    args = (x, g_mix, w_in, b_in, w_dw, b_dw, ln_g, ln_b, w_pool, s_pool, w_out, g_ffn, w_gate, w_up, w_down, g_final, loss_target, m_g_mix, m_w_in, m_b_in, m_w_dw, m_b_dw, m_ln_g, m_ln_b, m_w_pool, m_s_pool, m_w_out, m_g_ffn, m_w_gate, m_w_up, m_w_down, m_g_final, v_g_mix, v_w_in, v_b_in, v_w_dw, v_b_dw, v_ln_g, v_ln_b, v_w_pool, v_s_pool, v_w_out, v_g_ffn, v_w_gate, v_w_up, v_w_down, v_g_final)
    return _step(args, ts_mix_fwd=512, ts_ffn=256, ts_mix_bwd=512, tm_grad=256)
```

```python
import jax
import jax.numpy as jnp
from jax import lax
from jax.experimental import pallas as pl
from jax.experimental.pallas import tpu as pltpu

F32 = jnp.float32
BF16 = jnp.bfloat16
MESH = pl.DeviceIdType.MESH
N_DEV = 8

C_CONV = 512
CONV_WIDTH = 31
POOL_WINDOWS = (2, 4, 8, 16)
POOL_GROUP = 128
RMS_EPS = 1e-6
LN_EPS = 1e-5

ADAM_LR = 0.001
ADAM_B1 = 0.9
ADAM_B2 = 0.999
ADAM_EPS = 1e-08
ADAM_WD = 0.01
ADAM_STEP = 10

HALO = 32
SUBLANES = 8
LANES = 128
CONV_ROWS = 64
VMEM_LIMIT = 56 * 1024 * 1024


def _dot(a, b):
    return jnp.dot(a, b, preferred_element_type=F32)


def _dot_nt(a, b):
    return lax.dot_general(a, b, (((1,), (1,)), ((), ())), preferred_element_type=F32)


def _dot_tn(a, b):
    return lax.dot_general(a, b, (((0,), (0,)), ((), ())), preferred_element_type=F32)


def _mean_last(v):
    return jnp.mean(v, axis=-1, keepdims=True)


def _full(shape):
    nd = len(shape)
    return pl.BlockSpec(shape, lambda *_: (0,) * nd)


def _full1(shape):
    nd = len(shape)
    return pl.BlockSpec(shape, lambda *_: (0,) * nd, pipeline_mode=pl.Buffered(1))


def _shifted_copies(sh_ref, rows):
    for s in range(1, SUBLANES):
        sh_ref[s, 0:rows, :] = sh_ref[0, s:s + rows, :]


def _tap(sh_ref, off, r0, rows):
    q, s = divmod(off, SUBLANES)
    return sh_ref[s, pl.ds(r0 + q * SUBLANES, rows), :]


def _window_sums(src, bufs, rows, backward):
    assert all(w == 2 << g for g, w in enumerate(POOL_WINDOWS))
    n = len(POOL_WINDOWS)
    out = []
    for level in range(n):
        dst, shift = bufs[level % 2], 1 << level
        lanes = slice(level * POOL_GROUP, n * POOL_GROUP)
        lo, hi = (SUBLANES * (level + 1), rows) if backward else (0, rows - SUBLANES * (level + 1))
        other = slice(lo - shift, hi - shift) if backward else slice(lo + shift, hi + shift)
        dst[lo:hi, lanes] = src[lo:hi, lanes] + src[other, lanes]
        src = dst
        out.append(dst)
    return out


def _mix_fwd(x, g_mix, b_in, b_dw, ln_g, ln_b, w_pool, s_pool, mix_shards, ag_shards, ts):
    S, D = x.shape
    d_in = mix_shards[0].shape[0] * N_DEV
    C = C_CONV
    nt = S // ts
    nrb = ts // CONV_ROWS
    n_ag = len(ag_shards)
    relay_step = nt // 2
    mix_dtypes = [BF16, BF16, F32]

    def body(x_ref, g_ref, bin_ref, bdw_ref, lng_ref, lnb_ref, wp_ref, sp_ref, *rest):
        mx_in, ag_in, rest = rest[:3], rest[3:3 + n_ag], rest[3 + n_ag:]
        a_ref, gate_ref, v_ref, m_ref, y_ref, h1_ref, xn_ref, wdw_ref = rest[:8]
        mx_out, ag_out, rest = rest[8:11], rest[11:11 + n_ag], rest[11 + n_ag:]
        ush, pbuf, pa, pb, win_ref, wout_ref, gdw, load_sems = rest[:8]
        rest = rest[8:]
        (ma_stage, ma_bufs, ma_sems), rest = (rest[:2], rest[2:4], rest[4:7]), rest[7:]
        (mb_stage, mb_bufs, mb_sems), rest = (rest[:1], rest[1:2], rest[2:5]), rest[5:]
        ag_stage, ag_bufs, ag_sems = rest[:n_ag], rest[n_ag:2 * n_ag], rest[2 * n_ag:]
        i = pl.program_id(0)
        ma_start, ma_relay, ma_pass_on, ma_finish = _gather_plan(
            mx_in[0::2], mx_out[0::2], ma_stage, ma_bufs, *ma_sems)
        mb_start, mb_relay, mb_pass_on, mb_finish = _gather_plan(
            mx_in[1:2], mx_out[1:2], mb_stage, mb_bufs, *mb_sems)
        ag_start, ag_relay, ag_pass_on, ag_finish = _gather_plan(ag_in, ag_out, ag_stage, ag_bufs, *ag_sems)

        def load_weight(k, dst_ref):
            rows = mx_out[k].shape[1]
            loads = [pltpu.make_async_copy(mx_out[k].at[j], dst_ref.at[pl.ds(j * rows, rows), :],
                                           load_sems.at[N_DEV * k + j]) for j in range(N_DEV)]
            for cp in loads:
                cp.start()
            for cp in loads:
                cp.wait()

        @pl.when(i == 0)
        def _():
            ma_start()
            mb_start()
            ma_relay()
            mb_relay()
            ag_start()
            ma_pass_on()
            ma_finish()
            load_weight(0, win_ref)
            dw_load = pltpu.make_async_copy(mx_out[2], gdw, load_sems.at[2 * N_DEV])
            dw_load.start()
            dw_load.wait()
            first_half = lax.broadcasted_iota(jnp.int32, gdw.shape[1:], 1) < C // N_DEV
            for p in range(N_DEV // 2):
                wdw_ref[:, LANES * p:LANES * (p + 1)] = jnp.where(
                    first_half, gdw[2 * p], pltpu.roll(gdw[2 * p + 1], C // N_DEV, axis=1))
            ush[0, 0:HALO, :] = jnp.zeros((HALO, C), F32)
            pbuf[0:HALO, :] = jnp.zeros((HALO, C), F32)

        @pl.when(i == relay_step)
        def _():
            ag_relay()

        @pl.when(i == nt - 1)
        def _():
            ag_pass_on()

        x = x_ref[...]
        r1 = lax.rsqrt(_mean_last(x * x) + RMS_EPS)
        xn = (x * r1 * g_ref[...]).astype(BF16)
        xn_ref[...] = xn
        z = _dot_nt(xn, win_ref[...]) + bin_ref[...]
        a = z[:, 0:C]
        gate = z[:, C:2 * C]
        a_ref[...] = a
        gate_ref[...] = gate
        ush[0, HALO:HALO + ts, :] = a * jax.nn.sigmoid(gate)
        pbuf[HALO:HALO + ts, :] = z[:, 2 * C:]

        _shifted_copies(ush, ts + HALO - SUBLANES)

        def conv_block(rb, carry):
            r0 = pl.multiple_of(rb * CONV_ROWS, CONV_ROWS)
            acc = jnp.zeros((CONV_ROWS, C), F32)
            for k in range(CONV_WIDTH):
                acc = acc + wdw_ref[k:k + 1, :] * _tap(ush, HALO - (CONV_WIDTH - 1) + k, r0, CONV_ROWS)
            v_ref[pl.ds(r0, CONV_ROWS), :] = acc + bdw_ref[...]
            return carry

        lax.fori_loop(0, nrb, conv_block, 0)

        v = v_ref[...]
        mu = _mean_last(v)
        xc = v - mu
        rstd = lax.rsqrt(_mean_last(xc * xc) + LN_EPS)
        ln = xc * rstd * lng_ref[...] + lnb_ref[...]
        y_ref[:, 0:C] = (ln * jax.nn.sigmoid(ln)).astype(BF16)

        sums = _window_sums(pbuf, (pa, pb), ts + HALO, backward=True)
        row = lax.broadcasted_iota(jnp.int32, (ts, 1), 0) + i * ts
        for gi, w in enumerate(POOL_WINDOWS):
            lanes = slice(gi * POOL_GROUP, (gi + 1) * POOL_GROUP)
            seg = pbuf[HALO:HALO + ts, lanes]
            ws = sums[gi][HALO:HALO + ts, lanes]
            cnt = jnp.minimum(row + 1, w).astype(F32)
            m = (ws / cnt - seg).astype(BF16)
            m_ref[:, lanes] = m
            ypre = _dot(m, wp_ref[gi])
            y_ref[:, C + gi * POOL_GROUP:C + (gi + 1) * POOL_GROUP] = (ypre * sp_ref[:, lanes]).astype(BF16)

        @pl.when(i == 0)
        def _():
            mb_pass_on()
            mb_finish()
            load_weight(1, wout_ref)

        h1_ref[...] = x + _dot(y_ref[...], wout_ref[...])

        ush[0, 0:HALO, :] = ush[0, ts:ts + HALO, :]
        pbuf[0:HALO, :] = pbuf[ts:ts + HALO, :]

        @pl.when(i == nt - 1)
        def _():
            ag_finish()

    tile = lambda w, dt: (pl.BlockSpec((ts, w), lambda i: (i, 0)), jax.ShapeDtypeStruct((S, w), dt))
    wdw_rows = mix_shards[2].shape[0]
    outs = [tile(C, F32), tile(C, F32), tile(C, F32), tile(C, BF16), tile(D, BF16), tile(D, F32), tile(D, BF16),
            (_full((wdw_rows, C)), jax.ShapeDtypeStruct((wdw_rows, C), F32))]
    return pl.pallas_call(
        body, name="mix_fwd", grid=(nt,),
        in_specs=[pl.BlockSpec((ts, D), lambda i: (i, 0)), _full((1, D)), _full((1, d_in)), _full((1, C)),
                  _full((1, C)), _full((1, C)), _full(w_pool.shape), _full((1, C))] + [_HBM] * (3 + n_ag),
        out_specs=[o[0] for o in outs] + [_HBM] * (3 + n_ag),
        out_shape=[o[1] for o in outs] + _gather_out_shapes(mix_shards, mix_dtypes)
        + _gather_out_shapes(ag_shards, [BF16] * n_ag),
        scratch_shapes=[pltpu.VMEM((SUBLANES, ts + HALO, C), F32)] + [pltpu.VMEM((ts + HALO, C), F32)] * 3
        + [pltpu.VMEM((d_in, D), BF16), pltpu.VMEM((D, D), BF16),
                        pltpu.VMEM((N_DEV,) + mix_shards[2].shape, F32), pltpu.SemaphoreType.DMA((2 * N_DEV + 1,))]
        + _gather_scratch(mix_shards[0::2], mix_dtypes[0::2]) + _gather_scratch(mix_shards[1:2], mix_dtypes[1:2])
        + _gather_scratch(ag_shards, [BF16] * n_ag),
        compiler_params=pltpu.CompilerParams(dimension_semantics=("arbitrary",), vmem_limit_bytes=VMEM_LIMIT),
    )(x, g_mix, b_in, b_dw, ln_g, ln_b, w_pool, s_pool, *mix_shards, *ag_shards)


def _ffn(h1, target, g_ffn, g_final, w_gate, w_up, w_down, ts, f_chunks):
    S, D = h1.shape
    Fd = w_gate.shape[0]
    nt = S // ts
    bounds = []
    lo = 0
    for n in f_chunks:
        bounds.append((lo, lo + n))
        lo += n
    assert lo == Fd

    def body(h1_ref, tgt_ref, gf_ref, gl_ref, wg_ref, wu_ref, wd_ref,
             dh1_ref, hn_ref, act_ref, dgt_ref, dup_ref, dh2_ref, loss_ref, dgl_ref, dgf_ref, gt_s, up_s):
        i = pl.program_id(0)

        @pl.when(i == 0)
        def _():
            loss_ref[...] = jnp.zeros_like(loss_ref)
            dgl_ref[...] = jnp.zeros_like(dgl_ref)
            dgf_ref[...] = jnp.zeros_like(dgf_ref)

        h1 = h1_ref[...]
        r2 = lax.rsqrt(_mean_last(h1 * h1) + RMS_EPS)
        hhat = h1 * r2
        hn = (hhat * gf_ref[...]).astype(BF16)
        hn_ref[...] = hn
        h2 = h1
        for lo, hi in bounds:
            gt = _dot_nt(hn, wg_ref[lo:hi, :])
            up = _dot_nt(hn, wu_ref[lo:hi, :])
            gt_s[:, lo:hi] = gt
            up_s[:, lo:hi] = up
            act = (gt * jax.nn.sigmoid(gt) * up).astype(BF16)
            act_ref[:, lo:hi] = act
            h2 = h2 + _dot(act, wd_ref[lo:hi, :])

        r3 = lax.rsqrt(_mean_last(h2 * h2) + RMS_EPS)
        n3 = h2 * r3
        gl = gl_ref[...]
        diff = n3 * gl - tgt_ref[...]
        loss_ref[...] += jnp.sum(0.5 * jnp.sum(diff * diff, axis=-1, keepdims=True) / D, axis=0, keepdims=True)
        dout = diff / D
        dgl_ref[...] += jnp.sum(dout * n3, axis=0, keepdims=True)
        dn = dout * gl
        dh2 = r3 * (dn - n3 * _mean_last(dn * n3))
        dh2b = dh2.astype(BF16)
        dh2_ref[...] = dh2b

        dhn = jnp.zeros((ts, D), F32)
        for lo, hi in bounds:
            gt = gt_s[:, lo:hi]
            up = up_s[:, lo:hi]
            sg = jax.nn.sigmoid(gt)
            dact = _dot_nt(dh2b, wd_ref[lo:hi, :])
            dgt = (dact * up * (sg * (1.0 + gt * (1.0 - sg)))).astype(BF16)
            dup = (dact * (gt * sg)).astype(BF16)
            dgt_ref[:, lo:hi] = dgt
            dup_ref[:, lo:hi] = dup
            dhn = dhn + _dot(dgt, wg_ref[lo:hi, :]) + _dot(dup, wu_ref[lo:hi, :])

        dgf_ref[...] += jnp.sum(dhn * hhat, axis=0, keepdims=True)
        dnn = dhn * gf_ref[...]
        dh1_ref[...] = dh2 + r2 * (dnn - hhat * _mean_last(dnn * hhat))

    tile = lambda w, dt: (pl.BlockSpec((ts, w), lambda i: (i, 0)), jax.ShapeDtypeStruct((S, w), dt))
    acc = lambda w: (_full((1, w)), jax.ShapeDtypeStruct((1, w), F32))
    outs = [tile(D, F32), tile(D, BF16), tile(Fd, BF16), tile(Fd, BF16), tile(Fd, BF16), tile(D, BF16),
            acc(LANES), acc(D), acc(D)]
    return pl.pallas_call(
        body, name="ffn_fwd_bwd", grid=(nt,),
        in_specs=[pl.BlockSpec((ts, D), lambda i: (i, 0)), pl.BlockSpec((ts, D), lambda i: (i, 0)),
                  _full((1, D)), _full((1, D)), _full1((Fd, D)), _full1((Fd, D)), _full1((Fd, D))],
        out_specs=[o[0] for o in outs], out_shape=[o[1] for o in outs],
        scratch_shapes=[pltpu.VMEM((ts, Fd), F32), pltpu.VMEM((ts, Fd), F32)],
        compiler_params=pltpu.CompilerParams(dimension_semantics=("arbitrary",), vmem_limit_bytes=VMEM_LIMIT),
    )(h1, target, g_ffn, g_final, w_gate, w_up, w_down)


def _mix_bwd(dh1, x, a, gate, v, m, g_mix, w_in, w_dw, ln_g, ln_b, w_pool, s_pool, w_out, rs_parts, ts):
    S, D = x.shape
    n_rs = len(rs_parts)
    d_in = w_in.shape[0]
    C = C_CONV
    nt = S // ts
    nrb = ts // CONV_ROWS
    wrows =((CONV_WIDTH + SUBLANES - 1) // SUBLANES) * SUBLANES

    def body(dh1_ref, x_ref, a_ref, gate_ref, v_ref, m_ref, g_ref, win_ref, wdw_ref, lng_ref,
             lnb_ref, wp_ref, sp_ref, wout_ref, *rest):
        rs_in, rest = rest[:n_rs], rest[n_rs:]
        (dx_ref, dz_ref, dh1b_ref, dgm_ref, dbin_ref, dwdw_ref, dbdw_ref, dlng_ref, dlnb_ref, dwp_ref,
         dsp_ref) = rest[:11]
        rs_out, rest = rest[11:11 + n_rs], rest[11 + n_rs:]
        dvsh, dqbuf, qa, qb, du_s, dm_s = rest[:6]
        rs_bufs, (send_sems, recv_sems, local_sems) = rest[6:6 + n_rs], rest[6 + n_rs:]
        i = pl.program_id(0)
        t = nt - 1 - i
        rs_start, rs_finish = _chip_exchange_plan(rs_in, rs_out, rs_bufs, send_sems, recv_sems, local_sems)

        @pl.when(i == 0)
        def _():
            rs_start()
            dvsh[0, ts:ts + HALO, :] = jnp.zeros((HALO, C), F32)
            dqbuf[ts:ts + HALO, :] = jnp.zeros((HALO, C), F32)
            for r in (dgm_ref, dbin_ref, dwdw_ref, dbdw_ref, dlng_ref, dlnb_ref, dwp_ref, dsp_ref):
                r[...] = jnp.zeros_like(r)

        dh1 = dh1_ref[...]
        dh1b = dh1.astype(BF16)
        dh1b_ref[...] = dh1b
        dy = _dot_nt(dh1b, wout_ref[...])

        v = v_ref[...]
        mu = _mean_last(v)
        xc = v - mu
        rstd = lax.rsqrt(_mean_last(xc * xc) + LN_EPS)
        vhat = xc * rstd
        lng = lng_ref[...]
        ln = vhat * lng + lnb_ref[...]
        sg = jax.nn.sigmoid(ln)
        dln = dy[:, 0:C] * (sg * (1.0 + ln * (1.0 - sg)))
        dlng_ref[...] += jnp.sum(dln * vhat, axis=0, keepdims=True)
        dlnb_ref[...] += jnp.sum(dln, axis=0, keepdims=True)
        dvh = dln * lng
        dv = rstd * (dvh - _mean_last(dvh) - vhat * _mean_last(dvh * vhat))
        dbdw_ref[...] += jnp.sum(dv, axis=0, keepdims=True)
        dvsh[0, 0:ts, :] = dv
        _shifted_copies(dvsh, ts + HALO - SUBLANES)

        def conv_block(rb, carry):
            r0 = pl.multiple_of(rb * CONV_ROWS, CONV_ROWS)
            acc = jnp.zeros((CONV_ROWS, C), F32)
            for k in range(CONV_WIDTH):
                acc = acc + wdw_ref[k:k + 1, :] * _tap(dvsh, CONV_WIDTH - 1 - k, r0, CONV_ROWS)
            du_s[pl.ds(r0, CONV_ROWS), :] = acc
            return carry

        lax.fori_loop(0, nrb, conv_block, 0)

        a = a_ref[...]
        sgate = jax.nn.sigmoid(gate_ref[...])
        u = a * sgate
        for k in range(CONV_WIDTH):
            q, s = divmod(CONV_WIDTH - 1 - k, SUBLANES)
            prod = u * dvsh[s, q * SUBLANES:q * SUBLANES + ts, :]
            dwdw_ref[k:k + 1, :] += jnp.sum(prod, axis=0, keepdims=True)

        du = du_s[...]
        da = du * sgate
        dgate = du * a * sgate * (1.0 - sgate)
        dz_ref[:, 0:C] = da.astype(BF16)
        dz_ref[:, C:2 * C] = dgate.astype(BF16)
        dbin_ref[:, 0:C] += jnp.sum(da, axis=0, keepdims=True)
        dbin_ref[:, C:2 * C] += jnp.sum(dgate, axis=0, keepdims=True)

        row = lax.broadcasted_iota(jnp.int32, (ts, 1), 0) + t * ts
        for gi, w in enumerate(POOL_WINDOWS):
            lanes = slice(gi * POOL_GROUP, (gi + 1) * POOL_GROUP)
            dyp = dy[:, C + gi * POOL_GROUP:C + (gi + 1) * POOL_GROUP]
            mg = m_ref[:, lanes]
            ypre = _dot(mg, wp_ref[gi])
            dsp_ref[:, lanes] += jnp.sum(dyp * ypre, axis=0, keepdims=True)
            dyi = (dyp * sp_ref[:, lanes]).astype(BF16)
            dwp_ref[gi] += _dot_tn(mg, dyi)
            dm = _dot_nt(dyi, wp_ref[gi])
            cnt = jnp.minimum(row + 1, w).astype(F32)
            dqbuf[0:ts, lanes] = dm / cnt
            dm_s[:, lanes] = dm
        sums = _window_sums(dqbuf, (qa, qb), ts + HALO, backward=False)
        for gi in range(len(POOL_WINDOWS)):
            lanes = slice(gi * POOL_GROUP, (gi + 1) * POOL_GROUP)
            dp = sums[gi][0:ts, lanes] - dm_s[:, lanes]
            dz_ref[:, 2 * C + gi * POOL_GROUP:2 * C + (gi + 1) * POOL_GROUP] = dp.astype(BF16)
            dbin_ref[:, 2 * C + gi * POOL_GROUP:2 * C + (gi + 1) * POOL_GROUP] += jnp.sum(dp, axis=0, keepdims=True)

        dxn = _dot(dz_ref[...], win_ref[...])
        x = x_ref[...]
        r1 = lax.rsqrt(_mean_last(x * x) + RMS_EPS)
        xhat = x * r1
        dgm_ref[...] += jnp.sum(dxn * xhat, axis=0, keepdims=True)
        dnn = dxn * g_ref[...]
        dx_ref[...] = dh1 + r1 * (dnn - xhat * _mean_last(dnn * xhat))

        dvsh[0, ts:ts + HALO, :] = dvsh[0, 0:HALO, :]
        dqbuf[ts:ts + HALO, :] = dqbuf[0:HALO, :]

        @pl.when(i == nt - 1)
        def _():
            rs_finish()

    rev = lambda w: pl.BlockSpec((ts, w), lambda i: (nt - 1 - i, 0))
    acc = lambda shape: (_full(shape), jax.ShapeDtypeStruct(shape, F32))
    outs = [(rev(D), jax.ShapeDtypeStruct((S, D), F32)), (rev(d_in), jax.ShapeDtypeStruct((S, d_in), BF16)),
            (rev(D), jax.ShapeDtypeStruct((S, D), BF16)), acc((1, D)), acc((1, d_in)), acc((wrows, C)),
            acc((1, C)), acc((1, C)), acc((1, C)), acc(w_pool.shape), acc((1, C))]
    return pl.pallas_call(
        body, name="mix_bwd", grid=(nt,),
        in_specs=[rev(D), rev(D), rev(C), rev(C), rev(C), rev(C), _full((1, D)), _full((d_in, D)),
                  _full(w_dw.shape), _full((1, C)), _full((1, C)), _full(w_pool.shape), _full((1, C)),
                  _full((D, D))] + [_HBM] * n_rs,
        out_specs=[o[0] for o in outs] + [_HBM] * n_rs,
        out_shape=[o[1] for o in outs] + [jax.ShapeDtypeStruct(p.shape, p.dtype) for p in rs_parts],
        scratch_shapes=[pltpu.VMEM((SUBLANES, ts + HALO, C), F32)] + [pltpu.VMEM((ts + HALO, C), F32)] * 3
        + [pltpu.VMEM((ts, C), F32)] * 2
        + [pltpu.VMEM(p.shape[1:], p.dtype) for p in rs_parts] + _comm_sems(n_rs, 3),
        compiler_params=pltpu.CompilerParams(dimension_semantics=("arbitrary",), vmem_limit_bytes=VMEM_LIMIT),
    )(dh1, x, a, gate, v, m, g_mix, w_in, w_dw, ln_g, ln_b, w_pool, s_pool, w_out, *rs_parts)


def _grad_mix(y, dh1b, dz, xn, dw_dw, vals):
    S, D = y.shape
    d_in = dz.shape[1]
    r_out, r_in = D // N_DEV, d_in // N_DEV
    n_out, n_in = N_DEV // 2, N_DEV // 2
    steps = n_out + n_in
    ns = len(vals)
    dw_out = jax.ShapeDtypeStruct((N_DEV, r_out, D), BF16)
    s_scratch, o_scratch = _reduce_scratch([], vals), _reduce_scratch([dw_out, dw_dw], [])

    def body(y_ref, dh_ref, dz_ref, xn_ref, dwdw_ref, *rest):
        v_in, dwi_ref, p_out, v_out = rest[:ns], rest[ns], rest[ns + 1:ns + 3], rest[ns + 3:2 * ns + 3]
        rest = rest[2 * ns + 3:]
        dwo_v, s_refs, o_refs = rest[0], rest[1:1 + len(s_scratch)], rest[1 + len(s_scratch):]
        s_start, s_middle, s_finish = _reduce_plan((), v_in, (), v_out, s_refs)
        o_start, o_middle, o_finish = _reduce_plan((dwo_v, dwdw_ref), (), p_out, (), o_refs)
        s = pl.program_id(0)
        pl.when(s == 0)(s_start)
        pl.when(s == 1)(s_middle)

        @pl.when(s < n_out)
        def _():
            res = _dot_tn(y_ref[...], dh_ref[...]).astype(BF16)
            dwo_v[2 * s] = res[0:r_out]
            dwo_v[2 * s + 1] = res[r_out:]

        pl.when(s == n_out - 1)(s_finish)
        pl.when(s == n_out)(o_start)
        pl.when(s == n_out + 1)(o_middle)

        @pl.when(s >= n_out)
        def _():
            dwi_ref[...] = _dot_tn(dz_ref[...], xn_ref[...]).astype(BF16)

        pl.when(s == steps - 1)(o_finish)

    late = lambda s: jnp.clip(s - n_out, 0, n_in - 1)
    res = pl.pallas_call(
        body, name="grad_w_mix", grid=(steps,),
        in_specs=[pl.BlockSpec((S, 2 * r_out), lambda s: (0, jnp.minimum(s, n_out - 1))), _full1((S, D)),
                  pl.BlockSpec((S, 2 * r_in), lambda s: (0, late(s))), _full1((S, D))] + [_VMEM] * (1 + ns),
        out_specs=[pl.BlockSpec((2 * r_in, D), lambda s: (late(s), 0))] + [_VMEM] * (2 + ns),
        out_shape=[jax.ShapeDtypeStruct((d_in, D), BF16)] + _reduce_out_shapes([dw_out, dw_dw], [])
        + _reduce_out_shapes([], vals),
        scratch_shapes=[pltpu.VMEM(dw_out.shape, BF16)] + s_scratch + o_scratch,
        compiler_params=pltpu.CompilerParams(dimension_semantics=("arbitrary",), vmem_limit_bytes=VMEM_LIMIT),
    )(y, dh1b, dz, xn, dw_dw, *vals)
    return res[0], res[1], res[2], res[3:]


def _grad_ffn(a_list, b_list, tm):
    n = len(a_list)
    S, M = a_list[0].shape
    N = b_list[0].shape[1]
    nb = M // tm
    steps = n * nb
    R = M // N_DEV
    b_unique = [b for k, b in enumerate(b_list) if all(b is not o for o in b_list[:k])]
    b_index = [[b is o for o in b_unique].index(True) for b in b_list]

    def body(*refs):
        a_refs, b_refs = refs[:n], refs[n:n + len(b_unique)]
        rest = refs[n + len(b_unique):]
        g_refs, sib_refs, (obuf, w_sems, send_sems, recv_sems) = rest[:n], rest[n:2 * n - 1], rest[2 * n - 1:]
        s = pl.program_id(0)
        slot = s % 2
        x, y, c = _position()

        def write_back(dst):
            return pltpu.make_async_copy(obuf.at[slot], dst, w_sems.at[slot])

        def exchange(k):
            return [pltpu.make_async_remote_copy(
                src_ref=g_refs[k].at[pl.ds(pl.multiple_of((2 * q + 1 - c) * R, SUBLANES * 2), R), :],
                dst_ref=sib_refs[k].at[q], send_sem=send_sems.at[k, q], recv_sem=recv_sems.at[k, q],
                device_id=(x, y, 1 - c), device_id_type=MESH) for q in range(N_CHIPS)]

        pl.when(s >= 2)(write_back(g_refs[0].at[pl.ds(0, tm), :]).wait)
        for k in range(n):
            @pl.when(jnp.logical_and(s >= k * nb, s < (k + 1) * nb))
            def _(k=k):
                obuf[slot] = _dot_tn(a_refs[k][...], b_refs[b_index[k]][...]).astype(BF16)
                row = pl.multiple_of((s - k * nb) * tm, tm)
                write_back(g_refs[k].at[pl.ds(row, tm), :]).start()
            if k < n - 1:
                @pl.when(s == (k + 1) * nb + 2)
                def _(k=k):
                    for cp in exchange(k):
                        cp.start()

        @pl.when(s == steps - 1)
        def _():
            write_back(g_refs[0].at[pl.ds(0, tm), :]).wait()
            pltpu.make_async_copy(obuf.at[1 - slot], g_refs[0].at[pl.ds(0, tm), :], w_sems.at[1 - slot]).wait()
            for k in range(n - 1):
                for cp in exchange(k):
                    cp.wait()

    clamp = lambda k: (lambda s: (0, jnp.clip(s - k * nb, 0, nb - 1)))
    sems = pltpu.SemaphoreType.DMA((n - 1, N_CHIPS))
    res = pl.pallas_call(
        body, name="grad_w_ffn", grid=(steps,),
        in_specs=[pl.BlockSpec((S, tm), clamp(k)) for k in range(n)] + [_full1((S, N))] * len(b_unique),
        out_specs=[_HBM] * (2 * n - 1),
        out_shape=[jax.ShapeDtypeStruct((M, N), BF16)] * n
        + [jax.ShapeDtypeStruct((N_CHIPS, R, N), BF16)] * (n - 1),
        scratch_shapes=[pltpu.VMEM((2, tm, N), BF16), pltpu.SemaphoreType.DMA((2,)), sems, sems],
        compiler_params=pltpu.CompilerParams(dimension_semantics=("arbitrary",), vmem_limit_bytes=VMEM_LIMIT),
    )(*a_list, *b_unique)
    return res[:n], res[n:]


def _position():
    return lax.axis_index("x"), lax.axis_index("y"), lax.axis_index("c")


def _slot(px, py, pc):
    return 4 * px + 2 * py + pc


_HBM = pl.BlockSpec(memory_space=pl.ANY)


def _comm_sems(n, copies):
    return [pltpu.SemaphoreType.DMA((n, copies)), pltpu.SemaphoreType.DMA((n, copies)),
            pltpu.SemaphoreType.DMA((n, 2))]


def _local_copy(srcs, dsts, bufs, local_sems):
    n = len(srcs)
    loads = [pltpu.make_async_copy(srcs[k], bufs[k], local_sems.at[k, 0]) for k in range(n)]
    for cp in loads:
        cp.start()
    for cp in loads:
        cp.wait()
    stores = _local_stores(dsts, bufs, local_sems)
    for cp in stores:
        cp.start()
    return stores


def _local_stores(dsts, bufs, local_sems):
    return [pltpu.make_async_copy(bufs[k], dsts[k], local_sems.at[k, 1]) for k in range(len(dsts))]


def _gather_out_shapes(shards, dtypes):
    return [jax.ShapeDtypeStruct((N_DEV,) + s.shape, dt) for s, dt in zip(shards, dtypes)]


def _gather_scratch(shards, dtypes):
    return ([pltpu.VMEM(s.shape, s.dtype) for s in shards] + [pltpu.VMEM(s.shape, dt) for s, dt in zip(shards, dtypes)]
            + _comm_sems(len(shards), 7))


def _gather_plan(ins, outs, stage, bufs, send_sems, recv_sems, local_sems):
    n = len(ins)
    x, y, c = _position()
    me, sibling = (x, y, c), (x, y, 1 - c)
    na, nb, dg = (x ^ (1 - c), y ^ c), (x ^ c, y ^ (1 - c)), (1 - x, 1 - y)
    own = [outs[k].at[_slot(*me)] for k in range(n)]

    def copy(k, sem, block, to, src=None):
        dst = outs[k].at[_slot(*block)]
        return pltpu.make_async_remote_copy(
            src_ref=dst if src is None else src, dst_ref=dst, send_sem=send_sems.at[k, sem],
            recv_sem=recv_sems.at[k, sem], device_id=to, device_id_type=MESH)

    def first():
        cps = []
        for k in range(n):
            cps += [copy(k, 0, me, sibling, src=bufs[k]), copy(k, 1, me, (*na, c), src=bufs[k]),
                    copy(k, 2, me, (*nb, c), src=bufs[k])]
        return cps

    def onward():
        return [copy(k, 3, (*na, c), (*nb, c)) for k in range(n)]

    def to_sibling(j, chip):
        return [copy(k, 4 + j, (*chip, c), sibling) for k in range(n)]

    def start():
        loads = [pltpu.make_async_copy(ins[k], stage[k], local_sems.at[k, 0]) for k in range(n)]
        for cp in loads:
            cp.start()
        for cp in loads:
            cp.wait()
        for k in range(n):
            bufs[k][...] = stage[k][...].astype(bufs[k].dtype)
        for cp in first() + _local_stores(own, bufs, local_sems):
            cp.start()

    def relay():
        for k in range(n):
            copy(k, 1, (*na, c), me).wait_recv()
        for cp in onward() + to_sibling(0, na):
            cp.start()
        for k in range(n):
            copy(k, 2, (*nb, c), me).wait_recv()
        for cp in to_sibling(1, nb):
            cp.start()

    def pass_on():
        for k in range(n):
            copy(k, 3, (*dg, c), me).wait_recv()
        for cp in to_sibling(2, dg):
            cp.start()

    def finish():
        for k in range(n):
            copy(k, 0, sibling, me).wait_recv()
            for j, chip in enumerate((nb, na, dg)):
                copy(k, 4 + j, (*chip, 1 - c), me).wait_recv()
        for cp in first() + onward() + to_sibling(0, na) + to_sibling(1, nb) + to_sibling(2, dg):
            cp.wait_send()
        for cp in _local_stores(own, bufs, local_sems):
            cp.wait()

    return start, relay, pass_on, finish


N_CHIPS = 4
_CHIP_FLIPS = [(1, 0), (0, 1), (1, 1)]
F32_TRAVEL_LIMIT = 4096


def _chip_exchange_plan(ins, outs, bufs, send_sems, recv_sems, local_sems):
    n = len(ins)
    x, y, c = _position()
    my_q = 2 * x + y
    peers = [(x ^ fx, y ^ fy) for fx, fy in _CHIP_FLIPS]
    own = [outs[k].at[my_q] for k in range(n)]

    def sends():
        return [pltpu.make_async_remote_copy(
            src_ref=ins[k].at[2 * px + py], dst_ref=outs[k].at[my_q], send_sem=send_sems.at[k, f],
            recv_sem=recv_sems.at[k, f], device_id=(px, py, c), device_id_type=MESH)
            for f, (px, py) in enumerate(peers) for k in range(n)]

    def start():
        for cp in sends():
            cp.start()
        _local_copy([ins[k].at[my_q] for k in range(n)], own, bufs, local_sems)

    def finish():
        for f, (px, py) in enumerate(peers):
            for k in range(n):
                pltpu.make_async_remote_copy(
                    src_ref=ins[k].at[my_q], dst_ref=outs[k].at[2 * px + py], send_sem=send_sems.at[k, f],
                    recv_sem=recv_sems.at[k, f], device_id=(px, py, c), device_id_type=MESH).wait_recv()
        for cp in sends():
            cp.wait_send()
        for cp in _local_stores(own, bufs, local_sems):
            cp.wait()

    return start, finish


def _pair_exchange(parts):
    n = len(parts)

    def body(*refs):
        start, finish = _pair_exchange_plan(refs[:n], refs[n:2 * n], *refs[2 * n:])
        start()
        finish()

    sems = pltpu.SemaphoreType.DMA((n, N_CHIPS))
    return pl.pallas_call(
        body, name="pair_exchange", out_shape=_pair_out_shapes(parts),
        in_specs=[_HBM] * n, out_specs=[_HBM] * n, scratch_shapes=[sems, sems],
    )(*parts)


def _pair_out_shapes(parts):
    return [jax.ShapeDtypeStruct((N_CHIPS,) + p.shape[1:], p.dtype) for p in parts]


def _pair_exchange_plan(ins, sib, send_sems, recv_sems):
    x, y, c = _position()

    def copies():
        return [pltpu.make_async_remote_copy(
            src_ref=ins[k].at[2 * q + 1 - c], dst_ref=sib[k].at[q], send_sem=send_sems.at[k, q],
            recv_sem=recv_sems.at[k, q], device_id=(x, y, 1 - c), device_id_type=MESH)
            for k in range(len(ins)) for q in range(N_CHIPS)]

    def start():
        for cp in copies():
            cp.start()

    def finish():
        for cp in copies():
            cp.wait()

    return start, finish


def _pair_sum(parts, sib):
    n = len(parts)

    def body(*refs):
        c = lax.axis_index("c")
        for k in range(n):
            refs[2 * n + k][0] = (refs[k][0, c].astype(F32) + refs[n + k][0].astype(F32)).astype(BF16)

    pair = [pl.BlockSpec((1, 2) + p.shape[1:], lambda q: (q, 0, 0, 0)) for p in parts]
    one = [pl.BlockSpec((1,) + p.shape[1:], lambda q: (q, 0, 0)) for p in parts]
    return pl.pallas_call(
        body, name="pair_sum", grid=(N_CHIPS,), in_specs=pair + one, out_specs=one,
        out_shape=[jax.ShapeDtypeStruct(s.shape, BF16) for s in sib],
        compiler_params=pltpu.CompilerParams(dimension_semantics=("parallel",)),
    )(*[p.reshape((N_CHIPS, 2) + p.shape[1:]) for p in parts], *sib)


def _tail_reduce(parts, vals):
    n, ns = len(parts), len(vals)

    def body(*refs):
        start, middle, finish = _reduce_plan(refs[:n], refs[n:n + ns], refs[n + ns:2 * n + ns],
                                             refs[2 * n + ns:2 * (n + ns)], refs[2 * (n + ns):])
        start()
        middle()
        finish()

    return pl.pallas_call(
        body, name="tail_reduce", out_shape=_reduce_out_shapes(parts, vals),
        in_specs=[_VMEM] * (n + ns), out_specs=[_VMEM] * (n + ns), scratch_shapes=_reduce_scratch(parts, vals),
        compiler_params=pltpu.CompilerParams(vmem_limit_bytes=VMEM_LIMIT),
    )(*parts, *vals)


_VMEM = pl.BlockSpec(memory_space=pltpu.VMEM)


def _reduce_out_shapes(parts, vals):
    return ([jax.ShapeDtypeStruct((N_CHIPS,) + p.shape[1:], p.dtype) for p in parts]
            + [jax.ShapeDtypeStruct(v.shape, v.dtype) for v in vals])


def _reduce_scratch(parts, vals):
    n, ns = len(parts), len(vals)
    quarter = [pltpu.VMEM((N_CHIPS,) + p.shape[1:], p.dtype) for p in parts]
    dma = pltpu.SemaphoreType.DMA
    travel = [BF16 if v.size > F32_TRAVEL_LIMIT else v.dtype for v in vals]
    return (quarter * 3 + [pltpu.VMEM(v.shape, v.dtype) for v in vals]
            + [pltpu.VMEM(v.shape, t) for v, t in zip(vals, travel)]
            + [pltpu.VMEM((N_CHIPS,) + v.shape, t) for v, t in zip(vals, travel)]
            + [dma((max(n, 1), N_CHIPS)), dma((max(n, 1), N_CHIPS)), dma((max(ns, 1),)), dma((max(ns, 1),)),
               dma((max(n, 1), 3)), dma((max(n, 1), 3)), dma((max(ns, 1), 3)), dma((max(ns, 1), 3))])


def _reduce_plan(p_in, v_in, p_out, v_out, scratch):
    n, ns = len(p_in), len(v_in)
    p_sib, p_sum, p_all = scratch[:n], scratch[n:2 * n], scratch[2 * n:3 * n]
    v_sib, v_sum, v_all = (scratch[3 * n + j * ns:3 * n + (j + 1) * ns] for j in range(3))
    p1_send, p1_recv, v1_send, v1_recv, p3_send, p3_recv, v3_send, v3_recv = scratch[3 * n + 3 * ns:]
    x, y, c = _position()
    my_q = 2 * x + y
    peers = [(x ^ fx, y ^ fy) for fx, fy in _CHIP_FLIPS]

    def to_sibling(src, dst, send, recv):
        return pltpu.make_async_remote_copy(src_ref=src, dst_ref=dst, send_sem=send, recv_sem=recv,
                                            device_id=(x, y, 1 - c), device_id_type=MESH)

    def level1():
        cps = [to_sibling(p_in[k].at[2 * q + 1 - c], p_sib[k].at[q], p1_send.at[k, q], p1_recv.at[k, q])
               for k in range(n) for q in range(N_CHIPS)]
        return cps + [to_sibling(v_in[k], v_sib[k], v1_send.at[k], v1_recv.at[k]) for k in range(ns)]

    def to_chip(f, src, dst, send, recv):
        px, py = peers[f]
        return pltpu.make_async_remote_copy(src_ref=src, dst_ref=dst, send_sem=send, recv_sem=recv,
                                            device_id=(px, py, c), device_id_type=MESH)

    def level2(sending):
        cps = []
        for f, (px, py) in enumerate(peers):
            their_q = 2 * px + py
            for k in range(n):
                src, dst = (p_sum[k].at[their_q], p_all[k].at[my_q]) if sending else (
                    p_sum[k].at[my_q], p_all[k].at[their_q])
                cps.append(to_chip(f, src, dst, p3_send.at[k, f], p3_recv.at[k, f]))
            for k in range(ns):
                dst = v_all[k].at[my_q] if sending else v_all[k].at[their_q]
                cps.append(to_chip(f, v_sum[k], dst, v3_send.at[k, f], v3_recv.at[k, f]))
        return cps

    def start():
        for cp in level1():
            cp.start()

    def middle():
        for cp in level1():
            cp.wait_recv()
        for k in range(n):
            for q in range(N_CHIPS):
                p_sum[k][q] = (p_in[k][2 * q + c].astype(F32) + p_sib[k][q].astype(F32)).astype(p_sum[k].dtype)
        for k in range(ns):
            v_sum[k][...] = (v_in[k][...] + v_sib[k][...]).astype(v_sum[k].dtype)
        for cp in level2(True):
            cp.start()
        for k in range(n):
            p_all[k][my_q] = p_sum[k][my_q]
        for k in range(ns):
            v_all[k][my_q] = v_sum[k][...]

    def finish():
        for cp in level2(False):
            cp.wait_recv()
        for k in range(n):
            p_out[k][...] = p_all[k][...]
        for k in range(ns):
            total = v_all[k][0].astype(F32)
            for q in range(1, N_CHIPS):
                total = total + v_all[k][q].astype(F32)
            v_out[k][...] = total
        for cp in level1() + level2(True):
            cp.wait_send()

    return start, middle, finish


def _adamw_math(w, g, m, v):
    m = ADAM_B1 * m + (1.0 - ADAM_B1) * g
    v = ADAM_B2 * v + (1.0 - ADAM_B2) * (g * g)
    m_hat = m / (1.0 - ADAM_B1 ** ADAM_STEP)
    v_hat = v / (1.0 - ADAM_B2 ** ADAM_STEP)
    delta = -ADAM_LR * (m_hat / (jnp.sqrt(v_hat) + ADAM_EPS) + ADAM_WD * w)
    return delta, m, v


def _adamw_shards(parts, ws, ms, vs, steps, name):
    n = len(ws)

    def body(*refs):
        ins, outs = refs[:4 * n], refs[4 * n:]
        for k in range(n):
            p_ref, w_ref, m_ref, v_ref = ins[4 * k:4 * k + 4]
            g = p_ref[0].astype(F32)
            for j in range(1, N_CHIPS):
                g = g + p_ref[j].astype(F32)
            d, mn, vn = _adamw_math(w_ref[...], g, m_ref[...], v_ref[...])
            for o, val in zip(outs[4 * k:4 * k + 4], (g, d, mn, vn)):
                o[...] = val

    in_specs, out_specs, out_shape, operands = [], [], [], []
    for p, w, m, v in zip(parts, ws, ms, vs):
        R, Cc = w.shape
        blk = pl.BlockSpec((R // steps, Cc), lambda i: (i, 0))
        in_specs += [pl.BlockSpec((N_CHIPS, R // steps, Cc), lambda i: (0, i, 0)), blk, blk, blk]
        out_specs += [blk] * 4
        out_shape += [jax.ShapeDtypeStruct((R, Cc), F32)] * 4
        operands += [p, w, m, v]
    res = pl.pallas_call(
        body, name=name, grid=(steps,), in_specs=in_specs, out_specs=out_specs, out_shape=out_shape,
        compiler_params=pltpu.CompilerParams(dimension_semantics=("parallel",), vmem_limit_bytes=VMEM_LIMIT),
    )(*operands)
    return [res[4 * k:4 * k + 4] for k in range(n)]


def _adamw_small(grads, ws, ms, vs):
    n = len(grads)
    vm = pl.BlockSpec(memory_space=pltpu.VMEM)

    def body(*refs):
        g_in, w_in, m_in, v_in = (refs[k * n:(k + 1) * n] for k in range(4))
        g_out, d_out, m_out, v_out = (refs[(4 + k) * n:(5 + k) * n] for k in range(4))
        for k in range(n):
            g = g_in[k][...]
            d, mn, vn = _adamw_math(w_in[k][...], g, m_in[k][...], v_in[k][...])
            g_out[k][...] = g
            d_out[k][...] = d
            m_out[k][...] = mn
            v_out[k][...] = vn

    shapes = [jax.ShapeDtypeStruct(g.shape, F32) for g in grads]
    return pl.pallas_call(
        body, name="adamw_small", out_shape=shapes * 4, in_specs=[vm] * (4 * n), out_specs=[vm] * (4 * n),
    )(*grads, *ws, *ms, *vs)


def _shard_cols(full):
    R, Ct = full.shape
    return jnp.transpose(full.reshape(R, N_DEV, Ct // N_DEV), (1, 0, 2))


_COL_SHARDED = ("w_in", "w_gate", "w_up")
_BIG = ("w_in", "w_out", "w_gate", "w_up", "w_down")


def _rows(nm, p):
    return p[0].T if nm in _COL_SHARDED else p[0]


def _step(args, ts_mix_fwd, ts_ffn, ts_mix_bwd, tm_grad):
    (x, g_mix, w_in, b_in, w_dw, b_dw, ln_g, ln_b, w_pool, s_pool, w_out, g_ffn, w_gate, w_up, w_down, g_final,
     loss_target) = args[:17]
    names = ["g_mix", "w_in", "b_in", "w_dw", "b_dw", "ln_g", "ln_b", "w_pool", "s_pool", "w_out", "g_ffn",
             "w_gate", "w_up", "w_down", "g_final"]
    weights = dict(zip(names, args[1:16]))
    moms = dict(zip(names, args[17:32]))
    vars_ = dict(zip(names, args[32:47]))

    S, D = x.shape[1], x.shape[2]
    x2 = x.reshape(S, D)
    tgt2 = loss_target.reshape(S, D)

    shard = lambda nm: _rows(nm, weights[nm])
    w_pool_b = w_pool[0].astype(BF16)
    dw_rows = -(-CONV_WIDTH // SUBLANES) * SUBLANES
    dw_shard = jnp.pad(w_dw[0], ((0, dw_rows - CONV_WIDTH), (0, LANES - w_dw.shape[2])))

    a, gate, v, m, y, h1, xn, w_dw_f, g_in, g_out, _, g_gate, g_up, g_down = _mix_fwd(
        x2, g_mix, b_in, b_dw, ln_g, ln_b, w_pool_b, s_pool, [shard("w_in"), shard("w_out"), dw_shard],
        [shard("w_gate"), shard("w_up"), shard("w_down")], ts_mix_fwd)
    wt_in, w_out_f = g_in.reshape(-1, D), g_out.reshape(-1, D)
    wt_gate, wt_up, w_down_f = g_gate.reshape(-1, D), g_up.reshape(-1, D), g_down.reshape(-1, D)
    Fd = wt_gate.shape[0]
    f_chunks = [1024] * (Fd // 1024) + ([Fd % 1024] if Fd % 1024 else [])
    dh1, hn, act, dgt, dup, dh2, loss_p, dg_final, dg_ffn = _ffn(
        h1, tgt2, g_ffn, g_final.reshape(1, D), wt_gate, wt_up, w_down_f, ts_ffn, f_chunks)

    by_shard = lambda g: g.reshape(N_DEV, -1, D)
    ffn_grads, (s_gate, s_up) = _grad_ffn([dgt, dup, act], [hn, hn, dh2], tm_grad)
    p_gate, p_up, p_down = (by_shard(g) for g in ffn_grads)
    s_down, = _pair_exchange([p_down])
    pair_sums = _pair_sum([p_gate, p_up, p_down], [s_gate, s_up, s_down])
    (dx, dz, dh1b, dg_mix, db_in, dw_dw, db_dw, dln_g, dln_b, dw_pool, ds_pool, r_gate, r_up, r_down) = _mix_bwd(
        dh1, x2, a, gate, v, m, g_mix, wt_in, w_dw_f, ln_g, ln_b, w_pool_b, s_pool, w_out_f, pair_sums, ts_mix_bwd)

    small_names = ["g_mix", "b_in", "b_dw", "ln_g", "ln_b", "w_pool", "s_pool", "g_ffn", "g_final"]
    small_shape = lambda p: p.reshape(-1, p.shape[-1])
    partial = [dg_mix, db_in, db_dw, dln_g, dln_b, dw_pool.reshape(-1, POOL_GROUP), ds_pool, dg_ffn, dg_final, loss_p]
    dwt_in, r_out, r_dw, summed = _grad_mix(y, dh1b, dz, xn, _shard_cols(dw_dw[0:CONV_WIDTH]), partial)
    r_in, = _tail_reduce([dwt_in.reshape(N_DEV, -1, D)], [])

    big = {}
    recv = dict(zip(_BIG, [r_in, r_out, r_gate, r_up, r_down]))
    for call, group in (("adamw_ffn", ("w_gate", "w_up", "w_down")), ("adamw_mix", ("w_in", "w_out"))):
        results = _adamw_shards([recv[nm] for nm in group], [_rows(nm, weights[nm]) for nm in group],
                                [_rows(nm, moms[nm]) for nm in group], [_rows(nm, vars_[nm]) for nm in group], 2, call)
        for nm, res in zip(group, results):
            big[nm] = [o.T if nm in _COL_SHARDED else o for o in res]
    big["w_dw"], = _adamw_shards([r_dw], [w_dw[0]], [moms["w_dw"][0]], [vars_["w_dw"][0]], 1, "adamw_w_dw")

    sm = _adamw_small(summed[:-1], [small_shape(weights[nm]) for nm in small_names],
                      [small_shape(moms[nm]) for nm in small_names], [small_shape(vars_[nm]) for nm in small_names])
    n_small = len(small_names)

    def result(kind, nm):
        if nm in big:
            return big[nm][kind].reshape(weights[nm].shape)
        return sm[kind * n_small + small_names.index(nm)].reshape(weights[nm].shape)

    loss = summed[-1][0, 0]
    out = [loss, dx.reshape(x.shape)]
    for kind in range(4):
        out += [result(kind, nm) for nm in names]
    return tuple(out)


def kernel(x, g_mix, w_in, b_in, w_dw, b_dw, ln_g, ln_b, w_pool, s_pool, w_out, g_ffn, w_gate, w_up, w_down, g_final, loss_target, m_g_mix, m_w_in, m_b_in, m_w_dw, m_b_dw, m_ln_g, m_ln_b, m_w_pool, m_s_pool, m_w_out, m_g_ffn, m_w_gate, m_w_up, m_w_down, m_g_final, v_g_mix, v_w_in, v_b_in, v_w_dw, v_b_dw, v_ln_g, v_ln_b, v_w_pool, v_s_pool, v_w_out, v_g_ffn, v_w_gate, v_w_up, v_w_down, v_g_final):
    args = (x, g_mix, w_in, b_in, w_dw, b_dw, ln_g, ln_b, w_pool, s_pool, w_out, g_ffn, w_gate, w_up, w_down, g_final, loss_target, m_g_mix, m_w_in, m_b_in, m_w_dw, m_b_dw, m_ln_g, m_ln_b, m_w_pool, m_s_pool, m_w_out, m_g_ffn, m_w_gate, m_w_up, m_w_down, m_g_final, v_g_mix, v_w_in, v_b_in, v_w_dw, v_b_dw, v_ln_g, v_ln_b, v_w_pool, v_s_pool, v_w_out, v_g_ffn, v_w_gate, v_w_up, v_w_down, v_g_final)
    return _step(args, ts_mix_fwd=512, ts_ffn=256, ts_mix_bwd=512, tm_grad=256)
```

```python
import jax
import jax.numpy as jnp
from jax import lax
from jax.experimental import pallas as pl
from jax.experimental.pallas import tpu as pltpu

F32 = jnp.float32
BF16 = jnp.bfloat16
MESH = pl.DeviceIdType.MESH
N_DEV = 8

C_CONV = 512
CONV_WIDTH = 31
POOL_WINDOWS = (2, 4, 8, 16)
POOL_GROUP = 128
RMS_EPS = 1e-6
LN_EPS = 1e-5

ADAM_LR = 0.001
ADAM_B1 = 0.9
ADAM_B2 = 0.999
ADAM_EPS = 1e-08
ADAM_WD = 0.01
ADAM_STEP = 10

HALO = 32
SUBLANES = 8
LANES = 128
CONV_ROWS = 64
VMEM_LIMIT = 56 * 1024 * 1024


def _dot(a, b):
    return jnp.dot(a, b, preferred_element_type=F32)


def _dot_nt(a, b):
    return lax.dot_general(a, b, (((1,), (1,)), ((), ())), preferred_element_type=F32)


def _dot_tn(a, b):
    return lax.dot_general(a, b, (((0,), (0,)), ((), ())), preferred_element_type=F32)


def _mean_last(v):
    return jnp.mean(v, axis=-1, keepdims=True)


def _full(shape):
    nd = len(shape)
    return pl.BlockSpec(shape, lambda *_: (0,) * nd)


def _full1(shape):
    nd = len(shape)
    return pl.BlockSpec(shape, lambda *_: (0,) * nd, pipeline_mode=pl.Buffered(1))


def _shifted_copies(sh_ref, rows):
    for s in range(1, SUBLANES):
        sh_ref[s, 0:rows, :] = sh_ref[0, s:s + rows, :]


def _tap(sh_ref, off, r0, rows):
    q, s = divmod(off, SUBLANES)
    return sh_ref[s, pl.ds(r0 + q * SUBLANES, rows), :]


def _window_sums(src, bufs, rows, backward):
    assert all(w == 2 << g for g, w in enumerate(POOL_WINDOWS))
    n = len(POOL_WINDOWS)
    out = []
    for level in range(n):
        dst, shift = bufs[level % 2], 1 << level
        lanes = slice(level * POOL_GROUP, n * POOL_GROUP)
        lo, hi = (SUBLANES * (level + 1), rows) if backward else (0, rows - SUBLANES * (level + 1))
        other = slice(lo - shift, hi - shift) if backward else slice(lo + shift, hi + shift)
        dst[lo:hi, lanes] = src[lo:hi, lanes] + src[other, lanes]
        src = dst
        out.append(dst)
    return out


def _mix_fwd(x, g_mix, b_in, b_dw, ln_g, ln_b, w_pool, s_pool, mix_shards, ag_shards, ts):
    S, D = x.shape
    d_in = mix_shards[0].shape[0] * N_DEV
    C = C_CONV
    nt = S // ts
    nrb = ts // CONV_ROWS
    n_ag = len(ag_shards)
    relay_step = nt // 2
    mix_dtypes = [BF16, BF16, F32]

    def body(x_ref, g_ref, bin_ref, bdw_ref, lng_ref, lnb_ref, wp_ref, sp_ref, *rest):
        mx_in, ag_in, rest = rest[:3], rest[3:3 + n_ag], rest[3 + n_ag:]
        a_ref, gate_ref, v_ref, m_ref, y_ref, h1_ref, xn_ref, wdw_ref = rest[:8]
        mx_out, ag_out, rest = rest[8:11], rest[11:11 + n_ag], rest[11 + n_ag:]
        ush, pbuf, pa, pb, win_ref, wout_ref, gdw, load_sems = rest[:8]
        rest = rest[8:]
        (ma_stage, ma_bufs, ma_sems), rest = (rest[:2], rest[2:4], rest[4:7]), rest[7:]
        (mb_stage, mb_bufs, mb_sems), rest = (rest[:1], rest[1:2], rest[2:5]), rest[5:]
        ag_stage, ag_bufs, ag_sems = rest[:n_ag], rest[n_ag:2 * n_ag], rest[2 * n_ag:]
        i = pl.program_id(0)
        ma_start, ma_relay, ma_pass_on, ma_finish = _gather_plan(
            mx_in[0::2], mx_out[0::2], ma_stage, ma_bufs, *ma_sems)
        mb_start, mb_relay, mb_pass_on, mb_finish = _gather_plan(
            mx_in[1:2], mx_out[1:2], mb_stage, mb_bufs, *mb_sems)
        ag_start, ag_relay, ag_pass_on, ag_finish = _gather_plan(ag_in, ag_out, ag_stage, ag_bufs, *ag_sems)

        def load_weight(k, dst_ref):
            rows = mx_out[k].shape[1]
            loads = [pltpu.make_async_copy(mx_out[k].at[j], dst_ref.at[pl.ds(j * rows, rows), :],
                                           load_sems.at[N_DEV * k + j]) for j in range(N_DEV)]
            for cp in loads:
                cp.start()
            for cp in loads:
                cp.wait()

        @pl.when(i == 0)
        def _():
            ma_start()
            mb_start()
            ma_relay()
            mb_relay()
            ag_start()
            ma_pass_on()
            ma_finish()
            load_weight(0, win_ref)
            dw_load = pltpu.make_async_copy(mx_out[2], gdw, load_sems.at[2 * N_DEV])
            dw_load.start()
            dw_load.wait()
            first_half = lax.broadcasted_iota(jnp.int32, gdw.shape[1:], 1) < C // N_DEV
            for p in range(N_DEV // 2):
                wdw_ref[:, LANES * p:LANES * (p + 1)] = jnp.where(
                    first_half, gdw[2 * p], pltpu.roll(gdw[2 * p + 1], C // N_DEV, axis=1))
            ush[0, 0:HALO, :] = jnp.zeros((HALO, C), F32)
            pbuf[0:HALO, :] = jnp.zeros((HALO, C), F32)

        @pl.when(i == relay_step)
        def _():
            ag_relay()

        @pl.when(i == nt - 1)
        def _():
            ag_pass_on()

        x = x_ref[...]
        r1 = lax.rsqrt(_mean_last(x * x) + RMS_EPS)
        xn = (x * r1 * g_ref[...]).astype(BF16)
        xn_ref[...] = xn
        z = _dot_nt(xn, win_ref[...]) + bin_ref[...]
        a = z[:, 0:C]
        gate = z[:, C:2 * C]
        a_ref[...] = a
        gate_ref[...] = gate
        ush[0, HALO:HALO + ts, :] = a * jax.nn.sigmoid(gate)
        pbuf[HALO:HALO + ts, :] = z[:, 2 * C:]

        _shifted_copies(ush, ts + HALO - SUBLANES)

        def conv_block(rb, carry):
            r0 = pl.multiple_of(rb * CONV_ROWS, CONV_ROWS)
            acc = jnp.zeros((CONV_ROWS, C), F32)
            for k in range(CONV_WIDTH):
                acc = acc + wdw_ref[k:k + 1, :] * _tap(ush, HALO - (CONV_WIDTH - 1) + k, r0, CONV_ROWS)
            v_ref[pl.ds(r0, CONV_ROWS), :] = acc + bdw_ref[...]
            return carry

        lax.fori_loop(0, nrb, conv_block, 0)

        v = v_ref[...]
        mu = _mean_last(v)
        xc = v - mu
        rstd = lax.rsqrt(_mean_last(xc * xc) + LN_EPS)
        ln = xc * rstd * lng_ref[...] + lnb_ref[...]
        y_ref[:, 0:C] = (ln * jax.nn.sigmoid(ln)).astype(BF16)

        sums = _window_sums(pbuf, (pa, pb), ts + HALO, backward=True)
        row = lax.broadcasted_iota(jnp.int32, (ts, 1), 0) + i * ts
        for gi, w in enumerate(POOL_WINDOWS):
            lanes = slice(gi * POOL_GROUP, (gi + 1) * POOL_GROUP)
            seg = pbuf[HALO:HALO + ts, lanes]
            ws = sums[gi][HALO:HALO + ts, lanes]
            cnt = jnp.minimum(row + 1, w).astype(F32)
            m = (ws / cnt - seg).astype(BF16)
            m_ref[:, lanes] = m
            ypre = _dot(m, wp_ref[gi])
            y_ref[:, C + gi * POOL_GROUP:C + (gi + 1) * POOL_GROUP] = (ypre * sp_ref[:, lanes]).astype(BF16)

        @pl.when(i == 0)
        def _():
            mb_pass_on()
            mb_finish()
            load_weight(1, wout_ref)

        h1_ref[...] = x + _dot(y_ref[...], wout_ref[...])

        ush[0, 0:HALO, :] = ush[0, ts:ts + HALO, :]
        pbuf[0:HALO, :] = pbuf[ts:ts + HALO, :]

        @pl.when(i == nt - 1)
        def _():
            ag_finish()

    tile = lambda w, dt: (pl.BlockSpec((ts, w), lambda i: (i, 0)), jax.ShapeDtypeStruct((S, w), dt))
    wdw_rows = mix_shards[2].shape[0]
    outs = [tile(C, F32), tile(C, F32), tile(C, F32), tile(C, BF16), tile(D, BF16), tile(D, F32), tile(D, BF16),
            (_full((wdw_rows, C)), jax.ShapeDtypeStruct((wdw_rows, C), F32))]
    return pl.pallas_call(
        body, name="mix_fwd", grid=(nt,),
        in_specs=[pl.BlockSpec((ts, D), lambda i: (i, 0)), _full((1, D)), _full((1, d_in)), _full((1, C)),
                  _full((1, C)), _full((1, C)), _full(w_pool.shape), _full((1, C))] + [_HBM] * (3 + n_ag),
        out_specs=[o[0] for o in outs] + [_HBM] * (3 + n_ag),
        out_shape=[o[1] for o in outs] + _gather_out_shapes(mix_shards, mix_dtypes)
        + _gather_out_shapes(ag_shards, [BF16] * n_ag),
        scratch_shapes=[pltpu.VMEM((SUBLANES, ts + HALO, C), F32)] + [pltpu.VMEM((ts + HALO, C), F32)] * 3
        + [pltpu.VMEM((d_in, D), BF16), pltpu.VMEM((D, D), BF16),
                        pltpu.VMEM((N_DEV,) + mix_shards[2].shape, F32), pltpu.SemaphoreType.DMA((2 * N_DEV + 1,))]
        + _gather_scratch(mix_shards[0::2], mix_dtypes[0::2]) + _gather_scratch(mix_shards[1:2], mix_dtypes[1:2])
        + _gather_scratch(ag_shards, [BF16] * n_ag),
        compiler_params=pltpu.CompilerParams(dimension_semantics=("arbitrary",), vmem_limit_bytes=VMEM_LIMIT),
    )(x, g_mix, b_in, b_dw, ln_g, ln_b, w_pool, s_pool, *mix_shards, *ag_shards)


def _ffn(h1, target, g_ffn, g_final, w_gate, w_up, w_down, ts, f_chunks):
    S, D = h1.shape
    Fd = w_gate.shape[0]
    nt = S // ts
    bounds = []
    lo = 0
    for n in f_chunks:
        bounds.append((lo, lo + n))
        lo += n
    assert lo == Fd

    def body(h1_ref, tgt_ref, gf_ref, gl_ref, wg_ref, wu_ref, wd_ref,
             dh1_ref, hn_ref, act_ref, dgt_ref, dup_ref, dh2_ref, loss_ref, dgl_ref, dgf_ref, gt_s, up_s):
        i = pl.program_id(0)

        @pl.when(i == 0)
        def _():
            loss_ref[...] = jnp.zeros_like(loss_ref)
            dgl_ref[...] = jnp.zeros_like(dgl_ref)
            dgf_ref[...] = jnp.zeros_like(dgf_ref)

        h1 = h1_ref[...]
        r2 = lax.rsqrt(_mean_last(h1 * h1) + RMS_EPS)
        hhat = h1 * r2
        hn = (hhat * gf_ref[...]).astype(BF16)
        hn_ref[...] = hn
        h2 = h1
        for lo, hi in bounds:
            gt = _dot_nt(hn, wg_ref[lo:hi, :])
            up = _dot_nt(hn, wu_ref[lo:hi, :])
            gt_s[:, lo:hi] = gt
            up_s[:, lo:hi] = up
            act = (gt * jax.nn.sigmoid(gt) * up).astype(BF16)
            act_ref[:, lo:hi] = act
            h2 = h2 + _dot(act, wd_ref[lo:hi, :])

        r3 = lax.rsqrt(_mean_last(h2 * h2) + RMS_EPS)
        n3 = h2 * r3
        gl = gl_ref[...]
        diff = n3 * gl - tgt_ref[...]
        loss_ref[...] += jnp.sum(0.5 * jnp.sum(diff * diff, axis=-1, keepdims=True) / D, axis=0, keepdims=True)
        dout = diff / D
        dgl_ref[...] += jnp.sum(dout * n3, axis=0, keepdims=True)
        dn = dout * gl
        dh2 = r3 * (dn - n3 * _mean_last(dn * n3))
        dh2b = dh2.astype(BF16)
        dh2_ref[...] = dh2b

        dhn = jnp.zeros((ts, D), F32)
        for lo, hi in bounds:
            gt = gt_s[:, lo:hi]
            up = up_s[:, lo:hi]
            sg = jax.nn.sigmoid(gt)
            dact = _dot_nt(dh2b, wd_ref[lo:hi, :])
            dgt = (dact * up * (sg * (1.0 + gt * (1.0 - sg)))).astype(BF16)
            dup = (dact * (gt * sg)).astype(BF16)
            dgt_ref[:, lo:hi] = dgt
            dup_ref[:, lo:hi] = dup
            dhn = dhn + _dot(dgt, wg_ref[lo:hi, :]) + _dot(dup, wu_ref[lo:hi, :])

        dgf_ref[...] += jnp.sum(dhn * hhat, axis=0, keepdims=True)
        dnn = dhn * gf_ref[...]
        dh1_ref[...] = dh2 + r2 * (dnn - hhat * _mean_last(dnn * hhat))

    tile = lambda w, dt: (pl.BlockSpec((ts, w), lambda i: (i, 0)), jax.ShapeDtypeStruct((S, w), dt))
    acc = lambda w: (_full((1, w)), jax.ShapeDtypeStruct((1, w), F32))
    outs = [tile(D, F32), tile(D, BF16), tile(Fd, BF16), tile(Fd, BF16), tile(Fd, BF16), tile(D, BF16),
            acc(LANES), acc(D), acc(D)]
    return pl.pallas_call(
        body, name="ffn_fwd_bwd", grid=(nt,),
        in_specs=[pl.BlockSpec((ts, D), lambda i: (i, 0)), pl.BlockSpec((ts, D), lambda i: (i, 0)),
                  _full((1, D)), _full((1, D)), _full1((Fd, D)), _full1((Fd, D)), _full1((Fd, D))],
        out_specs=[o[0] for o in outs], out_shape=[o[1] for o in outs],
        scratch_shapes=[pltpu.VMEM((ts, Fd), F32), pltpu.VMEM((ts, Fd), F32)],
        compiler_params=pltpu.CompilerParams(dimension_semantics=("arbitrary",), vmem_limit_bytes=VMEM_LIMIT),
    )(h1, target, g_ffn, g_final, w_gate, w_up, w_down)


def _mix_bwd(dh1, x, a, gate, v, m, g_mix, w_in, w_dw, ln_g, ln_b, w_pool, s_pool, w_out, rs_parts, ts):
    S, D = x.shape
    n_rs = len(rs_parts)
    d_in = w_in.shape[0]
    C = C_CONV
    nt = S // ts
    nrb = ts // CONV_ROWS
    wrows =((CONV_WIDTH + SUBLANES - 1) // SUBLANES) * SUBLANES

    def body(dh1_ref, x_ref, a_ref, gate_ref, v_ref, m_ref, g_ref, win_ref, wdw_ref, lng_ref,
             lnb_ref, wp_ref, sp_ref, wout_ref, *rest):
        rs_in, rest = rest[:n_rs], rest[n_rs:]
        (dx_ref, dz_ref, dh1b_ref, dgm_ref, dbin_ref, dwdw_ref, dbdw_ref, dlng_ref, dlnb_ref, dwp_ref,
         dsp_ref) = rest[:11]
        rs_out, rest = rest[11:11 + n_rs], rest[11 + n_rs:]
        dvsh, dqbuf, qa, qb, du_s, dm_s = rest[:6]
        rs_bufs, (send_sems, recv_sems, local_sems) = rest[6:6 + n_rs], rest[6 + n_rs:]
        i = pl.program_id(0)
        t = nt - 1 - i
        rs_start, rs_finish = _chip_exchange_plan(rs_in, rs_out, rs_bufs, send_sems, recv_sems, local_sems)

        @pl.when(i == 0)
        def _():
            rs_start()
            dvsh[0, ts:ts + HALO, :] = jnp.zeros((HALO, C), F32)
            dqbuf[ts:ts + HALO, :] = jnp.zeros((HALO, C), F32)
            for r in (dgm_ref, dbin_ref, dwdw_ref, dbdw_ref, dlng_ref, dlnb_ref, dwp_ref, dsp_ref):
                r[...] = jnp.zeros_like(r)

        dh1 = dh1_ref[...]
        dh1b = dh1.astype(BF16)
        dh1b_ref[...] = dh1b
        dy = _dot_nt(dh1b, wout_ref[...])

        v = v_ref[...]
        mu = _mean_last(v)
        xc = v - mu
        rstd = lax.rsqrt(_mean_last(xc * xc) + LN_EPS)
        vhat = xc * rstd
        lng = lng_ref[...]
        ln = vhat * lng + lnb_ref[...]
        sg = jax.nn.sigmoid(ln)
        dln = dy[:, 0:C] * (sg * (1.0 + ln * (1.0 - sg)))
        dlng_ref[...] += jnp.sum(dln * vhat, axis=0, keepdims=True)
        dlnb_ref[...] += jnp.sum(dln, axis=0, keepdims=True)
        dvh = dln * lng
        dv = rstd * (dvh - _mean_last(dvh) - vhat * _mean_last(dvh * vhat))
        dbdw_ref[...] += jnp.sum(dv, axis=0, keepdims=True)
        dvsh[0, 0:ts, :] = dv
        _shifted_copies(dvsh, ts + HALO - SUBLANES)

        def conv_block(rb, carry):
            r0 = pl.multiple_of(rb * CONV_ROWS, CONV_ROWS)
            acc = jnp.zeros((CONV_ROWS, C), F32)
            for k in range(CONV_WIDTH):
                acc = acc + wdw_ref[k:k + 1, :] * _tap(dvsh, CONV_WIDTH - 1 - k, r0, CONV_ROWS)
            du_s[pl.ds(r0, CONV_ROWS), :] = acc
            return carry

        lax.fori_loop(0, nrb, conv_block, 0)

        a = a_ref[...]
        sgate = jax.nn.sigmoid(gate_ref[...])
        u = a * sgate
        for k in range(CONV_WIDTH):
            q, s = divmod(CONV_WIDTH - 1 - k, SUBLANES)
            prod = u * dvsh[s, q * SUBLANES:q * SUBLANES + ts, :]
            dwdw_ref[k:k + 1, :] += jnp.sum(prod, axis=0, keepdims=True)

        du = du_s[...]
        da = du * sgate
        dgate = du * a * sgate * (1.0 - sgate)
        dz_ref[:, 0:C] = da.astype(BF16)
        dz_ref[:, C:2 * C] = dgate.astype(BF16)
        dbin_ref[:, 0:C] += jnp.sum(da, axis=0, keepdims=True)
        dbin_ref[:, C:2 * C] += jnp.sum(dgate, axis=0, keepdims=True)

        row = lax.broadcasted_iota(jnp.int32, (ts, 1), 0) + t * ts
        for gi, w in enumerate(POOL_WINDOWS):
            lanes = slice(gi * POOL_GROUP, (gi + 1) * POOL_GROUP)
            dyp = dy[:, C + gi * POOL_GROUP:C + (gi + 1) * POOL_GROUP]
            mg = m_ref[:, lanes]
            ypre = _dot(mg, wp_ref[gi])
            dsp_ref[:, lanes] += jnp.sum(dyp * ypre, axis=0, keepdims=True)
            dyi = (dyp * sp_ref[:, lanes]).astype(BF16)
            dwp_ref[gi] += _dot_tn(mg, dyi)
            dm = _dot_nt(dyi, wp_ref[gi])
            cnt = jnp.minimum(row + 1, w).astype(F32)
            dqbuf[0:ts, lanes] = dm / cnt
            dm_s[:, lanes] = dm
        sums = _window_sums(dqbuf, (qa, qb), ts + HALO, backward=False)
        for gi in range(len(POOL_WINDOWS)):
            lanes = slice(gi * POOL_GROUP, (gi + 1) * POOL_GROUP)
            dp = sums[gi][0:ts, lanes] - dm_s[:, lanes]
            dz_ref[:, 2 * C + gi * POOL_GROUP:2 * C + (gi + 1) * POOL_GROUP] = dp.astype(BF16)
            dbin_ref[:, 2 * C + gi * POOL_GROUP:2 * C + (gi + 1) * POOL_GROUP] += jnp.sum(dp, axis=0, keepdims=True)

        dxn = _dot(dz_ref[...], win_ref[...])
        x = x_ref[...]
        r1 = lax.rsqrt(_mean_last(x * x) + RMS_EPS)
        xhat = x * r1
        dgm_ref[...] += jnp.sum(dxn * xhat, axis=0, keepdims=True)
        dnn = dxn * g_ref[...]
        dx_ref[...] = dh1 + r1 * (dnn - xhat * _mean_last(dnn * xhat))

        dvsh[0, ts:ts + HALO, :] = dvsh[0, 0:HALO, :]
        dqbuf[ts:ts + HALO, :] = dqbuf[0:HALO, :]

        @pl.when(i == nt - 1)
        def _():
            rs_finish()

    rev = lambda w: pl.BlockSpec((ts, w), lambda i: (nt - 1 - i, 0))
    acc = lambda shape: (_full(shape), jax.ShapeDtypeStruct(shape, F32))
    outs = [(rev(D), jax.ShapeDtypeStruct((S, D), F32)), (rev(d_in), jax.ShapeDtypeStruct((S, d_in), BF16)),
            (rev(D), jax.ShapeDtypeStruct((S, D), BF16)), acc((1, D)), acc((1, d_in)), acc((wrows, C)),
            acc((1, C)), acc((1, C)), acc((1, C)), acc(w_pool.shape), acc((1, C))]
    return pl.pallas_call(
        body, name="mix_bwd", grid=(nt,),
        in_specs=[rev(D), rev(D), rev(C), rev(C), rev(C), rev(C), _full((1, D)), _full((d_in, D)),
                  _full(w_dw.shape), _full((1, C)), _full((1, C)), _full(w_pool.shape), _full((1, C)),
                  _full((D, D))] + [_HBM] * n_rs,
        out_specs=[o[0] for o in outs] + [_HBM] * n_rs,
        out_shape=[o[1] for o in outs] + [jax.ShapeDtypeStruct(p.shape, p.dtype) for p in rs_parts],
        scratch_shapes=[pltpu.VMEM((SUBLANES, ts + HALO, C), F32)] + [pltpu.VMEM((ts + HALO, C), F32)] * 3
        + [pltpu.VMEM((ts, C), F32)] * 2
        + [pltpu.VMEM(p.shape[1:], p.dtype) for p in rs_parts] + _comm_sems(n_rs, 3),
        compiler_params=pltpu.CompilerParams(dimension_semantics=("arbitrary",), vmem_limit_bytes=VMEM_LIMIT),
    )(dh1, x, a, gate, v, m, g_mix, w_in, w_dw, ln_g, ln_b, w_pool, s_pool, w_out, *rs_parts)


def _grad_mix(y, dh1b, dz, xn, dw_dw, vals):
    S, D = y.shape
    d_in = dz.shape[1]
    r_out, r_in = D // N_DEV, d_in // N_DEV
    n_out, n_in = N_DEV // 2, N_DEV // 2
    steps = n_out + n_in
    ns = len(vals)
    dw_out = jax.ShapeDtypeStruct((N_DEV, r_out, D), BF16)
    dw_in = jax.ShapeDtypeStruct((N_DEV, r_in, D), BF16)
    s_scratch, o_scratch = _reduce_scratch([], vals), _reduce_scratch([dw_out, dw_dw], [])
    i_scratch = _reduce_scratch([dw_in], [])

    def body(y_ref, dh_ref, dz_ref, xn_ref, dwdw_ref, *rest):
        v_in, p_out, v_out = rest[:ns], rest[ns:ns + 3], rest[ns + 3:2 * ns + 3]
        rest = rest[2 * ns + 3:]
        dwo_v, dwi_v, rest = rest[0], rest[1], rest[2:]
        s_refs, rest = rest[:len(s_scratch)], rest[len(s_scratch):]
        o_refs, i_refs = rest[:len(o_scratch)], rest[len(o_scratch):]
        s_start, s_middle, s_finish = _reduce_plan((), v_in, (), v_out, s_refs)
        o_start, o_middle, o_finish = _reduce_plan((dwo_v, dwdw_ref), (), p_out[:2], (), o_refs)
        i_start, i_middle, i_finish = _reduce_plan((dwi_v,), (), p_out[2:], (), i_refs)
        s = pl.program_id(0)
        pl.when(s == 0)(s_start)
        pl.when(s == 1)(s_middle)

        @pl.when(s < n_out)
        def _():
            res = _dot_tn(y_ref[...], dh_ref[...]).astype(BF16)
            dwo_v[2 * s] = res[0:r_out]
            dwo_v[2 * s + 1] = res[r_out:]

        pl.when(s == n_out - 1)(s_finish)
        pl.when(s == n_out)(o_start)
        pl.when(s == n_out + 1)(o_middle)

        @pl.when(s >= n_out)
        def _():
            res = _dot_tn(dz_ref[...], xn_ref[...]).astype(BF16)
            dwi_v[2 * (s - n_out)] = res[0:r_in]
            dwi_v[2 * (s - n_out) + 1] = res[r_in:]

        @pl.when(s == steps - 1)
        def _():
            o_finish()
            i_start()
            i_middle()
            i_finish()

    late = lambda s: jnp.clip(s - n_out, 0, n_in - 1)
    res = pl.pallas_call(
        body, name="grad_w_mix", grid=(steps,),
        in_specs=[pl.BlockSpec((S, 2 * r_out), lambda s: (0, jnp.minimum(s, n_out - 1))), _full1((S, D)),
                  pl.BlockSpec((S, 2 * r_in), lambda s: (0, late(s))), _full1((S, D))] + [_VMEM] * (1 + ns),
        out_specs=[_VMEM] * (3 + ns),
        out_shape=_reduce_out_shapes([dw_out, dw_dw, dw_in], []) + _reduce_out_shapes([], vals),
        scratch_shapes=[pltpu.VMEM(dw_out.shape, BF16), pltpu.VMEM(dw_in.shape, BF16)] + s_scratch + o_scratch
        + i_scratch,
        compiler_params=pltpu.CompilerParams(dimension_semantics=("arbitrary",), vmem_limit_bytes=VMEM_LIMIT),
    )(y, dh1b, dz, xn, dw_dw, *vals)
    return res[0], res[1], res[2], res[3:]


def _grad_ffn(a_list, b_list, tm):
    n = len(a_list)
    S, M = a_list[0].shape
    N = b_list[0].shape[1]
    nb = M // tm
    steps = n * nb
    R = M // N_DEV
    b_unique = [b for k, b in enumerate(b_list) if all(b is not o for o in b_list[:k])]
    b_index = [[b is o for o in b_unique].index(True) for b in b_list]

    def body(*refs):
        a_refs, b_refs = refs[:n], refs[n:n + len(b_unique)]
        rest = refs[n + len(b_unique):]
        g_refs, sib_refs, (obuf, w_sems, send_sems, recv_sems) = rest[:n], rest[n:2 * n - 1], rest[2 * n - 1:]
        s = pl.program_id(0)
        slot = s % 2
        x, y, c = _position()

        def write_back(dst):
            return pltpu.make_async_copy(obuf.at[slot], dst, w_sems.at[slot])

        def exchange(k):
            return [pltpu.make_async_remote_copy(
                src_ref=g_refs[k].at[pl.ds(pl.multiple_of((2 * q + 1 - c) * R, SUBLANES * 2), R), :],
                dst_ref=sib_refs[k].at[q], send_sem=send_sems.at[k, q], recv_sem=recv_sems.at[k, q],
                device_id=(x, y, 1 - c), device_id_type=MESH) for q in range(N_CHIPS)]

        pl.when(s >= 2)(write_back(g_refs[0].at[pl.ds(0, tm), :]).wait)
        for k in range(n):
            @pl.when(jnp.logical_and(s >= k * nb, s < (k + 1) * nb))
            def _(k=k):
                obuf[slot] = _dot_tn(a_refs[k][...], b_refs[b_index[k]][...]).astype(BF16)
                row = pl.multiple_of((s - k * nb) * tm, tm)
                write_back(g_refs[k].at[pl.ds(row, tm), :]).start()
            if k < n - 1:
                @pl.when(s == (k + 1) * nb + 2)
                def _(k=k):
                    for cp in exchange(k):
                        cp.start()

        @pl.when(s == steps - 1)
        def _():
            write_back(g_refs[0].at[pl.ds(0, tm), :]).wait()
            pltpu.make_async_copy(obuf.at[1 - slot], g_refs[0].at[pl.ds(0, tm), :], w_sems.at[1 - slot]).wait()
            for k in range(n - 1):
                for cp in exchange(k):
                    cp.wait()

    clamp = lambda k: (lambda s: (0, jnp.clip(s - k * nb, 0, nb - 1)))
    sems = pltpu.SemaphoreType.DMA((n - 1, N_CHIPS))
    res = pl.pallas_call(
        body, name="grad_w_ffn", grid=(steps,),
        in_specs=[pl.BlockSpec((S, tm), clamp(k)) for k in range(n)] + [_full1((S, N))] * len(b_unique),
        out_specs=[_HBM] * (2 * n - 1),
        out_shape=[jax.ShapeDtypeStruct((M, N), BF16)] * n
        + [jax.ShapeDtypeStruct((N_CHIPS, R, N), BF16)] * (n - 1),
        scratch_shapes=[pltpu.VMEM((2, tm, N), BF16), pltpu.SemaphoreType.DMA((2,)), sems, sems],
        compiler_params=pltpu.CompilerParams(dimension_semantics=("arbitrary",), vmem_limit_bytes=VMEM_LIMIT),
    )(*a_list, *b_unique)
    return res[:n], res[n:]


def _position():
    return lax.axis_index("x"), lax.axis_index("y"), lax.axis_index("c")


def _slot(px, py, pc):
    return 4 * px + 2 * py + pc


_HBM = pl.BlockSpec(memory_space=pl.ANY)


def _comm_sems(n, copies):
    return [pltpu.SemaphoreType.DMA((n, copies)), pltpu.SemaphoreType.DMA((n, copies)),
            pltpu.SemaphoreType.DMA((n, 2))]


def _local_copy(srcs, dsts, bufs, local_sems):
    n = len(srcs)
    loads = [pltpu.make_async_copy(srcs[k], bufs[k], local_sems.at[k, 0]) for k in range(n)]
    for cp in loads:
        cp.start()
    for cp in loads:
        cp.wait()
    stores = _local_stores(dsts, bufs, local_sems)
    for cp in stores:
        cp.start()
    return stores


def _local_stores(dsts, bufs, local_sems):
    return [pltpu.make_async_copy(bufs[k], dsts[k], local_sems.at[k, 1]) for k in range(len(dsts))]


def _gather_out_shapes(shards, dtypes):
    return [jax.ShapeDtypeStruct((N_DEV,) + s.shape, dt) for s, dt in zip(shards, dtypes)]


def _gather_scratch(shards, dtypes):
    return ([pltpu.VMEM(s.shape, s.dtype) for s in shards] + [pltpu.VMEM(s.shape, dt) for s, dt in zip(shards, dtypes)]
            + _comm_sems(len(shards), 7))


def _gather_plan(ins, outs, stage, bufs, send_sems, recv_sems, local_sems):
    n = len(ins)
    x, y, c = _position()
    me, sibling = (x, y, c), (x, y, 1 - c)
    na, nb, dg = (x ^ (1 - c), y ^ c), (x ^ c, y ^ (1 - c)), (1 - x, 1 - y)
    own = [outs[k].at[_slot(*me)] for k in range(n)]

    def copy(k, sem, block, to, src=None):
        dst = outs[k].at[_slot(*block)]
        return pltpu.make_async_remote_copy(
            src_ref=dst if src is None else src, dst_ref=dst, send_sem=send_sems.at[k, sem],
            recv_sem=recv_sems.at[k, sem], device_id=to, device_id_type=MESH)

    def first():
        cps = []
        for k in range(n):
            cps += [copy(k, 0, me, sibling, src=bufs[k]), copy(k, 1, me, (*na, c), src=bufs[k]),
                    copy(k, 2, me, (*nb, c), src=bufs[k])]
        return cps

    def onward():
        return [copy(k, 3, (*na, c), (*nb, c)) for k in range(n)]

    def to_sibling(j, chip):
        return [copy(k, 4 + j, (*chip, c), sibling) for k in range(n)]

    def start():
        loads = [pltpu.make_async_copy(ins[k], stage[k], local_sems.at[k, 0]) for k in range(n)]
        for cp in loads:
            cp.start()
        for cp in loads:
            cp.wait()
        for k in range(n):
            bufs[k][...] = stage[k][...].astype(bufs[k].dtype)
        for cp in first() + _local_stores(own, bufs, local_sems):
            cp.start()

    def relay():
        for k in range(n):
            copy(k, 1, (*na, c), me).wait_recv()
        for cp in onward() + to_sibling(0, na):
            cp.start()
        for k in range(n):
            copy(k, 2, (*nb, c), me).wait_recv()
        for cp in to_sibling(1, nb):
            cp.start()

    def pass_on():
        for k in range(n):
            copy(k, 3, (*dg, c), me).wait_recv()
        for cp in to_sibling(2, dg):
            cp.start()

    def finish():
        for k in range(n):
            copy(k, 0, sibling, me).wait_recv()
            for j, chip in enumerate((nb, na, dg)):
                copy(k, 4 + j, (*chip, 1 - c), me).wait_recv()
        for cp in first() + onward() + to_sibling(0, na) + to_sibling(1, nb) + to_sibling(2, dg):
            cp.wait_send()
        for cp in _local_stores(own, bufs, local_sems):
            cp.wait()

    return start, relay, pass_on, finish


N_CHIPS = 4
_CHIP_FLIPS = [(1, 0), (0, 1), (1, 1)]
F32_TRAVEL_LIMIT = 4096


def _chip_exchange_plan(ins, outs, bufs, send_sems, recv_sems, local_sems):
    n = len(ins)
    x, y, c = _position()
    my_q = 2 * x + y
    peers = [(x ^ fx, y ^ fy) for fx, fy in _CHIP_FLIPS]
    own = [outs[k].at[my_q] for k in range(n)]

    def sends():
        return [pltpu.make_async_remote_copy(
            src_ref=ins[k].at[2 * px + py], dst_ref=outs[k].at[my_q], send_sem=send_sems.at[k, f],
            recv_sem=recv_sems.at[k, f], device_id=(px, py, c), device_id_type=MESH)
            for f, (px, py) in enumerate(peers) for k in range(n)]

    def start():
        for cp in sends():
            cp.start()
        _local_copy([ins[k].at[my_q] for k in range(n)], own, bufs, local_sems)

    def finish():
        for f, (px, py) in enumerate(peers):
            for k in range(n):
                pltpu.make_async_remote_copy(
                    src_ref=ins[k].at[my_q], dst_ref=outs[k].at[2 * px + py], send_sem=send_sems.at[k, f],
                    recv_sem=recv_sems.at[k, f], device_id=(px, py, c), device_id_type=MESH).wait_recv()
        for cp in sends():
            cp.wait_send()
        for cp in _local_stores(own, bufs, local_sems):
            cp.wait()

    return start, finish


def _pair_exchange(parts):
    n = len(parts)

    def body(*refs):
        start, finish = _pair_exchange_plan(refs[:n], refs[n:2 * n], *refs[2 * n:])
        start()
        finish()

    sems = pltpu.SemaphoreType.DMA((n, N_CHIPS))
    return pl.pallas_call(
        body, name="pair_exchange", out_shape=_pair_out_shapes(parts),
        in_specs=[_HBM] * n, out_specs=[_HBM] * n, scratch_shapes=[sems, sems],
    )(*parts)


def _pair_out_shapes(parts):
    return [jax.ShapeDtypeStruct((N_CHIPS,) + p.shape[1:], p.dtype) for p in parts]


def _pair_exchange_plan(ins, sib, send_sems, recv_sems):
    x, y, c = _position()

    def copies():
        return [pltpu.make_async_remote_copy(
            src_ref=ins[k].at[2 * q + 1 - c], dst_ref=sib[k].at[q], send_sem=send_sems.at[k, q],
            recv_sem=recv_sems.at[k, q], device_id=(x, y, 1 - c), device_id_type=MESH)
            for k in range(len(ins)) for q in range(N_CHIPS)]

    def start():
        for cp in copies():
            cp.start()

    def finish():
        for cp in copies():
            cp.wait()

    return start, finish


def _pair_sum(parts, sib):
    n = len(parts)

    def body(*refs):
        c = lax.axis_index("c")
        for k in range(n):
            refs[2 * n + k][0] = (refs[k][0, c].astype(F32) + refs[n + k][0].astype(F32)).astype(BF16)

    pair = [pl.BlockSpec((1, 2) + p.shape[1:], lambda q: (q, 0, 0, 0)) for p in parts]
    one = [pl.BlockSpec((1,) + p.shape[1:], lambda q: (q, 0, 0)) for p in parts]
    return pl.pallas_call(
        body, name="pair_sum", grid=(N_CHIPS,), in_specs=pair + one, out_specs=one,
        out_shape=[jax.ShapeDtypeStruct(s.shape, BF16) for s in sib],
        compiler_params=pltpu.CompilerParams(dimension_semantics=("parallel",)),
    )(*[p.reshape((N_CHIPS, 2) + p.shape[1:]) for p in parts], *sib)


_VMEM = pl.BlockSpec(memory_space=pltpu.VMEM)


def _reduce_out_shapes(parts, vals):
    return ([jax.ShapeDtypeStruct((N_CHIPS,) + p.shape[1:], p.dtype) for p in parts]
            + [jax.ShapeDtypeStruct(v.shape, v.dtype) for v in vals])


def _reduce_scratch(parts, vals):
    n, ns = len(parts), len(vals)
    quarter = [pltpu.VMEM((N_CHIPS,) + p.shape[1:], p.dtype) for p in parts]
    dma = pltpu.SemaphoreType.DMA
    travel = [BF16 if v.size > F32_TRAVEL_LIMIT else v.dtype for v in vals]
    return (quarter * 3 + [pltpu.VMEM(v.shape, v.dtype) for v in vals]
            + [pltpu.VMEM(v.shape, t) for v, t in zip(vals, travel)]
            + [pltpu.VMEM((N_CHIPS,) + v.shape, t) for v, t in zip(vals, travel)]
            + [dma((max(n, 1), N_CHIPS)), dma((max(n, 1), N_CHIPS)), dma((max(ns, 1),)), dma((max(ns, 1),)),
               dma((max(n, 1), 3)), dma((max(n, 1), 3)), dma((max(ns, 1), 3)), dma((max(ns, 1), 3))])


def _reduce_plan(p_in, v_in, p_out, v_out, scratch):
    n, ns = len(p_in), len(v_in)
    p_sib, p_sum, p_all = scratch[:n], scratch[n:2 * n], scratch[2 * n:3 * n]
    v_sib, v_sum, v_all = (scratch[3 * n + j * ns:3 * n + (j + 1) * ns] for j in range(3))
    p1_send, p1_recv, v1_send, v1_recv, p3_send, p3_recv, v3_send, v3_recv = scratch[3 * n + 3 * ns:]
    x, y, c = _position()
    my_q = 2 * x + y
    peers = [(x ^ fx, y ^ fy) for fx, fy in _CHIP_FLIPS]

    def to_sibling(src, dst, send, recv):
        return pltpu.make_async_remote_copy(src_ref=src, dst_ref=dst, send_sem=send, recv_sem=recv,
                                            device_id=(x, y, 1 - c), device_id_type=MESH)

    def level1():
        cps = [to_sibling(p_in[k].at[2 * q + 1 - c], p_sib[k].at[q], p1_send.at[k, q], p1_recv.at[k, q])
               for k in range(n) for q in range(N_CHIPS)]
        return cps + [to_sibling(v_in[k], v_sib[k], v1_send.at[k], v1_recv.at[k]) for k in range(ns)]

    def to_chip(f, src, dst, send, recv):
        px, py = peers[f]
        return pltpu.make_async_remote_copy(src_ref=src, dst_ref=dst, send_sem=send, recv_sem=recv,
                                            device_id=(px, py, c), device_id_type=MESH)

    def level2(sending):
        cps = []
        for f, (px, py) in enumerate(peers):
            their_q = 2 * px + py
            for k in range(n):
                src, dst = (p_sum[k].at[their_q], p_all[k].at[my_q]) if sending else (
                    p_sum[k].at[my_q], p_all[k].at[their_q])
                cps.append(to_chip(f, src, dst, p3_send.at[k, f], p3_recv.at[k, f]))
            for k in range(ns):
                dst = v_all[k].at[my_q] if sending else v_all[k].at[their_q]
                cps.append(to_chip(f, v_sum[k], dst, v3_send.at[k, f], v3_recv.at[k, f]))
        return cps

    def start():
        for cp in level1():
            cp.start()

    def middle():
        for cp in level1():
            cp.wait_recv()
        for k in range(n):
            for q in range(N_CHIPS):
                p_sum[k][q] = (p_in[k][2 * q + c].astype(F32) + p_sib[k][q].astype(F32)).astype(p_sum[k].dtype)
        for k in range(ns):
            v_sum[k][...] = (v_in[k][...] + v_sib[k][...]).astype(v_sum[k].dtype)
        for cp in level2(True):
            cp.start()
        for k in range(n):
            p_all[k][my_q] = p_sum[k][my_q]
        for k in range(ns):
            v_all[k][my_q] = v_sum[k][...]

    def finish():
        for cp in level2(False):
            cp.wait_recv()
        for k in range(n):
            p_out[k][...] = p_all[k][...]
        for k in range(ns):
            total = v_all[k][0].astype(F32)
            for q in range(1, N_CHIPS):
                total = total + v_all[k][q].astype(F32)
            v_out[k][...] = total
        for cp in level1() + level2(True):
            cp.wait_send()

    return start, middle, finish


def _adamw_math(w, g, m, v):
    m = ADAM_B1 * m + (1.0 - ADAM_B1) * g
    v = ADAM_B2 * v + (1.0 - ADAM_B2) * (g * g)
    m_hat = m / (1.0 - ADAM_B1 ** ADAM_STEP)
    v_hat = v / (1.0 - ADAM_B2 ** ADAM_STEP)
    delta = -ADAM_LR * (m_hat / (jnp.sqrt(v_hat) + ADAM_EPS) + ADAM_WD * w)
    return delta, m, v


def _adamw_shards(parts, ws, ms, vs, steps, name):
    n = len(ws)

    def body(*refs):
        ins, outs = refs[:4 * n], refs[4 * n:]
        for k in range(n):
            p_ref, w_ref, m_ref, v_ref = ins[4 * k:4 * k + 4]
            g = p_ref[0].astype(F32)
            for j in range(1, N_CHIPS):
                g = g + p_ref[j].astype(F32)
            d, mn, vn = _adamw_math(w_ref[...], g, m_ref[...], v_ref[...])
            for o, val in zip(outs[4 * k:4 * k + 4], (g, d, mn, vn)):
                o[...] = val

    in_specs, out_specs, out_shape, operands = [], [], [], []
    for p, w, m, v in zip(parts, ws, ms, vs):
        R, Cc = w.shape
        blk = pl.BlockSpec((R // steps, Cc), lambda i: (i, 0))
        in_specs += [pl.BlockSpec((N_CHIPS, R // steps, Cc), lambda i: (0, i, 0)), blk, blk, blk]
        out_specs += [blk] * 4
        out_shape += [jax.ShapeDtypeStruct((R, Cc), F32)] * 4
        operands += [p, w, m, v]
    res = pl.pallas_call(
        body, name=name, grid=(steps,), in_specs=in_specs, out_specs=out_specs, out_shape=out_shape,
        compiler_params=pltpu.CompilerParams(dimension_semantics=("parallel",), vmem_limit_bytes=VMEM_LIMIT),
    )(*operands)
    return [res[4 * k:4 * k + 4] for k in range(n)]


def _adamw_small(grads, ws, ms, vs):
    n = len(grads)
    vm = pl.BlockSpec(memory_space=pltpu.VMEM)

    def body(*refs):
        g_in, w_in, m_in, v_in = (refs[k * n:(k + 1) * n] for k in range(4))
        g_out, d_out, m_out, v_out = (refs[(4 + k) * n:(5 + k) * n] for k in range(4))
        for k in range(n):
            g = g_in[k][...]
            d, mn, vn = _adamw_math(w_in[k][...], g, m_in[k][...], v_in[k][...])
            g_out[k][...] = g
            d_out[k][...] = d
            m_out[k][...] = mn
            v_out[k][...] = vn

    shapes = [jax.ShapeDtypeStruct(g.shape, F32) for g in grads]
    return pl.pallas_call(
        body, name="adamw_small", out_shape=shapes * 4, in_specs=[vm] * (4 * n), out_specs=[vm] * (4 * n),
    )(*grads, *ws, *ms, *vs)


def _shard_cols(full):
    R, Ct = full.shape
    return jnp.transpose(full.reshape(R, N_DEV, Ct // N_DEV), (1, 0, 2))


_COL_SHARDED = ("w_in", "w_gate", "w_up")
_BIG = ("w_in", "w_out", "w_gate", "w_up", "w_down")


def _rows(nm, p):
    return p[0].T if nm in _COL_SHARDED else p[0]


def _step(args, ts_mix_fwd, ts_ffn, ts_mix_bwd, tm_grad):
    (x, g_mix, w_in, b_in, w_dw, b_dw, ln_g, ln_b, w_pool, s_pool, w_out, g_ffn, w_gate, w_up, w_down, g_final,
     loss_target) = args[:17]
    names = ["g_mix", "w_in", "b_in", "w_dw", "b_dw", "ln_g", "ln_b", "w_pool", "s_pool", "w_out", "g_ffn",
             "w_gate", "w_up", "w_down", "g_final"]
    weights = dict(zip(names, args[1:16]))
    moms = dict(zip(names, args[17:32]))
    vars_ = dict(zip(names, args[32:47]))

    S, D = x.shape[1], x.shape[2]
    x2 = x.reshape(S, D)
    tgt2 = loss_target.reshape(S, D)

    shard = lambda nm: _rows(nm, weights[nm])
    w_pool_b = w_pool[0].astype(BF16)
    dw_rows = -(-CONV_WIDTH // SUBLANES) * SUBLANES
    dw_shard = jnp.pad(w_dw[0], ((0, dw_rows - CONV_WIDTH), (0, LANES - w_dw.shape[2])))

    a, gate, v, m, y, h1, xn, w_dw_f, g_in, g_out, _, g_gate, g_up, g_down = _mix_fwd(
        x2, g_mix, b_in, b_dw, ln_g, ln_b, w_pool_b, s_pool, [shard("w_in"), shard("w_out"), dw_shard],
        [shard("w_gate"), shard("w_up"), shard("w_down")], ts_mix_fwd)
    wt_in, w_out_f = g_in.reshape(-1, D), g_out.reshape(-1, D)
    wt_gate, wt_up, w_down_f = g_gate.reshape(-1, D), g_up.reshape(-1, D), g_down.reshape(-1, D)
    Fd = wt_gate.shape[0]
    f_chunks = [1024] * (Fd // 1024) + ([Fd % 1024] if Fd % 1024 else [])
    dh1, hn, act, dgt, dup, dh2, loss_p, dg_final, dg_ffn = _ffn(
        h1, tgt2, g_ffn, g_final.reshape(1, D), wt_gate, wt_up, w_down_f, ts_ffn, f_chunks)

    by_shard = lambda g: g.reshape(N_DEV, -1, D)
    ffn_grads, (s_gate, s_up) = _grad_ffn([dgt, dup, act], [hn, hn, dh2], tm_grad)
    p_gate, p_up, p_down = (by_shard(g) for g in ffn_grads)
    s_down, = _pair_exchange([p_down])
    pair_sums = _pair_sum([p_gate, p_up, p_down], [s_gate, s_up, s_down])
    (dx, dz, dh1b, dg_mix, db_in, dw_dw, db_dw, dln_g, dln_b, dw_pool, ds_pool, r_gate, r_up, r_down) = _mix_bwd(
        dh1, x2, a, gate, v, m, g_mix, wt_in, w_dw_f, ln_g, ln_b, w_pool_b, s_pool, w_out_f, pair_sums, ts_mix_bwd)

    small_names = ["g_mix", "b_in", "b_dw", "ln_g", "ln_b", "w_pool", "s_pool", "g_ffn", "g_final"]
    small_shape = lambda p: p.reshape(-1, p.shape[-1])
    partial = [dg_mix, db_in, db_dw, dln_g, dln_b, dw_pool.reshape(-1, POOL_GROUP), ds_pool, dg_ffn, dg_final, loss_p]
    r_out, r_dw, r_in, summed = _grad_mix(y, dh1b, dz, xn, _shard_cols(dw_dw[0:CONV_WIDTH]), partial)

    big = {}
    recv = dict(zip(_BIG, [r_in, r_out, r_gate, r_up, r_down]))
    for call, group in (("adamw_ffn", ("w_gate", "w_up", "w_down")), ("adamw_mix", ("w_in", "w_out"))):
        results = _adamw_shards([recv[nm] for nm in group], [_rows(nm, weights[nm]) for nm in group],
                                [_rows(nm, moms[nm]) for nm in group], [_rows(nm, vars_[nm]) for nm in group], 2, call)
        for nm, res in zip(group, results):
            big[nm] = [o.T if nm in _COL_SHARDED else o for o in res]
    big["w_dw"], = _adamw_shards([r_dw], [w_dw[0]], [moms["w_dw"][0]], [vars_["w_dw"][0]], 1, "adamw_w_dw")

    sm = _adamw_small(summed[:-1], [small_shape(weights[nm]) for nm in small_names],
                      [small_shape(moms[nm]) for nm in small_names], [small_shape(vars_[nm]) for nm in small_names])
    n_small = len(small_names)

    def result(kind, nm):
        if nm in big:
            return big[nm][kind].reshape(weights[nm].shape)
        return sm[kind * n_small + small_names.index(nm)].reshape(weights[nm].shape)

    loss = summed[-1][0, 0]
    out = [loss, dx.reshape(x.shape)]
    for kind in range(4):
        out += [result(kind, nm) for nm in names]
    return tuple(out)


def kernel(x, g_mix, w_in, b_in, w_dw, b_dw, ln_g, ln_b, w_pool, s_pool, w_out, g_ffn, w_gate, w_up, w_down, g_final, loss_target, m_g_mix, m_w_in, m_b_in, m_w_dw, m_b_dw, m_ln_g, m_ln_b, m_w_pool, m_s_pool, m_w_out, m_g_ffn, m_w_gate, m_w_up, m_w_down, m_g_final, v_g_mix, v_w_in, v_b_in, v_w_dw, v_b_dw, v_ln_g, v_ln_b, v_w_pool, v_s_pool, v_w_out, v_g_ffn, v_w_gate, v_w_up, v_w_down, v_g_final):
    args = (x, g_mix, w_in, b_in, w_dw, b_dw, ln_g, ln_b, w_pool, s_pool, w_out, g_ffn, w_gate, w_up, w_down, g_final, loss_target, m_g_mix, m_w_in, m_b_in, m_w_dw, m_b_dw, m_ln_g, m_ln_b, m_w_pool, m_s_pool, m_w_out, m_g_ffn, m_w_gate, m_w_up, m_w_down, m_g_final, v_g_mix, v_w_in, v_b_in, v_w_dw, v_b_dw, v_ln_g, v_ln_b, v_w_pool, v_s_pool, v_w_out, v_g_ffn, v_w_gate, v_w_up, v_w_down, v_g_final)
    return _step(args, ts_mix_fwd=512, ts_ffn=256, ts_mix_bwd=512, tm_grad=256)
```

```python
import jax
import jax.numpy as jnp
from jax import lax
from jax.experimental import pallas as pl
from jax.experimental.pallas import tpu as pltpu

F32 = jnp.float32
BF16 = jnp.bfloat16
MESH = pl.DeviceIdType.MESH
N_DEV = 8

C_CONV = 512
CONV_WIDTH = 31
POOL_WINDOWS = (2, 4, 8, 16)
POOL_GROUP = 128
RMS_EPS = 1e-6
LN_EPS = 1e-5

ADAM_LR = 0.001
ADAM_B1 = 0.9
ADAM_B2 = 0.999
ADAM_EPS = 1e-08
ADAM_WD = 0.01
ADAM_STEP = 10

HALO = 32
SUBLANES = 8
LANES = 128
CONV_ROWS = 64
VMEM_LIMIT = 56 * 1024 * 1024


def _dot(a, b):
    return jnp.dot(a, b, preferred_element_type=F32)


def _dot_nt(a, b):
    return lax.dot_general(a, b, (((1,), (1,)), ((), ())), preferred_element_type=F32)


def _dot_tn(a, b):
    return lax.dot_general(a, b, (((0,), (0,)), ((), ())), preferred_element_type=F32)


def _mean_last(v):
    return jnp.mean(v, axis=-1, keepdims=True)


def _full(shape):
    nd = len(shape)
    return pl.BlockSpec(shape, lambda *_: (0,) * nd)


def _full1(shape):
    nd = len(shape)
    return pl.BlockSpec(shape, lambda *_: (0,) * nd, pipeline_mode=pl.Buffered(1))


def _shifted_copies(sh_ref, rows):
    for s in range(1, SUBLANES):
        sh_ref[s, 0:rows, :] = sh_ref[0, s:s + rows, :]


def _tap(sh_ref, off, r0, rows):
    q, s = divmod(off, SUBLANES)
    return sh_ref[s, pl.ds(r0 + q * SUBLANES, rows), :]


def _window_sums(src, bufs, rows, backward):
    assert all(w == 2 << g for g, w in enumerate(POOL_WINDOWS))
    n = len(POOL_WINDOWS)
    out = []
    for level in range(n):
        dst, shift = bufs[level % 2], 1 << level
        lanes = slice(level * POOL_GROUP, n * POOL_GROUP)
        lo, hi = (SUBLANES * (level + 1), rows) if backward else (0, rows - SUBLANES * (level + 1))
        other = slice(lo - shift, hi - shift) if backward else slice(lo + shift, hi + shift)
        dst[lo:hi, lanes] = src[lo:hi, lanes] + src[other, lanes]
        src = dst
        out.append(dst)
    return out


def _mix_fwd(x, g_mix, b_in, b_dw, ln_g, ln_b, w_pool, s_pool, mix_shards, ag_shards, ts):
    S, D = x.shape
    d_in = mix_shards[0].shape[0] * N_DEV
    C = C_CONV
    nt = S // ts
    nrb = ts // CONV_ROWS
    n_ag = len(ag_shards)
    relay_step = nt // 2
    mix_dtypes = [BF16, BF16, F32]

    def body(x_ref, g_ref, bin_ref, bdw_ref, lng_ref, lnb_ref, wp_ref, sp_ref, *rest):
        mx_in, ag_in, rest = rest[:3], rest[3:3 + n_ag], rest[3 + n_ag:]
        a_ref, gate_ref, v_ref, m_ref, y_ref, h1_ref, xn_ref, wdw_ref = rest[:8]
        mx_out, ag_out, rest = rest[8:11], rest[11:11 + n_ag], rest[11 + n_ag:]
        ush, pbuf, pa, pb, win_ref, wout_ref, gdw, load_sems = rest[:8]
        rest = rest[8:]
        (ma_stage, ma_bufs, ma_sems), rest = (rest[:2], rest[2:4], rest[4:7]), rest[7:]
        (mb_stage, mb_bufs, mb_sems), rest = (rest[:1], rest[1:2], rest[2:5]), rest[5:]
        ag_stage, ag_bufs, ag_sems = rest[:n_ag], rest[n_ag:2 * n_ag], rest[2 * n_ag:]
        i = pl.program_id(0)
        ma_start, ma_relay, ma_pass_on, ma_finish = _gather_plan(
            mx_in[0::2], mx_out[0::2], ma_stage, ma_bufs, *ma_sems)
        mb_start, mb_relay, mb_pass_on, mb_finish = _gather_plan(
            mx_in[1:2], mx_out[1:2], mb_stage, mb_bufs, *mb_sems)
        ag_start, ag_relay, ag_pass_on, ag_finish = _gather_plan(ag_in, ag_out, ag_stage, ag_bufs, *ag_sems)

        def load_weight(k, dst_ref):
            rows = mx_out[k].shape[1]
            loads = [pltpu.make_async_copy(mx_out[k].at[j], dst_ref.at[pl.ds(j * rows, rows), :],
                                           load_sems.at[N_DEV * k + j]) for j in range(N_DEV)]
            for cp in loads:
                cp.start()
            for cp in loads:
                cp.wait()

        @pl.when(i == 0)
        def _():
            ma_start()
            mb_start()
            ma_relay()
            mb_relay()
            ag_start()
            ma_pass_on()
            ma_finish()
            load_weight(0, win_ref)
            dw_load = pltpu.make_async_copy(mx_out[2], gdw, load_sems.at[2 * N_DEV])
            dw_load.start()
            dw_load.wait()
            first_half = lax.broadcasted_iota(jnp.int32, gdw.shape[1:], 1) < C // N_DEV
            for p in range(N_DEV // 2):
                wdw_ref[:, LANES * p:LANES * (p + 1)] = jnp.where(
                    first_half, gdw[2 * p], pltpu.roll(gdw[2 * p + 1], C // N_DEV, axis=1))
            ush[0, 0:HALO, :] = jnp.zeros((HALO, C), F32)
            pbuf[0:HALO, :] = jnp.zeros((HALO, C), F32)

        @pl.when(i == relay_step)
        def _():
            ag_relay()

        @pl.when(i == nt - 1)
        def _():
            ag_pass_on()

        x = x_ref[...]
        r1 = lax.rsqrt(_mean_last(x * x) + RMS_EPS)
        xn = (x * r1 * g_ref[...]).astype(BF16)
        xn_ref[...] = xn
        z = _dot_nt(xn, win_ref[...]) + bin_ref[...]
        a = z[:, 0:C]
        gate = z[:, C:2 * C]
        a_ref[...] = a
        gate_ref[...] = gate
        ush[0, HALO:HALO + ts, :] = a * jax.nn.sigmoid(gate)
        pbuf[HALO:HALO + ts, :] = z[:, 2 * C:]

        _shifted_copies(ush, ts + HALO - SUBLANES)

        def conv_block(rb, carry):
            r0 = pl.multiple_of(rb * CONV_ROWS, CONV_ROWS)
            acc = jnp.zeros((CONV_ROWS, C), F32)
            for k in range(CONV_WIDTH):
                acc = acc + wdw_ref[k:k + 1, :] * _tap(ush, HALO - (CONV_WIDTH - 1) + k, r0, CONV_ROWS)
            v_ref[pl.ds(r0, CONV_ROWS), :] = acc + bdw_ref[...]
            return carry

        lax.fori_loop(0, nrb, conv_block, 0)

        v = v_ref[...]
        mu = _mean_last(v)
        xc = v - mu
        rstd = lax.rsqrt(_mean_last(xc * xc) + LN_EPS)
        ln = xc * rstd * lng_ref[...] + lnb_ref[...]
        y_ref[:, 0:C] = (ln * jax.nn.sigmoid(ln)).astype(BF16)

        sums = _window_sums(pbuf, (pa, pb), ts + HALO, backward=True)
        row = lax.broadcasted_iota(jnp.int32, (ts, 1), 0) + i * ts
        for gi, w in enumerate(POOL_WINDOWS):
            lanes = slice(gi * POOL_GROUP, (gi + 1) * POOL_GROUP)
            seg = pbuf[HALO:HALO + ts, lanes]
            ws = sums[gi][HALO:HALO + ts, lanes]
            cnt = jnp.minimum(row + 1, w).astype(F32)
            m = (ws / cnt - seg).astype(BF16)
            m_ref[:, lanes] = m
            ypre = _dot(m, wp_ref[gi])
            y_ref[:, C + gi * POOL_GROUP:C + (gi + 1) * POOL_GROUP] = (ypre * sp_ref[:, lanes]).astype(BF16)

        @pl.when(i == 0)
        def _():
            mb_pass_on()
            mb_finish()
            load_weight(1, wout_ref)

        h1_ref[...] = x + _dot(y_ref[...], wout_ref[...])

        ush[0, 0:HALO, :] = ush[0, ts:ts + HALO, :]
        pbuf[0:HALO, :] = pbuf[ts:ts + HALO, :]

        @pl.when(i == nt - 1)
        def _():
            ag_finish()

    tile = lambda w, dt: (pl.BlockSpec((ts, w), lambda i: (i, 0)), jax.ShapeDtypeStruct((S, w), dt))
    wdw_rows = mix_shards[2].shape[0]
    outs = [tile(C, F32), tile(C, F32), tile(C, F32), tile(C, BF16), tile(D, BF16), tile(D, F32), tile(D, BF16),
            (_full((wdw_rows, C)), jax.ShapeDtypeStruct((wdw_rows, C), F32))]
    return pl.pallas_call(
        body, name="mix_fwd", grid=(nt,),
        in_specs=[pl.BlockSpec((ts, D), lambda i: (i, 0)), _full((1, D)), _full((1, d_in)), _full((1, C)),
                  _full((1, C)), _full((1, C)), _full(w_pool.shape), _full((1, C))] + [_HBM] * (3 + n_ag),
        out_specs=[o[0] for o in outs] + [_HBM] * (3 + n_ag),
        out_shape=[o[1] for o in outs] + _gather_out_shapes(mix_shards, mix_dtypes)
        + _gather_out_shapes(ag_shards, [BF16] * n_ag),
        scratch_shapes=[pltpu.VMEM((SUBLANES, ts + HALO, C), F32)] + [pltpu.VMEM((ts + HALO, C), F32)] * 3
        + [pltpu.VMEM((d_in, D), BF16), pltpu.VMEM((D, D), BF16),
                        pltpu.VMEM((N_DEV,) + mix_shards[2].shape, F32), pltpu.SemaphoreType.DMA((2 * N_DEV + 1,))]
        + _gather_scratch(mix_shards[0::2], mix_dtypes[0::2]) + _gather_scratch(mix_shards[1:2], mix_dtypes[1:2])
        + _gather_scratch(ag_shards, [BF16] * n_ag),
        compiler_params=pltpu.CompilerParams(dimension_semantics=("arbitrary",), vmem_limit_bytes=VMEM_LIMIT),
    )(x, g_mix, b_in, b_dw, ln_g, ln_b, w_pool, s_pool, *mix_shards, *ag_shards)


def _ffn(h1, target, g_ffn, g_final, w_gate, w_up, w_down, ts, f_chunks):
    S, D = h1.shape
    Fd = w_gate.shape[0]
    nt = S // ts
    bounds = []
    lo = 0
    for n in f_chunks:
        bounds.append((lo, lo + n))
        lo += n
    assert lo == Fd

    def body(h1_ref, tgt_ref, gf_ref, gl_ref, wg_ref, wu_ref, wd_ref,
             dh1_ref, hn_ref, act_ref, dgt_ref, dup_ref, dh2_ref, loss_ref, dgl_ref, dgf_ref, gt_s, up_s):
        i = pl.program_id(0)

        @pl.when(i == 0)
        def _():
            loss_ref[...] = jnp.zeros_like(loss_ref)
            dgl_ref[...] = jnp.zeros_like(dgl_ref)
            dgf_ref[...] = jnp.zeros_like(dgf_ref)

        h1 = h1_ref[...]
        r2 = lax.rsqrt(_mean_last(h1 * h1) + RMS_EPS)
        hhat = h1 * r2
        hn = (hhat * gf_ref[...]).astype(BF16)
        hn_ref[...] = hn
        h2 = h1
        for lo, hi in bounds:
            gt = _dot_nt(hn, wg_ref[lo:hi, :])
            up = _dot_nt(hn, wu_ref[lo:hi, :])
            gt_s[:, lo:hi] = gt
            up_s[:, lo:hi] = up
            act = (gt * jax.nn.sigmoid(gt) * up).astype(BF16)
            act_ref[:, lo:hi] = act
            h2 = h2 + _dot(act, wd_ref[lo:hi, :])

        r3 = lax.rsqrt(_mean_last(h2 * h2) + RMS_EPS)
        n3 = h2 * r3
        gl = gl_ref[...]
        diff = n3 * gl - tgt_ref[...]
        loss_ref[...] += jnp.sum(0.5 * jnp.sum(diff * diff, axis=-1, keepdims=True) / D, axis=0, keepdims=True)
        dout = diff / D
        dgl_ref[...] += jnp.sum(dout * n3, axis=0, keepdims=True)
        dn = dout * gl
        dh2 = r3 * (dn - n3 * _mean_last(dn * n3))
        dh2b = dh2.astype(BF16)
        dh2_ref[...] = dh2b

        dhn = jnp.zeros((ts, D), F32)
        for lo, hi in bounds:
            gt = gt_s[:, lo:hi]
            up = up_s[:, lo:hi]
            sg = jax.nn.sigmoid(gt)
            dact = _dot_nt(dh2b, wd_ref[lo:hi, :])
            dgt = (dact * up * (sg * (1.0 + gt * (1.0 - sg)))).astype(BF16)
            dup = (dact * (gt * sg)).astype(BF16)
            dgt_ref[:, lo:hi] = dgt
            dup_ref[:, lo:hi] = dup
            dhn = dhn + _dot(dgt, wg_ref[lo:hi, :]) + _dot(dup, wu_ref[lo:hi, :])

        dgf_ref[...] += jnp.sum(dhn * hhat, axis=0, keepdims=True)
        dnn = dhn * gf_ref[...]
        dh1_ref[...] = dh2 + r2 * (dnn - hhat * _mean_last(dnn * hhat))

    tile = lambda w, dt: (pl.BlockSpec((ts, w), lambda i: (i, 0)), jax.ShapeDtypeStruct((S, w), dt))
    acc = lambda w: (_full((1, w)), jax.ShapeDtypeStruct((1, w), F32))
    outs = [tile(D, F32), tile(D, BF16), tile(Fd, BF16), tile(Fd, BF16), tile(Fd, BF16), tile(D, BF16),
            acc(LANES), acc(D), acc(D)]
    return pl.pallas_call(
        body, name="ffn_fwd_bwd", grid=(nt,),
        in_specs=[pl.BlockSpec((ts, D), lambda i: (i, 0)), pl.BlockSpec((ts, D), lambda i: (i, 0)),
                  _full((1, D)), _full((1, D)), _full1((Fd, D)), _full1((Fd, D)), _full1((Fd, D))],
        out_specs=[o[0] for o in outs], out_shape=[o[1] for o in outs],
        scratch_shapes=[pltpu.VMEM((ts, Fd), F32), pltpu.VMEM((ts, Fd), F32)],
        compiler_params=pltpu.CompilerParams(dimension_semantics=("arbitrary",), vmem_limit_bytes=VMEM_LIMIT),
    )(h1, target, g_ffn, g_final, w_gate, w_up, w_down)


def _mix_bwd(dh1, x, a, gate, v, m, g_mix, w_in, w_dw, ln_g, ln_b, w_pool, s_pool, w_out, rs_parts, ts):
    S, D = x.shape
    n_rs = len(rs_parts)
    d_in = w_in.shape[0]
    C = C_CONV
    nt = S // ts
    nrb = ts // CONV_ROWS
    wrows =((CONV_WIDTH + SUBLANES - 1) // SUBLANES) * SUBLANES

    def body(dh1_ref, x_ref, a_ref, gate_ref, v_ref, m_ref, g_ref, win_ref, wdw_ref, lng_ref,
             lnb_ref, wp_ref, sp_ref, wout_ref, *rest):
        rs_in, rest = rest[:n_rs], rest[n_rs:]
        (dx_ref, dz_ref, dh1b_ref, dgm_ref, dbin_ref, dwdw_ref, dbdw_ref, dlng_ref, dlnb_ref, dwp_ref,
         dsp_ref) = rest[:11]
        rs_out, rest = rest[11:11 + n_rs], rest[11 + n_rs:]
        dvsh, dqbuf, qa, qb, du_s, dm_s = rest[:6]
        rs_bufs, (send_sems, recv_sems, local_sems) = rest[6:6 + n_rs], rest[6 + n_rs:]
        i = pl.program_id(0)
        t = nt - 1 - i
        rs_start, rs_finish = _chip_exchange_plan(rs_in, rs_out, rs_bufs, send_sems, recv_sems, local_sems)

        @pl.when(i == 0)
        def _():
            rs_start()
            dvsh[0, ts:ts + HALO, :] = jnp.zeros((HALO, C), F32)
            dqbuf[ts:ts + HALO, :] = jnp.zeros((HALO, C), F32)
            for r in (dgm_ref, dbin_ref, dwdw_ref, dbdw_ref, dlng_ref, dlnb_ref, dwp_ref, dsp_ref):
                r[...] = jnp.zeros_like(r)

        dh1 = dh1_ref[...]
        dh1b = dh1.astype(BF16)
        dh1b_ref[...] = dh1b
        dy = _dot_nt(dh1b, wout_ref[...])

        v = v_ref[...]
        mu = _mean_last(v)
        xc = v - mu
        rstd = lax.rsqrt(_mean_last(xc * xc) + LN_EPS)
        vhat = xc * rstd
        lng = lng_ref[...]
        ln = vhat * lng + lnb_ref[...]
        sg = jax.nn.sigmoid(ln)
        dln = dy[:, 0:C] * (sg * (1.0 + ln * (1.0 - sg)))
        dlng_ref[...] += jnp.sum(dln * vhat, axis=0, keepdims=True)
        dlnb_ref[...] += jnp.sum(dln, axis=0, keepdims=True)
        dvh = dln * lng
        dv = rstd * (dvh - _mean_last(dvh) - vhat * _mean_last(dvh * vhat))
        dbdw_ref[...] += jnp.sum(dv, axis=0, keepdims=True)
        dvsh[0, 0:ts, :] = dv
        _shifted_copies(dvsh, ts + HALO - SUBLANES)

        def conv_block(rb, carry):
            r0 = pl.multiple_of(rb * CONV_ROWS, CONV_ROWS)
            acc = jnp.zeros((CONV_ROWS, C), F32)
            for k in range(CONV_WIDTH):
                acc = acc + wdw_ref[k:k + 1, :] * _tap(dvsh, CONV_WIDTH - 1 - k, r0, CONV_ROWS)
            du_s[pl.ds(r0, CONV_ROWS), :] = acc
            return carry

        lax.fori_loop(0, nrb, conv_block, 0)

        a = a_ref[...]
        sgate = jax.nn.sigmoid(gate_ref[...])
        u = a * sgate
        for k in range(CONV_WIDTH):
            q, s = divmod(CONV_WIDTH - 1 - k, SUBLANES)
            prod = u * dvsh[s, q * SUBLANES:q * SUBLANES + ts, :]
            dwdw_ref[k:k + 1, :] += jnp.sum(prod, axis=0, keepdims=True)

        du = du_s[...]
        da = du * sgate
        dgate = du * a * sgate * (1.0 - sgate)
        dz_ref[:, 0:C] = da.astype(BF16)
        dz_ref[:, C:2 * C] = dgate.astype(BF16)
        dbin_ref[:, 0:C] += jnp.sum(da, axis=0, keepdims=True)
        dbin_ref[:, C:2 * C] += jnp.sum(dgate, axis=0, keepdims=True)

        row = lax.broadcasted_iota(jnp.int32, (ts, 1), 0) + t * ts
        for gi, w in enumerate(POOL_WINDOWS):
            lanes = slice(gi * POOL_GROUP, (gi + 1) * POOL_GROUP)
            dyp = dy[:, C + gi * POOL_GROUP:C + (gi + 1) * POOL_GROUP]
            mg = m_ref[:, lanes]
            ypre = _dot(mg, wp_ref[gi])
            dsp_ref[:, lanes] += jnp.sum(dyp * ypre, axis=0, keepdims=True)
            dyi = (dyp * sp_ref[:, lanes]).astype(BF16)
            dwp_ref[gi] += _dot_tn(mg, dyi)
            dm = _dot_nt(dyi, wp_ref[gi])
            cnt = jnp.minimum(row + 1, w).astype(F32)
            dqbuf[0:ts, lanes] = dm / cnt
            dm_s[:, lanes] = dm
        sums = _window_sums(dqbuf, (qa, qb), ts + HALO, backward=False)
        for gi in range(len(POOL_WINDOWS)):
            lanes = slice(gi * POOL_GROUP, (gi + 1) * POOL_GROUP)
            dp = sums[gi][0:ts, lanes] - dm_s[:, lanes]
            dz_ref[:, 2 * C + gi * POOL_GROUP:2 * C + (gi + 1) * POOL_GROUP] = dp.astype(BF16)
            dbin_ref[:, 2 * C + gi * POOL_GROUP:2 * C + (gi + 1) * POOL_GROUP] += jnp.sum(dp, axis=0, keepdims=True)

        dxn = _dot(dz_ref[...], win_ref[...])
        x = x_ref[...]
        r1 = lax.rsqrt(_mean_last(x * x) + RMS_EPS)
        xhat = x * r1
        dgm_ref[...] += jnp.sum(dxn * xhat, axis=0, keepdims=True)
        dnn = dxn * g_ref[...]
        dx_ref[...] = dh1 + r1 * (dnn - xhat * _mean_last(dnn * xhat))

        dvsh[0, ts:ts + HALO, :] = dvsh[0, 0:HALO, :]
        dqbuf[ts:ts + HALO, :] = dqbuf[0:HALO, :]

        @pl.when(i == nt - 1)
        def _():
            rs_finish()

    rev = lambda w: pl.BlockSpec((ts, w), lambda i: (nt - 1 - i, 0))
    acc = lambda shape: (_full(shape), jax.ShapeDtypeStruct(shape, F32))
    outs = [(rev(D), jax.ShapeDtypeStruct((S, D), F32)), (rev(d_in), jax.ShapeDtypeStruct((S, d_in), BF16)),
            (rev(D), jax.ShapeDtypeStruct((S, D), BF16)), acc((1, D)), acc((1, d_in)), acc((wrows, C)),
            acc((1, C)), acc((1, C)), acc((1, C)), acc(w_pool.shape), acc((1, C))]
    return pl.pallas_call(
        body, name="mix_bwd", grid=(nt,),
        in_specs=[rev(D), rev(D), rev(C), rev(C), rev(C), rev(C), _full((1, D)), _full((d_in, D)),
                  _full(w_dw.shape), _full((1, C)), _full((1, C)), _full(w_pool.shape), _full((1, C)),
                  _full((D, D))] + [_HBM] * n_rs,
        out_specs=[o[0] for o in outs] + [_HBM] * n_rs,
        out_shape=[o[1] for o in outs] + [jax.ShapeDtypeStruct(p.shape, p.dtype) for p in rs_parts],
        scratch_shapes=[pltpu.VMEM((SUBLANES, ts + HALO, C), F32)] + [pltpu.VMEM((ts + HALO, C), F32)] * 3
        + [pltpu.VMEM((ts, C), F32)] * 2
        + [pltpu.VMEM(p.shape[1:], p.dtype) for p in rs_parts] + _comm_sems(n_rs, 3),
        compiler_params=pltpu.CompilerParams(dimension_semantics=("arbitrary",), vmem_limit_bytes=VMEM_LIMIT),
    )(dh1, x, a, gate, v, m, g_mix, w_in, w_dw, ln_g, ln_b, w_pool, s_pool, w_out, *rs_parts)


def _grad_mix(y, dh1b, dz, xn, dw_dw, vals):
    S, D = y.shape
    d_in = dz.shape[1]
    r_out, r_in = D // N_DEV, d_in // N_DEV
    n_out, n_in = N_DEV // 2, N_DEV // 2
    steps = n_out + n_in
    ns = len(vals)
    dw_out = jax.ShapeDtypeStruct((N_DEV, r_out, D), BF16)
    dw_in = jax.ShapeDtypeStruct((N_DEV, r_in, D), BF16)
    s_scratch, o_scratch = _reduce_scratch([], vals), _reduce_scratch([dw_out, dw_dw], [])
    i_scratch = _reduce_scratch([dw_in], [])

    def body(y_ref, dh_ref, dz_ref, xn_ref, dwdw_ref, *rest):
        v_in, p_out, v_out = rest[:ns], rest[ns:ns + 3], rest[ns + 3:2 * ns + 3]
        rest = rest[2 * ns + 3:]
        dwo_v, dwi_v, rest = rest[0], rest[1], rest[2:]
        s_refs, rest = rest[:len(s_scratch)], rest[len(s_scratch):]
        o_refs, i_refs = rest[:len(o_scratch)], rest[len(o_scratch):]
        s_start, s_middle, s_finish = _reduce_plan((), v_in, (), v_out, s_refs)
        o_start, o_middle, o_finish = _reduce_plan((dwo_v, dwdw_ref), (), p_out[:2], (), o_refs)
        i_start, i_middle, i_finish = _reduce_plan((dwi_v,), (), p_out[2:], (), i_refs)
        s = pl.program_id(0)
        pl.when(s == 0)(s_start)
        pl.when(s == 1)(s_middle)

        @pl.when(s < n_out)
        def _():
            res = _dot_tn(y_ref[...], dh_ref[...]).astype(BF16)
            dwo_v[2 * s] = res[0:r_out]
            dwo_v[2 * s + 1] = res[r_out:]

        pl.when(s == n_out - 1)(s_finish)
        pl.when(s == n_out)(o_start)
        pl.when(s == n_out + 1)(o_middle)

        @pl.when(s >= n_out)
        def _():
            res = _dot_tn(dz_ref[...], xn_ref[...]).astype(BF16)
            dwi_v[2 * (s - n_out)] = res[0:r_in]
            dwi_v[2 * (s - n_out) + 1] = res[r_in:]

        @pl.when(s == steps - 1)
        def _():
            o_finish()
            i_start()
            i_middle()
            i_finish()

    late = lambda s: jnp.clip(s - n_out, 0, n_in - 1)
    res = pl.pallas_call(
        body, name="grad_w_mix", grid=(steps,),
        in_specs=[pl.BlockSpec((S, 2 * r_out), lambda s: (0, jnp.minimum(s, n_out - 1))), _full1((S, D)),
                  pl.BlockSpec((S, 2 * r_in), lambda s: (0, late(s))), _full1((S, D))] + [_VMEM] * (1 + ns),
        out_specs=[_VMEM] * (3 + ns),
        out_shape=_reduce_out_shapes([dw_out, dw_dw, dw_in], []) + _reduce_out_shapes([], vals),
        scratch_shapes=[pltpu.VMEM(dw_out.shape, BF16), pltpu.VMEM(dw_in.shape, BF16)] + s_scratch + o_scratch
        + i_scratch,
        compiler_params=pltpu.CompilerParams(dimension_semantics=("arbitrary",), vmem_limit_bytes=VMEM_LIMIT),
    )(y, dh1b, dz, xn, dw_dw, *vals)
    return res[0], res[1], res[2], res[3:]


def _grad_ffn(a_list, b_list, tm):
    n = len(a_list)
    S, M = a_list[0].shape
    N = b_list[0].shape[1]
    nb = M // tm
    steps = n * nb
    R = M // N_DEV
    b_unique = [b for k, b in enumerate(b_list) if all(b is not o for o in b_list[:k])]
    b_index = [[b is o for o in b_unique].index(True) for b in b_list]

    def body(*refs):
        a_refs, b_refs = refs[:n], refs[n:n + len(b_unique)]
        rest = refs[n + len(b_unique):]
        g_refs, sib_refs, (obuf, w_sems, send_sems, recv_sems) = rest[:n], rest[n:2 * n], rest[2 * n:]
        s = pl.program_id(0)
        slot = s % 2
        x, y, c = _position()

        def write_back(dst):
            return pltpu.make_async_copy(obuf.at[slot], dst, w_sems.at[slot])

        def exchange(k):
            return [pltpu.make_async_remote_copy(
                src_ref=g_refs[k].at[pl.ds(pl.multiple_of((2 * q + 1 - c) * R, SUBLANES * 2), R), :],
                dst_ref=sib_refs[k].at[q], send_sem=send_sems.at[k, q], recv_sem=recv_sems.at[k, q],
                device_id=(x, y, 1 - c), device_id_type=MESH) for q in range(N_CHIPS)]

        pl.when(s >= 2)(write_back(g_refs[0].at[pl.ds(0, tm), :]).wait)
        for k in range(n):
            @pl.when(jnp.logical_and(s >= k * nb, s < (k + 1) * nb))
            def _(k=k):
                obuf[slot] = _dot_tn(a_refs[k][...], b_refs[b_index[k]][...]).astype(BF16)
                row = pl.multiple_of((s - k * nb) * tm, tm)
                write_back(g_refs[k].at[pl.ds(row, tm), :]).start()
            if k < n - 1:
                @pl.when(s == (k + 1) * nb + 2)
                def _(k=k):
                    for cp in exchange(k):
                        cp.start()

        @pl.when(s == steps - 1)
        def _():
            write_back(g_refs[0].at[pl.ds(0, tm), :]).wait()
            pltpu.make_async_copy(obuf.at[1 - slot], g_refs[0].at[pl.ds(0, tm), :], w_sems.at[1 - slot]).wait()
            for cp in exchange(n - 1):
                cp.start()
            for k in range(n):
                for cp in exchange(k):
                    cp.wait()

    clamp = lambda k: (lambda s: (0, jnp.clip(s - k * nb, 0, nb - 1)))
    sems = pltpu.SemaphoreType.DMA((n, N_CHIPS))
    res = pl.pallas_call(
        body, name="grad_w_ffn", grid=(steps,),
        in_specs=[pl.BlockSpec((S, tm), clamp(k)) for k in range(n)] + [_full1((S, N))] * len(b_unique),
        out_specs=[_HBM] * (2 * n),
        out_shape=[jax.ShapeDtypeStruct((M, N), BF16)] * n + [jax.ShapeDtypeStruct((N_CHIPS, R, N), BF16)] * n,
        scratch_shapes=[pltpu.VMEM((2, tm, N), BF16), pltpu.SemaphoreType.DMA((2,)), sems, sems],
        compiler_params=pltpu.CompilerParams(dimension_semantics=("arbitrary",), vmem_limit_bytes=VMEM_LIMIT),
    )(*a_list, *b_unique)
    return res[:n], res[n:]


def _position():
    return lax.axis_index("x"), lax.axis_index("y"), lax.axis_index("c")


def _slot(px, py, pc):
    return 4 * px + 2 * py + pc


_HBM = pl.BlockSpec(memory_space=pl.ANY)


def _comm_sems(n, copies):
    return [pltpu.SemaphoreType.DMA((n, copies)), pltpu.SemaphoreType.DMA((n, copies)),
            pltpu.SemaphoreType.DMA((n, 2))]


def _local_copy(srcs, dsts, bufs, local_sems):
    n = len(srcs)
    loads = [pltpu.make_async_copy(srcs[k], bufs[k], local_sems.at[k, 0]) for k in range(n)]
    for cp in loads:
        cp.start()
    for cp in loads:
        cp.wait()
    stores = _local_stores(dsts, bufs, local_sems)
    for cp in stores:
        cp.start()
    return stores


def _local_stores(dsts, bufs, local_sems):
    return [pltpu.make_async_copy(bufs[k], dsts[k], local_sems.at[k, 1]) for k in range(len(dsts))]


def _gather_out_shapes(shards, dtypes):
    return [jax.ShapeDtypeStruct((N_DEV,) + s.shape, dt) for s, dt in zip(shards, dtypes)]


def _gather_scratch(shards, dtypes):
    return ([pltpu.VMEM(s.shape, s.dtype) for s in shards] + [pltpu.VMEM(s.shape, dt) for s, dt in zip(shards, dtypes)]
            + _comm_sems(len(shards), 7))


def _gather_plan(ins, outs, stage, bufs, send_sems, recv_sems, local_sems):
    n = len(ins)
    x, y, c = _position()
    me, sibling = (x, y, c), (x, y, 1 - c)
    na, nb, dg = (x ^ (1 - c), y ^ c), (x ^ c, y ^ (1 - c)), (1 - x, 1 - y)
    own = [outs[k].at[_slot(*me)] for k in range(n)]

    def copy(k, sem, block, to, src=None):
        dst = outs[k].at[_slot(*block)]
        return pltpu.make_async_remote_copy(
            src_ref=dst if src is None else src, dst_ref=dst, send_sem=send_sems.at[k, sem],
            recv_sem=recv_sems.at[k, sem], device_id=to, device_id_type=MESH)

    def first():
        cps = []
        for k in range(n):
            cps += [copy(k, 0, me, sibling, src=bufs[k]), copy(k, 1, me, (*na, c), src=bufs[k]),
                    copy(k, 2, me, (*nb, c), src=bufs[k])]
        return cps

    def onward():
        return [copy(k, 3, (*na, c), (*nb, c)) for k in range(n)]

    def to_sibling(j, chip):
        return [copy(k, 4 + j, (*chip, c), sibling) for k in range(n)]

    def start():
        loads = [pltpu.make_async_copy(ins[k], stage[k], local_sems.at[k, 0]) for k in range(n)]
        for cp in loads:
            cp.start()
        for cp in loads:
            cp.wait()
        for k in range(n):
            bufs[k][...] = stage[k][...].astype(bufs[k].dtype)
        for cp in first() + _local_stores(own, bufs, local_sems):
            cp.start()

    def relay():
        for k in range(n):
            copy(k, 1, (*na, c), me).wait_recv()
        for cp in onward() + to_sibling(0, na):
            cp.start()
        for k in range(n):
            copy(k, 2, (*nb, c), me).wait_recv()
        for cp in to_sibling(1, nb):
            cp.start()

    def pass_on():
        for k in range(n):
            copy(k, 3, (*dg, c), me).wait_recv()
        for cp in to_sibling(2, dg):
            cp.start()

    def finish():
        for k in range(n):
            copy(k, 0, sibling, me).wait_recv()
            for j, chip in enumerate((nb, na, dg)):
                copy(k, 4 + j, (*chip, 1 - c), me).wait_recv()
        for cp in first() + onward() + to_sibling(0, na) + to_sibling(1, nb) + to_sibling(2, dg):
            cp.wait_send()
        for cp in _local_stores(own, bufs, local_sems):
            cp.wait()

    return start, relay, pass_on, finish


N_CHIPS = 4
_CHIP_FLIPS = [(1, 0), (0, 1), (1, 1)]
F32_TRAVEL_LIMIT = 4096


def _chip_exchange_plan(ins, outs, bufs, send_sems, recv_sems, local_sems):
    n = len(ins)
    x, y, c = _position()
    my_q = 2 * x + y
    peers = [(x ^ fx, y ^ fy) for fx, fy in _CHIP_FLIPS]
    own = [outs[k].at[my_q] for k in range(n)]

    def sends():
        return [pltpu.make_async_remote_copy(
            src_ref=ins[k].at[2 * px + py], dst_ref=outs[k].at[my_q], send_sem=send_sems.at[k, f],
            recv_sem=recv_sems.at[k, f], device_id=(px, py, c), device_id_type=MESH)
            for f, (px, py) in enumerate(peers) for k in range(n)]

    def start():
        for cp in sends():
            cp.start()
        _local_copy([ins[k].at[my_q] for k in range(n)], own, bufs, local_sems)

    def finish():
        for f, (px, py) in enumerate(peers):
            for k in range(n):
                pltpu.make_async_remote_copy(
                    src_ref=ins[k].at[my_q], dst_ref=outs[k].at[2 * px + py], send_sem=send_sems.at[k, f],
                    recv_sem=recv_sems.at[k, f], device_id=(px, py, c), device_id_type=MESH).wait_recv()
        for cp in sends():
            cp.wait_send()
        for cp in _local_stores(own, bufs, local_sems):
            cp.wait()

    return start, finish


def _pair_sum(parts, sib):
    n = len(parts)

    def body(*refs):
        c = lax.axis_index("c")
        for k in range(n):
            refs[2 * n + k][0] = (refs[k][0, c].astype(F32) + refs[n + k][0].astype(F32)).astype(BF16)

    pair = [pl.BlockSpec((1, 2) + p.shape[1:], lambda q: (q, 0, 0, 0)) for p in parts]
    one = [pl.BlockSpec((1,) + p.shape[1:], lambda q: (q, 0, 0)) for p in parts]
    return pl.pallas_call(
        body, name="pair_sum", grid=(N_CHIPS,), in_specs=pair + one, out_specs=one,
        out_shape=[jax.ShapeDtypeStruct(s.shape, BF16) for s in sib],
        compiler_params=pltpu.CompilerParams(dimension_semantics=("parallel",)),
    )(*[p.reshape((N_CHIPS, 2) + p.shape[1:]) for p in parts], *sib)


_VMEM = pl.BlockSpec(memory_space=pltpu.VMEM)


def _reduce_out_shapes(parts, vals):
    return ([jax.ShapeDtypeStruct((N_CHIPS,) + p.shape[1:], p.dtype) for p in parts]
            + [jax.ShapeDtypeStruct(v.shape, v.dtype) for v in vals])


def _reduce_scratch(parts, vals):
    n, ns = len(parts), len(vals)
    quarter = [pltpu.VMEM((N_CHIPS,) + p.shape[1:], p.dtype) for p in parts]
    dma = pltpu.SemaphoreType.DMA
    travel = [BF16 if v.size > F32_TRAVEL_LIMIT else v.dtype for v in vals]
    return (quarter * 3 + [pltpu.VMEM(v.shape, v.dtype) for v in vals]
            + [pltpu.VMEM(v.shape, t) for v, t in zip(vals, travel)]
            + [pltpu.VMEM((N_CHIPS,) + v.shape, t) for v, t in zip(vals, travel)]
            + [dma((max(n, 1), N_CHIPS)), dma((max(n, 1), N_CHIPS)), dma((max(ns, 1),)), dma((max(ns, 1),)),
               dma((max(n, 1), 3)), dma((max(n, 1), 3)), dma((max(ns, 1), 3)), dma((max(ns, 1), 3))])


def _reduce_plan(p_in, v_in, p_out, v_out, scratch):
    n, ns = len(p_in), len(v_in)
    p_sib, p_sum, p_all = scratch[:n], scratch[n:2 * n], scratch[2 * n:3 * n]
    v_sib, v_sum, v_all = (scratch[3 * n + j * ns:3 * n + (j + 1) * ns] for j in range(3))
    p1_send, p1_recv, v1_send, v1_recv, p3_send, p3_recv, v3_send, v3_recv = scratch[3 * n + 3 * ns:]
    x, y, c = _position()
    my_q = 2 * x + y
    peers = [(x ^ fx, y ^ fy) for fx, fy in _CHIP_FLIPS]

    def to_sibling(src, dst, send, recv):
        return pltpu.make_async_remote_copy(src_ref=src, dst_ref=dst, send_sem=send, recv_sem=recv,
                                            device_id=(x, y, 1 - c), device_id_type=MESH)

    def level1():
        cps = [to_sibling(p_in[k].at[2 * q + 1 - c], p_sib[k].at[q], p1_send.at[k, q], p1_recv.at[k, q])
               for k in range(n) for q in range(N_CHIPS)]
        return cps + [to_sibling(v_in[k], v_sib[k], v1_send.at[k], v1_recv.at[k]) for k in range(ns)]

    def to_chip(f, src, dst, send, recv):
        px, py = peers[f]
        return pltpu.make_async_remote_copy(src_ref=src, dst_ref=dst, send_sem=send, recv_sem=recv,
                                            device_id=(px, py, c), device_id_type=MESH)

    def level2(sending):
        cps = []
        for f, (px, py) in enumerate(peers):
            their_q = 2 * px + py
            for k in range(n):
                src, dst = (p_sum[k].at[their_q], p_all[k].at[my_q]) if sending else (
                    p_sum[k].at[my_q], p_all[k].at[their_q])
                cps.append(to_chip(f, src, dst, p3_send.at[k, f], p3_recv.at[k, f]))
            for k in range(ns):
                dst = v_all[k].at[my_q] if sending else v_all[k].at[their_q]
                cps.append(to_chip(f, v_sum[k], dst, v3_send.at[k, f], v3_recv.at[k, f]))
        return cps

    def start():
        for cp in level1():
            cp.start()

    def middle():
        for cp in level1():
            cp.wait_recv()
        for k in range(n):
            for q in range(N_CHIPS):
                p_sum[k][q] = (p_in[k][2 * q + c].astype(F32) + p_sib[k][q].astype(F32)).astype(p_sum[k].dtype)
        for k in range(ns):
            v_sum[k][...] = (v_in[k][...] + v_sib[k][...]).astype(v_sum[k].dtype)
        for cp in level2(True):
            cp.start()
        for k in range(n):
            p_all[k][my_q] = p_sum[k][my_q]
        for k in range(ns):
            v_all[k][my_q] = v_sum[k][...]

    def finish():
        for cp in level2(False):
            cp.wait_recv()
        for k in range(n):
            p_out[k][...] = p_all[k][...]
        for k in range(ns):
            total = v_all[k][0].astype(F32)
            for q in range(1, N_CHIPS):
                total = total + v_all[k][q].astype(F32)
            v_out[k][...] = total
        for cp in level1() + level2(True):
            cp.wait_send()

    return start, middle, finish


def _adamw_math(w, g, m, v):
    m = ADAM_B1 * m + (1.0 - ADAM_B1) * g
    v = ADAM_B2 * v + (1.0 - ADAM_B2) * (g * g)
    m_hat = m / (1.0 - ADAM_B1 ** ADAM_STEP)
    v_hat = v / (1.0 - ADAM_B2 ** ADAM_STEP)
    delta = -ADAM_LR * (m_hat / (jnp.sqrt(v_hat) + ADAM_EPS) + ADAM_WD * w)
    return delta, m, v


def _adamw_shards(parts, ws, ms, vs, steps, name):
    n = len(ws)

    def body(*refs):
        ins, outs = refs[:4 * n], refs[4 * n:]
        for k in range(n):
            p_ref, w_ref, m_ref, v_ref = ins[4 * k:4 * k + 4]
            g = p_ref[0].astype(F32)
            for j in range(1, N_CHIPS):
                g = g + p_ref[j].astype(F32)
            d, mn, vn = _adamw_math(w_ref[...], g, m_ref[...], v_ref[...])
            for o, val in zip(outs[4 * k:4 * k + 4], (g, d, mn, vn)):
                o[...] = val

    in_specs, out_specs, out_shape, operands = [], [], [], []
    for p, w, m, v in zip(parts, ws, ms, vs):
        R, Cc = w.shape
        blk = pl.BlockSpec((R // steps, Cc), lambda i: (i, 0))
        in_specs += [pl.BlockSpec((N_CHIPS, R // steps, Cc), lambda i: (0, i, 0)), blk, blk, blk]
        out_specs += [blk] * 4
        out_shape += [jax.ShapeDtypeStruct((R, Cc), F32)] * 4
        operands += [p, w, m, v]
    res = pl.pallas_call(
        body, name=name, grid=(steps,), in_specs=in_specs, out_specs=out_specs, out_shape=out_shape,
        compiler_params=pltpu.CompilerParams(dimension_semantics=("parallel",), vmem_limit_bytes=VMEM_LIMIT),
    )(*operands)
    return [res[4 * k:4 * k + 4] for k in range(n)]


def _adamw_small(grads, ws, ms, vs):
    n = len(grads)
    vm = pl.BlockSpec(memory_space=pltpu.VMEM)

    def body(*refs):
        g_in, w_in, m_in, v_in = (refs[k * n:(k + 1) * n] for k in range(4))
        g_out, d_out, m_out, v_out = (refs[(4 + k) * n:(5 + k) * n] for k in range(4))
        for k in range(n):
            g = g_in[k][...]
            d, mn, vn = _adamw_math(w_in[k][...], g, m_in[k][...], v_in[k][...])
            g_out[k][...] = g
            d_out[k][...] = d
            m_out[k][...] = mn
            v_out[k][...] = vn

    shapes = [jax.ShapeDtypeStruct(g.shape, F32) for g in grads]
    return pl.pallas_call(
        body, name="adamw_small", out_shape=shapes * 4, in_specs=[vm] * (4 * n), out_specs=[vm] * (4 * n),
    )(*grads, *ws, *ms, *vs)


def _shard_cols(full):
    R, Ct = full.shape
    return jnp.transpose(full.reshape(R, N_DEV, Ct // N_DEV), (1, 0, 2))


_COL_SHARDED = ("w_in", "w_gate", "w_up")
_BIG = ("w_in", "w_out", "w_gate", "w_up", "w_down")


def _rows(nm, p):
    return p[0].T if nm in _COL_SHARDED else p[0]


def _step(args, ts_mix_fwd, ts_ffn, ts_mix_bwd, tm_grad):
    (x, g_mix, w_in, b_in, w_dw, b_dw, ln_g, ln_b, w_pool, s_pool, w_out, g_ffn, w_gate, w_up, w_down, g_final,
     loss_target) = args[:17]
    names = ["g_mix", "w_in", "b_in", "w_dw", "b_dw", "ln_g", "ln_b", "w_pool", "s_pool", "w_out", "g_ffn",
             "w_gate", "w_up", "w_down", "g_final"]
    weights = dict(zip(names, args[1:16]))
    moms = dict(zip(names, args[17:32]))
    vars_ = dict(zip(names, args[32:47]))

    S, D = x.shape[1], x.shape[2]
    x2 = x.reshape(S, D)
    tgt2 = loss_target.reshape(S, D)

    shard = lambda nm: _rows(nm, weights[nm])
    w_pool_b = w_pool[0].astype(BF16)
    dw_rows = -(-CONV_WIDTH // SUBLANES) * SUBLANES
    dw_shard = jnp.pad(w_dw[0], ((0, dw_rows - CONV_WIDTH), (0, LANES - w_dw.shape[2])))

    a, gate, v, m, y, h1, xn, w_dw_f, g_in, g_out, _, g_gate, g_up, g_down = _mix_fwd(
        x2, g_mix, b_in, b_dw, ln_g, ln_b, w_pool_b, s_pool, [shard("w_in"), shard("w_out"), dw_shard],
        [shard("w_gate"), shard("w_up"), shard("w_down")], ts_mix_fwd)
    wt_in, w_out_f = g_in.reshape(-1, D), g_out.reshape(-1, D)
    wt_gate, wt_up, w_down_f = g_gate.reshape(-1, D), g_up.reshape(-1, D), g_down.reshape(-1, D)
    Fd = wt_gate.shape[0]
    f_chunks = [1024] * (Fd // 1024) + ([Fd % 1024] if Fd % 1024 else [])
    dh1, hn, act, dgt, dup, dh2, loss_p, dg_final, dg_ffn = _ffn(
        h1, tgt2, g_ffn, g_final.reshape(1, D), wt_gate, wt_up, w_down_f, ts_ffn, f_chunks)

    by_shard = lambda g: g.reshape(N_DEV, -1, D)
    ffn_grads, ffn_sib = _grad_ffn([dgt, dup, act], [hn, hn, dh2], tm_grad)
    pair_sums = _pair_sum([by_shard(g) for g in ffn_grads], list(ffn_sib))
    (dx, dz, dh1b, dg_mix, db_in, dw_dw, db_dw, dln_g, dln_b, dw_pool, ds_pool, r_gate, r_up, r_down) = _mix_bwd(
        dh1, x2, a, gate, v, m, g_mix, wt_in, w_dw_f, ln_g, ln_b, w_pool_b, s_pool, w_out_f, pair_sums, ts_mix_bwd)

    small_names = ["g_mix", "b_in", "b_dw", "ln_g", "ln_b", "w_pool", "s_pool", "g_ffn", "g_final"]
    small_shape = lambda p: p.reshape(-1, p.shape[-1])
    partial = [dg_mix, db_in, db_dw, dln_g, dln_b, dw_pool.reshape(-1, POOL_GROUP), ds_pool, dg_ffn, dg_final, loss_p]
    r_out, r_dw, r_in, summed = _grad_mix(y, dh1b, dz, xn, _shard_cols(dw_dw[0:CONV_WIDTH]), partial)

    big = {}
    recv = dict(zip(_BIG, [r_in, r_out, r_gate, r_up, r_down]))
    for call, group in (("adamw_ffn", ("w_gate", "w_up", "w_down")), ("adamw_mix", ("w_in", "w_out"))):
        results = _adamw_shards([recv[nm] for nm in group], [_rows(nm, weights[nm]) for nm in group],
                                [_rows(nm, moms[nm]) for nm in group], [_rows(nm, vars_[nm]) for nm in group], 2, call)
        for nm, res in zip(group, results):
            big[nm] = [o.T if nm in _COL_SHARDED else o for o in res]
    big["w_dw"], = _adamw_shards([r_dw], [w_dw[0]], [moms["w_dw"][0]], [vars_["w_dw"][0]], 1, "adamw_w_dw")

    sm = _adamw_small(summed[:-1], [small_shape(weights[nm]) for nm in small_names],
                      [small_shape(moms[nm]) for nm in small_names], [small_shape(vars_[nm]) for nm in small_names])
    n_small = len(small_names)

    def result(kind, nm):
        if nm in big:
            return big[nm][kind].reshape(weights[nm].shape)
        return sm[kind * n_small + small_names.index(nm)].reshape(weights[nm].shape)

    loss = summed[-1][0, 0]
    out = [loss, dx.reshape(x.shape)]
    for kind in range(4):
        out += [result(kind, nm) for nm in names]
    return tuple(out)


def kernel(x, g_mix, w_in, b_in, w_dw, b_dw, ln_g, ln_b, w_pool, s_pool, w_out, g_ffn, w_gate, w_up, w_down, g_final, loss_target, m_g_mix, m_w_in, m_b_in, m_w_dw, m_b_dw, m_ln_g, m_ln_b, m_w_pool, m_s_pool, m_w_out, m_g_ffn, m_w_gate, m_w_up, m_w_down, m_g_final, v_g_mix, v_w_in, v_b_in, v_w_dw, v_b_dw, v_ln_g, v_ln_b, v_w_pool, v_s_pool, v_w_out, v_g_ffn, v_w_gate, v_w_up, v_w_down, v_g_final):
    args = (x, g_mix, w_in, b_in, w_dw, b_dw, ln_g, ln_b, w_pool, s_pool, w_out, g_ffn, w_gate, w_up, w_down, g_final, loss_target, m_g_mix, m_w_in, m_b_in, m_w_dw, m_b_dw, m_ln_g, m_ln_b, m_w_pool, m_s_pool, m_w_out, m_g_ffn, m_w_gate, m_w_up, m_w_down, m_g_final, v_g_mix, v_w_in, v_b_in, v_w_dw, v_b_dw, v_ln_g, v_ln_b, v_w_pool, v_s_pool, v_w_out, v_g_ffn, v_w_gate, v_w_up, v_w_down, v_g_final)
    return _step(args, ts_mix_fwd=512, ts_ffn=256, ts_mix_bwd=512, tm_grad=256)
```

```python
import jax
import jax.numpy as jnp
from jax import lax
from jax.experimental import pallas as pl
from jax.experimental.pallas import tpu as pltpu

F32 = jnp.float32
BF16 = jnp.bfloat16
MESH = pl.DeviceIdType.MESH
N_DEV = 8

C_CONV = 512
CONV_WIDTH = 31
POOL_WINDOWS = (2, 4, 8, 16)
POOL_GROUP = 128
RMS_EPS = 1e-6
LN_EPS = 1e-5

ADAM_LR = 0.001
ADAM_B1 = 0.9
ADAM_B2 = 0.999
ADAM_EPS = 1e-08
ADAM_WD = 0.01
ADAM_STEP = 10

HALO = 32
SUBLANES = 8
LANES = 128
CONV_ROWS = 64
VMEM_LIMIT = 56 * 1024 * 1024


def _dot(a, b):
    return jnp.dot(a, b, preferred_element_type=F32)


def _dot_nt(a, b):
    return lax.dot_general(a, b, (((1,), (1,)), ((), ())), preferred_element_type=F32)


def _dot_tn(a, b):
    return lax.dot_general(a, b, (((0,), (0,)), ((), ())), preferred_element_type=F32)


def _mean_last(v):
    return jnp.mean(v, axis=-1, keepdims=True)


def _full(shape):
    nd = len(shape)
    return pl.BlockSpec(shape, lambda *_: (0,) * nd)


def _full1(shape):
    nd = len(shape)
    return pl.BlockSpec(shape, lambda *_: (0,) * nd, pipeline_mode=pl.Buffered(1))


def _shifted_copies(sh_ref, rows):
    for s in range(1, SUBLANES):
        sh_ref[s, 0:rows, :] = sh_ref[0, s:s + rows, :]


def _tap(sh_ref, off, r0, rows):
    q, s = divmod(off, SUBLANES)
    return sh_ref[s, pl.ds(r0 + q * SUBLANES, rows), :]


def _window_sums(src, bufs, rows, backward):
    assert all(w == 2 << g for g, w in enumerate(POOL_WINDOWS))
    n = len(POOL_WINDOWS)
    out = []
    for level in range(n):
        dst, shift = bufs[level % 2], 1 << level
        lanes = slice(level * POOL_GROUP, n * POOL_GROUP)
        lo, hi = (SUBLANES * (level + 1), rows) if backward else (0, rows - SUBLANES * (level + 1))
        other = slice(lo - shift, hi - shift) if backward else slice(lo + shift, hi + shift)
        dst[lo:hi, lanes] = src[lo:hi, lanes] + src[other, lanes]
        src = dst
        out.append(dst)
    return out


def _mix_fwd(x, g_mix, b_in, b_dw, ln_g, ln_b, w_pool, s_pool, mix_shards, ag_shards, ts):
    S, D = x.shape
    d_in = mix_shards[0].shape[0] * N_DEV
    C = C_CONV
    nt = S // ts
    nrb = ts // CONV_ROWS
    n_ag = len(ag_shards)
    relay_step = nt // 2
    mix_dtypes = [BF16, BF16, F32]

    def body(x_ref, g_ref, bin_ref, bdw_ref, lng_ref, lnb_ref, wp_ref, sp_ref, *rest):
        mx_in, ag_in, rest = rest[:3], rest[3:3 + n_ag], rest[3 + n_ag:]
        a_ref, gate_ref, v_ref, m_ref, y_ref, h1_ref, xn_ref, wdw_ref = rest[:8]
        mx_out, ag_out, rest = rest[8:11], rest[11:11 + n_ag], rest[11 + n_ag:]
        ush, pbuf, pa, pb, win_ref, wout_ref, gdw, load_sems = rest[:8]
        rest = rest[8:]
        (ma_stage, ma_bufs, ma_sems), rest = (rest[:2], rest[2:4], rest[4:7]), rest[7:]
        (mb_stage, mb_bufs, mb_sems), rest = (rest[:1], rest[1:2], rest[2:5]), rest[5:]
        ag_stage, ag_bufs, ag_sems = rest[:n_ag], rest[n_ag:2 * n_ag], rest[2 * n_ag:]
        i = pl.program_id(0)
        ma_start, ma_relay, ma_pass_on, ma_finish = _gather_plan(
            mx_in[0::2], mx_out[0::2], ma_stage, ma_bufs, *ma_sems)
        mb_start, mb_relay, mb_pass_on, mb_finish = _gather_plan(
            mx_in[1:2], mx_out[1:2], mb_stage, mb_bufs, *mb_sems)
        ag_start, ag_relay, ag_pass_on, ag_finish = _gather_plan(ag_in, ag_out, ag_stage, ag_bufs, *ag_sems)

        def load_weight(k, dst_ref):
            rows = mx_out[k].shape[1]
            loads = [pltpu.make_async_copy(mx_out[k].at[j], dst_ref.at[pl.ds(j * rows, rows), :],
                                           load_sems.at[N_DEV * k + j]) for j in range(N_DEV)]
            for cp in loads:
                cp.start()
            for cp in loads:
                cp.wait()

        @pl.when(i == 0)
        def _():
            ma_start()
            mb_start()
            ma_relay()
            mb_relay()
            ag_start()
            ma_pass_on()
            ma_finish()
            load_weight(0, win_ref)
            dw_load = pltpu.make_async_copy(mx_out[2], gdw, load_sems.at[2 * N_DEV])
            dw_load.start()
            dw_load.wait()
            first_half = lax.broadcasted_iota(jnp.int32, gdw.shape[1:], 1) < C // N_DEV
            for p in range(N_DEV // 2):
                wdw_ref[:, LANES * p:LANES * (p + 1)] = jnp.where(
                    first_half, gdw[2 * p], pltpu.roll(gdw[2 * p + 1], C // N_DEV, axis=1))
            ush[0, 0:HALO, :] = jnp.zeros((HALO, C), F32)
            pbuf[0:HALO, :] = jnp.zeros((HALO, C), F32)

        @pl.when(i == relay_step)
        def _():
            ag_relay()

        @pl.when(i == nt - 1)
        def _():
            ag_pass_on()

        x = x_ref[...]
        r1 = lax.rsqrt(_mean_last(x * x) + RMS_EPS)
        xn = (x * r1 * g_ref[...]).astype(BF16)
        xn_ref[...] = xn
        z = _dot_nt(xn, win_ref[...]) + bin_ref[...]
        a = z[:, 0:C]
        gate = z[:, C:2 * C]
        a_ref[...] = a
        gate_ref[...] = gate
        ush[0, HALO:HALO + ts, :] = a * jax.nn.sigmoid(gate)
        pbuf[HALO:HALO + ts, :] = z[:, 2 * C:]

        _shifted_copies(ush, ts + HALO - SUBLANES)

        def conv_block(rb, carry):
            r0 = pl.multiple_of(rb * CONV_ROWS, CONV_ROWS)
            acc = jnp.zeros((CONV_ROWS, C), F32)
            for k in range(CONV_WIDTH):
                acc = acc + wdw_ref[k:k + 1, :] * _tap(ush, HALO - (CONV_WIDTH - 1) + k, r0, CONV_ROWS)
            v_ref[pl.ds(r0, CONV_ROWS), :] = acc + bdw_ref[...]
            return carry

        lax.fori_loop(0, nrb, conv_block, 0)

        v = v_ref[...]
        mu = _mean_last(v)
        xc = v - mu
        rstd = lax.rsqrt(_mean_last(xc * xc) + LN_EPS)
        ln = xc * rstd * lng_ref[...] + lnb_ref[...]
        y_ref[:, 0:C] = (ln * jax.nn.sigmoid(ln)).astype(BF16)

        sums = _window_sums(pbuf, (pa, pb), ts + HALO, backward=True)
        row = lax.broadcasted_iota(jnp.int32, (ts, 1), 0) + i * ts
        for gi, w in enumerate(POOL_WINDOWS):
            lanes = slice(gi * POOL_GROUP, (gi + 1) * POOL_GROUP)
            seg = pbuf[HALO:HALO + ts, lanes]
            ws = sums[gi][HALO:HALO + ts, lanes]
            cnt = jnp.minimum(row + 1, w).astype(F32)
            m = (ws / cnt - seg).astype(BF16)
            m_ref[:, lanes] = m
            ypre = _dot(m, wp_ref[gi])
            y_ref[:, C + gi * POOL_GROUP:C + (gi + 1) * POOL_GROUP] = (ypre * sp_ref[:, lanes]).astype(BF16)

        @pl.when(i == 0)
        def _():
            mb_pass_on()
            mb_finish()
            load_weight(1, wout_ref)

        h1_ref[...] = x + _dot(y_ref[...], wout_ref[...])

        ush[0, 0:HALO, :] = ush[0, ts:ts + HALO, :]
        pbuf[0:HALO, :] = pbuf[ts:ts + HALO, :]

        @pl.when(i == nt - 1)
        def _():
            ag_finish()

    tile = lambda w, dt: (pl.BlockSpec((ts, w), lambda i: (i, 0)), jax.ShapeDtypeStruct((S, w), dt))
    wdw_rows = mix_shards[2].shape[0]
    outs = [tile(C, F32), tile(C, F32), tile(C, F32), tile(C, BF16), tile(D, BF16), tile(D, F32), tile(D, BF16),
            (_full((wdw_rows, C)), jax.ShapeDtypeStruct((wdw_rows, C), F32))]
    return pl.pallas_call(
        body, name="mix_fwd", grid=(nt,),
        in_specs=[pl.BlockSpec((ts, D), lambda i: (i, 0)), _full((1, D)), _full((1, d_in)), _full((1, C)),
                  _full((1, C)), _full((1, C)), _full(w_pool.shape), _full((1, C))] + [_HBM] * (3 + n_ag),
        out_specs=[o[0] for o in outs] + [_HBM] * (3 + n_ag),
        out_shape=[o[1] for o in outs] + _gather_out_shapes(mix_shards, mix_dtypes)
        + _gather_out_shapes(ag_shards, [BF16] * n_ag),
        scratch_shapes=[pltpu.VMEM((SUBLANES, ts + HALO, C), F32)] + [pltpu.VMEM((ts + HALO, C), F32)] * 3
        + [pltpu.VMEM((d_in, D), BF16), pltpu.VMEM((D, D), BF16),
                        pltpu.VMEM((N_DEV,) + mix_shards[2].shape, F32), pltpu.SemaphoreType.DMA((2 * N_DEV + 1,))]
        + _gather_scratch(mix_shards[0::2], mix_dtypes[0::2]) + _gather_scratch(mix_shards[1:2], mix_dtypes[1:2])
        + _gather_scratch(ag_shards, [BF16] * n_ag),
        compiler_params=pltpu.CompilerParams(dimension_semantics=("arbitrary",), vmem_limit_bytes=VMEM_LIMIT),
    )(x, g_mix, b_in, b_dw, ln_g, ln_b, w_pool, s_pool, *mix_shards, *ag_shards)


def _ffn(h1, target, g_ffn, g_final, w_gate, w_up, w_down, ts, f_chunks):
    S, D = h1.shape
    Fd = w_gate.shape[0]
    nt = S // ts
    bounds = []
    lo = 0
    for n in f_chunks:
        bounds.append((lo, lo + n))
        lo += n
    assert lo == Fd

    def body(h1_ref, tgt_ref, gf_ref, gl_ref, wg_ref, wu_ref, wd_ref,
             dh1_ref, hn_ref, act_ref, dgt_ref, dup_ref, dh2_ref, loss_ref, dgl_ref, dgf_ref, gt_s, up_s):
        i = pl.program_id(0)

        @pl.when(i == 0)
        def _():
            loss_ref[...] = jnp.zeros_like(loss_ref)
            dgl_ref[...] = jnp.zeros_like(dgl_ref)
            dgf_ref[...] = jnp.zeros_like(dgf_ref)

        h1 = h1_ref[...]
        r2 = lax.rsqrt(_mean_last(h1 * h1) + RMS_EPS)
        hhat = h1 * r2
        hn = (hhat * gf_ref[...]).astype(BF16)
        hn_ref[...] = hn
        h2 = h1
        for lo, hi in bounds:
            gt = _dot_nt(hn, wg_ref[lo:hi, :])
            up = _dot_nt(hn, wu_ref[lo:hi, :])
            gt_s[:, lo:hi] = gt
            up_s[:, lo:hi] = up
            act = (gt * jax.nn.sigmoid(gt) * up).astype(BF16)
            act_ref[:, lo:hi] = act
            h2 = h2 + _dot(act, wd_ref[lo:hi, :])

        r3 = lax.rsqrt(_mean_last(h2 * h2) + RMS_EPS)
        n3 = h2 * r3
        gl = gl_ref[...]
        diff = n3 * gl - tgt_ref[...]
        loss_ref[...] += jnp.sum(0.5 * jnp.sum(diff * diff, axis=-1, keepdims=True) / D, axis=0, keepdims=True)
        dout = diff / D
        dgl_ref[...] += jnp.sum(dout * n3, axis=0, keepdims=True)
        dn = dout * gl
        dh2 = r3 * (dn - n3 * _mean_last(dn * n3))
        dh2b = dh2.astype(BF16)
        dh2_ref[...] = dh2b

        dhn = jnp.zeros((ts, D), F32)
        for lo, hi in bounds:
            gt = gt_s[:, lo:hi]
            up = up_s[:, lo:hi]
            sg = jax.nn.sigmoid(gt)
            dact = _dot_nt(dh2b, wd_ref[lo:hi, :])
            dgt = (dact * up * (sg * (1.0 + gt * (1.0 - sg)))).astype(BF16)
            dup = (dact * (gt * sg)).astype(BF16)
            dgt_ref[:, lo:hi] = dgt
            dup_ref[:, lo:hi] = dup
            dhn = dhn + _dot(dgt, wg_ref[lo:hi, :]) + _dot(dup, wu_ref[lo:hi, :])

        dgf_ref[...] += jnp.sum(dhn * hhat, axis=0, keepdims=True)
        dnn = dhn * gf_ref[...]
        dh1_ref[...] = dh2 + r2 * (dnn - hhat * _mean_last(dnn * hhat))

    tile = lambda w, dt: (pl.BlockSpec((ts, w), lambda i: (i, 0)), jax.ShapeDtypeStruct((S, w), dt))
    acc = lambda w: (_full((1, w)), jax.ShapeDtypeStruct((1, w), F32))
    outs = [tile(D, F32), tile(D, BF16), tile(Fd, BF16), tile(Fd, BF16), tile(Fd, BF16), tile(D, BF16),
            acc(LANES), acc(D), acc(D)]
    return pl.pallas_call(
        body, name="ffn_fwd_bwd", grid=(nt,),
        in_specs=[pl.BlockSpec((ts, D), lambda i: (i, 0)), pl.BlockSpec((ts, D), lambda i: (i, 0)),
                  _full((1, D)), _full((1, D)), _full1((Fd, D)), _full1((Fd, D)), _full1((Fd, D))],
        out_specs=[o[0] for o in outs], out_shape=[o[1] for o in outs],
        scratch_shapes=[pltpu.VMEM((ts, Fd), F32), pltpu.VMEM((ts, Fd), F32)],
        compiler_params=pltpu.CompilerParams(dimension_semantics=("arbitrary",), vmem_limit_bytes=VMEM_LIMIT),
    )(h1, target, g_ffn, g_final, w_gate, w_up, w_down)


def _mix_bwd(dh1, x, a, gate, v, m, g_mix, w_in, w_dw, ln_g, ln_b, w_pool, s_pool, w_out, rs_parts, ts):
    S, D = x.shape
    n_rs = len(rs_parts)
    d_in = w_in.shape[0]
    C = C_CONV
    nt = S // ts
    nrb = ts // CONV_ROWS
    wrows =((CONV_WIDTH + SUBLANES - 1) // SUBLANES) * SUBLANES

    def body(dh1_ref, x_ref, a_ref, gate_ref, v_ref, m_ref, g_ref, win_ref, wdw_ref, lng_ref,
             lnb_ref, wp_ref, sp_ref, wout_ref, *rest):
        rs_in, rest = rest[:n_rs], rest[n_rs:]
        (dx_ref, dz_ref, dh1b_ref, dgm_ref, dbin_ref, dwdw_ref, dbdw_ref, dlng_ref, dlnb_ref, dwp_ref,
         dsp_ref) = rest[:11]
        rs_out, rest = rest[11:11 + n_rs], rest[11 + n_rs:]
        dvsh, dqbuf, qa, qb, du_s, dm_s = rest[:6]
        rs_bufs, (send_sems, recv_sems, local_sems) = rest[6:6 + n_rs], rest[6 + n_rs:]
        i = pl.program_id(0)
        t = nt - 1 - i
        rs_start, rs_finish = _chip_exchange_plan(rs_in, rs_out, rs_bufs, send_sems, recv_sems, local_sems)

        @pl.when(i == 0)
        def _():
            rs_start()
            dvsh[0, ts:ts + HALO, :] = jnp.zeros((HALO, C), F32)
            dqbuf[ts:ts + HALO, :] = jnp.zeros((HALO, C), F32)
            for r in (dgm_ref, dbin_ref, dwdw_ref, dbdw_ref, dlng_ref, dlnb_ref, dwp_ref, dsp_ref):
                r[...] = jnp.zeros_like(r)

        dh1 = dh1_ref[...]
        dh1b = dh1.astype(BF16)
        dh1b_ref[...] = dh1b
        dy = _dot_nt(dh1b, wout_ref[...])

        v = v_ref[...]
        mu = _mean_last(v)
        xc = v - mu
        rstd = lax.rsqrt(_mean_last(xc * xc) + LN_EPS)
        vhat = xc * rstd
        lng = lng_ref[...]
        ln = vhat * lng + lnb_ref[...]
        sg = jax.nn.sigmoid(ln)
        dln = dy[:, 0:C] * (sg * (1.0 + ln * (1.0 - sg)))
        dlng_ref[...] += jnp.sum(dln * vhat, axis=0, keepdims=True)
        dlnb_ref[...] += jnp.sum(dln, axis=0, keepdims=True)
        dvh = dln * lng
        dv = rstd * (dvh - _mean_last(dvh) - vhat * _mean_last(dvh * vhat))
        dbdw_ref[...] += jnp.sum(dv, axis=0, keepdims=True)
        dvsh[0, 0:ts, :] = dv
        _shifted_copies(dvsh, ts + HALO - SUBLANES)

        def conv_block(rb, carry):
            r0 = pl.multiple_of(rb * CONV_ROWS, CONV_ROWS)
            acc = jnp.zeros((CONV_ROWS, C), F32)
            for k in range(CONV_WIDTH):
                acc = acc + wdw_ref[k:k + 1, :] * _tap(dvsh, CONV_WIDTH - 1 - k, r0, CONV_ROWS)
            du_s[pl.ds(r0, CONV_ROWS), :] = acc
            return carry

        lax.fori_loop(0, nrb, conv_block, 0)

        a = a_ref[...]
        sgate = jax.nn.sigmoid(gate_ref[...])
        u = a * sgate
        for k in range(CONV_WIDTH):
            q, s = divmod(CONV_WIDTH - 1 - k, SUBLANES)
            prod = u * dvsh[s, q * SUBLANES:q * SUBLANES + ts, :]
            dwdw_ref[k:k + 1, :] += jnp.sum(prod, axis=0, keepdims=True)

        du = du_s[...]
        da = du * sgate
        dgate = du * a * sgate * (1.0 - sgate)
        dz_ref[:, 0:C] = da.astype(BF16)
        dz_ref[:, C:2 * C] = dgate.astype(BF16)
        dbin_ref[:, 0:C] += jnp.sum(da, axis=0, keepdims=True)
        dbin_ref[:, C:2 * C] += jnp.sum(dgate, axis=0, keepdims=True)

        row = lax.broadcasted_iota(jnp.int32, (ts, 1), 0) + t * ts
        for gi, w in enumerate(POOL_WINDOWS):
            lanes = slice(gi * POOL_GROUP, (gi + 1) * POOL_GROUP)
            dyp = dy[:, C + gi * POOL_GROUP:C + (gi + 1) * POOL_GROUP]
            mg = m_ref[:, lanes]
            ypre = _dot(mg, wp_ref[gi])
            dsp_ref[:, lanes] += jnp.sum(dyp * ypre, axis=0, keepdims=True)
            dyi = (dyp * sp_ref[:, lanes]).astype(BF16)
            dwp_ref[gi] += _dot_tn(mg, dyi)
            dm = _dot_nt(dyi, wp_ref[gi])
            cnt = jnp.minimum(row + 1, w).astype(F32)
            dqbuf[0:ts, lanes] = dm / cnt
            dm_s[:, lanes] = dm
        sums = _window_sums(dqbuf, (qa, qb), ts + HALO, backward=False)
        for gi in range(len(POOL_WINDOWS)):
            lanes = slice(gi * POOL_GROUP, (gi + 1) * POOL_GROUP)
            dp = sums[gi][0:ts, lanes] - dm_s[:, lanes]
            dz_ref[:, 2 * C + gi * POOL_GROUP:2 * C + (gi + 1) * POOL_GROUP] = dp.astype(BF16)
            dbin_ref[:, 2 * C + gi * POOL_GROUP:2 * C + (gi + 1) * POOL_GROUP] += jnp.sum(dp, axis=0, keepdims=True)

        dxn = _dot(dz_ref[...], win_ref[...])
        x = x_ref[...]
        r1 = lax.rsqrt(_mean_last(x * x) + RMS_EPS)
        xhat = x * r1
        dgm_ref[...] += jnp.sum(dxn * xhat, axis=0, keepdims=True)
        dnn = dxn * g_ref[...]
        dx_ref[...] = dh1 + r1 * (dnn - xhat * _mean_last(dnn * xhat))

        dvsh[0, ts:ts + HALO, :] = dvsh[0, 0:HALO, :]
        dqbuf[ts:ts + HALO, :] = dqbuf[0:HALO, :]

        @pl.when(i == nt - 1)
        def _():
            rs_finish()

    rev = lambda w: pl.BlockSpec((ts, w), lambda i: (nt - 1 - i, 0))
    acc = lambda shape: (_full(shape), jax.ShapeDtypeStruct(shape, F32))
    outs = [(rev(D), jax.ShapeDtypeStruct((S, D), F32)), (rev(d_in), jax.ShapeDtypeStruct((S, d_in), BF16)),
            (rev(D), jax.ShapeDtypeStruct((S, D), BF16)), acc((1, D)), acc((1, d_in)), acc((wrows, C)),
            acc((1, C)), acc((1, C)), acc((1, C)), acc(w_pool.shape), acc((1, C))]
    return pl.pallas_call(
        body, name="mix_bwd", grid=(nt,),
        in_specs=[rev(D), rev(D), rev(C), rev(C), rev(C), rev(C), _full((1, D)), _full((d_in, D)),
                  _full(w_dw.shape), _full((1, C)), _full((1, C)), _full(w_pool.shape), _full((1, C)),
                  _full((D, D))] + [_HBM] * n_rs,
        out_specs=[o[0] for o in outs] + [_HBM] * n_rs,
        out_shape=[o[1] for o in outs] + [jax.ShapeDtypeStruct(p.shape, p.dtype) for p in rs_parts],
        scratch_shapes=[pltpu.VMEM((SUBLANES, ts + HALO, C), F32)] + [pltpu.VMEM((ts + HALO, C), F32)] * 3
        + [pltpu.VMEM((ts, C), F32)] * 2
        + [pltpu.VMEM(p.shape[1:], p.dtype) for p in rs_parts] + _comm_sems(n_rs, 3),
        compiler_params=pltpu.CompilerParams(dimension_semantics=("arbitrary",), vmem_limit_bytes=VMEM_LIMIT),
    )(dh1, x, a, gate, v, m, g_mix, w_in, w_dw, ln_g, ln_b, w_pool, s_pool, w_out, *rs_parts)


def _grad_mix(y, dh1b, dz, xn, dw_dw, vals):
    S, D = y.shape
    d_in = dz.shape[1]
    r_out, r_in = D // N_DEV, d_in // N_DEV
    n_out, n_in = N_DEV // 2, N_DEV // 2
    steps = n_out + n_in
    ns = len(vals)
    dw_out = jax.ShapeDtypeStruct((N_DEV, r_out, D), BF16)
    dw_in = jax.ShapeDtypeStruct((N_DEV, r_in, D), BF16)
    s_scratch, o_scratch = _reduce_scratch([], vals), _reduce_scratch([dw_out, dw_dw], [])
    i_scratch = _reduce_scratch([dw_in], [])

    def body(y_ref, dh_ref, dz_ref, xn_ref, dwdw_ref, *rest):
        v_in, p_out, v_out = rest[:ns], rest[ns:ns + 3], rest[ns + 3:2 * ns + 3]
        rest = rest[2 * ns + 3:]
        dwo_v, dwi_v, rest = rest[0], rest[1], rest[2:]
        s_refs, rest = rest[:len(s_scratch)], rest[len(s_scratch):]
        o_refs, i_refs = rest[:len(o_scratch)], rest[len(o_scratch):]
        s_start, s_middle, s_finish = _reduce_plan((), v_in, (), v_out, s_refs)
        o_start, o_middle, o_finish = _reduce_plan((dwo_v, dwdw_ref), (), p_out[:2], (), o_refs)
        i_start, i_middle, i_finish = _reduce_plan((dwi_v,), (), p_out[2:], (), i_refs)
        s = pl.program_id(0)
        pl.when(s == 0)(s_start)
        pl.when(s == 1)(s_middle)

        @pl.when(s < n_out)
        def _():
            res = _dot_tn(y_ref[...], dh_ref[...]).astype(BF16)
            dwo_v[2 * s] = res[0:r_out]
            dwo_v[2 * s + 1] = res[r_out:]

        pl.when(s == n_out - 1)(s_finish)
        pl.when(s == n_out)(o_start)
        pl.when(s == n_out + 1)(o_middle)

        @pl.when(s >= n_out)
        def _():
            res = _dot_tn(dz_ref[...], xn_ref[...]).astype(BF16)
            dwi_v[2 * (s - n_out)] = res[0:r_in]
            dwi_v[2 * (s - n_out) + 1] = res[r_in:]

        @pl.when(s == steps - 1)
        def _():
            o_finish()
            i_start()
            i_middle()
            i_finish()

    late = lambda s: jnp.clip(s - n_out, 0, n_in - 1)
    res = pl.pallas_call(
        body, name="grad_w_mix", grid=(steps,),
        in_specs=[pl.BlockSpec((S, 2 * r_out), lambda s: (0, jnp.minimum(s, n_out - 1))), _full1((S, D)),
                  pl.BlockSpec((S, 2 * r_in), lambda s: (0, late(s))), _full1((S, D))] + [_VMEM] * (1 + ns),
        out_specs=[_VMEM] * (3 + ns),
        out_shape=_reduce_out_shapes([dw_out, dw_dw, dw_in], []) + _reduce_out_shapes([], vals),
        scratch_shapes=[pltpu.VMEM(dw_out.shape, BF16), pltpu.VMEM(dw_in.shape, BF16)] + s_scratch + o_scratch
        + i_scratch,
        compiler_params=pltpu.CompilerParams(dimension_semantics=("arbitrary",), vmem_limit_bytes=VMEM_LIMIT),
    )(y, dh1b, dz, xn, dw_dw, *vals)
    return res[0], res[1], res[2], res[3:]


def _grad_ffn(a_list, b_list, tm):
    n = len(a_list)
    S, M = a_list[0].shape
    N = b_list[0].shape[1]
    nb = M // tm
    steps = n * nb
    R = M // N_DEV
    b_unique = [b for k, b in enumerate(b_list) if all(b is not o for o in b_list[:k])]
    b_index = [[b is o for o in b_unique].index(True) for b in b_list]

    def body(*refs):
        a_refs, b_refs = refs[:n], refs[n:n + len(b_unique)]
        rest = refs[n + len(b_unique):]
        g_refs, sib_refs, (obuf, w_sems, send_sems, recv_sems) = rest[:n], rest[n:2 * n], rest[2 * n:]
        s = pl.program_id(0)
        slot = s % 2
        x, y, c = _position()

        def write_back(dst):
            return pltpu.make_async_copy(obuf.at[slot], dst, w_sems.at[slot])

        def exchange(k):
            return [pltpu.make_async_remote_copy(
                src_ref=g_refs[k].at[pl.ds(pl.multiple_of((2 * q + 1 - c) * R, SUBLANES * 2), R), :],
                dst_ref=sib_refs[k].at[q], send_sem=send_sems.at[k, q], recv_sem=recv_sems.at[k, q],
                device_id=(x, y, 1 - c), device_id_type=MESH) for q in range(N_CHIPS)]

        pl.when(s >= 2)(write_back(g_refs[0].at[pl.ds(0, tm), :]).wait)
        for k in range(n):
            @pl.when(jnp.logical_and(s >= k * nb, s < (k + 1) * nb))
            def _(k=k):
                obuf[slot] = _dot_tn(a_refs[k][...], b_refs[b_index[k]][...]).astype(BF16)
                row = pl.multiple_of((s - k * nb) * tm, tm)
                write_back(g_refs[k].at[pl.ds(row, tm), :]).start()
            if k < n - 1:
                @pl.when(s == (k + 1) * nb + 2)
                def _(k=k):
                    for cp in exchange(k):
                        cp.start()

        @pl.when(s == steps - 1)
        def _():
            write_back(g_refs[0].at[pl.ds(0, tm), :]).wait()
            pltpu.make_async_copy(obuf.at[1 - slot], g_refs[0].at[pl.ds(0, tm), :], w_sems.at[1 - slot]).wait()
            for cp in exchange(n - 1):
                cp.start()
            for k in range(n):
                for cp in exchange(k):
                    cp.wait()

    clamp = lambda k: (lambda s: (0, jnp.clip(s - k * nb, 0, nb - 1)))
    sems = pltpu.SemaphoreType.DMA((n, N_CHIPS))
    res = pl.pallas_call(
        body, name="grad_w_ffn", grid=(steps,),
        in_specs=[pl.BlockSpec((S, tm), clamp(k)) for k in range(n)] + [_full1((S, N))] * len(b_unique),
        out_specs=[_HBM] * (2 * n),
        out_shape=[jax.ShapeDtypeStruct((M, N), BF16)] * n + [jax.ShapeDtypeStruct((N_CHIPS, R, N), BF16)] * n,
        scratch_shapes=[pltpu.VMEM((2, tm, N), BF16), pltpu.SemaphoreType.DMA((2,)), sems, sems],
        compiler_params=pltpu.CompilerParams(dimension_semantics=("arbitrary",), vmem_limit_bytes=VMEM_LIMIT),
    )(*a_list, *b_unique)
    return res[:n], res[n:]


def _position():
    return lax.axis_index("x"), lax.axis_index("y"), lax.axis_index("c")


def _slot(px, py, pc):
    return 4 * px + 2 * py + pc


_HBM = pl.BlockSpec(memory_space=pl.ANY)


def _comm_sems(n, copies):
    return [pltpu.SemaphoreType.DMA((n, copies)), pltpu.SemaphoreType.DMA((n, copies)),
            pltpu.SemaphoreType.DMA((n, 2))]


def _local_copy(srcs, dsts, bufs, local_sems):
    n = len(srcs)
    loads = [pltpu.make_async_copy(srcs[k], bufs[k], local_sems.at[k, 0]) for k in range(n)]
    for cp in loads:
        cp.start()
    for cp in loads:
        cp.wait()
    stores = _local_stores(dsts, bufs, local_sems)
    for cp in stores:
        cp.start()
    return stores


def _local_stores(dsts, bufs, local_sems):
    return [pltpu.make_async_copy(bufs[k], dsts[k], local_sems.at[k, 1]) for k in range(len(dsts))]


def _gather_out_shapes(shards, dtypes):
    return [jax.ShapeDtypeStruct((N_DEV,) + s.shape, dt) for s, dt in zip(shards, dtypes)]


def _gather_scratch(shards, dtypes):
    return ([pltpu.VMEM(s.shape, s.dtype) for s in shards] + [pltpu.VMEM(s.shape, dt) for s, dt in zip(shards, dtypes)]
            + _comm_sems(len(shards), 7))


def _gather_plan(ins, outs, stage, bufs, send_sems, recv_sems, local_sems):
    n = len(ins)
    x, y, c = _position()
    me, sibling = (x, y, c), (x, y, 1 - c)
    na, nb, dg = (x ^ (1 - c), y ^ c), (x ^ c, y ^ (1 - c)), (1 - x, 1 - y)
    own = [outs[k].at[_slot(*me)] for k in range(n)]

    def copy(k, sem, block, to, src=None):
        dst = outs[k].at[_slot(*block)]
        return pltpu.make_async_remote_copy(
            src_ref=dst if src is None else src, dst_ref=dst, send_sem=send_sems.at[k, sem],
            recv_sem=recv_sems.at[k, sem], device_id=to, device_id_type=MESH)

    def first():
        cps = []
        for k in range(n):
            cps += [copy(k, 0, me, sibling, src=bufs[k]), copy(k, 1, me, (*na, c), src=bufs[k]),
                    copy(k, 2, me, (*nb, c), src=bufs[k])]
        return cps

    def onward():
        return [copy(k, 3, (*na, c), (*nb, c)) for k in range(n)]

    def to_sibling(j, chip):
        return [copy(k, 4 + j, (*chip, c), sibling) for k in range(n)]

    def start():
        loads = [pltpu.make_async_copy(ins[k], stage[k], local_sems.at[k, 0]) for k in range(n)]
        for cp in loads:
            cp.start()
        for cp in loads:
            cp.wait()
        for k in range(n):
            bufs[k][...] = stage[k][...].astype(bufs[k].dtype)
        for cp in first() + _local_stores(own, bufs, local_sems):
            cp.start()

    def relay():
        for k in range(n):
            copy(k, 1, (*na, c), me).wait_recv()
        for cp in onward() + to_sibling(0, na):
            cp.start()
        for k in range(n):
            copy(k, 2, (*nb, c), me).wait_recv()
        for cp in to_sibling(1, nb):
            cp.start()

    def pass_on():
        for k in range(n):
            copy(k, 3, (*dg, c), me).wait_recv()
        for cp in to_sibling(2, dg):
            cp.start()

    def finish():
        for k in range(n):
            copy(k, 0, sibling, me).wait_recv()
            for j, chip in enumerate((nb, na, dg)):
                copy(k, 4 + j, (*chip, 1 - c), me).wait_recv()
        for cp in first() + onward() + to_sibling(0, na) + to_sibling(1, nb) + to_sibling(2, dg):
            cp.wait_send()
        for cp in _local_stores(own, bufs, local_sems):
            cp.wait()

    return start, relay, pass_on, finish


N_CHIPS = 4
_CHIP_FLIPS = [(1, 0), (0, 1), (1, 1)]
F32_TRAVEL_LIMIT = 4096


def _chip_exchange_plan(ins, outs, bufs, send_sems, recv_sems, local_sems):
    n = len(ins)
    x, y, c = _position()
    my_q = 2 * x + y
    peers = [(x ^ fx, y ^ fy) for fx, fy in _CHIP_FLIPS]
    own = [outs[k].at[my_q] for k in range(n)]

    def sends():
        return [pltpu.make_async_remote_copy(
            src_ref=ins[k].at[2 * px + py], dst_ref=outs[k].at[my_q], send_sem=send_sems.at[k, f],
            recv_sem=recv_sems.at[k, f], device_id=(px, py, c), device_id_type=MESH)
            for f, (px, py) in enumerate(peers) for k in range(n)]

    def start():
        for cp in sends():
            cp.start()
        _local_copy([ins[k].at[my_q] for k in range(n)], own, bufs, local_sems)

    def finish():
        for f, (px, py) in enumerate(peers):
            for k in range(n):
                pltpu.make_async_remote_copy(
                    src_ref=ins[k].at[my_q], dst_ref=outs[k].at[2 * px + py], send_sem=send_sems.at[k, f],
                    recv_sem=recv_sems.at[k, f], device_id=(px, py, c), device_id_type=MESH).wait_recv()
        for cp in sends():
            cp.wait_send()
        for cp in _local_stores(own, bufs, local_sems):
            cp.wait()

    return start, finish


def _pair_sum(parts, sib):
    n = len(parts)

    def body(core_ref, *refs):
        for k in range(n):
            refs[2 * n + k][0] = (refs[k][0, 0].astype(F32) + refs[n + k][0].astype(F32)).astype(BF16)

    own = [pl.BlockSpec((1, 1) + p.shape[1:], lambda q, core: (q, core[0], 0, 0)) for p in parts]
    one = [pl.BlockSpec((1,) + p.shape[1:], lambda q, core: (q, 0, 0)) for p in parts]
    return pl.pallas_call(
        body, name="pair_sum",
        grid_spec=pltpu.PrefetchScalarGridSpec(num_scalar_prefetch=1, grid=(N_CHIPS,), in_specs=own + one,
                                               out_specs=one),
        out_shape=[jax.ShapeDtypeStruct(s.shape, BF16) for s in sib],
        compiler_params=pltpu.CompilerParams(dimension_semantics=("parallel",)),
    )(lax.axis_index("c").astype(jnp.int32).reshape(1), *[p.reshape((N_CHIPS, 2) + p.shape[1:]) for p in parts], *sib)


_VMEM = pl.BlockSpec(memory_space=pltpu.VMEM)


def _reduce_out_shapes(parts, vals):
    return ([jax.ShapeDtypeStruct((N_CHIPS,) + p.shape[1:], p.dtype) for p in parts]
            + [jax.ShapeDtypeStruct(v.shape, v.dtype) for v in vals])


def _reduce_scratch(parts, vals):
    n, ns = len(parts), len(vals)
    quarter = [pltpu.VMEM((N_CHIPS,) + p.shape[1:], p.dtype) for p in parts]
    dma = pltpu.SemaphoreType.DMA
    travel = [BF16 if v.size > F32_TRAVEL_LIMIT else v.dtype for v in vals]
    return (quarter * 3 + [pltpu.VMEM(v.shape, v.dtype) for v in vals]
            + [pltpu.VMEM(v.shape, t) for v, t in zip(vals, travel)]
            + [pltpu.VMEM((N_CHIPS,) + v.shape, t) for v, t in zip(vals, travel)]
            + [dma((max(n, 1), N_CHIPS)), dma((max(n, 1), N_CHIPS)), dma((max(ns, 1),)), dma((max(ns, 1),)),
               dma((max(n, 1), 3)), dma((max(n, 1), 3)), dma((max(ns, 1), 3)), dma((max(ns, 1), 3))])


def _reduce_plan(p_in, v_in, p_out, v_out, scratch):
    n, ns = len(p_in), len(v_in)
    p_sib, p_sum, p_all = scratch[:n], scratch[n:2 * n], scratch[2 * n:3 * n]
    v_sib, v_sum, v_all = (scratch[3 * n + j * ns:3 * n + (j + 1) * ns] for j in range(3))
    p1_send, p1_recv, v1_send, v1_recv, p3_send, p3_recv, v3_send, v3_recv = scratch[3 * n + 3 * ns:]
    x, y, c = _position()
    my_q = 2 * x + y
    peers = [(x ^ fx, y ^ fy) for fx, fy in _CHIP_FLIPS]

    def to_sibling(src, dst, send, recv):
        return pltpu.make_async_remote_copy(src_ref=src, dst_ref=dst, send_sem=send, recv_sem=recv,
                                            device_id=(x, y, 1 - c), device_id_type=MESH)

    def level1():
        cps = [to_sibling(p_in[k].at[2 * q + 1 - c], p_sib[k].at[q], p1_send.at[k, q], p1_recv.at[k, q])
               for k in range(n) for q in range(N_CHIPS)]
        return cps + [to_sibling(v_in[k], v_sib[k], v1_send.at[k], v1_recv.at[k]) for k in range(ns)]

    def to_chip(f, src, dst, send, recv):
        px, py = peers[f]
        return pltpu.make_async_remote_copy(src_ref=src, dst_ref=dst, send_sem=send, recv_sem=recv,
                                            device_id=(px, py, c), device_id_type=MESH)

    def level2(sending):
        cps = []
        for f, (px, py) in enumerate(peers):
            their_q = 2 * px + py
            for k in range(n):
                src, dst = (p_sum[k].at[their_q], p_all[k].at[my_q]) if sending else (
                    p_sum[k].at[my_q], p_all[k].at[their_q])
                cps.append(to_chip(f, src, dst, p3_send.at[k, f], p3_recv.at[k, f]))
            for k in range(ns):
                dst = v_all[k].at[my_q] if sending else v_all[k].at[their_q]
                cps.append(to_chip(f, v_sum[k], dst, v3_send.at[k, f], v3_recv.at[k, f]))
        return cps

    def start():
        for cp in level1():
            cp.start()

    def middle():
        for cp in level1():
            cp.wait_recv()
        for k in range(n):
            for q in range(N_CHIPS):
                p_sum[k][q] = (p_in[k][2 * q + c].astype(F32) + p_sib[k][q].astype(F32)).astype(p_sum[k].dtype)
        for k in range(ns):
            v_sum[k][...] = (v_in[k][...] + v_sib[k][...]).astype(v_sum[k].dtype)
        for cp in level2(True):
            cp.start()
        for k in range(n):
            p_all[k][my_q] = p_sum[k][my_q]
        for k in range(ns):
            v_all[k][my_q] = v_sum[k][...]

    def finish():
        for cp in level2(False):
            cp.wait_recv()
        for k in range(n):
            p_out[k][...] = p_all[k][...]
        for k in range(ns):
            total = v_all[k][0].astype(F32)
            for q in range(1, N_CHIPS):
                total = total + v_all[k][q].astype(F32)
            v_out[k][...] = total
        for cp in level1() + level2(True):
            cp.wait_send()

    return start, middle, finish


def _adamw_math(w, g, m, v):
    m = ADAM_B1 * m + (1.0 - ADAM_B1) * g
    v = ADAM_B2 * v + (1.0 - ADAM_B2) * (g * g)
    m_hat = m / (1.0 - ADAM_B1 ** ADAM_STEP)
    v_hat = v / (1.0 - ADAM_B2 ** ADAM_STEP)
    delta = -ADAM_LR * (m_hat / (jnp.sqrt(v_hat) + ADAM_EPS) + ADAM_WD * w)
    return delta, m, v


def _adamw_shards(parts, ws, ms, vs, steps, name):
    n = len(ws)

    def body(*refs):
        ins, outs = refs[:4 * n], refs[4 * n:]
        for k in range(n):
            p_ref, w_ref, m_ref, v_ref = ins[4 * k:4 * k + 4]
            g = p_ref[0].astype(F32)
            for j in range(1, N_CHIPS):
                g = g + p_ref[j].astype(F32)
            d, mn, vn = _adamw_math(w_ref[...], g, m_ref[...], v_ref[...])
            for o, val in zip(outs[4 * k:4 * k + 4], (g, d, mn, vn)):
                o[...] = val

    in_specs, out_specs, out_shape, operands = [], [], [], []
    for p, w, m, v in zip(parts, ws, ms, vs):
        R, Cc = w.shape
        blk = pl.BlockSpec((R // steps, Cc), lambda i: (i, 0))
        in_specs += [pl.BlockSpec((N_CHIPS, R // steps, Cc), lambda i: (0, i, 0)), blk, blk, blk]
        out_specs += [blk] * 4
        out_shape += [jax.ShapeDtypeStruct((R, Cc), F32)] * 4
        operands += [p, w, m, v]
    res = pl.pallas_call(
        body, name=name, grid=(steps,), in_specs=in_specs, out_specs=out_specs, out_shape=out_shape,
        compiler_params=pltpu.CompilerParams(dimension_semantics=("parallel",), vmem_limit_bytes=VMEM_LIMIT),
    )(*operands)
    return [res[4 * k:4 * k + 4] for k in range(n)]


def _adamw_small(grads, ws, ms, vs):
    n = len(grads)
    vm = pl.BlockSpec(memory_space=pltpu.VMEM)

    def body(*refs):
        g_in, w_in, m_in, v_in = (refs[k * n:(k + 1) * n] for k in range(4))
        g_out, d_out, m_out, v_out = (refs[(4 + k) * n:(5 + k) * n] for k in range(4))
        for k in range(n):
            g = g_in[k][...]
            d, mn, vn = _adamw_math(w_in[k][...], g, m_in[k][...], v_in[k][...])
            g_out[k][...] = g
            d_out[k][...] = d
            m_out[k][...] = mn
            v_out[k][...] = vn

    shapes = [jax.ShapeDtypeStruct(g.shape, F32) for g in grads]
    return pl.pallas_call(
        body, name="adamw_small", out_shape=shapes * 4, in_specs=[vm] * (4 * n), out_specs=[vm] * (4 * n),
    )(*grads, *ws, *ms, *vs)


def _shard_cols(full):
    R, Ct = full.shape
    return jnp.transpose(full.reshape(R, N_DEV, Ct // N_DEV), (1, 0, 2))


_COL_SHARDED = ("w_in", "w_gate", "w_up")
_BIG = ("w_in", "w_out", "w_gate", "w_up", "w_down")


def _rows(nm, p):
    return p[0].T if nm in _COL_SHARDED else p[0]


def _step(args, ts_mix_fwd, ts_ffn, ts_mix_bwd, tm_grad):
    (x, g_mix, w_in, b_in, w_dw, b_dw, ln_g, ln_b, w_pool, s_pool, w_out, g_ffn, w_gate, w_up, w_down, g_final,
     loss_target) = args[:17]
    names = ["g_mix", "w_in", "b_in", "w_dw", "b_dw", "ln_g", "ln_b", "w_pool", "s_pool", "w_out", "g_ffn",
             "w_gate", "w_up", "w_down", "g_final"]
    weights = dict(zip(names, args[1:16]))
    moms = dict(zip(names, args[17:32]))
    vars_ = dict(zip(names, args[32:47]))

    S, D = x.shape[1], x.shape[2]
    x2 = x.reshape(S, D)
    tgt2 = loss_target.reshape(S, D)

    shard = lambda nm: _rows(nm, weights[nm])
    w_pool_b = w_pool[0].astype(BF16)
    dw_rows = -(-CONV_WIDTH // SUBLANES) * SUBLANES
    dw_shard = jnp.pad(w_dw[0], ((0, dw_rows - CONV_WIDTH), (0, LANES - w_dw.shape[2])))

    a, gate, v, m, y, h1, xn, w_dw_f, g_in, g_out, _, g_gate, g_up, g_down = _mix_fwd(
        x2, g_mix, b_in, b_dw, ln_g, ln_b, w_pool_b, s_pool, [shard("w_in"), shard("w_out"), dw_shard],
        [shard("w_gate"), shard("w_up"), shard("w_down")], ts_mix_fwd)
    wt_in, w_out_f = g_in.reshape(-1, D), g_out.reshape(-1, D)
    wt_gate, wt_up, w_down_f = g_gate.reshape(-1, D), g_up.reshape(-1, D), g_down.reshape(-1, D)
    Fd = wt_gate.shape[0]
    f_chunks = [1024] * (Fd // 1024) + ([Fd % 1024] if Fd % 1024 else [])
    dh1, hn, act, dgt, dup, dh2, loss_p, dg_final, dg_ffn = _ffn(
        h1, tgt2, g_ffn, g_final.reshape(1, D), wt_gate, wt_up, w_down_f, ts_ffn, f_chunks)

    by_shard = lambda g: g.reshape(N_DEV, -1, D)
    ffn_grads, ffn_sib = _grad_ffn([dgt, dup, act], [hn, hn, dh2], tm_grad)
    pair_sums = _pair_sum([by_shard(g) for g in ffn_grads], list(ffn_sib))
    (dx, dz, dh1b, dg_mix, db_in, dw_dw, db_dw, dln_g, dln_b, dw_pool, ds_pool, r_gate, r_up, r_down) = _mix_bwd(
        dh1, x2, a, gate, v, m, g_mix, wt_in, w_dw_f, ln_g, ln_b, w_pool_b, s_pool, w_out_f, pair_sums, ts_mix_bwd)

    small_names = ["g_mix", "b_in", "b_dw", "ln_g", "ln_b", "w_pool", "s_pool", "g_ffn", "g_final"]
    small_shape = lambda p: p.reshape(-1, p.shape[-1])
    partial = [dg_mix, db_in, db_dw, dln_g, dln_b, dw_pool.reshape(-1, POOL_GROUP), ds_pool, dg_ffn, dg_final, loss_p]
    r_out, r_dw, r_in, summed = _grad_mix(y, dh1b, dz, xn, _shard_cols(dw_dw[0:CONV_WIDTH]), partial)

    big = {}
    recv = dict(zip(_BIG, [r_in, r_out, r_gate, r_up, r_down]))
    for call, group in (("adamw_ffn", ("w_gate", "w_up", "w_down")), ("adamw_mix", ("w_in", "w_out"))):
        results = _adamw_shards([recv[nm] for nm in group], [_rows(nm, weights[nm]) for nm in group],
                                [_rows(nm, moms[nm]) for nm in group], [_rows(nm, vars_[nm]) for nm in group], 2, call)
        for nm, res in zip(group, results):
            big[nm] = [o.T if nm in _COL_SHARDED else o for o in res]
    big["w_dw"], = _adamw_shards([r_dw], [w_dw[0]], [moms["w_dw"][0]], [vars_["w_dw"][0]], 1, "adamw_w_dw")

    sm = _adamw_small(summed[:-1], [small_shape(weights[nm]) for nm in small_names],
                      [small_shape(moms[nm]) for nm in small_names], [small_shape(vars_[nm]) for nm in small_names])
    n_small = len(small_names)

    def result(kind, nm):
        if nm in big:
            return big[nm][kind].reshape(weights[nm].shape)
        return sm[kind * n_small + small_names.index(nm)].reshape(weights[nm].shape)

    loss = summed[-1][0, 0]
    out = [loss, dx.reshape(x.shape)]
    for kind in range(4):
        out += [result(kind, nm) for nm in names]
    return tuple(out)


def kernel(x, g_mix, w_in, b_in, w_dw, b_dw, ln_g, ln_b, w_pool, s_pool, w_out, g_ffn, w_gate, w_up, w_down, g_final, loss_target, m_g_mix, m_w_in, m_b_in, m_w_dw, m_b_dw, m_ln_g, m_ln_b, m_w_pool, m_s_pool, m_w_out, m_g_ffn, m_w_gate, m_w_up, m_w_down, m_g_final, v_g_mix, v_w_in, v_b_in, v_w_dw, v_b_dw, v_ln_g, v_ln_b, v_w_pool, v_s_pool, v_w_out, v_g_ffn, v_w_gate, v_w_up, v_w_down, v_g_final):
    args = (x, g_mix, w_in, b_in, w_dw, b_dw, ln_g, ln_b, w_pool, s_pool, w_out, g_ffn, w_gate, w_up, w_down, g_final, loss_target, m_g_mix, m_w_in, m_b_in, m_w_dw, m_b_dw, m_ln_g, m_ln_b, m_w_pool, m_s_pool, m_w_out, m_g_ffn, m_w_gate, m_w_up, m_w_down, m_g_final, v_g_mix, v_w_in, v_b_in, v_w_dw, v_b_dw, v_ln_g, v_ln_b, v_w_pool, v_s_pool, v_w_out, v_g_ffn, v_w_gate, v_w_up, v_w_down, v_g_final)
    return _step(args, ts_mix_fwd=512, ts_ffn=256, ts_mix_bwd=512, tm_grad=256)
```

```python
import jax
import jax.numpy as jnp
from jax import lax
from jax.experimental import pallas as pl
from jax.experimental.pallas import tpu as pltpu

F32 = jnp.float32
BF16 = jnp.bfloat16
MESH = pl.DeviceIdType.MESH
N_DEV = 8

C_CONV = 512
CONV_WIDTH = 31
POOL_WINDOWS = (2, 4, 8, 16)
POOL_GROUP = 128
RMS_EPS = 1e-6
LN_EPS = 1e-5

ADAM_LR = 0.001
ADAM_B1 = 0.9
ADAM_B2 = 0.999
ADAM_EPS = 1e-08
ADAM_WD = 0.01
ADAM_STEP = 10

HALO = 32
SUBLANES = 8
LANES = 128
CONV_ROWS = 64
VMEM_LIMIT = 56 * 1024 * 1024


def _dot(a, b):
    return jnp.dot(a, b, preferred_element_type=F32)


def _dot_nt(a, b):
    return lax.dot_general(a, b, (((1,), (1,)), ((), ())), preferred_element_type=F32)


def _dot_tn(a, b):
    return lax.dot_general(a, b, (((0,), (0,)), ((), ())), preferred_element_type=F32)


def _mean_last(v):
    return jnp.mean(v, axis=-1, keepdims=True)


def _full(shape):
    nd = len(shape)
    return pl.BlockSpec(shape, lambda *_: (0,) * nd)


def _full1(shape):
    nd = len(shape)
    return pl.BlockSpec(shape, lambda *_: (0,) * nd, pipeline_mode=pl.Buffered(1))


def _shifted_copies(sh_ref, rows):
    for s in range(1, SUBLANES):
        sh_ref[s, 0:rows, :] = sh_ref[0, s:s + rows, :]


def _tap(sh_ref, off, r0, rows):
    q, s = divmod(off, SUBLANES)
    return sh_ref[s, pl.ds(r0 + q * SUBLANES, rows), :]


def _window_sums(src, bufs, rows, backward):
    assert all(w == 2 << g for g, w in enumerate(POOL_WINDOWS))
    n = len(POOL_WINDOWS)
    out = []
    for level in range(n):
        dst, shift = bufs[level % 2], 1 << level
        lanes = slice(level * POOL_GROUP, n * POOL_GROUP)
        lo, hi = (SUBLANES * (level + 1), rows) if backward else (0, rows - SUBLANES * (level + 1))
        other = slice(lo - shift, hi - shift) if backward else slice(lo + shift, hi + shift)
        dst[lo:hi, lanes] = src[lo:hi, lanes] + src[other, lanes]
        src = dst
        out.append(dst)
    return out


def _mix_fwd(x, g_mix, b_in, b_dw, ln_g, ln_b, w_pool, s_pool, mix_shards, ag_shards, ts):
    S, D = x.shape
    d_in = mix_shards[0].shape[0] * N_DEV
    C = C_CONV
    nt = S // ts
    nrb = ts // CONV_ROWS
    n_ag = len(ag_shards)
    relay_step = nt // 2
    mix_dtypes = [BF16, BF16, F32]

    def body(x_ref, g_ref, bin_ref, bdw_ref, lng_ref, lnb_ref, wp_ref, sp_ref, *rest):
        mx_in, ag_in, rest = rest[:3], rest[3:3 + n_ag], rest[3 + n_ag:]
        a_ref, gate_ref, v_ref, m_ref, y_ref, h1_ref, xn_ref, wdw_ref = rest[:8]
        mx_out, ag_out, rest = rest[8:11], rest[11:11 + n_ag], rest[11 + n_ag:]
        ush, pbuf, pa, pb, win_ref, wout_ref, gdw, load_sems = rest[:8]
        rest = rest[8:]
        (ma_stage, ma_bufs, ma_sems), rest = (rest[:2], rest[2:4], rest[4:7]), rest[7:]
        (mb_stage, mb_bufs, mb_sems), rest = (rest[:1], rest[1:2], rest[2:5]), rest[5:]
        ag_stage, ag_bufs, ag_sems = rest[:n_ag], rest[n_ag:2 * n_ag], rest[2 * n_ag:]
        i = pl.program_id(0)
        ma_start, ma_relay, ma_pass_on, ma_finish = _gather_plan(
            mx_in[0::2], mx_out[0::2], ma_stage, ma_bufs, *ma_sems)
        mb_start, mb_relay, mb_pass_on, mb_finish = _gather_plan(
            mx_in[1:2], mx_out[1:2], mb_stage, mb_bufs, *mb_sems)
        ag_start, ag_relay, ag_pass_on, ag_finish = _gather_plan(ag_in, ag_out, ag_stage, ag_bufs, *ag_sems)

        def load_weight(k, dst_ref):
            rows = mx_out[k].shape[1]
            loads = [pltpu.make_async_copy(mx_out[k].at[j], dst_ref.at[pl.ds(j * rows, rows), :],
                                           load_sems.at[N_DEV * k + j]) for j in range(N_DEV)]
            for cp in loads:
                cp.start()
            for cp in loads:
                cp.wait()

        @pl.when(i == 0)
        def _():
            ma_start()
            mb_start()
            ma_relay()
            mb_relay()
            ag_start()
            ma_pass_on()
            ma_finish()
            load_weight(0, win_ref)
            dw_load = pltpu.make_async_copy(mx_out[2], gdw, load_sems.at[2 * N_DEV])
            dw_load.start()
            dw_load.wait()
            first_half = lax.broadcasted_iota(jnp.int32, gdw.shape[1:], 1) < C // N_DEV
            for p in range(N_DEV // 2):
                wdw_ref[:, LANES * p:LANES * (p + 1)] = jnp.where(
                    first_half, gdw[2 * p], pltpu.roll(gdw[2 * p + 1], C // N_DEV, axis=1))
            ush[0, 0:HALO, :] = jnp.zeros((HALO, C), F32)
            pbuf[0:HALO, :] = jnp.zeros((HALO, C), F32)

        @pl.when(i == relay_step)
        def _():
            ag_relay()

        @pl.when(i == nt - 1)
        def _():
            ag_pass_on()

        x = x_ref[...]
        r1 = lax.rsqrt(_mean_last(x * x) + RMS_EPS)
        xn = (x * r1 * g_ref[...]).astype(BF16)
        xn_ref[...] = xn
        z = _dot_nt(xn, win_ref[...]) + bin_ref[...]
        a = z[:, 0:C]
        gate = z[:, C:2 * C]
        a_ref[...] = a
        gate_ref[...] = gate
        ush[0, HALO:HALO + ts, :] = a * jax.nn.sigmoid(gate)
        pbuf[HALO:HALO + ts, :] = z[:, 2 * C:]

        _shifted_copies(ush, ts + HALO - SUBLANES)

        def conv_block(rb, carry):
            r0 = pl.multiple_of(rb * CONV_ROWS, CONV_ROWS)
            acc = jnp.zeros((CONV_ROWS, C), F32)
            for k in range(CONV_WIDTH):
                acc = acc + wdw_ref[k:k + 1, :] * _tap(ush, HALO - (CONV_WIDTH - 1) + k, r0, CONV_ROWS)
            v_ref[pl.ds(r0, CONV_ROWS), :] = acc + bdw_ref[...]
            return carry

        lax.fori_loop(0, nrb, conv_block, 0)

        v = v_ref[...]
        mu = _mean_last(v)
        xc = v - mu
        rstd = lax.rsqrt(_mean_last(xc * xc) + LN_EPS)
        ln = xc * rstd * lng_ref[...] + lnb_ref[...]
        y_ref[:, 0:C] = (ln * jax.nn.sigmoid(ln)).astype(BF16)

        sums = _window_sums(pbuf, (pa, pb), ts + HALO, backward=True)
        row = lax.broadcasted_iota(jnp.int32, (ts, 1), 0) + i * ts
        for gi, w in enumerate(POOL_WINDOWS):
            lanes = slice(gi * POOL_GROUP, (gi + 1) * POOL_GROUP)
            seg = pbuf[HALO:HALO + ts, lanes]
            ws = sums[gi][HALO:HALO + ts, lanes]
            cnt = jnp.minimum(row + 1, w).astype(F32)
            m = (ws / cnt - seg).astype(BF16)
            m_ref[:, lanes] = m
            ypre = _dot(m, wp_ref[gi])
            y_ref[:, C + gi * POOL_GROUP:C + (gi + 1) * POOL_GROUP] = (ypre * sp_ref[:, lanes]).astype(BF16)

        @pl.when(i == 0)
        def _():
            mb_pass_on()
            mb_finish()
            load_weight(1, wout_ref)

        h1_ref[...] = x + _dot(y_ref[...], wout_ref[...])

        ush[0, 0:HALO, :] = ush[0, ts:ts + HALO, :]
        pbuf[0:HALO, :] = pbuf[ts:ts + HALO, :]

        @pl.when(i == nt - 1)
        def _():
            ag_finish()

    tile = lambda w, dt: (pl.BlockSpec((ts, w), lambda i: (i, 0)), jax.ShapeDtypeStruct((S, w), dt))
    wdw_rows = mix_shards[2].shape[0]
    outs = [tile(C, F32), tile(C, F32), tile(C, F32), tile(C, BF16), tile(D, BF16), tile(D, F32), tile(D, BF16),
            (_full((wdw_rows, C)), jax.ShapeDtypeStruct((wdw_rows, C), F32))]
    return pl.pallas_call(
        body, name="mix_fwd", grid=(nt,),
        in_specs=[pl.BlockSpec((ts, D), lambda i: (i, 0)), _full((1, D)), _full((1, d_in)), _full((1, C)),
                  _full((1, C)), _full((1, C)), _full(w_pool.shape), _full((1, C))] + [_HBM] * (3 + n_ag),
        out_specs=[o[0] for o in outs] + [_HBM] * (3 + n_ag),
        out_shape=[o[1] for o in outs] + _gather_out_shapes(mix_shards, mix_dtypes)
        + _gather_out_shapes(ag_shards, [BF16] * n_ag),
        scratch_shapes=[pltpu.VMEM((SUBLANES, ts + HALO, C), F32)] + [pltpu.VMEM((ts + HALO, C), F32)] * 3
        + [pltpu.VMEM((d_in, D), BF16), pltpu.VMEM((D, D), BF16),
                        pltpu.VMEM((N_DEV,) + mix_shards[2].shape, F32), pltpu.SemaphoreType.DMA((2 * N_DEV + 1,))]
        + _gather_scratch(mix_shards[0::2], mix_dtypes[0::2]) + _gather_scratch(mix_shards[1:2], mix_dtypes[1:2])
        + _gather_scratch(ag_shards, [BF16] * n_ag),
        compiler_params=pltpu.CompilerParams(dimension_semantics=("arbitrary",), vmem_limit_bytes=VMEM_LIMIT),
    )(x, g_mix, b_in, b_dw, ln_g, ln_b, w_pool, s_pool, *mix_shards, *ag_shards)


def _ffn(h1, target, g_ffn, g_final, w_gate, w_up, w_down, ts, f_chunks):
    S, D = h1.shape
    Fd = w_gate.shape[0]
    nt = S // ts
    bounds = []
    lo = 0
    for n in f_chunks:
        bounds.append((lo, lo + n))
        lo += n
    assert lo == Fd

    def body(h1_ref, tgt_ref, gf_ref, gl_ref, wg_ref, wu_ref, wd_ref,
             dh1_ref, hn_ref, act_ref, dgt_ref, dup_ref, dh2_ref, loss_ref, dgl_ref, dgf_ref, gt_s, up_s):
        i = pl.program_id(0)

        @pl.when(i == 0)
        def _():
            loss_ref[...] = jnp.zeros_like(loss_ref)
            dgl_ref[...] = jnp.zeros_like(dgl_ref)
            dgf_ref[...] = jnp.zeros_like(dgf_ref)

        h1 = h1_ref[...]
        r2 = lax.rsqrt(_mean_last(h1 * h1) + RMS_EPS)
        hhat = h1 * r2
        hn = (hhat * gf_ref[...]).astype(BF16)
        hn_ref[...] = hn
        h2 = h1
        for lo, hi in bounds:
            gt = _dot_nt(hn, wg_ref[lo:hi, :])
            up = _dot_nt(hn, wu_ref[lo:hi, :])
            gt_s[:, lo:hi] = gt
            up_s[:, lo:hi] = up
            act = (gt * jax.nn.sigmoid(gt) * up).astype(BF16)
            act_ref[:, lo:hi] = act
            h2 = h2 + _dot(act, wd_ref[lo:hi, :])

        r3 = lax.rsqrt(_mean_last(h2 * h2) + RMS_EPS)
        n3 = h2 * r3
        gl = gl_ref[...]
        diff = n3 * gl - tgt_ref[...]
        loss_ref[...] += jnp.sum(0.5 * jnp.sum(diff * diff, axis=-1, keepdims=True) / D, axis=0, keepdims=True)
        dout = diff / D
        dgl_ref[...] += jnp.sum(dout * n3, axis=0, keepdims=True)
        dn = dout * gl
        dh2 = r3 * (dn - n3 * _mean_last(dn * n3))
        dh2b = dh2.astype(BF16)
        dh2_ref[...] = dh2b

        dhn = jnp.zeros((ts, D), F32)
        for lo, hi in bounds:
            gt = gt_s[:, lo:hi]
            up = up_s[:, lo:hi]
            sg = jax.nn.sigmoid(gt)
            dact = _dot_nt(dh2b, wd_ref[lo:hi, :])
            dgt = (dact * up * (sg * (1.0 + gt * (1.0 - sg)))).astype(BF16)
            dup = (dact * (gt * sg)).astype(BF16)
            dgt_ref[:, lo:hi] = dgt
            dup_ref[:, lo:hi] = dup
            dhn = dhn + _dot(dgt, wg_ref[lo:hi, :]) + _dot(dup, wu_ref[lo:hi, :])

        dgf_ref[...] += jnp.sum(dhn * hhat, axis=0, keepdims=True)
        dnn = dhn * gf_ref[...]
        dh1_ref[...] = dh2 + r2 * (dnn - hhat * _mean_last(dnn * hhat))

    tile = lambda w, dt: (pl.BlockSpec((ts, w), lambda i: (i, 0)), jax.ShapeDtypeStruct((S, w), dt))
    acc = lambda w: (_full((1, w)), jax.ShapeDtypeStruct((1, w), F32))
    outs = [tile(D, F32), tile(D, BF16), tile(Fd, BF16), tile(Fd, BF16), tile(Fd, BF16), tile(D, BF16),
            acc(LANES), acc(D), acc(D)]
    return pl.pallas_call(
        body, name="ffn_fwd_bwd", grid=(nt,),
        in_specs=[pl.BlockSpec((ts, D), lambda i: (i, 0)), pl.BlockSpec((ts, D), lambda i: (i, 0)),
                  _full((1, D)), _full((1, D)), _full1((Fd, D)), _full1((Fd, D)), _full1((Fd, D))],
        out_specs=[o[0] for o in outs], out_shape=[o[1] for o in outs],
        scratch_shapes=[pltpu.VMEM((ts, Fd), F32), pltpu.VMEM((ts, Fd), F32)],
        compiler_params=pltpu.CompilerParams(dimension_semantics=("arbitrary",), vmem_limit_bytes=VMEM_LIMIT),
    )(h1, target, g_ffn, g_final, w_gate, w_up, w_down)


def _mix_bwd(dh1, x, a, gate, v, m, g_mix, w_in, w_dw, ln_g, ln_b, w_pool, s_pool, w_out, rs_parts, ts):
    S, D = x.shape
    n_rs = len(rs_parts)
    d_in = w_in.shape[0]
    C = C_CONV
    nt = S // ts
    nrb = ts // CONV_ROWS
    wrows =((CONV_WIDTH + SUBLANES - 1) // SUBLANES) * SUBLANES

    def body(dh1_ref, x_ref, a_ref, gate_ref, v_ref, m_ref, g_ref, win_ref, wdw_ref, lng_ref,
             lnb_ref, wp_ref, sp_ref, wout_ref, *rest):
        rs_in, rest = rest[:n_rs], rest[n_rs:]
        (dx_ref, dz_ref, dh1b_ref, dgm_ref, dbin_ref, dwdw_ref, dbdw_ref, dlng_ref, dlnb_ref, dwp_ref,
         dsp_ref) = rest[:11]
        rs_out, rest = rest[11:11 + n_rs], rest[11 + n_rs:]
        dvsh, dqbuf, qa, qb, du_s, dm_s = rest[:6]
        rs_bufs, (send_sems, recv_sems, local_sems) = rest[6:6 + n_rs], rest[6 + n_rs:]
        i = pl.program_id(0)
        t = nt - 1 - i
        rs_start, rs_finish = _chip_exchange_plan(rs_in, rs_out, rs_bufs, send_sems, recv_sems, local_sems)

        @pl.when(i == 0)
        def _():
            rs_start()
            dvsh[0, ts:ts + HALO, :] = jnp.zeros((HALO, C), F32)
            dqbuf[ts:ts + HALO, :] = jnp.zeros((HALO, C), F32)
            for r in (dgm_ref, dbin_ref, dwdw_ref, dbdw_ref, dlng_ref, dlnb_ref, dwp_ref, dsp_ref):
                r[...] = jnp.zeros_like(r)

        dh1 = dh1_ref[...]
        dh1b = dh1.astype(BF16)
        dh1b_ref[...] = dh1b
        dy = _dot_nt(dh1b, wout_ref[...])

        v = v_ref[...]
        mu = _mean_last(v)
        xc = v - mu
        rstd = lax.rsqrt(_mean_last(xc * xc) + LN_EPS)
        vhat = xc * rstd
        lng = lng_ref[...]
        ln = vhat * lng + lnb_ref[...]
        sg = jax.nn.sigmoid(ln)
        dln = dy[:, 0:C] * (sg * (1.0 + ln * (1.0 - sg)))
        dlng_ref[...] += jnp.sum(dln * vhat, axis=0, keepdims=True)
        dlnb_ref[...] += jnp.sum(dln, axis=0, keepdims=True)
        dvh = dln * lng
        dv = rstd * (dvh - _mean_last(dvh) - vhat * _mean_last(dvh * vhat))
        dbdw_ref[...] += jnp.sum(dv, axis=0, keepdims=True)
        dvsh[0, 0:ts, :] = dv
        _shifted_copies(dvsh, ts + HALO - SUBLANES)

        def conv_block(rb, carry):
            r0 = pl.multiple_of(rb * CONV_ROWS, CONV_ROWS)
            acc = jnp.zeros((CONV_ROWS, C), F32)
            for k in range(CONV_WIDTH):
                acc = acc + wdw_ref[k:k + 1, :] * _tap(dvsh, CONV_WIDTH - 1 - k, r0, CONV_ROWS)
            du_s[pl.ds(r0, CONV_ROWS), :] = acc
            return carry

        lax.fori_loop(0, nrb, conv_block, 0)

        a = a_ref[...]
        sgate = jax.nn.sigmoid(gate_ref[...])
        u = a * sgate
        for k in range(CONV_WIDTH):
            q, s = divmod(CONV_WIDTH - 1 - k, SUBLANES)
            prod = u * dvsh[s, q * SUBLANES:q * SUBLANES + ts, :]
            dwdw_ref[k:k + 1, :] += jnp.sum(prod, axis=0, keepdims=True)

        du = du_s[...]
        da = du * sgate
        dgate = du * a * sgate * (1.0 - sgate)
        dz_ref[:, 0:C] = da.astype(BF16)
        dz_ref[:, C:2 * C] = dgate.astype(BF16)
        dbin_ref[:, 0:C] += jnp.sum(da, axis=0, keepdims=True)
        dbin_ref[:, C:2 * C] += jnp.sum(dgate, axis=0, keepdims=True)

        row = lax.broadcasted_iota(jnp.int32, (ts, 1), 0) + t * ts
        for gi, w in enumerate(POOL_WINDOWS):
            lanes = slice(gi * POOL_GROUP, (gi + 1) * POOL_GROUP)
            dyp = dy[:, C + gi * POOL_GROUP:C + (gi + 1) * POOL_GROUP]
            mg = m_ref[:, lanes]
            ypre = _dot(mg, wp_ref[gi])
            dsp_ref[:, lanes] += jnp.sum(dyp * ypre, axis=0, keepdims=True)
            dyi = (dyp * sp_ref[:, lanes]).astype(BF16)
            dwp_ref[gi] += _dot_tn(mg, dyi)
            dm = _dot_nt(dyi, wp_ref[gi])
            cnt = jnp.minimum(row + 1, w).astype(F32)
            dqbuf[0:ts, lanes] = dm / cnt
            dm_s[:, lanes] = dm
        sums = _window_sums(dqbuf, (qa, qb), ts + HALO, backward=False)
        for gi in range(len(POOL_WINDOWS)):
            lanes = slice(gi * POOL_GROUP, (gi + 1) * POOL_GROUP)
            dp = sums[gi][0:ts, lanes] - dm_s[:, lanes]
            dz_ref[:, 2 * C + gi * POOL_GROUP:2 * C + (gi + 1) * POOL_GROUP] = dp.astype(BF16)
            dbin_ref[:, 2 * C + gi * POOL_GROUP:2 * C + (gi + 1) * POOL_GROUP] += jnp.sum(dp, axis=0, keepdims=True)

        dxn = _dot(dz_ref[...], win_ref[...])
        x = x_ref[...]
        r1 = lax.rsqrt(_mean_last(x * x) + RMS_EPS)
        xhat = x * r1
        dgm_ref[...] += jnp.sum(dxn * xhat, axis=0, keepdims=True)
        dnn = dxn * g_ref[...]
        dx_ref[...] = dh1 + r1 * (dnn - xhat * _mean_last(dnn * xhat))

        dvsh[0, ts:ts + HALO, :] = dvsh[0, 0:HALO, :]
        dqbuf[ts:ts + HALO, :] = dqbuf[0:HALO, :]

        @pl.when(i == nt - 1)
        def _():
            rs_finish()

    rev = lambda w: pl.BlockSpec((ts, w), lambda i: (nt - 1 - i, 0))
    acc = lambda shape: (_full(shape), jax.ShapeDtypeStruct(shape, F32))
    outs = [(rev(D), jax.ShapeDtypeStruct((S, D), F32)), (rev(d_in), jax.ShapeDtypeStruct((S, d_in), BF16)),
            (rev(D), jax.ShapeDtypeStruct((S, D), BF16)), acc((1, D)), acc((1, d_in)), acc((wrows, C)),
            acc((1, C)), acc((1, C)), acc((1, C)), acc(w_pool.shape), acc((1, C))]
    return pl.pallas_call(
        body, name="mix_bwd", grid=(nt,),
        in_specs=[rev(D), rev(D), rev(C), rev(C), rev(C), rev(C), _full((1, D)), _full((d_in, D)),
                  _full(w_dw.shape), _full((1, C)), _full((1, C)), _full(w_pool.shape), _full((1, C)),
                  _full((D, D))] + [_HBM] * n_rs,
        out_specs=[o[0] for o in outs] + [_HBM] * n_rs,
        out_shape=[o[1] for o in outs] + [jax.ShapeDtypeStruct(p.shape, p.dtype) for p in rs_parts],
        scratch_shapes=[pltpu.VMEM((SUBLANES, ts + HALO, C), F32)] + [pltpu.VMEM((ts + HALO, C), F32)] * 3
        + [pltpu.VMEM((ts, C), F32)] * 2
        + [pltpu.VMEM(p.shape[1:], p.dtype) for p in rs_parts] + _comm_sems(n_rs, 3),
        compiler_params=pltpu.CompilerParams(dimension_semantics=("arbitrary",), vmem_limit_bytes=VMEM_LIMIT),
    )(dh1, x, a, gate, v, m, g_mix, w_in, w_dw, ln_g, ln_b, w_pool, s_pool, w_out, *rs_parts)


def _grad_mix(y, dh1b, dz, xn, dw_dw, vals):
    S, D = y.shape
    d_in = dz.shape[1]
    r_out, r_in = D // N_DEV, d_in // N_DEV
    n_out, n_in = N_DEV // 2, N_DEV // 2
    steps = n_out + n_in
    ns = len(vals)
    dw_out = jax.ShapeDtypeStruct((N_DEV, r_out, D), BF16)
    dw_in = jax.ShapeDtypeStruct((N_DEV, r_in, D), BF16)
    s_scratch, o_scratch = _reduce_scratch([], vals), _reduce_scratch([dw_out, dw_dw], [])
    i_scratch = _reduce_scratch([dw_in], [])

    def body(y_ref, dh_ref, dz_ref, xn_ref, dwdw_ref, *rest):
        v_in, p_out, v_out = rest[:ns], rest[ns:ns + 3], rest[ns + 3:2 * ns + 3]
        rest = rest[2 * ns + 3:]
        dwo_v, dwi_v, rest = rest[0], rest[1], rest[2:]
        s_refs, rest = rest[:len(s_scratch)], rest[len(s_scratch):]
        o_refs, i_refs = rest[:len(o_scratch)], rest[len(o_scratch):]
        s_start, s_middle, s_finish = _reduce_plan((), v_in, (), v_out, s_refs)
        o_start, o_middle, o_finish = _reduce_plan((dwo_v, dwdw_ref), (), p_out[:2], (), o_refs)
        i_start, i_middle, i_finish = _reduce_plan((dwi_v,), (), p_out[2:], (), i_refs)
        s = pl.program_id(0)
        pl.when(s == 0)(s_start)
        pl.when(s == 1)(s_middle)

        @pl.when(s < n_out)
        def _():
            res = _dot_tn(y_ref[...], dh_ref[...]).astype(BF16)
            dwo_v[2 * s] = res[0:r_out]
            dwo_v[2 * s + 1] = res[r_out:]

        pl.when(s == n_out - 1)(s_finish)
        pl.when(s == n_out)(o_start)
        pl.when(s == n_out + 1)(o_middle)

        @pl.when(s >= n_out)
        def _():
            res = _dot_tn(dz_ref[...], xn_ref[...]).astype(BF16)
            dwi_v[2 * (s - n_out)] = res[0:r_in]
            dwi_v[2 * (s - n_out) + 1] = res[r_in:]

        @pl.when(s == steps - 1)
        def _():
            o_finish()
            i_start()
            i_middle()
            i_finish()

    late = lambda s: jnp.clip(s - n_out, 0, n_in - 1)
    res = pl.pallas_call(
        body, name="grad_w_mix", grid=(steps,),
        in_specs=[pl.BlockSpec((S, 2 * r_out), lambda s: (0, jnp.minimum(s, n_out - 1))), _full1((S, D)),
                  pl.BlockSpec((S, 2 * r_in), lambda s: (0, late(s))), _full1((S, D))] + [_VMEM] * (1 + ns),
        out_specs=[_VMEM] * (3 + ns),
        out_shape=_reduce_out_shapes([dw_out, dw_dw, dw_in], []) + _reduce_out_shapes([], vals),
        scratch_shapes=[pltpu.VMEM(dw_out.shape, BF16), pltpu.VMEM(dw_in.shape, BF16)] + s_scratch + o_scratch
        + i_scratch,
        compiler_params=pltpu.CompilerParams(dimension_semantics=("arbitrary",), vmem_limit_bytes=VMEM_LIMIT),
    )(y, dh1b, dz, xn, dw_dw, *vals)
    return res[0], res[1], res[2], res[3:]


def _grad_ffn(a_list, b_list, tm):
    n = len(a_list)
    S, M = a_list[0].shape
    N = b_list[0].shape[1]
    nb = M // tm
    steps = n * nb
    R = M // N_DEV
    b_unique = [b for k, b in enumerate(b_list) if all(b is not o for o in b_list[:k])]
    b_index = [[b is o for o in b_unique].index(True) for b in b_list]

    def body(*refs):
        a_refs, b_refs = refs[:n], refs[n:n + len(b_unique)]
        rest = refs[n + len(b_unique):]
        g_refs, sib_refs, (obuf, w_sems, send_sems, recv_sems) = rest[:n], rest[n:2 * n], rest[2 * n:]
        s = pl.program_id(0)
        slot = s % 2
        x, y, c = _position()

        def write_back(dst):
            return pltpu.make_async_copy(obuf.at[slot], dst, w_sems.at[slot])

        def exchange(k):
            return [pltpu.make_async_remote_copy(
                src_ref=g_refs[k].at[pl.ds(pl.multiple_of((2 * q + 1 - c) * R, SUBLANES * 2), R), :],
                dst_ref=sib_refs[k].at[q], send_sem=send_sems.at[k, q], recv_sem=recv_sems.at[k, q],
                device_id=(x, y, 1 - c), device_id_type=MESH) for q in range(N_CHIPS)]

        pl.when(s >= 2)(write_back(g_refs[0].at[pl.ds(0, tm), :]).wait)
        for k in range(n):
            @pl.when(jnp.logical_and(s >= k * nb, s < (k + 1) * nb))
            def _(k=k):
                obuf[slot] = _dot_tn(a_refs[k][...], b_refs[b_index[k]][...]).astype(BF16)
                row = pl.multiple_of((s - k * nb) * tm, tm)
                write_back(g_refs[k].at[pl.ds(row, tm), :]).start()
            if k < n - 1:
                @pl.when(s == (k + 1) * nb + 2)
                def _(k=k):
                    for cp in exchange(k):
                        cp.start()

        @pl.when(s == steps - 1)
        def _():
            write_back(g_refs[0].at[pl.ds(0, tm), :]).wait()
            pltpu.make_async_copy(obuf.at[1 - slot], g_refs[0].at[pl.ds(0, tm), :], w_sems.at[1 - slot]).wait()
            for cp in exchange(n - 1):
                cp.start()
            for k in range(n):
                for cp in exchange(k):
                    cp.wait()

    clamp = lambda k: (lambda s: (0, jnp.clip(s - k * nb, 0, nb - 1)))
    sems = pltpu.SemaphoreType.DMA((n, N_CHIPS))
    res = pl.pallas_call(
        body, name="grad_w_ffn", grid=(steps,),
        in_specs=[pl.BlockSpec((S, tm), clamp(k)) for k in range(n)] + [_full1((S, N))] * len(b_unique),
        out_specs=[_HBM] * (2 * n),
        out_shape=[jax.ShapeDtypeStruct((M, N), BF16)] * n + [jax.ShapeDtypeStruct((N_CHIPS, R, N), BF16)] * n,
        scratch_shapes=[pltpu.VMEM((2, tm, N), BF16), pltpu.SemaphoreType.DMA((2,)), sems, sems],
        compiler_params=pltpu.CompilerParams(dimension_semantics=("arbitrary",), vmem_limit_bytes=VMEM_LIMIT),
    )(*a_list, *b_unique)
    return res[:n], res[n:]


def _position():
    return lax.axis_index("x"), lax.axis_index("y"), lax.axis_index("c")


def _slot(px, py, pc):
    return 4 * px + 2 * py + pc


_HBM = pl.BlockSpec(memory_space=pl.ANY)


def _comm_sems(n, copies):
    return [pltpu.SemaphoreType.DMA((n, copies)), pltpu.SemaphoreType.DMA((n, copies)),
            pltpu.SemaphoreType.DMA((n, 2))]


def _local_copy(srcs, dsts, bufs, local_sems):
    n = len(srcs)
    loads = [pltpu.make_async_copy(srcs[k], bufs[k], local_sems.at[k, 0]) for k in range(n)]
    for cp in loads:
        cp.start()
    for cp in loads:
        cp.wait()
    stores = _local_stores(dsts, bufs, local_sems)
    for cp in stores:
        cp.start()
    return stores


def _local_stores(dsts, bufs, local_sems):
    return [pltpu.make_async_copy(bufs[k], dsts[k], local_sems.at[k, 1]) for k in range(len(dsts))]


def _gather_out_shapes(shards, dtypes):
    return [jax.ShapeDtypeStruct((N_DEV,) + s.shape, dt) for s, dt in zip(shards, dtypes)]


def _gather_scratch(shards, dtypes):
    return ([pltpu.VMEM(s.shape, s.dtype) for s in shards] + [pltpu.VMEM(s.shape, dt) for s, dt in zip(shards, dtypes)]
            + _comm_sems(len(shards), 7))


def _gather_plan(ins, outs, stage, bufs, send_sems, recv_sems, local_sems):
    n = len(ins)
    x, y, c = _position()
    me, sibling = (x, y, c), (x, y, 1 - c)
    na, nb, dg = (x ^ (1 - c), y ^ c), (x ^ c, y ^ (1 - c)), (1 - x, 1 - y)
    own = [outs[k].at[_slot(*me)] for k in range(n)]

    def copy(k, sem, block, to, src=None):
        dst = outs[k].at[_slot(*block)]
        return pltpu.make_async_remote_copy(
            src_ref=dst if src is None else src, dst_ref=dst, send_sem=send_sems.at[k, sem],
            recv_sem=recv_sems.at[k, sem], device_id=to, device_id_type=MESH)

    def first():
        cps = []
        for k in range(n):
            cps += [copy(k, 0, me, sibling, src=bufs[k]), copy(k, 1, me, (*na, c), src=bufs[k]),
                    copy(k, 2, me, (*nb, c), src=bufs[k])]
        return cps

    def onward():
        return [copy(k, 3, (*na, c), (*nb, c)) for k in range(n)]

    def to_sibling(j, chip):
        return [copy(k, 4 + j, (*chip, c), sibling) for k in range(n)]

    def start():
        loads = [pltpu.make_async_copy(ins[k], stage[k], local_sems.at[k, 0]) for k in range(n)]
        for cp in loads:
            cp.start()
        for cp in loads:
            cp.wait()
        for k in range(n):
            bufs[k][...] = stage[k][...].astype(bufs[k].dtype)
        for cp in first() + _local_stores(own, bufs, local_sems):
            cp.start()

    def relay():
        for k in range(n):
            copy(k, 1, (*na, c), me).wait_recv()
        for cp in onward() + to_sibling(0, na):
            cp.start()
        for k in range(n):
            copy(k, 2, (*nb, c), me).wait_recv()
        for cp in to_sibling(1, nb):
            cp.start()

    def pass_on():
        for k in range(n):
            copy(k, 3, (*dg, c), me).wait_recv()
        for cp in to_sibling(2, dg):
            cp.start()

    def finish():
        for k in range(n):
            copy(k, 0, sibling, me).wait_recv()
            for j, chip in enumerate((nb, na, dg)):
                copy(k, 4 + j, (*chip, 1 - c), me).wait_recv()
        for cp in first() + onward() + to_sibling(0, na) + to_sibling(1, nb) + to_sibling(2, dg):
            cp.wait_send()
        for cp in _local_stores(own, bufs, local_sems):
            cp.wait()

    return start, relay, pass_on, finish


N_CHIPS = 4
_CHIP_FLIPS = [(1, 0), (0, 1), (1, 1)]
F32_TRAVEL_LIMIT = 4096


def _chip_exchange_plan(ins, outs, bufs, send_sems, recv_sems, local_sems):
    n = len(ins)
    x, y, c = _position()
    my_q = 2 * x + y
    peers = [(x ^ fx, y ^ fy) for fx, fy in _CHIP_FLIPS]
    own = [outs[k].at[my_q] for k in range(n)]

    def sends():
        return [pltpu.make_async_remote_copy(
            src_ref=ins[k].at[2 * px + py], dst_ref=outs[k].at[my_q], send_sem=send_sems.at[k, f],
            recv_sem=recv_sems.at[k, f], device_id=(px, py, c), device_id_type=MESH)
            for f, (px, py) in enumerate(peers) for k in range(n)]

    def start():
        for cp in sends():
            cp.start()
        _local_copy([ins[k].at[my_q] for k in range(n)], own, bufs, local_sems)

    def finish():
        for f, (px, py) in enumerate(peers):
            for k in range(n):
                pltpu.make_async_remote_copy(
                    src_ref=ins[k].at[my_q], dst_ref=outs[k].at[2 * px + py], send_sem=send_sems.at[k, f],
                    recv_sem=recv_sems.at[k, f], device_id=(px, py, c), device_id_type=MESH).wait_recv()
        for cp in sends():
            cp.wait_send()
        for cp in _local_stores(own, bufs, local_sems):
            cp.wait()

    return start, finish


def _pair_sum(parts, sib):
    n = len(parts)

    def body(core_ref, *refs):
        for k in range(n):
            refs[2 * n + k][0] = (refs[k][0, 0].astype(F32) + refs[n + k][0].astype(F32)).astype(BF16)

    own = [pl.BlockSpec((1, 1) + p.shape[1:], lambda q, core: (q, core[0], 0, 0)) for p in parts]
    one = [pl.BlockSpec((1,) + p.shape[1:], lambda q, core: (q, 0, 0)) for p in parts]
    return pl.pallas_call(
        body, name="pair_sum",
        grid_spec=pltpu.PrefetchScalarGridSpec(num_scalar_prefetch=1, grid=(N_CHIPS,), in_specs=own + one,
                                               out_specs=one),
        out_shape=[jax.ShapeDtypeStruct(s.shape, BF16) for s in sib],
        compiler_params=pltpu.CompilerParams(dimension_semantics=("parallel",)),
    )(lax.axis_index("c").astype(jnp.int32).reshape(1), *[p.reshape((N_CHIPS, 2) + p.shape[1:]) for p in parts], *sib)


_VMEM = pl.BlockSpec(memory_space=pltpu.VMEM)


def _reduce_out_shapes(parts, vals):
    return ([jax.ShapeDtypeStruct((N_CHIPS,) + p.shape[1:], p.dtype) for p in parts]
            + [jax.ShapeDtypeStruct(v.shape, v.dtype) for v in vals])


def _reduce_scratch(parts, vals):
    n, ns = len(parts), len(vals)
    quarter = [pltpu.VMEM((N_CHIPS,) + p.shape[1:], p.dtype) for p in parts]
    dma = pltpu.SemaphoreType.DMA
    travel = [BF16 if v.size > F32_TRAVEL_LIMIT else v.dtype for v in vals]
    return (quarter * 3 + [pltpu.VMEM(v.shape, v.dtype) for v in vals]
            + [pltpu.VMEM(v.shape, t) for v, t in zip(vals, travel)]
            + [pltpu.VMEM((N_CHIPS,) + v.shape, t) for v, t in zip(vals, travel)]
            + [dma((max(n, 1), N_CHIPS)), dma((max(n, 1), N_CHIPS)), dma((max(ns, 1),)), dma((max(ns, 1),)),
               dma((max(n, 1), 3)), dma((max(n, 1), 3)), dma((max(ns, 1), 3)), dma((max(ns, 1), 3))])


def _reduce_plan(p_in, v_in, p_out, v_out, scratch):
    n, ns = len(p_in), len(v_in)
    p_sib, p_sum, p_all = scratch[:n], scratch[n:2 * n], scratch[2 * n:3 * n]
    v_sib, v_sum, v_all = (scratch[3 * n + j * ns:3 * n + (j + 1) * ns] for j in range(3))
    p1_send, p1_recv, v1_send, v1_recv, p3_send, p3_recv, v3_send, v3_recv = scratch[3 * n + 3 * ns:]
    x, y, c = _position()
    my_q = 2 * x + y
    peers = [(x ^ fx, y ^ fy) for fx, fy in _CHIP_FLIPS]

    def to_sibling(src, dst, send, recv):
        return pltpu.make_async_remote_copy(src_ref=src, dst_ref=dst, send_sem=send, recv_sem=recv,
                                            device_id=(x, y, 1 - c), device_id_type=MESH)

    def level1():
        cps = [to_sibling(p_in[k].at[2 * q + 1 - c], p_sib[k].at[q], p1_send.at[k, q], p1_recv.at[k, q])
               for k in range(n) for q in range(N_CHIPS)]
        return cps + [to_sibling(v_in[k], v_sib[k], v1_send.at[k], v1_recv.at[k]) for k in range(ns)]

    def to_chip(f, src, dst, send, recv):
        px, py = peers[f]
        return pltpu.make_async_remote_copy(src_ref=src, dst_ref=dst, send_sem=send, recv_sem=recv,
                                            device_id=(px, py, c), device_id_type=MESH)

    def level2(sending):
        cps = []
        for f, (px, py) in enumerate(peers):
            their_q = 2 * px + py
            for k in range(n):
                src, dst = (p_sum[k].at[their_q], p_all[k].at[my_q]) if sending else (
                    p_sum[k].at[my_q], p_all[k].at[their_q])
                cps.append(to_chip(f, src, dst, p3_send.at[k, f], p3_recv.at[k, f]))
            for k in range(ns):
                dst = v_all[k].at[my_q] if sending else v_all[k].at[their_q]
                cps.append(to_chip(f, v_sum[k], dst, v3_send.at[k, f], v3_recv.at[k, f]))
        return cps

    def start():
        for cp in level1():
            cp.start()

    def middle():
        for cp in level1():
            cp.wait_recv()
        for k in range(n):
            for q in range(N_CHIPS):
                p_sum[k][q] = (p_in[k][2 * q + c].astype(F32) + p_sib[k][q].astype(F32)).astype(p_sum[k].dtype)
        for k in range(ns):
            v_sum[k][...] = (v_in[k][...] + v_sib[k][...]).astype(v_sum[k].dtype)
        for cp in level2(True):
            cp.start()
        for k in range(n):
            p_all[k][my_q] = p_sum[k][my_q]
        for k in range(ns):
            v_all[k][my_q] = v_sum[k][...]

    def finish():
        for cp in level2(False):
            cp.wait_recv()
        for k in range(n):
            p_out[k][...] = p_all[k][...]
        for k in range(ns):
            total = v_all[k][0].astype(F32)
            for q in range(1, N_CHIPS):
                total = total + v_all[k][q].astype(F32)
            v_out[k][...] = total
        for cp in level1() + level2(True):
            cp.wait_send()

    return start, middle, finish


def _adamw_math(w, g, m, v):
    m = ADAM_B1 * m + (1.0 - ADAM_B1) * g
    v = ADAM_B2 * v + (1.0 - ADAM_B2) * (g * g)
    m_hat = m / (1.0 - ADAM_B1 ** ADAM_STEP)
    v_hat = v / (1.0 - ADAM_B2 ** ADAM_STEP)
    delta = -ADAM_LR * (m_hat / (jnp.sqrt(v_hat) + ADAM_EPS) + ADAM_WD * w)
    return delta, m, v


def _adamw_shards(parts, ws, ms, vs, steps, name):
    n = len(ws)

    def body(*refs):
        ins, outs = refs[:4 * n], refs[4 * n:]
        for k in range(n):
            p_ref, w_ref, m_ref, v_ref = ins[4 * k:4 * k + 4]
            g = p_ref[0].astype(F32)
            for j in range(1, N_CHIPS):
                g = g + p_ref[j].astype(F32)
            d, mn, vn = _adamw_math(w_ref[...], g, m_ref[...], v_ref[...])
            for o, val in zip(outs[4 * k:4 * k + 4], (g, d, mn, vn)):
                o[...] = val

    in_specs, out_specs, out_shape, operands = [], [], [], []
    for p, w, m, v in zip(parts, ws, ms, vs):
        R, Cc = w.shape
        blk = pl.BlockSpec((R // steps, Cc), lambda i: (i, 0))
        in_specs += [pl.BlockSpec((N_CHIPS, R // steps, Cc), lambda i: (0, i, 0)), blk, blk, blk]
        out_specs += [blk] * 4
        out_shape += [jax.ShapeDtypeStruct((R, Cc), F32)] * 4
        operands += [p, w, m, v]
    res = pl.pallas_call(
        body, name=name, grid=(steps,), in_specs=in_specs, out_specs=out_specs, out_shape=out_shape,
        compiler_params=pltpu.CompilerParams(dimension_semantics=("parallel",), vmem_limit_bytes=VMEM_LIMIT),
    )(*operands)
    return [res[4 * k:4 * k + 4] for k in range(n)]


def _adamw_small(grads, ws, ms, vs):
    n = len(grads)
    vm = pl.BlockSpec(memory_space=pltpu.VMEM)

    def body(*refs):
        g_in, w_in, m_in, v_in = (refs[k * n:(k + 1) * n] for k in range(4))
        g_out, d_out, m_out, v_out = (refs[(4 + k) * n:(5 + k) * n] for k in range(4))
        for k in range(n):
            if len(g_in[k].shape) > len(w_in[k].shape):
                g = g_in[k][0]
                for q in range(1, g_in[k].shape[0]):
                    g = g + g_in[k][q]
            else:
                g = g_in[k][...]
            d, mn, vn = _adamw_math(w_in[k][...], g, m_in[k][...], v_in[k][...])
            g_out[k][...] = g
            d_out[k][...] = d
            m_out[k][...] = mn
            v_out[k][...] = vn

    shapes = [jax.ShapeDtypeStruct(w.shape, F32) for w in ws]
    return pl.pallas_call(
        body, name="adamw_small", out_shape=shapes * 4, in_specs=[vm] * (4 * n), out_specs=[vm] * (4 * n),
    )(*grads, *ws, *ms, *vs)


def _shard_cols(full):
    R, Ct = full.shape
    return jnp.transpose(full.reshape(R, N_DEV, Ct // N_DEV), (1, 0, 2))


_COL_SHARDED = ("w_in", "w_gate", "w_up")
_BIG = ("w_in", "w_out", "w_gate", "w_up", "w_down")


def _rows(nm, p):
    return p[0].T if nm in _COL_SHARDED else p[0]


def _step(args, ts_mix_fwd, ts_ffn, ts_mix_bwd, tm_grad):
    (x, g_mix, w_in, b_in, w_dw, b_dw, ln_g, ln_b, w_pool, s_pool, w_out, g_ffn, w_gate, w_up, w_down, g_final,
     loss_target) = args[:17]
    names = ["g_mix", "w_in", "b_in", "w_dw", "b_dw", "ln_g", "ln_b", "w_pool", "s_pool", "w_out", "g_ffn",
             "w_gate", "w_up", "w_down", "g_final"]
    weights = dict(zip(names, args[1:16]))
    moms = dict(zip(names, args[17:32]))
    vars_ = dict(zip(names, args[32:47]))

    S, D = x.shape[1], x.shape[2]
    x2 = x.reshape(S, D)
    tgt2 = loss_target.reshape(S, D)

    shard = lambda nm: _rows(nm, weights[nm])
    w_pool_b = w_pool[0].astype(BF16)
    dw_rows = -(-CONV_WIDTH // SUBLANES) * SUBLANES
    dw_shard = jnp.pad(w_dw[0], ((0, dw_rows - CONV_WIDTH), (0, LANES - w_dw.shape[2])))

    a, gate, v, m, y, h1, xn, w_dw_f, g_in, g_out, _, g_gate, g_up, g_down = _mix_fwd(
        x2, g_mix, b_in, b_dw, ln_g, ln_b, w_pool_b, s_pool, [shard("w_in"), shard("w_out"), dw_shard],
        [shard("w_gate"), shard("w_up"), shard("w_down")], ts_mix_fwd)
    wt_in, w_out_f = g_in.reshape(-1, D), g_out.reshape(-1, D)
    wt_gate, wt_up, w_down_f = g_gate.reshape(-1, D), g_up.reshape(-1, D), g_down.reshape(-1, D)
    Fd = wt_gate.shape[0]
    f_chunks = [1024] * (Fd // 1024) + ([Fd % 1024] if Fd % 1024 else [])
    dh1, hn, act, dgt, dup, dh2, loss_p, dg_final, dg_ffn = _ffn(
        h1, tgt2, g_ffn, g_final.reshape(1, D), wt_gate, wt_up, w_down_f, ts_ffn, f_chunks)

    by_shard = lambda g: g.reshape(N_DEV, -1, D)
    ffn_grads, ffn_sib = _grad_ffn([dgt, dup, act], [hn, hn, dh2], tm_grad)
    pair_sums = _pair_sum([by_shard(g) for g in ffn_grads], list(ffn_sib))
    (dx, dz, dh1b, dg_mix, db_in, dw_dw, db_dw, dln_g, dln_b, dw_pool, ds_pool, r_gate, r_up, r_down) = _mix_bwd(
        dh1, x2, a, gate, v, m, g_mix, wt_in, w_dw_f, ln_g, ln_b, w_pool_b, s_pool, w_out_f, pair_sums, ts_mix_bwd)

    small_names = ["g_mix", "b_in", "b_dw", "ln_g", "ln_b", "w_pool", "s_pool", "g_ffn", "g_final"]
    small_shape = lambda p: p.reshape(-1, p.shape[-1])
    partial = [dg_mix, db_in, db_dw, dln_g, dln_b, dw_pool.reshape(-1, POOL_GROUP), ds_pool, dg_ffn, dg_final, loss_p]
    r_out, r_dw, r_in, summed = _grad_mix(y, dh1b, dz, xn, _shard_cols(dw_dw[0:CONV_WIDTH]), partial)

    big = {}
    recv = dict(zip(_BIG, [r_in, r_out, r_gate, r_up, r_down]))
    for call, group in (("adamw_ffn", ("w_gate", "w_up", "w_down")), ("adamw_mix", ("w_in", "w_out"))):
        results = _adamw_shards([recv[nm] for nm in group], [_rows(nm, weights[nm]) for nm in group],
                                [_rows(nm, moms[nm]) for nm in group], [_rows(nm, vars_[nm]) for nm in group], 2, call)
        for nm, res in zip(group, results):
            big[nm] = [o.T if nm in _COL_SHARDED else o for o in res]

    small_names = small_names + ["w_dw"]
    sm = _adamw_small(list(summed[:-1]) + [r_dw], [small_shape(weights[nm]) for nm in small_names],
                      [small_shape(moms[nm]) for nm in small_names], [small_shape(vars_[nm]) for nm in small_names])
    n_small = len(small_names)

    def result(kind, nm):
        if nm in big:
            return big[nm][kind].reshape(weights[nm].shape)
        return sm[kind * n_small + small_names.index(nm)].reshape(weights[nm].shape)

    loss = summed[-1][0, 0]
    out = [loss, dx.reshape(x.shape)]
    for kind in range(4):
        out += [result(kind, nm) for nm in names]
    return tuple(out)


def kernel(x, g_mix, w_in, b_in, w_dw, b_dw, ln_g, ln_b, w_pool, s_pool, w_out, g_ffn, w_gate, w_up, w_down, g_final, loss_target, m_g_mix, m_w_in, m_b_in, m_w_dw, m_b_dw, m_ln_g, m_ln_b, m_w_pool, m_s_pool, m_w_out, m_g_ffn, m_w_gate, m_w_up, m_w_down, m_g_final, v_g_mix, v_w_in, v_b_in, v_w_dw, v_b_dw, v_ln_g, v_ln_b, v_w_pool, v_s_pool, v_w_out, v_g_ffn, v_w_gate, v_w_up, v_w_down, v_g_final):
    args = (x, g_mix, w_in, b_in, w_dw, b_dw, ln_g, ln_b, w_pool, s_pool, w_out, g_ffn, w_gate, w_up, w_down, g_final, loss_target, m_g_mix, m_w_in, m_b_in, m_w_dw, m_b_dw, m_ln_g, m_ln_b, m_w_pool, m_s_pool, m_w_out, m_g_ffn, m_w_gate, m_w_up, m_w_down, m_g_final, v_g_mix, v_w_in, v_b_in, v_w_dw, v_b_dw, v_ln_g, v_ln_b, v_w_pool, v_s_pool, v_w_out, v_g_ffn, v_w_gate, v_w_up, v_w_down, v_g_final)
    return _step(args, ts_mix_fwd=512, ts_ffn=256, ts_mix_bwd=512, tm_grad=256)
```

```python
import jax
import jax.numpy as jnp
from jax import lax
from jax.experimental import pallas as pl
from jax.experimental.pallas import tpu as pltpu

F32 = jnp.float32
BF16 = jnp.bfloat16
MESH = pl.DeviceIdType.MESH
N_DEV = 8

C_CONV = 512
CONV_WIDTH = 31
POOL_WINDOWS = (2, 4, 8, 16)
POOL_GROUP = 128
RMS_EPS = 1e-6
LN_EPS = 1e-5

ADAM_LR = 0.001
ADAM_B1 = 0.9
ADAM_B2 = 0.999
ADAM_EPS = 1e-08
ADAM_WD = 0.01
ADAM_STEP = 10

HALO = 32
SUBLANES = 8
LANES = 128
CONV_ROWS = 64
VMEM_LIMIT = 56 * 1024 * 1024


def _dot(a, b):
    return jnp.dot(a, b, preferred_element_type=F32)


def _dot_nt(a, b):
    return lax.dot_general(a, b, (((1,), (1,)), ((), ())), preferred_element_type=F32)


def _dot_tn(a, b):
    return lax.dot_general(a, b, (((0,), (0,)), ((), ())), preferred_element_type=F32)


def _mean_last(v):
    return jnp.mean(v, axis=-1, keepdims=True)


def _full(shape):
    nd = len(shape)
    return pl.BlockSpec(shape, lambda *_: (0,) * nd)


def _full1(shape):
    nd = len(shape)
    return pl.BlockSpec(shape, lambda *_: (0,) * nd, pipeline_mode=pl.Buffered(1))


def _shifted_copies(sh_ref, rows):
    for s in range(1, SUBLANES):
        sh_ref[s, 0:rows, :] = sh_ref[0, s:s + rows, :]


def _tap(sh_ref, off, r0, rows):
    q, s = divmod(off, SUBLANES)
    return sh_ref[s, pl.ds(r0 + q * SUBLANES, rows), :]


def _window_sums(src, bufs, rows, backward):
    assert all(w == 2 << g for g, w in enumerate(POOL_WINDOWS))
    n = len(POOL_WINDOWS)
    out = []
    for level in range(n):
        dst, shift = bufs[level % 2], 1 << level
        lanes = slice(level * POOL_GROUP, n * POOL_GROUP)
        lo, hi = (SUBLANES * (level + 1), rows) if backward else (0, rows - SUBLANES * (level + 1))
        other = slice(lo - shift, hi - shift) if backward else slice(lo + shift, hi + shift)
        dst[lo:hi, lanes] = src[lo:hi, lanes] + src[other, lanes]
        src = dst
        out.append(dst)
    return out


def _mix_fwd(x, g_mix, b_in, b_dw, ln_g, ln_b, w_pool, s_pool, mix_shards, ag_shards, ts):
    S, D = x.shape
    d_in = mix_shards[0].shape[0] * N_DEV
    C = C_CONV
    nt = S // ts
    nrb = ts // CONV_ROWS
    n_ag = len(ag_shards)
    relay_step = nt // 2
    mix_dtypes = [BF16, BF16, F32]

    def body(x_ref, g_ref, bin_ref, bdw_ref, lng_ref, lnb_ref, wp_ref, sp_ref, *rest):
        mx_in, ag_in, rest = rest[:3], rest[3:3 + n_ag], rest[3 + n_ag:]
        a_ref, gate_ref, v_ref, m_ref, y_ref, h1_ref, xn_ref, wdw_ref = rest[:8]
        mx_out, ag_out, rest = rest[8:11], rest[11:11 + n_ag], rest[11 + n_ag:]
        ush, pbuf, pa, pb, win_ref, wout_ref, gdw, load_sems = rest[:8]
        rest = rest[8:]
        (ma_stage, ma_bufs, ma_sems), rest = (rest[:2], rest[2:4], rest[4:7]), rest[7:]
        (mb_stage, mb_bufs, mb_sems), rest = (rest[:1], rest[1:2], rest[2:5]), rest[5:]
        ag_stage, ag_bufs, ag_sems = rest[:n_ag], rest[n_ag:2 * n_ag], rest[2 * n_ag:]
        i = pl.program_id(0)
        ma_start, ma_relay, ma_pass_on, ma_finish = _gather_plan(
            mx_in[0::2], mx_out[0::2], ma_stage, ma_bufs, *ma_sems)
        mb_start, mb_relay, mb_pass_on, mb_finish = _gather_plan(
            mx_in[1:2], mx_out[1:2], mb_stage, mb_bufs, *mb_sems)
        ag_start, ag_relay, ag_pass_on, ag_finish = _gather_plan(ag_in, ag_out, ag_stage, ag_bufs, *ag_sems)

        def load_weight(k, dst_ref):
            rows = mx_out[k].shape[1]
            loads = [pltpu.make_async_copy(mx_out[k].at[j], dst_ref.at[pl.ds(j * rows, rows), :],
                                           load_sems.at[N_DEV * k + j]) for j in range(N_DEV)]
            for cp in loads:
                cp.start()
            for cp in loads:
                cp.wait()

        @pl.when(i == 0)
        def _():
            ma_start()
            mb_start()
            ma_relay()
            mb_relay()
            ag_start()
            ma_pass_on()
            ma_finish()
            load_weight(0, win_ref)
            dw_load = pltpu.make_async_copy(mx_out[2], gdw, load_sems.at[2 * N_DEV])
            dw_load.start()
            dw_load.wait()
            first_half = lax.broadcasted_iota(jnp.int32, gdw.shape[1:], 1) < C // N_DEV
            for p in range(N_DEV // 2):
                wdw_ref[:, LANES * p:LANES * (p + 1)] = jnp.where(
                    first_half, gdw[2 * p], pltpu.roll(gdw[2 * p + 1], C // N_DEV, axis=1))
            ush[0, 0:HALO, :] = jnp.zeros((HALO, C), F32)
            pbuf[0:HALO, :] = jnp.zeros((HALO, C), F32)

        @pl.when(i == relay_step)
        def _():
            ag_relay()

        @pl.when(i == nt - 1)
        def _():
            ag_pass_on()

        x = x_ref[...]
        r1 = lax.rsqrt(_mean_last(x * x) + RMS_EPS)
        xn = (x * r1 * g_ref[...]).astype(BF16)
        xn_ref[...] = xn
        z = _dot_nt(xn, win_ref[...]) + bin_ref[...]
        a = z[:, 0:C]
        gate = z[:, C:2 * C]
        a_ref[...] = a
        gate_ref[...] = gate
        ush[0, HALO:HALO + ts, :] = a * jax.nn.sigmoid(gate)
        pbuf[HALO:HALO + ts, :] = z[:, 2 * C:]

        _shifted_copies(ush, ts + HALO - SUBLANES)

        def conv_block(rb, carry):
            r0 = pl.multiple_of(rb * CONV_ROWS, CONV_ROWS)
            acc = jnp.zeros((CONV_ROWS, C), F32)
            for k in range(CONV_WIDTH):
                acc = acc + wdw_ref[k:k + 1, :] * _tap(ush, HALO - (CONV_WIDTH - 1) + k, r0, CONV_ROWS)
            v_ref[pl.ds(r0, CONV_ROWS), :] = acc + bdw_ref[...]
            return carry

        lax.fori_loop(0, nrb, conv_block, 0)

        v = v_ref[...]
        mu = _mean_last(v)
        xc = v - mu
        rstd = lax.rsqrt(_mean_last(xc * xc) + LN_EPS)
        ln = xc * rstd * lng_ref[...] + lnb_ref[...]
        y_ref[:, 0:C] = (ln * jax.nn.sigmoid(ln)).astype(BF16)

        sums = _window_sums(pbuf, (pa, pb), ts + HALO, backward=True)
        row = lax.broadcasted_iota(jnp.int32, (ts, 1), 0) + i * ts
        for gi, w in enumerate(POOL_WINDOWS):
            lanes = slice(gi * POOL_GROUP, (gi + 1) * POOL_GROUP)
            seg = pbuf[HALO:HALO + ts, lanes]
            ws = sums[gi][HALO:HALO + ts, lanes]
            cnt = jnp.minimum(row + 1, w).astype(F32)
            m = (ws / cnt - seg).astype(BF16)
            m_ref[:, lanes] = m
            ypre = _dot(m, wp_ref[gi])
            y_ref[:, C + gi * POOL_GROUP:C + (gi + 1) * POOL_GROUP] = (ypre * sp_ref[:, lanes]).astype(BF16)

        @pl.when(i == 0)
        def _():
            mb_pass_on()
            mb_finish()
            load_weight(1, wout_ref)

        h1_ref[...] = x + _dot(y_ref[...], wout_ref[...])

        ush[0, 0:HALO, :] = ush[0, ts:ts + HALO, :]
        pbuf[0:HALO, :] = pbuf[ts:ts + HALO, :]

        @pl.when(i == nt - 1)
        def _():
            ag_finish()

    tile = lambda w, dt: (pl.BlockSpec((ts, w), lambda i: (i, 0)), jax.ShapeDtypeStruct((S, w), dt))
    wdw_rows = mix_shards[2].shape[0]
    outs = [tile(C, F32), tile(C, F32), tile(C, F32), tile(C, BF16), tile(D, BF16), tile(D, F32), tile(D, BF16),
            (_full((wdw_rows, C)), jax.ShapeDtypeStruct((wdw_rows, C), F32))]
    return pl.pallas_call(
        body, name="mix_fwd", grid=(nt,),
        in_specs=[pl.BlockSpec((ts, D), lambda i: (i, 0)), _full((1, D)), _full((1, d_in)), _full((1, C)),
                  _full((1, C)), _full((1, C)), _full(w_pool.shape), _full((1, C))] + [_HBM] * (3 + n_ag),
        out_specs=[o[0] for o in outs] + [_HBM] * (3 + n_ag),
        out_shape=[o[1] for o in outs] + _gather_out_shapes(mix_shards, mix_dtypes)
        + _gather_out_shapes(ag_shards, [BF16] * n_ag),
        scratch_shapes=[pltpu.VMEM((SUBLANES, ts + HALO, C), F32)] + [pltpu.VMEM((ts + HALO, C), F32)] * 3
        + [pltpu.VMEM((d_in, D), BF16), pltpu.VMEM((D, D), BF16),
                        pltpu.VMEM((N_DEV,) + mix_shards[2].shape, F32), pltpu.SemaphoreType.DMA((2 * N_DEV + 1,))]
        + _gather_scratch(mix_shards[0::2], mix_dtypes[0::2]) + _gather_scratch(mix_shards[1:2], mix_dtypes[1:2])
        + _gather_scratch(ag_shards, [BF16] * n_ag),
        compiler_params=pltpu.CompilerParams(dimension_semantics=("arbitrary",), vmem_limit_bytes=VMEM_LIMIT),
    )(x, g_mix, b_in, b_dw, ln_g, ln_b, w_pool, s_pool, *mix_shards, *ag_shards)


def _ffn(h1, target, g_ffn, g_final, w_gate, w_up, w_down, ts, f_chunks):
    S, D = h1.shape
    Fd = w_gate.shape[0]
    nt = S // ts
    bounds = []
    lo = 0
    for n in f_chunks:
        bounds.append((lo, lo + n))
        lo += n
    assert lo == Fd

    def body(h1_ref, tgt_ref, gf_ref, gl_ref, wg_ref, wu_ref, wd_ref,
             dh1_ref, hn_ref, act_ref, dgt_ref, dup_ref, dh2_ref, loss_ref, dgl_ref, dgf_ref, gt_s, up_s):
        i = pl.program_id(0)

        @pl.when(i == 0)
        def _():
            loss_ref[...] = jnp.zeros_like(loss_ref)
            dgl_ref[...] = jnp.zeros_like(dgl_ref)
            dgf_ref[...] = jnp.zeros_like(dgf_ref)

        h1 = h1_ref[...]
        r2 = lax.rsqrt(_mean_last(h1 * h1) + RMS_EPS)
        hhat = h1 * r2
        hn = (hhat * gf_ref[...]).astype(BF16)
        hn_ref[...] = hn
        h2 = h1
        for lo, hi in bounds:
            gt = _dot_nt(hn, wg_ref[lo:hi, :])
            up = _dot_nt(hn, wu_ref[lo:hi, :])
            gt_s[:, lo:hi] = gt
            up_s[:, lo:hi] = up
            act = (gt * jax.nn.sigmoid(gt) * up).astype(BF16)
            act_ref[:, lo:hi] = act
            h2 = h2 + _dot(act, wd_ref[lo:hi, :])

        r3 = lax.rsqrt(_mean_last(h2 * h2) + RMS_EPS)
        n3 = h2 * r3
        gl = gl_ref[...]
        diff = n3 * gl - tgt_ref[...]
        loss_ref[...] += jnp.sum(0.5 * jnp.sum(diff * diff, axis=-1, keepdims=True) / D, axis=0, keepdims=True)
        dout = diff / D
        dgl_ref[...] += jnp.sum(dout * n3, axis=0, keepdims=True)
        dn = dout * gl
        dh2 = r3 * (dn - n3 * _mean_last(dn * n3))
        dh2b = dh2.astype(BF16)
        dh2_ref[...] = dh2b

        dhn = jnp.zeros((ts, D), F32)
        for lo, hi in bounds:
            gt = gt_s[:, lo:hi]
            up = up_s[:, lo:hi]
            sg = jax.nn.sigmoid(gt)
            dact = _dot_nt(dh2b, wd_ref[lo:hi, :])
            dgt = (dact * up * (sg * (1.0 + gt * (1.0 - sg)))).astype(BF16)
            dup = (dact * (gt * sg)).astype(BF16)
            dgt_ref[:, lo:hi] = dgt
            dup_ref[:, lo:hi] = dup
            dhn = dhn + _dot(dgt, wg_ref[lo:hi, :]) + _dot(dup, wu_ref[lo:hi, :])

        dgf_ref[...] += jnp.sum(dhn * hhat, axis=0, keepdims=True)
        dnn = dhn * gf_ref[...]
        dh1_ref[...] = dh2 + r2 * (dnn - hhat * _mean_last(dnn * hhat))

    tile = lambda w, dt: (pl.BlockSpec((ts, w), lambda i: (i, 0)), jax.ShapeDtypeStruct((S, w), dt))
    acc = lambda w: (_full((1, w)), jax.ShapeDtypeStruct((1, w), F32))
    outs = [tile(D, F32), tile(D, BF16), tile(Fd, BF16), tile(Fd, BF16), tile(Fd, BF16), tile(D, BF16),
            acc(LANES), acc(D), acc(D)]
    return pl.pallas_call(
        body, name="ffn_fwd_bwd", grid=(nt,),
        in_specs=[pl.BlockSpec((ts, D), lambda i: (i, 0)), pl.BlockSpec((ts, D), lambda i: (i, 0)),
                  _full((1, D)), _full((1, D)), _full1((Fd, D)), _full1((Fd, D)), _full1((Fd, D))],
        out_specs=[o[0] for o in outs], out_shape=[o[1] for o in outs],
        scratch_shapes=[pltpu.VMEM((ts, Fd), F32), pltpu.VMEM((ts, Fd), F32)],
        compiler_params=pltpu.CompilerParams(dimension_semantics=("arbitrary",), vmem_limit_bytes=VMEM_LIMIT),
    )(h1, target, g_ffn, g_final, w_gate, w_up, w_down)


def _mix_bwd(dh1, x, a, gate, v, m, g_mix, w_in, w_dw, ln_g, ln_b, w_pool, s_pool, w_out, rs_parts, ts):
    S, D = x.shape
    n_rs = len(rs_parts)
    d_in = w_in.shape[0]
    C = C_CONV
    nt = S // ts
    nrb = ts // CONV_ROWS
    wrows =((CONV_WIDTH + SUBLANES - 1) // SUBLANES) * SUBLANES

    def body(dh1_ref, x_ref, a_ref, gate_ref, v_ref, m_ref, g_ref, win_ref, wdw_ref, lng_ref,
             lnb_ref, wp_ref, sp_ref, wout_ref, *rest):
        rs_in, rest = rest[:n_rs], rest[n_rs:]
        (dx_ref, dz_ref, dh1b_ref, dgm_ref, dbin_ref, dwdw_ref, dbdw_ref, dlng_ref, dlnb_ref, dwp_ref,
         dsp_ref) = rest[:11]
        rs_out, rest = rest[11:11 + n_rs], rest[11 + n_rs:]
        dvsh, dqbuf, qa, qb, du_s, dm_s = rest[:6]
        rs_bufs, (send_sems, recv_sems, local_sems) = rest[6:6 + n_rs], rest[6 + n_rs:]
        i = pl.program_id(0)
        t = nt - 1 - i
        rs_start, rs_finish = _chip_exchange_plan(rs_in, rs_out, rs_bufs, send_sems, recv_sems, local_sems)

        @pl.when(i == 0)
        def _():
            rs_start()
            dvsh[0, ts:ts + HALO, :] = jnp.zeros((HALO, C), F32)
            dqbuf[ts:ts + HALO, :] = jnp.zeros((HALO, C), F32)
            for r in (dgm_ref, dbin_ref, dwdw_ref, dbdw_ref, dlng_ref, dlnb_ref, dwp_ref, dsp_ref):
                r[...] = jnp.zeros_like(r)

        dh1 = dh1_ref[...]
        dh1b = dh1.astype(BF16)
        dh1b_ref[...] = dh1b
        dy = _dot_nt(dh1b, wout_ref[...])

        v = v_ref[...]
        mu = _mean_last(v)
        xc = v - mu
        rstd = lax.rsqrt(_mean_last(xc * xc) + LN_EPS)
        vhat = xc * rstd
        lng = lng_ref[...]
        ln = vhat * lng + lnb_ref[...]
        sg = jax.nn.sigmoid(ln)
        dln = dy[:, 0:C] * (sg * (1.0 + ln * (1.0 - sg)))
        dlng_ref[...] += jnp.sum(dln * vhat, axis=0, keepdims=True)
        dlnb_ref[...] += jnp.sum(dln, axis=0, keepdims=True)
        dvh = dln * lng
        dv = rstd * (dvh - _mean_last(dvh) - vhat * _mean_last(dvh * vhat))
        dbdw_ref[...] += jnp.sum(dv, axis=0, keepdims=True)
        dvsh[0, 0:ts, :] = dv
        _shifted_copies(dvsh, ts + HALO - SUBLANES)

        def conv_block(rb, carry):
            r0 = pl.multiple_of(rb * CONV_ROWS, CONV_ROWS)
            acc = jnp.zeros((CONV_ROWS, C), F32)
            for k in range(CONV_WIDTH):
                acc = acc + wdw_ref[k:k + 1, :] * _tap(dvsh, CONV_WIDTH - 1 - k, r0, CONV_ROWS)
            du_s[pl.ds(r0, CONV_ROWS), :] = acc
            return carry

        lax.fori_loop(0, nrb, conv_block, 0)

        a = a_ref[...]
        sgate = jax.nn.sigmoid(gate_ref[...])
        u = a * sgate
        for k in range(CONV_WIDTH):
            q, s = divmod(CONV_WIDTH - 1 - k, SUBLANES)
            prod = u * dvsh[s, q * SUBLANES:q * SUBLANES + ts, :]
            dwdw_ref[k:k + 1, :] += jnp.sum(prod, axis=0, keepdims=True)

        du = du_s[...]
        da = du * sgate
        dgate = du * a * sgate * (1.0 - sgate)
        dz_ref[:, 0:C] = da.astype(BF16)
        dz_ref[:, C:2 * C] = dgate.astype(BF16)
        dbin_ref[:, 0:C] += jnp.sum(da, axis=0, keepdims=True)
        dbin_ref[:, C:2 * C] += jnp.sum(dgate, axis=0, keepdims=True)

        row = lax.broadcasted_iota(jnp.int32, (ts, 1), 0) + t * ts
        for gi, w in enumerate(POOL_WINDOWS):
            lanes = slice(gi * POOL_GROUP, (gi + 1) * POOL_GROUP)
            dyp = dy[:, C + gi * POOL_GROUP:C + (gi + 1) * POOL_GROUP]
            mg = m_ref[:, lanes]
            ypre = _dot(mg, wp_ref[gi])
            dsp_ref[:, lanes] += jnp.sum(dyp * ypre, axis=0, keepdims=True)
            dyi = (dyp * sp_ref[:, lanes]).astype(BF16)
            dwp_ref[gi] += _dot_tn(mg, dyi)
            dm = _dot_nt(dyi, wp_ref[gi])
            cnt = jnp.minimum(row + 1, w).astype(F32)
            dqbuf[0:ts, lanes] = dm / cnt
            dm_s[:, lanes] = dm
        sums = _window_sums(dqbuf, (qa, qb), ts + HALO, backward=False)
        for gi in range(len(POOL_WINDOWS)):
            lanes = slice(gi * POOL_GROUP, (gi + 1) * POOL_GROUP)
            dp = sums[gi][0:ts, lanes] - dm_s[:, lanes]
            dz_ref[:, 2 * C + gi * POOL_GROUP:2 * C + (gi + 1) * POOL_GROUP] = dp.astype(BF16)
            dbin_ref[:, 2 * C + gi * POOL_GROUP:2 * C + (gi + 1) * POOL_GROUP] += jnp.sum(dp, axis=0, keepdims=True)

        dxn = _dot(dz_ref[...], win_ref[...])
        x = x_ref[...]
        r1 = lax.rsqrt(_mean_last(x * x) + RMS_EPS)
        xhat = x * r1
        dgm_ref[...] += jnp.sum(dxn * xhat, axis=0, keepdims=True)
        dnn = dxn * g_ref[...]
        dx_ref[...] = dh1 + r1 * (dnn - xhat * _mean_last(dnn * xhat))

        dvsh[0, ts:ts + HALO, :] = dvsh[0, 0:HALO, :]
        dqbuf[ts:ts + HALO, :] = dqbuf[0:HALO, :]

        @pl.when(i == nt - 1)
        def _():
            rs_finish()

    rev = lambda w: pl.BlockSpec((ts, w), lambda i: (nt - 1 - i, 0))
    acc = lambda shape: (_full(shape), jax.ShapeDtypeStruct(shape, F32))
    outs = [(rev(D), jax.ShapeDtypeStruct((S, D), F32)), (rev(d_in), jax.ShapeDtypeStruct((S, d_in), BF16)),
            (rev(D), jax.ShapeDtypeStruct((S, D), BF16)), acc((1, D)), acc((1, d_in)), acc((wrows, C)),
            acc((1, C)), acc((1, C)), acc((1, C)), acc(w_pool.shape), acc((1, C))]
    return pl.pallas_call(
        body, name="mix_bwd", grid=(nt,),
        in_specs=[rev(D), rev(D), rev(C), rev(C), rev(C), rev(C), _full((1, D)), _full((d_in, D)),
                  _full(w_dw.shape), _full((1, C)), _full((1, C)), _full(w_pool.shape), _full((1, C)),
                  _full((D, D))] + [_HBM] * n_rs,
        out_specs=[o[0] for o in outs] + [_HBM] * n_rs,
        out_shape=[o[1] for o in outs] + [jax.ShapeDtypeStruct(p.shape, p.dtype) for p in rs_parts],
        scratch_shapes=[pltpu.VMEM((SUBLANES, ts + HALO, C), F32)] + [pltpu.VMEM((ts + HALO, C), F32)] * 3
        + [pltpu.VMEM((ts, C), F32)] * 2
        + [pltpu.VMEM(p.shape[1:], p.dtype) for p in rs_parts] + _comm_sems(n_rs, 3),
        compiler_params=pltpu.CompilerParams(dimension_semantics=("arbitrary",), vmem_limit_bytes=VMEM_LIMIT),
    )(dh1, x, a, gate, v, m, g_mix, w_in, w_dw, ln_g, ln_b, w_pool, s_pool, w_out, *rs_parts)


def _grad_mix(y, dh1b, dz, xn, dw_dw, vals):
    S, D = y.shape
    d_in = dz.shape[1]
    r_out, r_in = D // N_DEV, d_in // N_DEV
    n_out, n_in = N_DEV // 2, N_DEV // 2
    steps = n_out + n_in
    ns = len(vals)
    dw_out = jax.ShapeDtypeStruct((N_DEV, r_out, D), BF16)
    dw_in = jax.ShapeDtypeStruct((N_DEV, r_in, D), BF16)
    s_scratch, o_scratch = _reduce_scratch([], vals), _reduce_scratch([dw_out, dw_dw], [])
    i_scratch = _reduce_scratch([dw_in], [])

    def body(y_ref, dh_ref, dz_ref, xn_ref, dwdw_ref, *rest):
        v_in, p_out, v_out = rest[:ns], rest[ns:ns + 3], rest[ns + 3:2 * ns + 3]
        rest = rest[2 * ns + 3:]
        dwo_v, dwi_v, rest = rest[0], rest[1], rest[2:]
        s_refs, rest = rest[:len(s_scratch)], rest[len(s_scratch):]
        o_refs, i_refs = rest[:len(o_scratch)], rest[len(o_scratch):]
        s_start, s_middle, s_finish = _reduce_plan((), v_in, (), v_out, s_refs)
        o_start, o_middle, o_finish = _reduce_plan((dwo_v, dwdw_ref), (), p_out[:2], (), o_refs)
        i_start, i_middle, i_finish = _reduce_plan((dwi_v,), (), p_out[2:], (), i_refs)
        s = pl.program_id(0)
        pl.when(s == 0)(s_start)
        pl.when(s == 1)(s_middle)

        @pl.when(s < n_out)
        def _():
            res = _dot_tn(y_ref[...], dh_ref[...]).astype(BF16)
            dwo_v[2 * s] = res[0:r_out]
            dwo_v[2 * s + 1] = res[r_out:]

        pl.when(s == n_out - 1)(s_finish)
        pl.when(s == n_out)(o_start)
        pl.when(s == n_out + 1)(o_middle)

        @pl.when(s >= n_out)
        def _():
            res = _dot_tn(dz_ref[...], xn_ref[...]).astype(BF16)
            dwi_v[2 * (s - n_out)] = res[0:r_in]
            dwi_v[2 * (s - n_out) + 1] = res[r_in:]

        @pl.when(s == steps - 1)
        def _():
            o_finish()
            i_start()
            i_middle()
            i_finish()

    late = lambda s: jnp.clip(s - n_out, 0, n_in - 1)
    res = pl.pallas_call(
        body, name="grad_w_mix", grid=(steps,),
        in_specs=[pl.BlockSpec((S, 2 * r_out), lambda s: (0, jnp.minimum(s, n_out - 1))), _full1((S, D)),
                  pl.BlockSpec((S, 2 * r_in), lambda s: (0, late(s))), _full1((S, D))] + [_VMEM] * (1 + ns),
        out_specs=[_VMEM] * (3 + ns),
        out_shape=_reduce_out_shapes([dw_out, dw_dw, dw_in], []) + _reduce_out_shapes([], vals),
        scratch_shapes=[pltpu.VMEM(dw_out.shape, BF16), pltpu.VMEM(dw_in.shape, BF16)] + s_scratch + o_scratch
        + i_scratch,
        compiler_params=pltpu.CompilerParams(dimension_semantics=("arbitrary",), vmem_limit_bytes=VMEM_LIMIT),
    )(y, dh1b, dz, xn, dw_dw, *vals)
    return res[0], res[1], res[2], res[3:]


def _grad_ffn(a_list, b_list, tm):
    n = len(a_list)
    S, M = a_list[0].shape
    N = b_list[0].shape[1]
    nb = M // tm
    steps = n * nb
    R = M // N_DEV
    b_unique = [b for k, b in enumerate(b_list) if all(b is not o for o in b_list[:k])]
    b_index = [[b is o for o in b_unique].index(True) for b in b_list]

    def body(*refs):
        a_refs, b_refs = refs[:n], refs[n:n + len(b_unique)]
        rest = refs[n + len(b_unique):]
        g_refs, sib_refs, (obuf, w_sems, send_sems, recv_sems) = rest[:n], rest[n:2 * n], rest[2 * n:]
        s = pl.program_id(0)
        slot = s % 2
        x, y, c = _position()

        def write_back(dst):
            return pltpu.make_async_copy(obuf.at[slot], dst, w_sems.at[slot])

        def exchange(k):
            return [pltpu.make_async_remote_copy(
                src_ref=g_refs[k].at[pl.ds(pl.multiple_of((2 * q + 1 - c) * R, SUBLANES * 2), R), :],
                dst_ref=sib_refs[k].at[q], send_sem=send_sems.at[k, q], recv_sem=recv_sems.at[k, q],
                device_id=(x, y, 1 - c), device_id_type=MESH) for q in range(N_CHIPS)]

        pl.when(s >= 2)(write_back(g_refs[0].at[pl.ds(0, tm), :]).wait)
        for k in range(n):
            @pl.when(jnp.logical_and(s >= k * nb, s < (k + 1) * nb))
            def _(k=k):
                obuf[slot] = _dot_tn(a_refs[k][...], b_refs[b_index[k]][...]).astype(BF16)
                row = pl.multiple_of((s - k * nb) * tm, tm)
                write_back(g_refs[k].at[pl.ds(row, tm), :]).start()
            if k < n - 1:
                @pl.when(s == (k + 1) * nb + 2)
                def _(k=k):
                    for cp in exchange(k):
                        cp.start()

        @pl.when(s == steps - 1)
        def _():
            write_back(g_refs[0].at[pl.ds(0, tm), :]).wait()
            pltpu.make_async_copy(obuf.at[1 - slot], g_refs[0].at[pl.ds(0, tm), :], w_sems.at[1 - slot]).wait()
            for cp in exchange(n - 1):
                cp.start()
            for k in range(n):
                for cp in exchange(k):
                    cp.wait()

    clamp = lambda k: (lambda s: (0, jnp.clip(s - k * nb, 0, nb - 1)))
    sems = pltpu.SemaphoreType.DMA((n, N_CHIPS))
    res = pl.pallas_call(
        body, name="grad_w_ffn", grid=(steps,),
        in_specs=[pl.BlockSpec((S, tm), clamp(k)) for k in range(n)] + [_full1((S, N))] * len(b_unique),
        out_specs=[_HBM] * (2 * n),
        out_shape=[jax.ShapeDtypeStruct((M, N), BF16)] * n + [jax.ShapeDtypeStruct((N_CHIPS, R, N), BF16)] * n,
        scratch_shapes=[pltpu.VMEM((2, tm, N), BF16), pltpu.SemaphoreType.DMA((2,)), sems, sems],
        compiler_params=pltpu.CompilerParams(dimension_semantics=("arbitrary",), vmem_limit_bytes=VMEM_LIMIT),
    )(*a_list, *b_unique)
    return res[:n], res[n:]


def _position():
    return lax.axis_index("x"), lax.axis_index("y"), lax.axis_index("c")


def _slot(px, py, pc):
    return 4 * px + 2 * py + pc


_HBM = pl.BlockSpec(memory_space=pl.ANY)


def _comm_sems(n, copies):
    return [pltpu.SemaphoreType.DMA((n, copies)), pltpu.SemaphoreType.DMA((n, copies)),
            pltpu.SemaphoreType.DMA((n, 2))]


def _local_copy(srcs, dsts, bufs, local_sems):
    n = len(srcs)
    loads = [pltpu.make_async_copy(srcs[k], bufs[k], local_sems.at[k, 0]) for k in range(n)]
    for cp in loads:
        cp.start()
    for cp in loads:
        cp.wait()
    stores = _local_stores(dsts, bufs, local_sems)
    for cp in stores:
        cp.start()
    return stores


def _local_stores(dsts, bufs, local_sems):
    return [pltpu.make_async_copy(bufs[k], dsts[k], local_sems.at[k, 1]) for k in range(len(dsts))]


def _gather_out_shapes(shards, dtypes):
    return [jax.ShapeDtypeStruct((N_DEV,) + s.shape, dt) for s, dt in zip(shards, dtypes)]


def _gather_scratch(shards, dtypes):
    return ([pltpu.VMEM(s.shape, s.dtype) for s in shards] + [pltpu.VMEM(s.shape, dt) for s, dt in zip(shards, dtypes)]
            + _comm_sems(len(shards), 7))


def _gather_plan(ins, outs, stage, bufs, send_sems, recv_sems, local_sems):
    n = len(ins)
    x, y, c = _position()
    me, sibling = (x, y, c), (x, y, 1 - c)
    na, nb, dg = (x ^ (1 - c), y ^ c), (x ^ c, y ^ (1 - c)), (1 - x, 1 - y)
    own = [outs[k].at[_slot(*me)] for k in range(n)]

    def copy(k, sem, block, to, src=None):
        dst = outs[k].at[_slot(*block)]
        return pltpu.make_async_remote_copy(
            src_ref=dst if src is None else src, dst_ref=dst, send_sem=send_sems.at[k, sem],
            recv_sem=recv_sems.at[k, sem], device_id=to, device_id_type=MESH)

    def first():
        cps = []
        for k in range(n):
            cps += [copy(k, 0, me, sibling, src=bufs[k]), copy(k, 1, me, (*na, c), src=bufs[k]),
                    copy(k, 2, me, (*nb, c), src=bufs[k])]
        return cps

    def onward():
        return [copy(k, 3, (*na, c), (*nb, c)) for k in range(n)]

    def to_sibling(j, chip):
        return [copy(k, 4 + j, (*chip, c), sibling) for k in range(n)]

    def start():
        loads = [pltpu.make_async_copy(ins[k], stage[k], local_sems.at[k, 0]) for k in range(n)]
        for cp in loads:
            cp.start()
        for cp in loads:
            cp.wait()
        for k in range(n):
            bufs[k][...] = stage[k][...].astype(bufs[k].dtype)
        for cp in first() + _local_stores(own, bufs, local_sems):
            cp.start()

    def relay():
        for k in range(n):
            copy(k, 1, (*na, c), me).wait_recv()
        for cp in onward() + to_sibling(0, na):
            cp.start()
        for k in range(n):
            copy(k, 2, (*nb, c), me).wait_recv()
        for cp in to_sibling(1, nb):
            cp.start()

    def pass_on():
        for k in range(n):
            copy(k, 3, (*dg, c), me).wait_recv()
        for cp in to_sibling(2, dg):
            cp.start()

    def finish():
        for k in range(n):
            copy(k, 0, sibling, me).wait_recv()
            for j, chip in enumerate((nb, na, dg)):
                copy(k, 4 + j, (*chip, 1 - c), me).wait_recv()
        for cp in first() + onward() + to_sibling(0, na) + to_sibling(1, nb) + to_sibling(2, dg):
            cp.wait_send()
        for cp in _local_stores(own, bufs, local_sems):
            cp.wait()

    return start, relay, pass_on, finish


N_CHIPS = 4
_CHIP_FLIPS = [(1, 0), (0, 1), (1, 1)]
F32_TRAVEL_LIMIT = 4096


def _chip_exchange_plan(ins, outs, bufs, send_sems, recv_sems, local_sems):
    n = len(ins)
    x, y, c = _position()
    my_q = 2 * x + y
    peers = [(x ^ fx, y ^ fy) for fx, fy in _CHIP_FLIPS]
    own = [outs[k].at[my_q] for k in range(n)]

    def sends():
        return [pltpu.make_async_remote_copy(
            src_ref=ins[k].at[2 * px + py], dst_ref=outs[k].at[my_q], send_sem=send_sems.at[k, f],
            recv_sem=recv_sems.at[k, f], device_id=(px, py, c), device_id_type=MESH)
            for f, (px, py) in enumerate(peers) for k in range(n)]

    def start():
        for cp in sends():
            cp.start()
        _local_copy([ins[k].at[my_q] for k in range(n)], own, bufs, local_sems)

    def finish():
        for f, (px, py) in enumerate(peers):
            for k in range(n):
                pltpu.make_async_remote_copy(
                    src_ref=ins[k].at[my_q], dst_ref=outs[k].at[2 * px + py], send_sem=send_sems.at[k, f],
                    recv_sem=recv_sems.at[k, f], device_id=(px, py, c), device_id_type=MESH).wait_recv()
        for cp in sends():
            cp.wait_send()
        for cp in _local_stores(own, bufs, local_sems):
            cp.wait()

    return start, finish


def _pair_sum(parts, sib):
    n = len(parts)

    def body(core_ref, *refs):
        for k in range(n):
            refs[2 * n + k][0] = (refs[k][0, 0].astype(F32) + refs[n + k][0].astype(F32)).astype(BF16)

    own = [pl.BlockSpec((1, 1) + p.shape[1:], lambda q, core: (q, core[0], 0, 0)) for p in parts]
    one = [pl.BlockSpec((1,) + p.shape[1:], lambda q, core: (q, 0, 0)) for p in parts]
    return pl.pallas_call(
        body, name="pair_sum",
        grid_spec=pltpu.PrefetchScalarGridSpec(num_scalar_prefetch=1, grid=(N_CHIPS,), in_specs=own + one,
                                               out_specs=one),
        out_shape=[jax.ShapeDtypeStruct(s.shape, BF16) for s in sib],
        compiler_params=pltpu.CompilerParams(dimension_semantics=("parallel",)),
    )(lax.axis_index("c").astype(jnp.int32).reshape(1), *[p.reshape((N_CHIPS, 2) + p.shape[1:]) for p in parts], *sib)


_VMEM = pl.BlockSpec(memory_space=pltpu.VMEM)


def _reduce_out_shapes(parts, vals):
    return ([jax.ShapeDtypeStruct((N_CHIPS,) + p.shape[1:], p.dtype) for p in parts]
            + [jax.ShapeDtypeStruct(v.shape, v.dtype) for v in vals])


def _reduce_scratch(parts, vals):
    n, ns = len(parts), len(vals)
    quarter = [pltpu.VMEM((N_CHIPS,) + p.shape[1:], p.dtype) for p in parts]
    dma = pltpu.SemaphoreType.DMA
    travel = [BF16 if v.size > F32_TRAVEL_LIMIT else v.dtype for v in vals]
    return (quarter * 3 + [pltpu.VMEM(v.shape, v.dtype) for v in vals]
            + [pltpu.VMEM(v.shape, t) for v, t in zip(vals, travel)]
            + [pltpu.VMEM((N_CHIPS,) + v.shape, t) for v, t in zip(vals, travel)]
            + [dma((max(n, 1), N_CHIPS)), dma((max(n, 1), N_CHIPS)), dma((max(ns, 1),)), dma((max(ns, 1),)),
               dma((max(n, 1), 3)), dma((max(n, 1), 3)), dma((max(ns, 1), 3)), dma((max(ns, 1), 3))])


def _reduce_plan(p_in, v_in, p_out, v_out, scratch):
    n, ns = len(p_in), len(v_in)
    p_sib, p_sum, p_all = scratch[:n], scratch[n:2 * n], scratch[2 * n:3 * n]
    v_sib, v_sum, v_all = (scratch[3 * n + j * ns:3 * n + (j + 1) * ns] for j in range(3))
    p1_send, p1_recv, v1_send, v1_recv, p3_send, p3_recv, v3_send, v3_recv = scratch[3 * n + 3 * ns:]
    x, y, c = _position()
    my_q = 2 * x + y
    peers = [(x ^ fx, y ^ fy) for fx, fy in _CHIP_FLIPS]

    def to_sibling(src, dst, send, recv):
        return pltpu.make_async_remote_copy(src_ref=src, dst_ref=dst, send_sem=send, recv_sem=recv,
                                            device_id=(x, y, 1 - c), device_id_type=MESH)

    def level1():
        cps = [to_sibling(p_in[k].at[2 * q + 1 - c], p_sib[k].at[q], p1_send.at[k, q], p1_recv.at[k, q])
               for k in range(n) for q in range(N_CHIPS)]
        return cps + [to_sibling(v_in[k], v_sib[k], v1_send.at[k], v1_recv.at[k]) for k in range(ns)]

    def to_chip(f, src, dst, send, recv):
        px, py = peers[f]
        return pltpu.make_async_remote_copy(src_ref=src, dst_ref=dst, send_sem=send, recv_sem=recv,
                                            device_id=(px, py, c), device_id_type=MESH)

    def level2(sending):
        cps = []
        for f, (px, py) in enumerate(peers):
            their_q = 2 * px + py
            for k in range(n):
                src, dst = (p_sum[k].at[their_q], p_all[k].at[my_q]) if sending else (
                    p_sum[k].at[my_q], p_all[k].at[their_q])
                cps.append(to_chip(f, src, dst, p3_send.at[k, f], p3_recv.at[k, f]))
            for k in range(ns):
                dst = v_all[k].at[my_q] if sending else v_all[k].at[their_q]
                cps.append(to_chip(f, v_sum[k], dst, v3_send.at[k, f], v3_recv.at[k, f]))
        return cps

    def start():
        for cp in level1():
            cp.start()

    def middle():
        for cp in level1():
            cp.wait_recv()
        for k in range(n):
            for q in range(N_CHIPS):
                p_sum[k][q] = (p_in[k][2 * q + c].astype(F32) + p_sib[k][q].astype(F32)).astype(p_sum[k].dtype)
        for k in range(ns):
            v_sum[k][...] = (v_in[k][...] + v_sib[k][...]).astype(v_sum[k].dtype)
        for cp in level2(True):
            cp.start()
        for k in range(n):
            p_all[k][my_q] = p_sum[k][my_q]
        for k in range(ns):
            v_all[k][my_q] = v_sum[k][...]

    def finish():
        for cp in level2(False):
            cp.wait_recv()
        for k in range(n):
            p_out[k][...] = p_all[k][...]
        for k in range(ns):
            total = v_all[k][0].astype(F32)
            for q in range(1, N_CHIPS):
                total = total + v_all[k][q].astype(F32)
            v_out[k][...] = total
        for cp in level1() + level2(True):
            cp.wait_send()

    return start, middle, finish


def _adamw_math(w, g, m, v):
    m = ADAM_B1 * m + (1.0 - ADAM_B1) * g
    v = ADAM_B2 * v + (1.0 - ADAM_B2) * (g * g)
    m_hat = m / (1.0 - ADAM_B1 ** ADAM_STEP)
    v_hat = v / (1.0 - ADAM_B2 ** ADAM_STEP)
    delta = -ADAM_LR * (m_hat / (jnp.sqrt(v_hat) + ADAM_EPS) + ADAM_WD * w)
    return delta, m, v


def _adamw_shards(parts, ws, ms, vs, steps, name):
    n = len(ws)

    def body(*refs):
        ins, outs = refs[:4 * n], refs[4 * n:]
        for k in range(n):
            p_ref, w_ref, m_ref, v_ref = ins[4 * k:4 * k + 4]
            g = p_ref[0].astype(F32)
            for j in range(1, N_CHIPS):
                g = g + p_ref[j].astype(F32)
            d, mn, vn = _adamw_math(w_ref[...], g, m_ref[...], v_ref[...])
            for o, val in zip(outs[4 * k:4 * k + 4], (g, d, mn, vn)):
                o[...] = val

    in_specs, out_specs, out_shape, operands = [], [], [], []
    for p, w, m, v in zip(parts, ws, ms, vs):
        R, Cc = w.shape
        blk = pl.BlockSpec((R // steps, Cc), lambda i: (i, 0))
        in_specs += [pl.BlockSpec((N_CHIPS, R // steps, Cc), lambda i: (0, i, 0)), blk, blk, blk]
        out_specs += [blk] * 4
        out_shape += [jax.ShapeDtypeStruct((R, Cc), F32)] * 4
        operands += [p, w, m, v]
    res = pl.pallas_call(
        body, name=name, grid=(steps,), in_specs=in_specs, out_specs=out_specs, out_shape=out_shape,
        compiler_params=pltpu.CompilerParams(dimension_semantics=("parallel",), vmem_limit_bytes=VMEM_LIMIT),
    )(*operands)
    return [res[4 * k:4 * k + 4] for k in range(n)]


def _adamw_small(grads, ws, ms, vs):
    n = len(grads)
    vm = pl.BlockSpec(memory_space=pltpu.VMEM)

    def body(*refs):
        g_in, w_in, m_in, v_in = (refs[k * n:(k + 1) * n] for k in range(4))
        g_out, d_out, m_out, v_out = (refs[(4 + k) * n:(5 + k) * n] for k in range(4))
        for k in range(n):
            if len(g_in[k].shape) > len(w_in[k].shape):
                g = g_in[k][0].astype(F32)
                for q in range(1, g_in[k].shape[0]):
                    g = g + g_in[k][q].astype(F32)
            else:
                g = g_in[k][...]
            d, mn, vn = _adamw_math(w_in[k][...], g, m_in[k][...], v_in[k][...])
            g_out[k][...] = g
            d_out[k][...] = d
            m_out[k][...] = mn
            v_out[k][...] = vn

    shapes = [jax.ShapeDtypeStruct(w.shape, F32) for w in ws]
    return pl.pallas_call(
        body, name="adamw_small", out_shape=shapes * 4, in_specs=[vm] * (4 * n), out_specs=[vm] * (4 * n),
        compiler_params=pltpu.CompilerParams(vmem_limit_bytes=VMEM_LIMIT),
    )(*grads, *ws, *ms, *vs)


def _shard_cols(full):
    R, Ct = full.shape
    return jnp.transpose(full.reshape(R, N_DEV, Ct // N_DEV), (1, 0, 2))


_COL_SHARDED = ("w_in", "w_gate", "w_up")
_BIG = ("w_in", "w_out", "w_gate", "w_up", "w_down")


def _rows(nm, p):
    return p[0].T if nm in _COL_SHARDED else p[0]


def _step(args, ts_mix_fwd, ts_ffn, ts_mix_bwd, tm_grad):
    (x, g_mix, w_in, b_in, w_dw, b_dw, ln_g, ln_b, w_pool, s_pool, w_out, g_ffn, w_gate, w_up, w_down, g_final,
     loss_target) = args[:17]
    names = ["g_mix", "w_in", "b_in", "w_dw", "b_dw", "ln_g", "ln_b", "w_pool", "s_pool", "w_out", "g_ffn",
             "w_gate", "w_up", "w_down", "g_final"]
    weights = dict(zip(names, args[1:16]))
    moms = dict(zip(names, args[17:32]))
    vars_ = dict(zip(names, args[32:47]))

    S, D = x.shape[1], x.shape[2]
    x2 = x.reshape(S, D)
    tgt2 = loss_target.reshape(S, D)

    shard = lambda nm: _rows(nm, weights[nm])
    w_pool_b = w_pool[0].astype(BF16)
    dw_rows = -(-CONV_WIDTH // SUBLANES) * SUBLANES
    dw_shard = jnp.pad(w_dw[0], ((0, dw_rows - CONV_WIDTH), (0, LANES - w_dw.shape[2])))

    a, gate, v, m, y, h1, xn, w_dw_f, g_in, g_out, _, g_gate, g_up, g_down = _mix_fwd(
        x2, g_mix, b_in, b_dw, ln_g, ln_b, w_pool_b, s_pool, [shard("w_in"), shard("w_out"), dw_shard],
        [shard("w_gate"), shard("w_up"), shard("w_down")], ts_mix_fwd)
    wt_in, w_out_f = g_in.reshape(-1, D), g_out.reshape(-1, D)
    wt_gate, wt_up, w_down_f = g_gate.reshape(-1, D), g_up.reshape(-1, D), g_down.reshape(-1, D)
    Fd = wt_gate.shape[0]
    f_chunks = [1024] * (Fd // 1024) + ([Fd % 1024] if Fd % 1024 else [])
    dh1, hn, act, dgt, dup, dh2, loss_p, dg_final, dg_ffn = _ffn(
        h1, tgt2, g_ffn, g_final.reshape(1, D), wt_gate, wt_up, w_down_f, ts_ffn, f_chunks)

    by_shard = lambda g: g.reshape(N_DEV, -1, D)
    ffn_grads, ffn_sib = _grad_ffn([dgt, dup, act], [hn, hn, dh2], tm_grad)
    pair_sums = _pair_sum([by_shard(g) for g in ffn_grads], list(ffn_sib))
    (dx, dz, dh1b, dg_mix, db_in, dw_dw, db_dw, dln_g, dln_b, dw_pool, ds_pool, r_gate, r_up, r_down) = _mix_bwd(
        dh1, x2, a, gate, v, m, g_mix, wt_in, w_dw_f, ln_g, ln_b, w_pool_b, s_pool, w_out_f, pair_sums, ts_mix_bwd)

    small_names = ["g_mix", "b_in", "b_dw", "ln_g", "ln_b", "w_pool", "s_pool", "g_ffn", "g_final"]
    small_shape = lambda p: p.reshape(-1, p.shape[-1])
    partial = [dg_mix, db_in, db_dw, dln_g, dln_b, dw_pool.reshape(-1, POOL_GROUP), ds_pool, dg_ffn, dg_final, loss_p]
    r_out, r_dw, r_in, summed = _grad_mix(y, dh1b, dz, xn, _shard_cols(dw_dw[0:CONV_WIDTH]), partial)

    big = {}
    group = ("w_gate", "w_up", "w_down")
    results = _adamw_shards([r_gate, r_up, r_down], [_rows(nm, weights[nm]) for nm in group],
                            [_rows(nm, moms[nm]) for nm in group], [_rows(nm, vars_[nm]) for nm in group], 2, "adamw_ffn")
    for nm, res in zip(group, results):
        big[nm] = [o.T if nm in _COL_SHARDED else o for o in res]

    small_names = small_names + ["w_dw", "w_in", "w_out"]
    view = lambda nm, p: _rows(nm, p) if nm in _BIG else small_shape(p)
    sm = _adamw_small(list(summed[:-1]) + [r_dw, r_in, r_out], [view(nm, weights[nm]) for nm in small_names],
                      [view(nm, moms[nm]) for nm in small_names], [view(nm, vars_[nm]) for nm in small_names])
    n_small = len(small_names)

    def result(kind, nm):
        if nm in big:
            return big[nm][kind].reshape(weights[nm].shape)
        res = sm[kind * n_small + small_names.index(nm)]
        return (res.T if nm in _COL_SHARDED else res).reshape(weights[nm].shape)

    loss = summed[-1][0, 0]
    out = [loss, dx.reshape(x.shape)]
    for kind in range(4):
        out += [result(kind, nm) for nm in names]
    return tuple(out)


def kernel(x, g_mix, w_in, b_in, w_dw, b_dw, ln_g, ln_b, w_pool, s_pool, w_out, g_ffn, w_gate, w_up, w_down, g_final, loss_target, m_g_mix, m_w_in, m_b_in, m_w_dw, m_b_dw, m_ln_g, m_ln_b, m_w_pool, m_s_pool, m_w_out, m_g_ffn, m_w_gate, m_w_up, m_w_down, m_g_final, v_g_mix, v_w_in, v_b_in, v_w_dw, v_b_dw, v_ln_g, v_ln_b, v_w_pool, v_s_pool, v_w_out, v_g_ffn, v_w_gate, v_w_up, v_w_down, v_g_final):
    args = (x, g_mix, w_in, b_in, w_dw, b_dw, ln_g, ln_b, w_pool, s_pool, w_out, g_ffn, w_gate, w_up, w_down, g_final, loss_target, m_g_mix, m_w_in, m_b_in, m_w_dw, m_b_dw, m_ln_g, m_ln_b, m_w_pool, m_s_pool, m_w_out, m_g_ffn, m_w_gate, m_w_up, m_w_down, m_g_final, v_g_mix, v_w_in, v_b_in, v_w_dw, v_b_dw, v_ln_g, v_ln_b, v_w_pool, v_s_pool, v_w_out, v_g_ffn, v_w_gate, v_w_up, v_w_down, v_g_final)
    return _step(args, ts_mix_fwd=512, ts_ffn=256, ts_mix_bwd=512, tm_grad=256)
```

```python
import jax
import jax.numpy as jnp
from jax import lax
from jax.experimental import pallas as pl
from jax.experimental.pallas import tpu as pltpu

F32 = jnp.float32
BF16 = jnp.bfloat16
MESH = pl.DeviceIdType.MESH
N_DEV = 8

C_CONV = 512
CONV_WIDTH = 31
POOL_WINDOWS = (2, 4, 8, 16)
POOL_GROUP = 128
RMS_EPS = 1e-6
LN_EPS = 1e-5

ADAM_LR = 0.001
ADAM_B1 = 0.9
ADAM_B2 = 0.999
ADAM_EPS = 1e-08
ADAM_WD = 0.01
ADAM_STEP = 10

HALO = 32
SUBLANES = 8
LANES = 128
CONV_ROWS = 64
VMEM_LIMIT = 56 * 1024 * 1024


def _dot(a, b):
    return jnp.dot(a, b, preferred_element_type=F32)


def _dot_nt(a, b):
    return lax.dot_general(a, b, (((1,), (1,)), ((), ())), preferred_element_type=F32)


def _dot_tn(a, b):
    return lax.dot_general(a, b, (((0,), (0,)), ((), ())), preferred_element_type=F32)


def _mean_last(v):
    return jnp.mean(v, axis=-1, keepdims=True)


def _full(shape):
    nd = len(shape)
    return pl.BlockSpec(shape, lambda *_: (0,) * nd)


def _full1(shape):
    nd = len(shape)
    return pl.BlockSpec(shape, lambda *_: (0,) * nd, pipeline_mode=pl.Buffered(1))


def _shifted_copies(sh_ref, rows):
    for s in range(1, SUBLANES):
        sh_ref[s, 0:rows, :] = sh_ref[0, s:s + rows, :]


def _tap(sh_ref, off, r0, rows):
    q, s = divmod(off, SUBLANES)
    return sh_ref[s, pl.ds(r0 + q * SUBLANES, rows), :]


def _window_sums(src, bufs, rows, backward):
    assert all(w == 2 << g for g, w in enumerate(POOL_WINDOWS))
    n = len(POOL_WINDOWS)
    out = []
    for level in range(n):
        dst, shift = bufs[level % 2], 1 << level
        lanes = slice(level * POOL_GROUP, n * POOL_GROUP)
        lo, hi = (SUBLANES * (level + 1), rows) if backward else (0, rows - SUBLANES * (level + 1))
        other = slice(lo - shift, hi - shift) if backward else slice(lo + shift, hi + shift)
        dst[lo:hi, lanes] = src[lo:hi, lanes] + src[other, lanes]
        src = dst
        out.append(dst)
    return out


def _mix_fwd(x, g_mix, b_in, b_dw, ln_g, ln_b, w_pool, s_pool, mix_shards, ag_shards, ts):
    S, D = x.shape
    d_in = mix_shards[0].shape[0] * N_DEV
    C = C_CONV
    nt = S // ts
    nrb = ts // CONV_ROWS
    n_ag = len(ag_shards)
    relay_step = (5 * nt) // 8
    mix_dtypes = [BF16, BF16, F32]

    def body(x_ref, g_ref, bin_ref, bdw_ref, lng_ref, lnb_ref, wp_ref, sp_ref, *rest):
        mx_in, ag_in, rest = rest[:3], rest[3:3 + n_ag], rest[3 + n_ag:]
        a_ref, gate_ref, v_ref, m_ref, y_ref, h1_ref, xn_ref, wdw_ref = rest[:8]
        mx_out, ag_out, rest = rest[8:11], rest[11:11 + n_ag], rest[11 + n_ag:]
        ush, pbuf, pa, pb, win_ref, wout_ref, gdw, load_sems = rest[:8]
        rest = rest[8:]
        (ma_stage, ma_bufs, ma_sems), rest = (rest[:2], rest[2:4], rest[4:7]), rest[7:]
        (mb_stage, mb_bufs, mb_sems), rest = (rest[:1], rest[1:2], rest[2:5]), rest[5:]
        ag_stage, ag_bufs, ag_sems = rest[:n_ag], rest[n_ag:2 * n_ag], rest[2 * n_ag:]
        i = pl.program_id(0)
        ma_start, ma_relay, ma_pass_on, ma_finish = _gather_plan(
            mx_in[0::2], mx_out[0::2], ma_stage, ma_bufs, *ma_sems)
        mb_start, mb_relay, mb_pass_on, mb_finish = _gather_plan(
            mx_in[1:2], mx_out[1:2], mb_stage, mb_bufs, *mb_sems)
        ag_start, ag_relay, ag_pass_on, ag_finish = _gather_plan(ag_in, ag_out, ag_stage, ag_bufs, *ag_sems)

        def load_weight(k, dst_ref):
            rows = mx_out[k].shape[1]
            loads = [pltpu.make_async_copy(mx_out[k].at[j], dst_ref.at[pl.ds(j * rows, rows), :],
                                           load_sems.at[N_DEV * k + j]) for j in range(N_DEV)]
            for cp in loads:
                cp.start()
            for cp in loads:
                cp.wait()

        @pl.when(i == 0)
        def _():
            ma_start()
            mb_start()
            ma_relay()
            mb_relay()
            ag_start()
            ma_pass_on()
            ma_finish()
            load_weight(0, win_ref)
            dw_load = pltpu.make_async_copy(mx_out[2], gdw, load_sems.at[2 * N_DEV])
            dw_load.start()
            dw_load.wait()
            first_half = lax.broadcasted_iota(jnp.int32, gdw.shape[1:], 1) < C // N_DEV
            for p in range(N_DEV // 2):
                wdw_ref[:, LANES * p:LANES * (p + 1)] = jnp.where(
                    first_half, gdw[2 * p], pltpu.roll(gdw[2 * p + 1], C // N_DEV, axis=1))
            ush[0, 0:HALO, :] = jnp.zeros((HALO, C), F32)
            pbuf[0:HALO, :] = jnp.zeros((HALO, C), F32)

        @pl.when(i == relay_step)
        def _():
            ag_relay()

        @pl.when(i == nt - 1)
        def _():
            ag_pass_on()

        x = x_ref[...]
        r1 = lax.rsqrt(_mean_last(x * x) + RMS_EPS)
        xn = (x * r1 * g_ref[...]).astype(BF16)
        xn_ref[...] = xn
        z = _dot_nt(xn, win_ref[...]) + bin_ref[...]
        a = z[:, 0:C]
        gate = z[:, C:2 * C]
        a_ref[...] = a
        gate_ref[...] = gate
        ush[0, HALO:HALO + ts, :] = a * jax.nn.sigmoid(gate)
        pbuf[HALO:HALO + ts, :] = z[:, 2 * C:]

        _shifted_copies(ush, ts + HALO - SUBLANES)

        def conv_block(rb, carry):
            r0 = pl.multiple_of(rb * CONV_ROWS, CONV_ROWS)
            acc = jnp.zeros((CONV_ROWS, C), F32)
            for k in range(CONV_WIDTH):
                acc = acc + wdw_ref[k:k + 1, :] * _tap(ush, HALO - (CONV_WIDTH - 1) + k, r0, CONV_ROWS)
            v_ref[pl.ds(r0, CONV_ROWS), :] = acc + bdw_ref[...]
            return carry

        lax.fori_loop(0, nrb, conv_block, 0)

        v = v_ref[...]
        mu = _mean_last(v)
        xc = v - mu
        rstd = lax.rsqrt(_mean_last(xc * xc) + LN_EPS)
        ln = xc * rstd * lng_ref[...] + lnb_ref[...]
        y_ref[:, 0:C] = (ln * jax.nn.sigmoid(ln)).astype(BF16)

        sums = _window_sums(pbuf, (pa, pb), ts + HALO, backward=True)
        row = lax.broadcasted_iota(jnp.int32, (ts, 1), 0) + i * ts
        for gi, w in enumerate(POOL_WINDOWS):
            lanes = slice(gi * POOL_GROUP, (gi + 1) * POOL_GROUP)
            seg = pbuf[HALO:HALO + ts, lanes]
            ws = sums[gi][HALO:HALO + ts, lanes]
            cnt = jnp.minimum(row + 1, w).astype(F32)
            m = (ws / cnt - seg).astype(BF16)
            m_ref[:, lanes] = m
            ypre = _dot(m, wp_ref[gi])
            y_ref[:, C + gi * POOL_GROUP:C + (gi + 1) * POOL_GROUP] = (ypre * sp_ref[:, lanes]).astype(BF16)

        @pl.when(i == 0)
        def _():
            mb_pass_on()
            mb_finish()
            load_weight(1, wout_ref)

        h1_ref[...] = x + _dot(y_ref[...], wout_ref[...])

        ush[0, 0:HALO, :] = ush[0, ts:ts + HALO, :]
        pbuf[0:HALO, :] = pbuf[ts:ts + HALO, :]

        @pl.when(i == nt - 1)
        def _():
            ag_finish()

    tile = lambda w, dt: (pl.BlockSpec((ts, w), lambda i: (i, 0)), jax.ShapeDtypeStruct((S, w), dt))
    wdw_rows = mix_shards[2].shape[0]
    outs = [tile(C, F32), tile(C, F32), tile(C, F32), tile(C, BF16), tile(D, BF16), tile(D, F32), tile(D, BF16),
            (_full((wdw_rows, C)), jax.ShapeDtypeStruct((wdw_rows, C), F32))]
    return pl.pallas_call(
        body, name="mix_fwd", grid=(nt,),
        in_specs=[pl.BlockSpec((ts, D), lambda i: (i, 0)), _full((1, D)), _full((1, d_in)), _full((1, C)),
                  _full((1, C)), _full((1, C)), _full(w_pool.shape), _full((1, C))] + [_HBM] * (3 + n_ag),
        out_specs=[o[0] for o in outs] + [_HBM] * (3 + n_ag),
        out_shape=[o[1] for o in outs] + _gather_out_shapes(mix_shards, mix_dtypes)
        + _gather_out_shapes(ag_shards, [BF16] * n_ag),
        scratch_shapes=[pltpu.VMEM((SUBLANES, ts + HALO, C), F32)] + [pltpu.VMEM((ts + HALO, C), F32)] * 3
        + [pltpu.VMEM((d_in, D), BF16), pltpu.VMEM((D, D), BF16),
                        pltpu.VMEM((N_DEV,) + mix_shards[2].shape, F32), pltpu.SemaphoreType.DMA((2 * N_DEV + 1,))]
        + _gather_scratch(mix_shards[0::2], mix_dtypes[0::2]) + _gather_scratch(mix_shards[1:2], mix_dtypes[1:2])
        + _gather_scratch(ag_shards, [BF16] * n_ag),
        compiler_params=pltpu.CompilerParams(dimension_semantics=("arbitrary",), vmem_limit_bytes=VMEM_LIMIT),
    )(x, g_mix, b_in, b_dw, ln_g, ln_b, w_pool, s_pool, *mix_shards, *ag_shards)


def _ffn(h1, target, g_ffn, g_final, w_gate, w_up, w_down, ts, f_chunks):
    S, D = h1.shape
    Fd = w_gate.shape[0]
    nt = S // ts
    bounds = []
    lo = 0
    for n in f_chunks:
        bounds.append((lo, lo + n))
        lo += n
    assert lo == Fd

    def body(h1_ref, tgt_ref, gf_ref, gl_ref, wg_ref, wu_ref, wd_ref,
             dh1_ref, hn_ref, act_ref, dgt_ref, dup_ref, dh2_ref, loss_ref, dgl_ref, dgf_ref, gt_s, up_s):
        i = pl.program_id(0)

        @pl.when(i == 0)
        def _():
            loss_ref[...] = jnp.zeros_like(loss_ref)
            dgl_ref[...] = jnp.zeros_like(dgl_ref)
            dgf_ref[...] = jnp.zeros_like(dgf_ref)

        h1 = h1_ref[...]
        r2 = lax.rsqrt(_mean_last(h1 * h1) + RMS_EPS)
        hhat = h1 * r2
        hn = (hhat * gf_ref[...]).astype(BF16)
        hn_ref[...] = hn
        h2 = h1
        for lo, hi in bounds:
            gt = _dot_nt(hn, wg_ref[lo:hi, :])
            up = _dot_nt(hn, wu_ref[lo:hi, :])
            gt_s[:, lo:hi] = gt
            up_s[:, lo:hi] = up
            act = (gt * jax.nn.sigmoid(gt) * up).astype(BF16)
            act_ref[:, lo:hi] = act
            h2 = h2 + _dot(act, wd_ref[lo:hi, :])

        r3 = lax.rsqrt(_mean_last(h2 * h2) + RMS_EPS)
        n3 = h2 * r3
        gl = gl_ref[...]
        diff = n3 * gl - tgt_ref[...]
        loss_ref[...] += jnp.sum(0.5 * jnp.sum(diff * diff, axis=-1, keepdims=True) / D, axis=0, keepdims=True)
        dout = diff / D
        dgl_ref[...] += jnp.sum(dout * n3, axis=0, keepdims=True)
        dn = dout * gl
        dh2 = r3 * (dn - n3 * _mean_last(dn * n3))
        dh2b = dh2.astype(BF16)
        dh2_ref[...] = dh2b

        dhn = jnp.zeros((ts, D), F32)
        for lo, hi in bounds:
            gt = gt_s[:, lo:hi]
            up = up_s[:, lo:hi]
            sg = jax.nn.sigmoid(gt)
            dact = _dot_nt(dh2b, wd_ref[lo:hi, :])
            dgt = (dact * up * (sg * (1.0 + gt * (1.0 - sg)))).astype(BF16)
            dup = (dact * (gt * sg)).astype(BF16)
            dgt_ref[:, lo:hi] = dgt
            dup_ref[:, lo:hi] = dup
            dhn = dhn + _dot(dgt, wg_ref[lo:hi, :]) + _dot(dup, wu_ref[lo:hi, :])

        dgf_ref[...] += jnp.sum(dhn * hhat, axis=0, keepdims=True)
        dnn = dhn * gf_ref[...]
        dh1_ref[...] = dh2 + r2 * (dnn - hhat * _mean_last(dnn * hhat))

    tile = lambda w, dt: (pl.BlockSpec((ts, w), lambda i: (i, 0)), jax.ShapeDtypeStruct((S, w), dt))
    acc = lambda w: (_full((1, w)), jax.ShapeDtypeStruct((1, w), F32))
    outs = [tile(D, F32), tile(D, BF16), tile(Fd, BF16), tile(Fd, BF16), tile(Fd, BF16), tile(D, BF16),
            acc(LANES), acc(D), acc(D)]
    return pl.pallas_call(
        body, name="ffn_fwd_bwd", grid=(nt,),
        in_specs=[pl.BlockSpec((ts, D), lambda i: (i, 0)), pl.BlockSpec((ts, D), lambda i: (i, 0)),
                  _full((1, D)), _full((1, D)), _full1((Fd, D)), _full1((Fd, D)), _full1((Fd, D))],
        out_specs=[o[0] for o in outs], out_shape=[o[1] for o in outs],
        scratch_shapes=[pltpu.VMEM((ts, Fd), F32), pltpu.VMEM((ts, Fd), F32)],
        compiler_params=pltpu.CompilerParams(dimension_semantics=("arbitrary",), vmem_limit_bytes=VMEM_LIMIT),
    )(h1, target, g_ffn, g_final, w_gate, w_up, w_down)


def _mix_bwd(dh1, x, a, gate, v, m, g_mix, w_in, w_dw, ln_g, ln_b, w_pool, s_pool, w_out, rs_parts, ts):
    S, D = x.shape
    n_rs = len(rs_parts)
    d_in = w_in.shape[0]
    C = C_CONV
    nt = S // ts
    nrb = ts // CONV_ROWS
    wrows =((CONV_WIDTH + SUBLANES - 1) // SUBLANES) * SUBLANES

    def body(dh1_ref, x_ref, a_ref, gate_ref, v_ref, m_ref, g_ref, win_ref, wdw_ref, lng_ref,
             lnb_ref, wp_ref, sp_ref, wout_ref, *rest):
        rs_in, rest = rest[:n_rs], rest[n_rs:]
        (dx_ref, dz_ref, dh1b_ref, dgm_ref, dbin_ref, dwdw_ref, dbdw_ref, dlng_ref, dlnb_ref, dwp_ref,
         dsp_ref) = rest[:11]
        rs_out, rest = rest[11:11 + n_rs], rest[11 + n_rs:]
        dvsh, dqbuf, qa, qb, du_s, dm_s = rest[:6]
        rs_bufs, (send_sems, recv_sems, local_sems) = rest[6:6 + n_rs], rest[6 + n_rs:]
        i = pl.program_id(0)
        t = nt - 1 - i
        rs_start, rs_finish = _chip_exchange_plan(rs_in, rs_out, rs_bufs, send_sems, recv_sems, local_sems)

        @pl.when(i == 0)
        def _():
            rs_start()
            dvsh[0, ts:ts + HALO, :] = jnp.zeros((HALO, C), F32)
            dqbuf[ts:ts + HALO, :] = jnp.zeros((HALO, C), F32)
            for r in (dgm_ref, dbin_ref, dwdw_ref, dbdw_ref, dlng_ref, dlnb_ref, dwp_ref, dsp_ref):
                r[...] = jnp.zeros_like(r)

        dh1 = dh1_ref[...]
        dh1b = dh1.astype(BF16)
        dh1b_ref[...] = dh1b
        dy = _dot_nt(dh1b, wout_ref[...])

        v = v_ref[...]
        mu = _mean_last(v)
        xc = v - mu
        rstd = lax.rsqrt(_mean_last(xc * xc) + LN_EPS)
        vhat = xc * rstd
        lng = lng_ref[...]
        ln = vhat * lng + lnb_ref[...]
        sg = jax.nn.sigmoid(ln)
        dln = dy[:, 0:C] * (sg * (1.0 + ln * (1.0 - sg)))
        dlng_ref[...] += jnp.sum(dln * vhat, axis=0, keepdims=True)
        dlnb_ref[...] += jnp.sum(dln, axis=0, keepdims=True)
        dvh = dln * lng
        dv = rstd * (dvh - _mean_last(dvh) - vhat * _mean_last(dvh * vhat))
        dbdw_ref[...] += jnp.sum(dv, axis=0, keepdims=True)
        dvsh[0, 0:ts, :] = dv
        _shifted_copies(dvsh, ts + HALO - SUBLANES)

        def conv_block(rb, carry):
            r0 = pl.multiple_of(rb * CONV_ROWS, CONV_ROWS)
            acc = jnp.zeros((CONV_ROWS, C), F32)
            for k in range(CONV_WIDTH):
                acc = acc + wdw_ref[k:k + 1, :] * _tap(dvsh, CONV_WIDTH - 1 - k, r0, CONV_ROWS)
            du_s[pl.ds(r0, CONV_ROWS), :] = acc
            return carry

        lax.fori_loop(0, nrb, conv_block, 0)

        a = a_ref[...]
        sgate = jax.nn.sigmoid(gate_ref[...])
        u = a * sgate
        for k in range(CONV_WIDTH):
            q, s = divmod(CONV_WIDTH - 1 - k, SUBLANES)
            prod = u * dvsh[s, q * SUBLANES:q * SUBLANES + ts, :]
            dwdw_ref[k:k + 1, :] += jnp.sum(prod, axis=0, keepdims=True)

        du = du_s[...]
        da = du * sgate
        dgate = du * a * sgate * (1.0 - sgate)
        dz_ref[:, 0:C] = da.astype(BF16)
        dz_ref[:, C:2 * C] = dgate.astype(BF16)
        dbin_ref[:, 0:C] += jnp.sum(da, axis=0, keepdims=True)
        dbin_ref[:, C:2 * C] += jnp.sum(dgate, axis=0, keepdims=True)

        row = lax.broadcasted_iota(jnp.int32, (ts, 1), 0) + t * ts
        for gi, w in enumerate(POOL_WINDOWS):
            lanes = slice(gi * POOL_GROUP, (gi + 1) * POOL_GROUP)
            dyp = dy[:, C + gi * POOL_GROUP:C + (gi + 1) * POOL_GROUP]
            mg = m_ref[:, lanes]
            ypre = _dot(mg, wp_ref[gi])
            dsp_ref[:, lanes] += jnp.sum(dyp * ypre, axis=0, keepdims=True)
            dyi = (dyp * sp_ref[:, lanes]).astype(BF16)
            dwp_ref[gi] += _dot_tn(mg, dyi)
            dm = _dot_nt(dyi, wp_ref[gi])
            cnt = jnp.minimum(row + 1, w).astype(F32)
            dqbuf[0:ts, lanes] = dm / cnt
            dm_s[:, lanes] = dm
        sums = _window_sums(dqbuf, (qa, qb), ts + HALO, backward=False)
        for gi in range(len(POOL_WINDOWS)):
            lanes = slice(gi * POOL_GROUP, (gi + 1) * POOL_GROUP)
            dp = sums[gi][0:ts, lanes] - dm_s[:, lanes]
            dz_ref[:, 2 * C + gi * POOL_GROUP:2 * C + (gi + 1) * POOL_GROUP] = dp.astype(BF16)
            dbin_ref[:, 2 * C + gi * POOL_GROUP:2 * C + (gi + 1) * POOL_GROUP] += jnp.sum(dp, axis=0, keepdims=True)

        dxn = _dot(dz_ref[...], win_ref[...])
        x = x_ref[...]
        r1 = lax.rsqrt(_mean_last(x * x) + RMS_EPS)
        xhat = x * r1
        dgm_ref[...] += jnp.sum(dxn * xhat, axis=0, keepdims=True)
        dnn = dxn * g_ref[...]
        dx_ref[...] = dh1 + r1 * (dnn - xhat * _mean_last(dnn * xhat))

        dvsh[0, ts:ts + HALO, :] = dvsh[0, 0:HALO, :]
        dqbuf[ts:ts + HALO, :] = dqbuf[0:HALO, :]

        @pl.when(i == nt - 1)
        def _():
            rs_finish()

    rev = lambda w: pl.BlockSpec((ts, w), lambda i: (nt - 1 - i, 0))
    acc = lambda shape: (_full(shape), jax.ShapeDtypeStruct(shape, F32))
    outs = [(rev(D), jax.ShapeDtypeStruct((S, D), F32)), (rev(d_in), jax.ShapeDtypeStruct((S, d_in), BF16)),
            (rev(D), jax.ShapeDtypeStruct((S, D), BF16)), acc((1, D)), acc((1, d_in)), acc((wrows, C)),
            acc((1, C)), acc((1, C)), acc((1, C)), acc(w_pool.shape), acc((1, C))]
    return pl.pallas_call(
        body, name="mix_bwd", grid=(nt,),
        in_specs=[rev(D), rev(D), rev(C), rev(C), rev(C), rev(C), _full((1, D)), _full((d_in, D)),
                  _full(w_dw.shape), _full((1, C)), _full((1, C)), _full(w_pool.shape), _full((1, C)),
                  _full((D, D))] + [_HBM] * n_rs,
        out_specs=[o[0] for o in outs] + [_HBM] * n_rs,
        out_shape=[o[1] for o in outs] + [jax.ShapeDtypeStruct(p.shape, p.dtype) for p in rs_parts],
        scratch_shapes=[pltpu.VMEM((SUBLANES, ts + HALO, C), F32)] + [pltpu.VMEM((ts + HALO, C), F32)] * 3
        + [pltpu.VMEM((ts, C), F32)] * 2
        + [pltpu.VMEM(p.shape[1:], p.dtype) for p in rs_parts] + _comm_sems(n_rs, 3),
        compiler_params=pltpu.CompilerParams(dimension_semantics=("arbitrary",), vmem_limit_bytes=VMEM_LIMIT),
    )(dh1, x, a, gate, v, m, g_mix, w_in, w_dw, ln_g, ln_b, w_pool, s_pool, w_out, *rs_parts)


def _grad_mix(y, dh1b, dz, xn, dw_dw, vals):
    S, D = y.shape
    d_in = dz.shape[1]
    r_out, r_in = D // N_DEV, d_in // N_DEV
    n_out, n_in = N_DEV // 2, N_DEV // 2
    steps = n_out + n_in
    ns = len(vals)
    dw_out = jax.ShapeDtypeStruct((N_DEV, r_out, D), BF16)
    dw_in = jax.ShapeDtypeStruct((N_DEV, r_in, D), BF16)
    s_scratch, o_scratch = _reduce_scratch([], vals), _reduce_scratch([dw_out, dw_dw], [])
    i_scratch = _reduce_scratch([dw_in], [])

    def body(y_ref, dh_ref, dz_ref, xn_ref, dwdw_ref, *rest):
        v_in, p_out, v_out = rest[:ns], rest[ns:ns + 3], rest[ns + 3:2 * ns + 3]
        rest = rest[2 * ns + 3:]
        dwo_v, dwi_v, rest = rest[0], rest[1], rest[2:]
        s_refs, rest = rest[:len(s_scratch)], rest[len(s_scratch):]
        o_refs, i_refs = rest[:len(o_scratch)], rest[len(o_scratch):]
        s_start, s_middle, s_finish = _reduce_plan((), v_in, (), v_out, s_refs)
        o_start, o_middle, o_finish = _reduce_plan((dwo_v, dwdw_ref), (), p_out[:2], (), o_refs)
        i_start, i_middle, i_finish = _reduce_plan((dwi_v,), (), p_out[2:], (), i_refs)
        s = pl.program_id(0)
        pl.when(s == 0)(s_start)
        pl.when(s == 1)(s_middle)

        @pl.when(s < n_out)
        def _():
            res = _dot_tn(y_ref[...], dh_ref[...]).astype(BF16)
            dwo_v[2 * s] = res[0:r_out]
            dwo_v[2 * s + 1] = res[r_out:]

        pl.when(s == n_out - 1)(s_finish)
        pl.when(s == n_out)(o_start)
        pl.when(s == n_out + 1)(o_middle)

        @pl.when(s >= n_out)
        def _():
            res = _dot_tn(dz_ref[...], xn_ref[...]).astype(BF16)
            dwi_v[2 * (s - n_out)] = res[0:r_in]
            dwi_v[2 * (s - n_out) + 1] = res[r_in:]

        @pl.when(s == steps - 1)
        def _():
            o_finish()
            i_start()
            i_middle()
            i_finish()

    late = lambda s: jnp.clip(s - n_out, 0, n_in - 1)
    res = pl.pallas_call(
        body, name="grad_w_mix", grid=(steps,),
        in_specs=[pl.BlockSpec((S, 2 * r_out), lambda s: (0, jnp.minimum(s, n_out - 1))), _full1((S, D)),
                  pl.BlockSpec((S, 2 * r_in), lambda s: (0, late(s))), _full1((S, D))] + [_VMEM] * (1 + ns),
        out_specs=[_VMEM] * (3 + ns),
        out_shape=_reduce_out_shapes([dw_out, dw_dw, dw_in], []) + _reduce_out_shapes([], vals),
        scratch_shapes=[pltpu.VMEM(dw_out.shape, BF16), pltpu.VMEM(dw_in.shape, BF16)] + s_scratch + o_scratch
        + i_scratch,
        compiler_params=pltpu.CompilerParams(dimension_semantics=("arbitrary",), vmem_limit_bytes=VMEM_LIMIT),
    )(y, dh1b, dz, xn, dw_dw, *vals)
    return res[0], res[1], res[2], res[3:]


def _grad_ffn(a_list, b_list, tm):
    n = len(a_list)
    S, M = a_list[0].shape
    N = b_list[0].shape[1]
    nb = M // tm
    steps = n * nb
    R = M // N_DEV
    b_unique = [b for k, b in enumerate(b_list) if all(b is not o for o in b_list[:k])]
    b_index = [[b is o for o in b_unique].index(True) for b in b_list]

    def body(*refs):
        a_refs, b_refs = refs[:n], refs[n:n + len(b_unique)]
        rest = refs[n + len(b_unique):]
        g_refs, sib_refs, (obuf, w_sems, send_sems, recv_sems) = rest[:n], rest[n:2 * n], rest[2 * n:]
        s = pl.program_id(0)
        slot = s % 2
        x, y, c = _position()

        def write_back(dst):
            return pltpu.make_async_copy(obuf.at[slot], dst, w_sems.at[slot])

        def exchange(k):
            return [pltpu.make_async_remote_copy(
                src_ref=g_refs[k].at[pl.ds(pl.multiple_of((2 * q + 1 - c) * R, SUBLANES * 2), R), :],
                dst_ref=sib_refs[k].at[q], send_sem=send_sems.at[k, q], recv_sem=recv_sems.at[k, q],
                device_id=(x, y, 1 - c), device_id_type=MESH) for q in range(N_CHIPS)]

        pl.when(s >= 2)(write_back(g_refs[0].at[pl.ds(0, tm), :]).wait)
        for k in range(n):
            @pl.when(jnp.logical_and(s >= k * nb, s < (k + 1) * nb))
            def _(k=k):
                obuf[slot] = _dot_tn(a_refs[k][...], b_refs[b_index[k]][...]).astype(BF16)
                row = pl.multiple_of((s - k * nb) * tm, tm)
                write_back(g_refs[k].at[pl.ds(row, tm), :]).start()
            if k < n - 1:
                @pl.when(s == (k + 1) * nb + 2)
                def _(k=k):
                    for cp in exchange(k):
                        cp.start()

        @pl.when(s == steps - 1)
        def _():
            write_back(g_refs[0].at[pl.ds(0, tm), :]).wait()
            pltpu.make_async_copy(obuf.at[1 - slot], g_refs[0].at[pl.ds(0, tm), :], w_sems.at[1 - slot]).wait()
            for cp in exchange(n - 1):
                cp.start()
            for k in range(n):
                for cp in exchange(k):
                    cp.wait()

    clamp = lambda k: (lambda s: (0, jnp.clip(s - k * nb, 0, nb - 1)))
    sems = pltpu.SemaphoreType.DMA((n, N_CHIPS))
    res = pl.pallas_call(
        body, name="grad_w_ffn", grid=(steps,),
        in_specs=[pl.BlockSpec((S, tm), clamp(k)) for k in range(n)] + [_full1((S, N))] * len(b_unique),
        out_specs=[_HBM] * (2 * n),
        out_shape=[jax.ShapeDtypeStruct((M, N), BF16)] * n + [jax.ShapeDtypeStruct((N_CHIPS, R, N), BF16)] * n,
        scratch_shapes=[pltpu.VMEM((2, tm, N), BF16), pltpu.SemaphoreType.DMA((2,)), sems, sems],
        compiler_params=pltpu.CompilerParams(dimension_semantics=("arbitrary",), vmem_limit_bytes=VMEM_LIMIT),
    )(*a_list, *b_unique)
    return res[:n], res[n:]


def _position():
    return lax.axis_index("x"), lax.axis_index("y"), lax.axis_index("c")


def _slot(px, py, pc):
    return 4 * px + 2 * py + pc


_HBM = pl.BlockSpec(memory_space=pl.ANY)


def _comm_sems(n, copies):
    return [pltpu.SemaphoreType.DMA((n, copies)), pltpu.SemaphoreType.DMA((n, copies)),
            pltpu.SemaphoreType.DMA((n, 2))]


def _local_copy(srcs, dsts, bufs, local_sems):
    n = len(srcs)
    loads = [pltpu.make_async_copy(srcs[k], bufs[k], local_sems.at[k, 0]) for k in range(n)]
    for cp in loads:
        cp.start()
    for cp in loads:
        cp.wait()
    stores = _local_stores(dsts, bufs, local_sems)
    for cp in stores:
        cp.start()
    return stores


def _local_stores(dsts, bufs, local_sems):
    return [pltpu.make_async_copy(bufs[k], dsts[k], local_sems.at[k, 1]) for k in range(len(dsts))]


def _gather_out_shapes(shards, dtypes):
    return [jax.ShapeDtypeStruct((N_DEV,) + s.shape, dt) for s, dt in zip(shards, dtypes)]


def _gather_scratch(shards, dtypes):
    return ([pltpu.VMEM(s.shape, s.dtype) for s in shards] + [pltpu.VMEM(s.shape, dt) for s, dt in zip(shards, dtypes)]
            + _comm_sems(len(shards), 7))


def _gather_plan(ins, outs, stage, bufs, send_sems, recv_sems, local_sems):
    n = len(ins)
    x, y, c = _position()
    me, sibling = (x, y, c), (x, y, 1 - c)
    na, nb, dg = (x ^ (1 - c), y ^ c), (x ^ c, y ^ (1 - c)), (1 - x, 1 - y)
    own = [outs[k].at[_slot(*me)] for k in range(n)]

    def copy(k, sem, block, to, src=None):
        dst = outs[k].at[_slot(*block)]
        return pltpu.make_async_remote_copy(
            src_ref=dst if src is None else src, dst_ref=dst, send_sem=send_sems.at[k, sem],
            recv_sem=recv_sems.at[k, sem], device_id=to, device_id_type=MESH)

    def first():
        cps = []
        for k in range(n):
            cps += [copy(k, 0, me, sibling, src=bufs[k]), copy(k, 1, me, (*na, c), src=bufs[k]),
                    copy(k, 2, me, (*nb, c), src=bufs[k])]
        return cps

    def onward():
        return [copy(k, 3, (*na, c), (*nb, c)) for k in range(n)]

    def to_sibling(j, chip):
        return [copy(k, 4 + j, (*chip, c), sibling) for k in range(n)]

    def start():
        loads = [pltpu.make_async_copy(ins[k], stage[k], local_sems.at[k, 0]) for k in range(n)]
        for cp in loads:
            cp.start()
        for cp in loads:
            cp.wait()
        for k in range(n):
            bufs[k][...] = stage[k][...].astype(bufs[k].dtype)
        for cp in first() + _local_stores(own, bufs, local_sems):
            cp.start()

    def relay():
        for k in range(n):
            copy(k, 1, (*na, c), me).wait_recv()
        for cp in onward() + to_sibling(0, na):
            cp.start()
        for k in range(n):
            copy(k, 2, (*nb, c), me).wait_recv()
        for cp in to_sibling(1, nb):
            cp.start()

    def pass_on():
        for k in range(n):
            copy(k, 3, (*dg, c), me).wait_recv()
        for cp in to_sibling(2, dg):
            cp.start()

    def finish():
        for k in range(n):
            copy(k, 0, sibling, me).wait_recv()
            for j, chip in enumerate((nb, na, dg)):
                copy(k, 4 + j, (*chip, 1 - c), me).wait_recv()
        for cp in first() + onward() + to_sibling(0, na) + to_sibling(1, nb) + to_sibling(2, dg):
            cp.wait_send()
        for cp in _local_stores(own, bufs, local_sems):
            cp.wait()

    return start, relay, pass_on, finish


N_CHIPS = 4
_CHIP_FLIPS = [(1, 0), (0, 1), (1, 1)]
F32_TRAVEL_LIMIT = 4096


def _chip_exchange_plan(ins, outs, bufs, send_sems, recv_sems, local_sems):
    n = len(ins)
    x, y, c = _position()
    my_q = 2 * x + y
    peers = [(x ^ fx, y ^ fy) for fx, fy in _CHIP_FLIPS]
    own = [outs[k].at[my_q] for k in range(n)]

    def sends():
        return [pltpu.make_async_remote_copy(
            src_ref=ins[k].at[2 * px + py], dst_ref=outs[k].at[my_q], send_sem=send_sems.at[k, f],
            recv_sem=recv_sems.at[k, f], device_id=(px, py, c), device_id_type=MESH)
            for f, (px, py) in enumerate(peers) for k in range(n)]

    def start():
        for cp in sends():
            cp.start()
        _local_copy([ins[k].at[my_q] for k in range(n)], own, bufs, local_sems)

    def finish():
        for f, (px, py) in enumerate(peers):
            for k in range(n):
                pltpu.make_async_remote_copy(
                    src_ref=ins[k].at[my_q], dst_ref=outs[k].at[2 * px + py], send_sem=send_sems.at[k, f],
                    recv_sem=recv_sems.at[k, f], device_id=(px, py, c), device_id_type=MESH).wait_recv()
        for cp in sends():
            cp.wait_send()
        for cp in _local_stores(own, bufs, local_sems):
            cp.wait()

    return start, finish


def _pair_sum(parts, sib):
    n = len(parts)

    def body(core_ref, *refs):
        for k in range(n):
            refs[2 * n + k][0] = (refs[k][0, 0].astype(F32) + refs[n + k][0].astype(F32)).astype(BF16)

    own = [pl.BlockSpec((1, 1) + p.shape[1:], lambda q, core: (q, core[0], 0, 0)) for p in parts]
    one = [pl.BlockSpec((1,) + p.shape[1:], lambda q, core: (q, 0, 0)) for p in parts]
    return pl.pallas_call(
        body, name="pair_sum",
        grid_spec=pltpu.PrefetchScalarGridSpec(num_scalar_prefetch=1, grid=(N_CHIPS,), in_specs=own + one,
                                               out_specs=one),
        out_shape=[jax.ShapeDtypeStruct(s.shape, BF16) for s in sib],
        compiler_params=pltpu.CompilerParams(dimension_semantics=("parallel",)),
    )(lax.axis_index("c").astype(jnp.int32).reshape(1), *[p.reshape((N_CHIPS, 2) + p.shape[1:]) for p in parts], *sib)


_VMEM = pl.BlockSpec(memory_space=pltpu.VMEM)


def _reduce_out_shapes(parts, vals):
    return ([jax.ShapeDtypeStruct((N_CHIPS,) + p.shape[1:], p.dtype) for p in parts]
            + [jax.ShapeDtypeStruct(v.shape, v.dtype) for v in vals])


def _reduce_scratch(parts, vals):
    n, ns = len(parts), len(vals)
    quarter = [pltpu.VMEM((N_CHIPS,) + p.shape[1:], p.dtype) for p in parts]
    dma = pltpu.SemaphoreType.DMA
    travel = [BF16 if v.size > F32_TRAVEL_LIMIT else v.dtype for v in vals]
    return (quarter * 3 + [pltpu.VMEM(v.shape, v.dtype) for v in vals]
            + [pltpu.VMEM(v.shape, t) for v, t in zip(vals, travel)]
            + [pltpu.VMEM((N_CHIPS,) + v.shape, t) for v, t in zip(vals, travel)]
            + [dma((max(n, 1), N_CHIPS)), dma((max(n, 1), N_CHIPS)), dma((max(ns, 1),)), dma((max(ns, 1),)),
               dma((max(n, 1), 3)), dma((max(n, 1), 3)), dma((max(ns, 1), 3)), dma((max(ns, 1), 3))])


def _reduce_plan(p_in, v_in, p_out, v_out, scratch):
    n, ns = len(p_in), len(v_in)
    p_sib, p_sum, p_all = scratch[:n], scratch[n:2 * n], scratch[2 * n:3 * n]
    v_sib, v_sum, v_all = (scratch[3 * n + j * ns:3 * n + (j + 1) * ns] for j in range(3))
    p1_send, p1_recv, v1_send, v1_recv, p3_send, p3_recv, v3_send, v3_recv = scratch[3 * n + 3 * ns:]
    x, y, c = _position()
    my_q = 2 * x + y
    peers = [(x ^ fx, y ^ fy) for fx, fy in _CHIP_FLIPS]

    def to_sibling(src, dst, send, recv):
        return pltpu.make_async_remote_copy(src_ref=src, dst_ref=dst, send_sem=send, recv_sem=recv,
                                            device_id=(x, y, 1 - c), device_id_type=MESH)

    def level1():
        cps = [to_sibling(p_in[k].at[2 * q + 1 - c], p_sib[k].at[q], p1_send.at[k, q], p1_recv.at[k, q])
               for k in range(n) for q in range(N_CHIPS)]
        return cps + [to_sibling(v_in[k], v_sib[k], v1_send.at[k], v1_recv.at[k]) for k in range(ns)]

    def to_chip(f, src, dst, send, recv):
        px, py = peers[f]
        return pltpu.make_async_remote_copy(src_ref=src, dst_ref=dst, send_sem=send, recv_sem=recv,
                                            device_id=(px, py, c), device_id_type=MESH)

    def level2(sending):
        cps = []
        for f, (px, py) in enumerate(peers):
            their_q = 2 * px + py
            for k in range(n):
                src, dst = (p_sum[k].at[their_q], p_all[k].at[my_q]) if sending else (
                    p_sum[k].at[my_q], p_all[k].at[their_q])
                cps.append(to_chip(f, src, dst, p3_send.at[k, f], p3_recv.at[k, f]))
            for k in range(ns):
                dst = v_all[k].at[my_q] if sending else v_all[k].at[their_q]
                cps.append(to_chip(f, v_sum[k], dst, v3_send.at[k, f], v3_recv.at[k, f]))
        return cps

    def start():
        for cp in level1():
            cp.start()

    def middle():
        for cp in level1():
            cp.wait_recv()
        for k in range(n):
            for q in range(N_CHIPS):
                p_sum[k][q] = (p_in[k][2 * q + c].astype(F32) + p_sib[k][q].astype(F32)).astype(p_sum[k].dtype)
        for k in range(ns):
            v_sum[k][...] = (v_in[k][...] + v_sib[k][...]).astype(v_sum[k].dtype)
        for cp in level2(True):
            cp.start()
        for k in range(n):
            p_all[k][my_q] = p_sum[k][my_q]
        for k in range(ns):
            v_all[k][my_q] = v_sum[k][...]

    def finish():
        for cp in level2(False):
            cp.wait_recv()
        for k in range(n):
            p_out[k][...] = p_all[k][...]
        for k in range(ns):
            total = v_all[k][0].astype(F32)
            for q in range(1, N_CHIPS):
                total = total + v_all[k][q].astype(F32)
            v_out[k][...] = total
        for cp in level1() + level2(True):
            cp.wait_send()

    return start, middle, finish


def _adamw_math(w, g, m, v):
    m = ADAM_B1 * m + (1.0 - ADAM_B1) * g
    v = ADAM_B2 * v + (1.0 - ADAM_B2) * (g * g)
    m_hat = m / (1.0 - ADAM_B1 ** ADAM_STEP)
    v_hat = v / (1.0 - ADAM_B2 ** ADAM_STEP)
    delta = -ADAM_LR * (m_hat / (jnp.sqrt(v_hat) + ADAM_EPS) + ADAM_WD * w)
    return delta, m, v


def _adamw_shards(parts, ws, ms, vs, steps, name):
    n = len(ws)

    def body(*refs):
        ins, outs = refs[:4 * n], refs[4 * n:]
        for k in range(n):
            p_ref, w_ref, m_ref, v_ref = ins[4 * k:4 * k + 4]
            g = p_ref[0].astype(F32)
            for j in range(1, N_CHIPS):
                g = g + p_ref[j].astype(F32)
            d, mn, vn = _adamw_math(w_ref[...], g, m_ref[...], v_ref[...])
            for o, val in zip(outs[4 * k:4 * k + 4], (g, d, mn, vn)):
                o[...] = val

    in_specs, out_specs, out_shape, operands = [], [], [], []
    for p, w, m, v in zip(parts, ws, ms, vs):
        R, Cc = w.shape
        blk = pl.BlockSpec((R // steps, Cc), lambda i: (i, 0))
        in_specs += [pl.BlockSpec((N_CHIPS, R // steps, Cc), lambda i: (0, i, 0)), blk, blk, blk]
        out_specs += [blk] * 4
        out_shape += [jax.ShapeDtypeStruct((R, Cc), F32)] * 4
        operands += [p, w, m, v]
    res = pl.pallas_call(
        body, name=name, grid=(steps,), in_specs=in_specs, out_specs=out_specs, out_shape=out_shape,
        compiler_params=pltpu.CompilerParams(dimension_semantics=("parallel",), vmem_limit_bytes=VMEM_LIMIT),
    )(*operands)
    return [res[4 * k:4 * k + 4] for k in range(n)]


def _adamw_small(grads, ws, ms, vs):
    n = len(grads)
    vm = pl.BlockSpec(memory_space=pltpu.VMEM)

    def body(*refs):
        g_in, w_in, m_in, v_in = (refs[k * n:(k + 1) * n] for k in range(4))
        g_out, d_out, m_out, v_out = (refs[(4 + k) * n:(5 + k) * n] for k in range(4))
        for k in range(n):
            if len(g_in[k].shape) > len(w_in[k].shape):
                g = g_in[k][0]
                for q in range(1, g_in[k].shape[0]):
                    g = g + g_in[k][q]
            else:
                g = g_in[k][...]
            d, mn, vn = _adamw_math(w_in[k][...], g, m_in[k][...], v_in[k][...])
            g_out[k][...] = g
            d_out[k][...] = d
            m_out[k][...] = mn
            v_out[k][...] = vn

    shapes = [jax.ShapeDtypeStruct(w.shape, F32) for w in ws]
    return pl.pallas_call(
        body, name="adamw_small", out_shape=shapes * 4, in_specs=[vm] * (4 * n), out_specs=[vm] * (4 * n),
    )(*grads, *ws, *ms, *vs)


def _shard_cols(full):
    R, Ct = full.shape
    return jnp.transpose(full.reshape(R, N_DEV, Ct // N_DEV), (1, 0, 2))


_COL_SHARDED = ("w_in", "w_gate", "w_up")
_BIG = ("w_in", "w_out", "w_gate", "w_up", "w_down")


def _rows(nm, p):
    return p[0].T if nm in _COL_SHARDED else p[0]


def _step(args, ts_mix_fwd, ts_ffn, ts_mix_bwd, tm_grad):
    (x, g_mix, w_in, b_in, w_dw, b_dw, ln_g, ln_b, w_pool, s_pool, w_out, g_ffn, w_gate, w_up, w_down, g_final,
     loss_target) = args[:17]
    names = ["g_mix", "w_in", "b_in", "w_dw", "b_dw", "ln_g", "ln_b", "w_pool", "s_pool", "w_out", "g_ffn",
             "w_gate", "w_up", "w_down", "g_final"]
    weights = dict(zip(names, args[1:16]))
    moms = dict(zip(names, args[17:32]))
    vars_ = dict(zip(names, args[32:47]))

    S, D = x.shape[1], x.shape[2]
    x2 = x.reshape(S, D)
    tgt2 = loss_target.reshape(S, D)

    shard = lambda nm: _rows(nm, weights[nm])
    w_pool_b = w_pool[0].astype(BF16)
    dw_rows = -(-CONV_WIDTH // SUBLANES) * SUBLANES
    dw_shard = jnp.pad(w_dw[0], ((0, dw_rows - CONV_WIDTH), (0, LANES - w_dw.shape[2])))

    a, gate, v, m, y, h1, xn, w_dw_f, g_in, g_out, _, g_gate, g_up, g_down = _mix_fwd(
        x2, g_mix, b_in, b_dw, ln_g, ln_b, w_pool_b, s_pool, [shard("w_in"), shard("w_out"), dw_shard],
        [shard("w_gate"), shard("w_up"), shard("w_down")], ts_mix_fwd)
    wt_in, w_out_f = g_in.reshape(-1, D), g_out.reshape(-1, D)
    wt_gate, wt_up, w_down_f = g_gate.reshape(-1, D), g_up.reshape(-1, D), g_down.reshape(-1, D)
    Fd = wt_gate.shape[0]
    f_chunks = [1024] * (Fd // 1024) + ([Fd % 1024] if Fd % 1024 else [])
    dh1, hn, act, dgt, dup, dh2, loss_p, dg_final, dg_ffn = _ffn(
        h1, tgt2, g_ffn, g_final.reshape(1, D), wt_gate, wt_up, w_down_f, ts_ffn, f_chunks)

    by_shard = lambda g: g.reshape(N_DEV, -1, D)
    ffn_grads, ffn_sib = _grad_ffn([dgt, dup, act], [hn, hn, dh2], tm_grad)
    pair_sums = _pair_sum([by_shard(g) for g in ffn_grads], list(ffn_sib))
    (dx, dz, dh1b, dg_mix, db_in, dw_dw, db_dw, dln_g, dln_b, dw_pool, ds_pool, r_gate, r_up, r_down) = _mix_bwd(
        dh1, x2, a, gate, v, m, g_mix, wt_in, w_dw_f, ln_g, ln_b, w_pool_b, s_pool, w_out_f, pair_sums, ts_mix_bwd)

    small_names = ["g_mix", "b_in", "b_dw", "ln_g", "ln_b", "w_pool", "s_pool", "g_ffn", "g_final"]
    small_shape = lambda p: p.reshape(-1, p.shape[-1])
    partial = [dg_mix, db_in, db_dw, dln_g, dln_b, dw_pool.reshape(-1, POOL_GROUP), ds_pool, dg_ffn, dg_final, loss_p]
    r_out, r_dw, r_in, summed = _grad_mix(y, dh1b, dz, xn, _shard_cols(dw_dw[0:CONV_WIDTH]), partial)

    big = {}
    recv = dict(zip(_BIG, [r_in, r_out, r_gate, r_up, r_down]))
    for call, group in (("adamw_ffn", ("w_gate", "w_up", "w_down")), ("adamw_mix", ("w_in", "w_out"))):
        results = _adamw_shards([recv[nm] for nm in group], [_rows(nm, weights[nm]) for nm in group],
                                [_rows(nm, moms[nm]) for nm in group], [_rows(nm, vars_[nm]) for nm in group], 2, call)
        for nm, res in zip(group, results):
            big[nm] = [o.T if nm in _COL_SHARDED else o for o in res]

    small_names = small_names + ["w_dw"]
    sm = _adamw_small(list(summed[:-1]) + [r_dw], [small_shape(weights[nm]) for nm in small_names],
                      [small_shape(moms[nm]) for nm in small_names], [small_shape(vars_[nm]) for nm in small_names])
    n_small = len(small_names)

    def result(kind, nm):
        if nm in big:
            return big[nm][kind].reshape(weights[nm].shape)
        return sm[kind * n_small + small_names.index(nm)].reshape(weights[nm].shape)

    loss = summed[-1][0, 0]
    out = [loss, dx.reshape(x.shape)]
    for kind in range(4):
        out += [result(kind, nm) for nm in names]
    return tuple(out)


def kernel(x, g_mix, w_in, b_in, w_dw, b_dw, ln_g, ln_b, w_pool, s_pool, w_out, g_ffn, w_gate, w_up, w_down, g_final, loss_target, m_g_mix, m_w_in, m_b_in, m_w_dw, m_b_dw, m_ln_g, m_ln_b, m_w_pool, m_s_pool, m_w_out, m_g_ffn, m_w_gate, m_w_up, m_w_down, m_g_final, v_g_mix, v_w_in, v_b_in, v_w_dw, v_b_dw, v_ln_g, v_ln_b, v_w_pool, v_s_pool, v_w_out, v_g_ffn, v_w_gate, v_w_up, v_w_down, v_g_final):
    args = (x, g_mix, w_in, b_in, w_dw, b_dw, ln_g, ln_b, w_pool, s_pool, w_out, g_ffn, w_gate, w_up, w_down, g_final, loss_target, m_g_mix, m_w_in, m_b_in, m_w_dw, m_b_dw, m_ln_g, m_ln_b, m_w_pool, m_s_pool, m_w_out, m_g_ffn, m_w_gate, m_w_up, m_w_down, m_g_final, v_g_mix, v_w_in, v_b_in, v_w_dw, v_b_dw, v_ln_g, v_ln_b, v_w_pool, v_s_pool, v_w_out, v_g_ffn, v_w_gate, v_w_up, v_w_down, v_g_final)
    return _step(args, ts_mix_fwd=512, ts_ffn=256, ts_mix_bwd=512, tm_grad=256)
```

```python
import jax
import jax.numpy as jnp
from jax import lax
from jax.experimental import pallas as pl
from jax.experimental.pallas import tpu as pltpu

F32 = jnp.float32
BF16 = jnp.bfloat16
MESH = pl.DeviceIdType.MESH
N_DEV = 8

C_CONV = 512
CONV_WIDTH = 31
POOL_WINDOWS = (2, 4, 8, 16)
POOL_GROUP = 128
RMS_EPS = 1e-6
LN_EPS = 1e-5

ADAM_LR = 0.001
ADAM_B1 = 0.9
ADAM_B2 = 0.999
ADAM_EPS = 1e-08
ADAM_WD = 0.01
ADAM_STEP = 10

HALO = 32
SUBLANES = 8
LANES = 128
CONV_ROWS = 64
VMEM_LIMIT = 56 * 1024 * 1024


def _dot(a, b):
    return jnp.dot(a, b, preferred_element_type=F32)


def _dot_nt(a, b):
    return lax.dot_general(a, b, (((1,), (1,)), ((), ())), preferred_element_type=F32)


def _dot_tn(a, b):
    return lax.dot_general(a, b, (((0,), (0,)), ((), ())), preferred_element_type=F32)


def _mean_last(v):
    return jnp.mean(v, axis=-1, keepdims=True)


def _full(shape):
    nd = len(shape)
    return pl.BlockSpec(shape, lambda *_: (0,) * nd)


def _full1(shape):
    nd = len(shape)
    return pl.BlockSpec(shape, lambda *_: (0,) * nd, pipeline_mode=pl.Buffered(1))


def _shifted_copies(sh_ref, rows):
    for s in range(1, SUBLANES):
        sh_ref[s, 0:rows, :] = sh_ref[0, s:s + rows, :]


def _tap(sh_ref, off, r0, rows):
    q, s = divmod(off, SUBLANES)
    return sh_ref[s, pl.ds(r0 + q * SUBLANES, rows), :]


def _window_sums(src, bufs, rows, backward):
    assert all(w == 2 << g for g, w in enumerate(POOL_WINDOWS))
    n = len(POOL_WINDOWS)
    out = []
    for level in range(n):
        dst, shift = bufs[level % 2], 1 << level
        lanes = slice(level * POOL_GROUP, n * POOL_GROUP)
        lo, hi = (SUBLANES * (level + 1), rows) if backward else (0, rows - SUBLANES * (level + 1))
        other = slice(lo - shift, hi - shift) if backward else slice(lo + shift, hi + shift)
        dst[lo:hi, lanes] = src[lo:hi, lanes] + src[other, lanes]
        src = dst
        out.append(dst)
    return out


def _mix_fwd(x, g_mix, b_in, b_dw, ln_g, ln_b, w_pool, s_pool, mix_shards, ag_shards, ts):
    S, D = x.shape
    d_in = mix_shards[0].shape[0] * N_DEV
    C = C_CONV
    nt = S // ts
    nrb = ts // CONV_ROWS
    n_ag = len(ag_shards)
    relay_step = nt // 2
    mix_dtypes = [BF16, BF16, F32]

    def body(x_ref, g_ref, bin_ref, bdw_ref, lng_ref, lnb_ref, wp_ref, sp_ref, *rest):
        mx_in, ag_in, rest = rest[:3], rest[3:3 + n_ag], rest[3 + n_ag:]
        a_ref, gate_ref, v_ref, m_ref, y_ref, h1_ref, xn_ref, wdw_ref = rest[:8]
        mx_out, ag_out, rest = rest[8:11], rest[11:11 + n_ag], rest[11 + n_ag:]
        ush, pbuf, pa, pb, win_ref, wout_ref, gdw, load_sems = rest[:8]
        rest = rest[8:]
        (ma_stage, ma_bufs, ma_sems), rest = (rest[:2], rest[2:4], rest[4:7]), rest[7:]
        (mb_stage, mb_bufs, mb_sems), rest = (rest[:1], rest[1:2], rest[2:5]), rest[5:]
        ag_stage, ag_bufs, ag_sems = rest[:n_ag], rest[n_ag:2 * n_ag], rest[2 * n_ag:]
        i = pl.program_id(0)
        ma_start, ma_relay, ma_pass_on, ma_finish = _gather_plan(
            mx_in[0::2], mx_out[0::2], ma_stage, ma_bufs, *ma_sems)
        mb_start, mb_relay, mb_pass_on, mb_finish = _gather_plan(
            mx_in[1:2], mx_out[1:2], mb_stage, mb_bufs, *mb_sems)
        ag_start, ag_relay, ag_pass_on, ag_finish = _gather_plan(ag_in, ag_out, ag_stage, ag_bufs, *ag_sems)

        def load_weight(k, dst_ref):
            rows = mx_out[k].shape[1]
            loads = [pltpu.make_async_copy(mx_out[k].at[j], dst_ref.at[pl.ds(j * rows, rows), :],
                                           load_sems.at[N_DEV * k + j]) for j in range(N_DEV)]
            for cp in loads:
                cp.start()
            for cp in loads:
                cp.wait()

        @pl.when(i == 0)
        def _():
            ma_start()
            mb_start()
            ma_relay()
            mb_relay()
            ag_start()
            ma_pass_on()
            ma_finish()
            load_weight(0, win_ref)
            dw_load = pltpu.make_async_copy(mx_out[2], gdw, load_sems.at[2 * N_DEV])
            dw_load.start()
            dw_load.wait()
            first_half = lax.broadcasted_iota(jnp.int32, gdw.shape[1:], 1) < C // N_DEV
            for p in range(N_DEV // 2):
                wdw_ref[:, LANES * p:LANES * (p + 1)] = jnp.where(
                    first_half, gdw[2 * p], pltpu.roll(gdw[2 * p + 1], C // N_DEV, axis=1))
            ush[0, 0:HALO, :] = jnp.zeros((HALO, C), F32)
            pbuf[0:HALO, :] = jnp.zeros((HALO, C), F32)

        @pl.when(i == relay_step)
        def _():
            ag_relay()

        @pl.when(i == nt - 1)
        def _():
            ag_pass_on()

        x = x_ref[...]
        r1 = lax.rsqrt(_mean_last(x * x) + RMS_EPS)
        xn = (x * r1 * g_ref[...]).astype(BF16)
        xn_ref[...] = xn
        z = _dot_nt(xn, win_ref[...]) + bin_ref[...]
        a = z[:, 0:C]
        gate = z[:, C:2 * C]
        a_ref[...] = a
        gate_ref[...] = gate
        ush[0, HALO:HALO + ts, :] = a * jax.nn.sigmoid(gate)
        pbuf[HALO:HALO + ts, :] = z[:, 2 * C:]

        _shifted_copies(ush, ts + HALO - SUBLANES)

        def conv_block(rb, carry):
            r0 = pl.multiple_of(rb * CONV_ROWS, CONV_ROWS)
            acc = jnp.zeros((CONV_ROWS, C), F32)
            for k in range(CONV_WIDTH):
                acc = acc + wdw_ref[k:k + 1, :] * _tap(ush, HALO - (CONV_WIDTH - 1) + k, r0, CONV_ROWS)
            v_ref[pl.ds(r0, CONV_ROWS), :] = acc + bdw_ref[...]
            return carry

        lax.fori_loop(0, nrb, conv_block, 0)

        v = v_ref[...]
        mu = _mean_last(v)
        xc = v - mu
        rstd = lax.rsqrt(_mean_last(xc * xc) + LN_EPS)
        ln = xc * rstd * lng_ref[...] + lnb_ref[...]
        y_ref[:, 0:C] = (ln * jax.nn.sigmoid(ln)).astype(BF16)

        sums = _window_sums(pbuf, (pa, pb), ts + HALO, backward=True)
        row = lax.broadcasted_iota(jnp.int32, (ts, 1), 0) + i * ts
        for gi, w in enumerate(POOL_WINDOWS):
            lanes = slice(gi * POOL_GROUP, (gi + 1) * POOL_GROUP)
            seg = pbuf[HALO:HALO + ts, lanes]
            ws = sums[gi][HALO:HALO + ts, lanes]
            cnt = jnp.minimum(row + 1, w).astype(F32)
            m = (ws / cnt - seg).astype(BF16)
            m_ref[:, lanes] = m
            ypre = _dot(m, wp_ref[gi])
            y_ref[:, C + gi * POOL_GROUP:C + (gi + 1) * POOL_GROUP] = (ypre * sp_ref[:, lanes]).astype(BF16)

        @pl.when(i == 0)
        def _():
            mb_pass_on()
            mb_finish()
            load_weight(1, wout_ref)

        h1_ref[...] = x + _dot(y_ref[...], wout_ref[...])

        ush[0, 0:HALO, :] = ush[0, ts:ts + HALO, :]
        pbuf[0:HALO, :] = pbuf[ts:ts + HALO, :]

        @pl.when(i == nt - 1)
        def _():
            ag_finish()

    tile = lambda w, dt: (pl.BlockSpec((ts, w), lambda i: (i, 0)), jax.ShapeDtypeStruct((S, w), dt))
    wdw_rows = mix_shards[2].shape[0]
    outs = [tile(C, F32), tile(C, F32), tile(C, F32), tile(C, BF16), tile(D, BF16), tile(D, F32), tile(D, BF16),
            (_full((wdw_rows, C)), jax.ShapeDtypeStruct((wdw_rows, C), F32))]
    return pl.pallas_call(
        body, name="mix_fwd", grid=(nt,),
        in_specs=[pl.BlockSpec((ts, D), lambda i: (i, 0)), _full((1, D)), _full((1, d_in)), _full((1, C)),
                  _full((1, C)), _full((1, C)), _full(w_pool.shape), _full((1, C))] + [_HBM] * (3 + n_ag),
        out_specs=[o[0] for o in outs] + [_HBM] * (3 + n_ag),
        out_shape=[o[1] for o in outs] + _gather_out_shapes(mix_shards, mix_dtypes)
        + _gather_out_shapes(ag_shards, [BF16] * n_ag),
        scratch_shapes=[pltpu.VMEM((SUBLANES, ts + HALO, C), F32)] + [pltpu.VMEM((ts + HALO, C), F32)] * 3
        + [pltpu.VMEM((d_in, D), BF16), pltpu.VMEM((D, D), BF16),
                        pltpu.VMEM((N_DEV,) + mix_shards[2].shape, F32), pltpu.SemaphoreType.DMA((2 * N_DEV + 1,))]
        + _gather_scratch(mix_shards[0::2], mix_dtypes[0::2]) + _gather_scratch(mix_shards[1:2], mix_dtypes[1:2])
        + _gather_scratch(ag_shards, [BF16] * n_ag),
        compiler_params=pltpu.CompilerParams(dimension_semantics=("arbitrary",), vmem_limit_bytes=VMEM_LIMIT),
    )(x, g_mix, b_in, b_dw, ln_g, ln_b, w_pool, s_pool, *mix_shards, *ag_shards)


def _ffn(h1, target, g_ffn, g_final, w_gate, w_up, w_down, ts, f_chunks, tm):
    S, D = h1.shape
    Fd = w_gate.shape[0]
    nt = S // ts
    bounds = []
    lo = 0
    for n in f_chunks:
        assert n % tm == 0
        bounds.append((lo, lo + n))
        lo += n
    assert lo == Fd

    def put(ref, val, lo):
        for j in range(val.shape[1] // tm):
            ref[lo // tm + j] = val[:, j * tm:(j + 1) * tm]

    def body(h1_ref, tgt_ref, gf_ref, gl_ref, wg_ref, wu_ref, wd_ref,
             dh1_ref, hn_ref, act_ref, dgt_ref, dup_ref, dh2_ref, loss_ref, dgl_ref, dgf_ref, gt_s, up_s):
        i = pl.program_id(0)

        @pl.when(i == 0)
        def _():
            loss_ref[...] = jnp.zeros_like(loss_ref)
            dgl_ref[...] = jnp.zeros_like(dgl_ref)
            dgf_ref[...] = jnp.zeros_like(dgf_ref)

        h1 = h1_ref[...]
        r2 = lax.rsqrt(_mean_last(h1 * h1) + RMS_EPS)
        hhat = h1 * r2
        hn = (hhat * gf_ref[...]).astype(BF16)
        hn_ref[...] = hn
        h2 = h1
        for lo, hi in bounds:
            gt = _dot_nt(hn, wg_ref[lo:hi, :])
            up = _dot_nt(hn, wu_ref[lo:hi, :])
            gt_s[:, lo:hi] = gt
            up_s[:, lo:hi] = up
            act = (gt * jax.nn.sigmoid(gt) * up).astype(BF16)
            put(act_ref, act, lo)
            h2 = h2 + _dot(act, wd_ref[lo:hi, :])

        r3 = lax.rsqrt(_mean_last(h2 * h2) + RMS_EPS)
        n3 = h2 * r3
        gl = gl_ref[...]
        diff = n3 * gl - tgt_ref[...]
        loss_ref[...] += jnp.sum(0.5 * jnp.sum(diff * diff, axis=-1, keepdims=True) / D, axis=0, keepdims=True)
        dout = diff / D
        dgl_ref[...] += jnp.sum(dout * n3, axis=0, keepdims=True)
        dn = dout * gl
        dh2 = r3 * (dn - n3 * _mean_last(dn * n3))
        dh2b = dh2.astype(BF16)
        dh2_ref[...] = dh2b

        dhn = jnp.zeros((ts, D), F32)
        for lo, hi in bounds:
            gt = gt_s[:, lo:hi]
            up = up_s[:, lo:hi]
            sg = jax.nn.sigmoid(gt)
            dact = _dot_nt(dh2b, wd_ref[lo:hi, :])
            dgt = (dact * up * (sg * (1.0 + gt * (1.0 - sg)))).astype(BF16)
            dup = (dact * (gt * sg)).astype(BF16)
            put(dgt_ref, dgt, lo)
            put(dup_ref, dup, lo)
            dhn = dhn + _dot(dgt, wg_ref[lo:hi, :]) + _dot(dup, wu_ref[lo:hi, :])

        dgf_ref[...] += jnp.sum(dhn * hhat, axis=0, keepdims=True)
        dnn = dhn * gf_ref[...]
        dh1_ref[...] = dh2 + r2 * (dnn - hhat * _mean_last(dnn * hhat))

    tile = lambda w, dt: (pl.BlockSpec((ts, w), lambda i: (i, 0)), jax.ShapeDtypeStruct((S, w), dt))
    acc = lambda w: (_full((1, w)), jax.ShapeDtypeStruct((1, w), F32))
    cols = (pl.BlockSpec((Fd // tm, ts, tm), lambda i: (0, i, 0)), jax.ShapeDtypeStruct((Fd // tm, S, tm), BF16))
    outs = [tile(D, F32), tile(D, BF16), cols, cols, cols, tile(D, BF16),
            acc(LANES), acc(D), acc(D)]
    return pl.pallas_call(
        body, name="ffn_fwd_bwd", grid=(nt,),
        in_specs=[pl.BlockSpec((ts, D), lambda i: (i, 0)), pl.BlockSpec((ts, D), lambda i: (i, 0)),
                  _full((1, D)), _full((1, D)), _full1((Fd, D)), _full1((Fd, D)), _full1((Fd, D))],
        out_specs=[o[0] for o in outs], out_shape=[o[1] for o in outs],
        scratch_shapes=[pltpu.VMEM((ts, Fd), F32), pltpu.VMEM((ts, Fd), F32)],
        compiler_params=pltpu.CompilerParams(dimension_semantics=("arbitrary",), vmem_limit_bytes=VMEM_LIMIT),
    )(h1, target, g_ffn, g_final, w_gate, w_up, w_down)


def _mix_bwd(dh1, x, a, gate, v, m, g_mix, w_in, w_dw, ln_g, ln_b, w_pool, s_pool, w_out, rs_parts, ts):
    S, D = x.shape
    n_rs = len(rs_parts)
    d_in = w_in.shape[0]
    C = C_CONV
    nt = S // ts
    nrb = ts // CONV_ROWS
    wrows =((CONV_WIDTH + SUBLANES - 1) // SUBLANES) * SUBLANES

    def body(dh1_ref, x_ref, a_ref, gate_ref, v_ref, m_ref, g_ref, win_ref, wdw_ref, lng_ref,
             lnb_ref, wp_ref, sp_ref, wout_ref, *rest):
        rs_in, rest = rest[:n_rs], rest[n_rs:]
        (dx_ref, dz_ref, dh1b_ref, dgm_ref, dbin_ref, dwdw_ref, dbdw_ref, dlng_ref, dlnb_ref, dwp_ref,
         dsp_ref) = rest[:11]
        rs_out, rest = rest[11:11 + n_rs], rest[11 + n_rs:]
        dvsh, dqbuf, qa, qb, du_s, dm_s = rest[:6]
        rs_bufs, (send_sems, recv_sems, local_sems) = rest[6:6 + n_rs], rest[6 + n_rs:]
        i = pl.program_id(0)
        t = nt - 1 - i
        rs_start, rs_finish = _chip_exchange_plan(rs_in, rs_out, rs_bufs, send_sems, recv_sems, local_sems)

        @pl.when(i == 0)
        def _():
            rs_start()
            dvsh[0, ts:ts + HALO, :] = jnp.zeros((HALO, C), F32)
            dqbuf[ts:ts + HALO, :] = jnp.zeros((HALO, C), F32)
            for r in (dgm_ref, dbin_ref, dwdw_ref, dbdw_ref, dlng_ref, dlnb_ref, dwp_ref, dsp_ref):
                r[...] = jnp.zeros_like(r)

        dh1 = dh1_ref[...]
        dh1b = dh1.astype(BF16)
        dh1b_ref[...] = dh1b
        dy = _dot_nt(dh1b, wout_ref[...])

        v = v_ref[...]
        mu = _mean_last(v)
        xc = v - mu
        rstd = lax.rsqrt(_mean_last(xc * xc) + LN_EPS)
        vhat = xc * rstd
        lng = lng_ref[...]
        ln = vhat * lng + lnb_ref[...]
        sg = jax.nn.sigmoid(ln)
        dln = dy[:, 0:C] * (sg * (1.0 + ln * (1.0 - sg)))
        dlng_ref[...] += jnp.sum(dln * vhat, axis=0, keepdims=True)
        dlnb_ref[...] += jnp.sum(dln, axis=0, keepdims=True)
        dvh = dln * lng
        dv = rstd * (dvh - _mean_last(dvh) - vhat * _mean_last(dvh * vhat))
        dbdw_ref[...] += jnp.sum(dv, axis=0, keepdims=True)
        dvsh[0, 0:ts, :] = dv
        _shifted_copies(dvsh, ts + HALO - SUBLANES)

        def conv_block(rb, carry):
            r0 = pl.multiple_of(rb * CONV_ROWS, CONV_ROWS)
            acc = jnp.zeros((CONV_ROWS, C), F32)
            for k in range(CONV_WIDTH):
                acc = acc + wdw_ref[k:k + 1, :] * _tap(dvsh, CONV_WIDTH - 1 - k, r0, CONV_ROWS)
            du_s[pl.ds(r0, CONV_ROWS), :] = acc
            return carry

        lax.fori_loop(0, nrb, conv_block, 0)

        a = a_ref[...]
        sgate = jax.nn.sigmoid(gate_ref[...])
        u = a * sgate
        for k in range(CONV_WIDTH):
            q, s = divmod(CONV_WIDTH - 1 - k, SUBLANES)
            prod = u * dvsh[s, q * SUBLANES:q * SUBLANES + ts, :]
            dwdw_ref[k:k + 1, :] += jnp.sum(prod, axis=0, keepdims=True)

        du = du_s[...]
        da = du * sgate
        dgate = du * a * sgate * (1.0 - sgate)
        dz_ref[:, 0:C] = da.astype(BF16)
        dz_ref[:, C:2 * C] = dgate.astype(BF16)
        dbin_ref[:, 0:C] += jnp.sum(da, axis=0, keepdims=True)
        dbin_ref[:, C:2 * C] += jnp.sum(dgate, axis=0, keepdims=True)

        row = lax.broadcasted_iota(jnp.int32, (ts, 1), 0) + t * ts
        for gi, w in enumerate(POOL_WINDOWS):
            lanes = slice(gi * POOL_GROUP, (gi + 1) * POOL_GROUP)
            dyp = dy[:, C + gi * POOL_GROUP:C + (gi + 1) * POOL_GROUP]
            mg = m_ref[:, lanes]
            ypre = _dot(mg, wp_ref[gi])
            dsp_ref[:, lanes] += jnp.sum(dyp * ypre, axis=0, keepdims=True)
            dyi = (dyp * sp_ref[:, lanes]).astype(BF16)
            dwp_ref[gi] += _dot_tn(mg, dyi)
            dm = _dot_nt(dyi, wp_ref[gi])
            cnt = jnp.minimum(row + 1, w).astype(F32)
            dqbuf[0:ts, lanes] = dm / cnt
            dm_s[:, lanes] = dm
        sums = _window_sums(dqbuf, (qa, qb), ts + HALO, backward=False)
        for gi in range(len(POOL_WINDOWS)):
            lanes = slice(gi * POOL_GROUP, (gi + 1) * POOL_GROUP)
            dp = sums[gi][0:ts, lanes] - dm_s[:, lanes]
            dz_ref[:, 2 * C + gi * POOL_GROUP:2 * C + (gi + 1) * POOL_GROUP] = dp.astype(BF16)
            dbin_ref[:, 2 * C + gi * POOL_GROUP:2 * C + (gi + 1) * POOL_GROUP] += jnp.sum(dp, axis=0, keepdims=True)

        dxn = _dot(dz_ref[...], win_ref[...])
        x = x_ref[...]
        r1 = lax.rsqrt(_mean_last(x * x) + RMS_EPS)
        xhat = x * r1
        dgm_ref[...] += jnp.sum(dxn * xhat, axis=0, keepdims=True)
        dnn = dxn * g_ref[...]
        dx_ref[...] = dh1 + r1 * (dnn - xhat * _mean_last(dnn * xhat))

        dvsh[0, ts:ts + HALO, :] = dvsh[0, 0:HALO, :]
        dqbuf[ts:ts + HALO, :] = dqbuf[0:HALO, :]

        @pl.when(i == nt - 1)
        def _():
            rs_finish()

    rev = lambda w: pl.BlockSpec((ts, w), lambda i: (nt - 1 - i, 0))
    acc = lambda shape: (_full(shape), jax.ShapeDtypeStruct(shape, F32))
    outs = [(rev(D), jax.ShapeDtypeStruct((S, D), F32)), (rev(d_in), jax.ShapeDtypeStruct((S, d_in), BF16)),
            (rev(D), jax.ShapeDtypeStruct((S, D), BF16)), acc((1, D)), acc((1, d_in)), acc((wrows, C)),
            acc((1, C)), acc((1, C)), acc((1, C)), acc(w_pool.shape), acc((1, C))]
    return pl.pallas_call(
        body, name="mix_bwd", grid=(nt,),
        in_specs=[rev(D), rev(D), rev(C), rev(C), rev(C), rev(C), _full((1, D)), _full((d_in, D)),
                  _full(w_dw.shape), _full((1, C)), _full((1, C)), _full(w_pool.shape), _full((1, C)),
                  _full((D, D))] + [_HBM] * n_rs,
        out_specs=[o[0] for o in outs] + [_HBM] * n_rs,
        out_shape=[o[1] for o in outs] + [jax.ShapeDtypeStruct(p.shape, p.dtype) for p in rs_parts],
        scratch_shapes=[pltpu.VMEM((SUBLANES, ts + HALO, C), F32)] + [pltpu.VMEM((ts + HALO, C), F32)] * 3
        + [pltpu.VMEM((ts, C), F32)] * 2
        + [pltpu.VMEM(p.shape[1:], p.dtype) for p in rs_parts] + _comm_sems(n_rs, 3),
        compiler_params=pltpu.CompilerParams(dimension_semantics=("arbitrary",), vmem_limit_bytes=VMEM_LIMIT),
    )(dh1, x, a, gate, v, m, g_mix, w_in, w_dw, ln_g, ln_b, w_pool, s_pool, w_out, *rs_parts)


def _grad_mix(y, dh1b, dz, xn, dw_dw, vals):
    S, D = y.shape
    d_in = dz.shape[1]
    r_out, r_in = D // N_DEV, d_in // N_DEV
    n_out, n_in = N_DEV // 2, N_DEV // 2
    steps = n_out + n_in
    ns = len(vals)
    dw_out = jax.ShapeDtypeStruct((N_DEV, r_out, D), BF16)
    dw_in = jax.ShapeDtypeStruct((N_DEV, r_in, D), BF16)
    s_scratch, o_scratch = _reduce_scratch([], vals), _reduce_scratch([dw_out, dw_dw], [])
    i_scratch = _reduce_scratch([dw_in], [])

    def body(y_ref, dh_ref, dz_ref, xn_ref, dwdw_ref, *rest):
        v_in, p_out, v_out = rest[:ns], rest[ns:ns + 3], rest[ns + 3:2 * ns + 3]
        rest = rest[2 * ns + 3:]
        dwo_v, dwi_v, rest = rest[0], rest[1], rest[2:]
        s_refs, rest = rest[:len(s_scratch)], rest[len(s_scratch):]
        o_refs, i_refs = rest[:len(o_scratch)], rest[len(o_scratch):]
        s_start, s_middle, s_finish = _reduce_plan((), v_in, (), v_out, s_refs)
        o_start, o_middle, o_finish = _reduce_plan((dwo_v, dwdw_ref), (), p_out[:2], (), o_refs)
        i_start, i_middle, i_finish = _reduce_plan((dwi_v,), (), p_out[2:], (), i_refs)
        s = pl.program_id(0)
        pl.when(s == 0)(s_start)
        pl.when(s == 1)(s_middle)

        @pl.when(s < n_out)
        def _():
            res = _dot_tn(y_ref[...], dh_ref[...]).astype(BF16)
            dwo_v[2 * s] = res[0:r_out]
            dwo_v[2 * s + 1] = res[r_out:]

        pl.when(s == n_out - 1)(s_finish)
        pl.when(s == n_out)(o_start)
        pl.when(s == n_out + 1)(o_middle)

        @pl.when(s >= n_out)
        def _():
            res = _dot_tn(dz_ref[...], xn_ref[...]).astype(BF16)
            dwi_v[2 * (s - n_out)] = res[0:r_in]
            dwi_v[2 * (s - n_out) + 1] = res[r_in:]

        @pl.when(s == steps - 1)
        def _():
            o_finish()
            i_start()
            i_middle()
            i_finish()

    late = lambda s: jnp.clip(s - n_out, 0, n_in - 1)
    res = pl.pallas_call(
        body, name="grad_w_mix", grid=(steps,),
        in_specs=[pl.BlockSpec((S, 2 * r_out), lambda s: (0, jnp.minimum(s, n_out - 1))), _full1((S, D)),
                  pl.BlockSpec((S, 2 * r_in), lambda s: (0, late(s))), _full1((S, D))] + [_VMEM] * (1 + ns),
        out_specs=[_VMEM] * (3 + ns),
        out_shape=_reduce_out_shapes([dw_out, dw_dw, dw_in], []) + _reduce_out_shapes([], vals),
        scratch_shapes=[pltpu.VMEM(dw_out.shape, BF16), pltpu.VMEM(dw_in.shape, BF16)] + s_scratch + o_scratch
        + i_scratch,
        compiler_params=pltpu.CompilerParams(dimension_semantics=("arbitrary",), vmem_limit_bytes=VMEM_LIMIT),
    )(y, dh1b, dz, xn, dw_dw, *vals)
    return res[0], res[1], res[2], res[3:]


def _grad_ffn(a_list, b_list, tm):
    n = len(a_list)
    nb, S, tm_a = a_list[0].shape
    assert tm_a == tm
    M = nb * tm
    N = b_list[0].shape[1]
    steps = n * nb
    R = M // N_DEV
    b_unique = [b for k, b in enumerate(b_list) if all(b is not o for o in b_list[:k])]
    b_index = [[b is o for o in b_unique].index(True) for b in b_list]

    def body(*refs):
        a_refs, b_refs = refs[:n], refs[n:n + len(b_unique)]
        rest = refs[n + len(b_unique):]
        g_refs, sib_refs, (obuf, w_sems, send_sems, recv_sems) = rest[:n], rest[n:2 * n], rest[2 * n:]
        s = pl.program_id(0)
        slot = s % 2
        x, y, c = _position()

        def write_back(dst):
            return pltpu.make_async_copy(obuf.at[slot], dst, w_sems.at[slot])

        def exchange(k):
            return [pltpu.make_async_remote_copy(
                src_ref=g_refs[k].at[pl.ds(pl.multiple_of((2 * q + 1 - c) * R, SUBLANES * 2), R), :],
                dst_ref=sib_refs[k].at[q], send_sem=send_sems.at[k, q], recv_sem=recv_sems.at[k, q],
                device_id=(x, y, 1 - c), device_id_type=MESH) for q in range(N_CHIPS)]

        pl.when(s >= 2)(write_back(g_refs[0].at[pl.ds(0, tm), :]).wait)
        for k in range(n):
            @pl.when(jnp.logical_and(s >= k * nb, s < (k + 1) * nb))
            def _(k=k):
                obuf[slot] = _dot_tn(a_refs[k][...], b_refs[b_index[k]][...]).astype(BF16)
                row = pl.multiple_of((s - k * nb) * tm, tm)
                write_back(g_refs[k].at[pl.ds(row, tm), :]).start()
            if k < n - 1:
                @pl.when(s == (k + 1) * nb + 2)
                def _(k=k):
                    for cp in exchange(k):
                        cp.start()

        @pl.when(s == steps - 1)
        def _():
            write_back(g_refs[0].at[pl.ds(0, tm), :]).wait()
            pltpu.make_async_copy(obuf.at[1 - slot], g_refs[0].at[pl.ds(0, tm), :], w_sems.at[1 - slot]).wait()
            for cp in exchange(n - 1):
                cp.start()
            for k in range(n):
                for cp in exchange(k):
                    cp.wait()

    clamp = lambda k: (lambda s: (jnp.clip(s - k * nb, 0, nb - 1), 0, 0))
    sems = pltpu.SemaphoreType.DMA((n, N_CHIPS))
    res = pl.pallas_call(
        body, name="grad_w_ffn", grid=(steps,),
        in_specs=[pl.BlockSpec((None, S, tm), clamp(k)) for k in range(n)] + [_full1((S, N))] * len(b_unique),
        out_specs=[_HBM] * (2 * n),
        out_shape=[jax.ShapeDtypeStruct((M, N), BF16)] * n + [jax.ShapeDtypeStruct((N_CHIPS, R, N), BF16)] * n,
        scratch_shapes=[pltpu.VMEM((2, tm, N), BF16), pltpu.SemaphoreType.DMA((2,)), sems, sems],
        compiler_params=pltpu.CompilerParams(dimension_semantics=("arbitrary",), vmem_limit_bytes=VMEM_LIMIT),
    )(*a_list, *b_unique)
    return res[:n], res[n:]


def _position():
    return lax.axis_index("x"), lax.axis_index("y"), lax.axis_index("c")


def _slot(px, py, pc):
    return 4 * px + 2 * py + pc


_HBM = pl.BlockSpec(memory_space=pl.ANY)


def _comm_sems(n, copies):
    return [pltpu.SemaphoreType.DMA((n, copies)), pltpu.SemaphoreType.DMA((n, copies)),
            pltpu.SemaphoreType.DMA((n, 2))]


def _local_copy(srcs, dsts, bufs, local_sems):
    n = len(srcs)
    loads = [pltpu.make_async_copy(srcs[k], bufs[k], local_sems.at[k, 0]) for k in range(n)]
    for cp in loads:
        cp.start()
    for cp in loads:
        cp.wait()
    stores = _local_stores(dsts, bufs, local_sems)
    for cp in stores:
        cp.start()
    return stores


def _local_stores(dsts, bufs, local_sems):
    return [pltpu.make_async_copy(bufs[k], dsts[k], local_sems.at[k, 1]) for k in range(len(dsts))]


def _gather_out_shapes(shards, dtypes):
    return [jax.ShapeDtypeStruct((N_DEV,) + s.shape, dt) for s, dt in zip(shards, dtypes)]


def _gather_scratch(shards, dtypes):
    return ([pltpu.VMEM(s.shape, s.dtype) for s in shards] + [pltpu.VMEM(s.shape, dt) for s, dt in zip(shards, dtypes)]
            + _comm_sems(len(shards), 7))


def _gather_plan(ins, outs, stage, bufs, send_sems, recv_sems, local_sems):
    n = len(ins)
    x, y, c = _position()
    me, sibling = (x, y, c), (x, y, 1 - c)
    na, nb, dg = (x ^ (1 - c), y ^ c), (x ^ c, y ^ (1 - c)), (1 - x, 1 - y)
    own = [outs[k].at[_slot(*me)] for k in range(n)]

    def copy(k, sem, block, to, src=None):
        dst = outs[k].at[_slot(*block)]
        return pltpu.make_async_remote_copy(
            src_ref=dst if src is None else src, dst_ref=dst, send_sem=send_sems.at[k, sem],
            recv_sem=recv_sems.at[k, sem], device_id=to, device_id_type=MESH)

    def first():
        cps = []
        for k in range(n):
            cps += [copy(k, 0, me, sibling, src=bufs[k]), copy(k, 1, me, (*na, c), src=bufs[k]),
                    copy(k, 2, me, (*nb, c), src=bufs[k])]
        return cps

    def onward():
        return [copy(k, 3, (*na, c), (*nb, c)) for k in range(n)]

    def to_sibling(j, chip):
        return [copy(k, 4 + j, (*chip, c), sibling) for k in range(n)]

    def start():
        loads = [pltpu.make_async_copy(ins[k], stage[k], local_sems.at[k, 0]) for k in range(n)]
        for cp in loads:
            cp.start()
        for cp in loads:
            cp.wait()
        for k in range(n):
            bufs[k][...] = stage[k][...].astype(bufs[k].dtype)
        for cp in first() + _local_stores(own, bufs, local_sems):
            cp.start()

    def relay():
        for k in range(n):
            copy(k, 1, (*na, c), me).wait_recv()
        for cp in onward() + to_sibling(0, na):
            cp.start()
        for k in range(n):
            copy(k, 2, (*nb, c), me).wait_recv()
        for cp in to_sibling(1, nb):
            cp.start()

    def pass_on():
        for k in range(n):
            copy(k, 3, (*dg, c), me).wait_recv()
        for cp in to_sibling(2, dg):
            cp.start()

    def finish():
        for k in range(n):
            copy(k, 0, sibling, me).wait_recv()
            for j, chip in enumerate((nb, na, dg)):
                copy(k, 4 + j, (*chip, 1 - c), me).wait_recv()
        for cp in first() + onward() + to_sibling(0, na) + to_sibling(1, nb) + to_sibling(2, dg):
            cp.wait_send()
        for cp in _local_stores(own, bufs, local_sems):
            cp.wait()

    return start, relay, pass_on, finish


N_CHIPS = 4
_CHIP_FLIPS = [(1, 0), (0, 1), (1, 1)]
F32_TRAVEL_LIMIT = 4096


def _chip_exchange_plan(ins, outs, bufs, send_sems, recv_sems, local_sems):
    n = len(ins)
    x, y, c = _position()
    my_q = 2 * x + y
    peers = [(x ^ fx, y ^ fy) for fx, fy in _CHIP_FLIPS]
    own = [outs[k].at[my_q] for k in range(n)]

    def sends():
        return [pltpu.make_async_remote_copy(
            src_ref=ins[k].at[2 * px + py], dst_ref=outs[k].at[my_q], send_sem=send_sems.at[k, f],
            recv_sem=recv_sems.at[k, f], device_id=(px, py, c), device_id_type=MESH)
            for f, (px, py) in enumerate(peers) for k in range(n)]

    def start():
        for cp in sends():
            cp.start()
        _local_copy([ins[k].at[my_q] for k in range(n)], own, bufs, local_sems)

    def finish():
        for f, (px, py) in enumerate(peers):
            for k in range(n):
                pltpu.make_async_remote_copy(
                    src_ref=ins[k].at[my_q], dst_ref=outs[k].at[2 * px + py], send_sem=send_sems.at[k, f],
                    recv_sem=recv_sems.at[k, f], device_id=(px, py, c), device_id_type=MESH).wait_recv()
        for cp in sends():
            cp.wait_send()
        for cp in _local_stores(own, bufs, local_sems):
            cp.wait()

    return start, finish


def _pair_sum(parts, sib):
    n = len(parts)

    def body(core_ref, *refs):
        for k in range(n):
            refs[2 * n + k][0] = (refs[k][0, 0].astype(F32) + refs[n + k][0].astype(F32)).astype(BF16)

    own = [pl.BlockSpec((1, 1) + p.shape[1:], lambda q, core: (q, core[0], 0, 0)) for p in parts]
    one = [pl.BlockSpec((1,) + p.shape[1:], lambda q, core: (q, 0, 0)) for p in parts]
    return pl.pallas_call(
        body, name="pair_sum",
        grid_spec=pltpu.PrefetchScalarGridSpec(num_scalar_prefetch=1, grid=(N_CHIPS,), in_specs=own + one,
                                               out_specs=one),
        out_shape=[jax.ShapeDtypeStruct(s.shape, BF16) for s in sib],
        compiler_params=pltpu.CompilerParams(dimension_semantics=("parallel",)),
    )(lax.axis_index("c").astype(jnp.int32).reshape(1), *[p.reshape((N_CHIPS, 2) + p.shape[1:]) for p in parts], *sib)


_VMEM = pl.BlockSpec(memory_space=pltpu.VMEM)


def _reduce_out_shapes(parts, vals):
    return ([jax.ShapeDtypeStruct((N_CHIPS,) + p.shape[1:], p.dtype) for p in parts]
            + [jax.ShapeDtypeStruct(v.shape, v.dtype) for v in vals])


def _reduce_scratch(parts, vals):
    n, ns = len(parts), len(vals)
    quarter = [pltpu.VMEM((N_CHIPS,) + p.shape[1:], p.dtype) for p in parts]
    dma = pltpu.SemaphoreType.DMA
    travel = [BF16 if v.size > F32_TRAVEL_LIMIT else v.dtype for v in vals]
    return (quarter * 3 + [pltpu.VMEM(v.shape, v.dtype) for v in vals]
            + [pltpu.VMEM(v.shape, t) for v, t in zip(vals, travel)]
            + [pltpu.VMEM((N_CHIPS,) + v.shape, t) for v, t in zip(vals, travel)]
            + [dma((max(n, 1), N_CHIPS)), dma((max(n, 1), N_CHIPS)), dma((max(ns, 1),)), dma((max(ns, 1),)),
               dma((max(n, 1), 3)), dma((max(n, 1), 3)), dma((max(ns, 1), 3)), dma((max(ns, 1), 3))])


def _reduce_plan(p_in, v_in, p_out, v_out, scratch):
    n, ns = len(p_in), len(v_in)
    p_sib, p_sum, p_all = scratch[:n], scratch[n:2 * n], scratch[2 * n:3 * n]
    v_sib, v_sum, v_all = (scratch[3 * n + j * ns:3 * n + (j + 1) * ns] for j in range(3))
    p1_send, p1_recv, v1_send, v1_recv, p3_send, p3_recv, v3_send, v3_recv = scratch[3 * n + 3 * ns:]
    x, y, c = _position()
    my_q = 2 * x + y
    peers = [(x ^ fx, y ^ fy) for fx, fy in _CHIP_FLIPS]

    def to_sibling(src, dst, send, recv):
        return pltpu.make_async_remote_copy(src_ref=src, dst_ref=dst, send_sem=send, recv_sem=recv,
                                            device_id=(x, y, 1 - c), device_id_type=MESH)

    def level1():
        cps = [to_sibling(p_in[k].at[2 * q + 1 - c], p_sib[k].at[q], p1_send.at[k, q], p1_recv.at[k, q])
               for k in range(n) for q in range(N_CHIPS)]
        return cps + [to_sibling(v_in[k], v_sib[k], v1_send.at[k], v1_recv.at[k]) for k in range(ns)]

    def to_chip(f, src, dst, send, recv):
        px, py = peers[f]
        return pltpu.make_async_remote_copy(src_ref=src, dst_ref=dst, send_sem=send, recv_sem=recv,
                                            device_id=(px, py, c), device_id_type=MESH)

    def level2(sending):
        cps = []
        for f, (px, py) in enumerate(peers):
            their_q = 2 * px + py
            for k in range(n):
                src, dst = (p_sum[k].at[their_q], p_all[k].at[my_q]) if sending else (
                    p_sum[k].at[my_q], p_all[k].at[their_q])
                cps.append(to_chip(f, src, dst, p3_send.at[k, f], p3_recv.at[k, f]))
            for k in range(ns):
                dst = v_all[k].at[my_q] if sending else v_all[k].at[their_q]
                cps.append(to_chip(f, v_sum[k], dst, v3_send.at[k, f], v3_recv.at[k, f]))
        return cps

    def start():
        for cp in level1():
            cp.start()

    def middle():
        for cp in level1():
            cp.wait_recv()
        for k in range(n):
            for q in range(N_CHIPS):
                p_sum[k][q] = (p_in[k][2 * q + c].astype(F32) + p_sib[k][q].astype(F32)).astype(p_sum[k].dtype)
        for k in range(ns):
            v_sum[k][...] = (v_in[k][...] + v_sib[k][...]).astype(v_sum[k].dtype)
        for cp in level2(True):
            cp.start()
        for k in range(n):
            p_all[k][my_q] = p_sum[k][my_q]
        for k in range(ns):
            v_all[k][my_q] = v_sum[k][...]

    def finish():
        for cp in level2(False):
            cp.wait_recv()
        for k in range(n):
            p_out[k][...] = p_all[k][...]
        for k in range(ns):
            total = v_all[k][0].astype(F32)
            for q in range(1, N_CHIPS):
                total = total + v_all[k][q].astype(F32)
            v_out[k][...] = total
        for cp in level1() + level2(True):
            cp.wait_send()

    return start, middle, finish


def _adamw_math(w, g, m, v):
    m = ADAM_B1 * m + (1.0 - ADAM_B1) * g
    v = ADAM_B2 * v + (1.0 - ADAM_B2) * (g * g)
    m_hat = m / (1.0 - ADAM_B1 ** ADAM_STEP)
    v_hat = v / (1.0 - ADAM_B2 ** ADAM_STEP)
    delta = -ADAM_LR * (m_hat / (jnp.sqrt(v_hat) + ADAM_EPS) + ADAM_WD * w)
    return delta, m, v


def _adamw_shards(parts, ws, ms, vs, steps, name):
    n = len(ws)

    def body(*refs):
        ins, outs = refs[:4 * n], refs[4 * n:]
        for k in range(n):
            p_ref, w_ref, m_ref, v_ref = ins[4 * k:4 * k + 4]
            g = p_ref[0].astype(F32)
            for j in range(1, N_CHIPS):
                g = g + p_ref[j].astype(F32)
            d, mn, vn = _adamw_math(w_ref[...], g, m_ref[...], v_ref[...])
            for o, val in zip(outs[4 * k:4 * k + 4], (g, d, mn, vn)):
                o[...] = val

    in_specs, out_specs, out_shape, operands = [], [], [], []
    for p, w, m, v in zip(parts, ws, ms, vs):
        R, Cc = w.shape
        blk = pl.BlockSpec((R // steps, Cc), lambda i: (i, 0))
        in_specs += [pl.BlockSpec((N_CHIPS, R // steps, Cc), lambda i: (0, i, 0)), blk, blk, blk]
        out_specs += [blk] * 4
        out_shape += [jax.ShapeDtypeStruct((R, Cc), F32)] * 4
        operands += [p, w, m, v]
    res = pl.pallas_call(
        body, name=name, grid=(steps,), in_specs=in_specs, out_specs=out_specs, out_shape=out_shape,
        compiler_params=pltpu.CompilerParams(dimension_semantics=("parallel",), vmem_limit_bytes=VMEM_LIMIT),
    )(*operands)
    return [res[4 * k:4 * k + 4] for k in range(n)]


def _adamw_small(grads, ws, ms, vs):
    n = len(grads)
    vm = pl.BlockSpec(memory_space=pltpu.VMEM)

    def body(*refs):
        g_in, w_in, m_in, v_in = (refs[k * n:(k + 1) * n] for k in range(4))
        g_out, d_out, m_out, v_out = (refs[(4 + k) * n:(5 + k) * n] for k in range(4))
        for k in range(n):
            if len(g_in[k].shape) > len(w_in[k].shape):
                g = g_in[k][0]
                for q in range(1, g_in[k].shape[0]):
                    g = g + g_in[k][q]
            else:
                g = g_in[k][...]
            d, mn, vn = _adamw_math(w_in[k][...], g, m_in[k][...], v_in[k][...])
            g_out[k][...] = g
            d_out[k][...] = d
            m_out[k][...] = mn
            v_out[k][...] = vn

    shapes = [jax.ShapeDtypeStruct(w.shape, F32) for w in ws]
    return pl.pallas_call(
        body, name="adamw_small", out_shape=shapes * 4, in_specs=[vm] * (4 * n), out_specs=[vm] * (4 * n),
    )(*grads, *ws, *ms, *vs)


def _shard_cols(full):
    R, Ct = full.shape
    return jnp.transpose(full.reshape(R, N_DEV, Ct // N_DEV), (1, 0, 2))


_COL_SHARDED = ("w_in", "w_gate", "w_up")
_BIG = ("w_in", "w_out", "w_gate", "w_up", "w_down")


def _rows(nm, p):
    return p[0].T if nm in _COL_SHARDED else p[0]


def _step(args, ts_mix_fwd, ts_ffn, ts_mix_bwd, tm_grad):
    (x, g_mix, w_in, b_in, w_dw, b_dw, ln_g, ln_b, w_pool, s_pool, w_out, g_ffn, w_gate, w_up, w_down, g_final,
     loss_target) = args[:17]
    names = ["g_mix", "w_in", "b_in", "w_dw", "b_dw", "ln_g", "ln_b", "w_pool", "s_pool", "w_out", "g_ffn",
             "w_gate", "w_up", "w_down", "g_final"]
    weights = dict(zip(names, args[1:16]))
    moms = dict(zip(names, args[17:32]))
    vars_ = dict(zip(names, args[32:47]))

    S, D = x.shape[1], x.shape[2]
    x2 = x.reshape(S, D)
    tgt2 = loss_target.reshape(S, D)

    shard = lambda nm: _rows(nm, weights[nm])
    w_pool_b = w_pool[0].astype(BF16)
    dw_rows = -(-CONV_WIDTH // SUBLANES) * SUBLANES
    dw_shard = jnp.pad(w_dw[0], ((0, dw_rows - CONV_WIDTH), (0, LANES - w_dw.shape[2])))

    a, gate, v, m, y, h1, xn, w_dw_f, g_in, g_out, _, g_gate, g_up, g_down = _mix_fwd(
        x2, g_mix, b_in, b_dw, ln_g, ln_b, w_pool_b, s_pool, [shard("w_in"), shard("w_out"), dw_shard],
        [shard("w_gate"), shard("w_up"), shard("w_down")], ts_mix_fwd)
    wt_in, w_out_f = g_in.reshape(-1, D), g_out.reshape(-1, D)
    wt_gate, wt_up, w_down_f = g_gate.reshape(-1, D), g_up.reshape(-1, D), g_down.reshape(-1, D)
    Fd = wt_gate.shape[0]
    f_chunks = [1024] * (Fd // 1024) + ([Fd % 1024] if Fd % 1024 else [])
    dh1, hn, act, dgt, dup, dh2, loss_p, dg_final, dg_ffn = _ffn(
        h1, tgt2, g_ffn, g_final.reshape(1, D), wt_gate, wt_up, w_down_f, ts_ffn, f_chunks, tm_grad)

    by_shard = lambda g: g.reshape(N_DEV, -1, D)
    ffn_grads, ffn_sib = _grad_ffn([dgt, dup, act], [hn, hn, dh2], tm_grad)
    pair_sums = _pair_sum([by_shard(g) for g in ffn_grads], list(ffn_sib))
    (dx, dz, dh1b, dg_mix, db_in, dw_dw, db_dw, dln_g, dln_b, dw_pool, ds_pool, r_gate, r_up, r_down) = _mix_bwd(
        dh1, x2, a, gate, v, m, g_mix, wt_in, w_dw_f, ln_g, ln_b, w_pool_b, s_pool, w_out_f, pair_sums, ts_mix_bwd)

    small_names = ["g_mix", "b_in", "b_dw", "ln_g", "ln_b", "w_pool", "s_pool", "g_ffn", "g_final"]
    small_shape = lambda p: p.reshape(-1, p.shape[-1])
    partial = [dg_mix, db_in, db_dw, dln_g, dln_b, dw_pool.reshape(-1, POOL_GROUP), ds_pool, dg_ffn, dg_final, loss_p]
    r_out, r_dw, r_in, summed = _grad_mix(y, dh1b, dz, xn, _shard_cols(dw_dw[0:CONV_WIDTH]), partial)

    big = {}
    recv = dict(zip(_BIG, [r_in, r_out, r_gate, r_up, r_down]))
    for call, group in (("adamw_ffn", ("w_gate", "w_up", "w_down")), ("adamw_mix", ("w_in", "w_out"))):
        results = _adamw_shards([recv[nm] for nm in group], [_rows(nm, weights[nm]) for nm in group],
                                [_rows(nm, moms[nm]) for nm in group], [_rows(nm, vars_[nm]) for nm in group], 2, call)
        for nm, res in zip(group, results):
            big[nm] = [o.T if nm in _COL_SHARDED else o for o in res]

    small_names = small_names + ["w_dw"]
    sm = _adamw_small(list(summed[:-1]) + [r_dw], [small_shape(weights[nm]) for nm in small_names],
                      [small_shape(moms[nm]) for nm in small_names], [small_shape(vars_[nm]) for nm in small_names])
    n_small = len(small_names)

    def result(kind, nm):
        if nm in big:
            return big[nm][kind].reshape(weights[nm].shape)
        return sm[kind * n_small + small_names.index(nm)].reshape(weights[nm].shape)

    loss = summed[-1][0, 0]
    out = [loss, dx.reshape(x.shape)]
    for kind in range(4):
        out += [result(kind, nm) for nm in names]
    return tuple(out)


def kernel(x, g_mix, w_in, b_in, w_dw, b_dw, ln_g, ln_b, w_pool, s_pool, w_out, g_ffn, w_gate, w_up, w_down, g_final, loss_target, m_g_mix, m_w_in, m_b_in, m_w_dw, m_b_dw, m_ln_g, m_ln_b, m_w_pool, m_s_pool, m_w_out, m_g_ffn, m_w_gate, m_w_up, m_w_down, m_g_final, v_g_mix, v_w_in, v_b_in, v_w_dw, v_b_dw, v_ln_g, v_ln_b, v_w_pool, v_s_pool, v_w_out, v_g_ffn, v_w_gate, v_w_up, v_w_down, v_g_final):
    args = (x, g_mix, w_in, b_in, w_dw, b_dw, ln_g, ln_b, w_pool, s_pool, w_out, g_ffn, w_gate, w_up, w_down, g_final, loss_target, m_g_mix, m_w_in, m_b_in, m_w_dw, m_b_dw, m_ln_g, m_ln_b, m_w_pool, m_s_pool, m_w_out, m_g_ffn, m_w_gate, m_w_up, m_w_down, m_g_final, v_g_mix, v_w_in, v_b_in, v_w_dw, v_b_dw, v_ln_g, v_ln_b, v_w_pool, v_s_pool, v_w_out, v_g_ffn, v_w_gate, v_w_up, v_w_down, v_g_final)
    return _step(args, ts_mix_fwd=512, ts_ffn=256, ts_mix_bwd=512, tm_grad=256)
```
